```python
import jax, jax.numpy as jnp
from jax import lax
import numpy as np

D_MODEL = 2048
BATCH = 8
SEQ = 2048
DEPTH = 2

HEAD_DIM = 128
A_HEADS = 6
A_KV_HEADS = 2
A_ROPE_THETA = 10000.0
B_HEADS = 4
B_PATTERNS = ((128, 1), (512, 4), (2048, 16))
B_N_GROUPS = 3
C_HEADS = 6
C_Q_RANK = 512
C_KV_RANK = 512
C_NOPE_DIM = 128
C_ROPE_DIM = 64
C_V_DIM = 128
C_ROPE_THETA = 10000.0
PARTIAL_ROPE_DIM = HEAD_DIM // 4
PARTIAL_ROPE_THETA = 500000.0
GRID_W = 64
Q_BLOCK = 128
D_FF = 5632
CONV_W = 3
EPS = 1e-6

A_WIDTH = A_HEADS * HEAD_DIM
B_WIDTH = B_HEADS * HEAD_DIM
C_WIDTH = C_HEADS * C_V_DIM
MIX_WIDTH = A_WIDTH + B_WIDTH + C_WIDTH

A_Q_COLS = A_HEADS * HEAD_DIM
A_KV_COLS = A_KV_HEADS * HEAD_DIM
B_Q_COLS = B_N_GROUPS * B_HEADS * HEAD_DIM
B_KV_COLS = B_HEADS * HEAD_DIM
IN_SPLITS = (A_Q_COLS, A_KV_COLS, A_KV_COLS, B_Q_COLS, B_KV_COLS, B_KV_COLS, C_Q_RANK, C_KV_RANK, C_ROPE_DIM)
IN_WIDTH = A_Q_COLS + 2 * A_KV_COLS + B_Q_COLS + 2 * B_KV_COLS + C_Q_RANK + C_KV_RANK + C_ROPE_DIM

kernel_name = "hybrid_parallel_gqa_dilated_mla_convffn"


def rms_norm(x, g):
    xf = x.astype(jnp.float32)
    y = xf * lax.rsqrt(jnp.mean(xf * xf, axis=-1, keepdims=True) + EPS)
    return (y * g.astype(jnp.float32)).astype(x.dtype)


def rope_table(pos, dim, theta):
    inv = theta ** (-jnp.arange(0, dim, 2, dtype=jnp.float32) / dim)
    ang = pos.astype(jnp.float32)[:, None] * inv[None, :]
    return jnp.cos(ang), jnp.sin(ang)


def apply_rope(x, cs):
    cos, sin = cs
    cos = cos[:, None, :]
    sin = sin[:, None, :]
    xf = x.astype(jnp.float32)
    x1, x2 = jnp.split(xf, 2, axis=-1)
    return jnp.concatenate([x1 * cos - x2 * sin, x1 * sin + x2 * cos], axis=-1).astype(x.dtype)


def axial_rope(x, cs_row, cs_col):
    half = x.shape[-1] // 2
    return jnp.concatenate([apply_rope(x[..., :half], cs_row), apply_rope(x[..., half:], cs_col)], axis=-1)


def partial_rope(x, cs):
    return jnp.concatenate([apply_rope(x[..., :PARTIAL_ROPE_DIM], cs), x[..., PARTIAL_ROPE_DIM:]], axis=-1)


def dense_block_attention(q, k, v):
    b, s, hq, dk = q.shape
    hkv = k.shape[2]
    g = hq // hkv
    nb = s // Q_BLOCK
    scale = dk ** -0.5
    qb = q.reshape(b, nb, Q_BLOCK, hkv, g, dk).transpose(1, 0, 2, 3, 4, 5)

    def one_block(qblk):
        sc = jnp.einsum('bqhgd,bkhd->bhgqk', qblk, k).astype(jnp.float32) * scale
        p = jax.nn.softmax(sc, axis=-1).astype(v.dtype)
        return jnp.einsum('bhgqk,bkhd->bqhgd', p, v)

    o = lax.map(one_block, qb)
    return o.transpose(1, 0, 2, 3, 4, 5).reshape(b, s, hq, v.shape[-1])


def fold_stride(x, d):
    b, s, h, dh = x.shape
    return x.reshape(b, s // d, d, h, dh).transpose(0, 2, 1, 3, 4).reshape(b * d, s // d, h, dh)


def unfold_stride(x, b, d):
    l = x.shape[1]
    rest = x.shape[2:]
    perm = (0, 2, 1) + tuple(range(3, 3 + len(rest)))
    return x.reshape((b, d, l) + rest).transpose(perm).reshape((b, l * d) + rest)


def banded_attention(q, k, v, half):
    n, l, h, dh = q.shape
    blk = half
    nb = -(-l // blk)
    lp = nb * blk
    qp = jnp.pad(q, ((0, 0), (0, lp - l), (0, 0), (0, 0))).reshape(n, nb, blk, h, dh)

    def key_blocks(t):
        tp = jnp.pad(t, ((0, 0), (blk, lp - l + blk), (0, 0), (0, 0))).reshape(n, nb + 2, blk, h, dh)
        return jnp.concatenate([tp[:, :-2], tp[:, 1:-1], tp[:, 2:]], axis=2)

    kb = key_blocks(k)
    vb = key_blocks(v)
    qpos = jnp.arange(nb)[:, None] * blk + jnp.arange(blk)[None, :]
    kpos = jnp.arange(nb)[:, None] * blk - blk + jnp.arange(3 * blk)[None, :]
    rel = kpos[:, None, :] - qpos[:, :, None]
    valid = (jnp.abs(rel) <= half) & (kpos[:, None, :] >= 0) & (kpos[:, None, :] < l)
    sc = jnp.einsum('nbqhd,nbkhd->nbhqk', qp, kb).astype(jnp.float32) * (dh ** -0.5)
    sc = jnp.where(valid[None, :, None], sc, -jnp.inf)
    mx = jnp.max(sc, axis=-1, keepdims=True)
    e = jnp.exp(sc - mx)
    den = jnp.sum(e, axis=-1, keepdims=True)
    o = jnp.einsum('nbhqk,nbkhd->nbqhd', (e / den).astype(v.dtype), vb)
    lse = (mx + jnp.log(den))[..., 0]
    o = o.reshape(n, lp, h, dh)[:, :l]
    lse = lse.transpose(0, 1, 3, 2).reshape(n, lp, h)[:, :l]
    return o, lse


def dilated_mixture(q_groups, k, v):
    b = k.shape[0]
    outs = []
    lses = []
    for g, (window, dil) in enumerate(B_PATTERNS):
        half = window // (2 * dil)
        o, lse = banded_attention(fold_stride(q_groups[:, :, g], dil), fold_stride(k, dil),
                                  fold_stride(v, dil), half)
        outs.append(unfold_stride(o, b, dil))
        lses.append(unfold_stride(lse, b, dil))
    w = jax.nn.softmax(jnp.stack(lses, axis=0), axis=0)
    y = jnp.einsum('gbsh,gbshd->bshd', w, jnp.stack(outs, axis=0).astype(jnp.float32))
    return y.astype(k.dtype)


def conv_ffn(x, w_up, conv_w, conv_b, w_down):
    h = x @ w_up
    s = h.shape[1]
    hp = jnp.pad(h, ((0, 0), (1, 1), (0, 0)))
    h = hp[:, :s] * conv_w[0] + hp[:, 1:s + 1] * conv_w[1] + hp[:, 2:] * conv_w[2] + conv_b
    gate, up = jnp.split(h, 2, axis=-1)
    return (jax.nn.silu(gate) * up) @ w_down


def _fwd_setup_inputs(seed: int = 0) -> dict:
    key = jax.random.key(seed)
    ks = jax.random.split(key, 20)
    f32 = jnp.float32

    def nrm(k, shape, scale):
        return jax.random.normal(k, shape, f32) * scale

    def gain(k, shape):
        return 1.0 + 0.02 * jax.random.normal(k, shape, f32)

    return {
        "x": jax.random.normal(ks[0], (BATCH, SEQ, D_MODEL), f32),
        "attn_norm": gain(ks[1], (DEPTH, D_MODEL)),
        "w_in": nrm(ks[2], (DEPTH, D_MODEL, IN_WIDTH), D_MODEL ** -0.5),
        "a_q_norm": gain(ks[3], (DEPTH, HEAD_DIM)),
        "a_k_norm": gain(ks[4], (DEPTH, HEAD_DIM)),
        "c_q_norm": gain(ks[5], (DEPTH, C_Q_RANK)),
        "c_kv_norm": gain(ks[6], (DEPTH, C_KV_RANK)),
        "w_uq": nrm(ks[7], (DEPTH, C_Q_RANK, C_HEADS * (C_NOPE_DIM + C_ROPE_DIM)), C_Q_RANK ** -0.5),
        "w_ukv": nrm(ks[8], (DEPTH, C_KV_RANK, C_HEADS * (C_NOPE_DIM + C_V_DIM)), C_KV_RANK ** -0.5),
        "out_norm": gain(ks[9], (DEPTH, MIX_WIDTH)),
        "w_out": nrm(ks[10], (DEPTH, MIX_WIDTH, D_MODEL), MIX_WIDTH ** -0.5),
        "ffn_norm": gain(ks[11], (DEPTH, D_MODEL)),
        "w_up": nrm(ks[12], (DEPTH, D_MODEL, 2 * D_FF), D_MODEL ** -0.5),
        "conv_w": nrm(ks[13], (DEPTH, CONV_W, 2 * D_FF), CONV_W ** -0.5),
        "conv_b": nrm(ks[14], (DEPTH, 2 * D_FF), 0.01),
        "w_down": nrm(ks[15], (DEPTH, D_FF, D_MODEL), D_FF ** -0.5),
        "final_norm": gain(ks[16], (D_MODEL,)),
    }


def _fwd_reference(x, attn_norm, w_in, a_q_norm, a_k_norm, c_q_norm, c_kv_norm, w_uq, w_ukv, out_norm,
              w_out, ffn_norm, w_up, conv_w, conv_b, w_down, final_norm):
    b, s, _ = x.shape
    rows = s // GRID_W
    row_pos = jnp.repeat(jnp.arange(rows), GRID_W)
    col_pos = jnp.tile(jnp.arange(GRID_W), rows)
    t_pos = jnp.arange(s)
    cs_row = rope_table(row_pos, HEAD_DIM // 2, A_ROPE_THETA)
    cs_col = rope_table(col_pos, HEAD_DIM // 2, A_ROPE_THETA)
    cs_part = rope_table(t_pos, PARTIAL_ROPE_DIM, PARTIAL_ROPE_THETA)
    cs_mla = rope_table(t_pos, C_ROPE_DIM, C_ROPE_THETA)
    split_idx = np.cumsum(np.array(IN_SPLITS))[:-1].tolist()

    for l in range(DEPTH):
        hn = rms_norm(x, attn_norm[l])
        proj = hn @ w_in[l]
        aq, ak, av, bq, bk, bv, cq, ckv, ckr = jnp.split(proj, split_idx, axis=-1)

        aq = axial_rope(rms_norm(aq.reshape(b, s, A_HEADS, HEAD_DIM), a_q_norm[l]), cs_row, cs_col)
        ak = axial_rope(rms_norm(ak.reshape(b, s, A_KV_HEADS, HEAD_DIM), a_k_norm[l]), cs_row, cs_col)
        av = av.reshape(b, s, A_KV_HEADS, HEAD_DIM)
        ya = dense_block_attention(aq, ak, av).reshape(b, s, A_WIDTH)

        bq = partial_rope(bq.reshape(b, s, B_N_GROUPS * B_HEADS, HEAD_DIM), cs_part)
        bq = bq.reshape(b, s, B_N_GROUPS, B_HEADS, HEAD_DIM)
        bk = partial_rope(bk.reshape(b, s, B_HEADS, HEAD_DIM), cs_part)
        bv = bv.reshape(b, s, B_HEADS, HEAD_DIM)
        yb = dilated_mixture(bq, bk, bv).reshape(b, s, B_WIDTH)

        q = (rms_norm(cq, c_q_norm[l]) @ w_uq[l]).reshape(b, s, C_HEADS, C_NOPE_DIM + C_ROPE_DIM)
        q = jnp.concatenate([q[..., :C_NOPE_DIM], apply_rope(q[..., C_NOPE_DIM:], cs_mla)], axis=-1)
        kv = (rms_norm(ckv, c_kv_norm[l]) @ w_ukv[l]).reshape(b, s, C_HEADS, C_NOPE_DIM + C_V_DIM)
        k_nope = kv[..., :C_NOPE_DIM]
        cv = kv[..., C_NOPE_DIM:]
        k_rope = apply_rope(ckr[:, :, None, :], cs_mla)
        k = jnp.concatenate([k_nope, jnp.broadcast_to(k_rope, (b, s, C_HEADS, C_ROPE_DIM))], axis=-1)
        yc = dense_block_attention(q, k, cv).reshape(b, s, C_WIDTH)

        g_a, g_b, g_c = jnp.split(out_norm[l], [A_WIDTH, A_WIDTH + B_WIDTH])
        y = jnp.concatenate([rms_norm(ya, g_a), rms_norm(yb, g_b), rms_norm(yc, g_c)], axis=-1)
        x = x + y @ w_out[l]

        x = x + conv_ffn(rms_norm(x, ffn_norm[l]), w_up[l], conv_w[l], conv_b[l], w_down[l])

    return rms_norm(x, final_norm)


import jax as _jax
import jax.numpy as _jnp

TWIN_FORMAT = 'train_step'
FWD_PARAMS = ['x', 'attn_norm', 'w_in', 'a_q_norm', 'a_k_norm', 'c_q_norm', 'c_kv_norm', 'w_uq', 'w_ukv', 'out_norm', 'w_out', 'ffn_norm', 'w_up', 'conv_w', 'conv_b', 'w_down', 'final_norm']
TWIN_WEIGHTS = ['attn_norm', 'w_in', 'a_q_norm', 'a_k_norm', 'c_q_norm', 'c_kv_norm', 'w_uq', 'w_ukv', 'out_norm', 'w_out', 'ffn_norm', 'w_up', 'conv_w', 'conv_b', 'w_down', 'final_norm']
TWIN_DIFF_INPUT = 'x'
TWIN_INPUTS = ['x', 'attn_norm', 'w_in', 'a_q_norm', 'a_k_norm', 'c_q_norm', 'c_kv_norm', 'w_uq', 'w_ukv', 'out_norm', 'w_out', 'ffn_norm', 'w_up', 'conv_w', 'conv_b', 'w_down', 'final_norm', 'loss_target', 'm_attn_norm', 'm_w_in', 'm_a_q_norm', 'm_a_k_norm', 'm_c_q_norm', 'm_c_kv_norm', 'm_w_uq', 'm_w_ukv', 'm_out_norm', 'm_w_out', 'm_ffn_norm', 'm_w_up', 'm_conv_w', 'm_conv_b', 'm_w_down', 'm_final_norm', 'v_attn_norm', 'v_w_in', 'v_a_q_norm', 'v_a_k_norm', 'v_c_q_norm', 'v_c_kv_norm', 'v_w_uq', 'v_w_ukv', 'v_out_norm', 'v_w_out', 'v_ffn_norm', 'v_w_up', 'v_conv_w', 'v_conv_b', 'v_w_down', 'v_final_norm']
TWIN_OUTPUTS = ['loss', 'grad_x', 'grad_attn_norm', 'grad_w_in', 'grad_a_q_norm', 'grad_a_k_norm', 'grad_c_q_norm', 'grad_c_kv_norm', 'grad_w_uq', 'grad_w_ukv', 'grad_out_norm', 'grad_w_out', 'grad_ffn_norm', 'grad_w_up', 'grad_conv_w', 'grad_conv_b', 'grad_w_down', 'grad_final_norm', 'delta_attn_norm', 'delta_w_in', 'delta_a_q_norm', 'delta_a_k_norm', 'delta_c_q_norm', 'delta_c_kv_norm', 'delta_w_uq', 'delta_w_ukv', 'delta_out_norm', 'delta_w_out', 'delta_ffn_norm', 'delta_w_up', 'delta_conv_w', 'delta_conv_b', 'delta_w_down', 'delta_final_norm', 'new_m_attn_norm', 'new_m_w_in', 'new_m_a_q_norm', 'new_m_a_k_norm', 'new_m_c_q_norm', 'new_m_c_kv_norm', 'new_m_w_uq', 'new_m_w_ukv', 'new_m_out_norm', 'new_m_w_out', 'new_m_ffn_norm', 'new_m_w_up', 'new_m_conv_w', 'new_m_conv_b', 'new_m_w_down', 'new_m_final_norm', 'new_v_attn_norm', 'new_v_w_in', 'new_v_a_q_norm', 'new_v_a_k_norm', 'new_v_c_q_norm', 'new_v_c_kv_norm', 'new_v_w_uq', 'new_v_w_ukv', 'new_v_out_norm', 'new_v_w_out', 'new_v_ffn_norm', 'new_v_w_up', 'new_v_conv_w', 'new_v_conv_b', 'new_v_w_down', 'new_v_final_norm']
TWIN_LEAF_KINDS = {'loss': 'loss', 'grad_x': 'grad_x', 'grad_attn_norm': 'grad_w', 'grad_w_in': 'grad_w', 'grad_a_q_norm': 'grad_w', 'grad_a_k_norm': 'grad_w', 'grad_c_q_norm': 'grad_w', 'grad_c_kv_norm': 'grad_w', 'grad_w_uq': 'grad_w', 'grad_w_ukv': 'grad_w', 'grad_out_norm': 'grad_w', 'grad_w_out': 'grad_w', 'grad_ffn_norm': 'grad_w', 'grad_w_up': 'grad_w', 'grad_conv_w': 'grad_w', 'grad_conv_b': 'grad_w', 'grad_w_down': 'grad_w', 'grad_final_norm': 'grad_w', 'delta_attn_norm': 'delta_w', 'delta_w_in': 'delta_w', 'delta_a_q_norm': 'delta_w', 'delta_a_k_norm': 'delta_w', 'delta_c_q_norm': 'delta_w', 'delta_c_kv_norm': 'delta_w', 'delta_w_uq': 'delta_w', 'delta_w_ukv': 'delta_w', 'delta_out_norm': 'delta_w', 'delta_w_out': 'delta_w', 'delta_ffn_norm': 'delta_w', 'delta_w_up': 'delta_w', 'delta_conv_w': 'delta_w', 'delta_conv_b': 'delta_w', 'delta_w_down': 'delta_w', 'delta_final_norm': 'delta_w', 'new_m_attn_norm': 'new_m', 'new_m_w_in': 'new_m', 'new_m_a_q_norm': 'new_m', 'new_m_a_k_norm': 'new_m', 'new_m_c_q_norm': 'new_m', 'new_m_c_kv_norm': 'new_m', 'new_m_w_uq': 'new_m', 'new_m_w_ukv': 'new_m', 'new_m_out_norm': 'new_m', 'new_m_w_out': 'new_m', 'new_m_ffn_norm': 'new_m', 'new_m_w_up': 'new_m', 'new_m_conv_w': 'new_m', 'new_m_conv_b': 'new_m', 'new_m_w_down': 'new_m', 'new_m_final_norm': 'new_m', 'new_v_attn_norm': 'new_v', 'new_v_w_in': 'new_v', 'new_v_a_q_norm': 'new_v', 'new_v_a_k_norm': 'new_v', 'new_v_c_q_norm': 'new_v', 'new_v_c_kv_norm': 'new_v', 'new_v_w_uq': 'new_v', 'new_v_w_ukv': 'new_v', 'new_v_out_norm': 'new_v', 'new_v_w_out': 'new_v', 'new_v_ffn_norm': 'new_v', 'new_v_w_up': 'new_v', 'new_v_conv_w': 'new_v', 'new_v_conv_b': 'new_v', 'new_v_w_down': 'new_v', 'new_v_final_norm': 'new_v'}


def _forward(args):
    return _fwd_reference(*[args[k] for k in FWD_PARAMS])


def _output_shape():
    out = _jax.eval_shape(lambda: _forward(_fwd_setup_inputs(0)))
    return out.shape, out.dtype

N_MICROBATCH = 1
ADAM_LR = 0.001
ADAM_B1 = 0.9
ADAM_B2 = 0.999
ADAM_EPS = 1e-08
ADAM_WD = 0.01
ADAM_STEP = 10
PER_EXAMPLE_BATCH_AXIS = {'x': 0, 'loss_target': 0}
SHARED_INPUTS = []
_WEIGHT_DTYPES = {'attn_norm': _jnp.float32, 'w_in': _jnp.float32, 'a_q_norm': _jnp.float32, 'a_k_norm': _jnp.float32, 'c_q_norm': _jnp.float32, 'c_kv_norm': _jnp.float32, 'w_uq': _jnp.float32, 'w_ukv': _jnp.float32, 'out_norm': _jnp.float32, 'w_out': _jnp.float32, 'ffn_norm': _jnp.float32, 'w_up': _jnp.float32, 'conv_w': _jnp.float32, 'conv_b': _jnp.float32, 'w_down': _jnp.float32, 'final_norm': _jnp.float32}
MOMENT_SCALE = {'attn_norm': 7.542107e-02, 'w_in': 4.514964e-02, 'a_q_norm': 8.058355e-02, 'a_k_norm': 7.893198e-02, 'c_q_norm': 3.877927e-02, 'c_kv_norm': 9.224409e-02, 'w_uq': 2.533301e-02, 'w_ukv': 4.547424e-02, 'out_norm': 5.981892e-02, 'w_out': 5.746642e-02, 'ffn_norm': 3.206972e-02, 'w_up': 1.374625e-02, 'conv_w': 1.387039e-02, 'conv_b': 1.792530e-02, 'w_down': 2.302623e-02, 'final_norm': 8.211326e+00}


def _to_microbatches(a, axis):
    t = _jnp.moveaxis(a, axis, 0)
    t = t.reshape((N_MICROBATCH, t.shape[0] // N_MICROBATCH) + t.shape[1:])
    return _jnp.moveaxis(t, 1, axis + 1)


def setup_inputs(seed: int = 0) -> dict:
    inp = _fwd_setup_inputs(seed)
    key = _jax.random.fold_in(_jax.random.key(seed), 7919)
    shape, _ = _output_shape()
    out = dict(inp)
    out["loss_target"] = _jax.random.normal(_jax.random.fold_in(key, 0), shape, _jnp.float32)
    for i, name in enumerate(TWIN_WEIGHTS):
        w = inp[name].astype(_jnp.float32)
        if MOMENT_SCALE is None:
            s = _jnp.sqrt(_jnp.mean(_jnp.square(w)) + 1e-30)
        else:
            s = MOMENT_SCALE[name]
        km, kv = _jax.random.split(_jax.random.fold_in(key, i + 1))
        out[name] = w
        out["m_" + name] = s * _jax.random.normal(km, w.shape, _jnp.float32)
        out["v_" + name] = (s * s) * _jax.random.uniform(kv, w.shape, _jnp.float32, 0.5, 1.5)
    if N_MICROBATCH > 1:
        for name, axis in PER_EXAMPLE_BATCH_AXIS.items():
            out[name] = _to_microbatches(out[name], axis)
    return {'x': out['x'], 'attn_norm': out['attn_norm'], 'w_in': out['w_in'], 'a_q_norm': out['a_q_norm'], 'a_k_norm': out['a_k_norm'], 'c_q_norm': out['c_q_norm'], 'c_kv_norm': out['c_kv_norm'], 'w_uq': out['w_uq'], 'w_ukv': out['w_ukv'], 'out_norm': out['out_norm'], 'w_out': out['w_out'], 'ffn_norm': out['ffn_norm'], 'w_up': out['w_up'], 'conv_w': out['conv_w'], 'conv_b': out['conv_b'], 'w_down': out['w_down'], 'final_norm': out['final_norm'], 'loss_target': out['loss_target'], 'm_attn_norm': out['m_attn_norm'], 'm_w_in': out['m_w_in'], 'm_a_q_norm': out['m_a_q_norm'], 'm_a_k_norm': out['m_a_k_norm'], 'm_c_q_norm': out['m_c_q_norm'], 'm_c_kv_norm': out['m_c_kv_norm'], 'm_w_uq': out['m_w_uq'], 'm_w_ukv': out['m_w_ukv'], 'm_out_norm': out['m_out_norm'], 'm_w_out': out['m_w_out'], 'm_ffn_norm': out['m_ffn_norm'], 'm_w_up': out['m_w_up'], 'm_conv_w': out['m_conv_w'], 'm_conv_b': out['m_conv_b'], 'm_w_down': out['m_w_down'], 'm_final_norm': out['m_final_norm'], 'v_attn_norm': out['v_attn_norm'], 'v_w_in': out['v_w_in'], 'v_a_q_norm': out['v_a_q_norm'], 'v_a_k_norm': out['v_a_k_norm'], 'v_c_q_norm': out['v_c_q_norm'], 'v_c_kv_norm': out['v_c_kv_norm'], 'v_w_uq': out['v_w_uq'], 'v_w_ukv': out['v_w_ukv'], 'v_out_norm': out['v_out_norm'], 'v_w_out': out['v_w_out'], 'v_ffn_norm': out['v_ffn_norm'], 'v_w_up': out['v_w_up'], 'v_conv_w': out['v_conv_w'], 'v_conv_b': out['v_conv_b'], 'v_w_down': out['v_w_down'], 'v_final_norm': out['v_final_norm']}


def _loss(weights, diff, rest, loss_target):
    with _jax.named_scope("forward"):
        args = {**rest, TWIN_DIFF_INPUT: diff, **{k: w.astype(_WEIGHT_DTYPES[k]) for k, w in weights.items()}}
        y = _forward(args)
    with _jax.named_scope("loss_head"):
        err = _jnp.square(y.astype(_jnp.float32) - loss_target)
        return 0.5 * _jnp.sum(_jnp.mean(err, axis=-1)) if err.ndim else 0.5 * err


def _adamw(w, g, m, v):
    m = ADAM_B1 * m + (1.0 - ADAM_B1) * g
    v = ADAM_B2 * v + (1.0 - ADAM_B2) * _jnp.square(g)
    m_hat = m / (1.0 - ADAM_B1 ** ADAM_STEP)
    v_hat = v / (1.0 - ADAM_B2 ** ADAM_STEP)
    delta = -ADAM_LR * (m_hat / (_jnp.sqrt(v_hat) + ADAM_EPS) + ADAM_WD * w)
    return delta, m, v


def reference(x, attn_norm, w_in, a_q_norm, a_k_norm, c_q_norm, c_kv_norm, w_uq, w_ukv, out_norm, w_out, ffn_norm, w_up, conv_w, conv_b, w_down, final_norm, loss_target, m_attn_norm, m_w_in, m_a_q_norm, m_a_k_norm, m_c_q_norm, m_c_kv_norm, m_w_uq, m_w_ukv, m_out_norm, m_w_out, m_ffn_norm, m_w_up, m_conv_w, m_conv_b, m_w_down, m_final_norm, v_attn_norm, v_w_in, v_a_q_norm, v_a_k_norm, v_c_q_norm, v_c_kv_norm, v_w_uq, v_w_ukv, v_out_norm, v_w_out, v_ffn_norm, v_w_up, v_conv_w, v_conv_b, v_w_down, v_final_norm):
    given = dict(x=x, attn_norm=attn_norm, w_in=w_in, a_q_norm=a_q_norm, a_k_norm=a_k_norm, c_q_norm=c_q_norm, c_kv_norm=c_kv_norm, w_uq=w_uq, w_ukv=w_ukv, out_norm=out_norm, w_out=w_out, ffn_norm=ffn_norm, w_up=w_up, conv_w=conv_w, conv_b=conv_b, w_down=w_down, final_norm=final_norm, loss_target=loss_target, m_attn_norm=m_attn_norm, m_w_in=m_w_in, m_a_q_norm=m_a_q_norm, m_a_k_norm=m_a_k_norm, m_c_q_norm=m_c_q_norm, m_c_kv_norm=m_c_kv_norm, m_w_uq=m_w_uq, m_w_ukv=m_w_ukv, m_out_norm=m_out_norm, m_w_out=m_w_out, m_ffn_norm=m_ffn_norm, m_w_up=m_w_up, m_conv_w=m_conv_w, m_conv_b=m_conv_b, m_w_down=m_w_down, m_final_norm=m_final_norm, v_attn_norm=v_attn_norm, v_w_in=v_w_in, v_a_q_norm=v_a_q_norm, v_a_k_norm=v_a_k_norm, v_c_q_norm=v_c_q_norm, v_c_kv_norm=v_c_kv_norm, v_w_uq=v_w_uq, v_w_ukv=v_w_ukv, v_out_norm=v_out_norm, v_w_out=v_w_out, v_ffn_norm=v_ffn_norm, v_w_up=v_w_up, v_conv_w=v_conv_w, v_conv_b=v_conv_b, v_w_down=v_w_down, v_final_norm=v_final_norm)
    weights = {n: given[n] for n in TWIN_WEIGHTS}
    shared = {n: given[n] for n in SHARED_INPUTS}
    per_example = {n: given[n] for n in ['x']}
    grad_fn = _jax.value_and_grad(_loss, argnums=(0, 1))

    def one_microbatch(ex, loss_target):
        ex = dict(ex)
        diff = ex.pop(TWIN_DIFF_INPUT)
        return grad_fn(weights, diff, {**shared, **ex}, loss_target)

    if N_MICROBATCH == 1:
        loss, (grad_w, grad_x) = one_microbatch(per_example, given["loss_target"])
    else:
        def body(carry, xs):
            loss_sum, grad_sum = carry
            l_k, (gw_k, gx_k) = one_microbatch(xs[0], xs[1])
            with _jax.named_scope("update"):
                return (loss_sum + l_k, _jax.tree.map(_jnp.add, grad_sum, gw_k)), gx_k

        init = (_jnp.zeros((), _jnp.float32), _jax.tree.map(_jnp.zeros_like, weights))
        (loss, grad_w), grad_x = _jax.lax.scan(body, init, (per_example, given["loss_target"]))
    with _jax.named_scope("update"):
        delta_w, new_m, new_v = {}, {}, {}
        for n in TWIN_WEIGHTS:
            delta_w[n], new_m[n], new_v[n] = _adamw(weights[n], grad_w[n], given["m_" + n], given["v_" + n])
    return (loss, grad_x, *[grad_w[n] for n in TWIN_WEIGHTS], *[delta_w[n] for n in TWIN_WEIGHTS],
            *[new_m[n] for n in TWIN_WEIGHTS], *[new_v[n] for n in TWIN_WEIGHTS])
```

```python
import functools

import jax
import jax.numpy as jnp
import numpy as np
from jax import lax
from jax.experimental import pallas as pl
from jax.experimental.pallas import tpu as pltpu

F32 = jnp.float32
BF16 = jnp.bfloat16
MESH = pl.DeviceIdType.MESH

V7X_VMEM_BYTES = 64 * 1024 * 1024
VMEM_LIMIT = V7X_VMEM_BYTES - 6 * 1024 * 1024
LANES = 128
ELEMENTWISE_BLOCK_BYTES = 2 * 1024 * 1024

HEAD = 128
A_HEADS, A_KV = 6, 2
B_HEADS, B_GROUPS = 4, 3
B_DILATIONS = (1, 4, 16)
B_HALF = 64
C_HEADS, C_RANK, C_ROPE = 6, 512, 64
GRID_W = 64
A_THETA, B_THETA, C_THETA = 10000.0, 500000.0, 10000.0
B_ROPE_DIM = 32
EPS = 1e-6
A_W, B_W, C_W = A_HEADS * HEAD, B_HEADS * HEAD, C_HEADS * HEAD
O_AQ, O_AK, O_AV = 0, 768, 1024
O_BQ, O_BK, O_BV = 1280, 2816, 3328
O_CQ, O_CKV, O_CKR = 3840, 4352, 4864
IN_W = 4928
PROJ_W = 5120

ADAM_LR, ADAM_B1, ADAM_B2, ADAM_EPS, ADAM_WD, ADAM_STEP = 0.001, 0.9, 0.999, 1e-08, 0.01, 10

NN = (((1,), (0,)), ((), ()))
NT = (((1,), (1,)), ((), ()))
TN = (((0,), (0,)), ((), ()))


def _dot(a, b, dims=NN):
    return lax.dot_general(a, b, dims, preferred_element_type=F32)


def _params(*sem):
    return pltpu.CompilerParams(dimension_semantics=sem if sem else None, vmem_limit_bytes=VMEM_LIMIT)


def _tile(n, target, unit=LANES):
    if n <= target:
        return n
    best = 0
    for t in range(unit, target + 1, unit):
        if n % t == 0:
            best = t
    assert best, (n, target)
    return best


def _rows(r, width, itemsize=4):
    return _tile(r, max(8, ELEMENTWISE_BLOCK_BYTES // (width * itemsize)), 8)


def _matmul_call(name, dims, grid, a_spec, b_spec, o_spec, out_shape, acc_shape, add_spec=None):
    nk = grid[2]

    def body(*refs):
        if add_spec is None:
            a_ref, b_ref, o_ref, acc = refs
            add_ref = None
        else:
            a_ref, b_ref, add_ref, o_ref, acc = refs
        k = pl.program_id(2)

        @pl.when(k == 0)
        def _():
            acc[...] = jnp.zeros_like(acc)

        acc[...] += _dot(a_ref[...], b_ref[...], dims)

        @pl.when(k == nk - 1)
        def _():
            r = acc[...]
            if add_ref is not None:
                r = r + add_ref[...]
            o_ref[...] = r.astype(o_ref.dtype)

    in_specs = [a_spec, b_spec] + ([] if add_spec is None else [add_spec])
    return pl.pallas_call(
        body, name=name, grid=grid, in_specs=in_specs, out_specs=o_spec, out_shape=out_shape,
        scratch_shapes=[pltpu.VMEM(acc_shape, F32)],
        compiler_params=_params("parallel", "parallel", "arbitrary"))


def matmul(name, a, b, mode, out_dtype, add=None, tm=1024, tn=1024, tk=1024):
    if mode == "nn":
        (m, k), (k2, n) = a.shape, b.shape
    elif mode == "nt":
        (m, k), (n, k2) = a.shape, b.shape
    else:
        (k, m), (k2, n) = a.shape, b.shape
    assert k == k2, (name, a.shape, b.shape)
    tm, tn, tk = _tile(m, tm), _tile(n, tn), _tile(k, tk)
    grid = (m // tm, n // tn, k // tk)
    if mode == "tn":
        a_spec = pl.BlockSpec((tk, tm), lambda i, j, kk: (kk, i))
    else:
        a_spec = pl.BlockSpec((tm, tk), lambda i, j, kk: (i, kk))
    if mode == "nt":
        b_spec = pl.BlockSpec((tn, tk), lambda i, j, kk: (j, kk))
    else:
        b_spec = pl.BlockSpec((tk, tn), lambda i, j, kk: (kk, j))
    o_spec = pl.BlockSpec((tm, tn), lambda i, j, kk: (i, j))
    dims = {"nn": NN, "nt": NT, "tn": TN}[mode]
    call = _matmul_call(name, dims, grid, a_spec, b_spec, o_spec, jax.ShapeDtypeStruct((m, n), out_dtype),
                        (tm, tn), None if add is None else o_spec)
    return call(a, b) if add is None else call(a, b, add)


def rmsnorm_fwd(name, x, g, width, out_dtype):
    r, cols = x.shape
    nb = cols // width
    tr = _rows(r, width)

    def body(x_ref, g_ref, o_ref):
        xv = x_ref[...]
        rs = lax.rsqrt(jnp.mean(xv * xv, axis=-1, keepdims=True) + EPS)
        o_ref[...] = (xv * rs * g_ref[...]).astype(o_ref.dtype)

    blk = pl.BlockSpec((tr, width), lambda i, j: (i, j))
    return pl.pallas_call(
        body, name=name, grid=(r // tr, nb),
        in_specs=[blk, pl.BlockSpec((1, width), lambda i, j: (0, 0))], out_specs=blk,
        out_shape=jax.ShapeDtypeStruct((r, cols), out_dtype),
        compiler_params=_params("parallel", "parallel"))(x, g.reshape(1, width))


def rmsnorm_bwd(name, dy, x, g, width, out_dtypes=(F32,), add=None):
    r, cols = x.shape
    nb = cols // width
    tr = _rows(r, width)
    n_out = len(out_dtypes)

    def body(*refs):
        dy_ref, x_ref, g_ref = refs[:3]
        add_ref = refs[3] if add is not None else None
        outs = refs[-(n_out + 1):-1]
        dg_ref = refs[-1]
        xv = x_ref[...]
        dyv = dy_ref[...].astype(F32)
        rs = lax.rsqrt(jnp.mean(xv * xv, axis=-1, keepdims=True) + EPS)
        xh = xv * rs
        dyg = dyv * g_ref[...]
        dx = rs * (dyg - xh * jnp.mean(dyg * xh, axis=-1, keepdims=True))
        if add_ref is not None:
            dx = dx + add_ref[...]
        for o in outs:
            o[...] = dx.astype(o.dtype)

        @pl.when((pl.program_id(0) == 0) & (pl.program_id(1) == 0))
        def _():
            dg_ref[...] = jnp.zeros_like(dg_ref)

        dg_ref[...] += jnp.sum(dyv * xh, axis=0, keepdims=True)

    blk = pl.BlockSpec((tr, width), lambda i, j: (i, j))
    vec = pl.BlockSpec((1, width), lambda i, j: (0, 0))
    ins = [dy, x, g.reshape(1, width)] + ([add] if add is not None else [])
    res = pl.pallas_call(
        body, name=name, grid=(r // tr, nb),
        in_specs=[blk, blk, vec] + ([blk] if add is not None else []),
        out_specs=[blk] * n_out + [vec],
        out_shape=[jax.ShapeDtypeStruct((r, cols), dt) for dt in out_dtypes] + [jax.ShapeDtypeStruct((1, width), F32)],
        compiler_params=_params("arbitrary", "arbitrary"))(*ins)
    return tuple(res[:n_out]) + (res[n_out].reshape(width),)


def _rope_angles(pos, dim, theta):
    inv = theta ** (-jnp.arange(0, dim, 2, dtype=F32) / dim)
    ang = pos.astype(F32)[:, None] * inv[None, :]
    return jnp.cos(ang), jnp.sin(ang)


def _rope_tables(s):
    rows = s // GRID_W
    row_pos = jnp.repeat(jnp.arange(rows), GRID_W)
    col_pos = jnp.tile(jnp.arange(GRID_W), rows)
    t_pos = jnp.arange(s)
    z = lambda n: jnp.zeros((s, n), F32)
    o = lambda n: jnp.ones((s, n), F32)
    cr, sr = _rope_angles(row_pos, HEAD // 2, A_THETA)
    cc, sc = _rope_angles(col_pos, HEAD // 2, A_THETA)
    tab_a = (jnp.concatenate([cr, cr, cc, cc], 1), jnp.concatenate([-sr, z(32), -sc, z(32)], 1),
             jnp.concatenate([z(32), sr, z(32), sc], 1), 32)
    cp, sp = _rope_angles(t_pos, B_ROPE_DIM, B_THETA)
    tab_b = (jnp.concatenate([cp, cp, o(96)], 1), jnp.concatenate([-sp, z(112)], 1),
             jnp.concatenate([z(16), sp, z(96)], 1), 16)
    cm, sm = _rope_angles(t_pos, C_ROPE, C_THETA)
    tab_c = (jnp.concatenate([cm, cm, o(64)], 1), jnp.concatenate([-sm, z(96)], 1),
             jnp.concatenate([z(32), sm, z(64)], 1), 32)

    def transposed(tab):
        c, s1, s2, h = tab
        return (c, jnp.roll(s2, -h, axis=1), jnp.roll(s1, h, axis=1), h)

    return {k: (t, transposed(t)) for k, t in (("a", tab_a), ("b", tab_b), ("c", tab_c))}


def rope(name, x, tab, out_dtype):
    c, s1, s2, h = tab
    s, cols = x.shape
    nh = cols // HEAD
    ts = _rows(s, HEAD)

    def body(x_ref, c_ref, s1_ref, s2_ref, o_ref):
        xv = x_ref[...].astype(F32)
        out = xv * c_ref[...] + pltpu.roll(xv, HEAD - h, 1) * s1_ref[...] + pltpu.roll(xv, h, 1) * s2_ref[...]
        o_ref[...] = out.astype(o_ref.dtype)

    blk = pl.BlockSpec((ts, HEAD), lambda i, j: (i, j))
    tb = pl.BlockSpec((ts, HEAD), lambda i, j: (i, 0))
    return pl.pallas_call(
        body, name=name, grid=(s // ts, nh), in_specs=[blk, tb, tb, tb], out_specs=blk,
        out_shape=jax.ShapeDtypeStruct((s, cols), out_dtype),
        compiler_params=_params("parallel", "parallel"))(x, c, s1, s2)


ATTN_TQ = 256


def attn_fwd(name, parts, v, v_group, nheads, scale):
    s = v.shape[0]
    tq = ATTN_TQ
    npart = len(parts)

    def body(*refs):
        v_ref, o_ref, lse_ref = refs[2 * npart:]
        sc = None
        for p in range(npart):
            t = _dot(refs[2 * p][...], refs[2 * p + 1][...], NT)
            sc = t if sc is None else sc + t
        sc = sc * scale
        m = jnp.max(sc, axis=-1, keepdims=True)
        e = jnp.exp(sc - m)
        l = jnp.sum(e, axis=-1, keepdims=True)
        o_ref[...] = _dot((e / l).astype(BF16), v_ref[...])
        lse_ref[...] = jnp.broadcast_to(m + jnp.log(l), (tq, HEAD))

    in_specs, ins = [], []
    for q, qoff, k, kg in parts:
        in_specs.append(pl.BlockSpec((tq, HEAD), lambda h, i, qoff=qoff: (i, qoff + h)))
        in_specs.append(pl.BlockSpec((s, HEAD), lambda h, i, kg=kg: (0, h // kg)))
        ins += [q, k]
    in_specs.append(pl.BlockSpec((s, HEAD), lambda h, i: (0, h // v_group)))
    out_blk = pl.BlockSpec((tq, HEAD), lambda h, i: (i, h))
    return pl.pallas_call(
        body, name=name, grid=(nheads, s // tq), in_specs=in_specs, out_specs=[out_blk, out_blk],
        out_shape=[jax.ShapeDtypeStruct((s, nheads * HEAD), F32)] * 2,
        compiler_params=_params("parallel", "parallel"))(*ins, v)


def attn_bwd(name, parts, v, v_group, lse, do, nheads, scale):
    s = v.shape[0]
    tq = ATTN_TQ
    npart = len(parts)

    def body(*refs):
        v_ref, lse_ref, do_ref = refs[2 * npart:2 * npart + 3]
        outs = refs[2 * npart + 3:]
        dq_refs, dk_refs, dv_ref = outs[:npart], outs[npart:2 * npart], outs[2 * npart]
        h, i = pl.program_id(0), pl.program_id(1)
        sc = None
        for p in range(npart):
            t = _dot(refs[2 * p][...], refs[2 * p + 1][...], NT)
            sc = t if sc is None else sc + t
        pr = jnp.exp(sc * scale - lse_ref[:, 0:1])
        dov = do_ref[...]
        dp = _dot(dov, v_ref[...], NT)
        delta = jnp.sum(pr * dp, axis=-1, keepdims=True)
        ds = (pr * (dp - delta) * scale).astype(BF16)

        @pl.when((i == 0) & (h % v_group == 0))
        def _():
            dv_ref[...] = jnp.zeros_like(dv_ref)

        dv_ref[...] += _dot(pr.astype(BF16), dov, TN)
        for p in range(npart):
            kg = parts[p][3]
            dq_refs[p][...] = _dot(ds, refs[2 * p + 1][...])

            @pl.when((i == 0) & (h % kg == 0))
            def _(p=p):
                dk_refs[p][...] = jnp.zeros_like(dk_refs[p])

            dk_refs[p][...] += _dot(ds, refs[2 * p][...], TN)

    in_specs, ins = [], []
    for q, qoff, k, kg in parts:
        in_specs.append(pl.BlockSpec((tq, HEAD), lambda h, i, qoff=qoff: (i, qoff + h)))
        in_specs.append(pl.BlockSpec((s, HEAD), lambda h, i, kg=kg: (0, h // kg)))
        ins += [q, k]
    hq_blk = pl.BlockSpec((tq, HEAD), lambda h, i: (i, h))
    in_specs += [pl.BlockSpec((s, HEAD), lambda h, i: (0, h // v_group)), hq_blk, hq_blk]
    out_specs = [hq_blk] * npart
    out_shape = [jax.ShapeDtypeStruct((s, nheads * HEAD), F32)] * npart
    for q, qoff, k, kg in parts:
        out_specs.append(pl.BlockSpec((s, HEAD), lambda h, i, kg=kg: (0, h // kg)))
        out_shape.append(jax.ShapeDtypeStruct((s, nheads // kg * HEAD), F32))
    out_specs.append(pl.BlockSpec((s, HEAD), lambda h, i: (0, h // v_group)))
    out_shape.append(jax.ShapeDtypeStruct((s, nheads // v_group * HEAD), F32))
    res = pl.pallas_call(
        body, name=name, grid=(nheads, s // tq), in_specs=in_specs, out_specs=out_specs, out_shape=out_shape,
        compiler_params=_params("arbitrary", "arbitrary"))(*ins, v, lse, do)
    return list(res[:npart]), list(res[npart:2 * npart]), res[2 * npart]


def _band_windows(lf):
    for ib in range(lf // HEAD):
        q0 = ib * HEAD
        yield q0, max(0, q0 - B_HALF), min(lf, q0 + HEAD + B_HALF)


def _band_mask(q0, lo, hi):
    qpos = q0 + lax.broadcasted_iota(jnp.int32, (HEAD, hi - lo), 0)
    kpos = lo + lax.broadcasted_iota(jnp.int32, (HEAD, hi - lo), 1)
    return jnp.abs(qpos - kpos) <= B_HALF


def _band_specs(d, q_cols, q_off):
    s_q = pl.BlockSpec
    return s_q, (lambda r, h: (0, r * (q_cols // HEAD) + q_off + h)), (lambda r, h: (0, r * B_HEADS + h))


def band_fwd(name, q, q_off, k, v, d):
    s = k.shape[0]
    lf = s // d
    scale = HEAD ** -0.5

    def body(q_ref, k_ref, v_ref, o_ref, lse_ref):
        for q0, lo, hi in _band_windows(lf):
            sc = _dot(q_ref[q0:q0 + HEAD, :], k_ref[lo:hi, :], NT) * scale
            sc = jnp.where(_band_mask(q0, lo, hi), sc, -1e30)
            m = jnp.max(sc, axis=-1, keepdims=True)
            e = jnp.exp(sc - m)
            l = jnp.sum(e, axis=-1, keepdims=True)
            o_ref[q0:q0 + HEAD, :] = _dot((e / l).astype(BF16), v_ref[lo:hi, :])
            lse_ref[q0:q0 + HEAD, :] = jnp.broadcast_to(m + jnp.log(l), (HEAD, HEAD))

    _, qmap, kmap = _band_specs(d, q.shape[1], q_off)
    blk = lambda f: pl.BlockSpec((lf, HEAD), f)
    o, lse = pl.pallas_call(
        body, name=name, grid=(d, B_HEADS), in_specs=[blk(qmap), blk(kmap), blk(kmap)],
        out_specs=[blk(kmap), blk(kmap)], out_shape=[jax.ShapeDtypeStruct((lf, d * B_W), F32)] * 2,
        compiler_params=_params("parallel", "parallel"))(
            q.reshape(lf, d * q.shape[1]), k.reshape(lf, d * B_W), v.reshape(lf, d * B_W))
    return o.reshape(s, B_W), lse.reshape(s, B_W)


def band_bwd(name, q, q_off, k, v, dy, lse, delta, d):
    s = k.shape[0]
    lf = s // d
    scale = HEAD ** -0.5

    def body(q_ref, k_ref, v_ref, dy_ref, lse_ref, dl_ref, dq_ref, dk_ref, dv_ref):
        dk_ref[...] = jnp.zeros_like(dk_ref)
        dv_ref[...] = jnp.zeros_like(dv_ref)
        for q0, lo, hi in _band_windows(lf):
            qv, kv, vv = q_ref[q0:q0 + HEAD, :], k_ref[lo:hi, :], v_ref[lo:hi, :]
            dyv = dy_ref[q0:q0 + HEAD, :]
            sc = _dot(qv, kv, NT) * scale
            pr = jnp.where(_band_mask(q0, lo, hi), jnp.exp(sc - lse_ref[q0:q0 + HEAD, 0:1]), 0.0)
            dp = _dot(dyv, vv, NT)
            ds = (pr * (dp - dl_ref[q0:q0 + HEAD, 0:1]) * scale).astype(BF16)
            dv_ref[lo:hi, :] += _dot(pr.astype(BF16), dyv, TN)
            dq_ref[q0:q0 + HEAD, :] = _dot(ds, kv)
            dk_ref[lo:hi, :] += _dot(ds, qv, TN)

    _, qmap, kmap = _band_specs(d, q.shape[1], q_off)
    blk = lambda f: pl.BlockSpec((lf, HEAD), f)
    fold = lambda a: a.reshape(lf, d * a.shape[1])
    res = pl.pallas_call(
        body, name=name, grid=(d, B_HEADS), in_specs=[blk(qmap)] + [blk(kmap)] * 5,
        out_specs=[blk(kmap)] * 3, out_shape=[jax.ShapeDtypeStruct((lf, d * B_W), F32)] * 3,
        compiler_params=_params("parallel", "parallel"))(fold(q), fold(k), fold(v), fold(dy), fold(lse), fold(delta))
    return tuple(r.reshape(s, B_W) for r in res)


def mix_fwd(name, outs, lses):
    s, w = outs[0].shape
    ts = _rows(s, w)

    def body(o0, o1, o2, l0, l1, l2, y_ref, lse_ref):
        a, b, c = l0[...], l1[...], l2[...]
        m = jnp.maximum(jnp.maximum(a, b), c)
        ea, eb, ec = jnp.exp(a - m), jnp.exp(b - m), jnp.exp(c - m)
        den = ea + eb + ec
        y_ref[...] = (ea / den) * o0[...] + (eb / den) * o1[...] + (ec / den) * o2[...]
        lse_ref[...] = m + jnp.log(den)

    blk = pl.BlockSpec((ts, w), lambda i: (i, 0))
    return pl.pallas_call(
        body, name=name, grid=(s // ts,), in_specs=[blk] * 6, out_specs=[blk, blk],
        out_shape=[jax.ShapeDtypeStruct((s, w), F32)] * 2, compiler_params=_params("parallel"))(*outs, *lses)


def head_rowdot(name, dy, y):
    s, cols = y.shape
    ts = _rows(s, HEAD)

    def body(dy_ref, y_ref, dl_ref, dyb_ref):
        dyv = dy_ref[...]
        dl_ref[...] = jnp.broadcast_to(jnp.sum(dyv * y_ref[...], axis=-1, keepdims=True), (ts, HEAD))
        dyb_ref[...] = dyv.astype(BF16)

    blk = pl.BlockSpec((ts, HEAD), lambda i, j: (i, j))
    return pl.pallas_call(
        body, name=name, grid=(s // ts, cols // HEAD), in_specs=[blk, blk], out_specs=[blk, blk],
        out_shape=[jax.ShapeDtypeStruct((s, cols), F32), jax.ShapeDtypeStruct((s, cols), BF16)],
        compiler_params=_params("parallel", "parallel"))(dy, y)


def add3(name, a, b, c, out_dtype=F32):
    s, w = a.shape
    ts = _rows(s, w)

    def body(a_ref, b_ref, c_ref, o_ref):
        o_ref[...] = (a_ref[...] + b_ref[...] + c_ref[...]).astype(o_ref.dtype)

    blk = pl.BlockSpec((ts, w), lambda i: (i, 0))
    return pl.pallas_call(
        body, name=name, grid=(s // ts,), in_specs=[blk] * 3, out_specs=blk,
        out_shape=jax.ShapeDtypeStruct((s, w), out_dtype), compiler_params=_params("parallel"))(a, b, c)


FFN_TN = 256


def _conv(h, cw):
    s = h.shape[0]
    row = lax.broadcasted_iota(jnp.int32, h.shape, 0)
    prev = jnp.where(row == 0, 0.0, pltpu.roll(h, 1, 0))
    nxt = jnp.where(row == s - 1, 0.0, pltpu.roll(h, s - 1, 0))
    return prev * cw[0:1, :] + h * cw[1:2, :] + nxt * cw[2:3, :] + cw[3:4, :], prev, nxt


def _sigmoid(x):
    return 1.0 / (1.0 + jnp.exp(-x))


def ffn_up(name, xn, w_up, cwb):
    s, dm = xn.shape
    quarter = w_up.shape[3]
    fh = 2 * quarter
    tn = FFN_TN
    per = quarter // tn

    def body(x_ref, w_ref, cw_ref, h_ref, act_ref):
        xv = x_ref[...]
        hg = _dot(xv, w_ref[0])
        hu = _dot(xv, w_ref[1])
        h_ref[0] = hg
        h_ref[1] = hu
        gc, _, _ = _conv(hg, cw_ref[0])
        uc, _, _ = _conv(hu, cw_ref[1])
        act_ref[...] = (gc * _sigmoid(gc) * uc).astype(BF16)

    return pl.pallas_call(
        body, name=name, grid=(fh // tn,),
        in_specs=[pl.BlockSpec((s, dm), lambda t: (0, 0)),
                  pl.BlockSpec((2, None, dm, tn), lambda t: (0, t // per, 0, t % per)),
                  pl.BlockSpec((2, 8, tn), lambda t: (0, 0, t))],
        out_specs=[pl.BlockSpec((2, s, tn), lambda t: (0, 0, t)), pl.BlockSpec((s, tn), lambda t: (0, t))],
        out_shape=[jax.ShapeDtypeStruct((2, s, fh), F32), jax.ShapeDtypeStruct((s, fh), BF16)],
        compiler_params=_params("parallel"))(xn, w_up, cwb)


def ffn_gate_bwd(name, h, dact, cwb):
    _, s, fh = h.shape
    tn = FFN_TN

    def body(h_ref, da_ref, cw_ref, dh_ref, dcw_ref):
        gc, gp, gn = _conv(h_ref[0], cw_ref[0])
        uc, up, un = _conv(h_ref[1], cw_ref[1])
        sg = _sigmoid(gc)
        da = da_ref[...]
        dgc = da * uc * (sg * (1.0 + gc * (1.0 - sg)))
        duc = da * (gc * sg)
        row = lax.broadcasted_iota(jnp.int32, da.shape, 0)
        for idx, (hv, prev, nxt, dc) in enumerate(((h_ref[0], gp, gn, dgc), (h_ref[1], up, un, duc))):
            cw = cw_ref[idx]
            from_next = jnp.where(row == s - 1, 0.0, pltpu.roll(dc, s - 1, 0))
            from_prev = jnp.where(row == 0, 0.0, pltpu.roll(dc, 1, 0))
            dh_ref[idx] = (from_next * cw[0:1, :] + dc * cw[1:2, :] + from_prev * cw[2:3, :]).astype(BF16)
            dcw_ref[idx, 0:1, :] = jnp.sum(prev * dc, axis=0, keepdims=True)
            dcw_ref[idx, 1:2, :] = jnp.sum(hv * dc, axis=0, keepdims=True)
            dcw_ref[idx, 2:3, :] = jnp.sum(nxt * dc, axis=0, keepdims=True)
            dcw_ref[idx, 3:4, :] = jnp.sum(dc, axis=0, keepdims=True)
            dcw_ref[idx, 4:8, :] = jnp.zeros((4, tn), F32)

    return pl.pallas_call(
        body, name=name, grid=(fh // tn,),
        in_specs=[pl.BlockSpec((2, s, tn), lambda t: (0, 0, t)), pl.BlockSpec((s, tn), lambda t: (0, t)),
                  pl.BlockSpec((2, 8, tn), lambda t: (0, 0, t))],
        out_specs=[pl.BlockSpec((2, s, tn), lambda t: (0, 0, t)), pl.BlockSpec((2, 8, tn), lambda t: (0, 0, t))],
        out_shape=[jax.ShapeDtypeStruct((2, s, fh), BF16), jax.ShapeDtypeStruct((2, 8, fh), F32)],
        compiler_params=_params("parallel"))(h, dact, cwb)


def ffn_dx(name, dh, w_up):
    _, s, fh = dh.shape
    dm, quarter = w_up.shape[2], w_up.shape[3]
    tk = _tile(quarter, 1536)
    per = quarter // tk
    tm, tn = _tile(s, 1024), _tile(dm, 1024)
    grid = (s // tm, dm // tn, 4 * per)
    a_spec = pl.BlockSpec((None, tm, tk), lambda i, j, k: (k // (2 * per), i, k % (2 * per)))
    b_spec = pl.BlockSpec((None, None, tn, tk), lambda i, j, k: (k // (2 * per), (k // per) % 2, j, k % per))
    o_spec = pl.BlockSpec((tm, tn), lambda i, j, k: (i, j))
    return _matmul_call(name, NT, grid, a_spec, b_spec, o_spec, jax.ShapeDtypeStruct((s, dm), F32), (tm, tn))(dh, w_up)


def ffn_dw_up(name, xn, dh):
    _, s, fh = dh.shape
    dm = xn.shape[1]
    quarter = fh // 2
    tn = _tile(quarter, 1536)
    per = quarter // tn
    tm, tk = _tile(dm, 1024), _tile(s, 1024)
    grid = (dm // tm, 4 * per, s // tk)
    a_spec = pl.BlockSpec((tk, tm), lambda i, j, k: (k, i))
    b_spec = pl.BlockSpec((None, tk, tn), lambda i, j, k: (j // (2 * per), k, j % (2 * per)))
    o_spec = pl.BlockSpec((None, None, tm, tn), lambda i, j, k: (j // (2 * per), (j // per) % 2, i, j % per))
    return _matmul_call(name, TN, grid, a_spec, b_spec, o_spec,
                        jax.ShapeDtypeStruct((2, 2, dm, quarter), F32), (tm, tn))(xn, dh)


def loss_head(name, y, target):
    s, dm = y.shape
    ts = _rows(s, dm)

    def body(y_ref, t_ref, loss_ref, dy_ref, acc):
        i = pl.program_id(0)

        @pl.when(i == 0)
        def _():
            acc[...] = jnp.zeros_like(acc)

        err = y_ref[...] - t_ref[...]
        dy_ref[...] = err * (1.0 / dm)
        acc[...] += jnp.sum(err * err, axis=0, keepdims=True)

        @pl.when(i == s // ts - 1)
        def _():
            loss_ref[...] = jnp.broadcast_to(jnp.sum(acc[...], axis=-1, keepdims=True) * (0.5 / dm), (1, LANES))

    blk = pl.BlockSpec((ts, dm), lambda i: (i, 0))
    loss, dy = pl.pallas_call(
        body, name=name, grid=(s // ts,), in_specs=[blk, blk],
        out_specs=[pl.BlockSpec((1, LANES), lambda i: (0, 0)), blk],
        out_shape=[jax.ShapeDtypeStruct((1, LANES), F32), jax.ShapeDtypeStruct((s, dm), F32)],
        scratch_shapes=[pltpu.VMEM((1, dm), F32)], compiler_params=_params("arbitrary"))(y, target)
    return loss[0, 0], dy


def _c_parts(qc, kvc, krc):
    return [(qc, 0, kvc, None), (qc, C_HEADS, krc, C_HEADS)]


def attn_fwd_c(name, qc, kvc, krc, scale):
    s = qc.shape[0]
    tq = ATTN_TQ

    def body(qn_ref, qr_ref, kn_ref, kr_ref, v_ref, o_ref, lse_ref):
        sc = (_dot(qn_ref[...], kn_ref[...], NT) + _dot(qr_ref[...], kr_ref[...], NT)) * scale
        m = jnp.max(sc, axis=-1, keepdims=True)
        e = jnp.exp(sc - m)
        l = jnp.sum(e, axis=-1, keepdims=True)
        o_ref[...] = _dot((e / l).astype(BF16), v_ref[...])
        lse_ref[...] = jnp.broadcast_to(m + jnp.log(l), (tq, HEAD))

    qb = lambda off: pl.BlockSpec((tq, HEAD), lambda h, i: (i, off + h))
    kb = lambda f: pl.BlockSpec((s, HEAD), f)
    out_blk = pl.BlockSpec((tq, HEAD), lambda h, i: (i, h))
    return pl.pallas_call(
        body, name=name, grid=(C_HEADS, s // tq),
        in_specs=[qb(0), qb(C_HEADS), kb(lambda h, i: (0, 2 * h)), kb(lambda h, i: (0, 0)), kb(lambda h, i: (0, 2 * h + 1))],
        out_specs=[out_blk, out_blk], out_shape=[jax.ShapeDtypeStruct((s, C_W), F32)] * 2,
        compiler_params=_params("parallel", "parallel"))(qc, qc, kvc, krc, kvc)


def attn_bwd_c(name, qc, kvc, krc, lse, do, scale):
    s = qc.shape[0]
    tq = ATTN_TQ

    def body(qn_ref, qr_ref, kn_ref, kr_ref, v_ref, lse_ref, do_ref, dqn_ref, dqr_ref, dkn_ref, dv_ref, dkr_ref):
        h, i = pl.program_id(0), pl.program_id(1)
        sc = (_dot(qn_ref[...], kn_ref[...], NT) + _dot(qr_ref[...], kr_ref[...], NT)) * scale
        pr = jnp.exp(sc - lse_ref[:, 0:1])
        dov = do_ref[...]
        dp = _dot(dov, v_ref[...], NT)
        delta = jnp.sum(pr * dp, axis=-1, keepdims=True)
        ds = (pr * (dp - delta) * scale).astype(BF16)

        @pl.when(i == 0)
        def _():
            dv_ref[...] = jnp.zeros_like(dv_ref)
            dkn_ref[...] = jnp.zeros_like(dkn_ref)

        @pl.when((i == 0) & (h == 0))
        def _():
            dkr_ref[...] = jnp.zeros_like(dkr_ref)

        dv_ref[...] += _dot(pr.astype(BF16), dov, TN)
        dqn_ref[...] = _dot(ds, kn_ref[...])
        dqr_ref[...] = _dot(ds, kr_ref[...])
        dkn_ref[...] += _dot(ds, qn_ref[...], TN)
        dkr_ref[...] += _dot(ds, qr_ref[...], TN)

    qb = lambda off: pl.BlockSpec((tq, HEAD), lambda h, i: (i, off + h))
    kb = lambda f: pl.BlockSpec((s, HEAD), f)
    hq = pl.BlockSpec((tq, HEAD), lambda h, i: (i, h))
    kn_map, v_map, kr_map = (lambda h, i: (0, 2 * h)), (lambda h, i: (0, 2 * h + 1)), (lambda h, i: (0, 0))
    dqc_shape = jax.ShapeDtypeStruct((s, 2 * C_W), F32)
    dqn, dqr, dkn, dv, dkr = pl.pallas_call(
        body, name=name, grid=(C_HEADS, s // tq),
        in_specs=[qb(0), qb(C_HEADS), kb(kn_map), kb(kr_map), kb(v_map), hq, hq],
        out_specs=[hq, hq, kb(lambda h, i: (0, h)), kb(lambda h, i: (0, h)), kb(kr_map)],
        out_shape=[jax.ShapeDtypeStruct((s, C_W), F32)] * 4 + [jax.ShapeDtypeStruct((s, HEAD), F32)],
        compiler_params=_params("arbitrary", "arbitrary"))(qc, qc, kvc, krc, kvc, lse, do)
    del dqc_shape
    return dqn, dqr, dkn, dv, dkr


def _layer_fwd(x, p, tabs):
    sv = {"x": x}
    hn = rmsnorm_fwd("attn_norm", x, p["attn_norm"], x.shape[1], BF16)
    proj = matmul("in_proj", hn, p["w_in"], "nn", F32)
    sv["hn"] = hn
    aq, ak = proj[:, O_AQ:O_AK], proj[:, O_AK:O_AV]
    av = proj[:, O_AV:O_BQ].astype(BF16)
    bq, bk = proj[:, O_BQ:O_BK], proj[:, O_BK:O_BV]
    bv = proj[:, O_BV:O_CQ].astype(BF16)
    cq, ckv, ckr = proj[:, O_CQ:O_CKV], proj[:, O_CKV:O_CKR], proj[:, O_CKR:O_CKR + HEAD]
    sv.update(aq=aq, ak=ak, av=av, bv=bv, cq=cq, ckv=ckv)

    qa = rope("a_q_rope", rmsnorm_fwd("a_q_norm", aq, p["a_q_norm"], HEAD, F32), tabs["a"][0], BF16)
    ka = rope("a_k_rope", rmsnorm_fwd("a_k_norm", ak, p["a_k_norm"], HEAD, F32), tabs["a"][0], BF16)
    ya, lse_a = attn_fwd("a_attn", [(qa, 0, ka, A_HEADS // A_KV)], av, A_HEADS // A_KV, A_HEADS, HEAD ** -0.5)
    sv.update(qa=qa, ka=ka, ya=ya, lse_a=lse_a)

    qb = rope("b_q_rope", bq, tabs["b"][0], BF16)
    kb = rope("b_k_rope", bk, tabs["b"][0], BF16)
    outs, lses = [], []
    for g, d in enumerate(B_DILATIONS):
        o, l = band_fwd(f"b_band{g}", qb, g * B_HEADS, kb, bv, d)
        outs.append(o)
        lses.append(l)
    yb, lse_b = mix_fwd("b_mix", outs, lses)
    sv.update(qb=qb, kb=kb, yb=yb, lse_b=lse_b)

    cqn = rmsnorm_fwd("c_q_norm", cq, p["c_q_norm"], C_RANK, BF16)
    qc_raw = matmul("c_uq", cqn, p["w_uq"], "nn", F32)
    qc = jnp.concatenate([qc_raw[:, :C_W].astype(BF16), rope("c_q_rope", qc_raw[:, C_W:], tabs["c"][0], BF16)], axis=1)
    ckvn = rmsnorm_fwd("c_kv_norm", ckv, p["c_kv_norm"], C_RANK, BF16)
    kvc = matmul("c_ukv", ckvn, p["w_ukv"], "nn", BF16)
    krc = rope("c_k_rope", ckr, tabs["c"][0], BF16)
    c_scale = (HEAD + C_ROPE) ** -0.5
    yc, lse_c = attn_fwd_c("c_attn", qc, kvc, krc, c_scale)
    sv.update(cqn=cqn, ckvn=ckvn, qc=qc, kvc=kvc, krc=krc, yc=yc, lse_c=lse_c)

    g_a, g_b, g_c = p["out_norm"][:A_W], p["out_norm"][A_W:A_W + B_W], p["out_norm"][A_W + B_W:]
    y = jnp.concatenate([rmsnorm_fwd("out_norm_a", ya, g_a, A_W, BF16), rmsnorm_fwd("out_norm_b", yb, g_b, B_W, BF16),
                         rmsnorm_fwd("out_norm_c", yc, g_c, C_W, BF16)], axis=1)
    x1 = matmul("out_proj", y, p["w_out"], "nn", F32, add=x)
    sv.update(y=y, x1=x1)

    xn = rmsnorm_fwd("ffn_norm", x1, p["ffn_norm"], x.shape[1], BF16)
    h, act = ffn_up("ffn_up", xn, p["w_up"], p["cwb"])
    x2 = matmul("ffn_down", act, p["w_down"], "nn", F32, add=x1)
    sv.update(xn=xn, h=h, act=act)
    return x2, sv


def _layer_bwd(dx2, sv, p, tabs):
    g = {}
    dm = dx2.shape[1]
    dx2b = dx2.astype(BF16)
    dact = matmul("ffn_down_dx", dx2b, p["w_down"], "nt", F32)
    g["w_down"] = matmul("ffn_down_dw", sv["act"], dx2b, "tn", F32)
    dh, g["cwb"] = ffn_gate_bwd("ffn_gate_bwd", sv["h"], dact, p["cwb"])
    dxn = ffn_dx("ffn_up_dx", dh, p["w_up"])
    g["w_up"] = ffn_dw_up("ffn_up_dw", sv["xn"], dh)
    dx1, dx1b, g["ffn_norm"] = rmsnorm_bwd("ffn_norm_bwd", dxn, sv["x1"], p["ffn_norm"], dm, (F32, BF16), add=dx2)

    dy = matmul("out_proj_dx", dx1b, p["w_out"], "nt", F32)
    g["w_out"] = matmul("out_proj_dw", sv["y"], dx1b, "tn", F32)
    g_a, g_b, g_c = p["out_norm"][:A_W], p["out_norm"][A_W:A_W + B_W], p["out_norm"][A_W + B_W:]
    dya, dga = rmsnorm_bwd("out_norm_a_bwd", dy[:, :A_W], sv["ya"], g_a, A_W, (BF16,))
    dyb, dgb = rmsnorm_bwd("out_norm_b_bwd", dy[:, A_W:A_W + B_W], sv["yb"], g_b, B_W, (F32,))
    dyc, dgc = rmsnorm_bwd("out_norm_c_bwd", dy[:, A_W + B_W:], sv["yc"], g_c, C_W, (BF16,))
    g["out_norm"] = jnp.concatenate([dga, dgb, dgc])

    c_scale = (HEAD + C_ROPE) ** -0.5
    dqn, dqr, dkn, dvc, dkr = attn_bwd_c("c_attn_bwd", sv["qc"], sv["kvc"], sv["krc"], sv["lse_c"], dyc, c_scale)
    s = dx2.shape[0]
    dqc = jnp.concatenate([dqn.astype(BF16), rope("c_q_rope_bwd", dqr, tabs["c"][1], BF16)], axis=1)
    dcqn = matmul("c_uq_dx", dqc, p["w_uq"], "nt", F32)
    g["w_uq"] = matmul("c_uq_dw", sv["cqn"], dqc, "tn", F32)
    dcq, g["c_q_norm"] = rmsnorm_bwd("c_q_norm_bwd", dcqn, sv["cq"], p["c_q_norm"], C_RANK, (BF16,))
    dkvc = jnp.stack([dkn.reshape(s, C_HEADS, HEAD), dvc.reshape(s, C_HEADS, HEAD)], axis=2).reshape(s, 2 * C_W).astype(BF16)
    dckvn = matmul("c_ukv_dx", dkvc, p["w_ukv"], "nt", F32)
    g["w_ukv"] = matmul("c_ukv_dw", sv["ckvn"], dkvc, "tn", F32)
    dckv, g["c_kv_norm"] = rmsnorm_bwd("c_kv_norm_bwd", dckvn, sv["ckv"], p["c_kv_norm"], C_RANK, (BF16,))
    dckr = rope("c_k_rope_bwd", dkr, tabs["c"][1], BF16)

    delta_b, dybb = head_rowdot("b_delta", dyb, sv["yb"])
    dqs, dks, dvs = [], [], []
    for gi, d in enumerate(B_DILATIONS):
        dq_g, dk_g, dv_g = band_bwd(f"b_band{gi}_bwd", sv["qb"], gi * B_HEADS, sv["kb"], sv["bv"], dybb,
                                    sv["lse_b"], delta_b, d)
        dqs.append(dq_g)
        dks.append(dk_g)
        dvs.append(dv_g)
    dbq = rope("b_q_rope_bwd", jnp.concatenate(dqs, axis=1), tabs["b"][1], BF16)
    dbk = rope("b_k_rope_bwd", add3("b_dk_sum", *dks), tabs["b"][1], BF16)
    dbv = add3("b_dv_sum", *dvs, out_dtype=BF16)

    kg = A_HEADS // A_KV
    (dqa,), (dka,), dva = attn_bwd("a_attn_bwd", [(sv["qa"], 0, sv["ka"], kg)], sv["av"], kg, sv["lse_a"], dya,
                                   A_HEADS, HEAD ** -0.5)
    daq, g["a_q_norm"] = rmsnorm_bwd("a_q_norm_bwd", rope("a_q_rope_bwd", dqa, tabs["a"][1], F32), sv["aq"],
                                     p["a_q_norm"], HEAD, (BF16,))
    dak, g["a_k_norm"] = rmsnorm_bwd("a_k_norm_bwd", rope("a_k_rope_bwd", dka, tabs["a"][1], F32), sv["ak"],
                                     p["a_k_norm"], HEAD, (BF16,))

    dproj = jnp.concatenate([daq, dak, dva.astype(BF16), dbq, dbk, dbv, dcq, dckv, dckr,
                             jnp.zeros((s, PROJ_W - O_CKR - HEAD), BF16)], axis=1)
    dhn = matmul("in_proj_dx", dproj, p["w_in"], "nt", F32)
    g["w_in"] = matmul("in_proj_dw", sv["hn"], dproj, "tn", F32)
    dx, g["attn_norm"] = rmsnorm_bwd("attn_norm_bwd", dhn, sv["x"], p["attn_norm"], dm, (F32,), add=dx1)
    return dx, g


def _local_step(x, target, layers, final_norm, tabs):
    saved = []
    for p in layers:
        x, sv = _layer_fwd(x, p, tabs)
        saved.append(sv)
    dm = x.shape[1]
    yf = rmsnorm_fwd("final_norm", x, final_norm, dm, F32)
    loss, dyf = loss_head("loss_head", yf, target)
    dx, d_final = rmsnorm_bwd("final_norm_bwd", dyf, x, final_norm, dm, (F32,))
    grads = [None] * len(layers)
    for l in reversed(range(len(layers))):
        dx, grads[l] = _layer_bwd(dx, saved[l], layers[l], tabs)
    return loss, dx, grads, d_final


HBM = pl.BlockSpec(memory_space=pl.ANY)


def _place():
    x, y, c = lax.axis_index("x"), lax.axis_index("y"), lax.axis_index("c")
    others = [(1 - x, y), (x, 1 - y), (1 - x, 1 - y)]
    return x, y, c, 2 * x + y, others


def _remote(src, dst, send_sem, recv_sem, device):
    return pltpu.make_async_remote_copy(src_ref=src, dst_ref=dst, send_sem=send_sem, recv_sem=recv_sem,
                                        device_id=device, device_id_type=MESH)


def gather_shards(name, shards, halved):
    n = len(shards)

    def body(*refs):
        ins, outs = refs[:n], refs[n:2 * n]
        send_sems, recv_sems, fwd_send, fwd_recv, local_sems = refs[2 * n:]
        x, y, c, chip, others = _place()
        sibling = (x, y, 1 - c)
        pending = []
        for t in range(n):
            cp = pltpu.make_async_copy(ins[t], outs[t].at[chip], local_sems.at[t])
            cp.start()
            pending.append(cp.wait)

        def rows(t, core):
            if not halved[t]:
                return slice(None)
            half = shards[t].shape[0] // 2
            return pl.ds(core * half, half)

        for t in range(n):
            for k, (ox, oy) in enumerate(others):
                cp = _remote(ins[t].at[rows(t, c)], outs[t].at[chip, rows(t, c)], send_sems.at[t, k],
                             recv_sems.at[t, k], (ox, oy, c))
                cp.start()
                pending.append(cp.wait_send)
        for t in range(n):
            for k, (ox, oy) in enumerate(others):
                landed = outs[t].at[2 * ox + oy, rows(t, c)]
                _remote(landed, landed, send_sems.at[t, k], recv_sems.at[t, k], (ox, oy, c)).wait_recv()
                if halved[t]:
                    cp = _remote(landed, landed, fwd_send.at[t, k], fwd_recv.at[t, k], sibling)
                    cp.start()
                    pending.append(cp.wait_send)
        for t in range(n):
            if halved[t]:
                for k, (ox, oy) in enumerate(others):
                    theirs = outs[t].at[2 * ox + oy, rows(t, 1 - c)]
                    _remote(theirs, theirs, fwd_send.at[t, k], fwd_recv.at[t, k], sibling).wait_recv()
        for wait in pending:
            wait()

    return pl.pallas_call(
        body, name=name, in_specs=[HBM] * n, out_specs=[HBM] * n,
        out_shape=[jax.ShapeDtypeStruct((4,) + s.shape, s.dtype) for s in shards],
        scratch_shapes=[pltpu.SemaphoreType.DMA((n, 3))] * 4 + [pltpu.SemaphoreType.DMA((n,))],
        compiler_params=pltpu.CompilerParams(has_side_effects=True))(*shards)


def sibling_halves(name, grads):
    n = len(grads)

    def body(*refs):
        ins, outs = refs[:n], refs[n:2 * n]
        send_sems, recv_sems = refs[2 * n:]
        x, y, c, _, _ = _place()
        copies = []
        for t in range(n):
            half = grads[t].shape[1] // 2
            cp = _remote(ins[t].at[:, pl.ds((1 - c) * half, half), :], outs[t], send_sems.at[t], recv_sems.at[t],
                         (x, y, 1 - c))
            cp.start()
            copies.append(cp)
        for cp in copies:
            cp.wait()

    return pl.pallas_call(
        body, name=name, in_specs=[HBM] * n, out_specs=[HBM] * n,
        out_shape=[jax.ShapeDtypeStruct((4, g.shape[1] // 2, g.shape[2]), g.dtype) for g in grads],
        scratch_shapes=[pltpu.SemaphoreType.DMA((n,))] * 2,
        compiler_params=pltpu.CompilerParams(has_side_effects=True))(*grads)


def add_half(name, core, grad, theirs):
    _, r, cols = grad.shape
    half = r // 2
    tr = _rows(half, cols)

    def body(core_ref, g_ref, t_ref, o_ref):
        o_ref[...] = (g_ref[...] + t_ref[...]).astype(o_ref.dtype)

    return pl.pallas_call(
        body, name=name,
        grid_spec=pltpu.PrefetchScalarGridSpec(
            num_scalar_prefetch=1, grid=(4, half // tr),
            in_specs=[pl.BlockSpec((None, None, tr, cols), lambda j, i, cr: (j, cr[0], i, 0)),
                      pl.BlockSpec((None, tr, cols), lambda j, i, cr: (j, i, 0))],
            out_specs=pl.BlockSpec((None, tr, cols), lambda j, i, cr: (j, i, 0))),
        out_shape=jax.ShapeDtypeStruct((4, half, cols), BF16),
        compiler_params=_params("parallel", "parallel"))(core, grad.reshape(4, 2, half, cols), theirs)


def exchange_shards(name, sums):
    n = len(sums)

    def body(*refs):
        ins, outs = refs[:n], refs[n:2 * n]
        send_sems, recv_sems, local_sems = refs[2 * n:]
        x, y, c, chip, others = _place()
        pending = []
        for t in range(n):
            cp = pltpu.make_async_copy(ins[t].at[chip], outs[t].at[chip], local_sems.at[t])
            cp.start()
            pending.append(cp.wait)
            for k, (ox, oy) in enumerate(others):
                cp = _remote(ins[t].at[2 * ox + oy], outs[t].at[chip], send_sems.at[t, k], recv_sems.at[t, k],
                             (ox, oy, c))
                cp.start()
                pending.append(cp.wait_send)
        for t in range(n):
            for k, (ox, oy) in enumerate(others):
                slot = outs[t].at[2 * ox + oy]
                _remote(slot, slot, send_sems.at[t, k], recv_sems.at[t, k], (ox, oy, c)).wait_recv()
        for wait in pending:
            wait()

    return pl.pallas_call(
        body, name=name, in_specs=[HBM] * n, out_specs=[HBM] * n,
        out_shape=[jax.ShapeDtypeStruct(s.shape, s.dtype) for s in sums],
        scratch_shapes=[pltpu.SemaphoreType.DMA((n, 3))] * 2 + [pltpu.SemaphoreType.DMA((n,))],
        compiler_params=pltpu.CompilerParams(has_side_effects=True))(*sums)


def sum_chips(name, parts):
    _, half, cols = parts.shape
    tr = _rows(half, cols)

    def body(p_ref, o_ref):
        acc = p_ref[0].astype(F32)
        for j in range(1, 4):
            acc = acc + p_ref[j].astype(F32)
        o_ref[...] = acc

    return pl.pallas_call(
        body, name=name, grid=(half // tr,), in_specs=[pl.BlockSpec((4, tr, cols), lambda i: (0, i, 0))],
        out_specs=pl.BlockSpec((tr, cols), lambda i: (i, 0)), out_shape=jax.ShapeDtypeStruct((half, cols), F32),
        compiler_params=_params("parallel"))(parts)


def join_halves(name, halves):
    n = len(halves)

    def body(*refs):
        ins, outs = refs[:n], refs[n:2 * n]
        send_sems, recv_sems, local_sems = refs[2 * n:]
        x, y, c, _, _ = _place()
        pending = []
        for t in range(n):
            half = halves[t].shape[0]
            mine = outs[t].at[pl.ds(c * half, half)]
            cp = pltpu.make_async_copy(ins[t], mine, local_sems.at[t])
            cp.start()
            pending.append(cp.wait)
            cp = _remote(ins[t], mine, send_sems.at[t], recv_sems.at[t], (x, y, 1 - c))
            cp.start()
            pending.append(cp.wait_send)
        for t in range(n):
            half = halves[t].shape[0]
            theirs = outs[t].at[pl.ds((1 - c) * half, half)]
            _remote(theirs, theirs, send_sems.at[t], recv_sems.at[t], (x, y, 1 - c)).wait_recv()
        for wait in pending:
            wait()

    return pl.pallas_call(
        body, name=name, in_specs=[HBM] * n, out_specs=[HBM] * n,
        out_shape=[jax.ShapeDtypeStruct((2 * h.shape[0], h.shape[1]), h.dtype) for h in halves],
        scratch_shapes=[pltpu.SemaphoreType.DMA((n,))] * 3,
        compiler_params=pltpu.CompilerParams(has_side_effects=True))(*halves)


def allreduce_small(name, buf):
    rows = buf.shape[0]

    def body(in_ref, out_ref, slots, send_sems, recv_sems):
        x, y, c, _, _ = _place()
        me = 4 * x + 2 * y + c
        slots[me] = in_ref[...]
        peers = []
        for k in range(1, 8):
            px = 1 - x if k & 4 else x
            py = 1 - y if k & 2 else y
            pc = 1 - c if k & 1 else c
            peers.append((px, py, pc))
        copies = []
        for k, peer in enumerate(peers):
            cp = _remote(in_ref, slots.at[me], send_sems.at[k], recv_sems.at[k], peer)
            cp.start()
            copies.append(cp)
        for k, (px, py, pc) in enumerate(peers):
            slot = slots.at[4 * px + 2 * py + pc]
            _remote(slot, slot, send_sems.at[k], recv_sems.at[k], (px, py, pc)).wait_recv()
        for cp in copies:
            cp.wait_send()
        acc = slots[0]
        for d in range(1, 8):
            acc = acc + slots[d]
        out_ref[...] = acc

    vm = pl.BlockSpec(memory_space=pltpu.VMEM)
    return pl.pallas_call(
        body, name=name, in_specs=[vm], out_specs=vm, out_shape=jax.ShapeDtypeStruct(buf.shape, F32),
        scratch_shapes=[pltpu.VMEM((8, rows, LANES), F32), pltpu.SemaphoreType.DMA((7,)), pltpu.SemaphoreType.DMA((7,))],
        compiler_params=pltpu.CompilerParams(has_side_effects=True, vmem_limit_bytes=VMEM_LIMIT))(buf)


def cast_layer(name, w, layer):
    _, r, cols = w.shape
    tr = _rows(r, cols)

    def body(w_ref, o_ref):
        o_ref[...] = w_ref[...].astype(BF16)

    return pl.pallas_call(
        body, name=name, grid=(r // tr,), in_specs=[pl.BlockSpec((None, tr, cols), lambda i: (layer, i, 0))],
        out_specs=pl.BlockSpec((tr, cols), lambda i: (i, 0)), out_shape=jax.ShapeDtypeStruct((r, cols), BF16),
        compiler_params=_params("parallel"))(w)


def _adamw_math(g, w, m, v):
    m = ADAM_B1 * m + (1.0 - ADAM_B1) * g
    v = ADAM_B2 * v + (1.0 - ADAM_B2) * (g * g)
    m_hat = m / (1.0 - ADAM_B1 ** ADAM_STEP)
    v_hat = v / (1.0 - ADAM_B2 ** ADAM_STEP)
    delta = -ADAM_LR * (m_hat / (jnp.sqrt(v_hat) + ADAM_EPS) + ADAM_WD * w)
    return delta, m, v


def adamw_layer(name, layer, g, w, m, v, prev=None):
    nl, r, cols = w.shape
    tr = _rows(r, cols, 8)

    def body(*refs):
        g_ref, w_ref, m_ref, v_ref = refs[:4]
        og, od, om, ov = refs[-4:]
        gv = g_ref[...]
        delta, m2, v2 = _adamw_math(gv, w_ref[...], m_ref[...], v_ref[...])
        og[...] = gv
        od[...] = delta
        om[...] = m2
        ov[...] = v2

    lay = pl.BlockSpec((None, tr, cols), lambda i: (layer, i, 0))
    ins = [g, w, m, v] + (list(prev) if prev is not None else [])
    return pl.pallas_call(
        body, name=name, grid=(r // tr,),
        in_specs=[pl.BlockSpec((tr, cols), lambda i: (i, 0)), lay, lay, lay] + ([HBM] * 4 if prev is not None else []),
        out_specs=[lay] * 4, out_shape=[jax.ShapeDtypeStruct((nl, r, cols), F32)] * 4,
        input_output_aliases={4 + k: k for k in range(4)} if prev is not None else {},
        compiler_params=_params("parallel"))(*ins)


def adamw_packed(name, g, w, m, v):
    def body(g_ref, w_ref, m_ref, v_ref, od, om, ov):
        delta, m2, v2 = _adamw_math(g_ref[...], w_ref[...], m_ref[...], v_ref[...])
        od[...] = delta
        om[...] = m2
        ov[...] = v2

    vm = pl.BlockSpec(memory_space=pltpu.VMEM)
    return pl.pallas_call(
        body, name=name, in_specs=[vm] * 4, out_specs=[vm] * 3, out_shape=[jax.ShapeDtypeStruct(g.shape, F32)] * 3,
        compiler_params=pltpu.CompilerParams(vmem_limit_bytes=VMEM_LIMIT))(g, w, m, v)


def _pack(arrays):
    flat = jnp.concatenate([a.reshape(-1) for a in arrays])
    pad = (-flat.shape[0]) % (8 * LANES)
    return jnp.pad(flat, (0, pad)).reshape(-1, LANES)


def _unpack(buf, shapes):
    flat = buf.reshape(-1)
    out, off = [], 0
    for shp in shapes:
        size = int(np.prod(shp))
        out.append(flat[off:off + size].reshape(shp))
        off += size
    return out


BIG = ("w_in", "w_uq", "w_ukv", "w_out", "w_up", "w_down")
COLUMN_CUT = ("w_in", "w_uq", "w_ukv", "w_up")
SMALL = ("attn_norm", "a_q_norm", "a_k_norm", "c_q_norm", "c_kv_norm", "out_norm", "ffn_norm", "conv_b", "final_norm")
WEIGHTS = ("attn_norm", "w_in", "a_q_norm", "a_k_norm", "c_q_norm", "c_kv_norm", "w_uq", "w_ukv", "out_norm", "w_out",
           "ffn_norm", "w_up", "conv_w", "conv_b", "w_down", "final_norm")
INPUTS = ("x",) + WEIGHTS + ("loss_target",) + tuple("m_" + n for n in WEIGHTS) + tuple("v_" + n for n in WEIGHTS)


def _columns(g):
    return jnp.transpose(g, (1, 0, 2)).reshape(g.shape[1], 4 * g.shape[2])


def _uncolumns(w):
    r, c4 = w.shape
    return jnp.transpose(w.reshape(r, 4, c4 // 4), (1, 0, 2))


def _compute_layouts(full, conv_w, small, layer):
    p = {n: small[n][layer] for n in SMALL if n != "final_norm"}
    w_in = _columns(full["w_in"])
    p["w_in"] = jnp.pad(w_in, ((0, 0), (0, PROJ_W - IN_W)))
    w_uq = _columns(full["w_uq"]).reshape(C_RANK, C_HEADS, HEAD + C_ROPE)
    p["w_uq"] = jnp.concatenate([w_uq[:, :, :HEAD].reshape(C_RANK, C_W),
                                 jnp.pad(w_uq[:, :, HEAD:], ((0, 0), (0, 0), (0, HEAD - C_ROPE))).reshape(C_RANK, C_W)], axis=1)
    p["w_ukv"] = _columns(full["w_ukv"])
    p["w_out"] = full["w_out"].reshape(-1, full["w_out"].shape[2])
    up = full["w_up"]
    p["w_up"] = up.reshape(2, 2, up.shape[1], up.shape[2])
    p["w_down"] = full["w_down"].reshape(-1, full["w_down"].shape[2])
    fh = conv_w.shape[1] // 2
    taps = jnp.transpose(conv_w.reshape(3, 2, fh), (1, 0, 2))
    p["cwb"] = jnp.concatenate([taps, small["conv_b"][layer].reshape(2, 1, fh), jnp.zeros((2, 4, fh), F32)], axis=1)
    return p


def _shard_layouts(g):
    out = {}
    out["w_in"] = _uncolumns(g["w_in"][:, :IN_W])
    uq = g["w_uq"]
    uq = jnp.concatenate([uq[:, :C_W].reshape(C_RANK, C_HEADS, HEAD),
                          uq[:, C_W:].reshape(C_RANK, C_HEADS, HEAD)[:, :, :C_ROPE]], axis=2)
    out["w_uq"] = _uncolumns(uq.reshape(C_RANK, C_HEADS * (HEAD + C_ROPE)))
    out["w_ukv"] = _uncolumns(g["w_ukv"])
    out["w_out"] = g["w_out"].reshape(4, -1, g["w_out"].shape[1])
    up = g["w_up"]
    out["w_up"] = up.reshape(4, up.shape[2], up.shape[3])
    out["w_down"] = g["w_down"].reshape(4, -1, g["w_down"].shape[1])
    dcwb = g["cwb"]
    fh = dcwb.shape[2]
    conv_w = jnp.transpose(dcwb[:, 0:3, :], (1, 0, 2)).reshape(3, 2 * fh)
    conv_b = dcwb[:, 3, :].reshape(2 * fh)
    return out, conv_w, conv_b


def kernel(x, attn_norm, w_in, a_q_norm, a_k_norm, c_q_norm, c_kv_norm, w_uq, w_ukv, out_norm, w_out, ffn_norm, w_up, conv_w, conv_b, w_down, final_norm, loss_target, m_attn_norm, m_w_in, m_a_q_norm, m_a_k_norm, m_c_q_norm, m_c_kv_norm, m_w_uq, m_w_ukv, m_out_norm, m_w_out, m_ffn_norm, m_w_up, m_conv_w, m_conv_b, m_w_down, m_final_norm, v_attn_norm, v_w_in, v_a_q_norm, v_a_k_norm, v_c_q_norm, v_c_kv_norm, v_w_uq, v_w_ukv, v_out_norm, v_w_out, v_ffn_norm, v_w_up, v_conv_w, v_conv_b, v_w_down, v_final_norm):
    a = dict(zip(INPUTS, (x, attn_norm, w_in, a_q_norm, a_k_norm, c_q_norm, c_kv_norm, w_uq, w_ukv, out_norm, w_out, ffn_norm, w_up, conv_w, conv_b, w_down, final_norm, loss_target, m_attn_norm, m_w_in, m_a_q_norm, m_a_k_norm, m_c_q_norm, m_c_kv_norm, m_w_uq, m_w_ukv, m_out_norm, m_w_out, m_ffn_norm, m_w_up, m_conv_w, m_conv_b, m_w_down, m_final_norm, v_attn_norm, v_w_in, v_a_q_norm, v_a_k_norm, v_c_q_norm, v_c_kv_norm, v_w_uq, v_w_ukv, v_out_norm, v_w_out, v_ffn_norm, v_w_up, v_conv_w, v_conv_b, v_w_down, v_final_norm)))
    nl = w_in.shape[0]
    seq = x.shape[1]
    core = lax.axis_index("c").astype(jnp.int32).reshape(1)
    chip = 2 * lax.axis_index("x") + lax.axis_index("y")
    tabs = _rope_tables(seq)

    quarter = conv_w.shape[2]
    taps = jnp.pad(conv_w, ((0, 0), (0, 8 - conv_w.shape[1]), (0, 0))).reshape(nl * 8, quarter)
    layers = []
    taps_full = None
    for l in range(nl):
        shards = [cast_layer(f"cast_{n}", a[n], l) for n in BIG]
        if l == 0:
            got = gather_shards(f"gather{l}", shards + [taps], [True] * len(BIG) + [False])
            taps_full = got[-1].reshape(4, nl, 8, quarter)
        else:
            got = gather_shards(f"gather{l}", shards, [True] * len(BIG))
        full = dict(zip(BIG, got))
        conv_full = jnp.transpose(taps_full[:, l, 0:3, :], (1, 0, 2)).reshape(3, 4 * quarter)
        layers.append(_compute_layouts(full, conv_full, a, l))

    loss, grad_x, grads, d_final = _local_step(x[0], loss_target[0], layers, final_norm, tabs)
    loss = lax.psum(loss, ("x", "y", "c"))

    reduced = []
    conv_grads = []
    for l in range(nl):
        by_shard, d_conv_w, d_conv_b = _shard_layouts(grads[l])
        conv_grads.append((d_conv_w, d_conv_b))
        g_list = [by_shard[n] for n in BIG]
        theirs = sibling_halves(f"reduce{l}_sibling", g_list)
        sums = [add_half(f"reduce{l}_add_{n}", core, g, t) for n, g, t in zip(BIG, g_list, theirs)]
        parts = exchange_shards(f"reduce{l}_chips", sums)
        halves = [sum_chips(f"reduce{l}_sum_{n}", p) for n, p in zip(BIG, parts)]
        reduced.append(dict(zip(BIG, join_halves(f"reduce{l}_join", halves))))

    small_g = {n: jnp.stack([grads[l][n] for l in range(nl)]) for n in SMALL if n not in ("conv_b", "final_norm")}
    small_g["conv_b"] = jnp.stack([cg[1] for cg in conv_grads])
    small_g["final_norm"] = d_final
    conv_w_g = jnp.stack([cg[0] for cg in conv_grads])
    shapes = [a[n].shape for n in SMALL] + [conv_w_g.shape]
    summed = _unpack(allreduce_small("reduce_small", _pack([small_g[n] for n in SMALL] + [conv_w_g])), shapes)
    small_g = dict(zip(SMALL, summed[:-1]))
    small_g["conv_w"] = lax.dynamic_slice_in_dim(summed[-1], chip * quarter, quarter, axis=2)

    out = {}
    for n in BIG:
        res = None
        for l in range(nl):
            res = adamw_layer(f"adamw_{n}", l, reduced[l][n], a[n], a["m_" + n], a["v_" + n], prev=res)
        out[n] = res
    names = SMALL + ("conv_w",)
    shapes = [a[n].shape for n in names]
    d_small, m_small, v_small = adamw_packed(
        "adamw_small", _pack([small_g[n] for n in names]), _pack([a[n] for n in names]),
        _pack([a["m_" + n] for n in names]), _pack([a["v_" + n] for n in names]))
    for n, d_, m_, v_ in zip(names, _unpack(d_small, shapes), _unpack(m_small, shapes), _unpack(v_small, shapes)):
        out[n] = (small_g[n], d_, m_, v_)

    return (loss, grad_x[None]) + tuple(out[n][k] for k in range(4) for n in WEIGHTS)
```

```python
import functools

import jax
import jax.numpy as jnp
import numpy as np
from jax import lax
from jax.experimental import pallas as pl
from jax.experimental.pallas import tpu as pltpu

F32 = jnp.float32
BF16 = jnp.bfloat16
MESH = pl.DeviceIdType.MESH

V7X_VMEM_BYTES = 64 * 1024 * 1024
VMEM_LIMIT = V7X_VMEM_BYTES - 6 * 1024 * 1024
LANES = 128
ELEMENTWISE_BLOCK_BYTES = 2 * 1024 * 1024

HEAD = 128
A_HEADS, A_KV = 6, 2
B_HEADS, B_GROUPS = 4, 3
B_DILATIONS = (1, 4, 16)
B_HALF = 64
C_HEADS, C_RANK, C_ROPE = 6, 512, 64
GRID_W = 64
A_THETA, B_THETA, C_THETA = 10000.0, 500000.0, 10000.0
B_ROPE_DIM = 32
EPS = 1e-6
A_W, B_W, C_W = A_HEADS * HEAD, B_HEADS * HEAD, C_HEADS * HEAD
O_AQ, O_AK, O_AV = 0, 768, 1024
O_BQ, O_BK, O_BV = 1280, 2816, 3328
O_CQ, O_CKV, O_CKR = 3840, 4352, 4864
IN_W = 4928
PROJ_W = 5120

ADAM_LR, ADAM_B1, ADAM_B2, ADAM_EPS, ADAM_WD, ADAM_STEP = 0.001, 0.9, 0.999, 1e-08, 0.01, 10

NN = (((1,), (0,)), ((), ()))
NT = (((1,), (1,)), ((), ()))
TN = (((0,), (0,)), ((), ()))


def _dot(a, b, dims=NN):
    return lax.dot_general(a, b, dims, preferred_element_type=F32)


def _params(*sem):
    return pltpu.CompilerParams(dimension_semantics=sem if sem else None, vmem_limit_bytes=VMEM_LIMIT)


def _tile(n, target, unit=LANES):
    if n <= target:
        return n
    best = 0
    for t in range(unit, target + 1, unit):
        if n % t == 0:
            best = t
    assert best, (n, target)
    return best


def _rows(r, width, itemsize=4):
    return _tile(r, max(8, ELEMENTWISE_BLOCK_BYTES // (width * itemsize)), 8)


def _matmul_call(name, dims, grid, a_spec, b_spec, o_spec, out_shape, acc_shape, add_spec=None):
    nk = grid[2]

    def body(*refs):
        if add_spec is None:
            a_ref, b_ref, o_ref, acc = refs
            add_ref = None
        else:
            a_ref, b_ref, add_ref, o_ref, acc = refs
        k = pl.program_id(2)

        @pl.when(k == 0)
        def _():
            acc[...] = jnp.zeros_like(acc)

        acc[...] += _dot(a_ref[...], b_ref[...], dims)

        @pl.when(k == nk - 1)
        def _():
            r = acc[...]
            if add_ref is not None:
                r = r + add_ref[...]
            o_ref[...] = r.astype(o_ref.dtype)

    in_specs = [a_spec, b_spec] + ([] if add_spec is None else [add_spec])
    return pl.pallas_call(
        body, name=name, grid=grid, in_specs=in_specs, out_specs=o_spec, out_shape=out_shape,
        scratch_shapes=[pltpu.VMEM(acc_shape, F32)],
        compiler_params=_params("parallel", "parallel", "arbitrary"))


def matmul(name, a, b, mode, out_dtype, add=None, tm=1024, tn=1024, tk=1024):
    if mode == "nn":
        (m, k), (k2, n) = a.shape, b.shape
    elif mode == "nt":
        (m, k), (n, k2) = a.shape, b.shape
    else:
        (k, m), (k2, n) = a.shape, b.shape
    assert k == k2, (name, a.shape, b.shape)
    tm, tn, tk = _tile(m, tm), _tile(n, tn), _tile(k, tk)
    grid = (m // tm, n // tn, k // tk)
    if mode == "tn":
        a_spec = pl.BlockSpec((tk, tm), lambda i, j, kk: (kk, i))
    else:
        a_spec = pl.BlockSpec((tm, tk), lambda i, j, kk: (i, kk))
    if mode == "nt":
        b_spec = pl.BlockSpec((tn, tk), lambda i, j, kk: (j, kk))
    else:
        b_spec = pl.BlockSpec((tk, tn), lambda i, j, kk: (kk, j))
    o_spec = pl.BlockSpec((tm, tn), lambda i, j, kk: (i, j))
    dims = {"nn": NN, "nt": NT, "tn": TN}[mode]
    call = _matmul_call(name, dims, grid, a_spec, b_spec, o_spec, jax.ShapeDtypeStruct((m, n), out_dtype),
                        (tm, tn), None if add is None else o_spec)
    return call(a, b) if add is None else call(a, b, add)


def rmsnorm_fwd(name, x, g, width, out_dtype):
    r, cols = x.shape
    nb = cols // width
    tr = _rows(r, width)

    def body(x_ref, g_ref, o_ref):
        xv = x_ref[...]
        rs = lax.rsqrt(jnp.mean(xv * xv, axis=-1, keepdims=True) + EPS)
        o_ref[...] = (xv * rs * g_ref[...]).astype(o_ref.dtype)

    blk = pl.BlockSpec((tr, width), lambda i, j: (i, j))
    return pl.pallas_call(
        body, name=name, grid=(r // tr, nb),
        in_specs=[blk, pl.BlockSpec((1, width), lambda i, j: (0, 0))], out_specs=blk,
        out_shape=jax.ShapeDtypeStruct((r, cols), out_dtype),
        compiler_params=_params("parallel", "parallel"))(x, g.reshape(1, width))


def rmsnorm_bwd(name, dy, x, g, width, out_dtypes=(F32,), add=None):
    r, cols = x.shape
    nb = cols // width
    tr = _rows(r, width)
    n_out = len(out_dtypes)

    def body(*refs):
        dy_ref, x_ref, g_ref = refs[:3]
        add_ref = refs[3] if add is not None else None
        outs = refs[-(n_out + 1):-1]
        dg_ref = refs[-1]
        xv = x_ref[...]
        dyv = dy_ref[...].astype(F32)
        rs = lax.rsqrt(jnp.mean(xv * xv, axis=-1, keepdims=True) + EPS)
        xh = xv * rs
        dyg = dyv * g_ref[...]
        dx = rs * (dyg - xh * jnp.mean(dyg * xh, axis=-1, keepdims=True))
        if add_ref is not None:
            dx = dx + add_ref[...]
        for o in outs:
            o[...] = dx.astype(o.dtype)

        @pl.when((pl.program_id(0) == 0) & (pl.program_id(1) == 0))
        def _():
            dg_ref[...] = jnp.zeros_like(dg_ref)

        dg_ref[...] += jnp.sum(dyv * xh, axis=0, keepdims=True)

    blk = pl.BlockSpec((tr, width), lambda i, j: (i, j))
    vec = pl.BlockSpec((1, width), lambda i, j: (0, 0))
    ins = [dy, x, g.reshape(1, width)] + ([add] if add is not None else [])
    res = pl.pallas_call(
        body, name=name, grid=(r // tr, nb),
        in_specs=[blk, blk, vec] + ([blk] if add is not None else []),
        out_specs=[blk] * n_out + [vec],
        out_shape=[jax.ShapeDtypeStruct((r, cols), dt) for dt in out_dtypes] + [jax.ShapeDtypeStruct((1, width), F32)],
        compiler_params=_params("arbitrary", "arbitrary"))(*ins)
    return tuple(res[:n_out]) + (res[n_out].reshape(width),)


def _rope_angles(pos, dim, theta):
    inv = theta ** (-jnp.arange(0, dim, 2, dtype=F32) / dim)
    ang = pos.astype(F32)[:, None] * inv[None, :]
    return jnp.cos(ang), jnp.sin(ang)


def _rope_tables(s):
    rows = s // GRID_W
    row_pos = jnp.repeat(jnp.arange(rows), GRID_W)
    col_pos = jnp.tile(jnp.arange(GRID_W), rows)
    t_pos = jnp.arange(s)
    z = lambda n: jnp.zeros((s, n), F32)
    o = lambda n: jnp.ones((s, n), F32)
    cr, sr = _rope_angles(row_pos, HEAD // 2, A_THETA)
    cc, sc = _rope_angles(col_pos, HEAD // 2, A_THETA)
    tab_a = (jnp.concatenate([cr, cr, cc, cc], 1), jnp.concatenate([-sr, z(32), -sc, z(32)], 1),
             jnp.concatenate([z(32), sr, z(32), sc], 1), 32)
    cp, sp = _rope_angles(t_pos, B_ROPE_DIM, B_THETA)
    tab_b = (jnp.concatenate([cp, cp, o(96)], 1), jnp.concatenate([-sp, z(112)], 1),
             jnp.concatenate([z(16), sp, z(96)], 1), 16)
    cm, sm = _rope_angles(t_pos, C_ROPE, C_THETA)
    tab_c = (jnp.concatenate([cm, cm, o(64)], 1), jnp.concatenate([-sm, z(96)], 1),
             jnp.concatenate([z(32), sm, z(64)], 1), 32)

    def transposed(tab):
        c, s1, s2, h = tab
        return (c, jnp.roll(s2, -h, axis=1), jnp.roll(s1, h, axis=1), h)

    return {k: (t, transposed(t)) for k, t in (("a", tab_a), ("b", tab_b), ("c", tab_c))}


def rope(name, x, tab, out_dtype):
    c, s1, s2, h = tab
    s, cols = x.shape
    nh = cols // HEAD
    ts = _rows(s, HEAD)

    def body(x_ref, c_ref, s1_ref, s2_ref, o_ref):
        xv = x_ref[...].astype(F32)
        out = xv * c_ref[...] + pltpu.roll(xv, HEAD - h, 1) * s1_ref[...] + pltpu.roll(xv, h, 1) * s2_ref[...]
        o_ref[...] = out.astype(o_ref.dtype)

    blk = pl.BlockSpec((ts, HEAD), lambda i, j: (i, j))
    tb = pl.BlockSpec((ts, HEAD), lambda i, j: (i, 0))
    return pl.pallas_call(
        body, name=name, grid=(s // ts, nh), in_specs=[blk, tb, tb, tb], out_specs=blk,
        out_shape=jax.ShapeDtypeStruct((s, cols), out_dtype),
        compiler_params=_params("parallel", "parallel"))(x, c, s1, s2)


ATTN_TQ = 256


def attn_fwd(name, parts, v, v_group, nheads, scale):
    s = v.shape[0]
    tq = ATTN_TQ
    npart = len(parts)

    def body(*refs):
        v_ref, o_ref, lse_ref = refs[2 * npart:]
        sc = None
        for p in range(npart):
            t = _dot(refs[2 * p][...], refs[2 * p + 1][...], NT)
            sc = t if sc is None else sc + t
        sc = sc * scale
        m = jnp.max(sc, axis=-1, keepdims=True)
        e = jnp.exp(sc - m)
        l = jnp.sum(e, axis=-1, keepdims=True)
        o_ref[...] = _dot((e / l).astype(BF16), v_ref[...])
        lse_ref[...] = jnp.broadcast_to(m + jnp.log(l), (tq, HEAD))

    in_specs, ins = [], []
    for q, qoff, k, kg in parts:
        in_specs.append(pl.BlockSpec((tq, HEAD), lambda h, i, qoff=qoff: (i, qoff + h)))
        in_specs.append(pl.BlockSpec((s, HEAD), lambda h, i, kg=kg: (0, h // kg)))
        ins += [q, k]
    in_specs.append(pl.BlockSpec((s, HEAD), lambda h, i: (0, h // v_group)))
    out_blk = pl.BlockSpec((tq, HEAD), lambda h, i: (i, h))
    return pl.pallas_call(
        body, name=name, grid=(nheads, s // tq), in_specs=in_specs, out_specs=[out_blk, out_blk],
        out_shape=[jax.ShapeDtypeStruct((s, nheads * HEAD), F32)] * 2,
        compiler_params=_params("parallel", "parallel"))(*ins, v)


def attn_bwd(name, parts, v, v_group, lse, do, nheads, scale):
    s = v.shape[0]
    tq = ATTN_TQ
    npart = len(parts)

    def body(*refs):
        v_ref, lse_ref, do_ref = refs[2 * npart:2 * npart + 3]
        outs = refs[2 * npart + 3:]
        dq_refs, dk_refs, dv_ref = outs[:npart], outs[npart:2 * npart], outs[2 * npart]
        h, i = pl.program_id(0), pl.program_id(1)
        sc = None
        for p in range(npart):
            t = _dot(refs[2 * p][...], refs[2 * p + 1][...], NT)
            sc = t if sc is None else sc + t
        pr = jnp.exp(sc * scale - lse_ref[:, 0:1])
        dov = do_ref[...]
        dp = _dot(dov, v_ref[...], NT)
        delta = jnp.sum(pr * dp, axis=-1, keepdims=True)
        ds = (pr * (dp - delta) * scale).astype(BF16)

        @pl.when((i == 0) & (h % v_group == 0))
        def _():
            dv_ref[...] = jnp.zeros_like(dv_ref)

        dv_ref[...] += _dot(pr.astype(BF16), dov, TN)
        for p in range(npart):
            kg = parts[p][3]
            dq_refs[p][...] = _dot(ds, refs[2 * p + 1][...])

            @pl.when((i == 0) & (h % kg == 0))
            def _(p=p):
                dk_refs[p][...] = jnp.zeros_like(dk_refs[p])

            dk_refs[p][...] += _dot(ds, refs[2 * p][...], TN)

    in_specs, ins = [], []
    for q, qoff, k, kg in parts:
        in_specs.append(pl.BlockSpec((tq, HEAD), lambda h, i, qoff=qoff: (i, qoff + h)))
        in_specs.append(pl.BlockSpec((s, HEAD), lambda h, i, kg=kg: (0, h // kg)))
        ins += [q, k]
    hq_blk = pl.BlockSpec((tq, HEAD), lambda h, i: (i, h))
    in_specs += [pl.BlockSpec((s, HEAD), lambda h, i: (0, h // v_group)), hq_blk, hq_blk]
    out_specs = [hq_blk] * npart
    out_shape = [jax.ShapeDtypeStruct((s, nheads * HEAD), F32)] * npart
    for q, qoff, k, kg in parts:
        out_specs.append(pl.BlockSpec((s, HEAD), lambda h, i, kg=kg: (0, h // kg)))
        out_shape.append(jax.ShapeDtypeStruct((s, nheads // kg * HEAD), F32))
    out_specs.append(pl.BlockSpec((s, HEAD), lambda h, i: (0, h // v_group)))
    out_shape.append(jax.ShapeDtypeStruct((s, nheads // v_group * HEAD), F32))
    res = pl.pallas_call(
        body, name=name, grid=(nheads, s // tq), in_specs=in_specs, out_specs=out_specs, out_shape=out_shape,
        compiler_params=_params("arbitrary", "arbitrary"))(*ins, v, lse, do)
    return list(res[:npart]), list(res[npart:2 * npart]), res[2 * npart]


def _band_windows(lf):
    for ib in range(lf // HEAD):
        q0 = ib * HEAD
        yield q0, max(0, q0 - B_HALF), min(lf, q0 + HEAD + B_HALF)


def _band_mask(q0, lo, hi):
    qpos = q0 + lax.broadcasted_iota(jnp.int32, (HEAD, hi - lo), 0)
    kpos = lo + lax.broadcasted_iota(jnp.int32, (HEAD, hi - lo), 1)
    return jnp.abs(qpos - kpos) <= B_HALF


def _band_specs(d, q_cols, q_off):
    s_q = pl.BlockSpec
    return s_q, (lambda r, h: (0, r * (q_cols // HEAD) + q_off + h)), (lambda r, h: (0, r * B_HEADS + h))


def band_fwd(name, q, q_off, k, v, d):
    s = k.shape[0]
    lf = s // d
    scale = HEAD ** -0.5

    def body(q_ref, k_ref, v_ref, o_ref, lse_ref):
        for q0, lo, hi in _band_windows(lf):
            sc = _dot(q_ref[q0:q0 + HEAD, :], k_ref[lo:hi, :], NT) * scale
            sc = jnp.where(_band_mask(q0, lo, hi), sc, -1e30)
            m = jnp.max(sc, axis=-1, keepdims=True)
            e = jnp.exp(sc - m)
            l = jnp.sum(e, axis=-1, keepdims=True)
            o_ref[q0:q0 + HEAD, :] = _dot((e / l).astype(BF16), v_ref[lo:hi, :])
            lse_ref[q0:q0 + HEAD, :] = jnp.broadcast_to(m + jnp.log(l), (HEAD, HEAD))

    _, qmap, kmap = _band_specs(d, q.shape[1], q_off)
    blk = lambda f: pl.BlockSpec((lf, HEAD), f)
    o, lse = pl.pallas_call(
        body, name=name, grid=(d, B_HEADS), in_specs=[blk(qmap), blk(kmap), blk(kmap)],
        out_specs=[blk(kmap), blk(kmap)], out_shape=[jax.ShapeDtypeStruct((lf, d * B_W), F32)] * 2,
        compiler_params=_params("parallel", "parallel"))(
            q.reshape(lf, d * q.shape[1]), k.reshape(lf, d * B_W), v.reshape(lf, d * B_W))
    return o.reshape(s, B_W), lse.reshape(s, B_W)


def band_bwd(name, q, q_off, k, v, dy, lse, delta, d):
    s = k.shape[0]
    lf = s // d
    scale = HEAD ** -0.5

    def body(q_ref, k_ref, v_ref, dy_ref, lse_ref, dl_ref, dq_ref, dk_ref, dv_ref):
        dk_ref[...] = jnp.zeros_like(dk_ref)
        dv_ref[...] = jnp.zeros_like(dv_ref)
        for q0, lo, hi in _band_windows(lf):
            qv, kv, vv = q_ref[q0:q0 + HEAD, :], k_ref[lo:hi, :], v_ref[lo:hi, :]
            dyv = dy_ref[q0:q0 + HEAD, :]
            sc = _dot(qv, kv, NT) * scale
            pr = jnp.where(_band_mask(q0, lo, hi), jnp.exp(sc - lse_ref[q0:q0 + HEAD, 0:1]), 0.0)
            dp = _dot(dyv, vv, NT)
            ds = (pr * (dp - dl_ref[q0:q0 + HEAD, 0:1]) * scale).astype(BF16)
            dv_ref[lo:hi, :] += _dot(pr.astype(BF16), dyv, TN)
            dq_ref[q0:q0 + HEAD, :] = _dot(ds, kv)
            dk_ref[lo:hi, :] += _dot(ds, qv, TN)

    _, qmap, kmap = _band_specs(d, q.shape[1], q_off)
    blk = lambda f: pl.BlockSpec((lf, HEAD), f)
    fold = lambda a: a.reshape(lf, d * a.shape[1])
    res = pl.pallas_call(
        body, name=name, grid=(d, B_HEADS), in_specs=[blk(qmap)] + [blk(kmap)] * 5,
        out_specs=[blk(kmap)] * 3, out_shape=[jax.ShapeDtypeStruct((lf, d * B_W), F32)] * 3,
        compiler_params=_params("parallel", "parallel"))(fold(q), fold(k), fold(v), fold(dy), fold(lse), fold(delta))
    return tuple(r.reshape(s, B_W) for r in res)


def mix_fwd(name, outs, lses):
    s, w = outs[0].shape
    ts = _rows(s, w)

    def body(o0, o1, o2, l0, l1, l2, y_ref, lse_ref):
        a, b, c = l0[...], l1[...], l2[...]
        m = jnp.maximum(jnp.maximum(a, b), c)
        ea, eb, ec = jnp.exp(a - m), jnp.exp(b - m), jnp.exp(c - m)
        den = ea + eb + ec
        y_ref[...] = (ea / den) * o0[...] + (eb / den) * o1[...] + (ec / den) * o2[...]
        lse_ref[...] = m + jnp.log(den)

    blk = pl.BlockSpec((ts, w), lambda i: (i, 0))
    return pl.pallas_call(
        body, name=name, grid=(s // ts,), in_specs=[blk] * 6, out_specs=[blk, blk],
        out_shape=[jax.ShapeDtypeStruct((s, w), F32)] * 2, compiler_params=_params("parallel"))(*outs, *lses)


def head_rowdot(name, dy, y):
    s, cols = y.shape
    ts = _rows(s, HEAD)

    def body(dy_ref, y_ref, dl_ref, dyb_ref):
        dyv = dy_ref[...]
        dl_ref[...] = jnp.broadcast_to(jnp.sum(dyv * y_ref[...], axis=-1, keepdims=True), (ts, HEAD))
        dyb_ref[...] = dyv.astype(BF16)

    blk = pl.BlockSpec((ts, HEAD), lambda i, j: (i, j))
    return pl.pallas_call(
        body, name=name, grid=(s // ts, cols // HEAD), in_specs=[blk, blk], out_specs=[blk, blk],
        out_shape=[jax.ShapeDtypeStruct((s, cols), F32), jax.ShapeDtypeStruct((s, cols), BF16)],
        compiler_params=_params("parallel", "parallel"))(dy, y)


def add3(name, a, b, c, out_dtype=F32):
    s, w = a.shape
    ts = _rows(s, w)

    def body(a_ref, b_ref, c_ref, o_ref):
        o_ref[...] = (a_ref[...] + b_ref[...] + c_ref[...]).astype(o_ref.dtype)

    blk = pl.BlockSpec((ts, w), lambda i: (i, 0))
    return pl.pallas_call(
        body, name=name, grid=(s // ts,), in_specs=[blk] * 3, out_specs=blk,
        out_shape=jax.ShapeDtypeStruct((s, w), out_dtype), compiler_params=_params("parallel"))(a, b, c)


FFN_TN = 256


def _conv(h, cw):
    s = h.shape[0]
    row = lax.broadcasted_iota(jnp.int32, h.shape, 0)
    prev = jnp.where(row == 0, 0.0, pltpu.roll(h, 1, 0))
    nxt = jnp.where(row == s - 1, 0.0, pltpu.roll(h, s - 1, 0))
    return prev * cw[0:1, :] + h * cw[1:2, :] + nxt * cw[2:3, :] + cw[3:4, :], prev, nxt


def _sigmoid(x):
    return 1.0 / (1.0 + jnp.exp(-x))


def ffn_up(name, xn, w_up, cwb):
    s, dm = xn.shape
    quarter = w_up.shape[3]
    fh = 2 * quarter
    tn = FFN_TN
    per = quarter // tn

    def body(x_ref, w_ref, cw_ref, h_ref, act_ref):
        xv = x_ref[...]
        hg = _dot(xv, w_ref[0])
        hu = _dot(xv, w_ref[1])
        h_ref[0] = hg
        h_ref[1] = hu
        gc, _, _ = _conv(hg, cw_ref[0])
        uc, _, _ = _conv(hu, cw_ref[1])
        act_ref[...] = (gc * _sigmoid(gc) * uc).astype(BF16)

    return pl.pallas_call(
        body, name=name, grid=(fh // tn,),
        in_specs=[pl.BlockSpec((s, dm), lambda t: (0, 0)),
                  pl.BlockSpec((2, None, dm, tn), lambda t: (0, t // per, 0, t % per)),
                  pl.BlockSpec((2, 8, tn), lambda t: (0, 0, t))],
        out_specs=[pl.BlockSpec((2, s, tn), lambda t: (0, 0, t)), pl.BlockSpec((s, tn), lambda t: (0, t))],
        out_shape=[jax.ShapeDtypeStruct((2, s, fh), F32), jax.ShapeDtypeStruct((s, fh), BF16)],
        compiler_params=_params("parallel"))(xn, w_up, cwb)


def ffn_gate_bwd(name, h, dact, cwb):
    _, s, fh = h.shape
    tn = FFN_TN

    def body(h_ref, da_ref, cw_ref, dh_ref, dcw_ref):
        gc, gp, gn = _conv(h_ref[0], cw_ref[0])
        uc, up, un = _conv(h_ref[1], cw_ref[1])
        sg = _sigmoid(gc)
        da = da_ref[...]
        dgc = da * uc * (sg * (1.0 + gc * (1.0 - sg)))
        duc = da * (gc * sg)
        row = lax.broadcasted_iota(jnp.int32, da.shape, 0)
        for idx, (hv, prev, nxt, dc) in enumerate(((h_ref[0], gp, gn, dgc), (h_ref[1], up, un, duc))):
            cw = cw_ref[idx]
            from_next = jnp.where(row == s - 1, 0.0, pltpu.roll(dc, s - 1, 0))
            from_prev = jnp.where(row == 0, 0.0, pltpu.roll(dc, 1, 0))
            dh_ref[idx] = (from_next * cw[0:1, :] + dc * cw[1:2, :] + from_prev * cw[2:3, :]).astype(BF16)
            dcw_ref[idx, 0:1, :] = jnp.sum(prev * dc, axis=0, keepdims=True)
            dcw_ref[idx, 1:2, :] = jnp.sum(hv * dc, axis=0, keepdims=True)
            dcw_ref[idx, 2:3, :] = jnp.sum(nxt * dc, axis=0, keepdims=True)
            dcw_ref[idx, 3:4, :] = jnp.sum(dc, axis=0, keepdims=True)
            dcw_ref[idx, 4:8, :] = jnp.zeros((4, tn), F32)

    return pl.pallas_call(
        body, name=name, grid=(fh // tn,),
        in_specs=[pl.BlockSpec((2, s, tn), lambda t: (0, 0, t)), pl.BlockSpec((s, tn), lambda t: (0, t)),
                  pl.BlockSpec((2, 8, tn), lambda t: (0, 0, t))],
        out_specs=[pl.BlockSpec((2, s, tn), lambda t: (0, 0, t)), pl.BlockSpec((2, 8, tn), lambda t: (0, 0, t))],
        out_shape=[jax.ShapeDtypeStruct((2, s, fh), BF16), jax.ShapeDtypeStruct((2, 8, fh), F32)],
        compiler_params=_params("parallel"))(h, dact, cwb)


def ffn_dx(name, dh, w_up):
    _, s, fh = dh.shape
    dm, quarter = w_up.shape[2], w_up.shape[3]
    tk = _tile(quarter, 1536)
    per = quarter // tk
    tm, tn = _tile(s, 1024), _tile(dm, 1024)
    grid = (s // tm, dm // tn, 4 * per)
    a_spec = pl.BlockSpec((None, tm, tk), lambda i, j, k: (k // (2 * per), i, k % (2 * per)))
    b_spec = pl.BlockSpec((None, None, tn, tk), lambda i, j, k: (k // (2 * per), (k // per) % 2, j, k % per))
    o_spec = pl.BlockSpec((tm, tn), lambda i, j, k: (i, j))
    return _matmul_call(name, NT, grid, a_spec, b_spec, o_spec, jax.ShapeDtypeStruct((s, dm), F32), (tm, tn))(dh, w_up)


def ffn_dw_up(name, xn, dh):
    _, s, fh = dh.shape
    dm = xn.shape[1]
    quarter = fh // 2
    tn = _tile(quarter, 1536)
    per = quarter // tn
    tm, tk = _tile(dm, 1024), _tile(s, 1024)
    grid = (dm // tm, 4 * per, s // tk)
    a_spec = pl.BlockSpec((tk, tm), lambda i, j, k: (k, i))
    b_spec = pl.BlockSpec((None, tk, tn), lambda i, j, k: (j // (2 * per), k, j % (2 * per)))
    o_spec = pl.BlockSpec((None, None, tm, tn), lambda i, j, k: (j // (2 * per), (j // per) % 2, i, j % per))
    return _matmul_call(name, TN, grid, a_spec, b_spec, o_spec,
                        jax.ShapeDtypeStruct((2, 2, dm, quarter), F32), (tm, tn))(xn, dh)


def loss_head(name, y, target):
    s, dm = y.shape
    ts = _rows(s, dm)

    def body(y_ref, t_ref, loss_ref, dy_ref, acc):
        i = pl.program_id(0)

        @pl.when(i == 0)
        def _():
            acc[...] = jnp.zeros_like(acc)

        err = y_ref[...] - t_ref[...]
        dy_ref[...] = err * (1.0 / dm)
        acc[...] += jnp.sum(err * err, axis=0, keepdims=True)

        @pl.when(i == s // ts - 1)
        def _():
            loss_ref[...] = jnp.broadcast_to(jnp.sum(acc[...], axis=-1, keepdims=True) * (0.5 / dm), (1, LANES))

    blk = pl.BlockSpec((ts, dm), lambda i: (i, 0))
    loss, dy = pl.pallas_call(
        body, name=name, grid=(s // ts,), in_specs=[blk, blk],
        out_specs=[pl.BlockSpec((1, LANES), lambda i: (0, 0)), blk],
        out_shape=[jax.ShapeDtypeStruct((1, LANES), F32), jax.ShapeDtypeStruct((s, dm), F32)],
        scratch_shapes=[pltpu.VMEM((1, dm), F32)], compiler_params=_params("arbitrary"))(y, target)
    return loss[0, 0], dy


def _c_parts(qc, kvc, krc):
    return [(qc, 0, kvc, None), (qc, C_HEADS, krc, C_HEADS)]


def attn_fwd_c(name, qc, kvc, krc, scale):
    s = qc.shape[0]
    tq = ATTN_TQ

    def body(qn_ref, qr_ref, kn_ref, kr_ref, v_ref, o_ref, lse_ref):
        sc = (_dot(qn_ref[...], kn_ref[...], NT) + _dot(qr_ref[...], kr_ref[...], NT)) * scale
        m = jnp.max(sc, axis=-1, keepdims=True)
        e = jnp.exp(sc - m)
        l = jnp.sum(e, axis=-1, keepdims=True)
        o_ref[...] = _dot((e / l).astype(BF16), v_ref[...])
        lse_ref[...] = jnp.broadcast_to(m + jnp.log(l), (tq, HEAD))

    qb = lambda off: pl.BlockSpec((tq, HEAD), lambda h, i: (i, off + h))
    kb = lambda f: pl.BlockSpec((s, HEAD), f)
    out_blk = pl.BlockSpec((tq, HEAD), lambda h, i: (i, h))
    return pl.pallas_call(
        body, name=name, grid=(C_HEADS, s // tq),
        in_specs=[qb(0), qb(C_HEADS), kb(lambda h, i: (0, 2 * h)), kb(lambda h, i: (0, 0)), kb(lambda h, i: (0, 2 * h + 1))],
        out_specs=[out_blk, out_blk], out_shape=[jax.ShapeDtypeStruct((s, C_W), F32)] * 2,
        compiler_params=_params("parallel", "parallel"))(qc, qc, kvc, krc, kvc)


def attn_bwd_c(name, qc, kvc, krc, lse, do, scale):
    s = qc.shape[0]
    tq = ATTN_TQ

    def body(qn_ref, qr_ref, kn_ref, kr_ref, v_ref, lse_ref, do_ref, dqn_ref, dqr_ref, dkn_ref, dv_ref, dkr_ref):
        h, i = pl.program_id(0), pl.program_id(1)
        sc = (_dot(qn_ref[...], kn_ref[...], NT) + _dot(qr_ref[...], kr_ref[...], NT)) * scale
        pr = jnp.exp(sc - lse_ref[:, 0:1])
        dov = do_ref[...]
        dp = _dot(dov, v_ref[...], NT)
        delta = jnp.sum(pr * dp, axis=-1, keepdims=True)
        ds = (pr * (dp - delta) * scale).astype(BF16)

        @pl.when(i == 0)
        def _():
            dv_ref[...] = jnp.zeros_like(dv_ref)
            dkn_ref[...] = jnp.zeros_like(dkn_ref)

        @pl.when((i == 0) & (h == 0))
        def _():
            dkr_ref[...] = jnp.zeros_like(dkr_ref)

        dv_ref[...] += _dot(pr.astype(BF16), dov, TN)
        dqn_ref[...] = _dot(ds, kn_ref[...])
        dqr_ref[...] = _dot(ds, kr_ref[...])
        dkn_ref[...] += _dot(ds, qn_ref[...], TN)
        dkr_ref[...] += _dot(ds, qr_ref[...], TN)

    qb = lambda off: pl.BlockSpec((tq, HEAD), lambda h, i: (i, off + h))
    kb = lambda f: pl.BlockSpec((s, HEAD), f)
    hq = pl.BlockSpec((tq, HEAD), lambda h, i: (i, h))
    kn_map, v_map, kr_map = (lambda h, i: (0, 2 * h)), (lambda h, i: (0, 2 * h + 1)), (lambda h, i: (0, 0))
    dqc_shape = jax.ShapeDtypeStruct((s, 2 * C_W), F32)
    dqn, dqr, dkn, dv, dkr = pl.pallas_call(
        body, name=name, grid=(C_HEADS, s // tq),
        in_specs=[qb(0), qb(C_HEADS), kb(kn_map), kb(kr_map), kb(v_map), hq, hq],
        out_specs=[hq, hq, kb(lambda h, i: (0, h)), kb(lambda h, i: (0, h)), kb(kr_map)],
        out_shape=[jax.ShapeDtypeStruct((s, C_W), F32)] * 4 + [jax.ShapeDtypeStruct((s, HEAD), F32)],
        compiler_params=_params("arbitrary", "arbitrary"))(qc, qc, kvc, krc, kvc, lse, do)
    del dqc_shape
    return dqn, dqr, dkn, dv, dkr


def _layer_fwd(x, p, tabs):
    sv = {"x": x}
    hn = rmsnorm_fwd("attn_norm", x, p["attn_norm"], x.shape[1], BF16)
    proj = matmul("in_proj", hn, p["w_in"], "nn", F32)
    sv["hn"] = hn
    aq, ak = proj[:, O_AQ:O_AK], proj[:, O_AK:O_AV]
    av = proj[:, O_AV:O_BQ].astype(BF16)
    bq, bk = proj[:, O_BQ:O_BK], proj[:, O_BK:O_BV]
    bv = proj[:, O_BV:O_CQ].astype(BF16)
    cq, ckv, ckr = proj[:, O_CQ:O_CKV], proj[:, O_CKV:O_CKR], proj[:, O_CKR:O_CKR + HEAD]
    sv.update(aq=aq, ak=ak, av=av, bv=bv, cq=cq, ckv=ckv)

    qa = rope("a_q_rope", rmsnorm_fwd("a_q_norm", aq, p["a_q_norm"], HEAD, F32), tabs["a"][0], BF16)
    ka = rope("a_k_rope", rmsnorm_fwd("a_k_norm", ak, p["a_k_norm"], HEAD, F32), tabs["a"][0], BF16)
    ya, lse_a = attn_fwd("a_attn", [(qa, 0, ka, A_HEADS // A_KV)], av, A_HEADS // A_KV, A_HEADS, HEAD ** -0.5)
    sv.update(qa=qa, ka=ka, ya=ya, lse_a=lse_a)

    qb = rope("b_q_rope", bq, tabs["b"][0], BF16)
    kb = rope("b_k_rope", bk, tabs["b"][0], BF16)
    outs, lses = [], []
    for g, d in enumerate(B_DILATIONS):
        o, l = band_fwd(f"b_band{g}", qb, g * B_HEADS, kb, bv, d)
        outs.append(o)
        lses.append(l)
    yb, lse_b = mix_fwd("b_mix", outs, lses)
    sv.update(qb=qb, kb=kb, yb=yb, lse_b=lse_b)

    cqn = rmsnorm_fwd("c_q_norm", cq, p["c_q_norm"], C_RANK, BF16)
    qc_raw = matmul("c_uq", cqn, p["w_uq"], "nn", F32)
    qc = jnp.concatenate([qc_raw[:, :C_W].astype(BF16), rope("c_q_rope", qc_raw[:, C_W:], tabs["c"][0], BF16)], axis=1)
    ckvn = rmsnorm_fwd("c_kv_norm", ckv, p["c_kv_norm"], C_RANK, BF16)
    kvc = matmul("c_ukv", ckvn, p["w_ukv"], "nn", BF16)
    krc = rope("c_k_rope", ckr, tabs["c"][0], BF16)
    c_scale = (HEAD + C_ROPE) ** -0.5
    yc, lse_c = attn_fwd_c("c_attn", qc, kvc, krc, c_scale)
    sv.update(cqn=cqn, ckvn=ckvn, qc=qc, kvc=kvc, krc=krc, yc=yc, lse_c=lse_c)

    g_a, g_b, g_c = p["out_norm"][:A_W], p["out_norm"][A_W:A_W + B_W], p["out_norm"][A_W + B_W:]
    y = jnp.concatenate([rmsnorm_fwd("out_norm_a", ya, g_a, A_W, BF16), rmsnorm_fwd("out_norm_b", yb, g_b, B_W, BF16),
                         rmsnorm_fwd("out_norm_c", yc, g_c, C_W, BF16)], axis=1)
    x1 = matmul("out_proj", y, p["w_out"], "nn", F32, add=x)
    sv.update(y=y, x1=x1)

    xn = rmsnorm_fwd("ffn_norm", x1, p["ffn_norm"], x.shape[1], BF16)
    h, act = ffn_up("ffn_up", xn, p["w_up"], p["cwb"])
    x2 = matmul("ffn_down", act, p["w_down"], "nn", F32, add=x1)
    sv.update(xn=xn, h=h, act=act)
    return x2, sv


def _layer_bwd(dx2, sv, p, tabs):
    g = {}
    dm = dx2.shape[1]
    dx2b = dx2.astype(BF16)
    dact = matmul("ffn_down_dx", dx2b, p["w_down"], "nt", F32)
    g["w_down"] = matmul("ffn_down_dw", sv["act"], dx2b, "tn", F32)
    dh, g["cwb"] = ffn_gate_bwd("ffn_gate_bwd", sv["h"], dact, p["cwb"])
    dxn = ffn_dx("ffn_up_dx", dh, p["w_up"])
    g["w_up"] = ffn_dw_up("ffn_up_dw", sv["xn"], dh)
    dx1, dx1b, g["ffn_norm"] = rmsnorm_bwd("ffn_norm_bwd", dxn, sv["x1"], p["ffn_norm"], dm, (F32, BF16), add=dx2)

    dy = matmul("out_proj_dx", dx1b, p["w_out"], "nt", F32)
    g["w_out"] = matmul("out_proj_dw", sv["y"], dx1b, "tn", F32)
    g_a, g_b, g_c = p["out_norm"][:A_W], p["out_norm"][A_W:A_W + B_W], p["out_norm"][A_W + B_W:]
    dya, dga = rmsnorm_bwd("out_norm_a_bwd", dy[:, :A_W], sv["ya"], g_a, A_W, (BF16,))
    dyb, dgb = rmsnorm_bwd("out_norm_b_bwd", dy[:, A_W:A_W + B_W], sv["yb"], g_b, B_W, (F32,))
    dyc, dgc = rmsnorm_bwd("out_norm_c_bwd", dy[:, A_W + B_W:], sv["yc"], g_c, C_W, (BF16,))
    g["out_norm"] = jnp.concatenate([dga, dgb, dgc])

    c_scale = (HEAD + C_ROPE) ** -0.5
    dqn, dqr, dkn, dvc, dkr = attn_bwd_c("c_attn_bwd", sv["qc"], sv["kvc"], sv["krc"], sv["lse_c"], dyc, c_scale)
    s = dx2.shape[0]
    dqc = jnp.concatenate([dqn.astype(BF16), rope("c_q_rope_bwd", dqr, tabs["c"][1], BF16)], axis=1)
    dcqn = matmul("c_uq_dx", dqc, p["w_uq"], "nt", F32)
    g["w_uq"] = matmul("c_uq_dw", sv["cqn"], dqc, "tn", F32)
    dcq, g["c_q_norm"] = rmsnorm_bwd("c_q_norm_bwd", dcqn, sv["cq"], p["c_q_norm"], C_RANK, (BF16,))
    dkvc = jnp.stack([dkn.reshape(s, C_HEADS, HEAD), dvc.reshape(s, C_HEADS, HEAD)], axis=2).reshape(s, 2 * C_W).astype(BF16)
    dckvn = matmul("c_ukv_dx", dkvc, p["w_ukv"], "nt", F32)
    g["w_ukv"] = matmul("c_ukv_dw", sv["ckvn"], dkvc, "tn", F32)
    dckv, g["c_kv_norm"] = rmsnorm_bwd("c_kv_norm_bwd", dckvn, sv["ckv"], p["c_kv_norm"], C_RANK, (BF16,))
    dckr = rope("c_k_rope_bwd", dkr, tabs["c"][1], BF16)

    delta_b, dybb = head_rowdot("b_delta", dyb, sv["yb"])
    dqs, dks, dvs = [], [], []
    for gi, d in enumerate(B_DILATIONS):
        dq_g, dk_g, dv_g = band_bwd(f"b_band{gi}_bwd", sv["qb"], gi * B_HEADS, sv["kb"], sv["bv"], dybb,
                                    sv["lse_b"], delta_b, d)
        dqs.append(dq_g)
        dks.append(dk_g)
        dvs.append(dv_g)
    dbq = rope("b_q_rope_bwd", jnp.concatenate(dqs, axis=1), tabs["b"][1], BF16)
    dbk = rope("b_k_rope_bwd", add3("b_dk_sum", *dks), tabs["b"][1], BF16)
    dbv = add3("b_dv_sum", *dvs, out_dtype=BF16)

    kg = A_HEADS // A_KV
    (dqa,), (dka,), dva = attn_bwd("a_attn_bwd", [(sv["qa"], 0, sv["ka"], kg)], sv["av"], kg, sv["lse_a"], dya,
                                   A_HEADS, HEAD ** -0.5)
    daq, g["a_q_norm"] = rmsnorm_bwd("a_q_norm_bwd", rope("a_q_rope_bwd", dqa, tabs["a"][1], F32), sv["aq"],
                                     p["a_q_norm"], HEAD, (BF16,))
    dak, g["a_k_norm"] = rmsnorm_bwd("a_k_norm_bwd", rope("a_k_rope_bwd", dka, tabs["a"][1], F32), sv["ak"],
                                     p["a_k_norm"], HEAD, (BF16,))

    dproj = jnp.concatenate([daq, dak, dva.astype(BF16), dbq, dbk, dbv, dcq, dckv, dckr,
                             jnp.zeros((s, PROJ_W - O_CKR - HEAD), BF16)], axis=1)
    dhn = matmul("in_proj_dx", dproj, p["w_in"], "nt", F32)
    g["w_in"] = matmul("in_proj_dw", sv["hn"], dproj, "tn", F32)
    dx, g["attn_norm"] = rmsnorm_bwd("attn_norm_bwd", dhn, sv["x"], p["attn_norm"], dm, (F32,), add=dx1)
    return dx, g


def _local_step(x, target, layers, final_norm, tabs):
    saved = []
    for p in layers:
        x, sv = _layer_fwd(x, p, tabs)
        saved.append(sv)
    dm = x.shape[1]
    yf = rmsnorm_fwd("final_norm", x, final_norm, dm, F32)
    loss, dyf = loss_head("loss_head", yf, target)
    dx, d_final = rmsnorm_bwd("final_norm_bwd", dyf, x, final_norm, dm, (F32,))
    grads = [None] * len(layers)
    for l in reversed(range(len(layers))):
        dx, grads[l] = _layer_bwd(dx, saved[l], layers[l], tabs)
    return loss, dx, grads, d_final


HBM = pl.BlockSpec(memory_space=pl.ANY)


def _place():
    x, y, c = lax.axis_index("x"), lax.axis_index("y"), lax.axis_index("c")
    others = [(1 - x, y), (x, 1 - y), (1 - x, 1 - y)]
    return x, y, c, 2 * x + y, others


def _remote(src, dst, send_sem, recv_sem, device):
    return pltpu.make_async_remote_copy(src_ref=src, dst_ref=dst, send_sem=send_sem, recv_sem=recv_sem,
                                        device_id=device, device_id_type=MESH)


def gather_shards(name, slots, small=()):
    n, ns = len(slots), len(small)

    def body(*refs):
        ins, small_ins = refs[:n], refs[n:n + ns]
        outs, small_outs = refs[n + ns:2 * n + ns], refs[2 * n + ns:2 * (n + ns)]
        send_sems, recv_sems, fwd_send, fwd_recv, local_sems = refs[2 * (n + ns):]
        x, y, c, chip, others = _place()
        sibling = (x, y, 1 - c)
        pending = []

        def rows(t, core):
            half = slots[t].shape[1] // 2
            return pl.ds(core * half, half)

        for t in range(n):
            for k, (ox, oy) in enumerate(others):
                cp = _remote(ins[t].at[chip, rows(t, c)], outs[t].at[chip, rows(t, c)], send_sems.at[t, k],
                             recv_sems.at[t, k], (ox, oy, c))
                cp.start()
                pending.append(cp.wait_send)
        for t in range(ns):
            cp = pltpu.make_async_copy(small_ins[t], small_outs[t].at[chip], local_sems.at[t])
            cp.start()
            pending.append(cp.wait)
            for k, (ox, oy) in enumerate(others):
                cp = _remote(small_ins[t], small_outs[t].at[chip], send_sems.at[n + t, k], recv_sems.at[n + t, k],
                             (ox, oy, c))
                cp.start()
                pending.append(cp.wait_send)
        for t in range(n):
            for k, (ox, oy) in enumerate(others):
                landed = outs[t].at[2 * ox + oy, rows(t, c)]
                _remote(landed, landed, send_sems.at[t, k], recv_sems.at[t, k], (ox, oy, c)).wait_recv()
                cp = _remote(landed, landed, fwd_send.at[t, k], fwd_recv.at[t, k], sibling)
                cp.start()
                pending.append(cp.wait_send)
        for t in range(ns):
            for k, (ox, oy) in enumerate(others):
                landed = small_outs[t].at[2 * ox + oy]
                _remote(landed, landed, send_sems.at[n + t, k], recv_sems.at[n + t, k], (ox, oy, c)).wait_recv()
        for t in range(n):
            for k, (ox, oy) in enumerate(others):
                theirs = outs[t].at[2 * ox + oy, rows(t, 1 - c)]
                _remote(theirs, theirs, fwd_send.at[t, k], fwd_recv.at[t, k], sibling).wait_recv()
        for wait in pending:
            wait()

    return pl.pallas_call(
        body, name=name, in_specs=[HBM] * (n + ns), out_specs=[HBM] * (n + ns),
        out_shape=[jax.ShapeDtypeStruct(s.shape, s.dtype) for s in slots]
        + [jax.ShapeDtypeStruct((4,) + s.shape, s.dtype) for s in small],
        input_output_aliases={t: t for t in range(n)},
        scratch_shapes=[pltpu.SemaphoreType.DMA((n + ns, 3))] * 2 + [pltpu.SemaphoreType.DMA((n, 3))] * 2
        + [pltpu.SemaphoreType.DMA((max(ns, 1),))],
        compiler_params=pltpu.CompilerParams(has_side_effects=True))(*slots, *small)


def sibling_halves(name, grads):
    n = len(grads)

    def body(*refs):
        ins, outs = refs[:n], refs[n:2 * n]
        send_sems, recv_sems = refs[2 * n:]
        x, y, c, _, _ = _place()
        copies = []
        for t in range(n):
            half = grads[t].shape[1] // 2
            cp = _remote(ins[t].at[:, pl.ds((1 - c) * half, half), :], outs[t], send_sems.at[t], recv_sems.at[t],
                         (x, y, 1 - c))
            cp.start()
            copies.append(cp)
        for cp in copies:
            cp.wait()

    return pl.pallas_call(
        body, name=name, in_specs=[HBM] * n, out_specs=[HBM] * n,
        out_shape=[jax.ShapeDtypeStruct((4, g.shape[1] // 2, g.shape[2]), g.dtype) for g in grads],
        scratch_shapes=[pltpu.SemaphoreType.DMA((n,))] * 2,
        compiler_params=pltpu.CompilerParams(has_side_effects=True))(*grads)


def add_half(name, place, grad, theirs):
    _, r, cols = grad.shape
    half = r // 2
    tr = _rows(half, cols)

    def body(place_ref, g_ref, t_ref, o_ref):
        o_ref[...] = (g_ref[...] + t_ref[...]).astype(o_ref.dtype)

    return pl.pallas_call(
        body, name=name,
        grid_spec=pltpu.PrefetchScalarGridSpec(
            num_scalar_prefetch=1, grid=(4, half // tr),
            in_specs=[pl.BlockSpec((None, None, tr, cols), lambda j, i, pr: (j, pr[0], i, 0)),
                      pl.BlockSpec((None, tr, cols), lambda j, i, pr: (j, i, 0))],
            out_specs=pl.BlockSpec((None, tr, cols), lambda j, i, pr: (j, i, 0))),
        out_shape=jax.ShapeDtypeStruct((4, half, cols), BF16),
        compiler_params=_params("parallel", "parallel"))(place, grad.reshape(4, 2, half, cols), theirs)


def exchange_shards(name, sums):
    n = len(sums)

    def body(*refs):
        ins, outs = refs[:n], refs[n:2 * n]
        send_sems, recv_sems = refs[2 * n:]
        x, y, c, chip, others = _place()
        pending = []
        for t in range(n):
            for k, (ox, oy) in enumerate(others):
                cp = _remote(ins[t].at[2 * ox + oy], outs[t].at[chip], send_sems.at[t, k], recv_sems.at[t, k],
                             (ox, oy, c))
                cp.start()
                pending.append(cp.wait_send)
        for t in range(n):
            for k, (ox, oy) in enumerate(others):
                slot = outs[t].at[2 * ox + oy]
                _remote(slot, slot, send_sems.at[t, k], recv_sems.at[t, k], (ox, oy, c)).wait_recv()
        for wait in pending:
            wait()

    return pl.pallas_call(
        body, name=name, in_specs=[HBM] * n, out_specs=[HBM] * n,
        out_shape=[jax.ShapeDtypeStruct(s.shape, s.dtype) for s in sums],
        scratch_shapes=[pltpu.SemaphoreType.DMA((n, 3))] * 2,
        compiler_params=pltpu.CompilerParams(has_side_effects=True))(*sums)


def sum_chips(name, place, own, parts):
    _, half, cols = parts.shape
    tr = _rows(half, cols)

    def body(place_ref, own_ref, p1, p2, p3, o_ref):
        o_ref[...] = own_ref[...].astype(F32) + p1[...].astype(F32) + p2[...].astype(F32) + p3[...].astype(F32)

    def slot(k):
        return pl.BlockSpec((None, tr, cols), lambda i, pr: (lax.rem(pr[1] + k, 4), i, 0))

    return pl.pallas_call(
        body, name=name,
        grid_spec=pltpu.PrefetchScalarGridSpec(
            num_scalar_prefetch=1, grid=(half // tr,), in_specs=[slot(0), slot(1), slot(2), slot(3)],
            out_specs=pl.BlockSpec((None, tr, cols), lambda i, pr: (pr[0], i, 0))),
        out_shape=jax.ShapeDtypeStruct((2, half, cols), F32),
        compiler_params=_params("parallel"))(place, own, parts, parts, parts)


def join_halves(name, halves):
    n = len(halves)

    def body(*refs):
        ins, outs = refs[:n], refs[n:2 * n]
        send_sems, recv_sems = refs[2 * n:]
        x, y, c, _, _ = _place()
        copies = []
        for t in range(n):
            cp = _remote(ins[t].at[c], outs[t].at[c], send_sems.at[t], recv_sems.at[t], (x, y, 1 - c))
            cp.start()
            copies.append(cp)
        for t in range(n):
            theirs = outs[t].at[1 - c]
            _remote(theirs, theirs, send_sems.at[t], recv_sems.at[t], (x, y, 1 - c)).wait_recv()
        for cp in copies:
            cp.wait_send()

    return pl.pallas_call(
        body, name=name, in_specs=[HBM] * n, out_specs=[HBM] * n,
        out_shape=[jax.ShapeDtypeStruct(h.shape, h.dtype) for h in halves],
        input_output_aliases={t: t for t in range(n)},
        scratch_shapes=[pltpu.SemaphoreType.DMA((n,))] * 2,
        compiler_params=pltpu.CompilerParams(has_side_effects=True))(*halves)


def allreduce_small(name, buf):
    rows = buf.shape[0]

    def body(in_ref, out_ref, slots, send_sems, recv_sems):
        x, y, c, _, _ = _place()
        me = 4 * x + 2 * y + c
        slots[me] = in_ref[...]
        peers = []
        for k in range(1, 8):
            px = 1 - x if k & 4 else x
            py = 1 - y if k & 2 else y
            pc = 1 - c if k & 1 else c
            peers.append((px, py, pc))
        copies = []
        for k, peer in enumerate(peers):
            cp = _remote(in_ref, slots.at[me], send_sems.at[k], recv_sems.at[k], peer)
            cp.start()
            copies.append(cp)
        for k, (px, py, pc) in enumerate(peers):
            slot = slots.at[4 * px + 2 * py + pc]
            _remote(slot, slot, send_sems.at[k], recv_sems.at[k], (px, py, pc)).wait_recv()
        for cp in copies:
            cp.wait_send()
        acc = slots[0]
        for d in range(1, 8):
            acc = acc + slots[d]
        out_ref[...] = acc

    vm = pl.BlockSpec(memory_space=pltpu.VMEM)
    return pl.pallas_call(
        body, name=name, in_specs=[vm], out_specs=vm, out_shape=jax.ShapeDtypeStruct(buf.shape, F32),
        scratch_shapes=[pltpu.VMEM((8, rows, LANES), F32), pltpu.SemaphoreType.DMA((7,)), pltpu.SemaphoreType.DMA((7,))],
        compiler_params=pltpu.CompilerParams(has_side_effects=True, vmem_limit_bytes=VMEM_LIMIT))(buf)


def cast_layer(name, place, w, layer):
    _, r, cols = w.shape
    tr = _rows(r, cols)

    def body(place_ref, w_ref, o_ref):
        o_ref[...] = w_ref[...].astype(BF16)

    return pl.pallas_call(
        body, name=name,
        grid_spec=pltpu.PrefetchScalarGridSpec(
            num_scalar_prefetch=1, grid=(r // tr,),
            in_specs=[pl.BlockSpec((None, tr, cols), lambda i, pr: (layer, i, 0))],
            out_specs=pl.BlockSpec((None, tr, cols), lambda i, pr: (pr[1], i, 0))),
        out_shape=jax.ShapeDtypeStruct((4, r, cols), BF16),
        compiler_params=_params("parallel"))(place, w)


def _adamw_math(g, w, m, v):
    m = ADAM_B1 * m + (1.0 - ADAM_B1) * g
    v = ADAM_B2 * v + (1.0 - ADAM_B2) * (g * g)
    m_hat = m / (1.0 - ADAM_B1 ** ADAM_STEP)
    v_hat = v / (1.0 - ADAM_B2 ** ADAM_STEP)
    delta = -ADAM_LR * (m_hat / (jnp.sqrt(v_hat) + ADAM_EPS) + ADAM_WD * w)
    return delta, m, v


def adamw_layer(name, layer, g, w, m, v, prev=None):
    nl, r, cols = w.shape
    tr = _rows(r, cols, 8)

    def body(*refs):
        g_ref, w_ref, m_ref, v_ref = refs[:4]
        og, od, om, ov = refs[-4:]
        gv = g_ref[...]
        delta, m2, v2 = _adamw_math(gv, w_ref[...], m_ref[...], v_ref[...])
        og[...] = gv
        od[...] = delta
        om[...] = m2
        ov[...] = v2

    lay = pl.BlockSpec((None, tr, cols), lambda i: (layer, i, 0))
    ins = [g, w, m, v] + (list(prev) if prev is not None else [])
    return pl.pallas_call(
        body, name=name, grid=(r // tr,),
        in_specs=[pl.BlockSpec((tr, cols), lambda i: (i, 0)), lay, lay, lay] + ([HBM] * 4 if prev is not None else []),
        out_specs=[lay] * 4, out_shape=[jax.ShapeDtypeStruct((nl, r, cols), F32)] * 4,
        input_output_aliases={4 + k: k for k in range(4)} if prev is not None else {},
        compiler_params=_params("parallel"))(*ins)


def adamw_packed(name, g, w, m, v):
    def body(g_ref, w_ref, m_ref, v_ref, od, om, ov):
        delta, m2, v2 = _adamw_math(g_ref[...], w_ref[...], m_ref[...], v_ref[...])
        od[...] = delta
        om[...] = m2
        ov[...] = v2

    vm = pl.BlockSpec(memory_space=pltpu.VMEM)
    return pl.pallas_call(
        body, name=name, in_specs=[vm] * 4, out_specs=[vm] * 3, out_shape=[jax.ShapeDtypeStruct(g.shape, F32)] * 3,
        compiler_params=pltpu.CompilerParams(vmem_limit_bytes=VMEM_LIMIT))(g, w, m, v)


def _pack(arrays):
    flat = jnp.concatenate([a.reshape(-1) for a in arrays])
    pad = (-flat.shape[0]) % (8 * LANES)
    return jnp.pad(flat, (0, pad)).reshape(-1, LANES)


def _unpack(buf, shapes):
    flat = buf.reshape(-1)
    out, off = [], 0
    for shp in shapes:
        size = int(np.prod(shp))
        out.append(flat[off:off + size].reshape(shp))
        off += size
    return out


BIG = ("w_in", "w_uq", "w_ukv", "w_out", "w_up", "w_down")
COLUMN_CUT = ("w_in", "w_uq", "w_ukv", "w_up")
SMALL = ("attn_norm", "a_q_norm", "a_k_norm", "c_q_norm", "c_kv_norm", "out_norm", "ffn_norm", "conv_b", "final_norm")
WEIGHTS = ("attn_norm", "w_in", "a_q_norm", "a_k_norm", "c_q_norm", "c_kv_norm", "w_uq", "w_ukv", "out_norm", "w_out",
           "ffn_norm", "w_up", "conv_w", "conv_b", "w_down", "final_norm")
INPUTS = ("x",) + WEIGHTS + ("loss_target",) + tuple("m_" + n for n in WEIGHTS) + tuple("v_" + n for n in WEIGHTS)


def _columns(g):
    return jnp.transpose(g, (1, 0, 2)).reshape(g.shape[1], 4 * g.shape[2])


def _uncolumns(w):
    r, c4 = w.shape
    return jnp.transpose(w.reshape(r, 4, c4 // 4), (1, 0, 2))


def _compute_layouts(full, conv_w, small, layer):
    p = {n: small[n][layer] for n in SMALL if n != "final_norm"}
    w_in = _columns(full["w_in"])
    p["w_in"] = jnp.pad(w_in, ((0, 0), (0, PROJ_W - IN_W)))
    w_uq = _columns(full["w_uq"]).reshape(C_RANK, C_HEADS, HEAD + C_ROPE)
    p["w_uq"] = jnp.concatenate([w_uq[:, :, :HEAD].reshape(C_RANK, C_W),
                                 jnp.pad(w_uq[:, :, HEAD:], ((0, 0), (0, 0), (0, HEAD - C_ROPE))).reshape(C_RANK, C_W)], axis=1)
    p["w_ukv"] = _columns(full["w_ukv"])
    p["w_out"] = full["w_out"].reshape(-1, full["w_out"].shape[2])
    up = full["w_up"]
    p["w_up"] = up.reshape(2, 2, up.shape[1], up.shape[2])
    p["w_down"] = full["w_down"].reshape(-1, full["w_down"].shape[2])
    fh = conv_w.shape[1] // 2
    taps = jnp.transpose(conv_w.reshape(3, 2, fh), (1, 0, 2))
    p["cwb"] = jnp.concatenate([taps, small["conv_b"][layer].reshape(2, 1, fh), jnp.zeros((2, 4, fh), F32)], axis=1)
    return p


def _shard_layouts(g):
    out = {}
    out["w_in"] = _uncolumns(g["w_in"][:, :IN_W])
    uq = g["w_uq"]
    uq = jnp.concatenate([uq[:, :C_W].reshape(C_RANK, C_HEADS, HEAD),
                          uq[:, C_W:].reshape(C_RANK, C_HEADS, HEAD)[:, :, :C_ROPE]], axis=2)
    out["w_uq"] = _uncolumns(uq.reshape(C_RANK, C_HEADS * (HEAD + C_ROPE)))
    out["w_ukv"] = _uncolumns(g["w_ukv"])
    out["w_out"] = g["w_out"].reshape(4, -1, g["w_out"].shape[1])
    up = g["w_up"]
    out["w_up"] = up.reshape(4, up.shape[2], up.shape[3])
    out["w_down"] = g["w_down"].reshape(4, -1, g["w_down"].shape[1])
    dcwb = g["cwb"]
    fh = dcwb.shape[2]
    conv_w = jnp.transpose(dcwb[:, 0:3, :], (1, 0, 2)).reshape(3, 2 * fh)
    conv_b = dcwb[:, 3, :].reshape(2 * fh)
    return out, conv_w, conv_b


def kernel(x, attn_norm, w_in, a_q_norm, a_k_norm, c_q_norm, c_kv_norm, w_uq, w_ukv, out_norm, w_out, ffn_norm, w_up, conv_w, conv_b, w_down, final_norm, loss_target, m_attn_norm, m_w_in, m_a_q_norm, m_a_k_norm, m_c_q_norm, m_c_kv_norm, m_w_uq, m_w_ukv, m_out_norm, m_w_out, m_ffn_norm, m_w_up, m_conv_w, m_conv_b, m_w_down, m_final_norm, v_attn_norm, v_w_in, v_a_q_norm, v_a_k_norm, v_c_q_norm, v_c_kv_norm, v_w_uq, v_w_ukv, v_out_norm, v_w_out, v_ffn_norm, v_w_up, v_conv_w, v_conv_b, v_w_down, v_final_norm):
    a = dict(zip(INPUTS, (x, attn_norm, w_in, a_q_norm, a_k_norm, c_q_norm, c_kv_norm, w_uq, w_ukv, out_norm, w_out, ffn_norm, w_up, conv_w, conv_b, w_down, final_norm, loss_target, m_attn_norm, m_w_in, m_a_q_norm, m_a_k_norm, m_c_q_norm, m_c_kv_norm, m_w_uq, m_w_ukv, m_out_norm, m_w_out, m_ffn_norm, m_w_up, m_conv_w, m_conv_b, m_w_down, m_final_norm, v_attn_norm, v_w_in, v_a_q_norm, v_a_k_norm, v_c_q_norm, v_c_kv_norm, v_w_uq, v_w_ukv, v_out_norm, v_w_out, v_ffn_norm, v_w_up, v_conv_w, v_conv_b, v_w_down, v_final_norm)))
    nl = w_in.shape[0]
    seq = x.shape[1]
    chip = 2 * lax.axis_index("x") + lax.axis_index("y")
    place = jnp.stack([lax.axis_index("c"), chip]).astype(jnp.int32)
    tabs = _rope_tables(seq)

    quarter = conv_w.shape[2]
    taps = jnp.pad(conv_w, ((0, 0), (0, 8 - conv_w.shape[1]), (0, 0))).reshape(nl * 8, quarter)
    layers = []
    taps_full = None
    for l in range(nl):
        slots = [cast_layer(f"cast_{n}", place, a[n], l) for n in BIG]
        got = gather_shards(f"gather{l}", slots, [taps] if l == 0 else [])
        if l == 0:
            taps_full = got[-1].reshape(4, nl, 8, quarter)
        full = dict(zip(BIG, got))
        conv_full = jnp.transpose(taps_full[:, l, 0:3, :], (1, 0, 2)).reshape(3, 4 * quarter)
        layers.append(_compute_layouts(full, conv_full, a, l))

    loss, grad_x, grads, d_final = _local_step(x[0], loss_target[0], layers, final_norm, tabs)
    loss = lax.psum(loss, ("x", "y", "c"))

    reduced = []
    conv_grads = []
    for l in range(nl):
        by_shard, d_conv_w, d_conv_b = _shard_layouts(grads[l])
        conv_grads.append((d_conv_w, d_conv_b))
        g_list = [by_shard[n] for n in BIG]
        theirs = sibling_halves(f"reduce{l}_sibling", g_list)
        sums = [add_half(f"reduce{l}_add_{n}", place, g, t) for n, g, t in zip(BIG, g_list, theirs)]
        parts = exchange_shards(f"reduce{l}_chips", sums)
        halves = [sum_chips(f"reduce{l}_sum_{n}", place, s, p) for n, s, p in zip(BIG, sums, parts)]
        joined = join_halves(f"reduce{l}_join", halves)
        reduced.append({n: j.reshape(2 * j.shape[1], j.shape[2]) for n, j in zip(BIG, joined)})

    small_g = {n: jnp.stack([grads[l][n] for l in range(nl)]) for n in SMALL if n not in ("conv_b", "final_norm")}
    small_g["conv_b"] = jnp.stack([cg[1] for cg in conv_grads])
    small_g["final_norm"] = d_final
    conv_w_g = jnp.stack([cg[0] for cg in conv_grads])
    shapes = [a[n].shape for n in SMALL] + [conv_w_g.shape]
    summed = _unpack(allreduce_small("reduce_small", _pack([small_g[n] for n in SMALL] + [conv_w_g])), shapes)
    small_g = dict(zip(SMALL, summed[:-1]))
    small_g["conv_w"] = lax.dynamic_slice_in_dim(summed[-1], chip * quarter, quarter, axis=2)

    out = {}
    for n in BIG:
        res = None
        for l in range(nl):
            res = adamw_layer(f"adamw_{n}", l, reduced[l][n], a[n], a["m_" + n], a["v_" + n], prev=res)
        out[n] = res
    names = SMALL + ("conv_w",)
    shapes = [a[n].shape for n in names]
    d_small, m_small, v_small = adamw_packed(
        "adamw_small", _pack([small_g[n] for n in names]), _pack([a[n] for n in names]),
        _pack([a["m_" + n] for n in names]), _pack([a["v_" + n] for n in names]))
    for n, d_, m_, v_ in zip(names, _unpack(d_small, shapes), _unpack(m_small, shapes), _unpack(v_small, shapes)):
        out[n] = (small_g[n], d_, m_, v_)

    return (loss, grad_x[None]) + tuple(out[n][k] for k in range(4) for n in WEIGHTS)
```

```python
import functools

import jax
import jax.numpy as jnp
import numpy as np
from jax import lax
from jax.experimental import pallas as pl
from jax.experimental.pallas import tpu as pltpu

F32 = jnp.float32
BF16 = jnp.bfloat16
MESH = pl.DeviceIdType.MESH

V7X_VMEM_BYTES = 64 * 1024 * 1024
VMEM_LIMIT = V7X_VMEM_BYTES - 6 * 1024 * 1024
LANES = 128
ELEMENTWISE_BLOCK_BYTES = 2 * 1024 * 1024

HEAD = 128
A_HEADS, A_KV = 6, 2
B_HEADS, B_GROUPS = 4, 3
B_DILATIONS = (1, 4, 16)
B_HALF = 64
C_HEADS, C_RANK, C_ROPE = 6, 512, 64
GRID_W = 64
A_THETA, B_THETA, C_THETA = 10000.0, 500000.0, 10000.0
B_ROPE_DIM = 32
EPS = 1e-6
A_W, B_W, C_W = A_HEADS * HEAD, B_HEADS * HEAD, C_HEADS * HEAD
O_AQ, O_AK, O_AV = 0, 768, 1024
O_BQ, O_BK, O_BV = 1280, 2816, 3328
O_CQ, O_CKV, O_CKR = 3840, 4352, 4864
IN_W = 4928
PROJ_W = 5120

ADAM_LR, ADAM_B1, ADAM_B2, ADAM_EPS, ADAM_WD, ADAM_STEP = 0.001, 0.9, 0.999, 1e-08, 0.01, 10

NN = (((1,), (0,)), ((), ()))
NT = (((1,), (1,)), ((), ()))
TN = (((0,), (0,)), ((), ()))


def _dot(a, b, dims=NN):
    return lax.dot_general(a, b, dims, preferred_element_type=F32)


def _params(*sem):
    return pltpu.CompilerParams(dimension_semantics=sem if sem else None, vmem_limit_bytes=VMEM_LIMIT)


def _tile(n, target, unit=LANES):
    if n <= target:
        return n
    best = 0
    for t in range(unit, target + 1, unit):
        if n % t == 0:
            best = t
    assert best, (n, target)
    return best


def _rows(r, width, itemsize=4):
    return _tile(r, max(8, ELEMENTWISE_BLOCK_BYTES // (width * itemsize)), 8)


def _matmul_call(name, dims, grid, a_spec, b_spec, o_spec, out_shape, acc_shape, add_spec=None):
    nk = grid[2]

    def body(*refs):
        if add_spec is None:
            a_ref, b_ref, o_ref, acc = refs
            add_ref = None
        else:
            a_ref, b_ref, add_ref, o_ref, acc = refs
        k = pl.program_id(2)

        @pl.when(k == 0)
        def _():
            acc[...] = jnp.zeros_like(acc)

        acc[...] += _dot(a_ref[...], b_ref[...], dims)

        @pl.when(k == nk - 1)
        def _():
            r = acc[...]
            if add_ref is not None:
                r = r + add_ref[...]
            o_ref[...] = r.astype(o_ref.dtype)

    in_specs = [a_spec, b_spec] + ([] if add_spec is None else [add_spec])
    return pl.pallas_call(
        body, name=name, grid=grid, in_specs=in_specs, out_specs=o_spec, out_shape=out_shape,
        scratch_shapes=[pltpu.VMEM(acc_shape, F32)],
        compiler_params=_params("parallel", "parallel", "arbitrary"))


def matmul(name, a, b, mode, out_dtype, add=None, tm=1024, tn=1024, tk=1024):
    if mode == "nn":
        (m, k), (k2, n) = a.shape, b.shape
    elif mode == "nt":
        (m, k), (n, k2) = a.shape, b.shape
    else:
        (k, m), (k2, n) = a.shape, b.shape
    assert k == k2, (name, a.shape, b.shape)
    tm, tn, tk = _tile(m, tm), _tile(n, tn), _tile(k, tk)
    grid = (m // tm, n // tn, k // tk)
    if mode == "tn":
        a_spec = pl.BlockSpec((tk, tm), lambda i, j, kk: (kk, i))
    else:
        a_spec = pl.BlockSpec((tm, tk), lambda i, j, kk: (i, kk))
    if mode == "nt":
        b_spec = pl.BlockSpec((tn, tk), lambda i, j, kk: (j, kk))
    else:
        b_spec = pl.BlockSpec((tk, tn), lambda i, j, kk: (kk, j))
    o_spec = pl.BlockSpec((tm, tn), lambda i, j, kk: (i, j))
    dims = {"nn": NN, "nt": NT, "tn": TN}[mode]
    call = _matmul_call(name, dims, grid, a_spec, b_spec, o_spec, jax.ShapeDtypeStruct((m, n), out_dtype),
                        (tm, tn), None if add is None else o_spec)
    return call(a, b) if add is None else call(a, b, add)


def rmsnorm_fwd(name, x, g, width, out_dtype):
    r, cols = x.shape
    nb = cols // width
    tr = _rows(r, width)

    def body(x_ref, g_ref, o_ref):
        xv = x_ref[...]
        rs = lax.rsqrt(jnp.mean(xv * xv, axis=-1, keepdims=True) + EPS)
        o_ref[...] = (xv * rs * g_ref[...]).astype(o_ref.dtype)

    blk = pl.BlockSpec((tr, width), lambda i, j: (i, j))
    return pl.pallas_call(
        body, name=name, grid=(r // tr, nb),
        in_specs=[blk, pl.BlockSpec((1, width), lambda i, j: (0, 0))], out_specs=blk,
        out_shape=jax.ShapeDtypeStruct((r, cols), out_dtype),
        compiler_params=_params("parallel", "parallel"))(x, g.reshape(1, width))


def rmsnorm_bwd(name, dy, x, g, width, out_dtypes=(F32,), add=None):
    r, cols = x.shape
    nb = cols // width
    tr = _rows(r, width)
    n_out = len(out_dtypes)

    def body(*refs):
        dy_ref, x_ref, g_ref = refs[:3]
        add_ref = refs[3] if add is not None else None
        outs = refs[-(n_out + 1):-1]
        dg_ref = refs[-1]
        xv = x_ref[...]
        dyv = dy_ref[...].astype(F32)
        rs = lax.rsqrt(jnp.mean(xv * xv, axis=-1, keepdims=True) + EPS)
        xh = xv * rs
        dyg = dyv * g_ref[...]
        dx = rs * (dyg - xh * jnp.mean(dyg * xh, axis=-1, keepdims=True))
        if add_ref is not None:
            dx = dx + add_ref[...]
        for o in outs:
            o[...] = dx.astype(o.dtype)

        @pl.when((pl.program_id(0) == 0) & (pl.program_id(1) == 0))
        def _():
            dg_ref[...] = jnp.zeros_like(dg_ref)

        dg_ref[...] += jnp.sum(dyv * xh, axis=0, keepdims=True)

    blk = pl.BlockSpec((tr, width), lambda i, j: (i, j))
    vec = pl.BlockSpec((1, width), lambda i, j: (0, 0))
    ins = [dy, x, g.reshape(1, width)] + ([add] if add is not None else [])
    res = pl.pallas_call(
        body, name=name, grid=(r // tr, nb),
        in_specs=[blk, blk, vec] + ([blk] if add is not None else []),
        out_specs=[blk] * n_out + [vec],
        out_shape=[jax.ShapeDtypeStruct((r, cols), dt) for dt in out_dtypes] + [jax.ShapeDtypeStruct((1, width), F32)],
        compiler_params=_params("arbitrary", "arbitrary"))(*ins)
    return tuple(res[:n_out]) + (res[n_out].reshape(width),)


def _rope_angles(pos, dim, theta):
    inv = theta ** (-jnp.arange(0, dim, 2, dtype=F32) / dim)
    ang = pos.astype(F32)[:, None] * inv[None, :]
    return jnp.cos(ang), jnp.sin(ang)


def _rope_tables(s):
    rows = s // GRID_W
    row_pos = jnp.repeat(jnp.arange(rows), GRID_W)
    col_pos = jnp.tile(jnp.arange(GRID_W), rows)
    t_pos = jnp.arange(s)
    z = lambda n: jnp.zeros((s, n), F32)
    o = lambda n: jnp.ones((s, n), F32)
    cr, sr = _rope_angles(row_pos, HEAD // 2, A_THETA)
    cc, sc = _rope_angles(col_pos, HEAD // 2, A_THETA)
    tab_a = (jnp.concatenate([cr, cr, cc, cc], 1), jnp.concatenate([-sr, z(32), -sc, z(32)], 1),
             jnp.concatenate([z(32), sr, z(32), sc], 1), 32)
    cp, sp = _rope_angles(t_pos, B_ROPE_DIM, B_THETA)
    tab_b = (jnp.concatenate([cp, cp, o(96)], 1), jnp.concatenate([-sp, z(112)], 1),
             jnp.concatenate([z(16), sp, z(96)], 1), 16)
    cm, sm = _rope_angles(t_pos, C_ROPE, C_THETA)
    tab_c = (jnp.concatenate([cm, cm, o(64)], 1), jnp.concatenate([-sm, z(96)], 1),
             jnp.concatenate([z(32), sm, z(64)], 1), 32)

    def transposed(tab):
        c, s1, s2, h = tab
        return (c, jnp.roll(s2, -h, axis=1), jnp.roll(s1, h, axis=1), h)

    return {k: (t, transposed(t)) for k, t in (("a", tab_a), ("b", tab_b), ("c", tab_c))}


def rope(name, x, tab, out_dtype):
    c, s1, s2, h = tab
    s, cols = x.shape
    nh = cols // HEAD
    ts = _rows(s, HEAD)

    def body(x_ref, c_ref, s1_ref, s2_ref, o_ref):
        xv = x_ref[...].astype(F32)
        out = xv * c_ref[...] + pltpu.roll(xv, HEAD - h, 1) * s1_ref[...] + pltpu.roll(xv, h, 1) * s2_ref[...]
        o_ref[...] = out.astype(o_ref.dtype)

    blk = pl.BlockSpec((ts, HEAD), lambda i, j: (i, j))
    tb = pl.BlockSpec((ts, HEAD), lambda i, j: (i, 0))
    return pl.pallas_call(
        body, name=name, grid=(s // ts, nh), in_specs=[blk, tb, tb, tb], out_specs=blk,
        out_shape=jax.ShapeDtypeStruct((s, cols), out_dtype),
        compiler_params=_params("parallel", "parallel"))(x, c, s1, s2)


ATTN_TQ = 256


def attn_fwd(name, parts, v, v_group, nheads, scale):
    s = v.shape[0]
    tq = ATTN_TQ
    npart = len(parts)

    def body(*refs):
        v_ref, o_ref, lse_ref = refs[2 * npart:]
        sc = None
        for p in range(npart):
            t = _dot(refs[2 * p][...], refs[2 * p + 1][...], NT)
            sc = t if sc is None else sc + t
        sc = sc * scale
        m = jnp.max(sc, axis=-1, keepdims=True)
        e = jnp.exp(sc - m)
        l = jnp.sum(e, axis=-1, keepdims=True)
        o_ref[...] = _dot((e / l).astype(BF16), v_ref[...])
        lse_ref[...] = jnp.broadcast_to(m + jnp.log(l), (tq, HEAD))

    in_specs, ins = [], []
    for q, qoff, k, kg in parts:
        in_specs.append(pl.BlockSpec((tq, HEAD), lambda h, i, qoff=qoff: (i, qoff + h)))
        in_specs.append(pl.BlockSpec((s, HEAD), lambda h, i, kg=kg: (0, h // kg)))
        ins += [q, k]
    in_specs.append(pl.BlockSpec((s, HEAD), lambda h, i: (0, h // v_group)))
    out_blk = pl.BlockSpec((tq, HEAD), lambda h, i: (i, h))
    return pl.pallas_call(
        body, name=name, grid=(nheads, s // tq), in_specs=in_specs, out_specs=[out_blk, out_blk],
        out_shape=[jax.ShapeDtypeStruct((s, nheads * HEAD), F32)] * 2,
        compiler_params=_params("parallel", "parallel"))(*ins, v)


def attn_bwd(name, parts, v, v_group, lse, do, nheads, scale):
    s = v.shape[0]
    tq = ATTN_TQ
    npart = len(parts)

    def body(*refs):
        v_ref, lse_ref, do_ref = refs[2 * npart:2 * npart + 3]
        outs = refs[2 * npart + 3:]
        dq_refs, dk_refs, dv_ref = outs[:npart], outs[npart:2 * npart], outs[2 * npart]
        h, i = pl.program_id(0), pl.program_id(1)
        sc = None
        for p in range(npart):
            t = _dot(refs[2 * p][...], refs[2 * p + 1][...], NT)
            sc = t if sc is None else sc + t
        pr = jnp.exp(sc * scale - lse_ref[:, 0:1])
        dov = do_ref[...]
        dp = _dot(dov, v_ref[...], NT)
        delta = jnp.sum(pr * dp, axis=-1, keepdims=True)
        ds = (pr * (dp - delta) * scale).astype(BF16)

        @pl.when((i == 0) & (h % v_group == 0))
        def _():
            dv_ref[...] = jnp.zeros_like(dv_ref)

        dv_ref[...] += _dot(pr.astype(BF16), dov, TN)
        for p in range(npart):
            kg = parts[p][3]
            dq_refs[p][...] = _dot(ds, refs[2 * p + 1][...])

            @pl.when((i == 0) & (h % kg == 0))
            def _(p=p):
                dk_refs[p][...] = jnp.zeros_like(dk_refs[p])

            dk_refs[p][...] += _dot(ds, refs[2 * p][...], TN)

    in_specs, ins = [], []
    for q, qoff, k, kg in parts:
        in_specs.append(pl.BlockSpec((tq, HEAD), lambda h, i, qoff=qoff: (i, qoff + h)))
        in_specs.append(pl.BlockSpec((s, HEAD), lambda h, i, kg=kg: (0, h // kg)))
        ins += [q, k]
    hq_blk = pl.BlockSpec((tq, HEAD), lambda h, i: (i, h))
    in_specs += [pl.BlockSpec((s, HEAD), lambda h, i: (0, h // v_group)), hq_blk, hq_blk]
    out_specs = [hq_blk] * npart
    out_shape = [jax.ShapeDtypeStruct((s, nheads * HEAD), F32)] * npart
    for q, qoff, k, kg in parts:
        out_specs.append(pl.BlockSpec((s, HEAD), lambda h, i, kg=kg: (0, h // kg)))
        out_shape.append(jax.ShapeDtypeStruct((s, nheads // kg * HEAD), F32))
    out_specs.append(pl.BlockSpec((s, HEAD), lambda h, i: (0, h // v_group)))
    out_shape.append(jax.ShapeDtypeStruct((s, nheads // v_group * HEAD), F32))
    res = pl.pallas_call(
        body, name=name, grid=(nheads, s // tq), in_specs=in_specs, out_specs=out_specs, out_shape=out_shape,
        compiler_params=_params("arbitrary", "arbitrary"))(*ins, v, lse, do)
    return list(res[:npart]), list(res[npart:2 * npart]), res[2 * npart]


def _band_windows(lf):
    for ib in range(lf // HEAD):
        q0 = ib * HEAD
        yield q0, max(0, q0 - B_HALF), min(lf, q0 + HEAD + B_HALF)


def _band_mask(q0, lo, hi):
    qpos = q0 + lax.broadcasted_iota(jnp.int32, (HEAD, hi - lo), 0)
    kpos = lo + lax.broadcasted_iota(jnp.int32, (HEAD, hi - lo), 1)
    return jnp.abs(qpos - kpos) <= B_HALF


def _band_specs(d, q_cols, q_off):
    s_q = pl.BlockSpec
    return s_q, (lambda r, h: (0, r * (q_cols // HEAD) + q_off + h)), (lambda r, h: (0, r * B_HEADS + h))


def band_fwd(name, q, q_off, k, v, d):
    s = k.shape[0]
    lf = s // d
    scale = HEAD ** -0.5

    def body(q_ref, k_ref, v_ref, o_ref, lse_ref):
        for q0, lo, hi in _band_windows(lf):
            sc = _dot(q_ref[q0:q0 + HEAD, :], k_ref[lo:hi, :], NT) * scale
            sc = jnp.where(_band_mask(q0, lo, hi), sc, -1e30)
            m = jnp.max(sc, axis=-1, keepdims=True)
            e = jnp.exp(sc - m)
            l = jnp.sum(e, axis=-1, keepdims=True)
            o_ref[q0:q0 + HEAD, :] = _dot((e / l).astype(BF16), v_ref[lo:hi, :])
            lse_ref[q0:q0 + HEAD, :] = jnp.broadcast_to(m + jnp.log(l), (HEAD, HEAD))

    _, qmap, kmap = _band_specs(d, q.shape[1], q_off)
    blk = lambda f: pl.BlockSpec((lf, HEAD), f)
    o, lse = pl.pallas_call(
        body, name=name, grid=(d, B_HEADS), in_specs=[blk(qmap), blk(kmap), blk(kmap)],
        out_specs=[blk(kmap), blk(kmap)], out_shape=[jax.ShapeDtypeStruct((lf, d * B_W), F32)] * 2,
        compiler_params=_params("parallel", "parallel"))(
            q.reshape(lf, d * q.shape[1]), k.reshape(lf, d * B_W), v.reshape(lf, d * B_W))
    return o.reshape(s, B_W), lse.reshape(s, B_W)


def band_bwd(name, q, q_off, k, v, dy, lse, delta, d):
    s = k.shape[0]
    lf = s // d
    scale = HEAD ** -0.5

    def body(q_ref, k_ref, v_ref, dy_ref, lse_ref, dl_ref, dq_ref, dk_ref, dv_ref):
        dk_ref[...] = jnp.zeros_like(dk_ref)
        dv_ref[...] = jnp.zeros_like(dv_ref)
        for q0, lo, hi in _band_windows(lf):
            qv, kv, vv = q_ref[q0:q0 + HEAD, :], k_ref[lo:hi, :], v_ref[lo:hi, :]
            dyv = dy_ref[q0:q0 + HEAD, :]
            sc = _dot(qv, kv, NT) * scale
            pr = jnp.where(_band_mask(q0, lo, hi), jnp.exp(sc - lse_ref[q0:q0 + HEAD, 0:1]), 0.0)
            dp = _dot(dyv, vv, NT)
            ds = (pr * (dp - dl_ref[q0:q0 + HEAD, 0:1]) * scale).astype(BF16)
            dv_ref[lo:hi, :] += _dot(pr.astype(BF16), dyv, TN)
            dq_ref[q0:q0 + HEAD, :] = _dot(ds, kv)
            dk_ref[lo:hi, :] += _dot(ds, qv, TN)

    _, qmap, kmap = _band_specs(d, q.shape[1], q_off)
    blk = lambda f: pl.BlockSpec((lf, HEAD), f)
    fold = lambda a: a.reshape(lf, d * a.shape[1])
    res = pl.pallas_call(
        body, name=name, grid=(d, B_HEADS), in_specs=[blk(qmap)] + [blk(kmap)] * 5,
        out_specs=[blk(kmap)] * 3, out_shape=[jax.ShapeDtypeStruct((lf, d * B_W), F32)] * 3,
        compiler_params=_params("parallel", "parallel"))(fold(q), fold(k), fold(v), fold(dy), fold(lse), fold(delta))
    return tuple(r.reshape(s, B_W) for r in res)


def mix_fwd(name, outs, lses):
    s, w = outs[0].shape
    ts = _rows(s, w)

    def body(o0, o1, o2, l0, l1, l2, y_ref, lse_ref):
        a, b, c = l0[...], l1[...], l2[...]
        m = jnp.maximum(jnp.maximum(a, b), c)
        ea, eb, ec = jnp.exp(a - m), jnp.exp(b - m), jnp.exp(c - m)
        den = ea + eb + ec
        y_ref[...] = (ea / den) * o0[...] + (eb / den) * o1[...] + (ec / den) * o2[...]
        lse_ref[...] = m + jnp.log(den)

    blk = pl.BlockSpec((ts, w), lambda i: (i, 0))
    return pl.pallas_call(
        body, name=name, grid=(s // ts,), in_specs=[blk] * 6, out_specs=[blk, blk],
        out_shape=[jax.ShapeDtypeStruct((s, w), F32)] * 2, compiler_params=_params("parallel"))(*outs, *lses)


def head_rowdot(name, dy, y):
    s, cols = y.shape
    ts = _rows(s, HEAD)

    def body(dy_ref, y_ref, dl_ref, dyb_ref):
        dyv = dy_ref[...]
        dl_ref[...] = jnp.broadcast_to(jnp.sum(dyv * y_ref[...], axis=-1, keepdims=True), (ts, HEAD))
        dyb_ref[...] = dyv.astype(BF16)

    blk = pl.BlockSpec((ts, HEAD), lambda i, j: (i, j))
    return pl.pallas_call(
        body, name=name, grid=(s // ts, cols // HEAD), in_specs=[blk, blk], out_specs=[blk, blk],
        out_shape=[jax.ShapeDtypeStruct((s, cols), F32), jax.ShapeDtypeStruct((s, cols), BF16)],
        compiler_params=_params("parallel", "parallel"))(dy, y)


def add3(name, a, b, c, out_dtype=F32):
    s, w = a.shape
    ts = _rows(s, w)

    def body(a_ref, b_ref, c_ref, o_ref):
        o_ref[...] = (a_ref[...] + b_ref[...] + c_ref[...]).astype(o_ref.dtype)

    blk = pl.BlockSpec((ts, w), lambda i: (i, 0))
    return pl.pallas_call(
        body, name=name, grid=(s // ts,), in_specs=[blk] * 3, out_specs=blk,
        out_shape=jax.ShapeDtypeStruct((s, w), out_dtype), compiler_params=_params("parallel"))(a, b, c)


FFN_TN = 256


def _conv(h, cw):
    s = h.shape[0]
    row = lax.broadcasted_iota(jnp.int32, h.shape, 0)
    prev = jnp.where(row == 0, 0.0, pltpu.roll(h, 1, 0))
    nxt = jnp.where(row == s - 1, 0.0, pltpu.roll(h, s - 1, 0))
    return prev * cw[0:1, :] + h * cw[1:2, :] + nxt * cw[2:3, :] + cw[3:4, :], prev, nxt


def _sigmoid(x):
    return 1.0 / (1.0 + jnp.exp(-x))


def ffn_up(name, xn, w_up, cwb):
    s, dm = xn.shape
    quarter = w_up.shape[3]
    fh = 2 * quarter
    tn = FFN_TN
    per = quarter // tn

    def body(x_ref, w_ref, cw_ref, h_ref, act_ref):
        xv = x_ref[...]
        hg = _dot(xv, w_ref[0])
        hu = _dot(xv, w_ref[1])
        h_ref[0] = hg
        h_ref[1] = hu
        gc, _, _ = _conv(hg, cw_ref[0])
        uc, _, _ = _conv(hu, cw_ref[1])
        act_ref[...] = (gc * _sigmoid(gc) * uc).astype(BF16)

    return pl.pallas_call(
        body, name=name, grid=(fh // tn,),
        in_specs=[pl.BlockSpec((s, dm), lambda t: (0, 0)),
                  pl.BlockSpec((2, None, dm, tn), lambda t: (0, t // per, 0, t % per)),
                  pl.BlockSpec((2, 8, tn), lambda t: (0, 0, t))],
        out_specs=[pl.BlockSpec((2, s, tn), lambda t: (0, 0, t)), pl.BlockSpec((s, tn), lambda t: (0, t))],
        out_shape=[jax.ShapeDtypeStruct((2, s, fh), F32), jax.ShapeDtypeStruct((s, fh), BF16)],
        compiler_params=_params("parallel"))(xn, w_up, cwb)


def ffn_gate_bwd(name, h, dact, cwb):
    _, s, fh = h.shape
    tn = FFN_TN

    def body(h_ref, da_ref, cw_ref, dh_ref, dcw_ref):
        gc, gp, gn = _conv(h_ref[0], cw_ref[0])
        uc, up, un = _conv(h_ref[1], cw_ref[1])
        sg = _sigmoid(gc)
        da = da_ref[...]
        dgc = da * uc * (sg * (1.0 + gc * (1.0 - sg)))
        duc = da * (gc * sg)
        row = lax.broadcasted_iota(jnp.int32, da.shape, 0)
        for idx, (hv, prev, nxt, dc) in enumerate(((h_ref[0], gp, gn, dgc), (h_ref[1], up, un, duc))):
            cw = cw_ref[idx]
            from_next = jnp.where(row == s - 1, 0.0, pltpu.roll(dc, s - 1, 0))
            from_prev = jnp.where(row == 0, 0.0, pltpu.roll(dc, 1, 0))
            dh_ref[idx] = (from_next * cw[0:1, :] + dc * cw[1:2, :] + from_prev * cw[2:3, :]).astype(BF16)
            dcw_ref[idx, 0:1, :] = jnp.sum(prev * dc, axis=0, keepdims=True)
            dcw_ref[idx, 1:2, :] = jnp.sum(hv * dc, axis=0, keepdims=True)
            dcw_ref[idx, 2:3, :] = jnp.sum(nxt * dc, axis=0, keepdims=True)
            dcw_ref[idx, 3:4, :] = jnp.sum(dc, axis=0, keepdims=True)
            dcw_ref[idx, 4:8, :] = jnp.zeros((4, tn), F32)

    return pl.pallas_call(
        body, name=name, grid=(fh // tn,),
        in_specs=[pl.BlockSpec((2, s, tn), lambda t: (0, 0, t)), pl.BlockSpec((s, tn), lambda t: (0, t)),
                  pl.BlockSpec((2, 8, tn), lambda t: (0, 0, t))],
        out_specs=[pl.BlockSpec((2, s, tn), lambda t: (0, 0, t)), pl.BlockSpec((2, 8, tn), lambda t: (0, 0, t))],
        out_shape=[jax.ShapeDtypeStruct((2, s, fh), BF16), jax.ShapeDtypeStruct((2, 8, fh), F32)],
        compiler_params=_params("parallel"))(h, dact, cwb)


def ffn_dx(name, dh, w_up):
    _, s, fh = dh.shape
    dm, quarter = w_up.shape[2], w_up.shape[3]
    tk = _tile(quarter, 1536)
    per = quarter // tk
    tm, tn = _tile(s, 1024), _tile(dm, 1024)
    grid = (s // tm, dm // tn, 4 * per)
    a_spec = pl.BlockSpec((None, tm, tk), lambda i, j, k: (k // (2 * per), i, k % (2 * per)))
    b_spec = pl.BlockSpec((None, None, tn, tk), lambda i, j, k: (k // (2 * per), (k // per) % 2, j, k % per))
    o_spec = pl.BlockSpec((tm, tn), lambda i, j, k: (i, j))
    return _matmul_call(name, NT, grid, a_spec, b_spec, o_spec, jax.ShapeDtypeStruct((s, dm), F32), (tm, tn))(dh, w_up)


def ffn_dw_up(name, xn, dh):
    _, s, fh = dh.shape
    dm = xn.shape[1]
    quarter = fh // 2
    tn = _tile(quarter, 1536)
    per = quarter // tn
    tm, tk = _tile(dm, 1024), _tile(s, 1024)
    grid = (dm // tm, 4 * per, s // tk)
    a_spec = pl.BlockSpec((tk, tm), lambda i, j, k: (k, i))
    b_spec = pl.BlockSpec((None, tk, tn), lambda i, j, k: (j // (2 * per), k, j % (2 * per)))
    o_spec = pl.BlockSpec((None, None, tm, tn), lambda i, j, k: (j // (2 * per), (j // per) % 2, i, j % per))
    return _matmul_call(name, TN, grid, a_spec, b_spec, o_spec,
                        jax.ShapeDtypeStruct((2, 2, dm, quarter), F32), (tm, tn))(xn, dh)


def loss_head(name, y, target):
    s, dm = y.shape
    ts = _rows(s, dm)

    def body(y_ref, t_ref, loss_ref, dy_ref, acc):
        i = pl.program_id(0)

        @pl.when(i == 0)
        def _():
            acc[...] = jnp.zeros_like(acc)

        err = y_ref[...] - t_ref[...]
        dy_ref[...] = err * (1.0 / dm)
        acc[...] += jnp.sum(err * err, axis=0, keepdims=True)

        @pl.when(i == s // ts - 1)
        def _():
            loss_ref[...] = jnp.broadcast_to(jnp.sum(acc[...], axis=-1, keepdims=True) * (0.5 / dm), (1, LANES))

    blk = pl.BlockSpec((ts, dm), lambda i: (i, 0))
    loss, dy = pl.pallas_call(
        body, name=name, grid=(s // ts,), in_specs=[blk, blk],
        out_specs=[pl.BlockSpec((1, LANES), lambda i: (0, 0)), blk],
        out_shape=[jax.ShapeDtypeStruct((1, LANES), F32), jax.ShapeDtypeStruct((s, dm), F32)],
        scratch_shapes=[pltpu.VMEM((1, dm), F32)], compiler_params=_params("arbitrary"))(y, target)
    return loss[0, 0], dy


def _c_parts(qc, kvc, krc):
    return [(qc, 0, kvc, None), (qc, C_HEADS, krc, C_HEADS)]


def attn_fwd_c(name, qc, kvc, krc, scale):
    s = qc.shape[0]
    tq = ATTN_TQ

    def body(qn_ref, qr_ref, kn_ref, kr_ref, v_ref, o_ref, lse_ref):
        sc = (_dot(qn_ref[...], kn_ref[...], NT) + _dot(qr_ref[...], kr_ref[...], NT)) * scale
        m = jnp.max(sc, axis=-1, keepdims=True)
        e = jnp.exp(sc - m)
        l = jnp.sum(e, axis=-1, keepdims=True)
        o_ref[...] = _dot((e / l).astype(BF16), v_ref[...])
        lse_ref[...] = jnp.broadcast_to(m + jnp.log(l), (tq, HEAD))

    qb = lambda off: pl.BlockSpec((tq, HEAD), lambda h, i: (i, off + h))
    kb = lambda f: pl.BlockSpec((s, HEAD), f)
    out_blk = pl.BlockSpec((tq, HEAD), lambda h, i: (i, h))
    return pl.pallas_call(
        body, name=name, grid=(C_HEADS, s // tq),
        in_specs=[qb(0), qb(C_HEADS), kb(lambda h, i: (0, 2 * h)), kb(lambda h, i: (0, 0)), kb(lambda h, i: (0, 2 * h + 1))],
        out_specs=[out_blk, out_blk], out_shape=[jax.ShapeDtypeStruct((s, C_W), F32)] * 2,
        compiler_params=_params("parallel", "parallel"))(qc, qc, kvc, krc, kvc)


def attn_bwd_c(name, qc, kvc, krc, lse, do, scale):
    s = qc.shape[0]
    tq = ATTN_TQ

    def body(qn_ref, qr_ref, kn_ref, kr_ref, v_ref, lse_ref, do_ref, dqn_ref, dqr_ref, dkn_ref, dv_ref, dkr_ref):
        h, i = pl.program_id(0), pl.program_id(1)
        sc = (_dot(qn_ref[...], kn_ref[...], NT) + _dot(qr_ref[...], kr_ref[...], NT)) * scale
        pr = jnp.exp(sc - lse_ref[:, 0:1])
        dov = do_ref[...]
        dp = _dot(dov, v_ref[...], NT)
        delta = jnp.sum(pr * dp, axis=-1, keepdims=True)
        ds = (pr * (dp - delta) * scale).astype(BF16)

        @pl.when(i == 0)
        def _():
            dv_ref[...] = jnp.zeros_like(dv_ref)
            dkn_ref[...] = jnp.zeros_like(dkn_ref)

        @pl.when((i == 0) & (h == 0))
        def _():
            dkr_ref[...] = jnp.zeros_like(dkr_ref)

        dv_ref[...] += _dot(pr.astype(BF16), dov, TN)
        dqn_ref[...] = _dot(ds, kn_ref[...])
        dqr_ref[...] = _dot(ds, kr_ref[...])
        dkn_ref[...] += _dot(ds, qn_ref[...], TN)
        dkr_ref[...] += _dot(ds, qr_ref[...], TN)

    qb = lambda off: pl.BlockSpec((tq, HEAD), lambda h, i: (i, off + h))
    kb = lambda f: pl.BlockSpec((s, HEAD), f)
    hq = pl.BlockSpec((tq, HEAD), lambda h, i: (i, h))
    kn_map, v_map, kr_map = (lambda h, i: (0, 2 * h)), (lambda h, i: (0, 2 * h + 1)), (lambda h, i: (0, 0))
    dqc_shape = jax.ShapeDtypeStruct((s, 2 * C_W), F32)
    dqn, dqr, dkn, dv, dkr = pl.pallas_call(
        body, name=name, grid=(C_HEADS, s // tq),
        in_specs=[qb(0), qb(C_HEADS), kb(kn_map), kb(kr_map), kb(v_map), hq, hq],
        out_specs=[hq, hq, kb(lambda h, i: (0, h)), kb(lambda h, i: (0, h)), kb(kr_map)],
        out_shape=[jax.ShapeDtypeStruct((s, C_W), F32)] * 4 + [jax.ShapeDtypeStruct((s, HEAD), F32)],
        compiler_params=_params("arbitrary", "arbitrary"))(qc, qc, kvc, krc, kvc, lse, do)
    del dqc_shape
    return dqn, dqr, dkn, dv, dkr


def _layer_fwd(x, p, tabs):
    x1, sv = _mixers_fwd(x, p, tabs)
    return _ffn_fwd(x1, p, sv), sv


def _mixers_fwd(x, p, tabs):
    sv = {"x": x}
    hn = rmsnorm_fwd("attn_norm", x, p["attn_norm"], x.shape[1], BF16)
    proj = matmul("in_proj", hn, p["w_in"], "nn", F32)
    sv["hn"] = hn
    aq, ak = proj[:, O_AQ:O_AK], proj[:, O_AK:O_AV]
    av = proj[:, O_AV:O_BQ].astype(BF16)
    bq, bk = proj[:, O_BQ:O_BK], proj[:, O_BK:O_BV]
    bv = proj[:, O_BV:O_CQ].astype(BF16)
    cq, ckv, ckr = proj[:, O_CQ:O_CKV], proj[:, O_CKV:O_CKR], proj[:, O_CKR:O_CKR + HEAD]
    sv.update(aq=aq, ak=ak, av=av, bv=bv, cq=cq, ckv=ckv)

    qa = rope("a_q_rope", rmsnorm_fwd("a_q_norm", aq, p["a_q_norm"], HEAD, F32), tabs["a"][0], BF16)
    ka = rope("a_k_rope", rmsnorm_fwd("a_k_norm", ak, p["a_k_norm"], HEAD, F32), tabs["a"][0], BF16)
    ya, lse_a = attn_fwd("a_attn", [(qa, 0, ka, A_HEADS // A_KV)], av, A_HEADS // A_KV, A_HEADS, HEAD ** -0.5)
    sv.update(qa=qa, ka=ka, ya=ya, lse_a=lse_a)

    qb = rope("b_q_rope", bq, tabs["b"][0], BF16)
    kb = rope("b_k_rope", bk, tabs["b"][0], BF16)
    outs, lses = [], []
    for g, d in enumerate(B_DILATIONS):
        o, l = band_fwd(f"b_band{g}", qb, g * B_HEADS, kb, bv, d)
        outs.append(o)
        lses.append(l)
    yb, lse_b = mix_fwd("b_mix", outs, lses)
    sv.update(qb=qb, kb=kb, yb=yb, lse_b=lse_b)

    cqn = rmsnorm_fwd("c_q_norm", cq, p["c_q_norm"], C_RANK, BF16)
    qc_raw = matmul("c_uq", cqn, p["w_uq"], "nn", F32)
    qc = jnp.concatenate([qc_raw[:, :C_W].astype(BF16), rope("c_q_rope", qc_raw[:, C_W:], tabs["c"][0], BF16)], axis=1)
    ckvn = rmsnorm_fwd("c_kv_norm", ckv, p["c_kv_norm"], C_RANK, BF16)
    kvc = matmul("c_ukv", ckvn, p["w_ukv"], "nn", BF16)
    krc = rope("c_k_rope", ckr, tabs["c"][0], BF16)
    c_scale = (HEAD + C_ROPE) ** -0.5
    yc, lse_c = attn_fwd_c("c_attn", qc, kvc, krc, c_scale)
    sv.update(cqn=cqn, ckvn=ckvn, qc=qc, kvc=kvc, krc=krc, yc=yc, lse_c=lse_c)

    g_a, g_b, g_c = p["out_norm"][:A_W], p["out_norm"][A_W:A_W + B_W], p["out_norm"][A_W + B_W:]
    y = jnp.concatenate([rmsnorm_fwd("out_norm_a", ya, g_a, A_W, BF16), rmsnorm_fwd("out_norm_b", yb, g_b, B_W, BF16),
                         rmsnorm_fwd("out_norm_c", yc, g_c, C_W, BF16)], axis=1)
    x1 = matmul("out_proj", y, p["w_out"], "nn", F32, add=x)
    sv.update(y=y, x1=x1)
    return x1, sv


def _ffn_fwd(x1, p, sv):
    xn = rmsnorm_fwd("ffn_norm", x1, p["ffn_norm"], x1.shape[1], BF16)
    h, act = ffn_up("ffn_up", xn, p["w_up"], p["cwb"])
    x2 = matmul("ffn_down", act, p["w_down"], "nn", F32, add=x1)
    sv.update(xn=xn, h=h, act=act)
    return x2


def _layer_bwd(dx2, sv, p, tabs, after=0.0):
    g = {}
    dm = dx2.shape[1]
    dx2b = (dx2 + after).astype(BF16)
    dact = matmul("ffn_down_dx", dx2b, p["w_down"], "nt", F32)
    g["w_down"] = matmul("ffn_down_dw", sv["act"], dx2b, "tn", F32)
    dh, g["cwb"] = ffn_gate_bwd("ffn_gate_bwd", sv["h"], dact, p["cwb"])
    dxn = ffn_dx("ffn_up_dx", dh, p["w_up"])
    g["w_up"] = ffn_dw_up("ffn_up_dw", sv["xn"], dh)
    dx1, dx1b, g["ffn_norm"] = rmsnorm_bwd("ffn_norm_bwd", dxn, sv["x1"], p["ffn_norm"], dm, (F32, BF16), add=dx2)

    dy = matmul("out_proj_dx", dx1b, p["w_out"], "nt", F32)
    g["w_out"] = matmul("out_proj_dw", sv["y"], dx1b, "tn", F32)
    g_a, g_b, g_c = p["out_norm"][:A_W], p["out_norm"][A_W:A_W + B_W], p["out_norm"][A_W + B_W:]
    dya, dga = rmsnorm_bwd("out_norm_a_bwd", dy[:, :A_W], sv["ya"], g_a, A_W, (BF16,))
    dyb, dgb = rmsnorm_bwd("out_norm_b_bwd", dy[:, A_W:A_W + B_W], sv["yb"], g_b, B_W, (F32,))
    dyc, dgc = rmsnorm_bwd("out_norm_c_bwd", dy[:, A_W + B_W:], sv["yc"], g_c, C_W, (BF16,))
    g["out_norm"] = jnp.concatenate([dga, dgb, dgc])

    c_scale = (HEAD + C_ROPE) ** -0.5
    dqn, dqr, dkn, dvc, dkr = attn_bwd_c("c_attn_bwd", sv["qc"], sv["kvc"], sv["krc"], sv["lse_c"], dyc, c_scale)
    s = dx2.shape[0]
    dqc = jnp.concatenate([dqn.astype(BF16), rope("c_q_rope_bwd", dqr, tabs["c"][1], BF16)], axis=1)
    dcqn = matmul("c_uq_dx", dqc, p["w_uq"], "nt", F32)
    g["w_uq"] = matmul("c_uq_dw", sv["cqn"], dqc, "tn", F32)
    dcq, g["c_q_norm"] = rmsnorm_bwd("c_q_norm_bwd", dcqn, sv["cq"], p["c_q_norm"], C_RANK, (BF16,))
    dkvc = jnp.stack([dkn.reshape(s, C_HEADS, HEAD), dvc.reshape(s, C_HEADS, HEAD)], axis=2).reshape(s, 2 * C_W).astype(BF16)
    dckvn = matmul("c_ukv_dx", dkvc, p["w_ukv"], "nt", F32)
    g["w_ukv"] = matmul("c_ukv_dw", sv["ckvn"], dkvc, "tn", F32)
    dckv, g["c_kv_norm"] = rmsnorm_bwd("c_kv_norm_bwd", dckvn, sv["ckv"], p["c_kv_norm"], C_RANK, (BF16,))
    dckr = rope("c_k_rope_bwd", dkr, tabs["c"][1], BF16)

    delta_b, dybb = head_rowdot("b_delta", dyb, sv["yb"])
    dqs, dks, dvs = [], [], []
    for gi, d in enumerate(B_DILATIONS):
        dq_g, dk_g, dv_g = band_bwd(f"b_band{gi}_bwd", sv["qb"], gi * B_HEADS, sv["kb"], sv["bv"], dybb,
                                    sv["lse_b"], delta_b, d)
        dqs.append(dq_g)
        dks.append(dk_g)
        dvs.append(dv_g)
    dbq = rope("b_q_rope_bwd", jnp.concatenate(dqs, axis=1), tabs["b"][1], BF16)
    dbk = rope("b_k_rope_bwd", add3("b_dk_sum", *dks), tabs["b"][1], BF16)
    dbv = add3("b_dv_sum", *dvs, out_dtype=BF16)

    kg = A_HEADS // A_KV
    (dqa,), (dka,), dva = attn_bwd("a_attn_bwd", [(sv["qa"], 0, sv["ka"], kg)], sv["av"], kg, sv["lse_a"], dya,
                                   A_HEADS, HEAD ** -0.5)
    daq, g["a_q_norm"] = rmsnorm_bwd("a_q_norm_bwd", rope("a_q_rope_bwd", dqa, tabs["a"][1], F32), sv["aq"],
                                     p["a_q_norm"], HEAD, (BF16,))
    dak, g["a_k_norm"] = rmsnorm_bwd("a_k_norm_bwd", rope("a_k_rope_bwd", dka, tabs["a"][1], F32), sv["ak"],
                                     p["a_k_norm"], HEAD, (BF16,))

    dproj = jnp.concatenate([daq, dak, dva.astype(BF16), dbq, dbk, dbv, dcq, dckv, dckr,
                             jnp.zeros((s, PROJ_W - O_CKR - HEAD), BF16)], axis=1)
    dhn = matmul("in_proj_dx", dproj, p["w_in"], "nt", F32)
    g["w_in"] = matmul("in_proj_dw", sv["hn"], dproj, "tn", F32)
    dx, g["attn_norm"] = rmsnorm_bwd("attn_norm_bwd", dhn, sv["x"], p["attn_norm"], dm, (F32,), add=dx1)
    return dx, g


def _local_step(x, target, layers, final_norm, tabs):
    saved = []
    for p in layers:
        x, sv = _layer_fwd(x, p, tabs)
        saved.append(sv)
    dm = x.shape[1]
    yf = rmsnorm_fwd("final_norm", x, final_norm, dm, F32)
    loss, dyf = loss_head("loss_head", yf, target)
    dx, d_final = rmsnorm_bwd("final_norm_bwd", dyf, x, final_norm, dm, (F32,))
    grads = [None] * len(layers)
    for l in reversed(range(len(layers))):
        dx, grads[l] = _layer_bwd(dx, saved[l], layers[l], tabs)
    return loss, dx, grads, d_final


HBM = pl.BlockSpec(memory_space=pl.ANY)


def _place():
    x, y, c = lax.axis_index("x"), lax.axis_index("y"), lax.axis_index("c")
    others = [(1 - x, y), (x, 1 - y), (1 - x, 1 - y)]
    return x, y, c, 2 * x + y, others


def _remote(src, dst, send_sem, recv_sem, device):
    return pltpu.make_async_remote_copy(src_ref=src, dst_ref=dst, send_sem=send_sem, recv_sem=recv_sem,
                                        device_id=device, device_id_type=MESH)


def gather_small(name, small):
    def body(in_ref, out_ref, send_sems, recv_sems, local_sem):
        x, y, c, chip, others = _place()
        mine = pltpu.make_async_copy(in_ref, out_ref.at[chip], local_sem)
        mine.start()
        copies = []
        for k, (ox, oy) in enumerate(others):
            cp = _remote(in_ref, out_ref.at[chip], send_sems.at[k], recv_sems.at[k], (ox, oy, c))
            cp.start()
            copies.append(cp)
        for k, (ox, oy) in enumerate(others):
            landed = out_ref.at[2 * ox + oy]
            _remote(landed, landed, send_sems.at[k], recv_sems.at[k], (ox, oy, c)).wait_recv()
        for cp in copies:
            cp.wait_send()
        mine.wait()

    return pl.pallas_call(
        body, name=name, in_specs=[HBM], out_specs=HBM, out_shape=jax.ShapeDtypeStruct((4,) + small.shape, small.dtype),
        scratch_shapes=[pltpu.SemaphoreType.DMA((3,)), pltpu.SemaphoreType.DMA((3,)), pltpu.SemaphoreType.DMA],
        compiler_params=pltpu.CompilerParams(has_side_effects=True))(small)


IN_HBM = pl.BlockSpec(memory_space=pltpu.HBM)
SEMS = pl.BlockSpec(memory_space=pltpu.SEMAPHORE)
DATAFLOW = pltpu.SideEffectType.DATAFLOW_SIDE_EFFECTING


def _gather_plan(bufs):
    x, y, c, chip, others = _place()
    plan = []
    for t, ref in enumerate(bufs):
        half = ref.shape[1] // 2
        rows = pl.ds(c * half, half)
        for k, (ox, oy) in enumerate(others):
            plan.append((3 * t + k, (ox, oy, c), ref.at[chip, rows], ref.at[chip, rows], ref.at[2 * ox + oy, rows]))
    return plan


def _exchange_plan(bufs):
    x, y, c, chip, others = _place()
    n = len(bufs) // 2
    plan = []
    for t in range(n):
        for k, (ox, oy) in enumerate(others):
            plan.append((3 * t + k, (ox, oy, c), bufs[t].at[2 * ox + oy], bufs[n + t].at[chip], bufs[n + t].at[2 * ox + oy]))
    return plan


def split_start(name, bufs, plan_of, after):
    n = len(bufs)

    def body(*refs):
        ins = refs[:n]
        send_sems, recv_sems = refs[n + 1], refs[n + 2]
        token = refs[-1]
        for idx, peer, src, dst, _ in plan_of(ins):
            _remote(src, dst, send_sems.at[idx], recv_sems.at[idx], peer).start()
        token[...] = jnp.zeros_like(token)

    nsem = n if plan_of is _gather_plan else n // 2
    res = pl.pallas_call(
        body, name=name, in_specs=[IN_HBM] * n + [HBM],
        out_specs=(SEMS, SEMS) + (IN_HBM,) * n + (pl.BlockSpec(memory_space=pltpu.VMEM),),
        out_shape=(pltpu.SemaphoreType.DMA((3 * nsem,)), pltpu.SemaphoreType.DMA((3 * nsem,)))
        + tuple(pltpu.HBM(b.shape, b.dtype) for b in bufs) + (jax.ShapeDtypeStruct((8, LANES), F32),),
        input_output_aliases={t: 2 + t for t in range(n)},
        compiler_params=pltpu.CompilerParams(has_side_effects=DATAFLOW))(
            *[pltpu.with_memory_space_constraint(b, pltpu.HBM) for b in bufs], after)
    return res[0], res[1], list(res[2:2 + n]), res[-1]


def split_wait(name, send_sems, recv_sems, flying, plan_of, after):
    n = len(flying)

    def body(*refs):
        ins = refs[:n]
        send_ref, recv_ref = refs[n], refs[n + 1]
        for idx, peer, src, dst, landing in plan_of(ins):
            _remote(src, dst, send_ref.at[idx], recv_ref.at[idx], peer).wait_send()
            _remote(landing, landing, send_ref.at[idx], recv_ref.at[idx], peer).wait_recv()

    return list(pl.pallas_call(
        body, name=name, in_specs=[IN_HBM] * n + [SEMS, SEMS] + [HBM] * len(after), out_specs=(IN_HBM,) * n,
        out_shape=tuple(pltpu.HBM(b.shape, b.dtype) for b in flying),
        input_output_aliases={t: t for t in range(n)},
        compiler_params=pltpu.CompilerParams(has_side_effects=DATAFLOW))(*flying, send_sems, recv_sems, *after))


def forward_halves(name, slots):
    n = len(slots)

    def body(*refs):
        ins, outs = refs[:n], refs[n:2 * n]
        send_sems, recv_sems = refs[2 * n:]
        x, y, c, chip, others = _place()
        sibling = (x, y, 1 - c)
        copies = []
        for t in range(n):
            half = slots[t].shape[1] // 2
            for k, (ox, oy) in enumerate(others):
                rows = pl.ds(c * half, half)
                cp = _remote(ins[t].at[2 * ox + oy, rows], outs[t].at[2 * ox + oy, rows], send_sems.at[t, k],
                             recv_sems.at[t, k], sibling)
                cp.start()
                copies.append(cp)
        for t in range(n):
            half = slots[t].shape[1] // 2
            for k, (ox, oy) in enumerate(others):
                theirs = outs[t].at[2 * ox + oy, pl.ds((1 - c) * half, half)]
                _remote(theirs, theirs, send_sems.at[t, k], recv_sems.at[t, k], sibling).wait_recv()
        for cp in copies:
            cp.wait_send()

    return list(pl.pallas_call(
        body, name=name, in_specs=[HBM] * n, out_specs=[HBM] * n,
        out_shape=[jax.ShapeDtypeStruct(s.shape, s.dtype) for s in slots],
        input_output_aliases={t: t for t in range(n)},
        scratch_shapes=[pltpu.SemaphoreType.DMA((n, 3))] * 2,
        compiler_params=pltpu.CompilerParams(has_side_effects=True))(*slots))


def sibling_halves(name, grads):
    n = len(grads)

    def body(*refs):
        ins, outs = refs[:n], refs[n:2 * n]
        send_sems, recv_sems = refs[2 * n:]
        x, y, c, _, _ = _place()
        copies = []
        for t in range(n):
            half = grads[t].shape[1] // 2
            cp = _remote(ins[t].at[:, pl.ds((1 - c) * half, half), :], outs[t], send_sems.at[t], recv_sems.at[t],
                         (x, y, 1 - c))
            cp.start()
            copies.append(cp)
        for cp in copies:
            cp.wait()

    return pl.pallas_call(
        body, name=name, in_specs=[HBM] * n, out_specs=[HBM] * n,
        out_shape=[jax.ShapeDtypeStruct((4, g.shape[1] // 2, g.shape[2]), g.dtype) for g in grads],
        scratch_shapes=[pltpu.SemaphoreType.DMA((n,))] * 2,
        compiler_params=pltpu.CompilerParams(has_side_effects=True))(*grads)


def add_half(name, place, grad, theirs):
    _, r, cols = grad.shape
    half = r // 2
    tr = _rows(half, cols)

    def body(place_ref, g_ref, t_ref, o_ref):
        o_ref[...] = (g_ref[...] + t_ref[...]).astype(o_ref.dtype)

    return pl.pallas_call(
        body, name=name,
        grid_spec=pltpu.PrefetchScalarGridSpec(
            num_scalar_prefetch=1, grid=(4, half // tr),
            in_specs=[pl.BlockSpec((None, None, tr, cols), lambda j, i, pr: (j, pr[0], i, 0)),
                      pl.BlockSpec((None, tr, cols), lambda j, i, pr: (j, i, 0))],
            out_specs=pl.BlockSpec((None, tr, cols), lambda j, i, pr: (j, i, 0))),
        out_shape=jax.ShapeDtypeStruct((4, half, cols), BF16),
        compiler_params=_params("parallel", "parallel"))(place, grad.reshape(4, 2, half, cols), theirs)


def sum_chips(name, place, own, parts):
    _, half, cols = parts.shape
    tr = _rows(half, cols)

    def body(place_ref, own_ref, p1, p2, p3, o_ref):
        o_ref[...] = own_ref[...].astype(F32) + p1[...].astype(F32) + p2[...].astype(F32) + p3[...].astype(F32)

    def slot(k):
        return pl.BlockSpec((None, tr, cols), lambda i, pr: (lax.rem(pr[1] + k, 4), i, 0))

    return pl.pallas_call(
        body, name=name,
        grid_spec=pltpu.PrefetchScalarGridSpec(
            num_scalar_prefetch=1, grid=(half // tr,), in_specs=[slot(0), slot(1), slot(2), slot(3)],
            out_specs=pl.BlockSpec((None, tr, cols), lambda i, pr: (pr[0], i, 0))),
        out_shape=jax.ShapeDtypeStruct((2, half, cols), F32),
        compiler_params=_params("parallel"))(place, own, parts, parts, parts)


def join_halves(name, halves):
    n = len(halves)

    def body(*refs):
        ins, outs = refs[:n], refs[n:2 * n]
        send_sems, recv_sems = refs[2 * n:]
        x, y, c, _, _ = _place()
        copies = []
        for t in range(n):
            cp = _remote(ins[t].at[c], outs[t].at[c], send_sems.at[t], recv_sems.at[t], (x, y, 1 - c))
            cp.start()
            copies.append(cp)
        for t in range(n):
            theirs = outs[t].at[1 - c]
            _remote(theirs, theirs, send_sems.at[t], recv_sems.at[t], (x, y, 1 - c)).wait_recv()
        for cp in copies:
            cp.wait_send()

    return pl.pallas_call(
        body, name=name, in_specs=[HBM] * n, out_specs=[HBM] * n,
        out_shape=[jax.ShapeDtypeStruct(h.shape, h.dtype) for h in halves],
        input_output_aliases={t: t for t in range(n)},
        scratch_shapes=[pltpu.SemaphoreType.DMA((n,))] * 2,
        compiler_params=pltpu.CompilerParams(has_side_effects=True))(*halves)


def allreduce_small(name, buf):
    rows = buf.shape[0]

    def body(in_ref, out_ref, slots, send_sems, recv_sems):
        x, y, c, _, _ = _place()
        me = 4 * x + 2 * y + c
        slots[me] = in_ref[...]
        peers = []
        for k in range(1, 8):
            px = 1 - x if k & 4 else x
            py = 1 - y if k & 2 else y
            pc = 1 - c if k & 1 else c
            peers.append((px, py, pc))
        copies = []
        for k, peer in enumerate(peers):
            cp = _remote(in_ref, slots.at[me], send_sems.at[k], recv_sems.at[k], peer)
            cp.start()
            copies.append(cp)
        for k, (px, py, pc) in enumerate(peers):
            slot = slots.at[4 * px + 2 * py + pc]
            _remote(slot, slot, send_sems.at[k], recv_sems.at[k], (px, py, pc)).wait_recv()
        for cp in copies:
            cp.wait_send()
        acc = slots[0]
        for d in range(1, 8):
            acc = acc + slots[d]
        out_ref[...] = acc

    vm = pl.BlockSpec(memory_space=pltpu.VMEM)
    return pl.pallas_call(
        body, name=name, in_specs=[vm], out_specs=vm, out_shape=jax.ShapeDtypeStruct(buf.shape, F32),
        scratch_shapes=[pltpu.VMEM((8, rows, LANES), F32), pltpu.SemaphoreType.DMA((7,)), pltpu.SemaphoreType.DMA((7,))],
        compiler_params=pltpu.CompilerParams(has_side_effects=True, vmem_limit_bytes=VMEM_LIMIT))(buf)


def cast_layer(name, place, w, layer):
    _, r, cols = w.shape
    tr = _rows(r, cols)

    def body(place_ref, w_ref, o_ref):
        o_ref[...] = w_ref[...].astype(BF16)

    return pl.pallas_call(
        body, name=name,
        grid_spec=pltpu.PrefetchScalarGridSpec(
            num_scalar_prefetch=1, grid=(r // tr,),
            in_specs=[pl.BlockSpec((None, tr, cols), lambda i, pr: (layer, i, 0))],
            out_specs=pl.BlockSpec((None, tr, cols), lambda i, pr: (pr[1], i, 0))),
        out_shape=jax.ShapeDtypeStruct((4, r, cols), BF16),
        compiler_params=_params("parallel"))(place, w)


def _adamw_math(g, w, m, v):
    m = ADAM_B1 * m + (1.0 - ADAM_B1) * g
    v = ADAM_B2 * v + (1.0 - ADAM_B2) * (g * g)
    m_hat = m / (1.0 - ADAM_B1 ** ADAM_STEP)
    v_hat = v / (1.0 - ADAM_B2 ** ADAM_STEP)
    delta = -ADAM_LR * (m_hat / (jnp.sqrt(v_hat) + ADAM_EPS) + ADAM_WD * w)
    return delta, m, v


def adamw_layer(name, layer, g, w, m, v, prev=None):
    nl, r, cols = w.shape
    tr = _rows(r, cols, 8)

    def body(*refs):
        g_ref, w_ref, m_ref, v_ref = refs[:4]
        og, od, om, ov = refs[-4:]
        gv = g_ref[...]
        delta, m2, v2 = _adamw_math(gv, w_ref[...], m_ref[...], v_ref[...])
        og[...] = gv
        od[...] = delta
        om[...] = m2
        ov[...] = v2

    lay = pl.BlockSpec((None, tr, cols), lambda i: (layer, i, 0))
    ins = [g, w, m, v] + (list(prev) if prev is not None else [])
    return pl.pallas_call(
        body, name=name, grid=(r // tr,),
        in_specs=[pl.BlockSpec((tr, cols), lambda i: (i, 0)), lay, lay, lay] + ([HBM] * 4 if prev is not None else []),
        out_specs=[lay] * 4, out_shape=[jax.ShapeDtypeStruct((nl, r, cols), F32)] * 4,
        input_output_aliases={4 + k: k for k in range(4)} if prev is not None else {},
        compiler_params=_params("parallel"))(*ins)


def adamw_packed(name, g, w, m, v):
    def body(g_ref, w_ref, m_ref, v_ref, od, om, ov):
        delta, m2, v2 = _adamw_math(g_ref[...], w_ref[...], m_ref[...], v_ref[...])
        od[...] = delta
        om[...] = m2
        ov[...] = v2

    vm = pl.BlockSpec(memory_space=pltpu.VMEM)
    return pl.pallas_call(
        body, name=name, in_specs=[vm] * 4, out_specs=[vm] * 3, out_shape=[jax.ShapeDtypeStruct(g.shape, F32)] * 3,
        compiler_params=pltpu.CompilerParams(vmem_limit_bytes=VMEM_LIMIT))(g, w, m, v)


def _pack(arrays):
    flat = jnp.concatenate([a.reshape(-1) for a in arrays])
    pad = (-flat.shape[0]) % (8 * LANES)
    return jnp.pad(flat, (0, pad)).reshape(-1, LANES)


def _unpack(buf, shapes):
    flat = buf.reshape(-1)
    out, off = [], 0
    for shp in shapes:
        size = int(np.prod(shp))
        out.append(flat[off:off + size].reshape(shp))
        off += size
    return out


MIXER_W = ("w_in", "w_uq", "w_ukv", "w_out")
FFN_W = ("w_up", "w_down")
BIG = MIXER_W + FFN_W
COLUMN_CUT = ("w_in", "w_uq", "w_ukv", "w_up")
SMALL = ("attn_norm", "a_q_norm", "a_k_norm", "c_q_norm", "c_kv_norm", "out_norm", "ffn_norm", "conv_b", "final_norm")
WEIGHTS = ("attn_norm", "w_in", "a_q_norm", "a_k_norm", "c_q_norm", "c_kv_norm", "w_uq", "w_ukv", "out_norm", "w_out",
           "ffn_norm", "w_up", "conv_w", "conv_b", "w_down", "final_norm")
INPUTS = ("x",) + WEIGHTS + ("loss_target",) + tuple("m_" + n for n in WEIGHTS) + tuple("v_" + n for n in WEIGHTS)


def _columns(g):
    return jnp.transpose(g, (1, 0, 2)).reshape(g.shape[1], 4 * g.shape[2])


def _uncolumns(w):
    r, c4 = w.shape
    return jnp.transpose(w.reshape(r, 4, c4 // 4), (1, 0, 2))


def _compute_layouts(full, conv_w, small, layer):
    p = {n: small[n][layer] for n in SMALL if n != "final_norm"}
    if "w_in" in full:
        w_in = _columns(full["w_in"])
        p["w_in"] = jnp.pad(w_in, ((0, 0), (0, PROJ_W - IN_W)))
        w_uq = _columns(full["w_uq"]).reshape(C_RANK, C_HEADS, HEAD + C_ROPE)
        p["w_uq"] = jnp.concatenate([w_uq[:, :, :HEAD].reshape(C_RANK, C_W),
                                     jnp.pad(w_uq[:, :, HEAD:], ((0, 0), (0, 0), (0, HEAD - C_ROPE))).reshape(C_RANK, C_W)], axis=1)
        p["w_ukv"] = _columns(full["w_ukv"])
        p["w_out"] = full["w_out"].reshape(-1, full["w_out"].shape[2])
    if "w_up" in full:
        up = full["w_up"]
        p["w_up"] = up.reshape(2, 2, up.shape[1], up.shape[2])
        p["w_down"] = full["w_down"].reshape(-1, full["w_down"].shape[2])
        fh = conv_w.shape[1] // 2
        taps = jnp.transpose(conv_w.reshape(3, 2, fh), (1, 0, 2))
        p["cwb"] = jnp.concatenate([taps, small["conv_b"][layer].reshape(2, 1, fh), jnp.zeros((2, 4, fh), F32)], axis=1)
    return p


def _shard_layouts(g):
    out = {}
    out["w_in"] = _uncolumns(g["w_in"][:, :IN_W])
    uq = g["w_uq"]
    uq = jnp.concatenate([uq[:, :C_W].reshape(C_RANK, C_HEADS, HEAD),
                          uq[:, C_W:].reshape(C_RANK, C_HEADS, HEAD)[:, :, :C_ROPE]], axis=2)
    out["w_uq"] = _uncolumns(uq.reshape(C_RANK, C_HEADS * (HEAD + C_ROPE)))
    out["w_ukv"] = _uncolumns(g["w_ukv"])
    out["w_out"] = g["w_out"].reshape(4, -1, g["w_out"].shape[1])
    up = g["w_up"]
    out["w_up"] = up.reshape(4, up.shape[2], up.shape[3])
    out["w_down"] = g["w_down"].reshape(4, -1, g["w_down"].shape[1])
    dcwb = g["cwb"]
    fh = dcwb.shape[2]
    conv_w = jnp.transpose(dcwb[:, 0:3, :], (1, 0, 2)).reshape(3, 2 * fh)
    conv_b = dcwb[:, 3, :].reshape(2 * fh)
    return out, conv_w, conv_b


def kernel(x, attn_norm, w_in, a_q_norm, a_k_norm, c_q_norm, c_kv_norm, w_uq, w_ukv, out_norm, w_out, ffn_norm, w_up, conv_w, conv_b, w_down, final_norm, loss_target, m_attn_norm, m_w_in, m_a_q_norm, m_a_k_norm, m_c_q_norm, m_c_kv_norm, m_w_uq, m_w_ukv, m_out_norm, m_w_out, m_ffn_norm, m_w_up, m_conv_w, m_conv_b, m_w_down, m_final_norm, v_attn_norm, v_w_in, v_a_q_norm, v_a_k_norm, v_c_q_norm, v_c_kv_norm, v_w_uq, v_w_ukv, v_out_norm, v_w_out, v_ffn_norm, v_w_up, v_conv_w, v_conv_b, v_w_down, v_final_norm):
    a = dict(zip(INPUTS, (x, attn_norm, w_in, a_q_norm, a_k_norm, c_q_norm, c_kv_norm, w_uq, w_ukv, out_norm, w_out, ffn_norm, w_up, conv_w, conv_b, w_down, final_norm, loss_target, m_attn_norm, m_w_in, m_a_q_norm, m_a_k_norm, m_c_q_norm, m_c_kv_norm, m_w_uq, m_w_ukv, m_out_norm, m_w_out, m_ffn_norm, m_w_up, m_conv_w, m_conv_b, m_w_down, m_final_norm, v_attn_norm, v_w_in, v_a_q_norm, v_a_k_norm, v_c_q_norm, v_c_kv_norm, v_w_uq, v_w_ukv, v_out_norm, v_w_out, v_ffn_norm, v_w_up, v_conv_w, v_conv_b, v_w_down, v_final_norm)))
    nl = w_in.shape[0]
    seq = x.shape[1]
    chip = 2 * lax.axis_index("x") + lax.axis_index("y")
    place = jnp.stack([lax.axis_index("c"), chip]).astype(jnp.int32)
    tabs = _rope_tables(seq)

    assert nl == 2
    quarter = conv_w.shape[2]
    taps = jnp.pad(conv_w, ((0, 0), (0, 8 - conv_w.shape[1]), (0, 0))).reshape(nl * 8, quarter)
    taps_full = gather_small("gather_taps", taps).reshape(4, nl, 8, quarter)
    conv_full = [jnp.transpose(taps_full[:, l, 0:3, :], (1, 0, 2)).reshape(3, 4 * quarter) for l in range(nl)]
    flights = {}
    after = taps_full
    for key, l, names in (("mix0", 0, MIXER_W), ("ffn0", 0, FFN_W), ("all1", 1, BIG)):
        slots = [cast_layer(f"cast{l}_{n}", place, a[n], l) for n in names]
        send_sems, recv_sems, flying, after = split_start(f"gather_{key}_start", slots, _gather_plan, after)
        flights[key] = (names, send_sems, recv_sems, flying)
    started = after[0, 0]

    def land(key, after):
        names, send_sems, recv_sems, flying = flights[key]
        landed = split_wait(f"gather_{key}_wait", send_sems, recv_sems, flying, _gather_plan, after)
        return dict(zip(names, forward_halves(f"gather_{key}_forward", landed)))

    h = x[0]
    layers, saved = [], []
    for l in range(nl):
        if l == 0:
            p = _compute_layouts(land("mix0", [after]), conv_full[l], a, l)
            p["attn_norm"] = p["attn_norm"] + started
            x1, sv = _mixers_fwd(h, p, tabs)
            p.update(_compute_layouts(land("ffn0", [x1]), conv_full[l], a, l))
        else:
            p = _compute_layouts(land("all1", [h]), conv_full[l], a, l)
            x1, sv = _mixers_fwd(h, p, tabs)
        h = _ffn_fwd(x1, p, sv)
        layers.append(p)
        saved.append(sv)
    dm = h.shape[1]
    yf = rmsnorm_fwd("final_norm", h, final_norm, dm, F32)
    loss, dyf = loss_head("loss_head", yf, loss_target[0])
    loss = lax.psum(loss, ("x", "y", "c"))
    dx, d_final = rmsnorm_bwd("final_norm_bwd", dyf, h, final_norm, dm, (F32,))

    def reduce_begin(l, g):
        by_shard, d_conv_w, d_conv_b = _shard_layouts(g)
        g_list = [by_shard[n] for n in BIG]
        theirs = sibling_halves(f"reduce{l}_sibling", g_list)
        sums = [add_half(f"reduce{l}_add_{n}", place, gr, t) for n, gr, t in zip(BIG, g_list, theirs)]
        parts = [lax.empty(s.shape, s.dtype) for s in sums]
        send_sems, recv_sems, flying, token = split_start(f"reduce{l}_chips_start", sums + parts, _exchange_plan, sums[0])
        return (send_sems, recv_sems, flying, token), (d_conv_w, d_conv_b)

    def reduce_end(l, flight, after):
        send_sems, recv_sems, flying, _ = flight
        landed = split_wait(f"reduce{l}_chips_wait", send_sems, recv_sems, flying, _exchange_plan, after)
        sums, parts = landed[:len(BIG)], landed[len(BIG):]
        halves = [sum_chips(f"reduce{l}_sum_{n}", place, s, p) for n, s, p in zip(BIG, sums, parts)]
        joined = join_halves(f"reduce{l}_join", halves)
        return {n: j.reshape(2 * j.shape[1], j.shape[2]) for n, j in zip(BIG, joined)}

    dx, g1 = _layer_bwd(dx, saved[1], layers[1], tabs)
    flight1, conv1 = reduce_begin(1, g1)
    grad_x, g0 = _layer_bwd(dx, saved[0], layers[0], tabs, after=flight1[3][0, 0])
    flight0, conv0 = reduce_begin(0, g0)
    grads, conv_grads = [g0, g1], [conv0, conv1]

    out = {}
    reduced1 = reduce_end(1, flight1, [flight0[3]])
    for n in BIG:
        out[n] = adamw_layer(f"adamw1_{n}", 1, reduced1[n], a[n], a["m_" + n], a["v_" + n])

    small_g = {n: jnp.stack([grads[l][n] for l in range(nl)]) for n in SMALL if n not in ("conv_b", "final_norm")}
    small_g["conv_b"] = jnp.stack([cg[1] for cg in conv_grads])
    small_g["final_norm"] = d_final
    conv_w_g = jnp.stack([cg[0] for cg in conv_grads])
    shapes = [a[n].shape for n in SMALL] + [conv_w_g.shape]
    summed = _unpack(allreduce_small("reduce_small", _pack([small_g[n] for n in SMALL] + [conv_w_g])), shapes)
    small_g = dict(zip(SMALL, summed[:-1]))
    small_g["conv_w"] = lax.dynamic_slice_in_dim(summed[-1], chip * quarter, quarter, axis=2)
    names = SMALL + ("conv_w",)
    shapes = [a[n].shape for n in names]
    d_small, m_small, v_small = adamw_packed(
        "adamw_small", _pack([small_g[n] for n in names]), _pack([a[n] for n in names]),
        _pack([a["m_" + n] for n in names]), _pack([a["v_" + n] for n in names]))

    reduced0 = reduce_end(0, flight0, [out[n][3] for n in BIG] + [d_small])
    for n in BIG:
        out[n] = adamw_layer(f"adamw0_{n}", 0, reduced0[n], a[n], a["m_" + n], a["v_" + n], prev=out[n])
    for n, d_, m_, v_ in zip(names, _unpack(d_small, shapes), _unpack(m_small, shapes), _unpack(v_small, shapes)):
        out[n] = (small_g[n], d_, m_, v_)

    return (loss, grad_x[None]) + tuple(out[n][k] for k in range(4) for n in WEIGHTS)
```

```python
import functools

import jax
import jax.numpy as jnp
import numpy as np
from jax import lax
from jax.experimental import pallas as pl
from jax.experimental.pallas import tpu as pltpu

F32 = jnp.float32
BF16 = jnp.bfloat16
MESH = pl.DeviceIdType.MESH

V7X_VMEM_BYTES = 64 * 1024 * 1024
VMEM_LIMIT = V7X_VMEM_BYTES - 6 * 1024 * 1024
LANES = 128
ELEMENTWISE_BLOCK_BYTES = 2 * 1024 * 1024

HEAD = 128
A_HEADS, A_KV = 6, 2
B_HEADS, B_GROUPS = 4, 3
B_DILATIONS = (1, 4, 16)
B_HALF = 64
C_HEADS, C_RANK, C_ROPE = 6, 512, 64
GRID_W = 64
A_THETA, B_THETA, C_THETA = 10000.0, 500000.0, 10000.0
B_ROPE_DIM = 32
EPS = 1e-6
A_W, B_W, C_W = A_HEADS * HEAD, B_HEADS * HEAD, C_HEADS * HEAD
O_AQ, O_AK, O_AV = 0, 768, 1024
O_BQ, O_BK, O_BV = 1280, 2816, 3328
O_CQ, O_CKV, O_CKR = 3840, 4352, 4864
IN_W = 4928
PROJ_W = 5120

ADAM_LR, ADAM_B1, ADAM_B2, ADAM_EPS, ADAM_WD, ADAM_STEP = 0.001, 0.9, 0.999, 1e-08, 0.01, 10

NN = (((1,), (0,)), ((), ()))
NT = (((1,), (1,)), ((), ()))
TN = (((0,), (0,)), ((), ()))


def _dot(a, b, dims=NN):
    return lax.dot_general(a, b, dims, preferred_element_type=F32)


def _params(*sem):
    return pltpu.CompilerParams(dimension_semantics=sem if sem else None, vmem_limit_bytes=VMEM_LIMIT)


def _tile(n, target, unit=LANES):
    if n <= target:
        return n
    best = 0
    for t in range(unit, target + 1, unit):
        if n % t == 0:
            best = t
    assert best, (n, target)
    return best


def _rows(r, width, itemsize=4):
    return _tile(r, max(8, ELEMENTWISE_BLOCK_BYTES // (width * itemsize)), 8)


def _matmul_call(name, dims, grid, a_spec, b_spec, o_spec, out_shape, acc_shape, add_spec=None):
    nk = grid[2]

    def body(*refs):
        a_ref, b_ref = refs[:2]
        add_ref = None if add_spec is None else refs[2]
        o_ref = refs[2 if add_spec is None else 3]

        def finish(r):
            if add_ref is not None:
                r = r + add_ref[...]
            o_ref[...] = r.astype(o_ref.dtype)

        if nk == 1:
            finish(_dot(a_ref[...], b_ref[...], dims))
            return
        acc = refs[-1]
        k = pl.program_id(2)

        @pl.when(k == 0)
        def _():
            acc[...] = _dot(a_ref[...], b_ref[...], dims)

        if nk > 2:
            @pl.when((k > 0) & (k < nk - 1))
            def _():
                acc[...] += _dot(a_ref[...], b_ref[...], dims)

        @pl.when(k == nk - 1)
        def _():
            finish(acc[...] + _dot(a_ref[...], b_ref[...], dims))

    in_specs = [a_spec, b_spec] + ([] if add_spec is None else [add_spec])
    return pl.pallas_call(
        body, name=name, grid=grid, in_specs=in_specs, out_specs=o_spec, out_shape=out_shape,
        scratch_shapes=[pltpu.VMEM(acc_shape, F32)] if nk > 1 else [],
        compiler_params=_params("parallel", "parallel", "arbitrary"))


def matmul(name, a, b, mode, out_dtype, add=None, tm=1024, tn=1024, tk=2816):
    if mode == "nn":
        (m, k), (k2, n) = a.shape, b.shape
    elif mode == "nt":
        (m, k), (n, k2) = a.shape, b.shape
    else:
        (k, m), (k2, n) = a.shape, b.shape
    assert k == k2, (name, a.shape, b.shape)
    tm, tn, tk = _tile(m, tm), _tile(n, tn), _tile(k, tk)
    grid = (m // tm, n // tn, k // tk)
    if mode == "tn":
        a_spec = pl.BlockSpec((tk, tm), lambda i, j, kk: (kk, i))
    else:
        a_spec = pl.BlockSpec((tm, tk), lambda i, j, kk: (i, kk))
    if mode == "nt":
        b_spec = pl.BlockSpec((tn, tk), lambda i, j, kk: (j, kk))
    else:
        b_spec = pl.BlockSpec((tk, tn), lambda i, j, kk: (kk, j))
    o_spec = pl.BlockSpec((tm, tn), lambda i, j, kk: (i, j))
    dims = {"nn": NN, "nt": NT, "tn": TN}[mode]
    call = _matmul_call(name, dims, grid, a_spec, b_spec, o_spec, jax.ShapeDtypeStruct((m, n), out_dtype),
                        (tm, tn), None if add is None else o_spec)
    return call(a, b) if add is None else call(a, b, add)


def rmsnorm_fwd(name, x, g, width, out_dtype):
    r, cols = x.shape
    nb = cols // width
    tr = _rows(r, width)

    def body(x_ref, g_ref, o_ref):
        xv = x_ref[...]
        rs = lax.rsqrt(jnp.mean(xv * xv, axis=-1, keepdims=True) + EPS)
        o_ref[...] = (xv * rs * g_ref[...]).astype(o_ref.dtype)

    blk = pl.BlockSpec((tr, width), lambda i, j: (i, j))
    return pl.pallas_call(
        body, name=name, grid=(r // tr, nb),
        in_specs=[blk, pl.BlockSpec((1, width), lambda i, j: (0, 0))], out_specs=blk,
        out_shape=jax.ShapeDtypeStruct((r, cols), out_dtype),
        compiler_params=_params("parallel", "parallel"))(x, g.reshape(1, width))


def rmsnorm_bwd(name, dy, x, g, width, out_dtypes=(F32,), add=None):
    r, cols = x.shape
    nb = cols // width
    tr = _rows(r, width)
    n_out = len(out_dtypes)

    def body(*refs):
        dy_ref, x_ref, g_ref = refs[:3]
        add_ref = refs[3] if add is not None else None
        outs = refs[-(n_out + 1):-1]
        dg_ref = refs[-1]
        xv = x_ref[...]
        dyv = dy_ref[...].astype(F32)
        rs = lax.rsqrt(jnp.mean(xv * xv, axis=-1, keepdims=True) + EPS)
        xh = xv * rs
        dyg = dyv * g_ref[...]
        dx = rs * (dyg - xh * jnp.mean(dyg * xh, axis=-1, keepdims=True))
        if add_ref is not None:
            dx = dx + add_ref[...]
        for o in outs:
            o[...] = dx.astype(o.dtype)

        @pl.when((pl.program_id(0) == 0) & (pl.program_id(1) == 0))
        def _():
            dg_ref[...] = jnp.zeros_like(dg_ref)

        dg_ref[...] += jnp.sum(dyv * xh, axis=0, keepdims=True)

    blk = pl.BlockSpec((tr, width), lambda i, j: (i, j))
    vec = pl.BlockSpec((1, width), lambda i, j: (0, 0))
    ins = [dy, x, g.reshape(1, width)] + ([add] if add is not None else [])
    res = pl.pallas_call(
        body, name=name, grid=(r // tr, nb),
        in_specs=[blk, blk, vec] + ([blk] if add is not None else []),
        out_specs=[blk] * n_out + [vec],
        out_shape=[jax.ShapeDtypeStruct((r, cols), dt) for dt in out_dtypes] + [jax.ShapeDtypeStruct((1, width), F32)],
        compiler_params=_params("arbitrary", "arbitrary"))(*ins)
    return tuple(res[:n_out]) + (res[n_out].reshape(width),)


def _rope_angles(pos, dim, theta):
    inv = theta ** (-jnp.arange(0, dim, 2, dtype=F32) / dim)
    ang = pos.astype(F32)[:, None] * inv[None, :]
    return jnp.cos(ang), jnp.sin(ang)


def _rope_tables(s):
    rows = s // GRID_W
    row_pos = jnp.repeat(jnp.arange(rows), GRID_W)
    col_pos = jnp.tile(jnp.arange(GRID_W), rows)
    t_pos = jnp.arange(s)
    z = lambda n: jnp.zeros((s, n), F32)
    o = lambda n: jnp.ones((s, n), F32)
    cr, sr = _rope_angles(row_pos, HEAD // 2, A_THETA)
    cc, sc = _rope_angles(col_pos, HEAD // 2, A_THETA)
    tab_a = (jnp.concatenate([cr, cr, cc, cc], 1), jnp.concatenate([-sr, z(32), -sc, z(32)], 1),
             jnp.concatenate([z(32), sr, z(32), sc], 1), 32)
    cp, sp = _rope_angles(t_pos, B_ROPE_DIM, B_THETA)
    tab_b = (jnp.concatenate([cp, cp, o(96)], 1), jnp.concatenate([-sp, z(112)], 1),
             jnp.concatenate([z(16), sp, z(96)], 1), 16)
    cm, sm = _rope_angles(t_pos, C_ROPE, C_THETA)
    tab_c = (jnp.concatenate([cm, cm, o(64)], 1), jnp.concatenate([-sm, z(96)], 1),
             jnp.concatenate([z(32), sm, z(64)], 1), 32)

    def transposed(tab):
        c, s1, s2, h = tab
        return (c, jnp.roll(s2, -h, axis=1), jnp.roll(s1, h, axis=1), h)

    return {k: (t, transposed(t)) for k, t in (("a", tab_a), ("b", tab_b), ("c", tab_c))}


def rope(name, x, tab, out_dtype):
    c, s1, s2, h = tab
    s, cols = x.shape
    nh = cols // HEAD
    ts = _rows(s, HEAD)

    def body(x_ref, c_ref, s1_ref, s2_ref, o_ref):
        xv = x_ref[...].astype(F32)
        out = xv * c_ref[...] + pltpu.roll(xv, HEAD - h, 1) * s1_ref[...] + pltpu.roll(xv, h, 1) * s2_ref[...]
        o_ref[...] = out.astype(o_ref.dtype)

    blk = pl.BlockSpec((ts, HEAD), lambda i, j: (i, j))
    tb = pl.BlockSpec((ts, HEAD), lambda i, j: (i, 0))
    return pl.pallas_call(
        body, name=name, grid=(s // ts, nh), in_specs=[blk, tb, tb, tb], out_specs=blk,
        out_shape=jax.ShapeDtypeStruct((s, cols), out_dtype),
        compiler_params=_params("parallel", "parallel"))(x, c, s1, s2)


ATTN_TQ = 256


def attn_fwd(name, parts, v, v_group, nheads, scale):
    s = v.shape[0]
    tq = ATTN_TQ
    npart = len(parts)

    def body(*refs):
        v_ref, o_ref, lse_ref = refs[2 * npart:]
        sc = None
        for p in range(npart):
            t = _dot(refs[2 * p][...], refs[2 * p + 1][...], NT)
            sc = t if sc is None else sc + t
        sc = sc * scale
        m = jnp.max(sc, axis=-1, keepdims=True)
        e = jnp.exp(sc - m)
        l = jnp.sum(e, axis=-1, keepdims=True)
        o_ref[...] = _dot((e / l).astype(BF16), v_ref[...])
        lse_ref[...] = jnp.broadcast_to(m + jnp.log(l), (tq, HEAD))

    in_specs, ins = [], []
    for q, qoff, k, kg in parts:
        in_specs.append(pl.BlockSpec((tq, HEAD), lambda h, i, qoff=qoff: (i, qoff + h)))
        in_specs.append(pl.BlockSpec((s, HEAD), lambda h, i, kg=kg: (0, h // kg)))
        ins += [q, k]
    in_specs.append(pl.BlockSpec((s, HEAD), lambda h, i: (0, h // v_group)))
    out_blk = pl.BlockSpec((tq, HEAD), lambda h, i: (i, h))
    return pl.pallas_call(
        body, name=name, grid=(nheads, s // tq), in_specs=in_specs, out_specs=[out_blk, out_blk],
        out_shape=[jax.ShapeDtypeStruct((s, nheads * HEAD), F32)] * 2,
        compiler_params=_params("parallel", "parallel"))(*ins, v)


def attn_bwd(name, parts, v, v_group, lse, do, nheads, scale):
    s = v.shape[0]
    tq = ATTN_TQ
    npart = len(parts)

    def body(*refs):
        v_ref, lse_ref, do_ref = refs[2 * npart:2 * npart + 3]
        outs = refs[2 * npart + 3:]
        dq_refs, dk_refs, dv_ref = outs[:npart], outs[npart:2 * npart], outs[2 * npart]
        h, i = pl.program_id(0), pl.program_id(1)
        sc = None
        for p in range(npart):
            t = _dot(refs[2 * p][...], refs[2 * p + 1][...], NT)
            sc = t if sc is None else sc + t
        pr = jnp.exp(sc * scale - lse_ref[:, 0:1])
        dov = do_ref[...]
        dp = _dot(dov, v_ref[...], NT)
        delta = jnp.sum(pr * dp, axis=-1, keepdims=True)
        ds = (pr * (dp - delta) * scale).astype(BF16)

        @pl.when((i == 0) & (h % v_group == 0))
        def _():
            dv_ref[...] = jnp.zeros_like(dv_ref)

        dv_ref[...] += _dot(pr.astype(BF16), dov, TN)
        for p in range(npart):
            kg = parts[p][3]
            dq_refs[p][...] = _dot(ds, refs[2 * p + 1][...])

            @pl.when((i == 0) & (h % kg == 0))
            def _(p=p):
                dk_refs[p][...] = jnp.zeros_like(dk_refs[p])

            dk_refs[p][...] += _dot(ds, refs[2 * p][...], TN)

    in_specs, ins = [], []
    for q, qoff, k, kg in parts:
        in_specs.append(pl.BlockSpec((tq, HEAD), lambda h, i, qoff=qoff: (i, qoff + h)))
        in_specs.append(pl.BlockSpec((s, HEAD), lambda h, i, kg=kg: (0, h // kg)))
        ins += [q, k]
    hq_blk = pl.BlockSpec((tq, HEAD), lambda h, i: (i, h))
    in_specs += [pl.BlockSpec((s, HEAD), lambda h, i: (0, h // v_group)), hq_blk, hq_blk]
    out_specs = [hq_blk] * npart
    out_shape = [jax.ShapeDtypeStruct((s, nheads * HEAD), F32)] * npart
    for q, qoff, k, kg in parts:
        out_specs.append(pl.BlockSpec((s, HEAD), lambda h, i, kg=kg: (0, h // kg)))
        out_shape.append(jax.ShapeDtypeStruct((s, nheads // kg * HEAD), F32))
    out_specs.append(pl.BlockSpec((s, HEAD), lambda h, i: (0, h // v_group)))
    out_shape.append(jax.ShapeDtypeStruct((s, nheads // v_group * HEAD), F32))
    res = pl.pallas_call(
        body, name=name, grid=(nheads, s // tq), in_specs=in_specs, out_specs=out_specs, out_shape=out_shape,
        compiler_params=_params("arbitrary", "arbitrary"))(*ins, v, lse, do)
    return list(res[:npart]), list(res[npart:2 * npart]), res[2 * npart]


def _band_windows(lf):
    for ib in range(lf // HEAD):
        q0 = ib * HEAD
        yield q0, max(0, q0 - B_HALF), min(lf, q0 + HEAD + B_HALF)


def _band_mask(q0, lo, hi):
    qpos = q0 + lax.broadcasted_iota(jnp.int32, (HEAD, hi - lo), 0)
    kpos = lo + lax.broadcasted_iota(jnp.int32, (HEAD, hi - lo), 1)
    return jnp.abs(qpos - kpos) <= B_HALF


def _band_specs(d, q_cols, q_off):
    s_q = pl.BlockSpec
    return s_q, (lambda r, h: (0, r * (q_cols // HEAD) + q_off + h)), (lambda r, h: (0, r * B_HEADS + h))


def band_fwd(name, q, q_off, k, v, d):
    s = k.shape[0]
    lf = s // d
    scale = HEAD ** -0.5

    def body(q_ref, k_ref, v_ref, o_ref, lse_ref):
        for q0, lo, hi in _band_windows(lf):
            sc = _dot(q_ref[q0:q0 + HEAD, :], k_ref[lo:hi, :], NT) * scale
            sc = jnp.where(_band_mask(q0, lo, hi), sc, -1e30)
            m = jnp.max(sc, axis=-1, keepdims=True)
            e = jnp.exp(sc - m)
            l = jnp.sum(e, axis=-1, keepdims=True)
            o_ref[q0:q0 + HEAD, :] = _dot((e / l).astype(BF16), v_ref[lo:hi, :])
            lse_ref[q0:q0 + HEAD, :] = jnp.broadcast_to(m + jnp.log(l), (HEAD, HEAD))

    _, qmap, kmap = _band_specs(d, q.shape[1], q_off)
    blk = lambda f: pl.BlockSpec((lf, HEAD), f)
    o, lse = pl.pallas_call(
        body, name=name, grid=(d, B_HEADS), in_specs=[blk(qmap), blk(kmap), blk(kmap)],
        out_specs=[blk(kmap), blk(kmap)], out_shape=[jax.ShapeDtypeStruct((lf, d * B_W), F32)] * 2,
        compiler_params=_params("parallel", "parallel"))(
            q.reshape(lf, d * q.shape[1]), k.reshape(lf, d * B_W), v.reshape(lf, d * B_W))
    return o.reshape(s, B_W), lse.reshape(s, B_W)


def band_bwd(name, q, q_off, k, v, dy, lse, delta, d):
    s = k.shape[0]
    lf = s // d
    scale = HEAD ** -0.5

    def body(q_ref, k_ref, v_ref, dy_ref, lse_ref, dl_ref, dq_ref, dk_ref, dv_ref):
        dk_ref[...] = jnp.zeros_like(dk_ref)
        dv_ref[...] = jnp.zeros_like(dv_ref)
        for q0, lo, hi in _band_windows(lf):
            qv, kv, vv = q_ref[q0:q0 + HEAD, :], k_ref[lo:hi, :], v_ref[lo:hi, :]
            dyv = dy_ref[q0:q0 + HEAD, :]
            sc = _dot(qv, kv, NT) * scale
            pr = jnp.where(_band_mask(q0, lo, hi), jnp.exp(sc - lse_ref[q0:q0 + HEAD, 0:1]), 0.0)
            dp = _dot(dyv, vv, NT)
            ds = (pr * (dp - dl_ref[q0:q0 + HEAD, 0:1]) * scale).astype(BF16)
            dv_ref[lo:hi, :] += _dot(pr.astype(BF16), dyv, TN)
            dq_ref[q0:q0 + HEAD, :] = _dot(ds, kv)
            dk_ref[lo:hi, :] += _dot(ds, qv, TN)

    _, qmap, kmap = _band_specs(d, q.shape[1], q_off)
    blk = lambda f: pl.BlockSpec((lf, HEAD), f)
    fold = lambda a: a.reshape(lf, d * a.shape[1])
    res = pl.pallas_call(
        body, name=name, grid=(d, B_HEADS), in_specs=[blk(qmap)] + [blk(kmap)] * 5,
        out_specs=[blk(kmap)] * 3, out_shape=[jax.ShapeDtypeStruct((lf, d * B_W), F32)] * 3,
        compiler_params=_params("parallel", "parallel"))(fold(q), fold(k), fold(v), fold(dy), fold(lse), fold(delta))
    return tuple(r.reshape(s, B_W) for r in res)


def mix_fwd(name, outs, lses):
    s, w = outs[0].shape
    ts = _rows(s, w)

    def body(o0, o1, o2, l0, l1, l2, y_ref, lse_ref):
        a, b, c = l0[...], l1[...], l2[...]
        m = jnp.maximum(jnp.maximum(a, b), c)
        ea, eb, ec = jnp.exp(a - m), jnp.exp(b - m), jnp.exp(c - m)
        den = ea + eb + ec
        y_ref[...] = (ea / den) * o0[...] + (eb / den) * o1[...] + (ec / den) * o2[...]
        lse_ref[...] = m + jnp.log(den)

    blk = pl.BlockSpec((ts, w), lambda i: (i, 0))
    return pl.pallas_call(
        body, name=name, grid=(s // ts,), in_specs=[blk] * 6, out_specs=[blk, blk],
        out_shape=[jax.ShapeDtypeStruct((s, w), F32)] * 2, compiler_params=_params("parallel"))(*outs, *lses)


def head_rowdot(name, dy, y):
    s, cols = y.shape
    ts = _rows(s, HEAD)

    def body(dy_ref, y_ref, dl_ref, dyb_ref):
        dyv = dy_ref[...]
        dl_ref[...] = jnp.broadcast_to(jnp.sum(dyv * y_ref[...], axis=-1, keepdims=True), (ts, HEAD))
        dyb_ref[...] = dyv.astype(BF16)

    blk = pl.BlockSpec((ts, HEAD), lambda i, j: (i, j))
    return pl.pallas_call(
        body, name=name, grid=(s // ts, cols // HEAD), in_specs=[blk, blk], out_specs=[blk, blk],
        out_shape=[jax.ShapeDtypeStruct((s, cols), F32), jax.ShapeDtypeStruct((s, cols), BF16)],
        compiler_params=_params("parallel", "parallel"))(dy, y)


def add3(name, a, b, c, out_dtype=F32):
    s, w = a.shape
    ts = _rows(s, w)

    def body(a_ref, b_ref, c_ref, o_ref):
        o_ref[...] = (a_ref[...] + b_ref[...] + c_ref[...]).astype(o_ref.dtype)

    blk = pl.BlockSpec((ts, w), lambda i: (i, 0))
    return pl.pallas_call(
        body, name=name, grid=(s // ts,), in_specs=[blk] * 3, out_specs=blk,
        out_shape=jax.ShapeDtypeStruct((s, w), out_dtype), compiler_params=_params("parallel"))(a, b, c)


FFN_TN = 256


def _conv(h, cw):
    s = h.shape[0]
    row = lax.broadcasted_iota(jnp.int32, h.shape, 0)
    prev = jnp.where(row == 0, 0.0, pltpu.roll(h, 1, 0))
    nxt = jnp.where(row == s - 1, 0.0, pltpu.roll(h, s - 1, 0))
    return prev * cw[0:1, :] + h * cw[1:2, :] + nxt * cw[2:3, :] + cw[3:4, :], prev, nxt


def _sigmoid(x):
    return 1.0 / (1.0 + jnp.exp(-x))


def ffn_up(name, xn, w_up, cwb):
    s, dm = xn.shape
    quarter = w_up.shape[3]
    fh = 2 * quarter
    tn = FFN_TN
    per = quarter // tn

    def body(x_ref, w_ref, cw_ref, h_ref, act_ref):
        xv = x_ref[...]
        hg = _dot(xv, w_ref[0])
        hu = _dot(xv, w_ref[1])
        h_ref[0] = hg
        h_ref[1] = hu
        gc, _, _ = _conv(hg, cw_ref[0])
        uc, _, _ = _conv(hu, cw_ref[1])
        act_ref[...] = (gc * _sigmoid(gc) * uc).astype(BF16)

    return pl.pallas_call(
        body, name=name, grid=(fh // tn,),
        in_specs=[pl.BlockSpec((s, dm), lambda t: (0, 0)),
                  pl.BlockSpec((2, None, dm, tn), lambda t: (0, t // per, 0, t % per)),
                  pl.BlockSpec((2, 8, tn), lambda t: (0, 0, t))],
        out_specs=[pl.BlockSpec((2, s, tn), lambda t: (0, 0, t)), pl.BlockSpec((s, tn), lambda t: (0, t))],
        out_shape=[jax.ShapeDtypeStruct((2, s, fh), F32), jax.ShapeDtypeStruct((s, fh), BF16)],
        compiler_params=_params("parallel"))(xn, w_up, cwb)


def ffn_gate_bwd(name, h, dact, cwb):
    _, s, fh = h.shape
    tn = FFN_TN

    def body(h_ref, da_ref, cw_ref, dh_ref, dcw_ref):
        gc, gp, gn = _conv(h_ref[0], cw_ref[0])
        uc, up, un = _conv(h_ref[1], cw_ref[1])
        sg = _sigmoid(gc)
        da = da_ref[...]
        dgc = da * uc * (sg * (1.0 + gc * (1.0 - sg)))
        duc = da * (gc * sg)
        row = lax.broadcasted_iota(jnp.int32, da.shape, 0)
        for idx, (hv, prev, nxt, dc) in enumerate(((h_ref[0], gp, gn, dgc), (h_ref[1], up, un, duc))):
            cw = cw_ref[idx]
            from_next = jnp.where(row == s - 1, 0.0, pltpu.roll(dc, s - 1, 0))
            from_prev = jnp.where(row == 0, 0.0, pltpu.roll(dc, 1, 0))
            dh_ref[idx] = (from_next * cw[0:1, :] + dc * cw[1:2, :] + from_prev * cw[2:3, :]).astype(BF16)
            dcw_ref[idx, 0:1, :] = jnp.sum(prev * dc, axis=0, keepdims=True)
            dcw_ref[idx, 1:2, :] = jnp.sum(hv * dc, axis=0, keepdims=True)
            dcw_ref[idx, 2:3, :] = jnp.sum(nxt * dc, axis=0, keepdims=True)
            dcw_ref[idx, 3:4, :] = jnp.sum(dc, axis=0, keepdims=True)
            dcw_ref[idx, 4:8, :] = jnp.zeros((4, tn), F32)

    return pl.pallas_call(
        body, name=name, grid=(fh // tn,),
        in_specs=[pl.BlockSpec((2, s, tn), lambda t: (0, 0, t)), pl.BlockSpec((s, tn), lambda t: (0, t)),
                  pl.BlockSpec((2, 8, tn), lambda t: (0, 0, t))],
        out_specs=[pl.BlockSpec((2, s, tn), lambda t: (0, 0, t)), pl.BlockSpec((2, 8, tn), lambda t: (0, 0, t))],
        out_shape=[jax.ShapeDtypeStruct((2, s, fh), BF16), jax.ShapeDtypeStruct((2, 8, fh), F32)],
        compiler_params=_params("parallel"))(h, dact, cwb)


def ffn_dx(name, dh, w_up):
    _, s, fh = dh.shape
    dm, quarter = w_up.shape[2], w_up.shape[3]
    tk = _tile(quarter, 2816)
    per = quarter // tk
    tm, tn = _tile(s, 1024), _tile(dm, 1024)
    grid = (s // tm, dm // tn, 4 * per)
    a_spec = pl.BlockSpec((None, tm, tk), lambda i, j, k: (k // (2 * per), i, k % (2 * per)))
    b_spec = pl.BlockSpec((None, None, tn, tk), lambda i, j, k: (k // (2 * per), (k // per) % 2, j, k % per))
    o_spec = pl.BlockSpec((tm, tn), lambda i, j, k: (i, j))
    return _matmul_call(name, NT, grid, a_spec, b_spec, o_spec, jax.ShapeDtypeStruct((s, dm), F32), (tm, tn))(dh, w_up)


def ffn_dw_up(name, xn, dh):
    _, s, fh = dh.shape
    dm = xn.shape[1]
    quarter = fh // 2
    tn = _tile(quarter, 1536)
    per = quarter // tn
    tm, tk = _tile(dm, 1024), _tile(s, 2048)
    grid = (dm // tm, 4 * per, s // tk)
    a_spec = pl.BlockSpec((tk, tm), lambda i, j, k: (k, i))
    b_spec = pl.BlockSpec((None, tk, tn), lambda i, j, k: (j // (2 * per), k, j % (2 * per)))
    o_spec = pl.BlockSpec((None, None, tm, tn), lambda i, j, k: (j // (2 * per), (j // per) % 2, i, j % per))
    return _matmul_call(name, TN, grid, a_spec, b_spec, o_spec,
                        jax.ShapeDtypeStruct((2, 2, dm, quarter), F32), (tm, tn))(xn, dh)


def loss_head(name, y, target):
    s, dm = y.shape
    ts = _rows(s, dm)

    def body(y_ref, t_ref, loss_ref, dy_ref, acc):
        i = pl.program_id(0)

        @pl.when(i == 0)
        def _():
            acc[...] = jnp.zeros_like(acc)

        err = y_ref[...] - t_ref[...]
        dy_ref[...] = err * (1.0 / dm)
        acc[...] += jnp.sum(err * err, axis=0, keepdims=True)

        @pl.when(i == s // ts - 1)
        def _():
            loss_ref[...] = jnp.broadcast_to(jnp.sum(acc[...], axis=-1, keepdims=True) * (0.5 / dm), (1, LANES))

    blk = pl.BlockSpec((ts, dm), lambda i: (i, 0))
    loss, dy = pl.pallas_call(
        body, name=name, grid=(s // ts,), in_specs=[blk, blk],
        out_specs=[pl.BlockSpec((1, LANES), lambda i: (0, 0)), blk],
        out_shape=[jax.ShapeDtypeStruct((1, LANES), F32), jax.ShapeDtypeStruct((s, dm), F32)],
        scratch_shapes=[pltpu.VMEM((1, dm), F32)], compiler_params=_params("arbitrary"))(y, target)
    return loss[0, 0], dy


def attn_fwd_c(name, qc, kvc, krc, scale):
    s = qc.shape[0]
    tq = ATTN_TQ

    def body(qn_ref, qr_ref, kn_ref, kr_ref, v_ref, o_ref, lse_ref):
        sc = (_dot(qn_ref[...], kn_ref[...], NT) + _dot(qr_ref[...], kr_ref[...], NT)) * scale
        m = jnp.max(sc, axis=-1, keepdims=True)
        e = jnp.exp(sc - m)
        l = jnp.sum(e, axis=-1, keepdims=True)
        o_ref[...] = _dot((e / l).astype(BF16), v_ref[...])
        lse_ref[...] = jnp.broadcast_to(m + jnp.log(l), (tq, HEAD))

    qb = lambda off: pl.BlockSpec((tq, HEAD), lambda h, i: (i, off + h))
    kb = lambda f: pl.BlockSpec((s, HEAD), f)
    out_blk = pl.BlockSpec((tq, HEAD), lambda h, i: (i, h))
    return pl.pallas_call(
        body, name=name, grid=(C_HEADS, s // tq),
        in_specs=[qb(0), qb(C_HEADS), kb(lambda h, i: (0, 2 * h)), kb(lambda h, i: (0, 0)), kb(lambda h, i: (0, 2 * h + 1))],
        out_specs=[out_blk, out_blk], out_shape=[jax.ShapeDtypeStruct((s, C_W), F32)] * 2,
        compiler_params=_params("parallel", "parallel"))(qc, qc, kvc, krc, kvc)


def attn_bwd_c(name, qc, kvc, krc, lse, do, scale):
    s = qc.shape[0]
    tq = ATTN_TQ

    def body(qn_ref, qr_ref, kn_ref, kr_ref, v_ref, lse_ref, do_ref, dqn_ref, dqr_ref, dkn_ref, dv_ref, dkr_ref):
        h, i = pl.program_id(0), pl.program_id(1)
        sc = (_dot(qn_ref[...], kn_ref[...], NT) + _dot(qr_ref[...], kr_ref[...], NT)) * scale
        pr = jnp.exp(sc - lse_ref[:, 0:1])
        dov = do_ref[...]
        dp = _dot(dov, v_ref[...], NT)
        delta = jnp.sum(pr * dp, axis=-1, keepdims=True)
        ds = (pr * (dp - delta) * scale).astype(BF16)

        @pl.when(i == 0)
        def _():
            dv_ref[...] = jnp.zeros_like(dv_ref)
            dkn_ref[...] = jnp.zeros_like(dkn_ref)

        @pl.when((i == 0) & (h == 0))
        def _():
            dkr_ref[...] = jnp.zeros_like(dkr_ref)

        dv_ref[...] += _dot(pr.astype(BF16), dov, TN)
        dqn_ref[...] = _dot(ds, kn_ref[...])
        dqr_ref[...] = _dot(ds, kr_ref[...])
        dkn_ref[...] += _dot(ds, qn_ref[...], TN)
        dkr_ref[...] += _dot(ds, qr_ref[...], TN)

    qb = lambda off: pl.BlockSpec((tq, HEAD), lambda h, i: (i, off + h))
    kb = lambda f: pl.BlockSpec((s, HEAD), f)
    hq = pl.BlockSpec((tq, HEAD), lambda h, i: (i, h))
    kn_map, v_map, kr_map = (lambda h, i: (0, 2 * h)), (lambda h, i: (0, 2 * h + 1)), (lambda h, i: (0, 0))
    dqc_shape = jax.ShapeDtypeStruct((s, 2 * C_W), F32)
    dqn, dqr, dkn, dv, dkr = pl.pallas_call(
        body, name=name, grid=(C_HEADS, s // tq),
        in_specs=[qb(0), qb(C_HEADS), kb(kn_map), kb(kr_map), kb(v_map), hq, hq],
        out_specs=[hq, hq, kb(lambda h, i: (0, h)), kb(lambda h, i: (0, h)), kb(kr_map)],
        out_shape=[jax.ShapeDtypeStruct((s, C_W), F32)] * 4 + [jax.ShapeDtypeStruct((s, HEAD), F32)],
        compiler_params=_params("arbitrary", "arbitrary"))(qc, qc, kvc, krc, kvc, lse, do)
    del dqc_shape
    return dqn, dqr, dkn, dv, dkr


def _layer_fwd(x, p, tabs):
    x1, sv = _mixers_fwd(x, p, tabs)
    return _ffn_fwd(x1, p, sv), sv


def _mixers_fwd(x, p, tabs):
    sv = {"x": x}
    hn = rmsnorm_fwd("attn_norm", x, p["attn_norm"], x.shape[1], BF16)
    proj = matmul("in_proj", hn, p["w_in"], "nn", F32)
    sv["hn"] = hn
    aq, ak = proj[:, O_AQ:O_AK], proj[:, O_AK:O_AV]
    av = proj[:, O_AV:O_BQ].astype(BF16)
    bq, bk = proj[:, O_BQ:O_BK], proj[:, O_BK:O_BV]
    bv = proj[:, O_BV:O_CQ].astype(BF16)
    cq, ckv, ckr = proj[:, O_CQ:O_CKV], proj[:, O_CKV:O_CKR], proj[:, O_CKR:O_CKR + HEAD]
    sv.update(aq=aq, ak=ak, av=av, bv=bv, cq=cq, ckv=ckv)

    qa = rope("a_q_rope", rmsnorm_fwd("a_q_norm", aq, p["a_q_norm"], HEAD, F32), tabs["a"][0], BF16)
    ka = rope("a_k_rope", rmsnorm_fwd("a_k_norm", ak, p["a_k_norm"], HEAD, F32), tabs["a"][0], BF16)
    ya, lse_a = attn_fwd("a_attn", [(qa, 0, ka, A_HEADS // A_KV)], av, A_HEADS // A_KV, A_HEADS, HEAD ** -0.5)
    sv.update(qa=qa, ka=ka, ya=ya, lse_a=lse_a)

    qb = rope("b_q_rope", bq, tabs["b"][0], BF16)
    kb = rope("b_k_rope", bk, tabs["b"][0], BF16)
    outs, lses = [], []
    for g, d in enumerate(B_DILATIONS):
        o, l = band_fwd(f"b_band{g}", qb, g * B_HEADS, kb, bv, d)
        outs.append(o)
        lses.append(l)
    yb, lse_b = mix_fwd("b_mix", outs, lses)
    sv.update(qb=qb, kb=kb, yb=yb, lse_b=lse_b)

    cqn = rmsnorm_fwd("c_q_norm", cq, p["c_q_norm"], C_RANK, BF16)
    qc_raw = matmul("c_uq", cqn, p["w_uq"], "nn", F32)
    qc = jnp.concatenate([qc_raw[:, :C_W].astype(BF16), rope("c_q_rope", qc_raw[:, C_W:], tabs["c"][0], BF16)], axis=1)
    ckvn = rmsnorm_fwd("c_kv_norm", ckv, p["c_kv_norm"], C_RANK, BF16)
    kvc = matmul("c_ukv", ckvn, p["w_ukv"], "nn", BF16)
    krc = rope("c_k_rope", ckr, tabs["c"][0], BF16)
    c_scale = (HEAD + C_ROPE) ** -0.5
    yc, lse_c = attn_fwd_c("c_attn", qc, kvc, krc, c_scale)
    sv.update(cqn=cqn, ckvn=ckvn, qc=qc, kvc=kvc, krc=krc, yc=yc, lse_c=lse_c)

    g_a, g_b, g_c = p["out_norm"][:A_W], p["out_norm"][A_W:A_W + B_W], p["out_norm"][A_W + B_W:]
    y = jnp.concatenate([rmsnorm_fwd("out_norm_a", ya, g_a, A_W, BF16), rmsnorm_fwd("out_norm_b", yb, g_b, B_W, BF16),
                         rmsnorm_fwd("out_norm_c", yc, g_c, C_W, BF16)], axis=1)
    x1 = matmul("out_proj", y, p["w_out"], "nn", F32, add=x)
    sv.update(y=y, x1=x1)
    return x1, sv


def _ffn_fwd(x1, p, sv):
    xn = rmsnorm_fwd("ffn_norm", x1, p["ffn_norm"], x1.shape[1], BF16)
    h, act = ffn_up("ffn_up", xn, p["w_up"], p["cwb"])
    x2 = matmul("ffn_down", act, p["w_down"], "nn", F32, add=x1)
    sv.update(xn=xn, h=h, act=act)
    return x2


def _layer_bwd(dx2, sv, p, tabs):
    dxn, g = _ffn_bwd(dx2, sv, p)
    dx1, dy, g_out = _out_proj_bwd(dxn, dx2, sv, p)
    dhn, g_mix = _mixers_bwd(dy, sv, p, tabs)
    dx, g_norm = _attn_norm_bwd(dhn, dx1, sv, p)
    return dx, {**g, **g_out, **g_mix, **g_norm}


def _ffn_bwd(dx2, sv, p, after=0.0):
    g = {}
    dx2b = (dx2 + after).astype(BF16)
    dact = matmul("ffn_down_dx", dx2b, p["w_down"], "nt", F32)
    g["w_down"] = matmul("ffn_down_dw", sv["act"], dx2b, "tn", F32)
    dh, g["cwb"] = ffn_gate_bwd("ffn_gate_bwd", sv["h"], dact, p["cwb"])
    dxn = ffn_dx("ffn_up_dx", dh, p["w_up"])
    g["w_up"] = ffn_dw_up("ffn_up_dw", sv["xn"], dh)
    return dxn, g


def _out_proj_bwd(dxn, dx2, sv, p, after=0.0):
    g = {}
    dm = dx2.shape[1]
    dx1, dx1b, g["ffn_norm"] = rmsnorm_bwd("ffn_norm_bwd", dxn, sv["x1"], p["ffn_norm"] + after, dm, (F32, BF16), add=dx2)
    dy = matmul("out_proj_dx", dx1b, p["w_out"], "nt", F32)
    g["w_out"] = matmul("out_proj_dw", sv["y"], dx1b, "tn", F32)
    return dx1, dy, g


def _attn_norm_bwd(dhn, dx1, sv, p, after=0.0):
    dx, d_gain = rmsnorm_bwd("attn_norm_bwd", dhn, sv["x"], p["attn_norm"] + after, dhn.shape[1], (F32,), add=dx1)
    return dx, {"attn_norm": d_gain}


def _mixers_bwd(dy, sv, p, tabs, after=0.0):
    g = {}
    out_norm = p["out_norm"] + after
    g_a, g_b, g_c = out_norm[:A_W], out_norm[A_W:A_W + B_W], out_norm[A_W + B_W:]
    dya, dga = rmsnorm_bwd("out_norm_a_bwd", dy[:, :A_W], sv["ya"], g_a, A_W, (BF16,))
    dyb, dgb = rmsnorm_bwd("out_norm_b_bwd", dy[:, A_W:A_W + B_W], sv["yb"], g_b, B_W, (F32,))
    dyc, dgc = rmsnorm_bwd("out_norm_c_bwd", dy[:, A_W + B_W:], sv["yc"], g_c, C_W, (BF16,))
    g["out_norm"] = jnp.concatenate([dga, dgb, dgc])

    c_scale = (HEAD + C_ROPE) ** -0.5
    dqn, dqr, dkn, dvc, dkr = attn_bwd_c("c_attn_bwd", sv["qc"], sv["kvc"], sv["krc"], sv["lse_c"], dyc, c_scale)
    s = dy.shape[0]
    dqc = jnp.concatenate([dqn.astype(BF16), rope("c_q_rope_bwd", dqr, tabs["c"][1], BF16)], axis=1)
    dcqn = matmul("c_uq_dx", dqc, p["w_uq"], "nt", F32)
    g["w_uq"] = matmul("c_uq_dw", sv["cqn"], dqc, "tn", F32)
    dcq, g["c_q_norm"] = rmsnorm_bwd("c_q_norm_bwd", dcqn, sv["cq"], p["c_q_norm"], C_RANK, (BF16,))
    dkvc = jnp.stack([dkn.reshape(s, C_HEADS, HEAD), dvc.reshape(s, C_HEADS, HEAD)], axis=2).reshape(s, 2 * C_W).astype(BF16)
    dckvn = matmul("c_ukv_dx", dkvc, p["w_ukv"], "nt", F32)
    g["w_ukv"] = matmul("c_ukv_dw", sv["ckvn"], dkvc, "tn", F32)
    dckv, g["c_kv_norm"] = rmsnorm_bwd("c_kv_norm_bwd", dckvn, sv["ckv"], p["c_kv_norm"], C_RANK, (BF16,))
    dckr = rope("c_k_rope_bwd", dkr, tabs["c"][1], BF16)

    delta_b, dybb = head_rowdot("b_delta", dyb, sv["yb"])
    dqs, dks, dvs = [], [], []
    for gi, d in enumerate(B_DILATIONS):
        dq_g, dk_g, dv_g = band_bwd(f"b_band{gi}_bwd", sv["qb"], gi * B_HEADS, sv["kb"], sv["bv"], dybb,
                                    sv["lse_b"], delta_b, d)
        dqs.append(dq_g)
        dks.append(dk_g)
        dvs.append(dv_g)
    dbq = rope("b_q_rope_bwd", jnp.concatenate(dqs, axis=1), tabs["b"][1], BF16)
    dbk = rope("b_k_rope_bwd", add3("b_dk_sum", *dks), tabs["b"][1], BF16)
    dbv = add3("b_dv_sum", *dvs, out_dtype=BF16)

    kg = A_HEADS // A_KV
    (dqa,), (dka,), dva = attn_bwd("a_attn_bwd", [(sv["qa"], 0, sv["ka"], kg)], sv["av"], kg, sv["lse_a"], dya,
                                   A_HEADS, HEAD ** -0.5)
    daq, g["a_q_norm"] = rmsnorm_bwd("a_q_norm_bwd", rope("a_q_rope_bwd", dqa, tabs["a"][1], F32), sv["aq"],
                                     p["a_q_norm"], HEAD, (BF16,))
    dak, g["a_k_norm"] = rmsnorm_bwd("a_k_norm_bwd", rope("a_k_rope_bwd", dka, tabs["a"][1], F32), sv["ak"],
                                     p["a_k_norm"], HEAD, (BF16,))

    dproj = jnp.concatenate([daq, dak, dva.astype(BF16), dbq, dbk, dbv, dcq, dckv, dckr,
                             jnp.zeros((s, PROJ_W - O_CKR - HEAD), BF16)], axis=1)
    dhn = matmul("in_proj_dx", dproj, p["w_in"], "nt", F32)
    g["w_in"] = matmul("in_proj_dw", sv["hn"], dproj, "tn", F32)
    return dhn, g


def _local_step(x, target, layers, final_norm, tabs):
    saved = []
    for p in layers:
        x, sv = _layer_fwd(x, p, tabs)
        saved.append(sv)
    dm = x.shape[1]
    yf = rmsnorm_fwd("final_norm", x, final_norm, dm, F32)
    loss, dyf = loss_head("loss_head", yf, target)
    dx, d_final = rmsnorm_bwd("final_norm_bwd", dyf, x, final_norm, dm, (F32,))
    grads = [None] * len(layers)
    for l in reversed(range(len(layers))):
        dx, grads[l] = _layer_bwd(dx, saved[l], layers[l], tabs)
    return loss, dx, grads, d_final


HBM = pl.BlockSpec(memory_space=pl.ANY)


def _place():
    x, y, c = lax.axis_index("x"), lax.axis_index("y"), lax.axis_index("c")
    others = [(1 - x, y), (x, 1 - y), (1 - x, 1 - y)]
    return x, y, c, 2 * x + y, others


def _remote(src, dst, send_sem, recv_sem, device):
    return pltpu.make_async_remote_copy(src_ref=src, dst_ref=dst, send_sem=send_sem, recv_sem=recv_sem,
                                        device_id=device, device_id_type=MESH)


def gather_small(name, small):
    def body(in_ref, out_ref, send_sems, recv_sems, local_sem):
        x, y, c, chip, others = _place()
        mine = pltpu.make_async_copy(in_ref, out_ref.at[chip], local_sem)
        mine.start()
        copies = []
        for k, (ox, oy) in enumerate(others):
            cp = _remote(in_ref, out_ref.at[chip], send_sems.at[k], recv_sems.at[k], (ox, oy, c))
            cp.start()
            copies.append(cp)
        for k, (ox, oy) in enumerate(others):
            landed = out_ref.at[2 * ox + oy]
            _remote(landed, landed, send_sems.at[k], recv_sems.at[k], (ox, oy, c)).wait_recv()
        for cp in copies:
            cp.wait_send()
        mine.wait()

    return pl.pallas_call(
        body, name=name, in_specs=[HBM], out_specs=HBM, out_shape=jax.ShapeDtypeStruct((4,) + small.shape, small.dtype),
        scratch_shapes=[pltpu.SemaphoreType.DMA((3,)), pltpu.SemaphoreType.DMA((3,)), pltpu.SemaphoreType.DMA],
        compiler_params=pltpu.CompilerParams(has_side_effects=True))(small)


IN_HBM = pl.BlockSpec(memory_space=pltpu.HBM)
SEMS = pl.BlockSpec(memory_space=pltpu.SEMAPHORE)
DATAFLOW = pltpu.SideEffectType.DATAFLOW_SIDE_EFFECTING


def _gather_plan(bufs):
    x, y, c, chip, others = _place()
    plan = []
    for t, ref in enumerate(bufs):
        half = ref.shape[1] // 2
        rows = pl.ds(c * half, half)
        for k, (ox, oy) in enumerate(others):
            plan.append((3 * t + k, (ox, oy, c), ref.at[chip, rows], ref.at[chip, rows], ref.at[2 * ox + oy, rows]))
    return plan


def _exchange_plan(bufs):
    x, y, c, chip, others = _place()
    n = len(bufs) // 2
    plan = []
    for t in range(n):
        for k, (ox, oy) in enumerate(others):
            plan.append((3 * t + k, (ox, oy, c), bufs[t].at[2 * ox + oy], bufs[n + t].at[chip], bufs[n + t].at[2 * ox + oy]))
    return plan


def _sibling_plan(bufs):
    x, y, c, chip, others = _place()
    n = len(bufs) // 2
    plan = []
    for t in range(n):
        half = bufs[t].shape[1] // 2
        plan.append((t, (x, y, 1 - c), bufs[t].at[:, pl.ds((1 - c) * half, half), :], bufs[n + t], bufs[n + t]))
    return plan


PLAN_COPIES = {_gather_plan: lambda n: 3 * n, _exchange_plan: lambda n: 3 * (n // 2), _sibling_plan: lambda n: n // 2}


def split_start(name, bufs, plan_of, after):
    n = len(bufs)

    def body(*refs):
        ins = refs[:n]
        send_sems, recv_sems = refs[n + 1], refs[n + 2]
        token = refs[-1]
        for idx, peer, src, dst, _ in plan_of(ins):
            _remote(src, dst, send_sems.at[idx], recv_sems.at[idx], peer).start()
        token[...] = jnp.zeros_like(token)

    copies = PLAN_COPIES[plan_of](n)
    res = pl.pallas_call(
        body, name=name, in_specs=[IN_HBM] * n + [HBM],
        out_specs=(SEMS, SEMS) + (IN_HBM,) * n + (pl.BlockSpec(memory_space=pltpu.VMEM),),
        out_shape=(pltpu.SemaphoreType.DMA((copies,)), pltpu.SemaphoreType.DMA((copies,)))
        + tuple(pltpu.HBM(b.shape, b.dtype) for b in bufs) + (jax.ShapeDtypeStruct((8, LANES), F32),),
        input_output_aliases={t: 2 + t for t in range(n)},
        compiler_params=pltpu.CompilerParams(has_side_effects=DATAFLOW))(
            *[pltpu.with_memory_space_constraint(b, pltpu.HBM) for b in bufs], after)
    return res[0], res[1], list(res[2:2 + n]), res[-1]


def split_wait(name, send_sems, recv_sems, flying, plan_of, after):
    n = len(flying)

    def body(*refs):
        ins = refs[:n]
        send_ref, recv_ref = refs[n], refs[n + 1]
        for idx, peer, src, dst, landing in plan_of(ins):
            _remote(src, dst, send_ref.at[idx], recv_ref.at[idx], peer).wait_send()
            _remote(landing, landing, send_ref.at[idx], recv_ref.at[idx], peer).wait_recv()

    return list(pl.pallas_call(
        body, name=name, in_specs=[IN_HBM] * n + [SEMS, SEMS] + [HBM] * len(after), out_specs=(IN_HBM,) * n,
        out_shape=tuple(pltpu.HBM(b.shape, b.dtype) for b in flying),
        input_output_aliases={t: t for t in range(n)},
        compiler_params=pltpu.CompilerParams(has_side_effects=DATAFLOW))(*flying, send_sems, recv_sems, *after))


def forward_halves(name, slots):
    n = len(slots)

    def body(*refs):
        ins, outs = refs[:n], refs[n:2 * n]
        send_sems, recv_sems = refs[2 * n:]
        x, y, c, chip, others = _place()
        sibling = (x, y, 1 - c)
        copies = []
        for t in range(n):
            half = slots[t].shape[1] // 2
            for k, (ox, oy) in enumerate(others):
                rows = pl.ds(c * half, half)
                cp = _remote(ins[t].at[2 * ox + oy, rows], outs[t].at[2 * ox + oy, rows], send_sems.at[t, k],
                             recv_sems.at[t, k], sibling)
                cp.start()
                copies.append(cp)
        for t in range(n):
            half = slots[t].shape[1] // 2
            for k, (ox, oy) in enumerate(others):
                theirs = outs[t].at[2 * ox + oy, pl.ds((1 - c) * half, half)]
                _remote(theirs, theirs, send_sems.at[t, k], recv_sems.at[t, k], sibling).wait_recv()
        for cp in copies:
            cp.wait_send()

    return list(pl.pallas_call(
        body, name=name, in_specs=[HBM] * n, out_specs=[HBM] * n,
        out_shape=[jax.ShapeDtypeStruct(s.shape, s.dtype) for s in slots],
        input_output_aliases={t: t for t in range(n)},
        scratch_shapes=[pltpu.SemaphoreType.DMA((n, 3))] * 2,
        compiler_params=pltpu.CompilerParams(has_side_effects=True))(*slots))


def add_half(name, place, grad, theirs):
    _, r, cols = grad.shape
    half = r // 2
    tr = _rows(half, cols)

    def body(place_ref, g_ref, t_ref, o_ref):
        o_ref[...] = (g_ref[...] + t_ref[...]).astype(o_ref.dtype)

    return pl.pallas_call(
        body, name=name,
        grid_spec=pltpu.PrefetchScalarGridSpec(
            num_scalar_prefetch=1, grid=(4, half // tr),
            in_specs=[pl.BlockSpec((None, None, tr, cols), lambda j, i, pr: (j, pr[0], i, 0)),
                      pl.BlockSpec((None, tr, cols), lambda j, i, pr: (j, i, 0))],
            out_specs=pl.BlockSpec((None, tr, cols), lambda j, i, pr: (j, i, 0))),
        out_shape=jax.ShapeDtypeStruct((4, half, cols), BF16),
        compiler_params=_params("parallel", "parallel"))(place, grad.reshape(4, 2, half, cols), theirs)


def sum_chips(name, place, own, parts):
    _, half, cols = parts.shape
    tr = _rows(half, cols)

    def body(place_ref, own_ref, p1, p2, p3, o_ref):
        o_ref[...] = own_ref[...].astype(F32) + p1[...].astype(F32) + p2[...].astype(F32) + p3[...].astype(F32)

    def slot(k):
        return pl.BlockSpec((None, tr, cols), lambda i, pr: (lax.rem(pr[1] + k, 4), i, 0))

    return pl.pallas_call(
        body, name=name,
        grid_spec=pltpu.PrefetchScalarGridSpec(
            num_scalar_prefetch=1, grid=(half // tr,), in_specs=[slot(0), slot(1), slot(2), slot(3)],
            out_specs=pl.BlockSpec((None, tr, cols), lambda i, pr: (pr[0], i, 0))),
        out_shape=jax.ShapeDtypeStruct((2, half, cols), F32),
        compiler_params=_params("parallel"))(place, own, parts, parts, parts)


def join_halves(name, halves):
    n = len(halves)

    def body(*refs):
        ins, outs = refs[:n], refs[n:2 * n]
        send_sems, recv_sems = refs[2 * n:]
        x, y, c, _, _ = _place()
        copies = []
        for t in range(n):
            cp = _remote(ins[t].at[c], outs[t].at[c], send_sems.at[t], recv_sems.at[t], (x, y, 1 - c))
            cp.start()
            copies.append(cp)
        for t in range(n):
            theirs = outs[t].at[1 - c]
            _remote(theirs, theirs, send_sems.at[t], recv_sems.at[t], (x, y, 1 - c)).wait_recv()
        for cp in copies:
            cp.wait_send()

    return pl.pallas_call(
        body, name=name, in_specs=[HBM] * n, out_specs=[HBM] * n,
        out_shape=[jax.ShapeDtypeStruct(h.shape, h.dtype) for h in halves],
        input_output_aliases={t: t for t in range(n)},
        scratch_shapes=[pltpu.SemaphoreType.DMA((n,))] * 2,
        compiler_params=pltpu.CompilerParams(has_side_effects=True))(*halves)


def allreduce_small(name, buf):
    rows = buf.shape[0]

    def body(in_ref, out_ref, slots, send_sems, recv_sems):
        x, y, c, _, _ = _place()
        me = 4 * x + 2 * y + c
        slots[me] = in_ref[...]
        peers = []
        for k in range(1, 8):
            px = 1 - x if k & 4 else x
            py = 1 - y if k & 2 else y
            pc = 1 - c if k & 1 else c
            peers.append((px, py, pc))
        copies = []
        for k, peer in enumerate(peers):
            cp = _remote(in_ref, slots.at[me], send_sems.at[k], recv_sems.at[k], peer)
            cp.start()
            copies.append(cp)
        for k, (px, py, pc) in enumerate(peers):
            slot = slots.at[4 * px + 2 * py + pc]
            _remote(slot, slot, send_sems.at[k], recv_sems.at[k], (px, py, pc)).wait_recv()
        for cp in copies:
            cp.wait_send()
        acc = slots[0]
        for d in range(1, 8):
            acc = acc + slots[d]
        out_ref[...] = acc

    vm = pl.BlockSpec(memory_space=pltpu.VMEM)
    return pl.pallas_call(
        body, name=name, in_specs=[vm], out_specs=vm, out_shape=jax.ShapeDtypeStruct(buf.shape, F32),
        scratch_shapes=[pltpu.VMEM((8, rows, LANES), F32), pltpu.SemaphoreType.DMA((7,)), pltpu.SemaphoreType.DMA((7,))],
        compiler_params=pltpu.CompilerParams(has_side_effects=True, vmem_limit_bytes=VMEM_LIMIT))(buf)


def cast_layer(name, place, w, layer):
    _, r, cols = w.shape
    tr = _rows(r, cols)

    def body(place_ref, w_ref, o_ref):
        o_ref[...] = w_ref[...].astype(BF16)

    return pl.pallas_call(
        body, name=name,
        grid_spec=pltpu.PrefetchScalarGridSpec(
            num_scalar_prefetch=1, grid=(r // tr,),
            in_specs=[pl.BlockSpec((None, tr, cols), lambda i, pr: (layer, i, 0))],
            out_specs=pl.BlockSpec((None, tr, cols), lambda i, pr: (pr[1], i, 0))),
        out_shape=jax.ShapeDtypeStruct((4, r, cols), BF16),
        compiler_params=_params("parallel"))(place, w)


def _adamw_math(g, w, m, v):
    m = ADAM_B1 * m + (1.0 - ADAM_B1) * g
    v = ADAM_B2 * v + (1.0 - ADAM_B2) * (g * g)
    m_hat = m / (1.0 - ADAM_B1 ** ADAM_STEP)
    v_hat = v / (1.0 - ADAM_B2 ** ADAM_STEP)
    delta = -ADAM_LR * (m_hat / (jnp.sqrt(v_hat) + ADAM_EPS) + ADAM_WD * w)
    return delta, m, v


def adamw_layer(name, layer, g, w, m, v, prev=None):
    nl, r, cols = w.shape
    tr = _rows(r, cols, 8)

    def body(*refs):
        g_ref, w_ref, m_ref, v_ref = refs[:4]
        og, od, om, ov = refs[-4:]
        gv = g_ref[...]
        delta, m2, v2 = _adamw_math(gv, w_ref[...], m_ref[...], v_ref[...])
        og[...] = gv
        od[...] = delta
        om[...] = m2
        ov[...] = v2

    lay = pl.BlockSpec((None, tr, cols), lambda i: (layer, i, 0))
    ins = [g, w, m, v] + (list(prev) if prev is not None else [])
    return pl.pallas_call(
        body, name=name, grid=(r // tr,),
        in_specs=[pl.BlockSpec((tr, cols), lambda i: (i, 0)), lay, lay, lay] + ([HBM] * 4 if prev is not None else []),
        out_specs=[lay] * 4, out_shape=[jax.ShapeDtypeStruct((nl, r, cols), F32)] * 4,
        input_output_aliases={4 + k: k for k in range(4)} if prev is not None else {},
        compiler_params=_params("parallel"))(*ins)


def adamw_packed(name, g, w, m, v):
    def body(g_ref, w_ref, m_ref, v_ref, od, om, ov):
        delta, m2, v2 = _adamw_math(g_ref[...], w_ref[...], m_ref[...], v_ref[...])
        od[...] = delta
        om[...] = m2
        ov[...] = v2

    vm = pl.BlockSpec(memory_space=pltpu.VMEM)
    return pl.pallas_call(
        body, name=name, in_specs=[vm] * 4, out_specs=[vm] * 3, out_shape=[jax.ShapeDtypeStruct(g.shape, F32)] * 3,
        compiler_params=pltpu.CompilerParams(vmem_limit_bytes=VMEM_LIMIT))(g, w, m, v)


def _pack(arrays):
    flat = jnp.concatenate([a.reshape(-1) for a in arrays])
    pad = (-flat.shape[0]) % (8 * LANES)
    return jnp.pad(flat, (0, pad)).reshape(-1, LANES)


def _unpack(buf, shapes):
    flat = buf.reshape(-1)
    out, off = [], 0
    for shp in shapes:
        size = int(np.prod(shp))
        out.append(flat[off:off + size].reshape(shp))
        off += size
    return out


MIXER_W = ("w_in", "w_uq", "w_ukv", "w_out")
FFN_W = ("w_up", "w_down")
BIG = MIXER_W + FFN_W
COLUMN_CUT = ("w_in", "w_uq", "w_ukv", "w_up")
SMALL = ("attn_norm", "a_q_norm", "a_k_norm", "c_q_norm", "c_kv_norm", "out_norm", "ffn_norm", "conv_b", "final_norm")
WEIGHTS = ("attn_norm", "w_in", "a_q_norm", "a_k_norm", "c_q_norm", "c_kv_norm", "w_uq", "w_ukv", "out_norm", "w_out",
           "ffn_norm", "w_up", "conv_w", "conv_b", "w_down", "final_norm")
INPUTS = ("x",) + WEIGHTS + ("loss_target",) + tuple("m_" + n for n in WEIGHTS) + tuple("v_" + n for n in WEIGHTS)


def _columns(g):
    return jnp.transpose(g, (1, 0, 2)).reshape(g.shape[1], 4 * g.shape[2])


def _uncolumns(w):
    r, c4 = w.shape
    return jnp.transpose(w.reshape(r, 4, c4 // 4), (1, 0, 2))


def _compute_layouts(full, conv_w, small, layer):
    p = {n: small[n][layer] for n in SMALL if n != "final_norm"}
    if "w_in" in full:
        w_in = _columns(full["w_in"])
        p["w_in"] = jnp.pad(w_in, ((0, 0), (0, PROJ_W - IN_W)))
        w_uq = _columns(full["w_uq"]).reshape(C_RANK, C_HEADS, HEAD + C_ROPE)
        p["w_uq"] = jnp.concatenate([w_uq[:, :, :HEAD].reshape(C_RANK, C_W),
                                     jnp.pad(w_uq[:, :, HEAD:], ((0, 0), (0, 0), (0, HEAD - C_ROPE))).reshape(C_RANK, C_W)], axis=1)
        p["w_ukv"] = _columns(full["w_ukv"])
        p["w_out"] = full["w_out"].reshape(-1, full["w_out"].shape[2])
    if "w_up" in full:
        up = full["w_up"]
        p["w_up"] = up.reshape(2, 2, up.shape[1], up.shape[2])
        p["w_down"] = full["w_down"].reshape(-1, full["w_down"].shape[2])
        fh = conv_w.shape[1] // 2
        taps = jnp.transpose(conv_w.reshape(3, 2, fh), (1, 0, 2))
        p["cwb"] = jnp.concatenate([taps, small["conv_b"][layer].reshape(2, 1, fh), jnp.zeros((2, 4, fh), F32)], axis=1)
    return p


def _shard_layout(name, g):
    if name == "w_in":
        return _uncolumns(g[:, :IN_W])
    if name == "w_uq":
        uq = jnp.concatenate([g[:, :C_W].reshape(C_RANK, C_HEADS, HEAD),
                              g[:, C_W:].reshape(C_RANK, C_HEADS, HEAD)[:, :, :C_ROPE]], axis=2)
        return _uncolumns(uq.reshape(C_RANK, C_HEADS * (HEAD + C_ROPE)))
    if name == "w_ukv":
        return _uncolumns(g)
    if name == "w_up":
        return g.reshape(4, g.shape[2], g.shape[3])
    return g.reshape(4, -1, g.shape[1])


def _conv_grads(dcwb):
    fh = dcwb.shape[2]
    return jnp.transpose(dcwb[:, 0:3, :], (1, 0, 2)).reshape(3, 2 * fh), dcwb[:, 3, :].reshape(2 * fh)


def _shard_layouts(g):
    return ({n: _shard_layout(n, g[n]) for n in BIG},) + _conv_grads(g["cwb"])


def kernel(x, attn_norm, w_in, a_q_norm, a_k_norm, c_q_norm, c_kv_norm, w_uq, w_ukv, out_norm, w_out, ffn_norm, w_up, conv_w, conv_b, w_down, final_norm, loss_target, m_attn_norm, m_w_in, m_a_q_norm, m_a_k_norm, m_c_q_norm, m_c_kv_norm, m_w_uq, m_w_ukv, m_out_norm, m_w_out, m_ffn_norm, m_w_up, m_conv_w, m_conv_b, m_w_down, m_final_norm, v_attn_norm, v_w_in, v_a_q_norm, v_a_k_norm, v_c_q_norm, v_c_kv_norm, v_w_uq, v_w_ukv, v_out_norm, v_w_out, v_ffn_norm, v_w_up, v_conv_w, v_conv_b, v_w_down, v_final_norm):
    a = dict(zip(INPUTS, (x, attn_norm, w_in, a_q_norm, a_k_norm, c_q_norm, c_kv_norm, w_uq, w_ukv, out_norm, w_out, ffn_norm, w_up, conv_w, conv_b, w_down, final_norm, loss_target, m_attn_norm, m_w_in, m_a_q_norm, m_a_k_norm, m_c_q_norm, m_c_kv_norm, m_w_uq, m_w_ukv, m_out_norm, m_w_out, m_ffn_norm, m_w_up, m_conv_w, m_conv_b, m_w_down, m_final_norm, v_attn_norm, v_w_in, v_a_q_norm, v_a_k_norm, v_c_q_norm, v_c_kv_norm, v_w_uq, v_w_ukv, v_out_norm, v_w_out, v_ffn_norm, v_w_up, v_conv_w, v_conv_b, v_w_down, v_final_norm)))
    nl = w_in.shape[0]
    seq = x.shape[1]
    chip = 2 * lax.axis_index("x") + lax.axis_index("y")
    place = jnp.stack([lax.axis_index("c"), chip]).astype(jnp.int32)
    tabs = _rope_tables(seq)

    assert nl == 2
    quarter = conv_w.shape[2]
    taps = jnp.pad(conv_w, ((0, 0), (0, 8 - conv_w.shape[1]), (0, 0))).reshape(nl * 8, quarter)
    taps_full = gather_small("gather_taps", taps).reshape(4, nl, 8, quarter)
    conv_full = [jnp.transpose(taps_full[:, l, 0:3, :], (1, 0, 2)).reshape(3, 4 * quarter) for l in range(nl)]
    flights = {}
    after = taps_full
    for key, l, names in (("mix0", 0, MIXER_W), ("ffn0", 0, FFN_W), ("all1", 1, BIG)):
        slots = [cast_layer(f"cast{l}_{n}", place, a[n], l) for n in names]
        send_sems, recv_sems, flying, after = split_start(f"gather_{key}_start", slots, _gather_plan, after)
        flights[key] = (names, send_sems, recv_sems, flying)
    started = after[0, 0]

    def land(key, after):
        names, send_sems, recv_sems, flying = flights[key]
        landed = split_wait(f"gather_{key}_wait", send_sems, recv_sems, flying, _gather_plan, after)
        return dict(zip(names, forward_halves(f"gather_{key}_forward", landed)))

    h = x[0]
    layers, saved = [], []
    for l in range(nl):
        if l == 0:
            p = _compute_layouts(land("mix0", [after]), conv_full[l], a, l)
            p["attn_norm"] = p["attn_norm"] + started
            x1, sv = _mixers_fwd(h, p, tabs)
            p.update(_compute_layouts(land("ffn0", [x1]), conv_full[l], a, l))
        else:
            p = _compute_layouts(land("all1", [h]), conv_full[l], a, l)
            x1, sv = _mixers_fwd(h, p, tabs)
        h = _ffn_fwd(x1, p, sv)
        layers.append(p)
        saved.append(sv)
    dm = h.shape[1]
    yf = rmsnorm_fwd("final_norm", h, final_norm, dm, F32)
    loss, dyf = loss_head("loss_head", yf, loss_target[0])
    loss = lax.psum(loss, ("x", "y", "c"))
    dx, d_final = rmsnorm_bwd("final_norm_bwd", dyf, h, final_norm, dm, (F32,))

    def swap_begin(tag, names, g):
        g_list = [_shard_layout(n, g[n]) for n in names]
        theirs = [lax.empty((4, s.shape[1] // 2, s.shape[2]), s.dtype) for s in g_list]
        send_sems, recv_sems, flying, token = split_start(f"{tag}_sibling_start", g_list + theirs, _sibling_plan, g_list[0])
        return (names, send_sems, recv_sems, flying), token[0, 0]

    def exchange_begin(tag, swap, after):
        names, send_sems, recv_sems, flying = swap
        landed = split_wait(f"{tag}_sibling_wait", send_sems, recv_sems, flying, _sibling_plan, after)
        g_list, theirs = landed[:len(names)], landed[len(names):]
        sums = [add_half(f"{tag}_add_{n}", place, gr, t) for n, gr, t in zip(names, g_list, theirs)]
        parts = [lax.empty(s.shape, s.dtype) for s in sums]
        send_sems, recv_sems, flying, token = split_start(f"{tag}_chips_start", sums + parts, _exchange_plan, sums[0])
        return (names, send_sems, recv_sems, flying), token

    def exchange_end(tag, flight, after):
        names, send_sems, recv_sems, flying = flight
        landed = split_wait(f"{tag}_chips_wait", send_sems, recv_sems, flying, _exchange_plan, after)
        sums, parts = landed[:len(names)], landed[len(names):]
        halves = [sum_chips(f"{tag}_sum_{n}", place, s, p) for n, s, p in zip(names, sums, parts)]
        joined = join_halves(f"{tag}_join", halves)
        return {n: j.reshape(2 * j.shape[1], j.shape[2]) for n, j in zip(names, joined)}

    grads, conv_grads, flights_f, flights_m = [None] * nl, [None] * nl, [None] * nl, [None] * nl
    after = 0.0
    for l in reversed(range(nl)):
        sv, p = saved[l], layers[l]
        dxn, g = _ffn_bwd(dx, sv, p, after)
        swap_f, after = swap_begin(f"reduce{l}f", FFN_W, g)
        dx1, dy, g_out = _out_proj_bwd(dxn, dx, sv, p, after)
        flights_f[l], token = exchange_begin(f"reduce{l}f", swap_f, [dy])
        dhn, g_mix = _mixers_bwd(dy, sv, p, tabs, token[0, 0])
        g.update(g_out)
        g.update(g_mix)
        swap_m, after = swap_begin(f"reduce{l}m", MIXER_W, g)
        dx, g_norm = _attn_norm_bwd(dhn, dx1, sv, p, after)
        g.update(g_norm)
        flights_m[l], token = exchange_begin(f"reduce{l}m", swap_m, [dx])
        after = token[0, 0]
        grads[l], conv_grads[l] = g, _conv_grads(g["cwb"])
    grad_x = dx

    out = {}
    reduced1 = exchange_end("reduce1f", flights_f[1], [token])
    for n in FFN_W:
        out[n] = adamw_layer(f"adamw1_{n}", 1, reduced1[n], a[n], a["m_" + n], a["v_" + n])
    reduced1 = exchange_end("reduce1m", flights_m[1], [out[n][3] for n in FFN_W])
    for n in MIXER_W:
        out[n] = adamw_layer(f"adamw1_{n}", 1, reduced1[n], a[n], a["m_" + n], a["v_" + n])

    small_g = {n: jnp.stack([grads[l][n] for l in range(nl)]) for n in SMALL if n not in ("conv_b", "final_norm")}
    small_g["conv_b"] = jnp.stack([cg[1] for cg in conv_grads])
    small_g["final_norm"] = d_final
    conv_w_g = jnp.stack([cg[0] for cg in conv_grads])
    shapes = [a[n].shape for n in SMALL] + [conv_w_g.shape]
    summed = _unpack(allreduce_small("reduce_small", _pack([small_g[n] for n in SMALL] + [conv_w_g])), shapes)
    small_g = dict(zip(SMALL, summed[:-1]))
    small_g["conv_w"] = lax.dynamic_slice_in_dim(summed[-1], chip * quarter, quarter, axis=2)
    names = SMALL + ("conv_w",)
    shapes = [a[n].shape for n in names]
    d_small, m_small, v_small = adamw_packed(
        "adamw_small", _pack([small_g[n] for n in names]), _pack([a[n] for n in names]),
        _pack([a["m_" + n] for n in names]), _pack([a["v_" + n] for n in names]))

    reduced0 = exchange_end("reduce0f", flights_f[0], [out[n][3] for n in BIG] + [d_small])
    for n in FFN_W:
        out[n] = adamw_layer(f"adamw0_{n}", 0, reduced0[n], a[n], a["m_" + n], a["v_" + n], prev=out[n])
    reduced0 = exchange_end("reduce0m", flights_m[0], [out[n][3] for n in FFN_W])
    for n in MIXER_W:
        out[n] = adamw_layer(f"adamw0_{n}", 0, reduced0[n], a[n], a["m_" + n], a["v_" + n], prev=out[n])
    for n, d_, m_, v_ in zip(names, _unpack(d_small, shapes), _unpack(m_small, shapes), _unpack(v_small, shapes)):
        out[n] = (small_g[n], d_, m_, v_)

    return (loss, grad_x[None]) + tuple(out[n][k] for k in range(4) for n in WEIGHTS)
```

```python
import functools

import jax
import jax.numpy as jnp
import numpy as np
from jax import lax
from jax.experimental import pallas as pl
from jax.experimental.pallas import tpu as pltpu

F32 = jnp.float32
BF16 = jnp.bfloat16
MESH = pl.DeviceIdType.MESH

V7X_VMEM_BYTES = 64 * 1024 * 1024
VMEM_LIMIT = V7X_VMEM_BYTES - 6 * 1024 * 1024
LANES = 128
ELEMENTWISE_BLOCK_BYTES = 2 * 1024 * 1024

HEAD = 128
A_HEADS, A_KV = 6, 2
B_HEADS, B_GROUPS = 4, 3
B_DILATIONS = (1, 4, 16)
B_HALF = 64
C_HEADS, C_RANK, C_ROPE = 6, 512, 64
GRID_W = 64
A_THETA, B_THETA, C_THETA = 10000.0, 500000.0, 10000.0
B_ROPE_DIM = 32
EPS = 1e-6
A_W, B_W, C_W = A_HEADS * HEAD, B_HEADS * HEAD, C_HEADS * HEAD
O_AQ, O_AK, O_AV = 0, 768, 1024
O_BQ, O_BK, O_BV = 1280, 2816, 3328
O_CQ, O_CKV, O_CKR = 3840, 4352, 4864
IN_W = 4928
PROJ_W = 5120

ADAM_LR, ADAM_B1, ADAM_B2, ADAM_EPS, ADAM_WD, ADAM_STEP = 0.001, 0.9, 0.999, 1e-08, 0.01, 10

NN = (((1,), (0,)), ((), ()))
NT = (((1,), (1,)), ((), ()))
TN = (((0,), (0,)), ((), ()))


def _dot(a, b, dims=NN):
    return lax.dot_general(a, b, dims, preferred_element_type=F32)


def _params(*sem):
    return pltpu.CompilerParams(dimension_semantics=sem if sem else None, vmem_limit_bytes=VMEM_LIMIT)


def _tile(n, target, unit=LANES):
    if n <= target:
        return n
    best = 0
    for t in range(unit, target + 1, unit):
        if n % t == 0:
            best = t
    return best if best else n


def _rows(r, width, itemsize=4):
    return _tile(r, max(8, ELEMENTWISE_BLOCK_BYTES // (width * itemsize)), 8)


def _matmul_call(name, dims, grid, a_spec, b_spec, o_spec, out_shape, acc_shape, add_spec=None):
    nk = grid[2]

    def body(*refs):
        a_ref, b_ref = refs[:2]
        add_ref = None if add_spec is None else refs[2]
        o_ref = refs[2 if add_spec is None else 3]

        def finish(r):
            if add_ref is not None:
                r = r + add_ref[...]
            o_ref[...] = r.astype(o_ref.dtype)

        if nk == 1:
            finish(_dot(a_ref[...], b_ref[...], dims))
            return
        acc = refs[-1]
        k = pl.program_id(2)

        @pl.when(k == 0)
        def _():
            acc[...] = _dot(a_ref[...], b_ref[...], dims)

        if nk > 2:
            @pl.when((k > 0) & (k < nk - 1))
            def _():
                acc[...] += _dot(a_ref[...], b_ref[...], dims)

        @pl.when(k == nk - 1)
        def _():
            finish(acc[...] + _dot(a_ref[...], b_ref[...], dims))

    in_specs = [a_spec, b_spec] + ([] if add_spec is None else [add_spec])
    return pl.pallas_call(
        body, name=name, grid=grid, in_specs=in_specs, out_specs=o_spec, out_shape=out_shape,
        scratch_shapes=[pltpu.VMEM(acc_shape, F32)] if nk > 1 else [],
        compiler_params=_params("parallel", "parallel", "arbitrary"))


def matmul(name, a, b, mode, out_dtype, add=None, tm=1024, tn=1024, tk=2816):
    if mode == "nn":
        (m, k), (k2, n) = a.shape, b.shape
    elif mode == "nt":
        (m, k), (n, k2) = a.shape, b.shape
    else:
        (k, m), (k2, n) = a.shape, b.shape
    assert k == k2, (name, a.shape, b.shape)
    tm, tn, tk = _tile(m, tm), _tile(n, tn), _tile(k, tk)
    grid = (m // tm, n // tn, k // tk)
    if mode == "tn":
        a_spec = pl.BlockSpec((tk, tm), lambda i, j, kk: (kk, i))
    else:
        a_spec = pl.BlockSpec((tm, tk), lambda i, j, kk: (i, kk))
    if mode == "nt":
        b_spec = pl.BlockSpec((tn, tk), lambda i, j, kk: (j, kk))
    else:
        b_spec = pl.BlockSpec((tk, tn), lambda i, j, kk: (kk, j))
    o_spec = pl.BlockSpec((tm, tn), lambda i, j, kk: (i, j))
    dims = {"nn": NN, "nt": NT, "tn": TN}[mode]
    call = _matmul_call(name, dims, grid, a_spec, b_spec, o_spec, jax.ShapeDtypeStruct((m, n), out_dtype),
                        (tm, tn), None if add is None else o_spec)
    return call(a, b) if add is None else call(a, b, add)


def rmsnorm_fwd(name, x, g, width, out_dtype):
    r, cols = x.shape
    nb = cols // width
    tr = _rows(r, width)

    def body(x_ref, g_ref, o_ref):
        xv = x_ref[...]
        rs = lax.rsqrt(jnp.mean(xv * xv, axis=-1, keepdims=True) + EPS)
        o_ref[...] = (xv * rs * g_ref[...]).astype(o_ref.dtype)

    blk = pl.BlockSpec((tr, width), lambda i, j: (i, j))
    return pl.pallas_call(
        body, name=name, grid=(r // tr, nb),
        in_specs=[blk, pl.BlockSpec((1, width), lambda i, j: (0, 0))], out_specs=blk,
        out_shape=jax.ShapeDtypeStruct((r, cols), out_dtype),
        compiler_params=_params("parallel", "parallel"))(x, g.reshape(1, width))


def rmsnorm_bwd(name, dy, x, g, width, out_dtypes=(F32,), add=None):
    r, cols = x.shape
    nb = cols // width
    tr = _rows(r, width)
    n_out = len(out_dtypes)

    def body(*refs):
        dy_ref, x_ref, g_ref = refs[:3]
        add_ref = refs[3] if add is not None else None
        outs = refs[-(n_out + 1):-1]
        dg_ref = refs[-1]
        xv = x_ref[...]
        dyv = dy_ref[...].astype(F32)
        rs = lax.rsqrt(jnp.mean(xv * xv, axis=-1, keepdims=True) + EPS)
        xh = xv * rs
        dyg = dyv * g_ref[...]
        dx = rs * (dyg - xh * jnp.mean(dyg * xh, axis=-1, keepdims=True))
        if add_ref is not None:
            dx = dx + add_ref[...]
        for o in outs:
            o[...] = dx.astype(o.dtype)

        @pl.when((pl.program_id(0) == 0) & (pl.program_id(1) == 0))
        def _():
            dg_ref[...] = jnp.zeros_like(dg_ref)

        dg_ref[...] += jnp.sum(dyv * xh, axis=0, keepdims=True)

    blk = pl.BlockSpec((tr, width), lambda i, j: (i, j))
    vec = pl.BlockSpec((1, width), lambda i, j: (0, 0))
    ins = [dy, x, g.reshape(1, width)] + ([add] if add is not None else [])
    res = pl.pallas_call(
        body, name=name, grid=(r // tr, nb),
        in_specs=[blk, blk, vec] + ([blk] if add is not None else []),
        out_specs=[blk] * n_out + [vec],
        out_shape=[jax.ShapeDtypeStruct((r, cols), dt) for dt in out_dtypes] + [jax.ShapeDtypeStruct((1, width), F32)],
        compiler_params=_params("arbitrary", "arbitrary"))(*ins)
    return tuple(res[:n_out]) + (res[n_out].reshape(width),)


def _rope_angles(pos, dim, theta):
    inv = theta ** (-jnp.arange(0, dim, 2, dtype=F32) / dim)
    ang = pos.astype(F32)[:, None] * inv[None, :]
    return jnp.cos(ang), jnp.sin(ang)


def _rope_tables(s):
    rows = s // GRID_W
    row_pos = jnp.repeat(jnp.arange(rows), GRID_W)
    col_pos = jnp.tile(jnp.arange(GRID_W), rows)
    t_pos = jnp.arange(s)
    z = lambda n: jnp.zeros((s, n), F32)
    o = lambda n: jnp.ones((s, n), F32)
    cr, sr = _rope_angles(row_pos, HEAD // 2, A_THETA)
    cc, sc = _rope_angles(col_pos, HEAD // 2, A_THETA)
    tab_a = (jnp.concatenate([cr, cr, cc, cc], 1), jnp.concatenate([-sr, z(32), -sc, z(32)], 1),
             jnp.concatenate([z(32), sr, z(32), sc], 1), 32)
    cp, sp = _rope_angles(t_pos, B_ROPE_DIM, B_THETA)
    tab_b = (jnp.concatenate([cp, cp, o(96)], 1), jnp.concatenate([-sp, z(112)], 1),
             jnp.concatenate([z(16), sp, z(96)], 1), 16)
    cm, sm = _rope_angles(t_pos, C_ROPE, C_THETA)
    tab_c = (jnp.concatenate([cm, cm, o(64)], 1), jnp.concatenate([-sm, z(96)], 1),
             jnp.concatenate([z(32), sm, z(64)], 1), 32)

    def transposed(tab):
        c, s1, s2, h = tab
        return (c, jnp.roll(s2, -h, axis=1), jnp.roll(s1, h, axis=1), h)

    return {k: (t, transposed(t)) for k, t in (("a", tab_a), ("b", tab_b), ("c", tab_c))}


def rope(name, x, tab, out_dtype):
    c, s1, s2, h = tab
    s, cols = x.shape
    nh = cols // HEAD
    ts = _rows(s, HEAD)

    def body(x_ref, c_ref, s1_ref, s2_ref, o_ref):
        xv = x_ref[...].astype(F32)
        out = xv * c_ref[...] + pltpu.roll(xv, HEAD - h, 1) * s1_ref[...] + pltpu.roll(xv, h, 1) * s2_ref[...]
        o_ref[...] = out.astype(o_ref.dtype)

    blk = pl.BlockSpec((ts, HEAD), lambda i, j: (i, j))
    tb = pl.BlockSpec((ts, HEAD), lambda i, j: (i, 0))
    return pl.pallas_call(
        body, name=name, grid=(s // ts, nh), in_specs=[blk, tb, tb, tb], out_specs=blk,
        out_shape=jax.ShapeDtypeStruct((s, cols), out_dtype),
        compiler_params=_params("parallel", "parallel"))(x, c, s1, s2)


ATTN_TQ = 256


def attn_fwd(name, parts, v, v_group, nheads, scale):
    s = v.shape[0]
    tq = ATTN_TQ
    npart = len(parts)

    def body(*refs):
        v_ref, o_ref, lse_ref = refs[2 * npart:]
        sc = None
        for p in range(npart):
            t = _dot(refs[2 * p][...], refs[2 * p + 1][...], NT)
            sc = t if sc is None else sc + t
        sc = sc * scale
        m = jnp.max(sc, axis=-1, keepdims=True)
        e = jnp.exp(sc - m)
        l = jnp.sum(e, axis=-1, keepdims=True)
        o_ref[...] = _dot((e / l).astype(BF16), v_ref[...])
        lse_ref[...] = jnp.broadcast_to(m + jnp.log(l), (tq, HEAD))

    in_specs, ins = [], []
    for q, qoff, k, kg in parts:
        in_specs.append(pl.BlockSpec((tq, HEAD), lambda h, i, qoff=qoff: (i, qoff + h)))
        in_specs.append(pl.BlockSpec((s, HEAD), lambda h, i, kg=kg: (0, h // kg)))
        ins += [q, k]
    in_specs.append(pl.BlockSpec((s, HEAD), lambda h, i: (0, h // v_group)))
    out_blk = pl.BlockSpec((tq, HEAD), lambda h, i: (i, h))
    return pl.pallas_call(
        body, name=name, grid=(nheads, s // tq), in_specs=in_specs, out_specs=[out_blk, out_blk],
        out_shape=[jax.ShapeDtypeStruct((s, nheads * HEAD), F32)] * 2,
        compiler_params=_params("parallel", "parallel"))(*ins, v)


def attn_bwd(name, parts, v, v_group, lse, do, nheads, scale):
    s = v.shape[0]
    tq = ATTN_TQ
    npart = len(parts)

    def body(*refs):
        v_ref, lse_ref, do_ref = refs[2 * npart:2 * npart + 3]
        outs = refs[2 * npart + 3:]
        dq_refs, dk_refs, dv_ref = outs[:npart], outs[npart:2 * npart], outs[2 * npart]
        h, i = pl.program_id(0), pl.program_id(1)
        sc = None
        for p in range(npart):
            t = _dot(refs[2 * p][...], refs[2 * p + 1][...], NT)
            sc = t if sc is None else sc + t
        pr = jnp.exp(sc * scale - lse_ref[:, 0:1])
        dov = do_ref[...]
        dp = _dot(dov, v_ref[...], NT)
        delta = jnp.sum(pr * dp, axis=-1, keepdims=True)
        ds = (pr * (dp - delta) * scale).astype(BF16)

        @pl.when((i == 0) & (h % v_group == 0))
        def _():
            dv_ref[...] = jnp.zeros_like(dv_ref)

        dv_ref[...] += _dot(pr.astype(BF16), dov, TN)
        for p in range(npart):
            kg = parts[p][3]
            dq_refs[p][...] = _dot(ds, refs[2 * p + 1][...])

            @pl.when((i == 0) & (h % kg == 0))
            def _(p=p):
                dk_refs[p][...] = jnp.zeros_like(dk_refs[p])

            dk_refs[p][...] += _dot(ds, refs[2 * p][...], TN)

    in_specs, ins = [], []
    for q, qoff, k, kg in parts:
        in_specs.append(pl.BlockSpec((tq, HEAD), lambda h, i, qoff=qoff: (i, qoff + h)))
        in_specs.append(pl.BlockSpec((s, HEAD), lambda h, i, kg=kg: (0, h // kg)))
        ins += [q, k]
    hq_blk = pl.BlockSpec((tq, HEAD), lambda h, i: (i, h))
    in_specs += [pl.BlockSpec((s, HEAD), lambda h, i: (0, h // v_group)), hq_blk, hq_blk]
    out_specs = [hq_blk] * npart
    out_shape = [jax.ShapeDtypeStruct((s, nheads * HEAD), F32)] * npart
    for q, qoff, k, kg in parts:
        out_specs.append(pl.BlockSpec((s, HEAD), lambda h, i, kg=kg: (0, h // kg)))
        out_shape.append(jax.ShapeDtypeStruct((s, nheads // kg * HEAD), F32))
    out_specs.append(pl.BlockSpec((s, HEAD), lambda h, i: (0, h // v_group)))
    out_shape.append(jax.ShapeDtypeStruct((s, nheads // v_group * HEAD), F32))
    res = pl.pallas_call(
        body, name=name, grid=(nheads, s // tq), in_specs=in_specs, out_specs=out_specs, out_shape=out_shape,
        compiler_params=_params("arbitrary", "arbitrary"))(*ins, v, lse, do)
    return list(res[:npart]), list(res[npart:2 * npart]), res[2 * npart]


def _band_windows(lf):
    for ib in range(lf // HEAD):
        q0 = ib * HEAD
        yield q0, max(0, q0 - B_HALF), min(lf, q0 + HEAD + B_HALF)


def _band_mask(q0, lo, hi):
    qpos = q0 + lax.broadcasted_iota(jnp.int32, (HEAD, hi - lo), 0)
    kpos = lo + lax.broadcasted_iota(jnp.int32, (HEAD, hi - lo), 1)
    return jnp.abs(qpos - kpos) <= B_HALF


def _band_specs(d, q_cols, q_off):
    s_q = pl.BlockSpec
    return s_q, (lambda r, h: (0, r * (q_cols // HEAD) + q_off + h)), (lambda r, h: (0, r * B_HEADS + h))


def band_fwd(name, q, q_off, k, v, d):
    s = k.shape[0]
    lf = s // d
    scale = HEAD ** -0.5

    def body(q_ref, k_ref, v_ref, o_ref, lse_ref):
        for q0, lo, hi in _band_windows(lf):
            sc = _dot(q_ref[q0:q0 + HEAD, :], k_ref[lo:hi, :], NT) * scale
            sc = jnp.where(_band_mask(q0, lo, hi), sc, -1e30)
            m = jnp.max(sc, axis=-1, keepdims=True)
            e = jnp.exp(sc - m)
            l = jnp.sum(e, axis=-1, keepdims=True)
            o_ref[q0:q0 + HEAD, :] = _dot((e / l).astype(BF16), v_ref[lo:hi, :])
            lse_ref[q0:q0 + HEAD, :] = jnp.broadcast_to(m + jnp.log(l), (HEAD, HEAD))

    _, qmap, kmap = _band_specs(d, q.shape[1], q_off)
    blk = lambda f: pl.BlockSpec((lf, HEAD), f)
    o, lse = pl.pallas_call(
        body, name=name, grid=(d, B_HEADS), in_specs=[blk(qmap), blk(kmap), blk(kmap)],
        out_specs=[blk(kmap), blk(kmap)], out_shape=[jax.ShapeDtypeStruct((lf, d * B_W), F32)] * 2,
        compiler_params=_params("parallel", "parallel"))(
            q.reshape(lf, d * q.shape[1]), k.reshape(lf, d * B_W), v.reshape(lf, d * B_W))
    return o.reshape(s, B_W), lse.reshape(s, B_W)


def band_bwd(name, q, q_off, k, v, dy, lse, delta, d):
    s = k.shape[0]
    lf = s // d
    scale = HEAD ** -0.5

    def body(q_ref, k_ref, v_ref, dy_ref, lse_ref, dl_ref, dq_ref, dk_ref, dv_ref):
        dk_ref[...] = jnp.zeros_like(dk_ref)
        dv_ref[...] = jnp.zeros_like(dv_ref)
        for q0, lo, hi in _band_windows(lf):
            qv, kv, vv = q_ref[q0:q0 + HEAD, :], k_ref[lo:hi, :], v_ref[lo:hi, :]
            dyv = dy_ref[q0:q0 + HEAD, :]
            sc = _dot(qv, kv, NT) * scale
            pr = jnp.where(_band_mask(q0, lo, hi), jnp.exp(sc - lse_ref[q0:q0 + HEAD, 0:1]), 0.0)
            dp = _dot(dyv, vv, NT)
            ds = (pr * (dp - dl_ref[q0:q0 + HEAD, 0:1]) * scale).astype(BF16)
            dv_ref[lo:hi, :] += _dot(pr.astype(BF16), dyv, TN)
            dq_ref[q0:q0 + HEAD, :] = _dot(ds, kv)
            dk_ref[lo:hi, :] += _dot(ds, qv, TN)

    _, qmap, kmap = _band_specs(d, q.shape[1], q_off)
    blk = lambda f: pl.BlockSpec((lf, HEAD), f)
    fold = lambda a: a.reshape(lf, d * a.shape[1])
    res = pl.pallas_call(
        body, name=name, grid=(d, B_HEADS), in_specs=[blk(qmap)] + [blk(kmap)] * 5,
        out_specs=[blk(kmap)] * 3, out_shape=[jax.ShapeDtypeStruct((lf, d * B_W), F32)] * 3,
        compiler_params=_params("parallel", "parallel"))(fold(q), fold(k), fold(v), fold(dy), fold(lse), fold(delta))
    return tuple(r.reshape(s, B_W) for r in res)


def mix_fwd(name, outs, lses):
    s, w = outs[0].shape
    ts = _rows(s, w)

    def body(o0, o1, o2, l0, l1, l2, y_ref, lse_ref):
        a, b, c = l0[...], l1[...], l2[...]
        m = jnp.maximum(jnp.maximum(a, b), c)
        ea, eb, ec = jnp.exp(a - m), jnp.exp(b - m), jnp.exp(c - m)
        den = ea + eb + ec
        y_ref[...] = (ea / den) * o0[...] + (eb / den) * o1[...] + (ec / den) * o2[...]
        lse_ref[...] = m + jnp.log(den)

    blk = pl.BlockSpec((ts, w), lambda i: (i, 0))
    return pl.pallas_call(
        body, name=name, grid=(s // ts,), in_specs=[blk] * 6, out_specs=[blk, blk],
        out_shape=[jax.ShapeDtypeStruct((s, w), F32)] * 2, compiler_params=_params("parallel"))(*outs, *lses)


def head_rowdot(name, dy, y):
    s, cols = y.shape
    ts = _rows(s, HEAD)

    def body(dy_ref, y_ref, dl_ref, dyb_ref):
        dyv = dy_ref[...]
        dl_ref[...] = jnp.broadcast_to(jnp.sum(dyv * y_ref[...], axis=-1, keepdims=True), (ts, HEAD))
        dyb_ref[...] = dyv.astype(BF16)

    blk = pl.BlockSpec((ts, HEAD), lambda i, j: (i, j))
    return pl.pallas_call(
        body, name=name, grid=(s // ts, cols // HEAD), in_specs=[blk, blk], out_specs=[blk, blk],
        out_shape=[jax.ShapeDtypeStruct((s, cols), F32), jax.ShapeDtypeStruct((s, cols), BF16)],
        compiler_params=_params("parallel", "parallel"))(dy, y)


def add3(name, a, b, c, out_dtype=F32):
    s, w = a.shape
    ts = _rows(s, w)

    def body(a_ref, b_ref, c_ref, o_ref):
        o_ref[...] = (a_ref[...] + b_ref[...] + c_ref[...]).astype(o_ref.dtype)

    blk = pl.BlockSpec((ts, w), lambda i: (i, 0))
    return pl.pallas_call(
        body, name=name, grid=(s // ts,), in_specs=[blk] * 3, out_specs=blk,
        out_shape=jax.ShapeDtypeStruct((s, w), out_dtype), compiler_params=_params("parallel"))(a, b, c)


FFN_TN = 256


def _conv(h, cw):
    s = h.shape[0]
    row = lax.broadcasted_iota(jnp.int32, h.shape, 0)
    prev = jnp.where(row == 0, 0.0, pltpu.roll(h, 1, 0))
    nxt = jnp.where(row == s - 1, 0.0, pltpu.roll(h, s - 1, 0))
    return prev * cw[0:1, :] + h * cw[1:2, :] + nxt * cw[2:3, :] + cw[3:4, :], prev, nxt


def _sigmoid(x):
    return 1.0 / (1.0 + jnp.exp(-x))


def ffn_up(name, xn, w_up, cwb):
    s, dm = xn.shape
    quarter = w_up.shape[3]
    fh = 2 * quarter
    tn = FFN_TN
    per = quarter // tn

    def body(x_ref, w_ref, cw_ref, h_ref, act_ref):
        xv = x_ref[...]
        hg = _dot(xv, w_ref[0])
        hu = _dot(xv, w_ref[1])
        h_ref[0] = hg
        h_ref[1] = hu
        gc, _, _ = _conv(hg, cw_ref[0])
        uc, _, _ = _conv(hu, cw_ref[1])
        act_ref[...] = (gc * _sigmoid(gc) * uc).astype(BF16)

    return pl.pallas_call(
        body, name=name, grid=(fh // tn,),
        in_specs=[pl.BlockSpec((s, dm), lambda t: (0, 0)),
                  pl.BlockSpec((2, None, dm, tn), lambda t: (0, t // per, 0, t % per)),
                  pl.BlockSpec((2, 8, tn), lambda t: (0, 0, t))],
        out_specs=[pl.BlockSpec((2, s, tn), lambda t: (0, 0, t)), pl.BlockSpec((s, tn), lambda t: (0, t))],
        out_shape=[jax.ShapeDtypeStruct((2, s, fh), F32), jax.ShapeDtypeStruct((s, fh), BF16)],
        compiler_params=_params("parallel"))(xn, w_up, cwb)


def ffn_gate_bwd(name, h, dact, cwb):
    _, s, fh = h.shape
    tn = FFN_TN

    def body(h_ref, da_ref, cw_ref, dh_ref, dcw_ref):
        gc, gp, gn = _conv(h_ref[0], cw_ref[0])
        uc, up, un = _conv(h_ref[1], cw_ref[1])
        sg = _sigmoid(gc)
        da = da_ref[...]
        dgc = da * uc * (sg * (1.0 + gc * (1.0 - sg)))
        duc = da * (gc * sg)
        row = lax.broadcasted_iota(jnp.int32, da.shape, 0)
        for idx, (hv, prev, nxt, dc) in enumerate(((h_ref[0], gp, gn, dgc), (h_ref[1], up, un, duc))):
            cw = cw_ref[idx]
            from_next = jnp.where(row == s - 1, 0.0, pltpu.roll(dc, s - 1, 0))
            from_prev = jnp.where(row == 0, 0.0, pltpu.roll(dc, 1, 0))
            dh_ref[idx] = (from_next * cw[0:1, :] + dc * cw[1:2, :] + from_prev * cw[2:3, :]).astype(BF16)
            dcw_ref[idx, 0:1, :] = jnp.sum(prev * dc, axis=0, keepdims=True)
            dcw_ref[idx, 1:2, :] = jnp.sum(hv * dc, axis=0, keepdims=True)
            dcw_ref[idx, 2:3, :] = jnp.sum(nxt * dc, axis=0, keepdims=True)
            dcw_ref[idx, 3:4, :] = jnp.sum(dc, axis=0, keepdims=True)
            dcw_ref[idx, 4:8, :] = jnp.zeros((4, tn), F32)

    return pl.pallas_call(
        body, name=name, grid=(fh // tn,),
        in_specs=[pl.BlockSpec((2, s, tn), lambda t: (0, 0, t)), pl.BlockSpec((s, tn), lambda t: (0, t)),
                  pl.BlockSpec((2, 8, tn), lambda t: (0, 0, t))],
        out_specs=[pl.BlockSpec((2, s, tn), lambda t: (0, 0, t)), pl.BlockSpec((2, 8, tn), lambda t: (0, 0, t))],
        out_shape=[jax.ShapeDtypeStruct((2, s, fh), BF16), jax.ShapeDtypeStruct((2, 8, fh), F32)],
        compiler_params=_params("parallel"))(h, dact, cwb)


def ffn_dx(name, dh, w_up):
    _, s, fh = dh.shape
    dm, quarter = w_up.shape[2], w_up.shape[3]
    tk = _tile(quarter, 2816)
    per = quarter // tk
    tm, tn = _tile(s, 1024), _tile(dm, 1024)
    grid = (s // tm, dm // tn, 4 * per)
    a_spec = pl.BlockSpec((None, tm, tk), lambda i, j, k: (k // (2 * per), i, k % (2 * per)))
    b_spec = pl.BlockSpec((None, None, tn, tk), lambda i, j, k: (k // (2 * per), (k // per) % 2, j, k % per))
    o_spec = pl.BlockSpec((tm, tn), lambda i, j, k: (i, j))
    return _matmul_call(name, NT, grid, a_spec, b_spec, o_spec, jax.ShapeDtypeStruct((s, dm), F32), (tm, tn))(dh, w_up)


def ffn_dw_up(name, xn, dh):
    _, s, fh = dh.shape
    dm = xn.shape[1]
    quarter = fh // 2
    tn = _tile(quarter, 1536)
    per = quarter // tn
    tm, tk = _tile(dm, 1024), _tile(s, 2048)
    grid = (dm // tm, 4 * per, s // tk)
    a_spec = pl.BlockSpec((tk, tm), lambda i, j, k: (k, i))
    b_spec = pl.BlockSpec((None, tk, tn), lambda i, j, k: (j // (2 * per), k, j % (2 * per)))
    o_spec = pl.BlockSpec((None, None, tm, tn), lambda i, j, k: (j // (2 * per), (j // per) % 2, i, j % per))
    return _matmul_call(name, TN, grid, a_spec, b_spec, o_spec,
                        jax.ShapeDtypeStruct((2, 2, dm, quarter), F32), (tm, tn))(xn, dh)


def loss_head(name, y, target):
    s, dm = y.shape
    ts = _rows(s, dm)

    def body(y_ref, t_ref, loss_ref, dy_ref, acc):
        i = pl.program_id(0)

        @pl.when(i == 0)
        def _():
            acc[...] = jnp.zeros_like(acc)

        err = y_ref[...] - t_ref[...]
        dy_ref[...] = err * (1.0 / dm)
        acc[...] += jnp.sum(err * err, axis=0, keepdims=True)

        @pl.when(i == s // ts - 1)
        def _():
            loss_ref[...] = jnp.broadcast_to(jnp.sum(acc[...], axis=-1, keepdims=True) * (0.5 / dm), (1, LANES))

    blk = pl.BlockSpec((ts, dm), lambda i: (i, 0))
    loss, dy = pl.pallas_call(
        body, name=name, grid=(s // ts,), in_specs=[blk, blk],
        out_specs=[pl.BlockSpec((1, LANES), lambda i: (0, 0)), blk],
        out_shape=[jax.ShapeDtypeStruct((1, LANES), F32), jax.ShapeDtypeStruct((s, dm), F32)],
        scratch_shapes=[pltpu.VMEM((1, dm), F32)], compiler_params=_params("arbitrary"))(y, target)
    return loss[0, 0], dy


def attn_fwd_c(name, qc, kvc, krc, scale):
    s = qc.shape[0]
    tq = ATTN_TQ

    def body(qn_ref, qr_ref, kn_ref, kr_ref, v_ref, o_ref, lse_ref):
        sc = (_dot(qn_ref[...], kn_ref[...], NT) + _dot(qr_ref[...], kr_ref[...], NT)) * scale
        m = jnp.max(sc, axis=-1, keepdims=True)
        e = jnp.exp(sc - m)
        l = jnp.sum(e, axis=-1, keepdims=True)
        o_ref[...] = _dot((e / l).astype(BF16), v_ref[...])
        lse_ref[...] = jnp.broadcast_to(m + jnp.log(l), (tq, HEAD))

    qb = lambda off: pl.BlockSpec((tq, HEAD), lambda h, i: (i, off + h))
    kb = lambda f: pl.BlockSpec((s, HEAD), f)
    out_blk = pl.BlockSpec((tq, HEAD), lambda h, i: (i, h))
    return pl.pallas_call(
        body, name=name, grid=(C_HEADS, s // tq),
        in_specs=[qb(0), qb(C_HEADS), kb(lambda h, i: (0, 2 * h)), kb(lambda h, i: (0, 0)), kb(lambda h, i: (0, 2 * h + 1))],
        out_specs=[out_blk, out_blk], out_shape=[jax.ShapeDtypeStruct((s, C_W), F32)] * 2,
        compiler_params=_params("parallel", "parallel"))(qc, qc, kvc, krc, kvc)


def attn_bwd_c(name, qc, kvc, krc, lse, do, scale):
    s = qc.shape[0]
    tq = ATTN_TQ

    def body(qn_ref, qr_ref, kn_ref, kr_ref, v_ref, lse_ref, do_ref, dqn_ref, dqr_ref, dkn_ref, dv_ref, dkr_ref):
        h, i = pl.program_id(0), pl.program_id(1)
        sc = (_dot(qn_ref[...], kn_ref[...], NT) + _dot(qr_ref[...], kr_ref[...], NT)) * scale
        pr = jnp.exp(sc - lse_ref[:, 0:1])
        dov = do_ref[...]
        dp = _dot(dov, v_ref[...], NT)
        delta = jnp.sum(pr * dp, axis=-1, keepdims=True)
        ds = (pr * (dp - delta) * scale).astype(BF16)

        @pl.when(i == 0)
        def _():
            dv_ref[...] = jnp.zeros_like(dv_ref)
            dkn_ref[...] = jnp.zeros_like(dkn_ref)

        @pl.when((i == 0) & (h == 0))
        def _():
            dkr_ref[...] = jnp.zeros_like(dkr_ref)

        dv_ref[...] += _dot(pr.astype(BF16), dov, TN)
        dqn_ref[...] = _dot(ds, kn_ref[...])
        dqr_ref[...] = _dot(ds, kr_ref[...])
        dkn_ref[...] += _dot(ds, qn_ref[...], TN)
        dkr_ref[...] += _dot(ds, qr_ref[...], TN)

    qb = lambda off: pl.BlockSpec((tq, HEAD), lambda h, i: (i, off + h))
    kb = lambda f: pl.BlockSpec((s, HEAD), f)
    hq = pl.BlockSpec((tq, HEAD), lambda h, i: (i, h))
    kn_map, v_map, kr_map = (lambda h, i: (0, 2 * h)), (lambda h, i: (0, 2 * h + 1)), (lambda h, i: (0, 0))
    dqc_shape = jax.ShapeDtypeStruct((s, 2 * C_W), F32)
    dqn, dqr, dkn, dv, dkr = pl.pallas_call(
        body, name=name, grid=(C_HEADS, s // tq),
        in_specs=[qb(0), qb(C_HEADS), kb(kn_map), kb(kr_map), kb(v_map), hq, hq],
        out_specs=[hq, hq, kb(lambda h, i: (0, h)), kb(lambda h, i: (0, h)), kb(kr_map)],
        out_shape=[jax.ShapeDtypeStruct((s, C_W), F32)] * 4 + [jax.ShapeDtypeStruct((s, HEAD), F32)],
        compiler_params=_params("arbitrary", "arbitrary"))(qc, qc, kvc, krc, kvc, lse, do)
    del dqc_shape
    return dqn, dqr, dkn, dv, dkr


def _layer_fwd(x, p, tabs):
    x1, sv = _mixers_fwd(x, p, tabs)
    return _ffn_fwd(x1, p, sv), sv


def _mixers_fwd(x, p, tabs):
    sv = {"x": x}
    hn = rmsnorm_fwd("attn_norm", x, p["attn_norm"], x.shape[1], BF16)
    proj = matmul("in_proj", hn, p["w_in_t"], "nt", F32, tm=512, tk=512)
    sv["hn"] = hn
    aq, ak = proj[:, O_AQ:O_AK], proj[:, O_AK:O_AV]
    av = proj[:, O_AV:O_BQ].astype(BF16)
    bq, bk = proj[:, O_BQ:O_BK], proj[:, O_BK:O_BV]
    bv = proj[:, O_BV:O_CQ].astype(BF16)
    cq, ckv = proj[:, O_CQ:O_CKV], proj[:, O_CKV:O_CKR]
    ckr = jnp.pad(proj[:, O_CKR:], ((0, 0), (0, HEAD - C_ROPE)))
    sv.update(aq=aq, ak=ak, av=av, bv=bv, cq=cq, ckv=ckv)

    qa = rope("a_q_rope", rmsnorm_fwd("a_q_norm", aq, p["a_q_norm"], HEAD, F32), tabs["a"][0], BF16)
    ka = rope("a_k_rope", rmsnorm_fwd("a_k_norm", ak, p["a_k_norm"], HEAD, F32), tabs["a"][0], BF16)
    ya, lse_a = attn_fwd("a_attn", [(qa, 0, ka, A_HEADS // A_KV)], av, A_HEADS // A_KV, A_HEADS, HEAD ** -0.5)
    sv.update(qa=qa, ka=ka, ya=ya, lse_a=lse_a)

    qb = rope("b_q_rope", bq, tabs["b"][0], BF16)
    kb = rope("b_k_rope", bk, tabs["b"][0], BF16)
    outs, lses = [], []
    for g, d in enumerate(B_DILATIONS):
        o, l = band_fwd(f"b_band{g}", qb, g * B_HEADS, kb, bv, d)
        outs.append(o)
        lses.append(l)
    yb, lse_b = mix_fwd("b_mix", outs, lses)
    sv.update(qb=qb, kb=kb, yb=yb, lse_b=lse_b)

    cqn = rmsnorm_fwd("c_q_norm", cq, p["c_q_norm"], C_RANK, BF16)
    qc_raw = matmul("c_uq", cqn, p["w_uq_t"], "nt", F32)
    qc = jnp.concatenate([qc_raw[:, :C_W].astype(BF16), rope("c_q_rope", qc_raw[:, C_W:], tabs["c"][0], BF16)], axis=1)
    ckvn = rmsnorm_fwd("c_kv_norm", ckv, p["c_kv_norm"], C_RANK, BF16)
    kvc = matmul("c_ukv", ckvn, p["w_ukv"], "nn", BF16)
    krc = rope("c_k_rope", ckr, tabs["c"][0], BF16)
    c_scale = (HEAD + C_ROPE) ** -0.5
    yc, lse_c = attn_fwd_c("c_attn", qc, kvc, krc, c_scale)
    sv.update(cqn=cqn, ckvn=ckvn, qc=qc, kvc=kvc, krc=krc, yc=yc, lse_c=lse_c)

    g_a, g_b, g_c = p["out_norm"][:A_W], p["out_norm"][A_W:A_W + B_W], p["out_norm"][A_W + B_W:]
    y = jnp.concatenate([rmsnorm_fwd("out_norm_a", ya, g_a, A_W, BF16), rmsnorm_fwd("out_norm_b", yb, g_b, B_W, BF16),
                         rmsnorm_fwd("out_norm_c", yc, g_c, C_W, BF16)], axis=1)
    x1 = matmul("out_proj", y, p["w_out"], "nn", F32, add=x)
    sv.update(y=y, x1=x1)
    return x1, sv


def _ffn_fwd(x1, p, sv):
    xn = rmsnorm_fwd("ffn_norm", x1, p["ffn_norm"], x1.shape[1], BF16)
    h, act = ffn_up("ffn_up", xn, p["w_up"], p["cwb"])
    x2 = matmul("ffn_down", act, p["w_down"], "nn", F32, add=x1)
    sv.update(xn=xn, h=h, act=act)
    return x2


def _layer_bwd(dx2, sv, p, tabs):
    dxn, g = _ffn_bwd(dx2, sv, p)
    dx1, dy, g_out = _out_proj_bwd(dxn, dx2, sv, p)
    dhn, g_mix = _mixers_bwd(dy, sv, p, tabs)
    dx, g_norm = _attn_norm_bwd(dhn, dx1, sv, p)
    return dx, {**g, **g_out, **g_mix, **g_norm}


def _ffn_bwd(dx2, sv, p, after=0.0):
    g = {}
    dx2b = (dx2 + after).astype(BF16)
    dact = matmul("ffn_down_dx", dx2b, p["w_down"], "nt", F32)
    g["w_down"] = matmul("ffn_down_dw", sv["act"], dx2b, "tn", F32)
    dh, g["cwb"] = ffn_gate_bwd("ffn_gate_bwd", sv["h"], dact, p["cwb"])
    dxn = ffn_dx("ffn_up_dx", dh, p["w_up"])
    g["w_up"] = ffn_dw_up("ffn_up_dw", sv["xn"], dh)
    return dxn, g


def _out_proj_bwd(dxn, dx2, sv, p, after=0.0):
    g = {}
    dm = dx2.shape[1]
    dx1, dx1b, g["ffn_norm"] = rmsnorm_bwd("ffn_norm_bwd", dxn, sv["x1"], p["ffn_norm"] + after, dm, (F32, BF16), add=dx2)
    dy = matmul("out_proj_dx", dx1b, p["w_out"], "nt", F32)
    g["w_out"] = matmul("out_proj_dw", sv["y"], dx1b, "tn", F32)
    return dx1, dy, g


def _attn_norm_bwd(dhn, dx1, sv, p, after=0.0):
    dx, d_gain = rmsnorm_bwd("attn_norm_bwd", dhn, sv["x"], p["attn_norm"] + after, dhn.shape[1], (F32,), add=dx1)
    return dx, {"attn_norm": d_gain}


def _mixers_bwd(dy, sv, p, tabs, after=0.0):
    g = {}
    out_norm = p["out_norm"] + after
    g_a, g_b, g_c = out_norm[:A_W], out_norm[A_W:A_W + B_W], out_norm[A_W + B_W:]
    dya, dga = rmsnorm_bwd("out_norm_a_bwd", dy[:, :A_W], sv["ya"], g_a, A_W, (BF16,))
    dyb, dgb = rmsnorm_bwd("out_norm_b_bwd", dy[:, A_W:A_W + B_W], sv["yb"], g_b, B_W, (F32,))
    dyc, dgc = rmsnorm_bwd("out_norm_c_bwd", dy[:, A_W + B_W:], sv["yc"], g_c, C_W, (BF16,))
    g["out_norm"] = jnp.concatenate([dga, dgb, dgc])

    c_scale = (HEAD + C_ROPE) ** -0.5
    dqn, dqr, dkn, dvc, dkr = attn_bwd_c("c_attn_bwd", sv["qc"], sv["kvc"], sv["krc"], sv["lse_c"], dyc, c_scale)
    s = dy.shape[0]
    dqc = jnp.concatenate([dqn.astype(BF16), rope("c_q_rope_bwd", dqr, tabs["c"][1], BF16)], axis=1)
    dcqn = matmul("c_uq_dx", dqc, p["w_uq_t"], "nn", F32)
    g["w_uq"] = matmul("c_uq_dw", dqc, sv["cqn"], "tn", F32)
    dcq, g["c_q_norm"] = rmsnorm_bwd("c_q_norm_bwd", dcqn, sv["cq"], p["c_q_norm"], C_RANK, (BF16,))
    dkvc = jnp.stack([dkn.reshape(s, C_HEADS, HEAD), dvc.reshape(s, C_HEADS, HEAD)], axis=2).reshape(s, 2 * C_W).astype(BF16)
    dckvn = matmul("c_ukv_dx", dkvc, p["w_ukv"], "nt", F32)
    g["w_ukv"] = matmul("c_ukv_dw", sv["ckvn"], dkvc, "tn", F32)
    dckv, g["c_kv_norm"] = rmsnorm_bwd("c_kv_norm_bwd", dckvn, sv["ckv"], p["c_kv_norm"], C_RANK, (BF16,))
    dckr = rope("c_k_rope_bwd", dkr, tabs["c"][1], BF16)

    delta_b, dybb = head_rowdot("b_delta", dyb, sv["yb"])
    dqs, dks, dvs = [], [], []
    for gi, d in enumerate(B_DILATIONS):
        dq_g, dk_g, dv_g = band_bwd(f"b_band{gi}_bwd", sv["qb"], gi * B_HEADS, sv["kb"], sv["bv"], dybb,
                                    sv["lse_b"], delta_b, d)
        dqs.append(dq_g)
        dks.append(dk_g)
        dvs.append(dv_g)
    dbq = rope("b_q_rope_bwd", jnp.concatenate(dqs, axis=1), tabs["b"][1], BF16)
    dbk = rope("b_k_rope_bwd", add3("b_dk_sum", *dks), tabs["b"][1], BF16)
    dbv = add3("b_dv_sum", *dvs, out_dtype=BF16)

    kg = A_HEADS // A_KV
    (dqa,), (dka,), dva = attn_bwd("a_attn_bwd", [(sv["qa"], 0, sv["ka"], kg)], sv["av"], kg, sv["lse_a"], dya,
                                   A_HEADS, HEAD ** -0.5)
    daq, g["a_q_norm"] = rmsnorm_bwd("a_q_norm_bwd", rope("a_q_rope_bwd", dqa, tabs["a"][1], F32), sv["aq"],
                                     p["a_q_norm"], HEAD, (BF16,))
    dak, g["a_k_norm"] = rmsnorm_bwd("a_k_norm_bwd", rope("a_k_rope_bwd", dka, tabs["a"][1], F32), sv["ak"],
                                     p["a_k_norm"], HEAD, (BF16,))

    dproj = jnp.concatenate([daq, dak, dva.astype(BF16), dbq, dbk, dbv, dcq, dckv, dckr[:, :C_ROPE]], axis=1)
    dhn = matmul("in_proj_dx", dproj, p["w_in_t"], "nn", F32, tm=512)
    g["w_in"] = matmul("in_proj_dw", dproj, sv["hn"], "tn", F32, tn=512, tk=512)
    return dhn, g


def _local_step(x, target, layers, final_norm, tabs):
    saved = []
    for p in layers:
        x, sv = _layer_fwd(x, p, tabs)
        saved.append(sv)
    dm = x.shape[1]
    yf = rmsnorm_fwd("final_norm", x, final_norm, dm, F32)
    loss, dyf = loss_head("loss_head", yf, target)
    dx, d_final = rmsnorm_bwd("final_norm_bwd", dyf, x, final_norm, dm, (F32,))
    grads = [None] * len(layers)
    for l in reversed(range(len(layers))):
        dx, grads[l] = _layer_bwd(dx, saved[l], layers[l], tabs)
    return loss, dx, grads, d_final


HBM = pl.BlockSpec(memory_space=pl.ANY)


def _place():
    x, y, c = lax.axis_index("x"), lax.axis_index("y"), lax.axis_index("c")
    others = [(1 - x, y), (x, 1 - y), (1 - x, 1 - y)]
    return x, y, c, 2 * x + y, others


def _remote(src, dst, send_sem, recv_sem, device):
    return pltpu.make_async_remote_copy(src_ref=src, dst_ref=dst, send_sem=send_sem, recv_sem=recv_sem,
                                        device_id=device, device_id_type=MESH)


def gather_small(name, small):
    def body(in_ref, out_ref, send_sems, recv_sems, local_sem):
        x, y, c, chip, others = _place()
        mine = pltpu.make_async_copy(in_ref, out_ref.at[chip], local_sem)
        mine.start()
        copies = []
        for k, (ox, oy) in enumerate(others):
            cp = _remote(in_ref, out_ref.at[chip], send_sems.at[k], recv_sems.at[k], (ox, oy, c))
            cp.start()
            copies.append(cp)
        for k, (ox, oy) in enumerate(others):
            landed = out_ref.at[2 * ox + oy]
            _remote(landed, landed, send_sems.at[k], recv_sems.at[k], (ox, oy, c)).wait_recv()
        for cp in copies:
            cp.wait_send()
        mine.wait()

    return pl.pallas_call(
        body, name=name, in_specs=[HBM], out_specs=HBM, out_shape=jax.ShapeDtypeStruct((4,) + small.shape, small.dtype),
        scratch_shapes=[pltpu.SemaphoreType.DMA((3,)), pltpu.SemaphoreType.DMA((3,)), pltpu.SemaphoreType.DMA],
        compiler_params=pltpu.CompilerParams(has_side_effects=True))(small)


IN_HBM = pl.BlockSpec(memory_space=pltpu.HBM)
SEMS = pl.BlockSpec(memory_space=pltpu.SEMAPHORE)
DATAFLOW = pltpu.SideEffectType.DATAFLOW_SIDE_EFFECTING


BF16_ROWS_PER_TILE = 16


def _halves_by_rows(shape):
    return (shape[-2] // 2) % BF16_ROWS_PER_TILE == 0


def _half(shape, core):
    if _halves_by_rows(shape):
        size = shape[-2] // 2
        return (pl.ds(core * size, size), slice(None))
    size = shape[-1] // 2
    return (slice(None), pl.ds(core * size, size))


def _half_shape(shape):
    r, c = shape[-2:]
    return (r // 2, c) if _halves_by_rows(shape) else (r, c // 2)


def _gather_plan(bufs):
    x, y, c, chip, others = _place()
    plan = []
    for t, ref in enumerate(bufs):
        mine = _half(ref.shape, c)
        for k, (ox, oy) in enumerate(others):
            plan.append((3 * t + k, (ox, oy, c), ref.at[(chip,) + mine], ref.at[(chip,) + mine],
                         ref.at[(2 * ox + oy,) + mine]))
    return plan


def _exchange_plan(bufs):
    x, y, c, chip, others = _place()
    n = len(bufs) // 2
    plan = []
    for t in range(n):
        for k, (ox, oy) in enumerate(others):
            plan.append((3 * t + k, (ox, oy, c), bufs[t].at[2 * ox + oy], bufs[n + t].at[chip], bufs[n + t].at[2 * ox + oy]))
    return plan


def _sibling_plan(bufs):
    x, y, c, chip, others = _place()
    n = len(bufs) // 2
    plan = []
    for t in range(n):
        plan.append((t, (x, y, 1 - c), bufs[t].at[(slice(None),) + _half(bufs[t].shape, 1 - c)], bufs[n + t], bufs[n + t]))
    return plan


PLAN_COPIES = {_gather_plan: lambda n: 3 * n, _exchange_plan: lambda n: 3 * (n // 2), _sibling_plan: lambda n: n // 2}


def split_start(name, bufs, plan_of, after):
    n = len(bufs)

    def body(*refs):
        ins = refs[:n]
        send_sems, recv_sems = refs[n + 1], refs[n + 2]
        token = refs[-1]
        for idx, peer, src, dst, _ in plan_of(ins):
            _remote(src, dst, send_sems.at[idx], recv_sems.at[idx], peer).start()
        token[...] = jnp.zeros_like(token)

    copies = PLAN_COPIES[plan_of](n)
    res = pl.pallas_call(
        body, name=name, in_specs=[IN_HBM] * n + [HBM],
        out_specs=(SEMS, SEMS) + (IN_HBM,) * n + (pl.BlockSpec(memory_space=pltpu.VMEM),),
        out_shape=(pltpu.SemaphoreType.DMA((copies,)), pltpu.SemaphoreType.DMA((copies,)))
        + tuple(pltpu.HBM(b.shape, b.dtype) for b in bufs) + (jax.ShapeDtypeStruct((8, LANES), F32),),
        input_output_aliases={t: 2 + t for t in range(n)},
        compiler_params=pltpu.CompilerParams(has_side_effects=DATAFLOW))(
            *[pltpu.with_memory_space_constraint(b, pltpu.HBM) for b in bufs], after)
    return res[0], res[1], list(res[2:2 + n]), res[-1]


def split_wait(name, send_sems, recv_sems, flying, plan_of, after):
    n = len(flying)

    def body(*refs):
        ins = refs[:n]
        send_ref, recv_ref = refs[n], refs[n + 1]
        for idx, peer, src, dst, landing in plan_of(ins):
            _remote(src, dst, send_ref.at[idx], recv_ref.at[idx], peer).wait_send()
            _remote(landing, landing, send_ref.at[idx], recv_ref.at[idx], peer).wait_recv()

    return list(pl.pallas_call(
        body, name=name, in_specs=[IN_HBM] * n + [SEMS, SEMS] + [HBM] * len(after), out_specs=(IN_HBM,) * n,
        out_shape=tuple(pltpu.HBM(b.shape, b.dtype) for b in flying),
        input_output_aliases={t: t for t in range(n)},
        compiler_params=pltpu.CompilerParams(has_side_effects=DATAFLOW))(*flying, send_sems, recv_sems, *after))


def forward_halves(name, slots):
    n = len(slots)

    def body(*refs):
        ins, outs = refs[:n], refs[n:2 * n]
        send_sems, recv_sems = refs[2 * n:]
        x, y, c, chip, others = _place()
        sibling = (x, y, 1 - c)
        copies = []
        for t in range(n):
            mine = _half(slots[t].shape, c)
            for k, (ox, oy) in enumerate(others):
                cp = _remote(ins[t].at[(2 * ox + oy,) + mine], outs[t].at[(2 * ox + oy,) + mine], send_sems.at[t, k],
                             recv_sems.at[t, k], sibling)
                cp.start()
                copies.append(cp)
        for t in range(n):
            for k, (ox, oy) in enumerate(others):
                theirs = outs[t].at[(2 * ox + oy,) + _half(slots[t].shape, 1 - c)]
                _remote(theirs, theirs, send_sems.at[t, k], recv_sems.at[t, k], sibling).wait_recv()
        for cp in copies:
            cp.wait_send()

    return list(pl.pallas_call(
        body, name=name, in_specs=[HBM] * n, out_specs=[HBM] * n,
        out_shape=[jax.ShapeDtypeStruct(s.shape, s.dtype) for s in slots],
        input_output_aliases={t: t for t in range(n)},
        scratch_shapes=[pltpu.SemaphoreType.DMA((n, 3))] * 2,
        compiler_params=pltpu.CompilerParams(has_side_effects=True))(*slots))


def _half_block(shape, tr):
    r, c = shape[-2:]
    if _halves_by_rows(shape):
        per = r // 2 // tr
        return tr, c, lambda i, core: (core * per + i, 0)
    return tr, c // 2, lambda i, core: (i, core)


def add_half(name, place, grad, theirs):
    hr, hc = _half_shape(grad.shape)
    tr = _rows(hr, hc)
    br, bc, at = _half_block(grad.shape, tr)

    def body(place_ref, g_ref, t_ref, o_ref):
        o_ref[...] = (g_ref[...] + t_ref[...]).astype(o_ref.dtype)

    whole = pl.BlockSpec((None, br, bc), lambda j, i, pr: (j, i, 0))
    return pl.pallas_call(
        body, name=name,
        grid_spec=pltpu.PrefetchScalarGridSpec(
            num_scalar_prefetch=1, grid=(4, hr // tr),
            in_specs=[pl.BlockSpec((None, br, bc), lambda j, i, pr: (j,) + at(i, pr[0])), whole],
            out_specs=whole),
        out_shape=jax.ShapeDtypeStruct((4, hr, hc), BF16),
        compiler_params=_params("parallel", "parallel"))(place, grad, theirs)


def sum_chips(name, place, own, parts, shard_shape):
    _, hr, hc = parts.shape
    tr = _rows(hr, hc)
    br, bc, at = _half_block(shard_shape, tr)

    def body(place_ref, own_ref, p1, p2, p3, o_ref):
        o_ref[...] = own_ref[...].astype(F32) + p1[...].astype(F32) + p2[...].astype(F32) + p3[...].astype(F32)

    def slot(k):
        return pl.BlockSpec((None, br, bc), lambda i, pr: (lax.rem(pr[1] + k, 4), i, 0))

    return pl.pallas_call(
        body, name=name,
        grid_spec=pltpu.PrefetchScalarGridSpec(
            num_scalar_prefetch=1, grid=(hr // tr,), in_specs=[slot(0), slot(1), slot(2), slot(3)],
            out_specs=pl.BlockSpec((br, bc), lambda i, pr: at(i, pr[0]))),
        out_shape=jax.ShapeDtypeStruct(tuple(shard_shape), F32),
        compiler_params=_params("parallel"))(place, own, parts, parts, parts)


def join_halves(name, halves):
    n = len(halves)

    def body(*refs):
        ins, outs = refs[:n], refs[n:2 * n]
        send_sems, recv_sems = refs[2 * n:]
        x, y, c, _, _ = _place()
        copies = []
        for t in range(n):
            mine = _half(halves[t].shape, c)
            cp = _remote(ins[t].at[mine], outs[t].at[mine], send_sems.at[t], recv_sems.at[t], (x, y, 1 - c))
            cp.start()
            copies.append(cp)
        for t in range(n):
            theirs = outs[t].at[_half(halves[t].shape, 1 - c)]
            _remote(theirs, theirs, send_sems.at[t], recv_sems.at[t], (x, y, 1 - c)).wait_recv()
        for cp in copies:
            cp.wait_send()

    return pl.pallas_call(
        body, name=name, in_specs=[HBM] * n, out_specs=[HBM] * n,
        out_shape=[jax.ShapeDtypeStruct(h.shape, h.dtype) for h in halves],
        input_output_aliases={t: t for t in range(n)},
        scratch_shapes=[pltpu.SemaphoreType.DMA((n,))] * 2,
        compiler_params=pltpu.CompilerParams(has_side_effects=True))(*halves)


def allreduce_small(name, buf):
    rows = buf.shape[0]

    def body(in_ref, out_ref, slots, send_sems, recv_sems):
        x, y, c, _, _ = _place()
        me = 4 * x + 2 * y + c
        slots[me] = in_ref[...]
        peers = []
        for k in range(1, 8):
            px = 1 - x if k & 4 else x
            py = 1 - y if k & 2 else y
            pc = 1 - c if k & 1 else c
            peers.append((px, py, pc))
        copies = []
        for k, peer in enumerate(peers):
            cp = _remote(in_ref, slots.at[me], send_sems.at[k], recv_sems.at[k], peer)
            cp.start()
            copies.append(cp)
        for k, (px, py, pc) in enumerate(peers):
            slot = slots.at[4 * px + 2 * py + pc]
            _remote(slot, slot, send_sems.at[k], recv_sems.at[k], (px, py, pc)).wait_recv()
        for cp in copies:
            cp.wait_send()
        acc = slots[0]
        for d in range(1, 8):
            acc = acc + slots[d]
        out_ref[...] = acc

    vm = pl.BlockSpec(memory_space=pltpu.VMEM)
    return pl.pallas_call(
        body, name=name, in_specs=[vm], out_specs=vm, out_shape=jax.ShapeDtypeStruct(buf.shape, F32),
        scratch_shapes=[pltpu.VMEM((8, rows, LANES), F32), pltpu.SemaphoreType.DMA((7,)), pltpu.SemaphoreType.DMA((7,))],
        compiler_params=pltpu.CompilerParams(has_side_effects=True, vmem_limit_bytes=VMEM_LIMIT))(buf)


def cast_layer(name, place, w, layer):
    _, r, cols = w.shape
    tr = _rows(r, cols)

    def body(place_ref, w_ref, o_ref):
        o_ref[...] = w_ref[...].astype(BF16)

    return pl.pallas_call(
        body, name=name,
        grid_spec=pltpu.PrefetchScalarGridSpec(
            num_scalar_prefetch=1, grid=(r // tr,),
            in_specs=[pl.BlockSpec((None, tr, cols), lambda i, pr: (layer, i, 0))],
            out_specs=pl.BlockSpec((None, tr, cols), lambda i, pr: (pr[1], i, 0))),
        out_shape=jax.ShapeDtypeStruct((4, r, cols), BF16),
        compiler_params=_params("parallel"))(place, w)


def _adamw_math(g, w, m, v):
    m = ADAM_B1 * m + (1.0 - ADAM_B1) * g
    v = ADAM_B2 * v + (1.0 - ADAM_B2) * (g * g)
    m_hat = m / (1.0 - ADAM_B1 ** ADAM_STEP)
    v_hat = v / (1.0 - ADAM_B2 ** ADAM_STEP)
    delta = -ADAM_LR * (m_hat / (jnp.sqrt(v_hat) + ADAM_EPS) + ADAM_WD * w)
    return delta, m, v


def adamw_layer(name, layer, g, w, m, v, prev=None):
    nl, r, cols = w.shape
    tr = _rows(r, cols, 8)

    def body(*refs):
        g_ref, w_ref, m_ref, v_ref = refs[:4]
        og, od, om, ov = refs[-4:]
        gv = g_ref[...]
        delta, m2, v2 = _adamw_math(gv, w_ref[...], m_ref[...], v_ref[...])
        og[...] = gv
        od[...] = delta
        om[...] = m2
        ov[...] = v2

    lay = pl.BlockSpec((None, tr, cols), lambda i: (layer, i, 0))
    ins = [g, w, m, v] + (list(prev) if prev is not None else [])
    return pl.pallas_call(
        body, name=name, grid=(r // tr,),
        in_specs=[pl.BlockSpec((tr, cols), lambda i: (i, 0)), lay, lay, lay] + ([HBM] * 4 if prev is not None else []),
        out_specs=[lay] * 4, out_shape=[jax.ShapeDtypeStruct((nl, r, cols), F32)] * 4,
        input_output_aliases={4 + k: k for k in range(4)} if prev is not None else {},
        compiler_params=_params("parallel"))(*ins)


def adamw_packed(name, g, w, m, v):
    def body(g_ref, w_ref, m_ref, v_ref, od, om, ov):
        delta, m2, v2 = _adamw_math(g_ref[...], w_ref[...], m_ref[...], v_ref[...])
        od[...] = delta
        om[...] = m2
        ov[...] = v2

    vm = pl.BlockSpec(memory_space=pltpu.VMEM)
    return pl.pallas_call(
        body, name=name, in_specs=[vm] * 4, out_specs=[vm] * 3, out_shape=[jax.ShapeDtypeStruct(g.shape, F32)] * 3,
        compiler_params=pltpu.CompilerParams(vmem_limit_bytes=VMEM_LIMIT))(g, w, m, v)


def _pack(arrays):
    flat = jnp.concatenate([a.reshape(-1) for a in arrays])
    pad = (-flat.shape[0]) % (8 * LANES)
    return jnp.pad(flat, (0, pad)).reshape(-1, LANES)


def _unpack(buf, shapes):
    flat = buf.reshape(-1)
    out, off = [], 0
    for shp in shapes:
        size = int(np.prod(shp))
        out.append(flat[off:off + size].reshape(shp))
        off += size
    return out


MIXER_W = ("w_in", "w_uq", "w_ukv", "w_out")
FFN_W = ("w_up", "w_down")
BIG = MIXER_W + FFN_W
TRANSPOSED = ("w_in", "w_uq")
COLUMN_CUT = ("w_ukv", "w_up")
SMALL = ("attn_norm", "a_q_norm", "a_k_norm", "c_q_norm", "c_kv_norm", "out_norm", "ffn_norm", "conv_b", "final_norm")
WEIGHTS = ("attn_norm", "w_in", "a_q_norm", "a_k_norm", "c_q_norm", "c_kv_norm", "w_uq", "w_ukv", "out_norm", "w_out",
           "ffn_norm", "w_up", "conv_w", "conv_b", "w_down", "final_norm")
INPUTS = ("x",) + WEIGHTS + ("loss_target",) + tuple("m_" + n for n in WEIGHTS) + tuple("v_" + n for n in WEIGHTS)


def _columns(g):
    return jnp.transpose(g, (1, 0, 2)).reshape(g.shape[1], 4 * g.shape[2])


def _uncolumns(w):
    r, c4 = w.shape
    return jnp.transpose(w.reshape(r, 4, c4 // 4), (1, 0, 2))


def _compute_layouts(full, conv_w, small, layer):
    p = {n: small[n][layer] for n in SMALL if n != "final_norm"}
    if "w_in" in full:
        p["w_in_t"] = full["w_in"].reshape(-1, full["w_in"].shape[2])
        uq_t = full["w_uq"].reshape(C_HEADS, HEAD + C_ROPE, C_RANK)
        p["w_uq_t"] = jnp.concatenate([uq_t[:, :HEAD].reshape(C_W, C_RANK),
                                       jnp.pad(uq_t[:, HEAD:], ((0, 0), (0, HEAD - C_ROPE), (0, 0))).reshape(C_W, C_RANK)], axis=0)
        p["w_ukv"] = _columns(full["w_ukv"])
        p["w_out"] = full["w_out"].reshape(-1, full["w_out"].shape[2])
    if "w_up" in full:
        up = full["w_up"]
        p["w_up"] = up.reshape(2, 2, up.shape[1], up.shape[2])
        p["w_down"] = full["w_down"].reshape(-1, full["w_down"].shape[2])
        fh = conv_w.shape[1] // 2
        taps = jnp.transpose(conv_w.reshape(3, 2, fh), (1, 0, 2))
        p["cwb"] = jnp.concatenate([taps, small["conv_b"][layer].reshape(2, 1, fh), jnp.zeros((2, 4, fh), F32)], axis=1)
    return p


def _shard_layout(name, g):
    if name == "w_in":
        return g.reshape(4, -1, g.shape[1])
    if name == "w_uq":
        uq_t = jnp.concatenate([g[:C_W].reshape(C_HEADS, HEAD, C_RANK),
                                g[C_W:].reshape(C_HEADS, HEAD, C_RANK)[:, :C_ROPE]], axis=1)
        return uq_t.reshape(4, -1, C_RANK)
    if name == "w_ukv":
        return _uncolumns(g)
    if name == "w_up":
        return g.reshape(4, g.shape[2], g.shape[3])
    return g.reshape(4, -1, g.shape[1])


def _conv_grads(dcwb):
    fh = dcwb.shape[2]
    return jnp.transpose(dcwb[:, 0:3, :], (1, 0, 2)).reshape(3, 2 * fh), dcwb[:, 3, :].reshape(2 * fh)


def _shard_layouts(g):
    return ({n: _shard_layout(n, g[n]) for n in BIG},) + _conv_grads(g["cwb"])


def kernel(x, attn_norm, w_in, a_q_norm, a_k_norm, c_q_norm, c_kv_norm, w_uq, w_ukv, out_norm, w_out, ffn_norm, w_up, conv_w, conv_b, w_down, final_norm, loss_target, m_attn_norm, m_w_in, m_a_q_norm, m_a_k_norm, m_c_q_norm, m_c_kv_norm, m_w_uq, m_w_ukv, m_out_norm, m_w_out, m_ffn_norm, m_w_up, m_conv_w, m_conv_b, m_w_down, m_final_norm, v_attn_norm, v_w_in, v_a_q_norm, v_a_k_norm, v_c_q_norm, v_c_kv_norm, v_w_uq, v_w_ukv, v_out_norm, v_w_out, v_ffn_norm, v_w_up, v_conv_w, v_conv_b, v_w_down, v_final_norm):
    a = dict(zip(INPUTS, (x, attn_norm, w_in, a_q_norm, a_k_norm, c_q_norm, c_kv_norm, w_uq, w_ukv, out_norm, w_out, ffn_norm, w_up, conv_w, conv_b, w_down, final_norm, loss_target, m_attn_norm, m_w_in, m_a_q_norm, m_a_k_norm, m_c_q_norm, m_c_kv_norm, m_w_uq, m_w_ukv, m_out_norm, m_w_out, m_ffn_norm, m_w_up, m_conv_w, m_conv_b, m_w_down, m_final_norm, v_attn_norm, v_w_in, v_a_q_norm, v_a_k_norm, v_c_q_norm, v_c_kv_norm, v_w_uq, v_w_ukv, v_out_norm, v_w_out, v_ffn_norm, v_w_up, v_conv_w, v_conv_b, v_w_down, v_final_norm)))
    nl = w_in.shape[0]
    seq = x.shape[1]
    for n in TRANSPOSED:
        for kind in ("", "m_", "v_"):
            a[kind + n] = jnp.swapaxes(a[kind + n], 1, 2)
    chip = 2 * lax.axis_index("x") + lax.axis_index("y")
    place = jnp.stack([lax.axis_index("c"), chip]).astype(jnp.int32)
    tabs = _rope_tables(seq)

    assert nl == 2
    quarter = conv_w.shape[2]
    taps = jnp.pad(conv_w, ((0, 0), (0, 8 - conv_w.shape[1]), (0, 0))).reshape(nl * 8, quarter)
    taps_full = gather_small("gather_taps", taps).reshape(4, nl, 8, quarter)
    conv_full = [jnp.transpose(taps_full[:, l, 0:3, :], (1, 0, 2)).reshape(3, 4 * quarter) for l in range(nl)]
    flights = {}
    after = taps_full
    for key, l, names in (("mix0", 0, MIXER_W), ("ffn0", 0, FFN_W), ("all1", 1, BIG)):
        slots = [cast_layer(f"cast{l}_{n}", place, a[n], l) for n in names]
        send_sems, recv_sems, flying, after = split_start(f"gather_{key}_start", slots, _gather_plan, after)
        flights[key] = (names, send_sems, recv_sems, flying)
    started = after[0, 0]

    def land(key, after):
        names, send_sems, recv_sems, flying = flights[key]
        landed = split_wait(f"gather_{key}_wait", send_sems, recv_sems, flying, _gather_plan, after)
        return dict(zip(names, forward_halves(f"gather_{key}_forward", landed)))

    h = x[0]
    layers, saved = [], []
    for l in range(nl):
        if l == 0:
            p = _compute_layouts(land("mix0", [after]), conv_full[l], a, l)
            p["attn_norm"] = p["attn_norm"] + started
            x1, sv = _mixers_fwd(h, p, tabs)
            p.update(_compute_layouts(land("ffn0", [x1]), conv_full[l], a, l))
        else:
            p = _compute_layouts(land("all1", [h]), conv_full[l], a, l)
            x1, sv = _mixers_fwd(h, p, tabs)
        h = _ffn_fwd(x1, p, sv)
        layers.append(p)
        saved.append(sv)
    dm = h.shape[1]
    yf = rmsnorm_fwd("final_norm", h, final_norm, dm, F32)
    loss, dyf = loss_head("loss_head", yf, loss_target[0])
    loss = lax.psum(loss, ("x", "y", "c"))
    dx, d_final = rmsnorm_bwd("final_norm_bwd", dyf, h, final_norm, dm, (F32,))

    def swap_begin(tag, names, g):
        g_list = [_shard_layout(n, g[n]) for n in names]
        theirs = [lax.empty((4,) + _half_shape(s.shape), s.dtype) for s in g_list]
        send_sems, recv_sems, flying, token = split_start(f"{tag}_sibling_start", g_list + theirs, _sibling_plan, place)
        return (names, send_sems, recv_sems, flying), token[0, 0]

    def exchange_begin(tag, swap, after):
        names, send_sems, recv_sems, flying = swap
        landed = split_wait(f"{tag}_sibling_wait", send_sems, recv_sems, flying, _sibling_plan, after)
        g_list, theirs = landed[:len(names)], landed[len(names):]
        sums = [add_half(f"{tag}_add_{n}", place, gr, t) for n, gr, t in zip(names, g_list, theirs)]
        parts = [lax.empty(s.shape, s.dtype) for s in sums]
        send_sems, recv_sems, flying, token = split_start(f"{tag}_chips_start", sums + parts, _exchange_plan, place)
        return (names, [gr.shape[1:] for gr in g_list], send_sems, recv_sems, flying), token

    def exchange_end(tag, flight, after):
        names, shard_shapes, send_sems, recv_sems, flying = flight
        landed = split_wait(f"{tag}_chips_wait", send_sems, recv_sems, flying, _exchange_plan, after)
        sums, parts = landed[:len(names)], landed[len(names):]
        halves = [sum_chips(f"{tag}_sum_{n}", place, s, p, shp) for n, s, p, shp in zip(names, sums, parts, shard_shapes)]
        return dict(zip(names, join_halves(f"{tag}_join", halves)))

    grads, conv_grads, flights_f, flights_m = [None] * nl, [None] * nl, [None] * nl, [None] * nl
    after = 0.0
    for l in reversed(range(nl)):
        sv, p = saved[l], layers[l]
        dxn, g = _ffn_bwd(dx, sv, p, after)
        swap_f, after = swap_begin(f"reduce{l}f", FFN_W, g)
        dx1, dy, g_out = _out_proj_bwd(dxn, dx, sv, p, after)
        flights_f[l], token = exchange_begin(f"reduce{l}f", swap_f, [dy])
        dhn, g_mix = _mixers_bwd(dy, sv, p, tabs, token[0, 0])
        g.update(g_out)
        g.update(g_mix)
        swap_m, after = swap_begin(f"reduce{l}m", MIXER_W, g)
        dx, g_norm = _attn_norm_bwd(dhn, dx1, sv, p, after)
        g.update(g_norm)
        flights_m[l], token = exchange_begin(f"reduce{l}m", swap_m, [dx])
        after = token[0, 0]
        grads[l], conv_grads[l] = g, _conv_grads(g["cwb"])
    grad_x = dx

    out = {}
    reduced1 = exchange_end("reduce1f", flights_f[1], [token])
    for n in FFN_W:
        out[n] = adamw_layer(f"adamw1_{n}", 1, reduced1[n], a[n], a["m_" + n], a["v_" + n])
    reduced1 = exchange_end("reduce1m", flights_m[1], [out[n][3] for n in FFN_W])
    for n in MIXER_W:
        out[n] = adamw_layer(f"adamw1_{n}", 1, reduced1[n], a[n], a["m_" + n], a["v_" + n])

    small_g = {n: jnp.stack([grads[l][n] for l in range(nl)]) for n in SMALL if n not in ("conv_b", "final_norm")}
    small_g["conv_b"] = jnp.stack([cg[1] for cg in conv_grads])
    small_g["final_norm"] = d_final
    conv_w_g = jnp.stack([cg[0] for cg in conv_grads])
    shapes = [a[n].shape for n in SMALL] + [conv_w_g.shape]
    summed = _unpack(allreduce_small("reduce_small", _pack([small_g[n] for n in SMALL] + [conv_w_g])), shapes)
    small_g = dict(zip(SMALL, summed[:-1]))
    small_g["conv_w"] = lax.dynamic_slice_in_dim(summed[-1], chip * quarter, quarter, axis=2)
    names = SMALL + ("conv_w",)
    shapes = [a[n].shape for n in names]
    d_small, m_small, v_small = adamw_packed(
        "adamw_small", _pack([small_g[n] for n in names]), _pack([a[n] for n in names]),
        _pack([a["m_" + n] for n in names]), _pack([a["v_" + n] for n in names]))

    reduced0 = exchange_end("reduce0f", flights_f[0], [out[n][3] for n in BIG] + [d_small])
    for n in FFN_W:
        out[n] = adamw_layer(f"adamw0_{n}", 0, reduced0[n], a[n], a["m_" + n], a["v_" + n], prev=out[n])
    reduced0 = exchange_end("reduce0m", flights_m[0], [out[n][3] for n in FFN_W])
    for n in MIXER_W:
        out[n] = adamw_layer(f"adamw0_{n}", 0, reduced0[n], a[n], a["m_" + n], a["v_" + n], prev=out[n])
    for n, d_, m_, v_ in zip(names, _unpack(d_small, shapes), _unpack(m_small, shapes), _unpack(v_small, shapes)):
        out[n] = (small_g[n], d_, m_, v_)
    for n in TRANSPOSED:
        out[n] = [jnp.swapaxes(o, 1, 2) for o in out[n]]

    return (loss, grad_x[None]) + tuple(out[n][k] for k in range(4) for n in WEIGHTS)
```

```python
import functools

import jax
import jax.numpy as jnp
import numpy as np
from jax import lax
from jax.experimental import pallas as pl
from jax.experimental.pallas import tpu as pltpu

F32 = jnp.float32
BF16 = jnp.bfloat16
MESH = pl.DeviceIdType.MESH

V7X_VMEM_BYTES = 64 * 1024 * 1024
VMEM_LIMIT = V7X_VMEM_BYTES - 6 * 1024 * 1024
LANES = 128
ELEMENTWISE_BLOCK_BYTES = 2 * 1024 * 1024

HEAD = 128
A_HEADS, A_KV = 6, 2
B_HEADS, B_GROUPS = 4, 3
B_DILATIONS = (1, 4, 16)
B_HALF = 64
C_HEADS, C_RANK, C_ROPE = 6, 512, 64
GRID_W = 64
A_THETA, B_THETA, C_THETA = 10000.0, 500000.0, 10000.0
B_ROPE_DIM = 32
EPS = 1e-6
A_W, B_W, C_W = A_HEADS * HEAD, B_HEADS * HEAD, C_HEADS * HEAD
O_AQ, O_AK, O_AV = 0, 768, 1024
O_BQ, O_BK, O_BV = 1280, 2816, 3328
O_CQ, O_CKV, O_CKR = 3840, 4352, 4864
IN_W = 4928
PROJ_W = 5120

ADAM_LR, ADAM_B1, ADAM_B2, ADAM_EPS, ADAM_WD, ADAM_STEP = 0.001, 0.9, 0.999, 1e-08, 0.01, 10

NN = (((1,), (0,)), ((), ()))
NT = (((1,), (1,)), ((), ()))
TN = (((0,), (0,)), ((), ()))


def _dot(a, b, dims=NN):
    return lax.dot_general(a, b, dims, preferred_element_type=F32)


def _params(*sem):
    return pltpu.CompilerParams(dimension_semantics=sem if sem else None, vmem_limit_bytes=VMEM_LIMIT)


def _tile(n, target, unit=LANES):
    if n <= target:
        return n
    best = 0
    for t in range(unit, target + 1, unit):
        if n % t == 0:
            best = t
    return best if best else n


def _rows(r, width, itemsize=4):
    return _tile(r, max(8, ELEMENTWISE_BLOCK_BYTES // (width * itemsize)), 8)


def _matmul_call(name, dims, grid, a_spec, b_spec, o_spec, out_shape, acc_shape, add_spec=None):
    nk = grid[2]

    def body(*refs):
        a_ref, b_ref = refs[:2]
        add_ref = None if add_spec is None else refs[2]
        o_ref = refs[2 if add_spec is None else 3]

        def finish(r):
            if add_ref is not None:
                r = r + add_ref[...]
            o_ref[...] = r.astype(o_ref.dtype)

        if nk == 1:
            finish(_dot(a_ref[...], b_ref[...], dims))
            return
        acc = refs[-1]
        k = pl.program_id(2)

        @pl.when(k == 0)
        def _():
            acc[...] = _dot(a_ref[...], b_ref[...], dims)

        if nk > 2:
            @pl.when((k > 0) & (k < nk - 1))
            def _():
                acc[...] += _dot(a_ref[...], b_ref[...], dims)

        @pl.when(k == nk - 1)
        def _():
            finish(acc[...] + _dot(a_ref[...], b_ref[...], dims))

    in_specs = [a_spec, b_spec] + ([] if add_spec is None else [add_spec])
    return pl.pallas_call(
        body, name=name, grid=grid, in_specs=in_specs, out_specs=o_spec, out_shape=out_shape,
        scratch_shapes=[pltpu.VMEM(acc_shape, F32)] if nk > 1 else [],
        compiler_params=_params("parallel", "parallel", "arbitrary"))


def matmul(name, a, b, mode, out_dtype, add=None, tm=1024, tn=1024, tk=2816):
    if mode == "nn":
        (m, k), (k2, n) = a.shape, b.shape
    elif mode == "nt":
        (m, k), (n, k2) = a.shape, b.shape
    else:
        (k, m), (k2, n) = a.shape, b.shape
    assert k == k2, (name, a.shape, b.shape)
    tm, tn, tk = _tile(m, tm), _tile(n, tn), _tile(k, tk)
    grid = (m // tm, n // tn, k // tk)
    if mode == "tn":
        a_spec = pl.BlockSpec((tk, tm), lambda i, j, kk: (kk, i))
    else:
        a_spec = pl.BlockSpec((tm, tk), lambda i, j, kk: (i, kk))
    if mode == "nt":
        b_spec = pl.BlockSpec((tn, tk), lambda i, j, kk: (j, kk))
    else:
        b_spec = pl.BlockSpec((tk, tn), lambda i, j, kk: (kk, j))
    o_spec = pl.BlockSpec((tm, tn), lambda i, j, kk: (i, j))
    dims = {"nn": NN, "nt": NT, "tn": TN}[mode]
    call = _matmul_call(name, dims, grid, a_spec, b_spec, o_spec, jax.ShapeDtypeStruct((m, n), out_dtype),
                        (tm, tn), None if add is None else o_spec)
    return call(a, b) if add is None else call(a, b, add)


def rmsnorm_fwd(name, x, g, width, out_dtype):
    r, cols = x.shape
    nb = cols // width
    tr = _rows(r, width)

    def body(x_ref, g_ref, o_ref):
        xv = x_ref[...]
        rs = lax.rsqrt(jnp.mean(xv * xv, axis=-1, keepdims=True) + EPS)
        o_ref[...] = (xv * rs * g_ref[...]).astype(o_ref.dtype)

    blk = pl.BlockSpec((tr, width), lambda i, j: (i, j))
    return pl.pallas_call(
        body, name=name, grid=(r // tr, nb),
        in_specs=[blk, pl.BlockSpec((1, width), lambda i, j: (0, 0))], out_specs=blk,
        out_shape=jax.ShapeDtypeStruct((r, cols), out_dtype),
        compiler_params=_params("parallel", "parallel"))(x, g.reshape(1, width))


def rmsnorm_bwd(name, dy, x, g, width, out_dtypes=(F32,), add=None):
    r, cols = x.shape
    nb = cols // width
    tr = _rows(r, width)
    n_out = len(out_dtypes)

    def body(*refs):
        dy_ref, x_ref, g_ref = refs[:3]
        add_ref = refs[3] if add is not None else None
        outs = refs[-(n_out + 1):-1]
        dg_ref = refs[-1]
        xv = x_ref[...]
        dyv = dy_ref[...].astype(F32)
        rs = lax.rsqrt(jnp.mean(xv * xv, axis=-1, keepdims=True) + EPS)
        xh = xv * rs
        dyg = dyv * g_ref[...]
        dx = rs * (dyg - xh * jnp.mean(dyg * xh, axis=-1, keepdims=True))
        if add_ref is not None:
            dx = dx + add_ref[...]
        for o in outs:
            o[...] = dx.astype(o.dtype)

        @pl.when((pl.program_id(0) == 0) & (pl.program_id(1) == 0))
        def _():
            dg_ref[...] = jnp.zeros_like(dg_ref)

        dg_ref[...] += jnp.sum(dyv * xh, axis=0, keepdims=True)

    blk = pl.BlockSpec((tr, width), lambda i, j: (i, j))
    vec = pl.BlockSpec((1, width), lambda i, j: (0, 0))
    ins = [dy, x, g.reshape(1, width)] + ([add] if add is not None else [])
    res = pl.pallas_call(
        body, name=name, grid=(r // tr, nb),
        in_specs=[blk, blk, vec] + ([blk] if add is not None else []),
        out_specs=[blk] * n_out + [vec],
        out_shape=[jax.ShapeDtypeStruct((r, cols), dt) for dt in out_dtypes] + [jax.ShapeDtypeStruct((1, width), F32)],
        compiler_params=_params("arbitrary", "arbitrary"))(*ins)
    return tuple(res[:n_out]) + (res[n_out].reshape(width),)


def _rope_angles(pos, dim, theta):
    inv = theta ** (-jnp.arange(0, dim, 2, dtype=F32) / dim)
    ang = pos.astype(F32)[:, None] * inv[None, :]
    return jnp.cos(ang), jnp.sin(ang)


def _rope_tables(s):
    rows = s // GRID_W
    row_pos = jnp.repeat(jnp.arange(rows), GRID_W)
    col_pos = jnp.tile(jnp.arange(GRID_W), rows)
    t_pos = jnp.arange(s)
    z = lambda n: jnp.zeros((s, n), F32)
    o = lambda n: jnp.ones((s, n), F32)
    cr, sr = _rope_angles(row_pos, HEAD // 2, A_THETA)
    cc, sc = _rope_angles(col_pos, HEAD // 2, A_THETA)
    tab_a = (jnp.concatenate([cr, cr, cc, cc], 1), jnp.concatenate([-sr, z(32), -sc, z(32)], 1),
             jnp.concatenate([z(32), sr, z(32), sc], 1), 32)
    cp, sp = _rope_angles(t_pos, B_ROPE_DIM, B_THETA)
    tab_b = (jnp.concatenate([cp, cp, o(96)], 1), jnp.concatenate([-sp, z(112)], 1),
             jnp.concatenate([z(16), sp, z(96)], 1), 16)
    cm, sm = _rope_angles(t_pos, C_ROPE, C_THETA)
    tab_c = (jnp.concatenate([cm, cm, o(64)], 1), jnp.concatenate([-sm, z(96)], 1),
             jnp.concatenate([z(32), sm, z(64)], 1), 32)

    def transposed(tab):
        c, s1, s2, h = tab
        return (c, jnp.roll(s2, -h, axis=1), jnp.roll(s1, h, axis=1), h)

    return {k: (t, transposed(t)) for k, t in (("a", tab_a), ("b", tab_b), ("c", tab_c))}


def rope(name, x, tab, out_dtype):
    c, s1, s2, h = tab
    s, cols = x.shape
    nh = cols // HEAD
    ts = _rows(s, HEAD)

    def body(x_ref, c_ref, s1_ref, s2_ref, o_ref):
        xv = x_ref[...].astype(F32)
        out = xv * c_ref[...] + pltpu.roll(xv, HEAD - h, 1) * s1_ref[...] + pltpu.roll(xv, h, 1) * s2_ref[...]
        o_ref[...] = out.astype(o_ref.dtype)

    blk = pl.BlockSpec((ts, HEAD), lambda i, j: (i, j))
    tb = pl.BlockSpec((ts, HEAD), lambda i, j: (i, 0))
    return pl.pallas_call(
        body, name=name, grid=(s // ts, nh), in_specs=[blk, tb, tb, tb], out_specs=blk,
        out_shape=jax.ShapeDtypeStruct((s, cols), out_dtype),
        compiler_params=_params("parallel", "parallel"))(x, c, s1, s2)


ATTN_TQ = 256


def attn_fwd(name, parts, v, v_group, nheads, scale):
    s = v.shape[0]
    tq = ATTN_TQ
    npart = len(parts)

    def body(*refs):
        v_ref, o_ref, lse_ref = refs[2 * npart:]
        sc = None
        for p in range(npart):
            t = _dot(refs[2 * p][...], refs[2 * p + 1][...], NT)
            sc = t if sc is None else sc + t
        sc = sc * scale
        m = jnp.max(sc, axis=-1, keepdims=True)
        e = jnp.exp(sc - m)
        l = jnp.sum(e, axis=-1, keepdims=True)
        o_ref[...] = _dot((e / l).astype(BF16), v_ref[...])
        lse_ref[...] = jnp.broadcast_to(m + jnp.log(l), (tq, HEAD))

    in_specs, ins = [], []
    for q, qoff, k, kg in parts:
        in_specs.append(pl.BlockSpec((tq, HEAD), lambda h, i, qoff=qoff: (i, qoff + h)))
        in_specs.append(pl.BlockSpec((s, HEAD), lambda h, i, kg=kg: (0, h // kg)))
        ins += [q, k]
    in_specs.append(pl.BlockSpec((s, HEAD), lambda h, i: (0, h // v_group)))
    out_blk = pl.BlockSpec((tq, HEAD), lambda h, i: (i, h))
    return pl.pallas_call(
        body, name=name, grid=(nheads, s // tq), in_specs=in_specs, out_specs=[out_blk, out_blk],
        out_shape=[jax.ShapeDtypeStruct((s, nheads * HEAD), F32)] * 2,
        compiler_params=_params("parallel", "parallel"))(*ins, v)


def attn_bwd(name, parts, v, v_group, lse, do, nheads, scale):
    s = v.shape[0]
    tq = ATTN_TQ
    npart = len(parts)

    def body(*refs):
        v_ref, lse_ref, do_ref = refs[2 * npart:2 * npart + 3]
        outs = refs[2 * npart + 3:]
        dq_refs, dk_refs, dv_ref = outs[:npart], outs[npart:2 * npart], outs[2 * npart]
        h, i = pl.program_id(0), pl.program_id(1)
        sc = None
        for p in range(npart):
            t = _dot(refs[2 * p][...], refs[2 * p + 1][...], NT)
            sc = t if sc is None else sc + t
        pr = jnp.exp(sc * scale - lse_ref[:, 0:1])
        dov = do_ref[...]
        dp = _dot(dov, v_ref[...], NT)
        delta = jnp.sum(pr * dp, axis=-1, keepdims=True)
        ds = (pr * (dp - delta) * scale).astype(BF16)

        @pl.when((i == 0) & (h % v_group == 0))
        def _():
            dv_ref[...] = jnp.zeros_like(dv_ref)

        dv_ref[...] += _dot(pr.astype(BF16), dov, TN)
        for p in range(npart):
            kg = parts[p][3]
            dq_refs[p][...] = _dot(ds, refs[2 * p + 1][...])

            @pl.when((i == 0) & (h % kg == 0))
            def _(p=p):
                dk_refs[p][...] = jnp.zeros_like(dk_refs[p])

            dk_refs[p][...] += _dot(ds, refs[2 * p][...], TN)

    in_specs, ins = [], []
    for q, qoff, k, kg in parts:
        in_specs.append(pl.BlockSpec((tq, HEAD), lambda h, i, qoff=qoff: (i, qoff + h)))
        in_specs.append(pl.BlockSpec((s, HEAD), lambda h, i, kg=kg: (0, h // kg)))
        ins += [q, k]
    hq_blk = pl.BlockSpec((tq, HEAD), lambda h, i: (i, h))
    in_specs += [pl.BlockSpec((s, HEAD), lambda h, i: (0, h // v_group)), hq_blk, hq_blk]
    out_specs = [hq_blk] * npart
    out_shape = [jax.ShapeDtypeStruct((s, nheads * HEAD), F32)] * npart
    for q, qoff, k, kg in parts:
        out_specs.append(pl.BlockSpec((s, HEAD), lambda h, i, kg=kg: (0, h // kg)))
        out_shape.append(jax.ShapeDtypeStruct((s, nheads // kg * HEAD), F32))
    out_specs.append(pl.BlockSpec((s, HEAD), lambda h, i: (0, h // v_group)))
    out_shape.append(jax.ShapeDtypeStruct((s, nheads // v_group * HEAD), F32))
    res = pl.pallas_call(
        body, name=name, grid=(nheads, s // tq), in_specs=in_specs, out_specs=out_specs, out_shape=out_shape,
        compiler_params=_params("arbitrary", "arbitrary"))(*ins, v, lse, do)
    return list(res[:npart]), list(res[npart:2 * npart]), res[2 * npart]


def _band_windows(lf):
    for ib in range(lf // HEAD):
        q0 = ib * HEAD
        yield q0, max(0, q0 - B_HALF), min(lf, q0 + HEAD + B_HALF)


def _band_mask(q0, lo, hi):
    qpos = q0 + lax.broadcasted_iota(jnp.int32, (HEAD, hi - lo), 0)
    kpos = lo + lax.broadcasted_iota(jnp.int32, (HEAD, hi - lo), 1)
    return jnp.abs(qpos - kpos) <= B_HALF


def _class_rows(r, d, start, size):
    return pl.ds(r + d * start, size, stride=d) if d > 1 else pl.ds(start, size)


def dilated_fwd(name, q, k, v_src, v_block):
    s = k.shape[0]
    scale = HEAD ** -0.5

    def body(q_ref, k_ref, v_ref, y_ref, lse_ref, o_scr, l_scr):
        g_now = pl.program_id(1)
        for g, d in enumerate(B_DILATIONS):
            @pl.when(g_now == g)
            def _(g=g, d=d):
                for r in range(d):
                    for q0, lo, hi in _band_windows(s // d):
                        mine = _class_rows(r, d, q0, HEAD)
                        keys = _class_rows(r, d, lo, hi - lo)
                        sc = _dot(q_ref[mine, :].astype(BF16), k_ref[keys, :].astype(BF16), NT) * scale
                        sc = jnp.where(_band_mask(q0, lo, hi), sc, -1e30)
                        m = jnp.max(sc, axis=-1, keepdims=True)
                        e = jnp.exp(sc - m)
                        l = jnp.sum(e, axis=-1, keepdims=True)
                        o_scr.at[g][mine, :] = _dot((e / l).astype(BF16), v_ref[keys, :].astype(BF16))
                        l_scr.at[g][mine, :] = jnp.broadcast_to(m + jnp.log(l), (HEAD, HEAD))

        @pl.when(g_now == B_GROUPS - 1)
        def _():
            a, b, c = l_scr[0], l_scr[1], l_scr[2]
            m = jnp.maximum(jnp.maximum(a, b), c)
            ea, eb, ec = jnp.exp(a - m), jnp.exp(b - m), jnp.exp(c - m)
            den = ea + eb + ec
            y_ref[...] = (ea / den) * o_scr[0] + (eb / den) * o_scr[1] + (ec / den) * o_scr[2]
            lse_ref[...] = m + jnp.log(den)

    blk = lambda f: pl.BlockSpec((s, HEAD), f)
    per_head = blk(lambda h, g: (0, h))
    return pl.pallas_call(
        body, name=name, grid=(B_HEADS, B_GROUPS),
        in_specs=[blk(lambda h, g: (0, B_HEADS * g + h)), per_head, blk(lambda h, g: (0, v_block + h))],
        out_specs=[per_head, per_head], out_shape=[jax.ShapeDtypeStruct((s, B_W), F32)] * 2,
        scratch_shapes=[pltpu.VMEM((B_GROUPS, s, HEAD), F32)] * 2,
        compiler_params=_params("parallel", "arbitrary"))(q, k, v_src)


def dilated_bwd(name, q, k, v_src, v_block, dy, y, lse):
    s = k.shape[0]
    scale = HEAD ** -0.5

    def body(q_ref, k_ref, v_ref, dy_ref, y_ref, lse_ref, dq_ref, dk_ref, dv_ref, delta):
        g_now = pl.program_id(1)

        @pl.when(g_now == 0)
        def _():
            dk_ref[...] = jnp.zeros_like(dk_ref)
            dv_ref[...] = jnp.zeros_like(dv_ref)
            delta[...] = jnp.broadcast_to(jnp.sum(dy_ref[...] * y_ref[...], axis=-1, keepdims=True), (s, HEAD))

        for g, d in enumerate(B_DILATIONS):
            @pl.when(g_now == g)
            def _(d=d):
                for r in range(d):
                    for q0, lo, hi in _band_windows(s // d):
                        mine = _class_rows(r, d, q0, HEAD)
                        keys = _class_rows(r, d, lo, hi - lo)
                        qv, kv = q_ref[mine, :].astype(BF16), k_ref[keys, :].astype(BF16)
                        dyv = dy_ref[mine, :].astype(BF16)
                        sc = _dot(qv, kv, NT) * scale
                        pr = jnp.where(_band_mask(q0, lo, hi), jnp.exp(sc - lse_ref[mine, :][:, 0:1]), 0.0)
                        dp = _dot(dyv, v_ref[keys, :].astype(BF16), NT)
                        ds = (pr * (dp - delta[mine, :][:, 0:1]) * scale).astype(BF16)
                        dv_ref[keys, :] = dv_ref[keys, :] + _dot(pr.astype(BF16), dyv, TN)
                        dq_ref[mine, :] = _dot(ds, kv)
                        dk_ref[keys, :] = dk_ref[keys, :] + _dot(ds, qv, TN)

    blk = lambda f: pl.BlockSpec((s, HEAD), f)
    per_head = blk(lambda h, g: (0, h))
    by_group = blk(lambda h, g: (0, B_HEADS * g + h))
    return pl.pallas_call(
        body, name=name, grid=(B_HEADS, B_GROUPS),
        in_specs=[by_group, per_head, blk(lambda h, g: (0, v_block + h)), per_head, per_head, per_head],
        out_specs=[by_group, per_head, per_head],
        out_shape=[jax.ShapeDtypeStruct((s, B_GROUPS * B_W), F32)] + [jax.ShapeDtypeStruct((s, B_W), F32)] * 2,
        scratch_shapes=[pltpu.VMEM((s, HEAD), F32)],
        compiler_params=_params("parallel", "arbitrary"))(q, k, v_src, dy, y, lse)


FFN_TN = 256


def _conv(h, cw):
    s = h.shape[0]
    row = lax.broadcasted_iota(jnp.int32, h.shape, 0)
    prev = jnp.where(row == 0, 0.0, pltpu.roll(h, 1, 0))
    nxt = jnp.where(row == s - 1, 0.0, pltpu.roll(h, s - 1, 0))
    return prev * cw[0:1, :] + h * cw[1:2, :] + nxt * cw[2:3, :] + cw[3:4, :], prev, nxt


def _sigmoid(x):
    return 1.0 / (1.0 + jnp.exp(-x))


def ffn_up(name, xn, w_up, cwb):
    s, dm = xn.shape
    quarter = w_up.shape[3]
    fh = 2 * quarter
    tn = FFN_TN
    per = quarter // tn

    def body(x_ref, w_ref, cw_ref, h_ref, act_ref):
        xv = x_ref[...]
        hg = _dot(xv, w_ref[0])
        hu = _dot(xv, w_ref[1])
        h_ref[0] = hg
        h_ref[1] = hu
        gc, _, _ = _conv(hg, cw_ref[0])
        uc, _, _ = _conv(hu, cw_ref[1])
        act_ref[...] = (gc * _sigmoid(gc) * uc).astype(BF16)

    return pl.pallas_call(
        body, name=name, grid=(fh // tn,),
        in_specs=[pl.BlockSpec((s, dm), lambda t: (0, 0)),
                  pl.BlockSpec((2, None, dm, tn), lambda t: (0, t // per, 0, t % per)),
                  pl.BlockSpec((2, 8, tn), lambda t: (0, 0, t))],
        out_specs=[pl.BlockSpec((2, s, tn), lambda t: (0, 0, t)), pl.BlockSpec((s, tn), lambda t: (0, t))],
        out_shape=[jax.ShapeDtypeStruct((2, s, fh), F32), jax.ShapeDtypeStruct((s, fh), BF16)],
        compiler_params=_params("parallel"))(xn, w_up, cwb)


def ffn_gate_bwd(name, h, dact, cwb):
    _, s, fh = h.shape
    tn = FFN_TN

    def body(h_ref, da_ref, cw_ref, dh_ref, dcw_ref):
        gc, gp, gn = _conv(h_ref[0], cw_ref[0])
        uc, up, un = _conv(h_ref[1], cw_ref[1])
        sg = _sigmoid(gc)
        da = da_ref[...]
        dgc = da * uc * (sg * (1.0 + gc * (1.0 - sg)))
        duc = da * (gc * sg)
        row = lax.broadcasted_iota(jnp.int32, da.shape, 0)
        for idx, (hv, prev, nxt, dc) in enumerate(((h_ref[0], gp, gn, dgc), (h_ref[1], up, un, duc))):
            cw = cw_ref[idx]
            from_next = jnp.where(row == s - 1, 0.0, pltpu.roll(dc, s - 1, 0))
            from_prev = jnp.where(row == 0, 0.0, pltpu.roll(dc, 1, 0))
            dh_ref[idx] = (from_next * cw[0:1, :] + dc * cw[1:2, :] + from_prev * cw[2:3, :]).astype(BF16)
            dcw_ref[idx, 0:1, :] = jnp.sum(prev * dc, axis=0, keepdims=True)
            dcw_ref[idx, 1:2, :] = jnp.sum(hv * dc, axis=0, keepdims=True)
            dcw_ref[idx, 2:3, :] = jnp.sum(nxt * dc, axis=0, keepdims=True)
            dcw_ref[idx, 3:4, :] = jnp.sum(dc, axis=0, keepdims=True)
            dcw_ref[idx, 4:8, :] = jnp.zeros((4, tn), F32)

    return pl.pallas_call(
        body, name=name, grid=(fh // tn,),
        in_specs=[pl.BlockSpec((2, s, tn), lambda t: (0, 0, t)), pl.BlockSpec((s, tn), lambda t: (0, t)),
                  pl.BlockSpec((2, 8, tn), lambda t: (0, 0, t))],
        out_specs=[pl.BlockSpec((2, s, tn), lambda t: (0, 0, t)), pl.BlockSpec((2, 8, tn), lambda t: (0, 0, t))],
        out_shape=[jax.ShapeDtypeStruct((2, s, fh), BF16), jax.ShapeDtypeStruct((2, 8, fh), F32)],
        compiler_params=_params("parallel"))(h, dact, cwb)


def ffn_dx(name, dh, w_up):
    _, s, fh = dh.shape
    dm, quarter = w_up.shape[2], w_up.shape[3]
    tk = _tile(quarter, 2816)
    per = quarter // tk
    tm, tn = _tile(s, 1024), _tile(dm, 1024)
    grid = (s // tm, dm // tn, 4 * per)
    a_spec = pl.BlockSpec((None, tm, tk), lambda i, j, k: (k // (2 * per), i, k % (2 * per)))
    b_spec = pl.BlockSpec((None, None, tn, tk), lambda i, j, k: (k // (2 * per), (k // per) % 2, j, k % per))
    o_spec = pl.BlockSpec((tm, tn), lambda i, j, k: (i, j))
    return _matmul_call(name, NT, grid, a_spec, b_spec, o_spec, jax.ShapeDtypeStruct((s, dm), F32), (tm, tn))(dh, w_up)


def ffn_dw_up(name, xn, dh):
    _, s, fh = dh.shape
    dm = xn.shape[1]
    quarter = fh // 2
    tn = _tile(quarter, 1536)
    per = quarter // tn
    tm, tk = _tile(dm, 1024), _tile(s, 2048)
    grid = (dm // tm, 4 * per, s // tk)
    a_spec = pl.BlockSpec((tk, tm), lambda i, j, k: (k, i))
    b_spec = pl.BlockSpec((None, tk, tn), lambda i, j, k: (j // (2 * per), k, j % (2 * per)))
    o_spec = pl.BlockSpec((None, None, tm, tn), lambda i, j, k: (j // (2 * per), (j // per) % 2, i, j % per))
    return _matmul_call(name, TN, grid, a_spec, b_spec, o_spec,
                        jax.ShapeDtypeStruct((2, 2, dm, quarter), F32), (tm, tn))(xn, dh)


def loss_head(name, y, target):
    s, dm = y.shape
    ts = _rows(s, dm)

    def body(y_ref, t_ref, loss_ref, dy_ref, acc):
        i = pl.program_id(0)

        @pl.when(i == 0)
        def _():
            acc[...] = jnp.zeros_like(acc)

        err = y_ref[...] - t_ref[...]
        dy_ref[...] = err * (1.0 / dm)
        acc[...] += jnp.sum(err * err, axis=0, keepdims=True)

        @pl.when(i == s // ts - 1)
        def _():
            loss_ref[...] = jnp.broadcast_to(jnp.sum(acc[...], axis=-1, keepdims=True) * (0.5 / dm), (1, LANES))

    blk = pl.BlockSpec((ts, dm), lambda i: (i, 0))
    loss, dy = pl.pallas_call(
        body, name=name, grid=(s // ts,), in_specs=[blk, blk],
        out_specs=[pl.BlockSpec((1, LANES), lambda i: (0, 0)), blk],
        out_shape=[jax.ShapeDtypeStruct((1, LANES), F32), jax.ShapeDtypeStruct((s, dm), F32)],
        scratch_shapes=[pltpu.VMEM((1, dm), F32)], compiler_params=_params("arbitrary"))(y, target)
    return loss[0, 0], dy


def attn_fwd_c(name, qc, kvc, krc, scale):
    s = qc.shape[0]
    tq = ATTN_TQ

    def body(qn_ref, qr_ref, kn_ref, kr_ref, v_ref, o_ref, lse_ref):
        sc = (_dot(qn_ref[...], kn_ref[...], NT) + _dot(qr_ref[...], kr_ref[...], NT)) * scale
        m = jnp.max(sc, axis=-1, keepdims=True)
        e = jnp.exp(sc - m)
        l = jnp.sum(e, axis=-1, keepdims=True)
        o_ref[...] = _dot((e / l).astype(BF16), v_ref[...])
        lse_ref[...] = jnp.broadcast_to(m + jnp.log(l), (tq, HEAD))

    qb = lambda off: pl.BlockSpec((tq, HEAD), lambda h, i: (i, off + h))
    kb = lambda f: pl.BlockSpec((s, HEAD), f)
    out_blk = pl.BlockSpec((tq, HEAD), lambda h, i: (i, h))
    return pl.pallas_call(
        body, name=name, grid=(C_HEADS, s // tq),
        in_specs=[qb(0), qb(C_HEADS), kb(lambda h, i: (0, 2 * h)), kb(lambda h, i: (0, 0)), kb(lambda h, i: (0, 2 * h + 1))],
        out_specs=[out_blk, out_blk], out_shape=[jax.ShapeDtypeStruct((s, C_W), F32)] * 2,
        compiler_params=_params("parallel", "parallel"))(qc, qc, kvc, krc, kvc)


def attn_bwd_c(name, qc, kvc, krc, lse, do, scale):
    s = qc.shape[0]
    tq = ATTN_TQ

    def body(qn_ref, qr_ref, kn_ref, kr_ref, v_ref, lse_ref, do_ref, dqn_ref, dqr_ref, dkn_ref, dv_ref, dkr_ref):
        h, i = pl.program_id(0), pl.program_id(1)
        sc = (_dot(qn_ref[...], kn_ref[...], NT) + _dot(qr_ref[...], kr_ref[...], NT)) * scale
        pr = jnp.exp(sc - lse_ref[:, 0:1])
        dov = do_ref[...]
        dp = _dot(dov, v_ref[...], NT)
        delta = jnp.sum(pr * dp, axis=-1, keepdims=True)
        ds = (pr * (dp - delta) * scale).astype(BF16)

        @pl.when(i == 0)
        def _():
            dv_ref[...] = jnp.zeros_like(dv_ref)
            dkn_ref[...] = jnp.zeros_like(dkn_ref)

        @pl.when((i == 0) & (h == 0))
        def _():
            dkr_ref[...] = jnp.zeros_like(dkr_ref)

        dv_ref[...] += _dot(pr.astype(BF16), dov, TN)
        dqn_ref[...] = _dot(ds, kn_ref[...])
        dqr_ref[...] = _dot(ds, kr_ref[...])
        dkn_ref[...] += _dot(ds, qn_ref[...], TN)
        dkr_ref[...] += _dot(ds, qr_ref[...], TN)

    qb = lambda off: pl.BlockSpec((tq, HEAD), lambda h, i: (i, off + h))
    kb = lambda f: pl.BlockSpec((s, HEAD), f)
    hq = pl.BlockSpec((tq, HEAD), lambda h, i: (i, h))
    kn_map, v_map, kr_map = (lambda h, i: (0, 2 * h)), (lambda h, i: (0, 2 * h + 1)), (lambda h, i: (0, 0))
    dqc_shape = jax.ShapeDtypeStruct((s, 2 * C_W), F32)
    dqn, dqr, dkn, dv, dkr = pl.pallas_call(
        body, name=name, grid=(C_HEADS, s // tq),
        in_specs=[qb(0), qb(C_HEADS), kb(kn_map), kb(kr_map), kb(v_map), hq, hq],
        out_specs=[hq, hq, kb(lambda h, i: (0, h)), kb(lambda h, i: (0, h)), kb(kr_map)],
        out_shape=[jax.ShapeDtypeStruct((s, C_W), F32)] * 4 + [jax.ShapeDtypeStruct((s, HEAD), F32)],
        compiler_params=_params("arbitrary", "arbitrary"))(qc, qc, kvc, krc, kvc, lse, do)
    del dqc_shape
    return dqn, dqr, dkn, dv, dkr


def _layer_fwd(x, p, tabs):
    x1, sv = _mixers_fwd(x, p, tabs)
    return _ffn_fwd(x1, p, sv), sv


def _mixers_fwd(x, p, tabs):
    sv = {"x": x}
    hn = rmsnorm_fwd("attn_norm", x, p["attn_norm"], x.shape[1], BF16)
    proj = matmul("in_proj", hn, p["w_in_t"], "nt", F32, tm=512, tk=512)
    sv["hn"] = hn
    aq, ak = proj[:, O_AQ:O_AK], proj[:, O_AK:O_AV]
    av = proj[:, O_AV:O_BQ].astype(BF16)
    bq, bk = proj[:, O_BQ:O_BK], proj[:, O_BK:O_BV]
    cq, ckv = proj[:, O_CQ:O_CKV], proj[:, O_CKV:O_CKR]
    ckr = jnp.pad(proj[:, O_CKR:], ((0, 0), (0, HEAD - C_ROPE)))
    sv.update(aq=aq, ak=ak, av=av, cq=cq, ckv=ckv)

    qa = rope("a_q_rope", rmsnorm_fwd("a_q_norm", aq, p["a_q_norm"], HEAD, F32), tabs["a"][0], BF16)
    ka = rope("a_k_rope", rmsnorm_fwd("a_k_norm", ak, p["a_k_norm"], HEAD, F32), tabs["a"][0], BF16)
    ya, lse_a = attn_fwd("a_attn", [(qa, 0, ka, A_HEADS // A_KV)], av, A_HEADS // A_KV, A_HEADS, HEAD ** -0.5)
    sv.update(qa=qa, ka=ka, ya=ya, lse_a=lse_a)

    qb = rope("b_q_rope", bq, tabs["b"][0], F32)
    kb = rope("b_k_rope", bk, tabs["b"][0], F32)
    yb, lse_b = dilated_fwd("b_dilated", qb, kb, proj, O_BV // HEAD)
    sv.update(qb=qb, kb=kb, yb=yb, lse_b=lse_b, proj=proj)

    cqn = rmsnorm_fwd("c_q_norm", cq, p["c_q_norm"], C_RANK, BF16)
    qc_raw = matmul("c_uq", cqn, p["w_uq_t"], "nt", F32)
    qc = jnp.concatenate([qc_raw[:, :C_W].astype(BF16), rope("c_q_rope", qc_raw[:, C_W:], tabs["c"][0], BF16)], axis=1)
    ckvn = rmsnorm_fwd("c_kv_norm", ckv, p["c_kv_norm"], C_RANK, BF16)
    kvc = matmul("c_ukv", ckvn, p["w_ukv"], "nn", BF16)
    krc = rope("c_k_rope", ckr, tabs["c"][0], BF16)
    c_scale = (HEAD + C_ROPE) ** -0.5
    yc, lse_c = attn_fwd_c("c_attn", qc, kvc, krc, c_scale)
    sv.update(cqn=cqn, ckvn=ckvn, qc=qc, kvc=kvc, krc=krc, yc=yc, lse_c=lse_c)

    g_a, g_b, g_c = p["out_norm"][:A_W], p["out_norm"][A_W:A_W + B_W], p["out_norm"][A_W + B_W:]
    y = jnp.concatenate([rmsnorm_fwd("out_norm_a", ya, g_a, A_W, BF16), rmsnorm_fwd("out_norm_b", yb, g_b, B_W, BF16),
                         rmsnorm_fwd("out_norm_c", yc, g_c, C_W, BF16)], axis=1)
    x1 = matmul("out_proj", y, p["w_out"], "nn", F32, add=x)
    sv.update(y=y, x1=x1)
    return x1, sv


def _ffn_fwd(x1, p, sv):
    xn = rmsnorm_fwd("ffn_norm", x1, p["ffn_norm"], x1.shape[1], BF16)
    h, act = ffn_up("ffn_up", xn, p["w_up"], p["cwb"])
    x2 = matmul("ffn_down", act, p["w_down"], "nn", F32, add=x1)
    sv.update(xn=xn, h=h, act=act)
    return x2


def _layer_bwd(dx2, sv, p, tabs):
    dxn, g = _ffn_bwd(dx2, sv, p)
    dx1, dy, g_out = _out_proj_bwd(dxn, dx2, sv, p)
    dhn, g_mix = _mixers_bwd(dy, sv, p, tabs)
    dx, g_norm = _attn_norm_bwd(dhn, dx1, sv, p)
    return dx, {**g, **g_out, **g_mix, **g_norm}


def _ffn_bwd(dx2, sv, p, after=0.0):
    g = {}
    dx2b = (dx2 + after).astype(BF16)
    dact = matmul("ffn_down_dx", dx2b, p["w_down"], "nt", F32)
    g["w_down"] = matmul("ffn_down_dw", sv["act"], dx2b, "tn", F32)
    dh, g["cwb"] = ffn_gate_bwd("ffn_gate_bwd", sv["h"], dact, p["cwb"])
    dxn = ffn_dx("ffn_up_dx", dh, p["w_up"])
    g["w_up"] = ffn_dw_up("ffn_up_dw", sv["xn"], dh)
    return dxn, g


def _out_proj_bwd(dxn, dx2, sv, p, after=0.0):
    g = {}
    dm = dx2.shape[1]
    dx1, dx1b, g["ffn_norm"] = rmsnorm_bwd("ffn_norm_bwd", dxn, sv["x1"], p["ffn_norm"] + after, dm, (F32, BF16), add=dx2)
    dy = matmul("out_proj_dx", dx1b, p["w_out"], "nt", F32)
    g["w_out"] = matmul("out_proj_dw", sv["y"], dx1b, "tn", F32)
    return dx1, dy, g


def _attn_norm_bwd(dhn, dx1, sv, p, after=0.0):
    dx, d_gain = rmsnorm_bwd("attn_norm_bwd", dhn, sv["x"], p["attn_norm"] + after, dhn.shape[1], (F32,), add=dx1)
    return dx, {"attn_norm": d_gain}


def _mixers_bwd(dy, sv, p, tabs, after=0.0):
    g = {}
    out_norm = p["out_norm"] + after
    g_a, g_b, g_c = out_norm[:A_W], out_norm[A_W:A_W + B_W], out_norm[A_W + B_W:]
    dya, dga = rmsnorm_bwd("out_norm_a_bwd", dy[:, :A_W], sv["ya"], g_a, A_W, (BF16,))
    dyb, dgb = rmsnorm_bwd("out_norm_b_bwd", dy[:, A_W:A_W + B_W], sv["yb"], g_b, B_W, (F32,))
    dyc, dgc = rmsnorm_bwd("out_norm_c_bwd", dy[:, A_W + B_W:], sv["yc"], g_c, C_W, (BF16,))
    g["out_norm"] = jnp.concatenate([dga, dgb, dgc])

    c_scale = (HEAD + C_ROPE) ** -0.5
    dqn, dqr, dkn, dvc, dkr = attn_bwd_c("c_attn_bwd", sv["qc"], sv["kvc"], sv["krc"], sv["lse_c"], dyc, c_scale)
    s = dy.shape[0]
    dqc = jnp.concatenate([dqn.astype(BF16), rope("c_q_rope_bwd", dqr, tabs["c"][1], BF16)], axis=1)
    dcqn = matmul("c_uq_dx", dqc, p["w_uq_t"], "nn", F32)
    g["w_uq"] = matmul("c_uq_dw", dqc, sv["cqn"], "tn", F32)
    dcq, g["c_q_norm"] = rmsnorm_bwd("c_q_norm_bwd", dcqn, sv["cq"], p["c_q_norm"], C_RANK, (BF16,))
    dkvc = jnp.stack([dkn.reshape(s, C_HEADS, HEAD), dvc.reshape(s, C_HEADS, HEAD)], axis=2).reshape(s, 2 * C_W).astype(BF16)
    dckvn = matmul("c_ukv_dx", dkvc, p["w_ukv"], "nt", F32)
    g["w_ukv"] = matmul("c_ukv_dw", sv["ckvn"], dkvc, "tn", F32)
    dckv, g["c_kv_norm"] = rmsnorm_bwd("c_kv_norm_bwd", dckvn, sv["ckv"], p["c_kv_norm"], C_RANK, (BF16,))
    dckr = rope("c_k_rope_bwd", dkr, tabs["c"][1], BF16)

    dqb, dkb, dvb = dilated_bwd("b_dilated_bwd", sv["qb"], sv["kb"], sv["proj"], O_BV // HEAD, dyb, sv["yb"], sv["lse_b"])
    dbq = rope("b_q_rope_bwd", dqb, tabs["b"][1], BF16)
    dbk = rope("b_k_rope_bwd", dkb, tabs["b"][1], BF16)
    dbv = dvb.astype(BF16)

    kg = A_HEADS // A_KV
    (dqa,), (dka,), dva = attn_bwd("a_attn_bwd", [(sv["qa"], 0, sv["ka"], kg)], sv["av"], kg, sv["lse_a"], dya,
                                   A_HEADS, HEAD ** -0.5)
    daq, g["a_q_norm"] = rmsnorm_bwd("a_q_norm_bwd", rope("a_q_rope_bwd", dqa, tabs["a"][1], F32), sv["aq"],
                                     p["a_q_norm"], HEAD, (BF16,))
    dak, g["a_k_norm"] = rmsnorm_bwd("a_k_norm_bwd", rope("a_k_rope_bwd", dka, tabs["a"][1], F32), sv["ak"],
                                     p["a_k_norm"], HEAD, (BF16,))

    dproj = jnp.concatenate([daq, dak, dva.astype(BF16), dbq, dbk, dbv, dcq, dckv, dckr[:, :C_ROPE]], axis=1)
    dhn = matmul("in_proj_dx", dproj, p["w_in_t"], "nn", F32, tm=512)
    g["w_in"] = matmul("in_proj_dw", dproj, sv["hn"], "tn", F32, tn=512, tk=512)
    return dhn, g


def _local_step(x, target, layers, final_norm, tabs):
    saved = []
    for p in layers:
        x, sv = _layer_fwd(x, p, tabs)
        saved.append(sv)
    dm = x.shape[1]
    yf = rmsnorm_fwd("final_norm", x, final_norm, dm, F32)
    loss, dyf = loss_head("loss_head", yf, target)
    dx, d_final = rmsnorm_bwd("final_norm_bwd", dyf, x, final_norm, dm, (F32,))
    grads = [None] * len(layers)
    for l in reversed(range(len(layers))):
        dx, grads[l] = _layer_bwd(dx, saved[l], layers[l], tabs)
    return loss, dx, grads, d_final


HBM = pl.BlockSpec(memory_space=pl.ANY)


def _place():
    x, y, c = lax.axis_index("x"), lax.axis_index("y"), lax.axis_index("c")
    others = [(1 - x, y), (x, 1 - y), (1 - x, 1 - y)]
    return x, y, c, 2 * x + y, others


def _remote(src, dst, send_sem, recv_sem, device):
    return pltpu.make_async_remote_copy(src_ref=src, dst_ref=dst, send_sem=send_sem, recv_sem=recv_sem,
                                        device_id=device, device_id_type=MESH)


def gather_small(name, small):
    def body(in_ref, out_ref, send_sems, recv_sems, local_sem):
        x, y, c, chip, others = _place()
        mine = pltpu.make_async_copy(in_ref, out_ref.at[chip], local_sem)
        mine.start()
        copies = []
        for k, (ox, oy) in enumerate(others):
            cp = _remote(in_ref, out_ref.at[chip], send_sems.at[k], recv_sems.at[k], (ox, oy, c))
            cp.start()
            copies.append(cp)
        for k, (ox, oy) in enumerate(others):
            landed = out_ref.at[2 * ox + oy]
            _remote(landed, landed, send_sems.at[k], recv_sems.at[k], (ox, oy, c)).wait_recv()
        for cp in copies:
            cp.wait_send()
        mine.wait()

    return pl.pallas_call(
        body, name=name, in_specs=[HBM], out_specs=HBM, out_shape=jax.ShapeDtypeStruct((4,) + small.shape, small.dtype),
        scratch_shapes=[pltpu.SemaphoreType.DMA((3,)), pltpu.SemaphoreType.DMA((3,)), pltpu.SemaphoreType.DMA],
        compiler_params=pltpu.CompilerParams(has_side_effects=True))(small)


IN_HBM = pl.BlockSpec(memory_space=pltpu.HBM)
SEMS = pl.BlockSpec(memory_space=pltpu.SEMAPHORE)
DATAFLOW = pltpu.SideEffectType.DATAFLOW_SIDE_EFFECTING


BF16_ROWS_PER_TILE = 16


def _halves_by_rows(shape):
    return (shape[-2] // 2) % BF16_ROWS_PER_TILE == 0


def _half(shape, core):
    if _halves_by_rows(shape):
        size = shape[-2] // 2
        return (pl.ds(core * size, size), slice(None))
    size = shape[-1] // 2
    return (slice(None), pl.ds(core * size, size))


def _half_shape(shape):
    r, c = shape[-2:]
    return (r // 2, c) if _halves_by_rows(shape) else (r, c // 2)


def _gather_plan(bufs):
    x, y, c, chip, others = _place()
    plan = []
    for t, ref in enumerate(bufs):
        mine = _half(ref.shape, c)
        for k, (ox, oy) in enumerate(others):
            plan.append((3 * t + k, (ox, oy, c), ref.at[(chip,) + mine], ref.at[(chip,) + mine],
                         ref.at[(2 * ox + oy,) + mine]))
    return plan


def _exchange_plan(bufs):
    x, y, c, chip, others = _place()
    n = len(bufs) // 2
    plan = []
    for t in range(n):
        for k, (ox, oy) in enumerate(others):
            plan.append((3 * t + k, (ox, oy, c), bufs[t].at[2 * ox + oy], bufs[n + t].at[chip], bufs[n + t].at[2 * ox + oy]))
    return plan


def _sibling_plan(bufs):
    x, y, c, chip, others = _place()
    n = len(bufs) // 2
    plan = []
    for t in range(n):
        plan.append((t, (x, y, 1 - c), bufs[t].at[(slice(None),) + _half(bufs[t].shape, 1 - c)], bufs[n + t], bufs[n + t]))
    return plan


PLAN_COPIES = {_gather_plan: lambda n: 3 * n, _exchange_plan: lambda n: 3 * (n // 2), _sibling_plan: lambda n: n // 2}


def split_start(name, bufs, plan_of, after):
    n = len(bufs)

    def body(*refs):
        ins = refs[:n]
        send_sems, recv_sems = refs[n + 1], refs[n + 2]
        token = refs[-1]
        for idx, peer, src, dst, _ in plan_of(ins):
            _remote(src, dst, send_sems.at[idx], recv_sems.at[idx], peer).start()
        token[...] = jnp.zeros_like(token)

    copies = PLAN_COPIES[plan_of](n)
    res = pl.pallas_call(
        body, name=name, in_specs=[IN_HBM] * n + [HBM],
        out_specs=(SEMS, SEMS) + (IN_HBM,) * n + (pl.BlockSpec(memory_space=pltpu.VMEM),),
        out_shape=(pltpu.SemaphoreType.DMA((copies,)), pltpu.SemaphoreType.DMA((copies,)))
        + tuple(pltpu.HBM(b.shape, b.dtype) for b in bufs) + (jax.ShapeDtypeStruct((8, LANES), F32),),
        input_output_aliases={t: 2 + t for t in range(n)},
        compiler_params=pltpu.CompilerParams(has_side_effects=DATAFLOW))(
            *[pltpu.with_memory_space_constraint(b, pltpu.HBM) for b in bufs], after)
    return res[0], res[1], list(res[2:2 + n]), res[-1]


def split_wait(name, send_sems, recv_sems, flying, plan_of, after):
    n = len(flying)

    def body(*refs):
        ins = refs[:n]
        send_ref, recv_ref = refs[n], refs[n + 1]
        for idx, peer, src, dst, landing in plan_of(ins):
            _remote(src, dst, send_ref.at[idx], recv_ref.at[idx], peer).wait_send()
            _remote(landing, landing, send_ref.at[idx], recv_ref.at[idx], peer).wait_recv()

    return list(pl.pallas_call(
        body, name=name, in_specs=[IN_HBM] * n + [SEMS, SEMS] + [HBM] * len(after), out_specs=(IN_HBM,) * n,
        out_shape=tuple(pltpu.HBM(b.shape, b.dtype) for b in flying),
        input_output_aliases={t: t for t in range(n)},
        compiler_params=pltpu.CompilerParams(has_side_effects=DATAFLOW))(*flying, send_sems, recv_sems, *after))


def forward_halves(name, slots):
    n = len(slots)

    def body(*refs):
        ins, outs = refs[:n], refs[n:2 * n]
        send_sems, recv_sems = refs[2 * n:]
        x, y, c, chip, others = _place()
        sibling = (x, y, 1 - c)
        copies = []
        for t in range(n):
            mine = _half(slots[t].shape, c)
            for k, (ox, oy) in enumerate(others):
                cp = _remote(ins[t].at[(2 * ox + oy,) + mine], outs[t].at[(2 * ox + oy,) + mine], send_sems.at[t, k],
                             recv_sems.at[t, k], sibling)
                cp.start()
                copies.append(cp)
        for t in range(n):
            for k, (ox, oy) in enumerate(others):
                theirs = outs[t].at[(2 * ox + oy,) + _half(slots[t].shape, 1 - c)]
                _remote(theirs, theirs, send_sems.at[t, k], recv_sems.at[t, k], sibling).wait_recv()
        for cp in copies:
            cp.wait_send()

    return list(pl.pallas_call(
        body, name=name, in_specs=[HBM] * n, out_specs=[HBM] * n,
        out_shape=[jax.ShapeDtypeStruct(s.shape, s.dtype) for s in slots],
        input_output_aliases={t: t for t in range(n)},
        scratch_shapes=[pltpu.SemaphoreType.DMA((n, 3))] * 2,
        compiler_params=pltpu.CompilerParams(has_side_effects=True))(*slots))


def _half_block(shape, tr):
    r, c = shape[-2:]
    if _halves_by_rows(shape):
        per = r // 2 // tr
        return tr, c, lambda i, core: (core * per + i, 0)
    return tr, c // 2, lambda i, core: (i, core)


def add_half(name, place, grad, theirs):
    hr, hc = _half_shape(grad.shape)
    tr = _rows(hr, hc)
    br, bc, at = _half_block(grad.shape, tr)

    def body(place_ref, g_ref, t_ref, o_ref):
        o_ref[...] = (g_ref[...] + t_ref[...]).astype(o_ref.dtype)

    whole = pl.BlockSpec((None, br, bc), lambda j, i, pr: (j, i, 0))
    return pl.pallas_call(
        body, name=name,
        grid_spec=pltpu.PrefetchScalarGridSpec(
            num_scalar_prefetch=1, grid=(4, hr // tr),
            in_specs=[pl.BlockSpec((None, br, bc), lambda j, i, pr: (j,) + at(i, pr[0])), whole],
            out_specs=whole),
        out_shape=jax.ShapeDtypeStruct((4, hr, hc), BF16),
        compiler_params=_params("parallel", "parallel"))(place, grad, theirs)


def sum_chips(name, place, own, parts, shard_shape):
    _, hr, hc = parts.shape
    tr = _rows(hr, hc)
    br, bc, at = _half_block(shard_shape, tr)

    def body(place_ref, own_ref, p1, p2, p3, o_ref):
        o_ref[...] = own_ref[...].astype(F32) + p1[...].astype(F32) + p2[...].astype(F32) + p3[...].astype(F32)

    def slot(k):
        return pl.BlockSpec((None, br, bc), lambda i, pr: (lax.rem(pr[1] + k, 4), i, 0))

    return pl.pallas_call(
        body, name=name,
        grid_spec=pltpu.PrefetchScalarGridSpec(
            num_scalar_prefetch=1, grid=(hr // tr,), in_specs=[slot(0), slot(1), slot(2), slot(3)],
            out_specs=pl.BlockSpec((br, bc), lambda i, pr: at(i, pr[0]))),
        out_shape=jax.ShapeDtypeStruct(tuple(shard_shape), F32),
        compiler_params=_params("parallel"))(place, own, parts, parts, parts)


def join_halves(name, halves):
    n = len(halves)

    def body(*refs):
        ins, outs = refs[:n], refs[n:2 * n]
        send_sems, recv_sems = refs[2 * n:]
        x, y, c, _, _ = _place()
        copies = []
        for t in range(n):
            mine = _half(halves[t].shape, c)
            cp = _remote(ins[t].at[mine], outs[t].at[mine], send_sems.at[t], recv_sems.at[t], (x, y, 1 - c))
            cp.start()
            copies.append(cp)
        for t in range(n):
            theirs = outs[t].at[_half(halves[t].shape, 1 - c)]
            _remote(theirs, theirs, send_sems.at[t], recv_sems.at[t], (x, y, 1 - c)).wait_recv()
        for cp in copies:
            cp.wait_send()

    return pl.pallas_call(
        body, name=name, in_specs=[HBM] * n, out_specs=[HBM] * n,
        out_shape=[jax.ShapeDtypeStruct(h.shape, h.dtype) for h in halves],
        input_output_aliases={t: t for t in range(n)},
        scratch_shapes=[pltpu.SemaphoreType.DMA((n,))] * 2,
        compiler_params=pltpu.CompilerParams(has_side_effects=True))(*halves)


def allreduce_small(name, buf):
    rows = buf.shape[0]

    def body(in_ref, out_ref, slots, send_sems, recv_sems):
        x, y, c, _, _ = _place()
        me = 4 * x + 2 * y + c
        slots[me] = in_ref[...]
        peers = []
        for k in range(1, 8):
            px = 1 - x if k & 4 else x
            py = 1 - y if k & 2 else y
            pc = 1 - c if k & 1 else c
            peers.append((px, py, pc))
        copies = []
        for k, peer in enumerate(peers):
            cp = _remote(in_ref, slots.at[me], send_sems.at[k], recv_sems.at[k], peer)
            cp.start()
            copies.append(cp)
        for k, (px, py, pc) in enumerate(peers):
            slot = slots.at[4 * px + 2 * py + pc]
            _remote(slot, slot, send_sems.at[k], recv_sems.at[k], (px, py, pc)).wait_recv()
        for cp in copies:
            cp.wait_send()
        acc = slots[0]
        for d in range(1, 8):
            acc = acc + slots[d]
        out_ref[...] = acc

    vm = pl.BlockSpec(memory_space=pltpu.VMEM)
    return pl.pallas_call(
        body, name=name, in_specs=[vm], out_specs=vm, out_shape=jax.ShapeDtypeStruct(buf.shape, F32),
        scratch_shapes=[pltpu.VMEM((8, rows, LANES), F32), pltpu.SemaphoreType.DMA((7,)), pltpu.SemaphoreType.DMA((7,))],
        compiler_params=pltpu.CompilerParams(has_side_effects=True, vmem_limit_bytes=VMEM_LIMIT))(buf)


def cast_layer(name, place, w, layer):
    _, r, cols = w.shape
    tr = _rows(r, cols)

    def body(place_ref, w_ref, o_ref):
        o_ref[...] = w_ref[...].astype(BF16)

    return pl.pallas_call(
        body, name=name,
        grid_spec=pltpu.PrefetchScalarGridSpec(
            num_scalar_prefetch=1, grid=(r // tr,),
            in_specs=[pl.BlockSpec((None, tr, cols), lambda i, pr: (layer, i, 0))],
            out_specs=pl.BlockSpec((None, tr, cols), lambda i, pr: (pr[1], i, 0))),
        out_shape=jax.ShapeDtypeStruct((4, r, cols), BF16),
        compiler_params=_params("parallel"))(place, w)


def _adamw_math(g, w, m, v):
    m = ADAM_B1 * m + (1.0 - ADAM_B1) * g
    v = ADAM_B2 * v + (1.0 - ADAM_B2) * (g * g)
    m_hat = m / (1.0 - ADAM_B1 ** ADAM_STEP)
    v_hat = v / (1.0 - ADAM_B2 ** ADAM_STEP)
    delta = -ADAM_LR * (m_hat / (jnp.sqrt(v_hat) + ADAM_EPS) + ADAM_WD * w)
    return delta, m, v


def adamw_layer(name, layer, g, w, m, v, prev=None):
    nl, r, cols = w.shape
    tr = _rows(r, cols, 8)

    def body(*refs):
        g_ref, w_ref, m_ref, v_ref = refs[:4]
        og, od, om, ov = refs[-4:]
        gv = g_ref[...]
        delta, m2, v2 = _adamw_math(gv, w_ref[...], m_ref[...], v_ref[...])
        og[...] = gv
        od[...] = delta
        om[...] = m2
        ov[...] = v2

    lay = pl.BlockSpec((None, tr, cols), lambda i: (layer, i, 0))
    ins = [g, w, m, v] + (list(prev) if prev is not None else [])
    return pl.pallas_call(
        body, name=name, grid=(r // tr,),
        in_specs=[pl.BlockSpec((tr, cols), lambda i: (i, 0)), lay, lay, lay] + ([HBM] * 4 if prev is not None else []),
        out_specs=[lay] * 4, out_shape=[jax.ShapeDtypeStruct((nl, r, cols), F32)] * 4,
        input_output_aliases={4 + k: k for k in range(4)} if prev is not None else {},
        compiler_params=_params("parallel"))(*ins)


def adamw_packed(name, g, w, m, v):
    def body(g_ref, w_ref, m_ref, v_ref, od, om, ov):
        delta, m2, v2 = _adamw_math(g_ref[...], w_ref[...], m_ref[...], v_ref[...])
        od[...] = delta
        om[...] = m2
        ov[...] = v2

    vm = pl.BlockSpec(memory_space=pltpu.VMEM)
    return pl.pallas_call(
        body, name=name, in_specs=[vm] * 4, out_specs=[vm] * 3, out_shape=[jax.ShapeDtypeStruct(g.shape, F32)] * 3,
        compiler_params=pltpu.CompilerParams(vmem_limit_bytes=VMEM_LIMIT))(g, w, m, v)


def _pack(arrays):
    flat = jnp.concatenate([a.reshape(-1) for a in arrays])
    pad = (-flat.shape[0]) % (8 * LANES)
    return jnp.pad(flat, (0, pad)).reshape(-1, LANES)


def _unpack(buf, shapes):
    flat = buf.reshape(-1)
    out, off = [], 0
    for shp in shapes:
        size = int(np.prod(shp))
        out.append(flat[off:off + size].reshape(shp))
        off += size
    return out


MIXER_W = ("w_in", "w_uq", "w_ukv", "w_out")
FFN_W = ("w_up", "w_down")
BIG = MIXER_W + FFN_W
TRANSPOSED = ("w_in", "w_uq")
COLUMN_CUT = ("w_ukv", "w_up")
SMALL = ("attn_norm", "a_q_norm", "a_k_norm", "c_q_norm", "c_kv_norm", "out_norm", "ffn_norm", "conv_b", "final_norm")
WEIGHTS = ("attn_norm", "w_in", "a_q_norm", "a_k_norm", "c_q_norm", "c_kv_norm", "w_uq", "w_ukv", "out_norm", "w_out",
           "ffn_norm", "w_up", "conv_w", "conv_b", "w_down", "final_norm")
INPUTS = ("x",) + WEIGHTS + ("loss_target",) + tuple("m_" + n for n in WEIGHTS) + tuple("v_" + n for n in WEIGHTS)


def _columns(g):
    return jnp.transpose(g, (1, 0, 2)).reshape(g.shape[1], 4 * g.shape[2])


def _uncolumns(w):
    r, c4 = w.shape
    return jnp.transpose(w.reshape(r, 4, c4 // 4), (1, 0, 2))


def _compute_layouts(full, conv_w, small, layer):
    p = {n: small[n][layer] for n in SMALL if n != "final_norm"}
    if "w_in" in full:
        p["w_in_t"] = full["w_in"].reshape(-1, full["w_in"].shape[2])
        uq_t = full["w_uq"].reshape(C_HEADS, HEAD + C_ROPE, C_RANK)
        p["w_uq_t"] = jnp.concatenate([uq_t[:, :HEAD].reshape(C_W, C_RANK),
                                       jnp.pad(uq_t[:, HEAD:], ((0, 0), (0, HEAD - C_ROPE), (0, 0))).reshape(C_W, C_RANK)], axis=0)
        p["w_ukv"] = _columns(full["w_ukv"])
        p["w_out"] = full["w_out"].reshape(-1, full["w_out"].shape[2])
    if "w_up" in full:
        up = full["w_up"]
        p["w_up"] = up.reshape(2, 2, up.shape[1], up.shape[2])
        p["w_down"] = full["w_down"].reshape(-1, full["w_down"].shape[2])
        fh = conv_w.shape[1] // 2
        taps = jnp.transpose(conv_w.reshape(3, 2, fh), (1, 0, 2))
        p["cwb"] = jnp.concatenate([taps, small["conv_b"][layer].reshape(2, 1, fh), jnp.zeros((2, 4, fh), F32)], axis=1)
    return p


def _shard_layout(name, g):
    if name == "w_in":
        return g.reshape(4, -1, g.shape[1])
    if name == "w_uq":
        uq_t = jnp.concatenate([g[:C_W].reshape(C_HEADS, HEAD, C_RANK),
                                g[C_W:].reshape(C_HEADS, HEAD, C_RANK)[:, :C_ROPE]], axis=1)
        return uq_t.reshape(4, -1, C_RANK)
    if name == "w_ukv":
        return _uncolumns(g)
    if name == "w_up":
        return g.reshape(4, g.shape[2], g.shape[3])
    return g.reshape(4, -1, g.shape[1])


def _conv_grads(dcwb):
    fh = dcwb.shape[2]
    return jnp.transpose(dcwb[:, 0:3, :], (1, 0, 2)).reshape(3, 2 * fh), dcwb[:, 3, :].reshape(2 * fh)


def _shard_layouts(g):
    return ({n: _shard_layout(n, g[n]) for n in BIG},) + _conv_grads(g["cwb"])


def kernel(x, attn_norm, w_in, a_q_norm, a_k_norm, c_q_norm, c_kv_norm, w_uq, w_ukv, out_norm, w_out, ffn_norm, w_up, conv_w, conv_b, w_down, final_norm, loss_target, m_attn_norm, m_w_in, m_a_q_norm, m_a_k_norm, m_c_q_norm, m_c_kv_norm, m_w_uq, m_w_ukv, m_out_norm, m_w_out, m_ffn_norm, m_w_up, m_conv_w, m_conv_b, m_w_down, m_final_norm, v_attn_norm, v_w_in, v_a_q_norm, v_a_k_norm, v_c_q_norm, v_c_kv_norm, v_w_uq, v_w_ukv, v_out_norm, v_w_out, v_ffn_norm, v_w_up, v_conv_w, v_conv_b, v_w_down, v_final_norm):
    a = dict(zip(INPUTS, (x, attn_norm, w_in, a_q_norm, a_k_norm, c_q_norm, c_kv_norm, w_uq, w_ukv, out_norm, w_out, ffn_norm, w_up, conv_w, conv_b, w_down, final_norm, loss_target, m_attn_norm, m_w_in, m_a_q_norm, m_a_k_norm, m_c_q_norm, m_c_kv_norm, m_w_uq, m_w_ukv, m_out_norm, m_w_out, m_ffn_norm, m_w_up, m_conv_w, m_conv_b, m_w_down, m_final_norm, v_attn_norm, v_w_in, v_a_q_norm, v_a_k_norm, v_c_q_norm, v_c_kv_norm, v_w_uq, v_w_ukv, v_out_norm, v_w_out, v_ffn_norm, v_w_up, v_conv_w, v_conv_b, v_w_down, v_final_norm)))
    nl = w_in.shape[0]
    seq = x.shape[1]
    for n in TRANSPOSED:
        for kind in ("", "m_", "v_"):
            a[kind + n] = jnp.swapaxes(a[kind + n], 1, 2)
    chip = 2 * lax.axis_index("x") + lax.axis_index("y")
    place = jnp.stack([lax.axis_index("c"), chip]).astype(jnp.int32)
    tabs = _rope_tables(seq)

    assert nl == 2
    quarter = conv_w.shape[2]
    taps = jnp.pad(conv_w, ((0, 0), (0, 8 - conv_w.shape[1]), (0, 0))).reshape(nl * 8, quarter)
    taps_full = gather_small("gather_taps", taps).reshape(4, nl, 8, quarter)
    conv_full = [jnp.transpose(taps_full[:, l, 0:3, :], (1, 0, 2)).reshape(3, 4 * quarter) for l in range(nl)]
    flights = {}
    after = taps_full
    for key, l, names in (("mix0", 0, MIXER_W), ("ffn0", 0, FFN_W), ("all1", 1, BIG)):
        slots = [cast_layer(f"cast{l}_{n}", place, a[n], l) for n in names]
        send_sems, recv_sems, flying, after = split_start(f"gather_{key}_start", slots, _gather_plan, after)
        flights[key] = (names, send_sems, recv_sems, flying)
    started = after[0, 0]

    def land(key, after):
        names, send_sems, recv_sems, flying = flights[key]
        landed = split_wait(f"gather_{key}_wait", send_sems, recv_sems, flying, _gather_plan, after)
        return dict(zip(names, forward_halves(f"gather_{key}_forward", landed)))

    h = x[0]
    layers, saved = [], []
    for l in range(nl):
        if l == 0:
            p = _compute_layouts(land("mix0", [after]), conv_full[l], a, l)
            p["attn_norm"] = p["attn_norm"] + started
            x1, sv = _mixers_fwd(h, p, tabs)
            p.update(_compute_layouts(land("ffn0", [x1]), conv_full[l], a, l))
        else:
            p = _compute_layouts(land("all1", [h]), conv_full[l], a, l)
            x1, sv = _mixers_fwd(h, p, tabs)
        h = _ffn_fwd(x1, p, sv)
        layers.append(p)
        saved.append(sv)
    dm = h.shape[1]
    yf = rmsnorm_fwd("final_norm", h, final_norm, dm, F32)
    loss, dyf = loss_head("loss_head", yf, loss_target[0])
    loss = lax.psum(loss, ("x", "y", "c"))
    dx, d_final = rmsnorm_bwd("final_norm_bwd", dyf, h, final_norm, dm, (F32,))

    def swap_begin(tag, names, g):
        g_list = [_shard_layout(n, g[n]) for n in names]
        theirs = [lax.empty((4,) + _half_shape(s.shape), s.dtype) for s in g_list]
        send_sems, recv_sems, flying, token = split_start(f"{tag}_sibling_start", g_list + theirs, _sibling_plan, place)
        return (names, send_sems, recv_sems, flying), token[0, 0]

    def exchange_begin(tag, swap, after):
        names, send_sems, recv_sems, flying = swap
        landed = split_wait(f"{tag}_sibling_wait", send_sems, recv_sems, flying, _sibling_plan, after)
        g_list, theirs = landed[:len(names)], landed[len(names):]
        sums = [add_half(f"{tag}_add_{n}", place, gr, t) for n, gr, t in zip(names, g_list, theirs)]
        parts = [lax.empty(s.shape, s.dtype) for s in sums]
        send_sems, recv_sems, flying, token = split_start(f"{tag}_chips_start", sums + parts, _exchange_plan, place)
        return (names, [gr.shape[1:] for gr in g_list], send_sems, recv_sems, flying), token

    def exchange_end(tag, flight, after):
        names, shard_shapes, send_sems, recv_sems, flying = flight
        landed = split_wait(f"{tag}_chips_wait", send_sems, recv_sems, flying, _exchange_plan, after)
        sums, parts = landed[:len(names)], landed[len(names):]
        halves = [sum_chips(f"{tag}_sum_{n}", place, s, p, shp) for n, s, p, shp in zip(names, sums, parts, shard_shapes)]
        return dict(zip(names, join_halves(f"{tag}_join", halves)))

    grads, conv_grads, flights_f, flights_m = [None] * nl, [None] * nl, [None] * nl, [None] * nl
    after = 0.0
    for l in reversed(range(nl)):
        sv, p = saved[l], layers[l]
        dxn, g = _ffn_bwd(dx, sv, p, after)
        swap_f, after = swap_begin(f"reduce{l}f", FFN_W, g)
        dx1, dy, g_out = _out_proj_bwd(dxn, dx, sv, p, after)
        flights_f[l], token = exchange_begin(f"reduce{l}f", swap_f, [dy])
        dhn, g_mix = _mixers_bwd(dy, sv, p, tabs, token[0, 0])
        g.update(g_out)
        g.update(g_mix)
        swap_m, after = swap_begin(f"reduce{l}m", MIXER_W, g)
        dx, g_norm = _attn_norm_bwd(dhn, dx1, sv, p, after)
        g.update(g_norm)
        flights_m[l], token = exchange_begin(f"reduce{l}m", swap_m, [dx])
        after = token[0, 0]
        grads[l], conv_grads[l] = g, _conv_grads(g["cwb"])
    grad_x = dx

    out = {}
    reduced1 = exchange_end("reduce1f", flights_f[1], [token])
    for n in FFN_W:
        out[n] = adamw_layer(f"adamw1_{n}", 1, reduced1[n], a[n], a["m_" + n], a["v_" + n])
    reduced1 = exchange_end("reduce1m", flights_m[1], [out[n][3] for n in FFN_W])
    for n in MIXER_W:
        out[n] = adamw_layer(f"adamw1_{n}", 1, reduced1[n], a[n], a["m_" + n], a["v_" + n])

    small_g = {n: jnp.stack([grads[l][n] for l in range(nl)]) for n in SMALL if n not in ("conv_b", "final_norm")}
    small_g["conv_b"] = jnp.stack([cg[1] for cg in conv_grads])
    small_g["final_norm"] = d_final
    conv_w_g = jnp.stack([cg[0] for cg in conv_grads])
    shapes = [a[n].shape for n in SMALL] + [conv_w_g.shape]
    summed = _unpack(allreduce_small("reduce_small", _pack([small_g[n] for n in SMALL] + [conv_w_g])), shapes)
    small_g = dict(zip(SMALL, summed[:-1]))
    small_g["conv_w"] = lax.dynamic_slice_in_dim(summed[-1], chip * quarter, quarter, axis=2)
    names = SMALL + ("conv_w",)
    shapes = [a[n].shape for n in names]
    d_small, m_small, v_small = adamw_packed(
        "adamw_small", _pack([small_g[n] for n in names]), _pack([a[n] for n in names]),
        _pack([a["m_" + n] for n in names]), _pack([a["v_" + n] for n in names]))

    reduced0 = exchange_end("reduce0f", flights_f[0], [out[n][3] for n in BIG] + [d_small])
    for n in FFN_W:
        out[n] = adamw_layer(f"adamw0_{n}", 0, reduced0[n], a[n], a["m_" + n], a["v_" + n], prev=out[n])
    reduced0 = exchange_end("reduce0m", flights_m[0], [out[n][3] for n in FFN_W])
    for n in MIXER_W:
        out[n] = adamw_layer(f"adamw0_{n}", 0, reduced0[n], a[n], a["m_" + n], a["v_" + n], prev=out[n])
    for n, d_, m_, v_ in zip(names, _unpack(d_small, shapes), _unpack(m_small, shapes), _unpack(v_small, shapes)):
        out[n] = (small_g[n], d_, m_, v_)
    for n in TRANSPOSED:
        out[n] = [jnp.swapaxes(o, 1, 2) for o in out[n]]

    return (loss, grad_x[None]) + tuple(out[n][k] for k in range(4) for n in WEIGHTS)
```

```python
import functools

import jax
import jax.numpy as jnp
import numpy as np
from jax import lax
from jax.experimental import pallas as pl
from jax.experimental.pallas import tpu as pltpu

F32 = jnp.float32
BF16 = jnp.bfloat16
MESH = pl.DeviceIdType.MESH

V7X_VMEM_BYTES = 64 * 1024 * 1024
VMEM_LIMIT = V7X_VMEM_BYTES - 6 * 1024 * 1024
LANES = 128
ELEMENTWISE_BLOCK_BYTES = 2 * 1024 * 1024

HEAD = 128
A_HEADS, A_KV = 6, 2
B_HEADS, B_GROUPS = 4, 3
B_DILATIONS = (1, 4, 16)
B_HALF = 64
C_HEADS, C_RANK, C_ROPE = 6, 512, 64
GRID_W = 64
A_THETA, B_THETA, C_THETA = 10000.0, 500000.0, 10000.0
B_ROPE_DIM = 32
EPS = 1e-6
A_W, B_W, C_W = A_HEADS * HEAD, B_HEADS * HEAD, C_HEADS * HEAD
O_AQ, O_AK, O_AV = 0, 768, 1024
O_BQ, O_BK, O_BV = 1280, 2816, 3328
O_CQ, O_CKV, O_CKR = 3840, 4352, 4864
IN_W = 4928
PROJ_W = 5120

ADAM_LR, ADAM_B1, ADAM_B2, ADAM_EPS, ADAM_WD, ADAM_STEP = 0.001, 0.9, 0.999, 1e-08, 0.01, 10

NN = (((1,), (0,)), ((), ()))
NT = (((1,), (1,)), ((), ()))
TN = (((0,), (0,)), ((), ()))


def _dot(a, b, dims=NN):
    return lax.dot_general(a, b, dims, preferred_element_type=F32)


def _params(*sem):
    return pltpu.CompilerParams(dimension_semantics=sem if sem else None, vmem_limit_bytes=VMEM_LIMIT)


def _tile(n, target, unit=LANES):
    if n <= target:
        return n
    best = 0
    for t in range(unit, target + 1, unit):
        if n % t == 0:
            best = t
    return best if best else n


def _rows(r, width, itemsize=4):
    return _tile(r, max(8, ELEMENTWISE_BLOCK_BYTES // (width * itemsize)), 8)


def _matmul_call(name, dims, grid, a_spec, b_spec, o_spec, out_shape, acc_shape, add_spec=None):
    nk = grid[2]

    def body(*refs):
        a_ref, b_ref = refs[:2]
        add_ref = None if add_spec is None else refs[2]
        o_ref = refs[2 if add_spec is None else 3]

        def finish(r):
            if add_ref is not None:
                r = r + add_ref[...]
            o_ref[...] = r.astype(o_ref.dtype)

        if nk == 1:
            finish(_dot(a_ref[...], b_ref[...], dims))
            return
        acc = refs[-1]
        k = pl.program_id(2)

        @pl.when(k == 0)
        def _():
            acc[...] = _dot(a_ref[...], b_ref[...], dims)

        if nk > 2:
            @pl.when((k > 0) & (k < nk - 1))
            def _():
                acc[...] += _dot(a_ref[...], b_ref[...], dims)

        @pl.when(k == nk - 1)
        def _():
            finish(acc[...] + _dot(a_ref[...], b_ref[...], dims))

    in_specs = [a_spec, b_spec] + ([] if add_spec is None else [add_spec])
    return pl.pallas_call(
        body, name=name, grid=grid, in_specs=in_specs, out_specs=o_spec, out_shape=out_shape,
        scratch_shapes=[pltpu.VMEM(acc_shape, F32)] if nk > 1 else [],
        compiler_params=_params("parallel", "parallel", "arbitrary"))


def matmul(name, a, b, mode, out_dtype, add=None, tm=1024, tn=1024, tk=2816):
    if mode == "nn":
        (m, k), (k2, n) = a.shape, b.shape
    elif mode == "nt":
        (m, k), (n, k2) = a.shape, b.shape
    else:
        (k, m), (k2, n) = a.shape, b.shape
    assert k == k2, (name, a.shape, b.shape)
    tm, tn, tk = _tile(m, tm), _tile(n, tn), _tile(k, tk)
    grid = (m // tm, n // tn, k // tk)
    if mode == "tn":
        a_spec = pl.BlockSpec((tk, tm), lambda i, j, kk: (kk, i))
    else:
        a_spec = pl.BlockSpec((tm, tk), lambda i, j, kk: (i, kk))
    if mode == "nt":
        b_spec = pl.BlockSpec((tn, tk), lambda i, j, kk: (j, kk))
    else:
        b_spec = pl.BlockSpec((tk, tn), lambda i, j, kk: (kk, j))
    o_spec = pl.BlockSpec((tm, tn), lambda i, j, kk: (i, j))
    dims = {"nn": NN, "nt": NT, "tn": TN}[mode]
    call = _matmul_call(name, dims, grid, a_spec, b_spec, o_spec, jax.ShapeDtypeStruct((m, n), out_dtype),
                        (tm, tn), None if add is None else o_spec)
    return call(a, b) if add is None else call(a, b, add)


def rmsnorm_fwd(name, x, g, width, out_dtype):
    r, cols = x.shape
    nb = cols // width
    tr = _rows(r, width)

    def body(x_ref, g_ref, o_ref):
        xv = x_ref[...]
        rs = lax.rsqrt(jnp.mean(xv * xv, axis=-1, keepdims=True) + EPS)
        o_ref[...] = (xv * rs * g_ref[...]).astype(o_ref.dtype)

    blk = pl.BlockSpec((tr, width), lambda i, j: (i, j))
    return pl.pallas_call(
        body, name=name, grid=(r // tr, nb),
        in_specs=[blk, pl.BlockSpec((1, width), lambda i, j: (0, 0))], out_specs=blk,
        out_shape=jax.ShapeDtypeStruct((r, cols), out_dtype),
        compiler_params=_params("parallel", "parallel"))(x, g.reshape(1, width))


def rmsnorm_bwd(name, dy, x, g, width, out_dtypes=(F32,), add=None):
    r, cols = x.shape
    nb = cols // width
    tr = _rows(r, width)
    n_out = len(out_dtypes)

    def body(*refs):
        dy_ref, x_ref, g_ref = refs[:3]
        add_ref = refs[3] if add is not None else None
        outs = refs[-(n_out + 1):-1]
        dg_ref = refs[-1]
        xv = x_ref[...]
        dyv = dy_ref[...].astype(F32)
        rs = lax.rsqrt(jnp.mean(xv * xv, axis=-1, keepdims=True) + EPS)
        xh = xv * rs
        dyg = dyv * g_ref[...]
        dx = rs * (dyg - xh * jnp.mean(dyg * xh, axis=-1, keepdims=True))
        if add_ref is not None:
            dx = dx + add_ref[...]
        for o in outs:
            o[...] = dx.astype(o.dtype)

        @pl.when((pl.program_id(0) == 0) & (pl.program_id(1) == 0))
        def _():
            dg_ref[...] = jnp.zeros_like(dg_ref)

        dg_ref[...] += jnp.sum(dyv * xh, axis=0, keepdims=True)

    blk = pl.BlockSpec((tr, width), lambda i, j: (i, j))
    vec = pl.BlockSpec((1, width), lambda i, j: (0, 0))
    ins = [dy, x, g.reshape(1, width)] + ([add] if add is not None else [])
    res = pl.pallas_call(
        body, name=name, grid=(r // tr, nb),
        in_specs=[blk, blk, vec] + ([blk] if add is not None else []),
        out_specs=[blk] * n_out + [vec],
        out_shape=[jax.ShapeDtypeStruct((r, cols), dt) for dt in out_dtypes] + [jax.ShapeDtypeStruct((1, width), F32)],
        compiler_params=_params("arbitrary", "arbitrary"))(*ins)
    return tuple(res[:n_out]) + (res[n_out].reshape(width),)


def _rope_angles(pos, dim, theta):
    inv = theta ** (-jnp.arange(0, dim, 2, dtype=F32) / dim)
    ang = pos.astype(F32)[:, None] * inv[None, :]
    return jnp.cos(ang), jnp.sin(ang)


def _rope_tables(s):
    rows = s // GRID_W
    row_pos = jnp.repeat(jnp.arange(rows), GRID_W)
    col_pos = jnp.tile(jnp.arange(GRID_W), rows)
    t_pos = jnp.arange(s)
    z = lambda n: jnp.zeros((s, n), F32)
    o = lambda n: jnp.ones((s, n), F32)
    cr, sr = _rope_angles(row_pos, HEAD // 2, A_THETA)
    cc, sc = _rope_angles(col_pos, HEAD // 2, A_THETA)
    tab_a = (jnp.concatenate([cr, cr, cc, cc], 1), jnp.concatenate([-sr, z(32), -sc, z(32)], 1),
             jnp.concatenate([z(32), sr, z(32), sc], 1), 32)
    cp, sp = _rope_angles(t_pos, B_ROPE_DIM, B_THETA)
    tab_b = (jnp.concatenate([cp, cp, o(96)], 1), jnp.concatenate([-sp, z(112)], 1),
             jnp.concatenate([z(16), sp, z(96)], 1), 16)
    cm, sm = _rope_angles(t_pos, C_ROPE, C_THETA)
    tab_c = (jnp.concatenate([cm, cm, o(64)], 1), jnp.concatenate([-sm, z(96)], 1),
             jnp.concatenate([z(32), sm, z(64)], 1), 32)

    def transposed(tab):
        c, s1, s2, h = tab
        return (c, jnp.roll(s2, -h, axis=1), jnp.roll(s1, h, axis=1), h)

    return {k: (t, transposed(t)) for k, t in (("a", tab_a), ("b", tab_b), ("c", tab_c))}


def rope(name, x, tab, out_dtype):
    c, s1, s2, h = tab
    s, cols = x.shape
    nh = cols // HEAD
    ts = _rows(s, HEAD)

    def body(x_ref, c_ref, s1_ref, s2_ref, o_ref):
        xv = x_ref[...].astype(F32)
        out = xv * c_ref[...] + pltpu.roll(xv, HEAD - h, 1) * s1_ref[...] + pltpu.roll(xv, h, 1) * s2_ref[...]
        o_ref[...] = out.astype(o_ref.dtype)

    blk = pl.BlockSpec((ts, HEAD), lambda i, j: (i, j))
    tb = pl.BlockSpec((ts, HEAD), lambda i, j: (i, 0))
    return pl.pallas_call(
        body, name=name, grid=(s // ts, nh), in_specs=[blk, tb, tb, tb], out_specs=blk,
        out_shape=jax.ShapeDtypeStruct((s, cols), out_dtype),
        compiler_params=_params("parallel", "parallel"))(x, c, s1, s2)


ATTN_TQ = 256


def attn_fwd(name, parts, v, v_group, nheads, scale):
    s = v.shape[0]
    tq = ATTN_TQ
    npart = len(parts)

    def body(*refs):
        v_ref, o_ref, lse_ref = refs[2 * npart:]
        sc = None
        for p in range(npart):
            t = _dot(refs[2 * p][...], refs[2 * p + 1][...], NT)
            sc = t if sc is None else sc + t
        sc = sc * scale
        m = jnp.max(sc, axis=-1, keepdims=True)
        e = jnp.exp(sc - m)
        l = jnp.sum(e, axis=-1, keepdims=True)
        o_ref[...] = _dot((e / l).astype(BF16), v_ref[...])
        lse_ref[...] = jnp.broadcast_to(m + jnp.log(l), (tq, HEAD))

    in_specs, ins = [], []
    for q, qoff, k, kg in parts:
        in_specs.append(pl.BlockSpec((tq, HEAD), lambda h, i, qoff=qoff: (i, qoff + h)))
        in_specs.append(pl.BlockSpec((s, HEAD), lambda h, i, kg=kg: (0, h // kg)))
        ins += [q, k]
    in_specs.append(pl.BlockSpec((s, HEAD), lambda h, i: (0, h // v_group)))
    out_blk = pl.BlockSpec((tq, HEAD), lambda h, i: (i, h))
    return pl.pallas_call(
        body, name=name, grid=(nheads, s // tq), in_specs=in_specs, out_specs=[out_blk, out_blk],
        out_shape=[jax.ShapeDtypeStruct((s, nheads * HEAD), F32)] * 2,
        compiler_params=_params("parallel", "parallel"))(*ins, v)


def attn_bwd(name, parts, v, v_group, lse, o, do, nheads, scale):
    s = v.shape[0]
    tq = ATTN_TQ
    npart = len(parts)

    def body(*refs):
        v_ref, lse_ref, o_ref, do_ref = refs[2 * npart:2 * npart + 4]
        outs = refs[2 * npart + 4:]
        dq_refs, dk_refs, dv_ref = outs[:npart], outs[npart:2 * npart], outs[2 * npart]
        h, i = pl.program_id(0), pl.program_id(1)
        sc = None
        for p in range(npart):
            t = _dot(refs[2 * p][...], refs[2 * p + 1][...], NT)
            sc = t if sc is None else sc + t
        pr = jnp.exp(sc * scale - lse_ref[:, 0:1])
        dov = do_ref[...]
        delta = jnp.sum(dov.astype(F32) * o_ref[...], axis=-1, keepdims=True)
        dp = _dot(dov, v_ref[...], NT)
        ds = (pr * (dp - delta) * scale).astype(BF16)

        @pl.when((i == 0) & (h % v_group == 0))
        def _():
            dv_ref[...] = jnp.zeros_like(dv_ref)

        dv_ref[...] += _dot(pr.astype(BF16), dov, TN)
        for p in range(npart):
            kg = parts[p][3]
            dq_refs[p][...] = _dot(ds, refs[2 * p + 1][...])

            @pl.when((i == 0) & (h % kg == 0))
            def _(p=p):
                dk_refs[p][...] = jnp.zeros_like(dk_refs[p])

            dk_refs[p][...] += _dot(ds, refs[2 * p][...], TN)

    in_specs, ins = [], []
    for q, qoff, k, kg in parts:
        in_specs.append(pl.BlockSpec((tq, HEAD), lambda h, i, qoff=qoff: (i, qoff + h)))
        in_specs.append(pl.BlockSpec((s, HEAD), lambda h, i, kg=kg: (0, h // kg)))
        ins += [q, k]
    hq_blk = pl.BlockSpec((tq, HEAD), lambda h, i: (i, h))
    in_specs += [pl.BlockSpec((s, HEAD), lambda h, i: (0, h // v_group)), hq_blk, hq_blk, hq_blk]
    out_specs = [hq_blk] * npart
    out_shape = [jax.ShapeDtypeStruct((s, nheads * HEAD), F32)] * npart
    for q, qoff, k, kg in parts:
        out_specs.append(pl.BlockSpec((s, HEAD), lambda h, i, kg=kg: (0, h // kg)))
        out_shape.append(jax.ShapeDtypeStruct((s, nheads // kg * HEAD), F32))
    out_specs.append(pl.BlockSpec((s, HEAD), lambda h, i: (0, h // v_group)))
    out_shape.append(jax.ShapeDtypeStruct((s, nheads // v_group * HEAD), F32))
    res = pl.pallas_call(
        body, name=name, grid=(nheads, s // tq), in_specs=in_specs, out_specs=out_specs, out_shape=out_shape,
        compiler_params=_params("arbitrary", "arbitrary"))(*ins, v, lse, o, do)
    return list(res[:npart]), list(res[npart:2 * npart]), res[2 * npart]


def _band_windows(lf):
    for ib in range(lf // HEAD):
        q0 = ib * HEAD
        yield q0, max(0, q0 - B_HALF), min(lf, q0 + HEAD + B_HALF)


def _band_mask(q0, lo, hi):
    qpos = q0 + lax.broadcasted_iota(jnp.int32, (HEAD, hi - lo), 0)
    kpos = lo + lax.broadcasted_iota(jnp.int32, (HEAD, hi - lo), 1)
    return jnp.abs(qpos - kpos) <= B_HALF


def _class_rows(r, d, start, size):
    return pl.ds(r + d * start, size, stride=d) if d > 1 else pl.ds(start, size)


def dilated_fwd(name, q, k, v_src, v_block):
    s = k.shape[0]
    scale = HEAD ** -0.5

    def body(q_ref, k_ref, v_ref, y_ref, lse_ref, o_scr, l_scr):
        g_now = pl.program_id(1)
        for g, d in enumerate(B_DILATIONS):
            @pl.when(g_now == g)
            def _(g=g, d=d):
                for r in range(d):
                    for q0, lo, hi in _band_windows(s // d):
                        mine = _class_rows(r, d, q0, HEAD)
                        keys = _class_rows(r, d, lo, hi - lo)
                        sc = _dot(q_ref[mine, :].astype(BF16), k_ref[keys, :].astype(BF16), NT) * scale
                        sc = jnp.where(_band_mask(q0, lo, hi), sc, -1e30)
                        m = jnp.max(sc, axis=-1, keepdims=True)
                        e = jnp.exp(sc - m)
                        l = jnp.sum(e, axis=-1, keepdims=True)
                        o_scr.at[g][mine, :] = _dot((e / l).astype(BF16), v_ref[keys, :].astype(BF16))
                        l_scr.at[g][mine, :] = jnp.broadcast_to(m + jnp.log(l), (HEAD, HEAD))

        @pl.when(g_now == B_GROUPS - 1)
        def _():
            a, b, c = l_scr[0], l_scr[1], l_scr[2]
            m = jnp.maximum(jnp.maximum(a, b), c)
            ea, eb, ec = jnp.exp(a - m), jnp.exp(b - m), jnp.exp(c - m)
            den = ea + eb + ec
            y_ref[...] = (ea / den) * o_scr[0] + (eb / den) * o_scr[1] + (ec / den) * o_scr[2]
            lse_ref[...] = m + jnp.log(den)

    blk = lambda f: pl.BlockSpec((s, HEAD), f)
    per_head = blk(lambda h, g: (0, h))
    return pl.pallas_call(
        body, name=name, grid=(B_HEADS, B_GROUPS),
        in_specs=[blk(lambda h, g: (0, B_HEADS * g + h)), per_head, blk(lambda h, g: (0, v_block + h))],
        out_specs=[per_head, per_head], out_shape=[jax.ShapeDtypeStruct((s, B_W), F32)] * 2,
        scratch_shapes=[pltpu.VMEM((B_GROUPS, s, HEAD), F32)] * 2,
        compiler_params=_params("parallel", "arbitrary"))(q, k, v_src)


def dilated_bwd(name, q, k, v_src, v_block, dy, y, lse):
    s = k.shape[0]
    scale = HEAD ** -0.5

    def body(q_ref, k_ref, v_ref, dy_ref, y_ref, lse_ref, dq_ref, dk_ref, dv_ref, delta):
        g_now = pl.program_id(1)

        @pl.when(g_now == 0)
        def _():
            dk_ref[...] = jnp.zeros_like(dk_ref)
            dv_ref[...] = jnp.zeros_like(dv_ref)
            delta[...] = jnp.broadcast_to(jnp.sum(dy_ref[...] * y_ref[...], axis=-1, keepdims=True), (s, HEAD))

        for g, d in enumerate(B_DILATIONS):
            @pl.when(g_now == g)
            def _(d=d):
                for r in range(d):
                    for q0, lo, hi in _band_windows(s // d):
                        mine = _class_rows(r, d, q0, HEAD)
                        keys = _class_rows(r, d, lo, hi - lo)
                        qv, kv = q_ref[mine, :].astype(BF16), k_ref[keys, :].astype(BF16)
                        dyv = dy_ref[mine, :].astype(BF16)
                        sc = _dot(qv, kv, NT) * scale
                        pr = jnp.where(_band_mask(q0, lo, hi), jnp.exp(sc - lse_ref[mine, :][:, 0:1]), 0.0)
                        dp = _dot(dyv, v_ref[keys, :].astype(BF16), NT)
                        ds = (pr * (dp - delta[mine, :][:, 0:1]) * scale).astype(BF16)
                        dv_ref[keys, :] = dv_ref[keys, :] + _dot(pr.astype(BF16), dyv, TN)
                        dq_ref[mine, :] = _dot(ds, kv)
                        dk_ref[keys, :] = dk_ref[keys, :] + _dot(ds, qv, TN)

    blk = lambda f: pl.BlockSpec((s, HEAD), f)
    per_head = blk(lambda h, g: (0, h))
    by_group = blk(lambda h, g: (0, B_HEADS * g + h))
    return pl.pallas_call(
        body, name=name, grid=(B_HEADS, B_GROUPS),
        in_specs=[by_group, per_head, blk(lambda h, g: (0, v_block + h)), per_head, per_head, per_head],
        out_specs=[by_group, per_head, per_head],
        out_shape=[jax.ShapeDtypeStruct((s, B_GROUPS * B_W), F32)] + [jax.ShapeDtypeStruct((s, B_W), F32)] * 2,
        scratch_shapes=[pltpu.VMEM((s, HEAD), F32)],
        compiler_params=_params("parallel", "arbitrary"))(q, k, v_src, dy, y, lse)


FFN_TN = 256


def _conv(h, cw):
    s = h.shape[0]
    row = lax.broadcasted_iota(jnp.int32, h.shape, 0)
    prev = jnp.where(row == 0, 0.0, pltpu.roll(h, 1, 0))
    nxt = jnp.where(row == s - 1, 0.0, pltpu.roll(h, s - 1, 0))
    return prev * cw[0:1, :] + h * cw[1:2, :] + nxt * cw[2:3, :] + cw[3:4, :], prev, nxt


def _sigmoid(x):
    return 1.0 / (1.0 + jnp.exp(-x))


def ffn_up(name, xn, w_up, cwb):
    s, dm = xn.shape
    quarter = w_up.shape[3]
    fh = 2 * quarter
    tn = FFN_TN
    per = quarter // tn

    def body(x_ref, w_ref, cw_ref, h_ref, act_ref):
        xv = x_ref[...]
        hg = _dot(xv, w_ref[0])
        hu = _dot(xv, w_ref[1])
        h_ref[0] = hg
        h_ref[1] = hu
        gc, _, _ = _conv(hg, cw_ref[0])
        uc, _, _ = _conv(hu, cw_ref[1])
        act_ref[...] = (gc * _sigmoid(gc) * uc).astype(BF16)

    return pl.pallas_call(
        body, name=name, grid=(fh // tn,),
        in_specs=[pl.BlockSpec((s, dm), lambda t: (0, 0)),
                  pl.BlockSpec((2, None, dm, tn), lambda t: (0, t // per, 0, t % per)),
                  pl.BlockSpec((2, 8, tn), lambda t: (0, 0, t))],
        out_specs=[pl.BlockSpec((2, s, tn), lambda t: (0, 0, t)), pl.BlockSpec((s, tn), lambda t: (0, t))],
        out_shape=[jax.ShapeDtypeStruct((2, s, fh), F32), jax.ShapeDtypeStruct((s, fh), BF16)],
        compiler_params=_params("parallel"))(xn, w_up, cwb)


def ffn_gate_bwd(name, h, dact, cwb):
    _, s, fh = h.shape
    tn = FFN_TN

    def body(h_ref, da_ref, cw_ref, dh_ref, dcw_ref):
        gc, gp, gn = _conv(h_ref[0], cw_ref[0])
        uc, up, un = _conv(h_ref[1], cw_ref[1])
        sg = _sigmoid(gc)
        da = da_ref[...]
        dgc = da * uc * (sg * (1.0 + gc * (1.0 - sg)))
        duc = da * (gc * sg)
        row = lax.broadcasted_iota(jnp.int32, da.shape, 0)
        for idx, (hv, prev, nxt, dc) in enumerate(((h_ref[0], gp, gn, dgc), (h_ref[1], up, un, duc))):
            cw = cw_ref[idx]
            from_next = jnp.where(row == s - 1, 0.0, pltpu.roll(dc, s - 1, 0))
            from_prev = jnp.where(row == 0, 0.0, pltpu.roll(dc, 1, 0))
            dh_ref[idx] = (from_next * cw[0:1, :] + dc * cw[1:2, :] + from_prev * cw[2:3, :]).astype(BF16)
            dcw_ref[idx, 0:1, :] = jnp.sum(prev * dc, axis=0, keepdims=True)
            dcw_ref[idx, 1:2, :] = jnp.sum(hv * dc, axis=0, keepdims=True)
            dcw_ref[idx, 2:3, :] = jnp.sum(nxt * dc, axis=0, keepdims=True)
            dcw_ref[idx, 3:4, :] = jnp.sum(dc, axis=0, keepdims=True)
            dcw_ref[idx, 4:8, :] = jnp.zeros((4, tn), F32)

    return pl.pallas_call(
        body, name=name, grid=(fh // tn,),
        in_specs=[pl.BlockSpec((2, s, tn), lambda t: (0, 0, t)), pl.BlockSpec((s, tn), lambda t: (0, t)),
                  pl.BlockSpec((2, 8, tn), lambda t: (0, 0, t))],
        out_specs=[pl.BlockSpec((2, s, tn), lambda t: (0, 0, t)), pl.BlockSpec((2, 8, tn), lambda t: (0, 0, t))],
        out_shape=[jax.ShapeDtypeStruct((2, s, fh), BF16), jax.ShapeDtypeStruct((2, 8, fh), F32)],
        compiler_params=_params("parallel"))(h, dact, cwb)


def ffn_dx(name, dh, w_up):
    _, s, fh = dh.shape
    dm, quarter = w_up.shape[2], w_up.shape[3]
    tk = _tile(quarter, 2816)
    per = quarter // tk
    tm, tn = _tile(s, 1024), _tile(dm, 1024)
    grid = (s // tm, dm // tn, 4 * per)
    a_spec = pl.BlockSpec((None, tm, tk), lambda i, j, k: (k // (2 * per), i, k % (2 * per)))
    b_spec = pl.BlockSpec((None, None, tn, tk), lambda i, j, k: (k // (2 * per), (k // per) % 2, j, k % per))
    o_spec = pl.BlockSpec((tm, tn), lambda i, j, k: (i, j))
    return _matmul_call(name, NT, grid, a_spec, b_spec, o_spec, jax.ShapeDtypeStruct((s, dm), F32), (tm, tn))(dh, w_up)


def ffn_dw_up(name, xn, dh):
    _, s, fh = dh.shape
    dm = xn.shape[1]
    quarter = fh // 2
    tn = _tile(quarter, 1536)
    per = quarter // tn
    tm, tk = _tile(dm, 1024), _tile(s, 2048)
    grid = (dm // tm, 4 * per, s // tk)
    a_spec = pl.BlockSpec((tk, tm), lambda i, j, k: (k, i))
    b_spec = pl.BlockSpec((None, tk, tn), lambda i, j, k: (j // (2 * per), k, j % (2 * per)))
    o_spec = pl.BlockSpec((None, None, tm, tn), lambda i, j, k: (j // (2 * per), (j // per) % 2, i, j % per))
    return _matmul_call(name, TN, grid, a_spec, b_spec, o_spec,
                        jax.ShapeDtypeStruct((2, 2, dm, quarter), F32), (tm, tn))(xn, dh)


def loss_head(name, y, target):
    s, dm = y.shape
    ts = _rows(s, dm)

    def body(y_ref, t_ref, loss_ref, dy_ref, acc):
        i = pl.program_id(0)

        @pl.when(i == 0)
        def _():
            acc[...] = jnp.zeros_like(acc)

        err = y_ref[...] - t_ref[...]
        dy_ref[...] = err * (1.0 / dm)
        acc[...] += jnp.sum(err * err, axis=0, keepdims=True)

        @pl.when(i == s // ts - 1)
        def _():
            loss_ref[...] = jnp.broadcast_to(jnp.sum(acc[...], axis=-1, keepdims=True) * (0.5 / dm), (1, LANES))

    blk = pl.BlockSpec((ts, dm), lambda i: (i, 0))
    loss, dy = pl.pallas_call(
        body, name=name, grid=(s // ts,), in_specs=[blk, blk],
        out_specs=[pl.BlockSpec((1, LANES), lambda i: (0, 0)), blk],
        out_shape=[jax.ShapeDtypeStruct((1, LANES), F32), jax.ShapeDtypeStruct((s, dm), F32)],
        scratch_shapes=[pltpu.VMEM((1, dm), F32)], compiler_params=_params("arbitrary"))(y, target)
    return loss[0, 0], dy


def attn_fwd_c(name, qc, kvc, krc, scale):
    s = qc.shape[0]
    tq = ATTN_TQ

    def body(qn_ref, qr_ref, kn_ref, kr_ref, v_ref, o_ref, lse_ref):
        sc = (_dot(qn_ref[...], kn_ref[...], NT) + _dot(qr_ref[...], kr_ref[...], NT)) * scale
        m = jnp.max(sc, axis=-1, keepdims=True)
        e = jnp.exp(sc - m)
        l = jnp.sum(e, axis=-1, keepdims=True)
        o_ref[...] = _dot((e / l).astype(BF16), v_ref[...])
        lse_ref[...] = jnp.broadcast_to(m + jnp.log(l), (tq, HEAD))

    qb = lambda off: pl.BlockSpec((tq, HEAD), lambda h, i: (i, off + h))
    kb = lambda f: pl.BlockSpec((s, HEAD), f)
    out_blk = pl.BlockSpec((tq, HEAD), lambda h, i: (i, h))
    return pl.pallas_call(
        body, name=name, grid=(C_HEADS, s // tq),
        in_specs=[qb(0), qb(C_HEADS), kb(lambda h, i: (0, 2 * h)), kb(lambda h, i: (0, 0)), kb(lambda h, i: (0, 2 * h + 1))],
        out_specs=[out_blk, out_blk], out_shape=[jax.ShapeDtypeStruct((s, C_W), F32)] * 2,
        compiler_params=_params("parallel", "parallel"))(qc, qc, kvc, krc, kvc)


def attn_bwd_c(name, qc, kvc, krc, lse, o, do, scale):
    s = qc.shape[0]
    tq = ATTN_TQ

    def body(qn_ref, qr_ref, kn_ref, kr_ref, v_ref, lse_ref, o_ref, do_ref, dqn_ref, dqr_ref, dkv_ref, dkr_ref):
        h, i = pl.program_id(0), pl.program_id(1)
        sc = (_dot(qn_ref[...], kn_ref[...], NT) + _dot(qr_ref[...], kr_ref[...], NT)) * scale
        pr = jnp.exp(sc - lse_ref[:, 0:1])
        dov = do_ref[...]
        delta = jnp.sum(dov.astype(F32) * o_ref[...], axis=-1, keepdims=True)
        dp = _dot(dov, v_ref[...], NT)
        ds = (pr * (dp - delta) * scale).astype(BF16)

        @pl.when(i == 0)
        def _():
            dkv_ref[...] = jnp.zeros_like(dkv_ref)

        @pl.when((i == 0) & (h == 0))
        def _():
            dkr_ref[...] = jnp.zeros_like(dkr_ref)

        dkv_ref[:, HEAD:] += _dot(pr.astype(BF16), dov, TN)
        dqn_ref[...] = _dot(ds, kn_ref[...])
        dqr_ref[...] = _dot(ds, kr_ref[...])
        dkv_ref[:, :HEAD] += _dot(ds, qn_ref[...], TN)
        dkr_ref[...] += _dot(ds, qr_ref[...], TN)

    qb = lambda off: pl.BlockSpec((tq, HEAD), lambda h, i: (i, off + h))
    kb = lambda f: pl.BlockSpec((s, HEAD), f)
    hq = pl.BlockSpec((tq, HEAD), lambda h, i: (i, h))
    kn_map, v_map, kr_map = (lambda h, i: (0, 2 * h)), (lambda h, i: (0, 2 * h + 1)), (lambda h, i: (0, 0))
    return pl.pallas_call(
        body, name=name, grid=(C_HEADS, s // tq),
        in_specs=[qb(0), qb(C_HEADS), kb(kn_map), kb(kr_map), kb(v_map), hq, hq, hq],
        out_specs=[hq, hq, pl.BlockSpec((s, 2 * HEAD), lambda h, i: (0, h)), kb(kr_map)],
        out_shape=[jax.ShapeDtypeStruct((s, C_W), F32)] * 2
        + [jax.ShapeDtypeStruct((s, 2 * C_W), F32), jax.ShapeDtypeStruct((s, HEAD), F32)],
        compiler_params=_params("arbitrary", "arbitrary"))(qc, qc, kvc, krc, kvc, lse, o, do)


def _layer_fwd(x, p, tabs):
    x1, sv = _mixers_fwd(x, p, tabs)
    return _ffn_fwd(x1, p, sv), sv


def _mixers_fwd(x, p, tabs):
    sv = {"x": x}
    hn = rmsnorm_fwd("attn_norm", x, p["attn_norm"], x.shape[1], BF16)
    proj = matmul("in_proj", hn, p["w_in_t"], "nt", F32, tm=512, tk=512)
    sv["hn"] = hn
    aq, ak = proj[:, O_AQ:O_AK], proj[:, O_AK:O_AV]
    av = proj[:, O_AV:O_BQ].astype(BF16)
    bq, bk = proj[:, O_BQ:O_BK], proj[:, O_BK:O_BV]
    cq, ckv = proj[:, O_CQ:O_CKV], proj[:, O_CKV:O_CKR]
    ckr = jnp.pad(proj[:, O_CKR:], ((0, 0), (0, HEAD - C_ROPE)))
    sv.update(aq=aq, ak=ak, av=av, cq=cq, ckv=ckv)

    qa = rope("a_q_rope", rmsnorm_fwd("a_q_norm", aq, p["a_q_norm"], HEAD, F32), tabs["a"][0], BF16)
    ka = rope("a_k_rope", rmsnorm_fwd("a_k_norm", ak, p["a_k_norm"], HEAD, F32), tabs["a"][0], BF16)
    ya, lse_a = attn_fwd("a_attn", [(qa, 0, ka, A_HEADS // A_KV)], av, A_HEADS // A_KV, A_HEADS, HEAD ** -0.5)
    sv.update(qa=qa, ka=ka, ya=ya, lse_a=lse_a)

    qb = rope("b_q_rope", bq, tabs["b"][0], F32)
    kb = rope("b_k_rope", bk, tabs["b"][0], F32)
    yb, lse_b = dilated_fwd("b_dilated", qb, kb, proj, O_BV // HEAD)
    sv.update(qb=qb, kb=kb, yb=yb, lse_b=lse_b, proj=proj)

    cqn = rmsnorm_fwd("c_q_norm", cq, p["c_q_norm"], C_RANK, BF16)
    qc_raw = matmul("c_uq", cqn, p["w_uq_t"], "nt", F32)
    qc = jnp.concatenate([qc_raw[:, :C_W].astype(BF16), rope("c_q_rope", qc_raw[:, C_W:], tabs["c"][0], BF16)], axis=1)
    ckvn = rmsnorm_fwd("c_kv_norm", ckv, p["c_kv_norm"], C_RANK, BF16)
    kvc = matmul("c_ukv", ckvn, p["w_ukv"], "nn", BF16)
    krc = rope("c_k_rope", ckr, tabs["c"][0], BF16)
    c_scale = (HEAD + C_ROPE) ** -0.5
    yc, lse_c = attn_fwd_c("c_attn", qc, kvc, krc, c_scale)
    sv.update(cqn=cqn, ckvn=ckvn, qc=qc, kvc=kvc, krc=krc, yc=yc, lse_c=lse_c)

    g_a, g_b, g_c = p["out_norm"][:A_W], p["out_norm"][A_W:A_W + B_W], p["out_norm"][A_W + B_W:]
    y = jnp.concatenate([rmsnorm_fwd("out_norm_a", ya, g_a, A_W, BF16), rmsnorm_fwd("out_norm_b", yb, g_b, B_W, BF16),
                         rmsnorm_fwd("out_norm_c", yc, g_c, C_W, BF16)], axis=1)
    x1 = matmul("out_proj", y, p["w_out"], "nn", F32, add=x)
    sv.update(y=y, x1=x1)
    return x1, sv


def _ffn_fwd(x1, p, sv):
    xn = rmsnorm_fwd("ffn_norm", x1, p["ffn_norm"], x1.shape[1], BF16)
    h, act = ffn_up("ffn_up", xn, p["w_up"], p["cwb"])
    x2 = matmul("ffn_down", act, p["w_down"], "nn", F32, add=x1)
    sv.update(xn=xn, h=h, act=act)
    return x2


def _layer_bwd(dx2, sv, p, tabs):
    dxn, g = _ffn_bwd(dx2, sv, p)
    dx1, dy, g_out = _out_proj_bwd(dxn, dx2, sv, p)
    dhn, g_mix = _mixers_bwd(dy, sv, p, tabs)
    dx, g_norm = _attn_norm_bwd(dhn, dx1, sv, p)
    return dx, {**g, **g_out, **g_mix, **g_norm}


def _ffn_bwd(dx2, sv, p, after=0.0):
    g = {}
    dx2b = (dx2 + after).astype(BF16)
    dact = matmul("ffn_down_dx", dx2b, p["w_down"], "nt", F32)
    g["w_down"] = matmul("ffn_down_dw", sv["act"], dx2b, "tn", F32)
    dh, g["cwb"] = ffn_gate_bwd("ffn_gate_bwd", sv["h"], dact, p["cwb"])
    dxn = ffn_dx("ffn_up_dx", dh, p["w_up"])
    g["w_up"] = ffn_dw_up("ffn_up_dw", sv["xn"], dh)
    return dxn, g


def _out_proj_bwd(dxn, dx2, sv, p, after=0.0):
    g = {}
    dm = dx2.shape[1]
    dx1, dx1b, g["ffn_norm"] = rmsnorm_bwd("ffn_norm_bwd", dxn, sv["x1"], p["ffn_norm"] + after, dm, (F32, BF16), add=dx2)
    dy = matmul("out_proj_dx", dx1b, p["w_out"], "nt", F32)
    g["w_out"] = matmul("out_proj_dw", sv["y"], dx1b, "tn", F32)
    return dx1, dy, g


def _attn_norm_bwd(dhn, dx1, sv, p, after=0.0):
    dx, d_gain = rmsnorm_bwd("attn_norm_bwd", dhn, sv["x"], p["attn_norm"] + after, dhn.shape[1], (F32,), add=dx1)
    return dx, {"attn_norm": d_gain}


def _mixers_bwd(dy, sv, p, tabs, after=0.0):
    g = {}
    out_norm = p["out_norm"] + after
    g_a, g_b, g_c = out_norm[:A_W], out_norm[A_W:A_W + B_W], out_norm[A_W + B_W:]
    dya, dga = rmsnorm_bwd("out_norm_a_bwd", dy[:, :A_W], sv["ya"], g_a, A_W, (BF16,))
    dyb, dgb = rmsnorm_bwd("out_norm_b_bwd", dy[:, A_W:A_W + B_W], sv["yb"], g_b, B_W, (F32,))
    dyc, dgc = rmsnorm_bwd("out_norm_c_bwd", dy[:, A_W + B_W:], sv["yc"], g_c, C_W, (BF16,))
    g["out_norm"] = jnp.concatenate([dga, dgb, dgc])

    c_scale = (HEAD + C_ROPE) ** -0.5
    dqn, dqr, dkv, dkr = attn_bwd_c("c_attn_bwd", sv["qc"], sv["kvc"], sv["krc"], sv["lse_c"], sv["yc"], dyc, c_scale)
    dqc = jnp.concatenate([dqn.astype(BF16), rope("c_q_rope_bwd", dqr, tabs["c"][1], BF16)], axis=1)
    dcqn = matmul("c_uq_dx", dqc, p["w_uq_t"], "nn", F32)
    g["w_uq"] = matmul("c_uq_dw", dqc, sv["cqn"], "tn", F32)
    dcq, g["c_q_norm"] = rmsnorm_bwd("c_q_norm_bwd", dcqn, sv["cq"], p["c_q_norm"], C_RANK, (BF16,))
    dkvc = dkv.astype(BF16)
    dckvn = matmul("c_ukv_dx", dkvc, p["w_ukv"], "nt", F32)
    g["w_ukv"] = matmul("c_ukv_dw", sv["ckvn"], dkvc, "tn", F32)
    dckv, g["c_kv_norm"] = rmsnorm_bwd("c_kv_norm_bwd", dckvn, sv["ckv"], p["c_kv_norm"], C_RANK, (BF16,))
    dckr = rope("c_k_rope_bwd", dkr, tabs["c"][1], BF16)

    dqb, dkb, dvb = dilated_bwd("b_dilated_bwd", sv["qb"], sv["kb"], sv["proj"], O_BV // HEAD, dyb, sv["yb"], sv["lse_b"])
    dbq = rope("b_q_rope_bwd", dqb, tabs["b"][1], BF16)
    dbk = rope("b_k_rope_bwd", dkb, tabs["b"][1], BF16)
    dbv = dvb.astype(BF16)

    kg = A_HEADS // A_KV
    (dqa,), (dka,), dva = attn_bwd("a_attn_bwd", [(sv["qa"], 0, sv["ka"], kg)], sv["av"], kg, sv["lse_a"], sv["ya"],
                                   dya, A_HEADS, HEAD ** -0.5)
    daq, g["a_q_norm"] = rmsnorm_bwd("a_q_norm_bwd", rope("a_q_rope_bwd", dqa, tabs["a"][1], F32), sv["aq"],
                                     p["a_q_norm"], HEAD, (BF16,))
    dak, g["a_k_norm"] = rmsnorm_bwd("a_k_norm_bwd", rope("a_k_rope_bwd", dka, tabs["a"][1], F32), sv["ak"],
                                     p["a_k_norm"], HEAD, (BF16,))

    dproj = jnp.concatenate([daq, dak, dva.astype(BF16), dbq, dbk, dbv, dcq, dckv, dckr[:, :C_ROPE]], axis=1)
    dhn = matmul("in_proj_dx", dproj, p["w_in_t"], "nn", F32, tm=512)
    g["w_in"] = matmul("in_proj_dw", dproj, sv["hn"], "tn", F32, tn=512, tk=512)
    return dhn, g


def _local_step(x, target, layers, final_norm, tabs):
    saved = []
    for p in layers:
        x, sv = _layer_fwd(x, p, tabs)
        saved.append(sv)
    dm = x.shape[1]
    yf = rmsnorm_fwd("final_norm", x, final_norm, dm, F32)
    loss, dyf = loss_head("loss_head", yf, target)
    dx, d_final = rmsnorm_bwd("final_norm_bwd", dyf, x, final_norm, dm, (F32,))
    grads = [None] * len(layers)
    for l in reversed(range(len(layers))):
        dx, grads[l] = _layer_bwd(dx, saved[l], layers[l], tabs)
    return loss, dx, grads, d_final


HBM = pl.BlockSpec(memory_space=pl.ANY)


def _place():
    x, y, c = lax.axis_index("x"), lax.axis_index("y"), lax.axis_index("c")
    others = [(1 - x, y), (x, 1 - y), (1 - x, 1 - y)]
    return x, y, c, 2 * x + y, others


def _remote(src, dst, send_sem, recv_sem, device):
    return pltpu.make_async_remote_copy(src_ref=src, dst_ref=dst, send_sem=send_sem, recv_sem=recv_sem,
                                        device_id=device, device_id_type=MESH)


def gather_small(name, small):
    def body(in_ref, out_ref, send_sems, recv_sems, local_sem):
        x, y, c, chip, others = _place()
        mine = pltpu.make_async_copy(in_ref, out_ref.at[chip], local_sem)
        mine.start()
        copies = []
        for k, (ox, oy) in enumerate(others):
            cp = _remote(in_ref, out_ref.at[chip], send_sems.at[k], recv_sems.at[k], (ox, oy, c))
            cp.start()
            copies.append(cp)
        for k, (ox, oy) in enumerate(others):
            landed = out_ref.at[2 * ox + oy]
            _remote(landed, landed, send_sems.at[k], recv_sems.at[k], (ox, oy, c)).wait_recv()
        for cp in copies:
            cp.wait_send()
        mine.wait()

    return pl.pallas_call(
        body, name=name, in_specs=[HBM], out_specs=HBM, out_shape=jax.ShapeDtypeStruct((4,) + small.shape, small.dtype),
        scratch_shapes=[pltpu.SemaphoreType.DMA((3,)), pltpu.SemaphoreType.DMA((3,)), pltpu.SemaphoreType.DMA],
        compiler_params=pltpu.CompilerParams(has_side_effects=True))(small)


IN_HBM = pl.BlockSpec(memory_space=pltpu.HBM)
SEMS = pl.BlockSpec(memory_space=pltpu.SEMAPHORE)
DATAFLOW = pltpu.SideEffectType.DATAFLOW_SIDE_EFFECTING


BF16_ROWS_PER_TILE = 16


def _halves_by_rows(shape):
    return (shape[-2] // 2) % BF16_ROWS_PER_TILE == 0


def _half(shape, core):
    if _halves_by_rows(shape):
        size = shape[-2] // 2
        return (pl.ds(core * size, size), slice(None))
    size = shape[-1] // 2
    return (slice(None), pl.ds(core * size, size))


def _half_shape(shape):
    r, c = shape[-2:]
    return (r // 2, c) if _halves_by_rows(shape) else (r, c // 2)


def _gather_plan(bufs):
    x, y, c, chip, others = _place()
    plan = []
    for t, ref in enumerate(bufs):
        mine = _half(ref.shape, c)
        for k, (ox, oy) in enumerate(others):
            plan.append((3 * t + k, (ox, oy, c), ref.at[(chip,) + mine], ref.at[(chip,) + mine],
                         ref.at[(2 * ox + oy,) + mine]))
    return plan


def _exchange_plan(bufs):
    x, y, c, chip, others = _place()
    n = len(bufs) // 2
    plan = []
    for t in range(n):
        for k, (ox, oy) in enumerate(others):
            plan.append((3 * t + k, (ox, oy, c), bufs[t].at[2 * ox + oy], bufs[n + t].at[chip], bufs[n + t].at[2 * ox + oy]))
    return plan


def _sibling_plan(bufs):
    x, y, c, chip, others = _place()
    n = len(bufs) // 2
    plan = []
    for t in range(n):
        plan.append((t, (x, y, 1 - c), bufs[t].at[(slice(None),) + _half(bufs[t].shape, 1 - c)], bufs[n + t], bufs[n + t]))
    return plan


PLAN_COPIES = {_gather_plan: lambda n: 3 * n, _exchange_plan: lambda n: 3 * (n // 2), _sibling_plan: lambda n: n // 2}


def split_start(name, bufs, plan_of, after):
    n = len(bufs)

    def body(*refs):
        ins = refs[:n]
        send_sems, recv_sems = refs[n + 1], refs[n + 2]
        token = refs[-1]
        for idx, peer, src, dst, _ in plan_of(ins):
            _remote(src, dst, send_sems.at[idx], recv_sems.at[idx], peer).start()
        token[...] = jnp.zeros_like(token)

    copies = PLAN_COPIES[plan_of](n)
    res = pl.pallas_call(
        body, name=name, in_specs=[IN_HBM] * n + [HBM],
        out_specs=(SEMS, SEMS) + (IN_HBM,) * n + (pl.BlockSpec(memory_space=pltpu.VMEM),),
        out_shape=(pltpu.SemaphoreType.DMA((copies,)), pltpu.SemaphoreType.DMA((copies,)))
        + tuple(pltpu.HBM(b.shape, b.dtype) for b in bufs) + (jax.ShapeDtypeStruct((8, LANES), F32),),
        input_output_aliases={t: 2 + t for t in range(n)},
        compiler_params=pltpu.CompilerParams(has_side_effects=DATAFLOW))(
            *[pltpu.with_memory_space_constraint(b, pltpu.HBM) for b in bufs], after)
    return res[0], res[1], list(res[2:2 + n]), res[-1]


def split_wait(name, send_sems, recv_sems, flying, plan_of, after):
    n = len(flying)

    def body(*refs):
        ins = refs[:n]
        send_ref, recv_ref = refs[n], refs[n + 1]
        for idx, peer, src, dst, landing in plan_of(ins):
            _remote(src, dst, send_ref.at[idx], recv_ref.at[idx], peer).wait_send()
            _remote(landing, landing, send_ref.at[idx], recv_ref.at[idx], peer).wait_recv()

    return list(pl.pallas_call(
        body, name=name, in_specs=[IN_HBM] * n + [SEMS, SEMS] + [HBM] * len(after), out_specs=(IN_HBM,) * n,
        out_shape=tuple(pltpu.HBM(b.shape, b.dtype) for b in flying),
        input_output_aliases={t: t for t in range(n)},
        compiler_params=pltpu.CompilerParams(has_side_effects=DATAFLOW))(*flying, send_sems, recv_sems, *after))


def forward_halves(name, slots):
    n = len(slots)

    def body(*refs):
        ins, outs = refs[:n], refs[n:2 * n]
        send_sems, recv_sems = refs[2 * n:]
        x, y, c, chip, others = _place()
        sibling = (x, y, 1 - c)
        copies = []
        for t in range(n):
            mine = _half(slots[t].shape, c)
            for k, (ox, oy) in enumerate(others):
                cp = _remote(ins[t].at[(2 * ox + oy,) + mine], outs[t].at[(2 * ox + oy,) + mine], send_sems.at[t, k],
                             recv_sems.at[t, k], sibling)
                cp.start()
                copies.append(cp)
        for t in range(n):
            for k, (ox, oy) in enumerate(others):
                theirs = outs[t].at[(2 * ox + oy,) + _half(slots[t].shape, 1 - c)]
                _remote(theirs, theirs, send_sems.at[t, k], recv_sems.at[t, k], sibling).wait_recv()
        for cp in copies:
            cp.wait_send()

    return list(pl.pallas_call(
        body, name=name, in_specs=[HBM] * n, out_specs=[HBM] * n,
        out_shape=[jax.ShapeDtypeStruct(s.shape, s.dtype) for s in slots],
        input_output_aliases={t: t for t in range(n)},
        scratch_shapes=[pltpu.SemaphoreType.DMA((n, 3))] * 2,
        compiler_params=pltpu.CompilerParams(has_side_effects=True))(*slots))


def _half_block(shape, tr):
    r, c = shape[-2:]
    if _halves_by_rows(shape):
        per = r // 2 // tr
        return tr, c, lambda i, core: (core * per + i, 0)
    return tr, c // 2, lambda i, core: (i, core)


def add_half(name, place, grad, theirs):
    hr, hc = _half_shape(grad.shape)
    tr = _rows(hr, hc)
    br, bc, at = _half_block(grad.shape, tr)

    def body(place_ref, g_ref, t_ref, o_ref):
        o_ref[...] = (g_ref[...] + t_ref[...]).astype(o_ref.dtype)

    whole = pl.BlockSpec((None, br, bc), lambda j, i, pr: (j, i, 0))
    return pl.pallas_call(
        body, name=name,
        grid_spec=pltpu.PrefetchScalarGridSpec(
            num_scalar_prefetch=1, grid=(4, hr // tr),
            in_specs=[pl.BlockSpec((None, br, bc), lambda j, i, pr: (j,) + at(i, pr[0])), whole],
            out_specs=whole),
        out_shape=jax.ShapeDtypeStruct((4, hr, hc), BF16),
        compiler_params=_params("parallel", "parallel"))(place, grad, theirs)


def sum_chips(name, place, own, parts, shard_shape):
    _, hr, hc = parts.shape
    tr = _rows(hr, hc)
    br, bc, at = _half_block(shard_shape, tr)

    def body(place_ref, own_ref, p1, p2, p3, o_ref):
        o_ref[...] = own_ref[...].astype(F32) + p1[...].astype(F32) + p2[...].astype(F32) + p3[...].astype(F32)

    def slot(k):
        return pl.BlockSpec((None, br, bc), lambda i, pr: (lax.rem(pr[1] + k, 4), i, 0))

    return pl.pallas_call(
        body, name=name,
        grid_spec=pltpu.PrefetchScalarGridSpec(
            num_scalar_prefetch=1, grid=(hr // tr,), in_specs=[slot(0), slot(1), slot(2), slot(3)],
            out_specs=pl.BlockSpec((br, bc), lambda i, pr: at(i, pr[0]))),
        out_shape=jax.ShapeDtypeStruct(tuple(shard_shape), F32),
        compiler_params=_params("parallel"))(place, own, parts, parts, parts)


def join_halves(name, halves):
    n = len(halves)

    def body(*refs):
        ins, outs = refs[:n], refs[n:2 * n]
        send_sems, recv_sems = refs[2 * n:]
        x, y, c, _, _ = _place()
        copies = []
        for t in range(n):
            mine = _half(halves[t].shape, c)
            cp = _remote(ins[t].at[mine], outs[t].at[mine], send_sems.at[t], recv_sems.at[t], (x, y, 1 - c))
            cp.start()
            copies.append(cp)
        for t in range(n):
            theirs = outs[t].at[_half(halves[t].shape, 1 - c)]
            _remote(theirs, theirs, send_sems.at[t], recv_sems.at[t], (x, y, 1 - c)).wait_recv()
        for cp in copies:
            cp.wait_send()

    return pl.pallas_call(
        body, name=name, in_specs=[HBM] * n, out_specs=[HBM] * n,
        out_shape=[jax.ShapeDtypeStruct(h.shape, h.dtype) for h in halves],
        input_output_aliases={t: t for t in range(n)},
        scratch_shapes=[pltpu.SemaphoreType.DMA((n,))] * 2,
        compiler_params=pltpu.CompilerParams(has_side_effects=True))(*halves)


def allreduce_small(name, buf):
    rows = buf.shape[0]

    def body(in_ref, out_ref, slots, send_sems, recv_sems):
        x, y, c, _, _ = _place()
        me = 4 * x + 2 * y + c
        slots[me] = in_ref[...]
        peers = []
        for k in range(1, 8):
            px = 1 - x if k & 4 else x
            py = 1 - y if k & 2 else y
            pc = 1 - c if k & 1 else c
            peers.append((px, py, pc))
        copies = []
        for k, peer in enumerate(peers):
            cp = _remote(in_ref, slots.at[me], send_sems.at[k], recv_sems.at[k], peer)
            cp.start()
            copies.append(cp)
        for k, (px, py, pc) in enumerate(peers):
            slot = slots.at[4 * px + 2 * py + pc]
            _remote(slot, slot, send_sems.at[k], recv_sems.at[k], (px, py, pc)).wait_recv()
        for cp in copies:
            cp.wait_send()
        acc = slots[0]
        for d in range(1, 8):
            acc = acc + slots[d]
        out_ref[...] = acc

    vm = pl.BlockSpec(memory_space=pltpu.VMEM)
    return pl.pallas_call(
        body, name=name, in_specs=[vm], out_specs=vm, out_shape=jax.ShapeDtypeStruct(buf.shape, F32),
        scratch_shapes=[pltpu.VMEM((8, rows, LANES), F32), pltpu.SemaphoreType.DMA((7,)), pltpu.SemaphoreType.DMA((7,))],
        compiler_params=pltpu.CompilerParams(has_side_effects=True, vmem_limit_bytes=VMEM_LIMIT))(buf)


def cast_layer(name, place, w, layer):
    _, r, cols = w.shape
    tr = _rows(r, cols)

    def body(place_ref, w_ref, o_ref):
        o_ref[...] = w_ref[...].astype(BF16)

    return pl.pallas_call(
        body, name=name,
        grid_spec=pltpu.PrefetchScalarGridSpec(
            num_scalar_prefetch=1, grid=(r // tr,),
            in_specs=[pl.BlockSpec((None, tr, cols), lambda i, pr: (layer, i, 0))],
            out_specs=pl.BlockSpec((None, tr, cols), lambda i, pr: (pr[1], i, 0))),
        out_shape=jax.ShapeDtypeStruct((4, r, cols), BF16),
        compiler_params=_params("parallel"))(place, w)


def _adamw_math(g, w, m, v):
    m = ADAM_B1 * m + (1.0 - ADAM_B1) * g
    v = ADAM_B2 * v + (1.0 - ADAM_B2) * (g * g)
    m_hat = m / (1.0 - ADAM_B1 ** ADAM_STEP)
    v_hat = v / (1.0 - ADAM_B2 ** ADAM_STEP)
    delta = -ADAM_LR * (m_hat / (jnp.sqrt(v_hat) + ADAM_EPS) + ADAM_WD * w)
    return delta, m, v


def adamw_layer(name, layer, g, w, m, v, prev=None):
    nl, r, cols = w.shape
    tr = _rows(r, cols, 8)

    def body(*refs):
        g_ref, w_ref, m_ref, v_ref = refs[:4]
        og, od, om, ov = refs[-4:]
        gv = g_ref[...]
        delta, m2, v2 = _adamw_math(gv, w_ref[...], m_ref[...], v_ref[...])
        og[...] = gv
        od[...] = delta
        om[...] = m2
        ov[...] = v2

    lay = pl.BlockSpec((None, tr, cols), lambda i: (layer, i, 0))
    ins = [g, w, m, v] + (list(prev) if prev is not None else [])
    return pl.pallas_call(
        body, name=name, grid=(r // tr,),
        in_specs=[pl.BlockSpec((tr, cols), lambda i: (i, 0)), lay, lay, lay] + ([HBM] * 4 if prev is not None else []),
        out_specs=[lay] * 4, out_shape=[jax.ShapeDtypeStruct((nl, r, cols), F32)] * 4,
        input_output_aliases={4 + k: k for k in range(4)} if prev is not None else {},
        compiler_params=_params("parallel"))(*ins)


def adamw_packed(name, g, w, m, v):
    def body(g_ref, w_ref, m_ref, v_ref, od, om, ov):
        delta, m2, v2 = _adamw_math(g_ref[...], w_ref[...], m_ref[...], v_ref[...])
        od[...] = delta
        om[...] = m2
        ov[...] = v2

    vm = pl.BlockSpec(memory_space=pltpu.VMEM)
    return pl.pallas_call(
        body, name=name, in_specs=[vm] * 4, out_specs=[vm] * 3, out_shape=[jax.ShapeDtypeStruct(g.shape, F32)] * 3,
        compiler_params=pltpu.CompilerParams(vmem_limit_bytes=VMEM_LIMIT))(g, w, m, v)


def _pack(arrays):
    flat = jnp.concatenate([a.reshape(-1) for a in arrays])
    pad = (-flat.shape[0]) % (8 * LANES)
    return jnp.pad(flat, (0, pad)).reshape(-1, LANES)


def _unpack(buf, shapes):
    flat = buf.reshape(-1)
    out, off = [], 0
    for shp in shapes:
        size = int(np.prod(shp))
        out.append(flat[off:off + size].reshape(shp))
        off += size
    return out


MIXER_W = ("w_in", "w_uq", "w_ukv", "w_out")
FFN_W = ("w_up", "w_down")
BIG = MIXER_W + FFN_W
TRANSPOSED = ("w_in", "w_uq")
COLUMN_CUT = ("w_ukv", "w_up")
SMALL = ("attn_norm", "a_q_norm", "a_k_norm", "c_q_norm", "c_kv_norm", "out_norm", "ffn_norm", "conv_b", "final_norm")
WEIGHTS = ("attn_norm", "w_in", "a_q_norm", "a_k_norm", "c_q_norm", "c_kv_norm", "w_uq", "w_ukv", "out_norm", "w_out",
           "ffn_norm", "w_up", "conv_w", "conv_b", "w_down", "final_norm")
INPUTS = ("x",) + WEIGHTS + ("loss_target",) + tuple("m_" + n for n in WEIGHTS) + tuple("v_" + n for n in WEIGHTS)


def _columns(g):
    return jnp.transpose(g, (1, 0, 2)).reshape(g.shape[1], 4 * g.shape[2])


def _uncolumns(w):
    r, c4 = w.shape
    return jnp.transpose(w.reshape(r, 4, c4 // 4), (1, 0, 2))


def _compute_layouts(full, conv_w, small, layer):
    p = {n: small[n][layer] for n in SMALL if n != "final_norm"}
    if "w_in" in full:
        p["w_in_t"] = full["w_in"].reshape(-1, full["w_in"].shape[2])
        uq_t = full["w_uq"].reshape(C_HEADS, HEAD + C_ROPE, C_RANK)
        p["w_uq_t"] = jnp.concatenate([uq_t[:, :HEAD].reshape(C_W, C_RANK),
                                       jnp.pad(uq_t[:, HEAD:], ((0, 0), (0, HEAD - C_ROPE), (0, 0))).reshape(C_W, C_RANK)], axis=0)
        p["w_ukv"] = _columns(full["w_ukv"])
        p["w_out"] = full["w_out"].reshape(-1, full["w_out"].shape[2])
    if "w_up" in full:
        up = full["w_up"]
        p["w_up"] = up.reshape(2, 2, up.shape[1], up.shape[2])
        p["w_down"] = full["w_down"].reshape(-1, full["w_down"].shape[2])
        fh = conv_w.shape[1] // 2
        taps = jnp.transpose(conv_w.reshape(3, 2, fh), (1, 0, 2))
        p["cwb"] = jnp.concatenate([taps, small["conv_b"][layer].reshape(2, 1, fh), jnp.zeros((2, 4, fh), F32)], axis=1)
    return p


def _shard_layout(name, g):
    if name == "w_in":
        return g.reshape(4, -1, g.shape[1])
    if name == "w_uq":
        uq_t = jnp.concatenate([g[:C_W].reshape(C_HEADS, HEAD, C_RANK),
                                g[C_W:].reshape(C_HEADS, HEAD, C_RANK)[:, :C_ROPE]], axis=1)
        return uq_t.reshape(4, -1, C_RANK)
    if name == "w_ukv":
        return _uncolumns(g)
    if name == "w_up":
        return g.reshape(4, g.shape[2], g.shape[3])
    return g.reshape(4, -1, g.shape[1])


def _conv_grads(dcwb):
    fh = dcwb.shape[2]
    return jnp.transpose(dcwb[:, 0:3, :], (1, 0, 2)).reshape(3, 2 * fh), dcwb[:, 3, :].reshape(2 * fh)


def _shard_layouts(g):
    return ({n: _shard_layout(n, g[n]) for n in BIG},) + _conv_grads(g["cwb"])


def kernel(x, attn_norm, w_in, a_q_norm, a_k_norm, c_q_norm, c_kv_norm, w_uq, w_ukv, out_norm, w_out, ffn_norm, w_up, conv_w, conv_b, w_down, final_norm, loss_target, m_attn_norm, m_w_in, m_a_q_norm, m_a_k_norm, m_c_q_norm, m_c_kv_norm, m_w_uq, m_w_ukv, m_out_norm, m_w_out, m_ffn_norm, m_w_up, m_conv_w, m_conv_b, m_w_down, m_final_norm, v_attn_norm, v_w_in, v_a_q_norm, v_a_k_norm, v_c_q_norm, v_c_kv_norm, v_w_uq, v_w_ukv, v_out_norm, v_w_out, v_ffn_norm, v_w_up, v_conv_w, v_conv_b, v_w_down, v_final_norm):
    a = dict(zip(INPUTS, (x, attn_norm, w_in, a_q_norm, a_k_norm, c_q_norm, c_kv_norm, w_uq, w_ukv, out_norm, w_out, ffn_norm, w_up, conv_w, conv_b, w_down, final_norm, loss_target, m_attn_norm, m_w_in, m_a_q_norm, m_a_k_norm, m_c_q_norm, m_c_kv_norm, m_w_uq, m_w_ukv, m_out_norm, m_w_out, m_ffn_norm, m_w_up, m_conv_w, m_conv_b, m_w_down, m_final_norm, v_attn_norm, v_w_in, v_a_q_norm, v_a_k_norm, v_c_q_norm, v_c_kv_norm, v_w_uq, v_w_ukv, v_out_norm, v_w_out, v_ffn_norm, v_w_up, v_conv_w, v_conv_b, v_w_down, v_final_norm)))
    nl = w_in.shape[0]
    seq = x.shape[1]
    for n in TRANSPOSED:
        for kind in ("", "m_", "v_"):
            a[kind + n] = jnp.swapaxes(a[kind + n], 1, 2)
    chip = 2 * lax.axis_index("x") + lax.axis_index("y")
    place = jnp.stack([lax.axis_index("c"), chip]).astype(jnp.int32)
    tabs = _rope_tables(seq)

    assert nl == 2
    quarter = conv_w.shape[2]
    taps = jnp.pad(conv_w, ((0, 0), (0, 8 - conv_w.shape[1]), (0, 0))).reshape(nl * 8, quarter)
    taps_full = gather_small("gather_taps", taps).reshape(4, nl, 8, quarter)
    conv_full = [jnp.transpose(taps_full[:, l, 0:3, :], (1, 0, 2)).reshape(3, 4 * quarter) for l in range(nl)]
    flights = {}
    after = taps_full
    for key, l, names in (("mix0", 0, MIXER_W), ("ffn0", 0, FFN_W), ("mix1", 1, MIXER_W), ("ffn1", 1, FFN_W)):
        slots = [cast_layer(f"cast{l}_{n}", place, a[n], l) for n in names]
        send_sems, recv_sems, flying, after = split_start(f"gather_{key}_start", slots, _gather_plan, after)
        flights[key] = (names, send_sems, recv_sems, flying)
    started = after[0, 0]

    def land(key, after):
        names, send_sems, recv_sems, flying = flights[key]
        landed = split_wait(f"gather_{key}_wait", send_sems, recv_sems, flying, _gather_plan, after)
        return dict(zip(names, forward_halves(f"gather_{key}_forward", landed)))

    h = x[0]
    layers, saved = [], []
    for l in range(nl):
        p = _compute_layouts(land(f"mix{l}", [after if l == 0 else h]), conv_full[l], a, l)
        if l == 0:
            p["attn_norm"] = p["attn_norm"] + started
        x1, sv = _mixers_fwd(h, p, tabs)
        p.update(_compute_layouts(land(f"ffn{l}", [x1]), conv_full[l], a, l))
        h = _ffn_fwd(x1, p, sv)
        layers.append(p)
        saved.append(sv)
    dm = h.shape[1]
    yf = rmsnorm_fwd("final_norm", h, final_norm, dm, F32)
    loss, dyf = loss_head("loss_head", yf, loss_target[0])
    loss = lax.psum(loss, ("x", "y", "c"))
    dx, d_final = rmsnorm_bwd("final_norm_bwd", dyf, h, final_norm, dm, (F32,))

    def swap_begin(tag, names, g):
        g_list = [_shard_layout(n, g[n]) for n in names]
        theirs = [lax.empty((4,) + _half_shape(s.shape), s.dtype) for s in g_list]
        send_sems, recv_sems, flying, token = split_start(f"{tag}_sibling_start", g_list + theirs, _sibling_plan, place)
        return (names, send_sems, recv_sems, flying), token[0, 0]

    def exchange_begin(tag, swap, after):
        names, send_sems, recv_sems, flying = swap
        landed = split_wait(f"{tag}_sibling_wait", send_sems, recv_sems, flying, _sibling_plan, after)
        g_list, theirs = landed[:len(names)], landed[len(names):]
        sums = [add_half(f"{tag}_add_{n}", place, gr, t) for n, gr, t in zip(names, g_list, theirs)]
        parts = [lax.empty(s.shape, s.dtype) for s in sums]
        send_sems, recv_sems, flying, token = split_start(f"{tag}_chips_start", sums + parts, _exchange_plan, place)
        return (names, [gr.shape[1:] for gr in g_list], send_sems, recv_sems, flying), token

    def exchange_end(tag, flight, after):
        names, shard_shapes, send_sems, recv_sems, flying = flight
        landed = split_wait(f"{tag}_chips_wait", send_sems, recv_sems, flying, _exchange_plan, after)
        sums, parts = landed[:len(names)], landed[len(names):]
        halves = [sum_chips(f"{tag}_sum_{n}", place, s, p, shp) for n, s, p, shp in zip(names, sums, parts, shard_shapes)]
        return dict(zip(names, join_halves(f"{tag}_join", halves)))

    grads, conv_grads, flights_f, flights_m = [None] * nl, [None] * nl, [None] * nl, [None] * nl
    after = 0.0
    for l in reversed(range(nl)):
        sv, p = saved[l], layers[l]
        dxn, g = _ffn_bwd(dx, sv, p, after)
        swap_f, after = swap_begin(f"reduce{l}f", FFN_W, g)
        dx1, dy, g_out = _out_proj_bwd(dxn, dx, sv, p, after)
        flights_f[l], token = exchange_begin(f"reduce{l}f", swap_f, [dy])
        dhn, g_mix = _mixers_bwd(dy, sv, p, tabs, token[0, 0])
        g.update(g_out)
        g.update(g_mix)
        swap_m, after = swap_begin(f"reduce{l}m", MIXER_W, g)
        dx, g_norm = _attn_norm_bwd(dhn, dx1, sv, p, after)
        g.update(g_norm)
        flights_m[l], token = exchange_begin(f"reduce{l}m", swap_m, [dx])
        after = token[0, 0]
        grads[l], conv_grads[l] = g, _conv_grads(g["cwb"])
    grad_x = dx

    out = {}
    reduced1 = exchange_end("reduce1f", flights_f[1], [token])
    for n in FFN_W:
        out[n] = adamw_layer(f"adamw1_{n}", 1, reduced1[n], a[n], a["m_" + n], a["v_" + n])
    reduced1 = exchange_end("reduce1m", flights_m[1], [out[n][3] for n in FFN_W])
    for n in MIXER_W:
        out[n] = adamw_layer(f"adamw1_{n}", 1, reduced1[n], a[n], a["m_" + n], a["v_" + n])

    small_g = {n: jnp.stack([grads[l][n] for l in range(nl)]) for n in SMALL if n not in ("conv_b", "final_norm")}
    small_g["conv_b"] = jnp.stack([cg[1] for cg in conv_grads])
    small_g["final_norm"] = d_final
    conv_w_g = jnp.stack([cg[0] for cg in conv_grads])
    shapes = [a[n].shape for n in SMALL] + [conv_w_g.shape]
    summed = _unpack(allreduce_small("reduce_small", _pack([small_g[n] for n in SMALL] + [conv_w_g])), shapes)
    small_g = dict(zip(SMALL, summed[:-1]))
    small_g["conv_w"] = lax.dynamic_slice_in_dim(summed[-1], chip * quarter, quarter, axis=2)
    names = SMALL + ("conv_w",)
    shapes = [a[n].shape for n in names]
    d_small, m_small, v_small = adamw_packed(
        "adamw_small", _pack([small_g[n] for n in names]), _pack([a[n] for n in names]),
        _pack([a["m_" + n] for n in names]), _pack([a["v_" + n] for n in names]))

    reduced0 = exchange_end("reduce0f", flights_f[0], [out[n][3] for n in BIG] + [d_small])
    for n in FFN_W:
        out[n] = adamw_layer(f"adamw0_{n}", 0, reduced0[n], a[n], a["m_" + n], a["v_" + n], prev=out[n])
    reduced0 = exchange_end("reduce0m", flights_m[0], [out[n][3] for n in FFN_W])
    for n in MIXER_W:
        out[n] = adamw_layer(f"adamw0_{n}", 0, reduced0[n], a[n], a["m_" + n], a["v_" + n], prev=out[n])
    for n, d_, m_, v_ in zip(names, _unpack(d_small, shapes), _unpack(m_small, shapes), _unpack(v_small, shapes)):
        out[n] = (small_g[n], d_, m_, v_)
    for n in TRANSPOSED:
        out[n] = [jnp.swapaxes(o, 1, 2) for o in out[n]]

    return (loss, grad_x[None]) + tuple(out[n][k] for k in range(4) for n in WEIGHTS)
```

```python
import functools

import jax
import jax.numpy as jnp
import numpy as np
from jax import lax
from jax.experimental import pallas as pl
from jax.experimental.pallas import tpu as pltpu

F32 = jnp.float32
BF16 = jnp.bfloat16
MESH = pl.DeviceIdType.MESH

V7X_VMEM_BYTES = 64 * 1024 * 1024
VMEM_LIMIT = V7X_VMEM_BYTES - 6 * 1024 * 1024
LANES = 128
ELEMENTWISE_BLOCK_BYTES = 2 * 1024 * 1024

HEAD = 128
A_HEADS, A_KV = 6, 2
B_HEADS, B_GROUPS = 4, 3
B_DILATIONS = (1, 4, 16)
B_HALF = 64
C_HEADS, C_RANK, C_ROPE = 6, 512, 64
GRID_W = 64
A_THETA, B_THETA, C_THETA = 10000.0, 500000.0, 10000.0
B_ROPE_DIM = 32
EPS = 1e-6
A_W, B_W, C_W = A_HEADS * HEAD, B_HEADS * HEAD, C_HEADS * HEAD
O_AQ, O_AK, O_AV = 0, 768, 1024
O_BQ, O_BK, O_BV = 1280, 2816, 3328
O_CQ, O_CKV, O_CKR = 3840, 4352, 4864
IN_W = 4928
PROJ_W = 5120

ADAM_LR, ADAM_B1, ADAM_B2, ADAM_EPS, ADAM_WD, ADAM_STEP = 0.001, 0.9, 0.999, 1e-08, 0.01, 10

NN = (((1,), (0,)), ((), ()))
NT = (((1,), (1,)), ((), ()))
TN = (((0,), (0,)), ((), ()))


def _dot(a, b, dims=NN):
    return lax.dot_general(a, b, dims, preferred_element_type=F32)


def _params(*sem):
    return pltpu.CompilerParams(dimension_semantics=sem if sem else None, vmem_limit_bytes=VMEM_LIMIT)


def _tile(n, target, unit=LANES):
    if n <= target:
        return n
    best = 0
    for t in range(unit, target + 1, unit):
        if n % t == 0:
            best = t
    return best if best else n


def _rows(r, width, itemsize=4):
    return _tile(r, max(8, ELEMENTWISE_BLOCK_BYTES // (width * itemsize)), 8)


def _matmul_call(name, dims, grid, a_spec, b_spec, o_spec, out_shape, acc_shape, add_spec=None):
    nk = grid[2]

    def body(*refs):
        a_ref, b_ref = refs[:2]
        add_ref = None if add_spec is None else refs[2]
        o_ref = refs[2 if add_spec is None else 3]

        def finish(r):
            if add_ref is not None:
                r = r + add_ref[...]
            o_ref[...] = r.astype(o_ref.dtype)

        if nk == 1:
            finish(_dot(a_ref[...], b_ref[...], dims))
            return
        acc = refs[-1]
        k = pl.program_id(2)

        @pl.when(k == 0)
        def _():
            acc[...] = _dot(a_ref[...], b_ref[...], dims)

        if nk > 2:
            @pl.when((k > 0) & (k < nk - 1))
            def _():
                acc[...] += _dot(a_ref[...], b_ref[...], dims)

        @pl.when(k == nk - 1)
        def _():
            finish(acc[...] + _dot(a_ref[...], b_ref[...], dims))

    in_specs = [a_spec, b_spec] + ([] if add_spec is None else [add_spec])
    return pl.pallas_call(
        body, name=name, grid=grid, in_specs=in_specs, out_specs=o_spec, out_shape=out_shape,
        scratch_shapes=[pltpu.VMEM(acc_shape, F32)] if nk > 1 else [],
        compiler_params=_params("parallel", "parallel", "arbitrary"))


def matmul(name, a, b, mode, out_dtype, add=None, tm=1024, tn=1024, tk=2816):
    if mode == "nn":
        (m, k), (k2, n) = a.shape, b.shape
    elif mode == "nt":
        (m, k), (n, k2) = a.shape, b.shape
    else:
        (k, m), (k2, n) = a.shape, b.shape
    assert k == k2, (name, a.shape, b.shape)
    tm, tn, tk = _tile(m, tm), _tile(n, tn), _tile(k, tk)
    grid = (m // tm, n // tn, k // tk)
    if mode == "tn":
        a_spec = pl.BlockSpec((tk, tm), lambda i, j, kk: (kk, i))
    else:
        a_spec = pl.BlockSpec((tm, tk), lambda i, j, kk: (i, kk))
    whole_b = {"pipeline_mode": pl.Buffered(1)} if grid[1] == 1 and grid[2] == 1 and grid[0] > 1 else {}
    if mode == "nt":
        b_spec = pl.BlockSpec((tn, tk), lambda i, j, kk: (j, kk), **whole_b)
    else:
        b_spec = pl.BlockSpec((tk, tn), lambda i, j, kk: (kk, j), **whole_b)
    o_spec = pl.BlockSpec((tm, tn), lambda i, j, kk: (i, j))
    dims = {"nn": NN, "nt": NT, "tn": TN}[mode]
    call = _matmul_call(name, dims, grid, a_spec, b_spec, o_spec, jax.ShapeDtypeStruct((m, n), out_dtype),
                        (tm, tn), None if add is None else o_spec)
    return call(a, b) if add is None else call(a, b, add)


def rmsnorm_fwd(name, x, g, width, out_dtype):
    r, cols = x.shape
    nb = cols // width
    tr = _rows(r, width)

    def body(x_ref, g_ref, o_ref):
        xv = x_ref[...]
        rs = lax.rsqrt(jnp.mean(xv * xv, axis=-1, keepdims=True) + EPS)
        o_ref[...] = (xv * rs * g_ref[...]).astype(o_ref.dtype)

    blk = pl.BlockSpec((tr, width), lambda i, j: (i, j))
    return pl.pallas_call(
        body, name=name, grid=(r // tr, nb),
        in_specs=[blk, pl.BlockSpec((1, width), lambda i, j: (0, 0))], out_specs=blk,
        out_shape=jax.ShapeDtypeStruct((r, cols), out_dtype),
        compiler_params=_params("parallel", "parallel"))(x, g.reshape(1, width))


def rmsnorm_bwd(name, dy, x, g, width, out_dtypes=(F32,), add=None):
    r, cols = x.shape
    nb = cols // width
    tr = _rows(r, width)
    n_out = len(out_dtypes)

    def body(*refs):
        dy_ref, x_ref, g_ref = refs[:3]
        add_ref = refs[3] if add is not None else None
        outs = refs[-(n_out + 1):-1]
        dg_ref = refs[-1]
        xv = x_ref[...]
        dyv = dy_ref[...].astype(F32)
        rs = lax.rsqrt(jnp.mean(xv * xv, axis=-1, keepdims=True) + EPS)
        xh = xv * rs
        dyg = dyv * g_ref[...]
        dx = rs * (dyg - xh * jnp.mean(dyg * xh, axis=-1, keepdims=True))
        if add_ref is not None:
            dx = dx + add_ref[...]
        for o in outs:
            o[...] = dx.astype(o.dtype)

        @pl.when((pl.program_id(0) == 0) & (pl.program_id(1) == 0))
        def _():
            dg_ref[...] = jnp.zeros_like(dg_ref)

        dg_ref[...] += jnp.sum(dyv * xh, axis=0, keepdims=True)

    blk = pl.BlockSpec((tr, width), lambda i, j: (i, j))
    vec = pl.BlockSpec((1, width), lambda i, j: (0, 0))
    ins = [dy, x, g.reshape(1, width)] + ([add] if add is not None else [])
    res = pl.pallas_call(
        body, name=name, grid=(r // tr, nb),
        in_specs=[blk, blk, vec] + ([blk] if add is not None else []),
        out_specs=[blk] * n_out + [vec],
        out_shape=[jax.ShapeDtypeStruct((r, cols), dt) for dt in out_dtypes] + [jax.ShapeDtypeStruct((1, width), F32)],
        compiler_params=_params("arbitrary", "arbitrary"))(*ins)
    return tuple(res[:n_out]) + (res[n_out].reshape(width),)


def _rope_angles(pos, dim, theta):
    inv = theta ** (-jnp.arange(0, dim, 2, dtype=F32) / dim)
    ang = pos.astype(F32)[:, None] * inv[None, :]
    return jnp.cos(ang), jnp.sin(ang)


def _rope_tables(s):
    rows = s // GRID_W
    row_pos = jnp.repeat(jnp.arange(rows), GRID_W)
    col_pos = jnp.tile(jnp.arange(GRID_W), rows)
    t_pos = jnp.arange(s)
    z = lambda n: jnp.zeros((s, n), F32)
    o = lambda n: jnp.ones((s, n), F32)
    cr, sr = _rope_angles(row_pos, HEAD // 2, A_THETA)
    cc, sc = _rope_angles(col_pos, HEAD // 2, A_THETA)
    tab_a = (jnp.concatenate([cr, cr, cc, cc], 1), jnp.concatenate([-sr, z(32), -sc, z(32)], 1),
             jnp.concatenate([z(32), sr, z(32), sc], 1), 32)
    cp, sp = _rope_angles(t_pos, B_ROPE_DIM, B_THETA)
    tab_b = (jnp.concatenate([cp, cp, o(96)], 1), jnp.concatenate([-sp, z(112)], 1),
             jnp.concatenate([z(16), sp, z(96)], 1), 16)
    cm, sm = _rope_angles(t_pos, C_ROPE, C_THETA)
    tab_c = (jnp.concatenate([cm, cm, o(64)], 1), jnp.concatenate([-sm, z(96)], 1),
             jnp.concatenate([z(32), sm, z(64)], 1), 32)

    def transposed(tab):
        c, s1, s2, h = tab
        return (c, jnp.roll(s2, -h, axis=1), jnp.roll(s1, h, axis=1), h)

    return {k: (t, transposed(t)) for k, t in (("a", tab_a), ("b", tab_b), ("c", tab_c))}


def rope(name, x, tab, out_dtype):
    c, s1, s2, h = tab
    s, cols = x.shape
    nh = cols // HEAD
    ts = _rows(s, HEAD)

    def body(x_ref, c_ref, s1_ref, s2_ref, o_ref):
        xv = x_ref[...].astype(F32)
        out = xv * c_ref[...] + pltpu.roll(xv, HEAD - h, 1) * s1_ref[...] + pltpu.roll(xv, h, 1) * s2_ref[...]
        o_ref[...] = out.astype(o_ref.dtype)

    blk = pl.BlockSpec((ts, HEAD), lambda i, j: (i, j))
    tb = pl.BlockSpec((ts, HEAD), lambda i, j: (i, 0))
    return pl.pallas_call(
        body, name=name, grid=(s // ts, nh), in_specs=[blk, tb, tb, tb], out_specs=blk,
        out_shape=jax.ShapeDtypeStruct((s, cols), out_dtype),
        compiler_params=_params("parallel", "parallel"))(x, c, s1, s2)


ATTN_TQ = 256


def attn_fwd(name, parts, v, v_group, nheads, scale):
    s = v.shape[0]
    tq = ATTN_TQ
    npart = len(parts)

    def body(*refs):
        v_ref, o_ref, lse_ref = refs[2 * npart:]
        sc = None
        for p in range(npart):
            t = _dot(refs[2 * p][...], refs[2 * p + 1][...], NT)
            sc = t if sc is None else sc + t
        sc = sc * scale
        m = jnp.max(sc, axis=-1, keepdims=True)
        e = jnp.exp(sc - m)
        l = jnp.sum(e, axis=-1, keepdims=True)
        o_ref[...] = _dot(e.astype(BF16), v_ref[...]) / l
        lse_ref[...] = jnp.broadcast_to(m + jnp.log(l), (tq, HEAD))

    in_specs, ins = [], []
    for q, qoff, k, kg in parts:
        in_specs.append(pl.BlockSpec((tq, HEAD), lambda h, i, qoff=qoff: (i, qoff + h)))
        in_specs.append(pl.BlockSpec((s, HEAD), lambda h, i, kg=kg: (0, h // kg)))
        ins += [q, k]
    in_specs.append(pl.BlockSpec((s, HEAD), lambda h, i: (0, h // v_group)))
    out_blk = pl.BlockSpec((tq, HEAD), lambda h, i: (i, h))
    return pl.pallas_call(
        body, name=name, grid=(nheads, s // tq), in_specs=in_specs, out_specs=[out_blk, out_blk],
        out_shape=[jax.ShapeDtypeStruct((s, nheads * HEAD), F32)] * 2,
        compiler_params=_params("parallel", "parallel"))(*ins, v)


def attn_bwd(name, parts, v, v_group, lse, o, do, nheads, scale):
    s = v.shape[0]
    tq = ATTN_TQ
    npart = len(parts)

    def body(*refs):
        v_ref, lse_ref, o_ref, do_ref = refs[2 * npart:2 * npart + 4]
        outs = refs[2 * npart + 4:]
        dq_refs, dk_refs, dv_ref = outs[:npart], outs[npart:2 * npart], outs[2 * npart]
        h, i = pl.program_id(0), pl.program_id(1)
        sc = None
        for p in range(npart):
            t = _dot(refs[2 * p][...], refs[2 * p + 1][...], NT)
            sc = t if sc is None else sc + t
        pr = jnp.exp(sc * scale - lse_ref[:, 0:1])
        dov = do_ref[...]
        delta = jnp.sum(dov.astype(F32) * o_ref[...], axis=-1, keepdims=True)
        dp = _dot(dov, v_ref[...], NT)
        ds = (pr * (dp - delta) * scale).astype(BF16)

        @pl.when((i == 0) & (h % v_group == 0))
        def _():
            dv_ref[...] = jnp.zeros_like(dv_ref)

        dv_ref[...] += _dot(pr.astype(BF16), dov, TN)
        for p in range(npart):
            kg = parts[p][3]
            dq_refs[p][...] = _dot(ds, refs[2 * p + 1][...])

            @pl.when((i == 0) & (h % kg == 0))
            def _(p=p):
                dk_refs[p][...] = jnp.zeros_like(dk_refs[p])

            dk_refs[p][...] += _dot(ds, refs[2 * p][...], TN)

    in_specs, ins = [], []
    for q, qoff, k, kg in parts:
        in_specs.append(pl.BlockSpec((tq, HEAD), lambda h, i, qoff=qoff: (i, qoff + h)))
        in_specs.append(pl.BlockSpec((s, HEAD), lambda h, i, kg=kg: (0, h // kg)))
        ins += [q, k]
    hq_blk = pl.BlockSpec((tq, HEAD), lambda h, i: (i, h))
    in_specs += [pl.BlockSpec((s, HEAD), lambda h, i: (0, h // v_group)), hq_blk, hq_blk, hq_blk]
    out_specs = [hq_blk] * npart
    out_shape = [jax.ShapeDtypeStruct((s, nheads * HEAD), F32)] * npart
    for q, qoff, k, kg in parts:
        out_specs.append(pl.BlockSpec((s, HEAD), lambda h, i, kg=kg: (0, h // kg)))
        out_shape.append(jax.ShapeDtypeStruct((s, nheads // kg * HEAD), F32))
    out_specs.append(pl.BlockSpec((s, HEAD), lambda h, i: (0, h // v_group)))
    out_shape.append(jax.ShapeDtypeStruct((s, nheads // v_group * HEAD), F32))
    res = pl.pallas_call(
        body, name=name, grid=(nheads, s // tq), in_specs=in_specs, out_specs=out_specs, out_shape=out_shape,
        compiler_params=_params("arbitrary", "arbitrary"))(*ins, v, lse, o, do)
    return list(res[:npart]), list(res[npart:2 * npart]), res[2 * npart]


def _band_windows(lf):
    for ib in range(lf // HEAD):
        q0 = ib * HEAD
        yield q0, max(0, q0 - B_HALF), min(lf, q0 + HEAD + B_HALF)


def _band_mask(q0, lo, hi):
    qpos = q0 + lax.broadcasted_iota(jnp.int32, (HEAD, hi - lo), 0)
    kpos = lo + lax.broadcasted_iota(jnp.int32, (HEAD, hi - lo), 1)
    return jnp.abs(qpos - kpos) <= B_HALF


def _class_rows(r, d, start, size):
    return pl.ds(r + d * start, size, stride=d) if d > 1 else pl.ds(start, size)


def dilated_fwd(name, q, k, v_src, v_block):
    s = k.shape[0]
    scale = HEAD ** -0.5

    def body(q_ref, k_ref, v_ref, y_ref, lse_ref, o_scr, l_scr):
        g_now = pl.program_id(1)
        for g, d in enumerate(B_DILATIONS):
            @pl.when(g_now == g)
            def _(g=g, d=d):
                for r in range(d):
                    for q0, lo, hi in _band_windows(s // d):
                        mine = _class_rows(r, d, q0, HEAD)
                        keys = _class_rows(r, d, lo, hi - lo)
                        sc = _dot(q_ref[mine, :].astype(BF16), k_ref[keys, :].astype(BF16), NT) * scale
                        sc = jnp.where(_band_mask(q0, lo, hi), sc, -1e30)
                        m = jnp.max(sc, axis=-1, keepdims=True)
                        e = jnp.exp(sc - m)
                        l = jnp.sum(e, axis=-1, keepdims=True)
                        o_scr.at[g][mine, :] = _dot((e / l).astype(BF16), v_ref[keys, :].astype(BF16))
                        l_scr.at[g][mine, :] = jnp.broadcast_to(m + jnp.log(l), (HEAD, HEAD))

        @pl.when(g_now == B_GROUPS - 1)
        def _():
            a, b, c = l_scr[0], l_scr[1], l_scr[2]
            m = jnp.maximum(jnp.maximum(a, b), c)
            ea, eb, ec = jnp.exp(a - m), jnp.exp(b - m), jnp.exp(c - m)
            den = ea + eb + ec
            y_ref[...] = (ea / den) * o_scr[0] + (eb / den) * o_scr[1] + (ec / den) * o_scr[2]
            lse_ref[...] = m + jnp.log(den)

    blk = lambda f: pl.BlockSpec((s, HEAD), f)
    per_head = blk(lambda h, g: (0, h))
    return pl.pallas_call(
        body, name=name, grid=(B_HEADS, B_GROUPS),
        in_specs=[blk(lambda h, g: (0, B_HEADS * g + h)), per_head, blk(lambda h, g: (0, v_block + h))],
        out_specs=[per_head, per_head], out_shape=[jax.ShapeDtypeStruct((s, B_W), F32)] * 2,
        scratch_shapes=[pltpu.VMEM((B_GROUPS, s, HEAD), F32)] * 2,
        compiler_params=_params("parallel", "arbitrary"))(q, k, v_src)


def dilated_bwd(name, q, k, v_src, v_block, dy, y, lse):
    s = k.shape[0]
    scale = HEAD ** -0.5

    def body(q_ref, k_ref, v_ref, dy_ref, y_ref, lse_ref, dq_ref, dk_ref, dv_ref, delta):
        g_now = pl.program_id(1)

        @pl.when(g_now == 0)
        def _():
            dk_ref[...] = jnp.zeros_like(dk_ref)
            dv_ref[...] = jnp.zeros_like(dv_ref)
            delta[...] = jnp.broadcast_to(jnp.sum(dy_ref[...] * y_ref[...], axis=-1, keepdims=True), (s, HEAD))

        for g, d in enumerate(B_DILATIONS):
            @pl.when(g_now == g)
            def _(d=d):
                for r in range(d):
                    for q0, lo, hi in _band_windows(s // d):
                        mine = _class_rows(r, d, q0, HEAD)
                        keys = _class_rows(r, d, lo, hi - lo)
                        qv, kv = q_ref[mine, :].astype(BF16), k_ref[keys, :].astype(BF16)
                        dyv = dy_ref[mine, :].astype(BF16)
                        sc = _dot(qv, kv, NT) * scale
                        pr = jnp.where(_band_mask(q0, lo, hi), jnp.exp(sc - lse_ref[mine, :][:, 0:1]), 0.0)
                        dp = _dot(dyv, v_ref[keys, :].astype(BF16), NT)
                        ds = (pr * (dp - delta[mine, :][:, 0:1]) * scale).astype(BF16)
                        dv_ref[keys, :] = dv_ref[keys, :] + _dot(pr.astype(BF16), dyv, TN)
                        dq_ref[mine, :] = _dot(ds, kv)
                        dk_ref[keys, :] = dk_ref[keys, :] + _dot(ds, qv, TN)

    blk = lambda f: pl.BlockSpec((s, HEAD), f)
    per_head = blk(lambda h, g: (0, h))
    by_group = blk(lambda h, g: (0, B_HEADS * g + h))
    return pl.pallas_call(
        body, name=name, grid=(B_HEADS, B_GROUPS),
        in_specs=[by_group, per_head, blk(lambda h, g: (0, v_block + h)), per_head, per_head, per_head],
        out_specs=[by_group, per_head, per_head],
        out_shape=[jax.ShapeDtypeStruct((s, B_GROUPS * B_W), F32)] + [jax.ShapeDtypeStruct((s, B_W), F32)] * 2,
        scratch_shapes=[pltpu.VMEM((s, HEAD), F32)],
        compiler_params=_params("parallel", "arbitrary"))(q, k, v_src, dy, y, lse)


FFN_TN = 256


def _conv(h, cw):
    s = h.shape[0]
    row = lax.broadcasted_iota(jnp.int32, h.shape, 0)
    prev = jnp.where(row == 0, 0.0, pltpu.roll(h, 1, 0))
    nxt = jnp.where(row == s - 1, 0.0, pltpu.roll(h, s - 1, 0))
    return prev * cw[0:1, :] + h * cw[1:2, :] + nxt * cw[2:3, :] + cw[3:4, :], prev, nxt


def _sigmoid(x):
    return 1.0 / (1.0 + jnp.exp(-x))


def ffn_up(name, xn, w_up, cwb):
    s, dm = xn.shape
    quarter = w_up.shape[3]
    fh = 2 * quarter
    tn = FFN_TN
    per = quarter // tn

    def body(x_ref, w_ref, cw_ref, h_ref, act_ref):
        xv = x_ref[...]
        hg = _dot(xv, w_ref[0])
        hu = _dot(xv, w_ref[1])
        h_ref[0] = hg
        h_ref[1] = hu
        gc, _, _ = _conv(hg, cw_ref[0])
        uc, _, _ = _conv(hu, cw_ref[1])
        act_ref[...] = (gc * _sigmoid(gc) * uc).astype(BF16)

    return pl.pallas_call(
        body, name=name, grid=(fh // tn,),
        in_specs=[pl.BlockSpec((s, dm), lambda t: (0, 0), pipeline_mode=pl.Buffered(1)),
                  pl.BlockSpec((2, None, dm, tn), lambda t: (0, t // per, 0, t % per)),
                  pl.BlockSpec((2, 8, tn), lambda t: (0, 0, t))],
        out_specs=[pl.BlockSpec((2, s, tn), lambda t: (0, 0, t)), pl.BlockSpec((s, tn), lambda t: (0, t))],
        out_shape=[jax.ShapeDtypeStruct((2, s, fh), F32), jax.ShapeDtypeStruct((s, fh), BF16)],
        compiler_params=_params("parallel"))(xn, w_up, cwb)


def ffn_gate_bwd(name, h, dact, cwb):
    _, s, fh = h.shape
    tn = FFN_TN

    def body(h_ref, da_ref, cw_ref, dh_ref, dcw_ref):
        gc, gp, gn = _conv(h_ref[0], cw_ref[0])
        uc, up, un = _conv(h_ref[1], cw_ref[1])
        sg = _sigmoid(gc)
        da = da_ref[...]
        dgc = da * uc * (sg * (1.0 + gc * (1.0 - sg)))
        duc = da * (gc * sg)
        row = lax.broadcasted_iota(jnp.int32, da.shape, 0)
        for idx, (hv, prev, nxt, dc) in enumerate(((h_ref[0], gp, gn, dgc), (h_ref[1], up, un, duc))):
            cw = cw_ref[idx]
            from_next = jnp.where(row == s - 1, 0.0, pltpu.roll(dc, s - 1, 0))
            from_prev = jnp.where(row == 0, 0.0, pltpu.roll(dc, 1, 0))
            dh_ref[idx] = (from_next * cw[0:1, :] + dc * cw[1:2, :] + from_prev * cw[2:3, :]).astype(BF16)
            dcw_ref[idx, 0:1, :] = jnp.sum(prev * dc, axis=0, keepdims=True)
            dcw_ref[idx, 1:2, :] = jnp.sum(hv * dc, axis=0, keepdims=True)
            dcw_ref[idx, 2:3, :] = jnp.sum(nxt * dc, axis=0, keepdims=True)
            dcw_ref[idx, 3:4, :] = jnp.sum(dc, axis=0, keepdims=True)
            dcw_ref[idx, 4:8, :] = jnp.zeros((4, tn), F32)

    return pl.pallas_call(
        body, name=name, grid=(fh // tn,),
        in_specs=[pl.BlockSpec((2, s, tn), lambda t: (0, 0, t)), pl.BlockSpec((s, tn), lambda t: (0, t)),
                  pl.BlockSpec((2, 8, tn), lambda t: (0, 0, t))],
        out_specs=[pl.BlockSpec((2, s, tn), lambda t: (0, 0, t)), pl.BlockSpec((2, 8, tn), lambda t: (0, 0, t))],
        out_shape=[jax.ShapeDtypeStruct((2, s, fh), BF16), jax.ShapeDtypeStruct((2, 8, fh), F32)],
        compiler_params=_params("parallel"))(h, dact, cwb)


def ffn_dx(name, dh, w_up):
    _, s, fh = dh.shape
    dm, quarter = w_up.shape[2], w_up.shape[3]
    tk = _tile(quarter, 2816)
    per = quarter // tk
    tm, tn = _tile(s, 1024), _tile(dm, 1024)
    grid = (s // tm, dm // tn, 4 * per)
    a_spec = pl.BlockSpec((None, tm, tk), lambda i, j, k: (k // (2 * per), i, k % (2 * per)))
    b_spec = pl.BlockSpec((None, None, tn, tk), lambda i, j, k: (k // (2 * per), (k // per) % 2, j, k % per))
    o_spec = pl.BlockSpec((tm, tn), lambda i, j, k: (i, j))
    return _matmul_call(name, NT, grid, a_spec, b_spec, o_spec, jax.ShapeDtypeStruct((s, dm), F32), (tm, tn))(dh, w_up)


def ffn_dw_up(name, xn, dh):
    _, s, fh = dh.shape
    dm = xn.shape[1]
    quarter = fh // 2
    tn = _tile(quarter, 1536)
    per = quarter // tn
    tm, tk = _tile(dm, 1024), _tile(s, 2048)
    grid = (dm // tm, 4 * per, s // tk)
    a_spec = pl.BlockSpec((tk, tm), lambda i, j, k: (k, i))
    b_spec = pl.BlockSpec((None, tk, tn), lambda i, j, k: (j // (2 * per), k, j % (2 * per)))
    o_spec = pl.BlockSpec((None, None, tm, tn), lambda i, j, k: (j // (2 * per), (j // per) % 2, i, j % per))
    return _matmul_call(name, TN, grid, a_spec, b_spec, o_spec,
                        jax.ShapeDtypeStruct((2, 2, dm, quarter), F32), (tm, tn))(xn, dh)


def loss_head(name, y, target):
    s, dm = y.shape
    ts = _rows(s, dm)

    def body(y_ref, t_ref, loss_ref, dy_ref, acc):
        i = pl.program_id(0)

        @pl.when(i == 0)
        def _():
            acc[...] = jnp.zeros_like(acc)

        err = y_ref[...] - t_ref[...]
        dy_ref[...] = err * (1.0 / dm)
        acc[...] += jnp.sum(err * err, axis=0, keepdims=True)

        @pl.when(i == s // ts - 1)
        def _():
            loss_ref[...] = jnp.broadcast_to(jnp.sum(acc[...], axis=-1, keepdims=True) * (0.5 / dm), (1, LANES))

    blk = pl.BlockSpec((ts, dm), lambda i: (i, 0))
    loss, dy = pl.pallas_call(
        body, name=name, grid=(s // ts,), in_specs=[blk, blk],
        out_specs=[pl.BlockSpec((1, LANES), lambda i: (0, 0)), blk],
        out_shape=[jax.ShapeDtypeStruct((1, LANES), F32), jax.ShapeDtypeStruct((s, dm), F32)],
        scratch_shapes=[pltpu.VMEM((1, dm), F32)], compiler_params=_params("arbitrary"))(y, target)
    return loss[0, 0], dy


def attn_fwd_c(name, qc, kvc, krc, scale):
    s = qc.shape[0]
    tq = ATTN_TQ

    def body(qn_ref, qr_ref, kn_ref, kr_ref, v_ref, o_ref, lse_ref):
        sc = (_dot(qn_ref[...], kn_ref[...], NT) + _dot(qr_ref[...], kr_ref[...], NT)) * scale
        m = jnp.max(sc, axis=-1, keepdims=True)
        e = jnp.exp(sc - m)
        l = jnp.sum(e, axis=-1, keepdims=True)
        o_ref[...] = _dot(e.astype(BF16), v_ref[...]) / l
        lse_ref[...] = jnp.broadcast_to(m + jnp.log(l), (tq, HEAD))

    qb = lambda off: pl.BlockSpec((tq, HEAD), lambda h, i: (i, off + h))
    kb = lambda f: pl.BlockSpec((s, HEAD), f)
    out_blk = pl.BlockSpec((tq, HEAD), lambda h, i: (i, h))
    return pl.pallas_call(
        body, name=name, grid=(C_HEADS, s // tq),
        in_specs=[qb(0), qb(C_HEADS), kb(lambda h, i: (0, 2 * h)), kb(lambda h, i: (0, 0)), kb(lambda h, i: (0, 2 * h + 1))],
        out_specs=[out_blk, out_blk], out_shape=[jax.ShapeDtypeStruct((s, C_W), F32)] * 2,
        compiler_params=_params("parallel", "parallel"))(qc, qc, kvc, krc, kvc)


def attn_bwd_c(name, qc, kvc, krc, lse, o, do, scale):
    s = qc.shape[0]
    tq = ATTN_TQ

    def body(qn_ref, qr_ref, kn_ref, kr_ref, v_ref, lse_ref, o_ref, do_ref, dqn_ref, dqr_ref, dkv_ref, dkr_ref):
        h, i = pl.program_id(0), pl.program_id(1)
        sc = (_dot(qn_ref[...], kn_ref[...], NT) + _dot(qr_ref[...], kr_ref[...], NT)) * scale
        pr = jnp.exp(sc - lse_ref[:, 0:1])
        dov = do_ref[...]
        delta = jnp.sum(dov.astype(F32) * o_ref[...], axis=-1, keepdims=True)
        dp = _dot(dov, v_ref[...], NT)
        ds = (pr * (dp - delta) * scale).astype(BF16)

        @pl.when(i == 0)
        def _():
            dkv_ref[...] = jnp.zeros_like(dkv_ref)

        @pl.when((i == 0) & (h == 0))
        def _():
            dkr_ref[...] = jnp.zeros_like(dkr_ref)

        dkv_ref[:, HEAD:] += _dot(pr.astype(BF16), dov, TN)
        dqn_ref[...] = _dot(ds, kn_ref[...])
        dqr_ref[...] = _dot(ds, kr_ref[...])
        dkv_ref[:, :HEAD] += _dot(ds, qn_ref[...], TN)
        dkr_ref[...] += _dot(ds, qr_ref[...], TN)

    qb = lambda off: pl.BlockSpec((tq, HEAD), lambda h, i: (i, off + h))
    kb = lambda f: pl.BlockSpec((s, HEAD), f)
    hq = pl.BlockSpec((tq, HEAD), lambda h, i: (i, h))
    kn_map, v_map, kr_map = (lambda h, i: (0, 2 * h)), (lambda h, i: (0, 2 * h + 1)), (lambda h, i: (0, 0))
    return pl.pallas_call(
        body, name=name, grid=(C_HEADS, s // tq),
        in_specs=[qb(0), qb(C_HEADS), kb(kn_map), kb(kr_map), kb(v_map), hq, hq, hq],
        out_specs=[hq, hq, pl.BlockSpec((s, 2 * HEAD), lambda h, i: (0, h)), kb(kr_map)],
        out_shape=[jax.ShapeDtypeStruct((s, C_W), F32)] * 2
        + [jax.ShapeDtypeStruct((s, 2 * C_W), F32), jax.ShapeDtypeStruct((s, HEAD), F32)],
        compiler_params=_params("arbitrary", "arbitrary"))(qc, qc, kvc, krc, kvc, lse, o, do)


def _layer_fwd(x, p, tabs):
    sv = _mixers_ab_fwd(x, p, tabs)
    x1 = _mixer_c_out_fwd(x, sv, p, tabs)
    return _ffn_fwd(x1, p, sv), sv


def _mixers_ab_fwd(x, p, tabs):
    sv = {"x": x}
    hn = rmsnorm_fwd("attn_norm", x, p["attn_norm"], x.shape[1], BF16)
    proj = matmul("in_proj", hn, p["w_in_t"], "nt", F32, tm=256)
    sv["hn"] = hn
    aq, ak = proj[:, O_AQ:O_AK], proj[:, O_AK:O_AV]
    av = proj[:, O_AV:O_BQ].astype(BF16)
    bq, bk = proj[:, O_BQ:O_BK], proj[:, O_BK:O_BV]
    cq, ckv = proj[:, O_CQ:O_CKV], proj[:, O_CKV:O_CKR]
    ckr = jnp.pad(proj[:, O_CKR:], ((0, 0), (0, HEAD - C_ROPE)))
    sv.update(aq=aq, ak=ak, av=av, cq=cq, ckv=ckv)

    qa = rope("a_q_rope", rmsnorm_fwd("a_q_norm", aq, p["a_q_norm"], HEAD, F32), tabs["a"][0], BF16)
    ka = rope("a_k_rope", rmsnorm_fwd("a_k_norm", ak, p["a_k_norm"], HEAD, F32), tabs["a"][0], BF16)
    ya, lse_a = attn_fwd("a_attn", [(qa, 0, ka, A_HEADS // A_KV)], av, A_HEADS // A_KV, A_HEADS, HEAD ** -0.5)
    sv.update(qa=qa, ka=ka, ya=ya, lse_a=lse_a)

    qb = rope("b_q_rope", bq, tabs["b"][0], F32)
    kb = rope("b_k_rope", bk, tabs["b"][0], F32)
    yb, lse_b = dilated_fwd("b_dilated", qb, kb, proj, O_BV // HEAD)
    sv.update(qb=qb, kb=kb, yb=yb, lse_b=lse_b, proj=proj, ckr=ckr)
    return sv


def _mixer_c_out_fwd(x, sv, p, tabs):
    cq, ckv, ckr = sv["cq"], sv["ckv"], sv.pop("ckr")
    ya, yb = sv["ya"], sv["yb"]
    cqn = rmsnorm_fwd("c_q_norm", cq, p["c_q_norm"], C_RANK, BF16)
    qc_raw = matmul("c_uq", cqn, p["w_uq_t"], "nt", F32)
    qc = jnp.concatenate([qc_raw[:, :C_W].astype(BF16), rope("c_q_rope", qc_raw[:, C_W:], tabs["c"][0], BF16)], axis=1)
    ckvn = rmsnorm_fwd("c_kv_norm", ckv, p["c_kv_norm"], C_RANK, BF16)
    kvc = matmul("c_ukv", ckvn, p["w_ukv"], "nn", BF16)
    krc = rope("c_k_rope", ckr, tabs["c"][0], BF16)
    c_scale = (HEAD + C_ROPE) ** -0.5
    yc, lse_c = attn_fwd_c("c_attn", qc, kvc, krc, c_scale)
    sv.update(cqn=cqn, ckvn=ckvn, qc=qc, kvc=kvc, krc=krc, yc=yc, lse_c=lse_c)

    g_a, g_b, g_c = p["out_norm"][:A_W], p["out_norm"][A_W:A_W + B_W], p["out_norm"][A_W + B_W:]
    y = jnp.concatenate([rmsnorm_fwd("out_norm_a", ya, g_a, A_W, BF16), rmsnorm_fwd("out_norm_b", yb, g_b, B_W, BF16),
                         rmsnorm_fwd("out_norm_c", yc, g_c, C_W, BF16)], axis=1)
    x1 = matmul("out_proj", y, p["w_out"], "nn", F32, add=x)
    sv.update(y=y, x1=x1)
    return x1


def _ffn_fwd(x1, p, sv):
    xn = rmsnorm_fwd("ffn_norm", x1, p["ffn_norm"], x1.shape[1], BF16)
    h, act = ffn_up("ffn_up", xn, p["w_up"], p["cwb"])
    x2 = matmul("ffn_down", act, p["w_down"], "nn", F32, add=x1)
    sv.update(xn=xn, h=h, act=act)
    return x2


def _layer_bwd(dx2, sv, p, tabs):
    dxn, g = _ffn_bwd(dx2, sv, p)
    dx1, dy, g_out = _out_proj_bwd(dxn, dx2, sv, p)
    dhn, g_mix = _mixers_bwd(dy, sv, p, tabs)
    dx, g_norm = _attn_norm_bwd(dhn, dx1, sv, p)
    return dx, {**g, **g_out, **g_mix, **g_norm}


def _ffn_bwd(dx2, sv, p, after=0.0):
    g = {}
    dx2b = (dx2 + after).astype(BF16)
    dact = matmul("ffn_down_dx", dx2b, p["w_down"], "nt", F32)
    g["w_down"] = matmul("ffn_down_dw", sv["act"], dx2b, "tn", F32)
    dh, g["cwb"] = ffn_gate_bwd("ffn_gate_bwd", sv["h"], dact, p["cwb"])
    dxn = ffn_dx("ffn_up_dx", dh, p["w_up"])
    g["w_up"] = ffn_dw_up("ffn_up_dw", sv["xn"], dh)
    return dxn, g


def _out_proj_bwd(dxn, dx2, sv, p, after=0.0):
    g = {}
    dm = dx2.shape[1]
    dx1, dx1b, g["ffn_norm"] = rmsnorm_bwd("ffn_norm_bwd", dxn, sv["x1"], p["ffn_norm"] + after, dm, (F32, BF16), add=dx2)
    dy = matmul("out_proj_dx", dx1b, p["w_out"], "nt", F32)
    g["w_out"] = matmul("out_proj_dw", sv["y"], dx1b, "tn", F32)
    return dx1, dy, g


def _attn_norm_bwd(dhn, dx1, sv, p, after=0.0):
    dx, d_gain = rmsnorm_bwd("attn_norm_bwd", dhn, sv["x"], p["attn_norm"] + after, dhn.shape[1], (F32,), add=dx1)
    return dx, {"attn_norm": d_gain}


def _mixers_bwd(dy, sv, p, tabs, after=0.0):
    g = {}
    out_norm = p["out_norm"] + after
    g_a, g_b, g_c = out_norm[:A_W], out_norm[A_W:A_W + B_W], out_norm[A_W + B_W:]
    dya, dga = rmsnorm_bwd("out_norm_a_bwd", dy[:, :A_W], sv["ya"], g_a, A_W, (BF16,))
    dyb, dgb = rmsnorm_bwd("out_norm_b_bwd", dy[:, A_W:A_W + B_W], sv["yb"], g_b, B_W, (F32,))
    dyc, dgc = rmsnorm_bwd("out_norm_c_bwd", dy[:, A_W + B_W:], sv["yc"], g_c, C_W, (BF16,))
    g["out_norm"] = jnp.concatenate([dga, dgb, dgc])

    c_scale = (HEAD + C_ROPE) ** -0.5
    dqn, dqr, dkv, dkr = attn_bwd_c("c_attn_bwd", sv["qc"], sv["kvc"], sv["krc"], sv["lse_c"], sv["yc"], dyc, c_scale)
    dqc = jnp.concatenate([dqn.astype(BF16), rope("c_q_rope_bwd", dqr, tabs["c"][1], BF16)], axis=1)
    dcqn = matmul("c_uq_dx", dqc, p["w_uq_t"], "nn", F32)
    g["w_uq"] = matmul("c_uq_dw", dqc, sv["cqn"], "tn", F32)
    dcq, g["c_q_norm"] = rmsnorm_bwd("c_q_norm_bwd", dcqn, sv["cq"], p["c_q_norm"], C_RANK, (BF16,))
    dkvc = dkv.astype(BF16)
    dckvn = matmul("c_ukv_dx", dkvc, p["w_ukv"], "nt", F32)
    g["w_ukv"] = matmul("c_ukv_dw", sv["ckvn"], dkvc, "tn", F32)
    dckv, g["c_kv_norm"] = rmsnorm_bwd("c_kv_norm_bwd", dckvn, sv["ckv"], p["c_kv_norm"], C_RANK, (BF16,))
    dckr = rope("c_k_rope_bwd", dkr, tabs["c"][1], BF16)

    dqb, dkb, dvb = dilated_bwd("b_dilated_bwd", sv["qb"], sv["kb"], sv["proj"], O_BV // HEAD, dyb, sv["yb"], sv["lse_b"])
    dbq = rope("b_q_rope_bwd", dqb, tabs["b"][1], BF16)
    dbk = rope("b_k_rope_bwd", dkb, tabs["b"][1], BF16)
    dbv = dvb.astype(BF16)

    kg = A_HEADS // A_KV
    (dqa,), (dka,), dva = attn_bwd("a_attn_bwd", [(sv["qa"], 0, sv["ka"], kg)], sv["av"], kg, sv["lse_a"], sv["ya"],
                                   dya, A_HEADS, HEAD ** -0.5)
    daq, g["a_q_norm"] = rmsnorm_bwd("a_q_norm_bwd", rope("a_q_rope_bwd", dqa, tabs["a"][1], F32), sv["aq"],
                                     p["a_q_norm"], HEAD, (BF16,))
    dak, g["a_k_norm"] = rmsnorm_bwd("a_k_norm_bwd", rope("a_k_rope_bwd", dka, tabs["a"][1], F32), sv["ak"],
                                     p["a_k_norm"], HEAD, (BF16,))

    dproj = jnp.concatenate([daq, dak, dva.astype(BF16), dbq, dbk, dbv, dcq, dckv, dckr[:, :C_ROPE]], axis=1)
    dhn = matmul("in_proj_dx", dproj, p["w_in_t"], "nn", F32, tm=512)
    g["w_in"] = matmul("in_proj_dw", dproj, sv["hn"], "tn", F32, tn=512, tk=512)
    return dhn, g


def _local_step(x, target, layers, final_norm, tabs):
    saved = []
    for p in layers:
        x, sv = _layer_fwd(x, p, tabs)
        saved.append(sv)
    dm = x.shape[1]
    yf = rmsnorm_fwd("final_norm", x, final_norm, dm, F32)
    loss, dyf = loss_head("loss_head", yf, target)
    dx, d_final = rmsnorm_bwd("final_norm_bwd", dyf, x, final_norm, dm, (F32,))
    grads = [None] * len(layers)
    for l in reversed(range(len(layers))):
        dx, grads[l] = _layer_bwd(dx, saved[l], layers[l], tabs)
    return loss, dx, grads, d_final


HBM = pl.BlockSpec(memory_space=pl.ANY)


def _place():
    x, y, c = lax.axis_index("x"), lax.axis_index("y"), lax.axis_index("c")
    others = [(1 - x, y), (x, 1 - y), (1 - x, 1 - y)]
    return x, y, c, 2 * x + y, others


def _remote(src, dst, send_sem, recv_sem, device):
    return pltpu.make_async_remote_copy(src_ref=src, dst_ref=dst, send_sem=send_sem, recv_sem=recv_sem,
                                        device_id=device, device_id_type=MESH)


def gather_small(name, small):
    def body(in_ref, out_ref, send_sems, recv_sems, local_sem):
        x, y, c, chip, others = _place()
        mine = pltpu.make_async_copy(in_ref, out_ref.at[chip], local_sem)
        mine.start()
        copies = []
        for k, (ox, oy) in enumerate(others):
            cp = _remote(in_ref, out_ref.at[chip], send_sems.at[k], recv_sems.at[k], (ox, oy, c))
            cp.start()
            copies.append(cp)
        for k, (ox, oy) in enumerate(others):
            landed = out_ref.at[2 * ox + oy]
            _remote(landed, landed, send_sems.at[k], recv_sems.at[k], (ox, oy, c)).wait_recv()
        for cp in copies:
            cp.wait_send()
        mine.wait()

    return pl.pallas_call(
        body, name=name, in_specs=[HBM], out_specs=HBM, out_shape=jax.ShapeDtypeStruct((4,) + small.shape, small.dtype),
        scratch_shapes=[pltpu.SemaphoreType.DMA((3,)), pltpu.SemaphoreType.DMA((3,)), pltpu.SemaphoreType.DMA],
        compiler_params=pltpu.CompilerParams(has_side_effects=True))(small)


IN_HBM = pl.BlockSpec(memory_space=pltpu.HBM)
SEMS = pl.BlockSpec(memory_space=pltpu.SEMAPHORE)
DATAFLOW = pltpu.SideEffectType.DATAFLOW_SIDE_EFFECTING


BF16_ROWS_PER_TILE = 16


def _halves_by_rows(shape):
    return (shape[-2] // 2) % BF16_ROWS_PER_TILE == 0


def _half(shape, core):
    if _halves_by_rows(shape):
        size = shape[-2] // 2
        return (pl.ds(core * size, size), slice(None))
    size = shape[-1] // 2
    return (slice(None), pl.ds(core * size, size))


def _half_shape(shape):
    r, c = shape[-2:]
    return (r // 2, c) if _halves_by_rows(shape) else (r, c // 2)


def _gather_plan(bufs):
    x, y, c, chip, others = _place()
    plan = []
    for t, ref in enumerate(bufs):
        mine = _half(ref.shape, c)
        for k, (ox, oy) in enumerate(others):
            plan.append((3 * t + k, (ox, oy, c), ref.at[(chip,) + mine], ref.at[(chip,) + mine],
                         ref.at[(2 * ox + oy,) + mine]))
    return plan


def _exchange_plan(bufs):
    x, y, c, chip, others = _place()
    n = len(bufs) // 2
    plan = []
    for t in range(n):
        for k, (ox, oy) in enumerate(others):
            plan.append((3 * t + k, (ox, oy, c), bufs[t].at[2 * ox + oy], bufs[n + t].at[chip], bufs[n + t].at[2 * ox + oy]))
    return plan


def _sibling_plan(bufs):
    x, y, c, chip, others = _place()
    n = len(bufs) // 2
    plan = []
    for t in range(n):
        plan.append((t, (x, y, 1 - c), bufs[t].at[(slice(None),) + _half(bufs[t].shape, 1 - c)], bufs[n + t], bufs[n + t]))
    return plan


PLAN_COPIES = {_gather_plan: lambda n: 3 * n, _exchange_plan: lambda n: 3 * (n // 2), _sibling_plan: lambda n: n // 2}


def split_start(name, bufs, plan_of, after):
    n = len(bufs)

    def body(*refs):
        ins = refs[:n]
        send_sems, recv_sems = refs[n + 1], refs[n + 2]
        token = refs[-1]
        for idx, peer, src, dst, _ in plan_of(ins):
            _remote(src, dst, send_sems.at[idx], recv_sems.at[idx], peer).start()
        token[...] = jnp.zeros_like(token)

    copies = PLAN_COPIES[plan_of](n)
    res = pl.pallas_call(
        body, name=name, in_specs=[IN_HBM] * n + [HBM],
        out_specs=(SEMS, SEMS) + (IN_HBM,) * n + (pl.BlockSpec(memory_space=pltpu.VMEM),),
        out_shape=(pltpu.SemaphoreType.DMA((copies,)), pltpu.SemaphoreType.DMA((copies,)))
        + tuple(pltpu.HBM(b.shape, b.dtype) for b in bufs) + (jax.ShapeDtypeStruct((8, LANES), F32),),
        input_output_aliases={t: 2 + t for t in range(n)},
        compiler_params=pltpu.CompilerParams(has_side_effects=DATAFLOW))(
            *[pltpu.with_memory_space_constraint(b, pltpu.HBM) for b in bufs], after)
    return res[0], res[1], list(res[2:2 + n]), res[-1]


def split_wait(name, send_sems, recv_sems, flying, plan_of, after):
    n = len(flying)

    def body(*refs):
        ins = refs[:n]
        send_ref, recv_ref = refs[n], refs[n + 1]
        for idx, peer, src, dst, landing in plan_of(ins):
            _remote(src, dst, send_ref.at[idx], recv_ref.at[idx], peer).wait_send()
            _remote(landing, landing, send_ref.at[idx], recv_ref.at[idx], peer).wait_recv()

    return list(pl.pallas_call(
        body, name=name, in_specs=[IN_HBM] * n + [SEMS, SEMS] + [HBM] * len(after), out_specs=(IN_HBM,) * n,
        out_shape=tuple(pltpu.HBM(b.shape, b.dtype) for b in flying),
        input_output_aliases={t: t for t in range(n)},
        compiler_params=pltpu.CompilerParams(has_side_effects=DATAFLOW))(*flying, send_sems, recv_sems, *after))


def forward_halves(name, slots):
    n = len(slots)

    def body(*refs):
        ins, outs = refs[:n], refs[n:2 * n]
        send_sems, recv_sems = refs[2 * n:]
        x, y, c, chip, others = _place()
        sibling = (x, y, 1 - c)
        copies = []
        for t in range(n):
            mine = _half(slots[t].shape, c)
            for k, (ox, oy) in enumerate(others):
                cp = _remote(ins[t].at[(2 * ox + oy,) + mine], outs[t].at[(2 * ox + oy,) + mine], send_sems.at[t, k],
                             recv_sems.at[t, k], sibling)
                cp.start()
                copies.append(cp)
        for t in range(n):
            for k, (ox, oy) in enumerate(others):
                theirs = outs[t].at[(2 * ox + oy,) + _half(slots[t].shape, 1 - c)]
                _remote(theirs, theirs, send_sems.at[t, k], recv_sems.at[t, k], sibling).wait_recv()
        for cp in copies:
            cp.wait_send()

    return list(pl.pallas_call(
        body, name=name, in_specs=[HBM] * n, out_specs=[HBM] * n,
        out_shape=[jax.ShapeDtypeStruct(s.shape, s.dtype) for s in slots],
        input_output_aliases={t: t for t in range(n)},
        scratch_shapes=[pltpu.SemaphoreType.DMA((n, 3))] * 2,
        compiler_params=pltpu.CompilerParams(has_side_effects=True))(*slots))


def _half_block(shape, tr):
    r, c = shape[-2:]
    if _halves_by_rows(shape):
        per = r // 2 // tr
        return tr, c, lambda i, core: (core * per + i, 0)
    return tr, c // 2, lambda i, core: (i, core)


def add_half(name, place, grad, theirs):
    hr, hc = _half_shape(grad.shape)
    tr = _rows(hr, hc)
    br, bc, at = _half_block(grad.shape, tr)

    def body(place_ref, g_ref, t_ref, o_ref):
        o_ref[...] = (g_ref[...] + t_ref[...]).astype(o_ref.dtype)

    whole = pl.BlockSpec((None, br, bc), lambda j, i, pr: (j, i, 0))
    return pl.pallas_call(
        body, name=name,
        grid_spec=pltpu.PrefetchScalarGridSpec(
            num_scalar_prefetch=1, grid=(4, hr // tr),
            in_specs=[pl.BlockSpec((None, br, bc), lambda j, i, pr: (j,) + at(i, pr[0])), whole],
            out_specs=whole),
        out_shape=jax.ShapeDtypeStruct((4, hr, hc), BF16),
        compiler_params=_params("parallel", "parallel"))(place, grad, theirs)


def sum_chips(name, place, own, parts, shard_shape):
    _, hr, hc = parts.shape
    tr = _rows(hr, hc)
    br, bc, at = _half_block(shard_shape, tr)

    def body(place_ref, own_ref, p1, p2, p3, o_ref):
        o_ref[...] = own_ref[...].astype(F32) + p1[...].astype(F32) + p2[...].astype(F32) + p3[...].astype(F32)

    def slot(k):
        return pl.BlockSpec((None, br, bc), lambda i, pr: (lax.rem(pr[1] + k, 4), i, 0))

    return pl.pallas_call(
        body, name=name,
        grid_spec=pltpu.PrefetchScalarGridSpec(
            num_scalar_prefetch=1, grid=(hr // tr,), in_specs=[slot(0), slot(1), slot(2), slot(3)],
            out_specs=pl.BlockSpec((br, bc), lambda i, pr: at(i, pr[0]))),
        out_shape=jax.ShapeDtypeStruct(tuple(shard_shape), F32),
        compiler_params=_params("parallel"))(place, own, parts, parts, parts)


def join_halves(name, halves):
    n = len(halves)

    def body(*refs):
        ins, outs = refs[:n], refs[n:2 * n]
        send_sems, recv_sems = refs[2 * n:]
        x, y, c, _, _ = _place()
        copies = []
        for t in range(n):
            mine = _half(halves[t].shape, c)
            cp = _remote(ins[t].at[mine], outs[t].at[mine], send_sems.at[t], recv_sems.at[t], (x, y, 1 - c))
            cp.start()
            copies.append(cp)
        for t in range(n):
            theirs = outs[t].at[_half(halves[t].shape, 1 - c)]
            _remote(theirs, theirs, send_sems.at[t], recv_sems.at[t], (x, y, 1 - c)).wait_recv()
        for cp in copies:
            cp.wait_send()

    return pl.pallas_call(
        body, name=name, in_specs=[HBM] * n, out_specs=[HBM] * n,
        out_shape=[jax.ShapeDtypeStruct(h.shape, h.dtype) for h in halves],
        input_output_aliases={t: t for t in range(n)},
        scratch_shapes=[pltpu.SemaphoreType.DMA((n,))] * 2,
        compiler_params=pltpu.CompilerParams(has_side_effects=True))(*halves)


def allreduce_small(name, buf):
    rows = buf.shape[0]

    def body(in_ref, out_ref, slots, send_sems, recv_sems):
        x, y, c, _, _ = _place()
        me = 4 * x + 2 * y + c
        slots[me] = in_ref[...]
        peers = []
        for k in range(1, 8):
            px = 1 - x if k & 4 else x
            py = 1 - y if k & 2 else y
            pc = 1 - c if k & 1 else c
            peers.append((px, py, pc))
        copies = []
        for k, peer in enumerate(peers):
            cp = _remote(in_ref, slots.at[me], send_sems.at[k], recv_sems.at[k], peer)
            cp.start()
            copies.append(cp)
        for k, (px, py, pc) in enumerate(peers):
            slot = slots.at[4 * px + 2 * py + pc]
            _remote(slot, slot, send_sems.at[k], recv_sems.at[k], (px, py, pc)).wait_recv()
        for cp in copies:
            cp.wait_send()
        acc = slots[0]
        for d in range(1, 8):
            acc = acc + slots[d]
        out_ref[...] = acc

    vm = pl.BlockSpec(memory_space=pltpu.VMEM)
    return pl.pallas_call(
        body, name=name, in_specs=[vm], out_specs=vm, out_shape=jax.ShapeDtypeStruct(buf.shape, F32),
        scratch_shapes=[pltpu.VMEM((8, rows, LANES), F32), pltpu.SemaphoreType.DMA((7,)), pltpu.SemaphoreType.DMA((7,))],
        compiler_params=pltpu.CompilerParams(has_side_effects=True, vmem_limit_bytes=VMEM_LIMIT))(buf)


def cast_layer(name, place, w, layer):
    _, r, cols = w.shape
    tr = _rows(r, cols)

    def body(place_ref, w_ref, o_ref):
        o_ref[...] = w_ref[...].astype(BF16)

    return pl.pallas_call(
        body, name=name,
        grid_spec=pltpu.PrefetchScalarGridSpec(
            num_scalar_prefetch=1, grid=(r // tr,),
            in_specs=[pl.BlockSpec((None, tr, cols), lambda i, pr: (layer, i, 0))],
            out_specs=pl.BlockSpec((None, tr, cols), lambda i, pr: (pr[1], i, 0))),
        out_shape=jax.ShapeDtypeStruct((4, r, cols), BF16),
        compiler_params=_params("parallel"))(place, w)


def _adamw_math(g, w, m, v):
    m = ADAM_B1 * m + (1.0 - ADAM_B1) * g
    v = ADAM_B2 * v + (1.0 - ADAM_B2) * (g * g)
    m_hat = m / (1.0 - ADAM_B1 ** ADAM_STEP)
    v_hat = v / (1.0 - ADAM_B2 ** ADAM_STEP)
    delta = -ADAM_LR * (m_hat / (jnp.sqrt(v_hat) + ADAM_EPS) + ADAM_WD * w)
    return delta, m, v


def adamw_layer(name, layer, g, w, m, v, prev=None):
    nl, r, cols = w.shape
    tr = _rows(r, cols, 8)

    def body(*refs):
        g_ref, w_ref, m_ref, v_ref = refs[:4]
        og, od, om, ov = refs[-4:]
        gv = g_ref[...]
        delta, m2, v2 = _adamw_math(gv, w_ref[...], m_ref[...], v_ref[...])
        og[...] = gv
        od[...] = delta
        om[...] = m2
        ov[...] = v2

    lay = pl.BlockSpec((None, tr, cols), lambda i: (layer, i, 0))
    ins = [g, w, m, v] + (list(prev) if prev is not None else [])
    return pl.pallas_call(
        body, name=name, grid=(r // tr,),
        in_specs=[pl.BlockSpec((tr, cols), lambda i: (i, 0)), lay, lay, lay] + ([HBM] * 4 if prev is not None else []),
        out_specs=[lay] * 4, out_shape=[jax.ShapeDtypeStruct((nl, r, cols), F32)] * 4,
        input_output_aliases={4 + k: k for k in range(4)} if prev is not None else {},
        compiler_params=_params("parallel"))(*ins)


def adamw_packed(name, g, w, m, v):
    def body(g_ref, w_ref, m_ref, v_ref, od, om, ov):
        delta, m2, v2 = _adamw_math(g_ref[...], w_ref[...], m_ref[...], v_ref[...])
        od[...] = delta
        om[...] = m2
        ov[...] = v2

    vm = pl.BlockSpec(memory_space=pltpu.VMEM)
    return pl.pallas_call(
        body, name=name, in_specs=[vm] * 4, out_specs=[vm] * 3, out_shape=[jax.ShapeDtypeStruct(g.shape, F32)] * 3,
        compiler_params=pltpu.CompilerParams(vmem_limit_bytes=VMEM_LIMIT))(g, w, m, v)


def _pack(arrays):
    flat = jnp.concatenate([a.reshape(-1) for a in arrays])
    pad = (-flat.shape[0]) % (8 * LANES)
    return jnp.pad(flat, (0, pad)).reshape(-1, LANES)


def _unpack(buf, shapes):
    flat = buf.reshape(-1)
    out, off = [], 0
    for shp in shapes:
        size = int(np.prod(shp))
        out.append(flat[off:off + size].reshape(shp))
        off += size
    return out


MIXER_W = ("w_in", "w_uq", "w_ukv", "w_out")
FFN_W = ("w_up", "w_down")
BIG = MIXER_W + FFN_W
TRANSPOSED = ("w_in", "w_uq")
COLUMN_CUT = ("w_ukv", "w_up")
SMALL = ("attn_norm", "a_q_norm", "a_k_norm", "c_q_norm", "c_kv_norm", "out_norm", "ffn_norm", "conv_b", "final_norm")
WEIGHTS = ("attn_norm", "w_in", "a_q_norm", "a_k_norm", "c_q_norm", "c_kv_norm", "w_uq", "w_ukv", "out_norm", "w_out",
           "ffn_norm", "w_up", "conv_w", "conv_b", "w_down", "final_norm")
INPUTS = ("x",) + WEIGHTS + ("loss_target",) + tuple("m_" + n for n in WEIGHTS) + tuple("v_" + n for n in WEIGHTS)


def _columns(g):
    return jnp.transpose(g, (1, 0, 2)).reshape(g.shape[1], 4 * g.shape[2])


def _uncolumns(w):
    r, c4 = w.shape
    return jnp.transpose(w.reshape(r, 4, c4 // 4), (1, 0, 2))


def _compute_layouts(full, conv_w, small, layer):
    p = {n: small[n][layer] for n in SMALL if n != "final_norm"}
    if "w_in" in full:
        p["w_in_t"] = full["w_in"].reshape(-1, full["w_in"].shape[2])
    if "w_uq" in full:
        uq_t = full["w_uq"].reshape(C_HEADS, HEAD + C_ROPE, C_RANK)
        p["w_uq_t"] = jnp.concatenate([uq_t[:, :HEAD].reshape(C_W, C_RANK),
                                       jnp.pad(uq_t[:, HEAD:], ((0, 0), (0, HEAD - C_ROPE), (0, 0))).reshape(C_W, C_RANK)], axis=0)
        p["w_ukv"] = _columns(full["w_ukv"])
        p["w_out"] = full["w_out"].reshape(-1, full["w_out"].shape[2])
    if "w_up" in full:
        up = full["w_up"]
        p["w_up"] = up.reshape(2, 2, up.shape[1], up.shape[2])
        p["w_down"] = full["w_down"].reshape(-1, full["w_down"].shape[2])
        fh = conv_w.shape[1] // 2
        taps = jnp.transpose(conv_w.reshape(3, 2, fh), (1, 0, 2))
        p["cwb"] = jnp.concatenate([taps, small["conv_b"][layer].reshape(2, 1, fh), jnp.zeros((2, 4, fh), F32)], axis=1)
    return p


def _shard_layout(name, g):
    if name == "w_in":
        return g.reshape(4, -1, g.shape[1])
    if name == "w_uq":
        uq_t = jnp.concatenate([g[:C_W].reshape(C_HEADS, HEAD, C_RANK),
                                g[C_W:].reshape(C_HEADS, HEAD, C_RANK)[:, :C_ROPE]], axis=1)
        return uq_t.reshape(4, -1, C_RANK)
    if name == "w_ukv":
        return _uncolumns(g)
    if name == "w_up":
        return g.reshape(4, g.shape[2], g.shape[3])
    return g.reshape(4, -1, g.shape[1])


def _conv_grads(dcwb):
    fh = dcwb.shape[2]
    return jnp.transpose(dcwb[:, 0:3, :], (1, 0, 2)).reshape(3, 2 * fh), dcwb[:, 3, :].reshape(2 * fh)


def _shard_layouts(g):
    return ({n: _shard_layout(n, g[n]) for n in BIG},) + _conv_grads(g["cwb"])


def kernel(x, attn_norm, w_in, a_q_norm, a_k_norm, c_q_norm, c_kv_norm, w_uq, w_ukv, out_norm, w_out, ffn_norm, w_up, conv_w, conv_b, w_down, final_norm, loss_target, m_attn_norm, m_w_in, m_a_q_norm, m_a_k_norm, m_c_q_norm, m_c_kv_norm, m_w_uq, m_w_ukv, m_out_norm, m_w_out, m_ffn_norm, m_w_up, m_conv_w, m_conv_b, m_w_down, m_final_norm, v_attn_norm, v_w_in, v_a_q_norm, v_a_k_norm, v_c_q_norm, v_c_kv_norm, v_w_uq, v_w_ukv, v_out_norm, v_w_out, v_ffn_norm, v_w_up, v_conv_w, v_conv_b, v_w_down, v_final_norm):
    a = dict(zip(INPUTS, (x, attn_norm, w_in, a_q_norm, a_k_norm, c_q_norm, c_kv_norm, w_uq, w_ukv, out_norm, w_out, ffn_norm, w_up, conv_w, conv_b, w_down, final_norm, loss_target, m_attn_norm, m_w_in, m_a_q_norm, m_a_k_norm, m_c_q_norm, m_c_kv_norm, m_w_uq, m_w_ukv, m_out_norm, m_w_out, m_ffn_norm, m_w_up, m_conv_w, m_conv_b, m_w_down, m_final_norm, v_attn_norm, v_w_in, v_a_q_norm, v_a_k_norm, v_c_q_norm, v_c_kv_norm, v_w_uq, v_w_ukv, v_out_norm, v_w_out, v_ffn_norm, v_w_up, v_conv_w, v_conv_b, v_w_down, v_final_norm)))
    nl = w_in.shape[0]
    seq = x.shape[1]
    for n in TRANSPOSED:
        for kind in ("", "m_", "v_"):
            a[kind + n] = jnp.swapaxes(a[kind + n], 1, 2)
    chip = 2 * lax.axis_index("x") + lax.axis_index("y")
    place = jnp.stack([lax.axis_index("c"), chip]).astype(jnp.int32)
    tabs = _rope_tables(seq)

    assert nl == 2
    quarter = conv_w.shape[2]
    flights = {}
    here = place
    for l in range(nl):
        for key, names in ((f"in{l}", ("w_in",)), (f"mix{l}", MIXER_W[1:]), (f"ffn{l}", FFN_W)):
            slots = [cast_layer(f"cast{l}_{n}", here, a[n], l) for n in names]
            send_sems, recv_sems, flying, token = split_start(f"gather_{key}_start", slots, _gather_plan, here)
            flights[key] = (names, send_sems, recv_sems, flying)
            here = place + token[0, 0].astype(jnp.int32)
    started = token[0, 0]
    taps = jnp.pad(conv_w, ((0, 0), (0, 8 - conv_w.shape[1]), (0, 0))).reshape(nl * 8, quarter) + started
    taps_full = gather_small("gather_taps", taps).reshape(4, nl, 8, quarter)
    conv_full = [jnp.transpose(taps_full[:, l, 0:3, :], (1, 0, 2)).reshape(3, 4 * quarter) for l in range(nl)]

    def land(key, after):
        names, send_sems, recv_sems, flying = flights[key]
        landed = split_wait(f"gather_{key}_wait", send_sems, recv_sems, flying, _gather_plan, after)
        return dict(zip(names, forward_halves(f"gather_{key}_forward", landed)))

    h = x[0]
    layers, saved = [], []
    for l in range(nl):
        p = _compute_layouts(land(f"in{l}", [taps_full if l == 0 else h]), conv_full[l], a, l)
        if l == 0:
            p["attn_norm"] = p["attn_norm"] + started
        sv = _mixers_ab_fwd(h, p, tabs)
        p.update(_compute_layouts(land(f"mix{l}", [sv["yb"]]), conv_full[l], a, l))
        x1 = _mixer_c_out_fwd(h, sv, p, tabs)
        p.update(_compute_layouts(land(f"ffn{l}", [x1]), conv_full[l], a, l))
        h = _ffn_fwd(x1, p, sv)
        layers.append(p)
        saved.append(sv)
    dm = h.shape[1]
    yf = rmsnorm_fwd("final_norm", h, final_norm, dm, F32)
    loss, dyf = loss_head("loss_head", yf, loss_target[0])
    loss = lax.psum(loss, ("x", "y", "c"))
    dx, d_final = rmsnorm_bwd("final_norm_bwd", dyf, h, final_norm, dm, (F32,))

    def swap_begin(tag, names, g):
        g_list = [_shard_layout(n, g[n]) for n in names]
        theirs = [lax.empty((4,) + _half_shape(s.shape), s.dtype) for s in g_list]
        send_sems, recv_sems, flying, token = split_start(f"{tag}_sibling_start", g_list + theirs, _sibling_plan, place)
        return (names, send_sems, recv_sems, flying), token[0, 0]

    def exchange_begin(tag, swap, after):
        names, send_sems, recv_sems, flying = swap
        landed = split_wait(f"{tag}_sibling_wait", send_sems, recv_sems, flying, _sibling_plan, after)
        g_list, theirs = landed[:len(names)], landed[len(names):]
        sums = [add_half(f"{tag}_add_{n}", place, gr, t) for n, gr, t in zip(names, g_list, theirs)]
        parts = [lax.empty(s.shape, s.dtype) for s in sums]
        send_sems, recv_sems, flying, token = split_start(f"{tag}_chips_start", sums + parts, _exchange_plan, place)
        return (names, [gr.shape[1:] for gr in g_list], send_sems, recv_sems, flying), token

    def exchange_end(tag, flight, after):
        names, shard_shapes, send_sems, recv_sems, flying = flight
        landed = split_wait(f"{tag}_chips_wait", send_sems, recv_sems, flying, _exchange_plan, after)
        sums, parts = landed[:len(names)], landed[len(names):]
        halves = [sum_chips(f"{tag}_sum_{n}", place, s, p, shp) for n, s, p, shp in zip(names, sums, parts, shard_shapes)]
        return dict(zip(names, join_halves(f"{tag}_join", halves)))

    grads, conv_grads, flights_f, flights_m = [None] * nl, [None] * nl, [None] * nl, [None] * nl
    after = 0.0
    for l in reversed(range(nl)):
        sv, p = saved[l], layers[l]
        dxn, g = _ffn_bwd(dx, sv, p, after)
        swap_f, after = swap_begin(f"reduce{l}f", FFN_W, g)
        dx1, dy, g_out = _out_proj_bwd(dxn, dx, sv, p, after)
        flights_f[l], token = exchange_begin(f"reduce{l}f", swap_f, [dy])
        dhn, g_mix = _mixers_bwd(dy, sv, p, tabs, token[0, 0])
        g.update(g_out)
        g.update(g_mix)
        swap_m, after = swap_begin(f"reduce{l}m", MIXER_W, g)
        dx, g_norm = _attn_norm_bwd(dhn, dx1, sv, p, after)
        g.update(g_norm)
        flights_m[l], token = exchange_begin(f"reduce{l}m", swap_m, [dx])
        after = token[0, 0]
        grads[l], conv_grads[l] = g, _conv_grads(g["cwb"])
    grad_x = dx

    out = {}
    reduced1 = exchange_end("reduce1f", flights_f[1], [token])
    for n in FFN_W:
        out[n] = adamw_layer(f"adamw1_{n}", 1, reduced1[n], a[n], a["m_" + n], a["v_" + n])
    reduced1 = exchange_end("reduce1m", flights_m[1], [out[n][3] for n in FFN_W])
    for n in MIXER_W:
        out[n] = adamw_layer(f"adamw1_{n}", 1, reduced1[n], a[n], a["m_" + n], a["v_" + n])

    small_g = {n: jnp.stack([grads[l][n] for l in range(nl)]) for n in SMALL if n not in ("conv_b", "final_norm")}
    small_g["conv_b"] = jnp.stack([cg[1] for cg in conv_grads])
    small_g["final_norm"] = d_final
    conv_w_g = jnp.stack([cg[0] for cg in conv_grads])
    shapes = [a[n].shape for n in SMALL] + [conv_w_g.shape]
    summed = _unpack(allreduce_small("reduce_small", _pack([small_g[n] for n in SMALL] + [conv_w_g])), shapes)
    small_g = dict(zip(SMALL, summed[:-1]))
    small_g["conv_w"] = lax.dynamic_slice_in_dim(summed[-1], chip * quarter, quarter, axis=2)
    names = SMALL + ("conv_w",)
    shapes = [a[n].shape for n in names]
    d_small, m_small, v_small = adamw_packed(
        "adamw_small", _pack([small_g[n] for n in names]), _pack([a[n] for n in names]),
        _pack([a["m_" + n] for n in names]), _pack([a["v_" + n] for n in names]))

    reduced0 = exchange_end("reduce0f", flights_f[0], [out[n][3] for n in BIG] + [d_small])
    for n in FFN_W:
        out[n] = adamw_layer(f"adamw0_{n}", 0, reduced0[n], a[n], a["m_" + n], a["v_" + n], prev=out[n])
    reduced0 = exchange_end("reduce0m", flights_m[0], [out[n][3] for n in FFN_W])
    for n in MIXER_W:
        out[n] = adamw_layer(f"adamw0_{n}", 0, reduced0[n], a[n], a["m_" + n], a["v_" + n], prev=out[n])
    for n, d_, m_, v_ in zip(names, _unpack(d_small, shapes), _unpack(m_small, shapes), _unpack(v_small, shapes)):
        out[n] = (small_g[n], d_, m_, v_)
    for n in TRANSPOSED:
        out[n] = [jnp.swapaxes(o, 1, 2) for o in out[n]]

    return (loss, grad_x[None]) + tuple(out[n][k] for k in range(4) for n in WEIGHTS)
```

```python
import functools

import jax
import jax.numpy as jnp
import numpy as np
from jax import lax
from jax.experimental import pallas as pl
from jax.experimental.pallas import tpu as pltpu

F32 = jnp.float32
BF16 = jnp.bfloat16
MESH = pl.DeviceIdType.MESH

V7X_VMEM_BYTES = 64 * 1024 * 1024
VMEM_LIMIT = V7X_VMEM_BYTES - 6 * 1024 * 1024
LANES = 128
ELEMENTWISE_BLOCK_BYTES = 2 * 1024 * 1024

HEAD = 128
A_HEADS, A_KV = 6, 2
B_HEADS, B_GROUPS = 4, 3
B_DILATIONS = (1, 4, 16)
B_HALF = 64
C_HEADS, C_RANK, C_ROPE = 6, 512, 64
GRID_W = 64
A_THETA, B_THETA, C_THETA = 10000.0, 500000.0, 10000.0
B_ROPE_DIM = 32
EPS = 1e-6
A_W, B_W, C_W = A_HEADS * HEAD, B_HEADS * HEAD, C_HEADS * HEAD
O_AQ, O_AK, O_AV = 0, 768, 1024
O_BQ, O_BK, O_BV = 1280, 2816, 3328
O_CQ, O_CKV, O_CKR = 3840, 4352, 4864
IN_W = 4928
PROJ_W = 5120

ADAM_LR, ADAM_B1, ADAM_B2, ADAM_EPS, ADAM_WD, ADAM_STEP = 0.001, 0.9, 0.999, 1e-08, 0.01, 10

NN = (((1,), (0,)), ((), ()))
NT = (((1,), (1,)), ((), ()))
TN = (((0,), (0,)), ((), ()))


def _dot(a, b, dims=NN):
    return lax.dot_general(a, b, dims, preferred_element_type=F32)


def _params(*sem):
    return pltpu.CompilerParams(dimension_semantics=sem if sem else None, vmem_limit_bytes=VMEM_LIMIT)


def _tile(n, target, unit=LANES):
    if n <= target:
        return n
    best = 0
    for t in range(unit, target + 1, unit):
        if n % t == 0:
            best = t
    return best if best else n


def _rows(r, width, itemsize=4):
    return _tile(r, max(8, ELEMENTWISE_BLOCK_BYTES // (width * itemsize)), 8)


def _matmul_call(name, dims, grid, a_spec, b_spec, o_spec, out_shape, acc_shape, add_spec=None):
    nk = grid[2]

    def body(*refs):
        a_ref, b_ref = refs[:2]
        add_ref = None if add_spec is None else refs[2]
        o_ref = refs[2 if add_spec is None else 3]

        def finish(r):
            if add_ref is not None:
                r = r + add_ref[...]
            o_ref[...] = r.astype(o_ref.dtype)

        if nk == 1:
            finish(_dot(a_ref[...], b_ref[...], dims))
            return
        acc = refs[-1]
        k = pl.program_id(2)

        @pl.when(k == 0)
        def _():
            acc[...] = _dot(a_ref[...], b_ref[...], dims)

        if nk > 2:
            @pl.when((k > 0) & (k < nk - 1))
            def _():
                acc[...] += _dot(a_ref[...], b_ref[...], dims)

        @pl.when(k == nk - 1)
        def _():
            finish(acc[...] + _dot(a_ref[...], b_ref[...], dims))

    in_specs = [a_spec, b_spec] + ([] if add_spec is None else [add_spec])
    return pl.pallas_call(
        body, name=name, grid=grid, in_specs=in_specs, out_specs=o_spec, out_shape=out_shape,
        scratch_shapes=[pltpu.VMEM(acc_shape, F32)] if nk > 1 else [],
        compiler_params=_params("parallel", "parallel", "arbitrary"))


def matmul(name, a, b, mode, out_dtype, add=None, tm=1024, tn=1024, tk=2816):
    if mode == "nn":
        (m, k), (k2, n) = a.shape, b.shape
    elif mode == "nt":
        (m, k), (n, k2) = a.shape, b.shape
    else:
        (k, m), (k2, n) = a.shape, b.shape
    assert k == k2, (name, a.shape, b.shape)
    tm, tn, tk = _tile(m, tm), _tile(n, tn), _tile(k, tk)
    grid = (m // tm, n // tn, k // tk)
    if mode == "tn":
        a_spec = pl.BlockSpec((tk, tm), lambda i, j, kk: (kk, i))
    else:
        a_spec = pl.BlockSpec((tm, tk), lambda i, j, kk: (i, kk))
    whole_b = {"pipeline_mode": pl.Buffered(1)} if grid[1] == 1 and grid[2] == 1 and grid[0] > 1 else {}
    if mode == "nt":
        b_spec = pl.BlockSpec((tn, tk), lambda i, j, kk: (j, kk), **whole_b)
    else:
        b_spec = pl.BlockSpec((tk, tn), lambda i, j, kk: (kk, j), **whole_b)
    o_spec = pl.BlockSpec((tm, tn), lambda i, j, kk: (i, j))
    dims = {"nn": NN, "nt": NT, "tn": TN}[mode]
    call = _matmul_call(name, dims, grid, a_spec, b_spec, o_spec, jax.ShapeDtypeStruct((m, n), out_dtype),
                        (tm, tn), None if add is None else o_spec)
    return call(a, b) if add is None else call(a, b, add)


def rmsnorm_fwd(name, x, g, width, out_dtype):
    r, cols = x.shape
    nb = cols // width
    tr = _rows(r, width)

    def body(x_ref, g_ref, o_ref):
        xv = x_ref[...]
        rs = lax.rsqrt(jnp.mean(xv * xv, axis=-1, keepdims=True) + EPS)
        o_ref[...] = (xv * rs * g_ref[...]).astype(o_ref.dtype)

    blk = pl.BlockSpec((tr, width), lambda i, j: (i, j))
    return pl.pallas_call(
        body, name=name, grid=(r // tr, nb),
        in_specs=[blk, pl.BlockSpec((1, width), lambda i, j: (0, 0))], out_specs=blk,
        out_shape=jax.ShapeDtypeStruct((r, cols), out_dtype),
        compiler_params=_params("parallel", "parallel"))(x, g.reshape(1, width))


def rmsnorm_bwd(name, dy, x, g, width, out_dtypes=(F32,), add=None):
    r, cols = x.shape
    nb = cols // width
    tr = _rows(r, width)
    n_out = len(out_dtypes)

    def body(*refs):
        dy_ref, x_ref, g_ref = refs[:3]
        add_ref = refs[3] if add is not None else None
        outs = refs[-(n_out + 1):-1]
        dg_ref = refs[-1]
        xv = x_ref[...]
        dyv = dy_ref[...].astype(F32)
        rs = lax.rsqrt(jnp.mean(xv * xv, axis=-1, keepdims=True) + EPS)
        xh = xv * rs
        dyg = dyv * g_ref[...]
        dx = rs * (dyg - xh * jnp.mean(dyg * xh, axis=-1, keepdims=True))
        if add_ref is not None:
            dx = dx + add_ref[...]
        for o in outs:
            o[...] = dx.astype(o.dtype)

        @pl.when((pl.program_id(0) == 0) & (pl.program_id(1) == 0))
        def _():
            dg_ref[...] = jnp.zeros_like(dg_ref)

        dg_ref[...] += jnp.sum(dyv * xh, axis=0, keepdims=True)

    blk = pl.BlockSpec((tr, width), lambda i, j: (i, j))
    vec = pl.BlockSpec((1, width), lambda i, j: (0, 0))
    ins = [dy, x, g.reshape(1, width)] + ([add] if add is not None else [])
    res = pl.pallas_call(
        body, name=name, grid=(r // tr, nb),
        in_specs=[blk, blk, vec] + ([blk] if add is not None else []),
        out_specs=[blk] * n_out + [vec],
        out_shape=[jax.ShapeDtypeStruct((r, cols), dt) for dt in out_dtypes] + [jax.ShapeDtypeStruct((1, width), F32)],
        compiler_params=_params("arbitrary", "arbitrary"))(*ins)
    return tuple(res[:n_out]) + (res[n_out].reshape(width),)


def _rope_angles(pos, dim, theta):
    inv = theta ** (-jnp.arange(0, dim, 2, dtype=F32) / dim)
    ang = pos.astype(F32)[:, None] * inv[None, :]
    return jnp.cos(ang), jnp.sin(ang)


def _rope_tables(s):
    rows = s // GRID_W
    row_pos = jnp.repeat(jnp.arange(rows), GRID_W)
    col_pos = jnp.tile(jnp.arange(GRID_W), rows)
    t_pos = jnp.arange(s)
    z = lambda n: jnp.zeros((s, n), F32)
    o = lambda n: jnp.ones((s, n), F32)
    cr, sr = _rope_angles(row_pos, HEAD // 2, A_THETA)
    cc, sc = _rope_angles(col_pos, HEAD // 2, A_THETA)
    tab_a = (jnp.concatenate([cr, cr, cc, cc], 1), jnp.concatenate([-sr, z(32), -sc, z(32)], 1),
             jnp.concatenate([z(32), sr, z(32), sc], 1), 32)
    cp, sp = _rope_angles(t_pos, B_ROPE_DIM, B_THETA)
    tab_b = (jnp.concatenate([cp, cp, o(96)], 1), jnp.concatenate([-sp, z(112)], 1),
             jnp.concatenate([z(16), sp, z(96)], 1), 16)
    cm, sm = _rope_angles(t_pos, C_ROPE, C_THETA)
    tab_c = (jnp.concatenate([cm, cm, o(64)], 1), jnp.concatenate([-sm, z(96)], 1),
             jnp.concatenate([z(32), sm, z(64)], 1), 32)

    def transposed(tab):
        c, s1, s2, h = tab
        return (c, jnp.roll(s2, -h, axis=1), jnp.roll(s1, h, axis=1), h)

    return {k: (t, transposed(t)) for k, t in (("a", tab_a), ("b", tab_b), ("c", tab_c))}


def rope(name, x, tab, out_dtype):
    c, s1, s2, h = tab
    s, cols = x.shape
    nh = cols // HEAD
    ts = _rows(s, HEAD)

    def body(x_ref, c_ref, s1_ref, s2_ref, o_ref):
        xv = x_ref[...].astype(F32)
        out = xv * c_ref[...] + pltpu.roll(xv, HEAD - h, 1) * s1_ref[...] + pltpu.roll(xv, h, 1) * s2_ref[...]
        o_ref[...] = out.astype(o_ref.dtype)

    blk = pl.BlockSpec((ts, HEAD), lambda i, j: (i, j))
    tb = pl.BlockSpec((ts, HEAD), lambda i, j: (i, 0))
    return pl.pallas_call(
        body, name=name, grid=(s // ts, nh), in_specs=[blk, tb, tb, tb], out_specs=blk,
        out_shape=jax.ShapeDtypeStruct((s, cols), out_dtype),
        compiler_params=_params("parallel", "parallel"))(x, c, s1, s2)


ATTN_TQ = 256


def attn_fwd(name, parts, v, v_group, nheads, scale):
    s = v.shape[0]
    tq = ATTN_TQ
    npart = len(parts)

    def body(*refs):
        v_ref, o_ref, lse_ref = refs[2 * npart:]
        sc = None
        for p in range(npart):
            t = _dot(refs[2 * p][...], refs[2 * p + 1][...], NT)
            sc = t if sc is None else sc + t
        sc = sc * scale
        m = jnp.max(sc, axis=-1, keepdims=True)
        e = jnp.exp(sc - m)
        l = jnp.sum(e, axis=-1, keepdims=True)
        o_ref[...] = _dot(e.astype(BF16), v_ref[...]) / l
        lse_ref[...] = jnp.broadcast_to(m + jnp.log(l), (tq, HEAD))

    in_specs, ins = [], []
    for q, qoff, k, kg in parts:
        in_specs.append(pl.BlockSpec((tq, HEAD), lambda h, i, qoff=qoff: (i, qoff + h)))
        in_specs.append(pl.BlockSpec((s, HEAD), lambda h, i, kg=kg: (0, h // kg)))
        ins += [q, k]
    in_specs.append(pl.BlockSpec((s, HEAD), lambda h, i: (0, h // v_group)))
    out_blk = pl.BlockSpec((tq, HEAD), lambda h, i: (i, h))
    return pl.pallas_call(
        body, name=name, grid=(nheads, s // tq), in_specs=in_specs, out_specs=[out_blk, out_blk],
        out_shape=[jax.ShapeDtypeStruct((s, nheads * HEAD), F32)] * 2,
        compiler_params=_params("parallel", "parallel"))(*ins, v)


def attn_bwd(name, parts, v, v_group, lse, o, do, nheads, scale):
    s = v.shape[0]
    tq = ATTN_TQ
    npart = len(parts)

    def body(*refs):
        v_ref, lse_ref, o_ref, do_ref = refs[2 * npart:2 * npart + 4]
        outs = refs[2 * npart + 4:]
        dq_refs, dk_refs, dv_ref = outs[:npart], outs[npart:2 * npart], outs[2 * npart]
        h, i = pl.program_id(0), pl.program_id(1)
        sc = None
        for p in range(npart):
            t = _dot(refs[2 * p][...], refs[2 * p + 1][...], NT)
            sc = t if sc is None else sc + t
        pr = jnp.exp(sc * scale - lse_ref[:, 0:1])
        dov = do_ref[...]
        delta = jnp.sum(dov.astype(F32) * o_ref[...], axis=-1, keepdims=True)
        dp = _dot(dov, v_ref[...], NT)
        ds = (pr * (dp - delta) * scale).astype(BF16)

        @pl.when((i == 0) & (h % v_group == 0))
        def _():
            dv_ref[...] = jnp.zeros_like(dv_ref)

        dv_ref[...] += _dot(pr.astype(BF16), dov, TN)
        for p in range(npart):
            kg = parts[p][3]
            dq_refs[p][...] = _dot(ds, refs[2 * p + 1][...])

            @pl.when((i == 0) & (h % kg == 0))
            def _(p=p):
                dk_refs[p][...] = jnp.zeros_like(dk_refs[p])

            dk_refs[p][...] += _dot(ds, refs[2 * p][...], TN)

    in_specs, ins = [], []
    for q, qoff, k, kg in parts:
        in_specs.append(pl.BlockSpec((tq, HEAD), lambda h, i, qoff=qoff: (i, qoff + h)))
        in_specs.append(pl.BlockSpec((s, HEAD), lambda h, i, kg=kg: (0, h // kg)))
        ins += [q, k]
    hq_blk = pl.BlockSpec((tq, HEAD), lambda h, i: (i, h))
    in_specs += [pl.BlockSpec((s, HEAD), lambda h, i: (0, h // v_group)), hq_blk, hq_blk, hq_blk]
    out_specs = [hq_blk] * npart
    out_shape = [jax.ShapeDtypeStruct((s, nheads * HEAD), F32)] * npart
    for q, qoff, k, kg in parts:
        out_specs.append(pl.BlockSpec((s, HEAD), lambda h, i, kg=kg: (0, h // kg)))
        out_shape.append(jax.ShapeDtypeStruct((s, nheads // kg * HEAD), F32))
    out_specs.append(pl.BlockSpec((s, HEAD), lambda h, i: (0, h // v_group)))
    out_shape.append(jax.ShapeDtypeStruct((s, nheads // v_group * HEAD), F32))
    res = pl.pallas_call(
        body, name=name, grid=(nheads, s // tq), in_specs=in_specs, out_specs=out_specs, out_shape=out_shape,
        compiler_params=_params("arbitrary", "arbitrary"))(*ins, v, lse, o, do)
    return list(res[:npart]), list(res[npart:2 * npart]), res[2 * npart]


def _band_windows(lf):
    for ib in range(lf // HEAD):
        q0 = ib * HEAD
        yield q0, max(0, q0 - B_HALF), min(lf, q0 + HEAD + B_HALF)


def _band_mask(q0, lo, hi):
    qpos = q0 + lax.broadcasted_iota(jnp.int32, (HEAD, hi - lo), 0)
    kpos = lo + lax.broadcasted_iota(jnp.int32, (HEAD, hi - lo), 1)
    return jnp.abs(qpos - kpos) <= B_HALF


def _class_rows(r, d, start, size):
    return pl.ds(r + d * start, size, stride=d) if d > 1 else pl.ds(start, size)


def dilated_fwd(name, q, k, v_src, v_block):
    s = k.shape[0]
    scale = HEAD ** -0.5

    def body(q_ref, k_ref, v_ref, y_ref, lse_ref, o_scr, l_scr):
        g_now = pl.program_id(1)
        for g, d in enumerate(B_DILATIONS):
            @pl.when(g_now == g)
            def _(g=g, d=d):
                for r in range(d):
                    for q0, lo, hi in _band_windows(s // d):
                        mine = _class_rows(r, d, q0, HEAD)
                        keys = _class_rows(r, d, lo, hi - lo)
                        sc = _dot(q_ref[mine, :].astype(BF16), k_ref[keys, :].astype(BF16), NT) * scale
                        sc = jnp.where(_band_mask(q0, lo, hi), sc, -1e30)
                        m = jnp.max(sc, axis=-1, keepdims=True)
                        e = jnp.exp(sc - m)
                        l = jnp.sum(e, axis=-1, keepdims=True)
                        o_scr.at[g][mine, :] = _dot((e / l).astype(BF16), v_ref[keys, :].astype(BF16))
                        l_scr.at[g][mine, :] = jnp.broadcast_to(m + jnp.log(l), (HEAD, HEAD))

        @pl.when(g_now == B_GROUPS - 1)
        def _():
            a, b, c = l_scr[0], l_scr[1], l_scr[2]
            m = jnp.maximum(jnp.maximum(a, b), c)
            ea, eb, ec = jnp.exp(a - m), jnp.exp(b - m), jnp.exp(c - m)
            den = ea + eb + ec
            y_ref[...] = (ea / den) * o_scr[0] + (eb / den) * o_scr[1] + (ec / den) * o_scr[2]
            lse_ref[...] = m + jnp.log(den)

    blk = lambda f: pl.BlockSpec((s, HEAD), f)
    per_head = blk(lambda h, g: (0, h))
    return pl.pallas_call(
        body, name=name, grid=(B_HEADS, B_GROUPS),
        in_specs=[blk(lambda h, g: (0, B_HEADS * g + h)), per_head, blk(lambda h, g: (0, v_block + h))],
        out_specs=[per_head, per_head], out_shape=[jax.ShapeDtypeStruct((s, B_W), F32)] * 2,
        scratch_shapes=[pltpu.VMEM((B_GROUPS, s, HEAD), F32)] * 2,
        compiler_params=_params("parallel", "arbitrary"))(q, k, v_src)


def dilated_bwd(name, q, k, v_src, v_block, dy, y, lse):
    s = k.shape[0]
    scale = HEAD ** -0.5

    def body(q_ref, k_ref, v_ref, dy_ref, y_ref, lse_ref, dq_ref, dk_ref, dv_ref, delta):
        g_now = pl.program_id(1)

        @pl.when(g_now == 0)
        def _():
            dk_ref[...] = jnp.zeros_like(dk_ref)
            dv_ref[...] = jnp.zeros_like(dv_ref)
            delta[...] = jnp.broadcast_to(jnp.sum(dy_ref[...] * y_ref[...], axis=-1, keepdims=True), (s, HEAD))

        for g, d in enumerate(B_DILATIONS):
            @pl.when(g_now == g)
            def _(d=d):
                for r in range(d):
                    for q0, lo, hi in _band_windows(s // d):
                        mine = _class_rows(r, d, q0, HEAD)
                        keys = _class_rows(r, d, lo, hi - lo)
                        qv, kv = q_ref[mine, :].astype(BF16), k_ref[keys, :].astype(BF16)
                        dyv = dy_ref[mine, :].astype(BF16)
                        sc = _dot(qv, kv, NT) * scale
                        pr = jnp.where(_band_mask(q0, lo, hi), jnp.exp(sc - lse_ref[mine, :][:, 0:1]), 0.0)
                        dp = _dot(dyv, v_ref[keys, :].astype(BF16), NT)
                        ds = (pr * (dp - delta[mine, :][:, 0:1]) * scale).astype(BF16)
                        dv_ref[keys, :] = dv_ref[keys, :] + _dot(pr.astype(BF16), dyv, TN)
                        dq_ref[mine, :] = _dot(ds, kv)
                        dk_ref[keys, :] = dk_ref[keys, :] + _dot(ds, qv, TN)

    blk = lambda f: pl.BlockSpec((s, HEAD), f)
    per_head = blk(lambda h, g: (0, h))
    by_group = blk(lambda h, g: (0, B_HEADS * g + h))
    return pl.pallas_call(
        body, name=name, grid=(B_HEADS, B_GROUPS),
        in_specs=[by_group, per_head, blk(lambda h, g: (0, v_block + h)), per_head, per_head, per_head],
        out_specs=[by_group, per_head, per_head],
        out_shape=[jax.ShapeDtypeStruct((s, B_GROUPS * B_W), F32)] + [jax.ShapeDtypeStruct((s, B_W), F32)] * 2,
        scratch_shapes=[pltpu.VMEM((s, HEAD), F32)],
        compiler_params=_params("parallel", "arbitrary"))(q, k, v_src, dy, y, lse)


FFN_TN = 256


def _conv(h, cw):
    s = h.shape[0]
    row = lax.broadcasted_iota(jnp.int32, h.shape, 0)
    prev = jnp.where(row == 0, 0.0, pltpu.roll(h, 1, 0))
    nxt = jnp.where(row == s - 1, 0.0, pltpu.roll(h, s - 1, 0))
    return prev * cw[0:1, :] + h * cw[1:2, :] + nxt * cw[2:3, :] + cw[3:4, :], prev, nxt


def _sigmoid(x):
    return 1.0 / (1.0 + jnp.exp(-x))


def ffn_up(name, xn, w_up, cwb):
    s, dm = xn.shape
    quarter = w_up.shape[3]
    fh = 2 * quarter
    tn = FFN_TN
    per = quarter // tn

    def body(x_ref, w_ref, cw_ref, h_ref, act_ref):
        xv = x_ref[...]
        hg = _dot(xv, w_ref[0])
        hu = _dot(xv, w_ref[1])
        h_ref[0] = hg
        h_ref[1] = hu
        gc, _, _ = _conv(hg, cw_ref[0])
        uc, _, _ = _conv(hu, cw_ref[1])
        act_ref[...] = (gc * _sigmoid(gc) * uc).astype(BF16)

    return pl.pallas_call(
        body, name=name, grid=(fh // tn,),
        in_specs=[pl.BlockSpec((s, dm), lambda t: (0, 0), pipeline_mode=pl.Buffered(1)),
                  pl.BlockSpec((2, None, dm, tn), lambda t: (0, t // per, 0, t % per)),
                  pl.BlockSpec((2, 8, tn), lambda t: (0, 0, t))],
        out_specs=[pl.BlockSpec((2, s, tn), lambda t: (0, 0, t)), pl.BlockSpec((s, tn), lambda t: (0, t))],
        out_shape=[jax.ShapeDtypeStruct((2, s, fh), F32), jax.ShapeDtypeStruct((s, fh), BF16)],
        compiler_params=_params("parallel"))(xn, w_up, cwb)


def ffn_gate_bwd(name, h, dact, cwb):
    _, s, fh = h.shape
    tn = FFN_TN

    def body(h_ref, da_ref, cw_ref, dh_ref, dcw_ref):
        gc, gp, gn = _conv(h_ref[0], cw_ref[0])
        uc, up, un = _conv(h_ref[1], cw_ref[1])
        sg = _sigmoid(gc)
        da = da_ref[...]
        dgc = da * uc * (sg * (1.0 + gc * (1.0 - sg)))
        duc = da * (gc * sg)
        row = lax.broadcasted_iota(jnp.int32, da.shape, 0)
        for idx, (hv, prev, nxt, dc) in enumerate(((h_ref[0], gp, gn, dgc), (h_ref[1], up, un, duc))):
            cw = cw_ref[idx]
            from_next = jnp.where(row == s - 1, 0.0, pltpu.roll(dc, s - 1, 0))
            from_prev = jnp.where(row == 0, 0.0, pltpu.roll(dc, 1, 0))
            dh_ref[idx] = (from_next * cw[0:1, :] + dc * cw[1:2, :] + from_prev * cw[2:3, :]).astype(BF16)
            dcw_ref[idx, 0:1, :] = jnp.sum(prev * dc, axis=0, keepdims=True)
            dcw_ref[idx, 1:2, :] = jnp.sum(hv * dc, axis=0, keepdims=True)
            dcw_ref[idx, 2:3, :] = jnp.sum(nxt * dc, axis=0, keepdims=True)
            dcw_ref[idx, 3:4, :] = jnp.sum(dc, axis=0, keepdims=True)
            dcw_ref[idx, 4:8, :] = jnp.zeros((4, tn), F32)

    return pl.pallas_call(
        body, name=name, grid=(fh // tn,),
        in_specs=[pl.BlockSpec((2, s, tn), lambda t: (0, 0, t)), pl.BlockSpec((s, tn), lambda t: (0, t)),
                  pl.BlockSpec((2, 8, tn), lambda t: (0, 0, t))],
        out_specs=[pl.BlockSpec((2, s, tn), lambda t: (0, 0, t)), pl.BlockSpec((2, 8, tn), lambda t: (0, 0, t))],
        out_shape=[jax.ShapeDtypeStruct((2, s, fh), BF16), jax.ShapeDtypeStruct((2, 8, fh), F32)],
        compiler_params=_params("parallel"))(h, dact, cwb)


def ffn_dx(name, dh, w_up):
    _, s, fh = dh.shape
    dm, quarter = w_up.shape[2], w_up.shape[3]
    tk = _tile(quarter, 2816)
    per = quarter // tk
    tm, tn = _tile(s, 1024), _tile(dm, 1024)
    grid = (s // tm, dm // tn, 4 * per)
    a_spec = pl.BlockSpec((None, tm, tk), lambda i, j, k: (k // (2 * per), i, k % (2 * per)))
    b_spec = pl.BlockSpec((None, None, tn, tk), lambda i, j, k: (k // (2 * per), (k // per) % 2, j, k % per))
    o_spec = pl.BlockSpec((tm, tn), lambda i, j, k: (i, j))
    return _matmul_call(name, NT, grid, a_spec, b_spec, o_spec, jax.ShapeDtypeStruct((s, dm), F32), (tm, tn))(dh, w_up)


def ffn_dw_up(name, xn, dh):
    _, s, fh = dh.shape
    dm = xn.shape[1]
    quarter = fh // 2
    tn = _tile(quarter, 1536)
    per = quarter // tn
    tm, tk = _tile(dm, 1024), _tile(s, 2048)
    grid = (dm // tm, 4 * per, s // tk)
    a_spec = pl.BlockSpec((tk, tm), lambda i, j, k: (k, i))
    b_spec = pl.BlockSpec((None, tk, tn), lambda i, j, k: (j // (2 * per), k, j % (2 * per)))
    o_spec = pl.BlockSpec((None, None, tm, tn), lambda i, j, k: (j // (2 * per), (j // per) % 2, i, j % per))
    return _matmul_call(name, TN, grid, a_spec, b_spec, o_spec,
                        jax.ShapeDtypeStruct((2, 2, dm, quarter), F32), (tm, tn))(xn, dh)


def loss_head(name, y, target):
    s, dm = y.shape
    ts = _rows(s, dm)

    def body(y_ref, t_ref, loss_ref, dy_ref, acc):
        i = pl.program_id(0)

        @pl.when(i == 0)
        def _():
            acc[...] = jnp.zeros_like(acc)

        err = y_ref[...] - t_ref[...]
        dy_ref[...] = err * (1.0 / dm)
        acc[...] += jnp.sum(err * err, axis=0, keepdims=True)

        @pl.when(i == s // ts - 1)
        def _():
            loss_ref[...] = jnp.broadcast_to(jnp.sum(acc[...], axis=-1, keepdims=True) * (0.5 / dm), (1, LANES))

    blk = pl.BlockSpec((ts, dm), lambda i: (i, 0))
    loss, dy = pl.pallas_call(
        body, name=name, grid=(s // ts,), in_specs=[blk, blk],
        out_specs=[pl.BlockSpec((1, LANES), lambda i: (0, 0)), blk],
        out_shape=[jax.ShapeDtypeStruct((1, LANES), F32), jax.ShapeDtypeStruct((s, dm), F32)],
        scratch_shapes=[pltpu.VMEM((1, dm), F32)], compiler_params=_params("arbitrary"))(y, target)
    return loss[0, 0], dy


def attn_fwd_c(name, qc, kvc, krc, scale):
    s = qc.shape[0]
    tq = ATTN_TQ

    def body(qn_ref, qr_ref, kn_ref, kr_ref, v_ref, o_ref, lse_ref):
        sc = (_dot(qn_ref[...], kn_ref[...], NT) + _dot(qr_ref[...], kr_ref[...], NT)) * scale
        m = jnp.max(sc, axis=-1, keepdims=True)
        e = jnp.exp(sc - m)
        l = jnp.sum(e, axis=-1, keepdims=True)
        o_ref[...] = _dot(e.astype(BF16), v_ref[...]) / l
        lse_ref[...] = jnp.broadcast_to(m + jnp.log(l), (tq, HEAD))

    qb = lambda off: pl.BlockSpec((tq, HEAD), lambda h, i: (i, off + h))
    kb = lambda f: pl.BlockSpec((s, HEAD), f)
    out_blk = pl.BlockSpec((tq, HEAD), lambda h, i: (i, h))
    return pl.pallas_call(
        body, name=name, grid=(C_HEADS, s // tq),
        in_specs=[qb(0), qb(C_HEADS), kb(lambda h, i: (0, 2 * h)), kb(lambda h, i: (0, 0)), kb(lambda h, i: (0, 2 * h + 1))],
        out_specs=[out_blk, out_blk], out_shape=[jax.ShapeDtypeStruct((s, C_W), F32)] * 2,
        compiler_params=_params("parallel", "parallel"))(qc, qc, kvc, krc, kvc)


def attn_bwd_c(name, qc, kvc, krc, lse, o, do, scale):
    s = qc.shape[0]
    tq = ATTN_TQ

    def body(qn_ref, qr_ref, kn_ref, kr_ref, v_ref, lse_ref, o_ref, do_ref, dqn_ref, dqr_ref, dkv_ref, dkr_ref):
        h, i = pl.program_id(0), pl.program_id(1)
        sc = (_dot(qn_ref[...], kn_ref[...], NT) + _dot(qr_ref[...], kr_ref[...], NT)) * scale
        pr = jnp.exp(sc - lse_ref[:, 0:1])
        dov = do_ref[...]
        delta = jnp.sum(dov.astype(F32) * o_ref[...], axis=-1, keepdims=True)
        dp = _dot(dov, v_ref[...], NT)
        ds = (pr * (dp - delta) * scale).astype(BF16)

        @pl.when(i == 0)
        def _():
            dkv_ref[...] = jnp.zeros_like(dkv_ref)

        @pl.when((i == 0) & (h == 0))
        def _():
            dkr_ref[...] = jnp.zeros_like(dkr_ref)

        dkv_ref[:, HEAD:] += _dot(pr.astype(BF16), dov, TN)
        dqn_ref[...] = _dot(ds, kn_ref[...])
        dqr_ref[...] = _dot(ds, kr_ref[...])
        dkv_ref[:, :HEAD] += _dot(ds, qn_ref[...], TN)
        dkr_ref[...] += _dot(ds, qr_ref[...], TN)

    qb = lambda off: pl.BlockSpec((tq, HEAD), lambda h, i: (i, off + h))
    kb = lambda f: pl.BlockSpec((s, HEAD), f)
    hq = pl.BlockSpec((tq, HEAD), lambda h, i: (i, h))
    kn_map, v_map, kr_map = (lambda h, i: (0, 2 * h)), (lambda h, i: (0, 2 * h + 1)), (lambda h, i: (0, 0))
    return pl.pallas_call(
        body, name=name, grid=(C_HEADS, s // tq),
        in_specs=[qb(0), qb(C_HEADS), kb(kn_map), kb(kr_map), kb(v_map), hq, hq, hq],
        out_specs=[hq, hq, pl.BlockSpec((s, 2 * HEAD), lambda h, i: (0, h)), kb(kr_map)],
        out_shape=[jax.ShapeDtypeStruct((s, C_W), F32)] * 2
        + [jax.ShapeDtypeStruct((s, 2 * C_W), F32), jax.ShapeDtypeStruct((s, HEAD), F32)],
        compiler_params=_params("arbitrary", "arbitrary"))(qc, qc, kvc, krc, kvc, lse, o, do)


def _layer_fwd(x, p, tabs):
    sv = _mixers_ab_fwd(x, p, tabs)
    x1 = _mixer_c_out_fwd(x, sv, p, tabs)
    return _ffn_fwd(x1, p, sv), sv


def _mixers_ab_fwd(x, p, tabs):
    sv = {"x": x}
    hn = rmsnorm_fwd("attn_norm", x, p["attn_norm"], x.shape[1], BF16)
    proj = matmul("in_proj", hn, p["w_in_t"], "nt", F32, tm=256)
    sv["hn"] = hn
    aq, ak = proj[:, O_AQ:O_AK], proj[:, O_AK:O_AV]
    av = proj[:, O_AV:O_BQ].astype(BF16)
    bq, bk = proj[:, O_BQ:O_BK], proj[:, O_BK:O_BV]
    cq, ckv = proj[:, O_CQ:O_CKV], proj[:, O_CKV:O_CKR]
    ckr = jnp.pad(proj[:, O_CKR:], ((0, 0), (0, HEAD - C_ROPE)))
    sv.update(aq=aq, ak=ak, av=av, cq=cq, ckv=ckv)

    qa = rope("a_q_rope", rmsnorm_fwd("a_q_norm", aq, p["a_q_norm"], HEAD, F32), tabs["a"][0], BF16)
    ka = rope("a_k_rope", rmsnorm_fwd("a_k_norm", ak, p["a_k_norm"], HEAD, F32), tabs["a"][0], BF16)
    ya, lse_a = attn_fwd("a_attn", [(qa, 0, ka, A_HEADS // A_KV)], av, A_HEADS // A_KV, A_HEADS, HEAD ** -0.5)
    sv.update(qa=qa, ka=ka, ya=ya, lse_a=lse_a)

    qb = rope("b_q_rope", bq, tabs["b"][0], F32)
    kb = rope("b_k_rope", bk, tabs["b"][0], F32)
    yb, lse_b = dilated_fwd("b_dilated", qb, kb, proj, O_BV // HEAD)
    sv.update(qb=qb, kb=kb, yb=yb, lse_b=lse_b, proj=proj, ckr=ckr)
    return sv


def _mixer_c_out_fwd(x, sv, p, tabs):
    cq, ckv, ckr = sv["cq"], sv["ckv"], sv.pop("ckr")
    ya, yb = sv["ya"], sv["yb"]
    cqn = rmsnorm_fwd("c_q_norm", cq, p["c_q_norm"], C_RANK, BF16)
    qc_raw = matmul("c_uq", cqn, p["w_uq_t"], "nt", F32)
    qc = jnp.concatenate([qc_raw[:, :C_W].astype(BF16), rope("c_q_rope", qc_raw[:, C_W:], tabs["c"][0], BF16)], axis=1)
    ckvn = rmsnorm_fwd("c_kv_norm", ckv, p["c_kv_norm"], C_RANK, BF16)
    kvc = matmul("c_ukv", ckvn, p["w_ukv"], "nn", BF16)
    krc = rope("c_k_rope", ckr, tabs["c"][0], BF16)
    c_scale = (HEAD + C_ROPE) ** -0.5
    yc, lse_c = attn_fwd_c("c_attn", qc, kvc, krc, c_scale)
    sv.update(cqn=cqn, ckvn=ckvn, qc=qc, kvc=kvc, krc=krc, yc=yc, lse_c=lse_c)

    g_a, g_b, g_c = p["out_norm"][:A_W], p["out_norm"][A_W:A_W + B_W], p["out_norm"][A_W + B_W:]
    y = jnp.concatenate([rmsnorm_fwd("out_norm_a", ya, g_a, A_W, BF16), rmsnorm_fwd("out_norm_b", yb, g_b, B_W, BF16),
                         rmsnorm_fwd("out_norm_c", yc, g_c, C_W, BF16)], axis=1)
    x1 = matmul("out_proj", y, p["w_out"], "nn", F32, add=x)
    sv.update(y=y, x1=x1)
    return x1


def _ffn_fwd(x1, p, sv):
    xn = rmsnorm_fwd("ffn_norm", x1, p["ffn_norm"], x1.shape[1], BF16)
    h, act = ffn_up("ffn_up", xn, p["w_up"], p["cwb"])
    x2 = matmul("ffn_down", act, p["w_down"], "nn", F32, add=x1)
    sv.update(xn=xn, h=h, act=act)
    return x2


def _layer_bwd(dx2, sv, p, tabs):
    dxn, g = _ffn_bwd(dx2, sv, p)
    dx1, dy, g_out = _out_proj_bwd(dxn, dx2, sv, p)
    dhn, g_mix = _mixers_bwd(dy, sv, p, tabs)
    dx, g_norm = _attn_norm_bwd(dhn, dx1, sv, p)
    return dx, {**g, **g_out, **g_mix, **g_norm}


def _ffn_bwd(dx2, sv, p, after=0.0):
    g = {}
    dx2b = (dx2 + after).astype(BF16)
    dact = matmul("ffn_down_dx", dx2b, p["w_down"], "nt", F32)
    g["w_down"] = matmul("ffn_down_dw", sv["act"], dx2b, "tn", F32)
    dh, g["cwb"] = ffn_gate_bwd("ffn_gate_bwd", sv["h"], dact, p["cwb"])
    dxn = ffn_dx("ffn_up_dx", dh, p["w_up"])
    g["w_up"] = ffn_dw_up("ffn_up_dw", sv["xn"], dh)
    return dxn, g


def _out_proj_bwd(dxn, dx2, sv, p, after=0.0):
    g = {}
    dm = dx2.shape[1]
    dx1, dx1b, g["ffn_norm"] = rmsnorm_bwd("ffn_norm_bwd", dxn, sv["x1"], p["ffn_norm"] + after, dm, (F32, BF16), add=dx2)
    dy = matmul("out_proj_dx", dx1b, p["w_out"], "nt", F32)
    g["w_out"] = matmul("out_proj_dw", sv["y"], dx1b, "tn", F32)
    return dx1, dy, g


def _attn_norm_bwd(dhn, dx1, sv, p, after=0.0):
    dx, d_gain = rmsnorm_bwd("attn_norm_bwd", dhn, sv["x"], p["attn_norm"] + after, dhn.shape[1], (F32,), add=dx1)
    return dx, {"attn_norm": d_gain}


def _mixers_bwd(dy, sv, p, tabs, after=0.0):
    g = {}
    out_norm = p["out_norm"] + after
    g_a, g_b, g_c = out_norm[:A_W], out_norm[A_W:A_W + B_W], out_norm[A_W + B_W:]
    dya, dga = rmsnorm_bwd("out_norm_a_bwd", dy[:, :A_W], sv["ya"], g_a, A_W, (BF16,))
    dyb, dgb = rmsnorm_bwd("out_norm_b_bwd", dy[:, A_W:A_W + B_W], sv["yb"], g_b, B_W, (F32,))
    dyc, dgc = rmsnorm_bwd("out_norm_c_bwd", dy[:, A_W + B_W:], sv["yc"], g_c, C_W, (BF16,))
    g["out_norm"] = jnp.concatenate([dga, dgb, dgc])

    c_scale = (HEAD + C_ROPE) ** -0.5
    dqn, dqr, dkv, dkr = attn_bwd_c("c_attn_bwd", sv["qc"], sv["kvc"], sv["krc"], sv["lse_c"], sv["yc"], dyc, c_scale)
    dqc = jnp.concatenate([dqn.astype(BF16), rope("c_q_rope_bwd", dqr, tabs["c"][1], BF16)], axis=1)
    dcqn = matmul("c_uq_dx", dqc, p["w_uq_t"], "nn", F32)
    g["w_uq"] = matmul("c_uq_dw", dqc, sv["cqn"], "tn", F32)
    dcq, g["c_q_norm"] = rmsnorm_bwd("c_q_norm_bwd", dcqn, sv["cq"], p["c_q_norm"], C_RANK, (BF16,))
    dkvc = dkv.astype(BF16)
    dckvn = matmul("c_ukv_dx", dkvc, p["w_ukv"], "nt", F32)
    g["w_ukv"] = matmul("c_ukv_dw", sv["ckvn"], dkvc, "tn", F32)
    dckv, g["c_kv_norm"] = rmsnorm_bwd("c_kv_norm_bwd", dckvn, sv["ckv"], p["c_kv_norm"], C_RANK, (BF16,))
    dckr = rope("c_k_rope_bwd", dkr, tabs["c"][1], BF16)

    dqb, dkb, dvb = dilated_bwd("b_dilated_bwd", sv["qb"], sv["kb"], sv["proj"], O_BV // HEAD, dyb, sv["yb"], sv["lse_b"])
    dbq = rope("b_q_rope_bwd", dqb, tabs["b"][1], BF16)
    dbk = rope("b_k_rope_bwd", dkb, tabs["b"][1], BF16)
    dbv = dvb.astype(BF16)

    kg = A_HEADS // A_KV
    (dqa,), (dka,), dva = attn_bwd("a_attn_bwd", [(sv["qa"], 0, sv["ka"], kg)], sv["av"], kg, sv["lse_a"], sv["ya"],
                                   dya, A_HEADS, HEAD ** -0.5)
    daq, g["a_q_norm"] = rmsnorm_bwd("a_q_norm_bwd", rope("a_q_rope_bwd", dqa, tabs["a"][1], F32), sv["aq"],
                                     p["a_q_norm"], HEAD, (BF16,))
    dak, g["a_k_norm"] = rmsnorm_bwd("a_k_norm_bwd", rope("a_k_rope_bwd", dka, tabs["a"][1], F32), sv["ak"],
                                     p["a_k_norm"], HEAD, (BF16,))

    dproj = jnp.concatenate([daq, dak, dva.astype(BF16), dbq, dbk, dbv, dcq, dckv, dckr[:, :C_ROPE]], axis=1)
    dhn = matmul("in_proj_dx", dproj, p["w_in_t"], "nn", F32, tm=512)
    g["w_in"] = matmul("in_proj_dw", dproj, sv["hn"], "tn", F32, tn=512, tk=512)
    return dhn, g


def _local_step(x, target, layers, final_norm, tabs):
    saved = []
    for p in layers:
        x, sv = _layer_fwd(x, p, tabs)
        saved.append(sv)
    dm = x.shape[1]
    yf = rmsnorm_fwd("final_norm", x, final_norm, dm, F32)
    loss, dyf = loss_head("loss_head", yf, target)
    dx, d_final = rmsnorm_bwd("final_norm_bwd", dyf, x, final_norm, dm, (F32,))
    grads = [None] * len(layers)
    for l in reversed(range(len(layers))):
        dx, grads[l] = _layer_bwd(dx, saved[l], layers[l], tabs)
    return loss, dx, grads, d_final


HBM = pl.BlockSpec(memory_space=pl.ANY)


def _place():
    x, y, c = lax.axis_index("x"), lax.axis_index("y"), lax.axis_index("c")
    others = [(1 - x, y), (x, 1 - y), (1 - x, 1 - y)]
    return x, y, c, 2 * x + y, others


def _remote(src, dst, send_sem, recv_sem, device):
    return pltpu.make_async_remote_copy(src_ref=src, dst_ref=dst, send_sem=send_sem, recv_sem=recv_sem,
                                        device_id=device, device_id_type=MESH)


def gather_small(name, small):
    def body(in_ref, out_ref, send_sems, recv_sems, local_sem):
        x, y, c, chip, others = _place()
        mine = pltpu.make_async_copy(in_ref, out_ref.at[chip], local_sem)
        mine.start()
        copies = []
        for k, (ox, oy) in enumerate(others):
            cp = _remote(in_ref, out_ref.at[chip], send_sems.at[k], recv_sems.at[k], (ox, oy, c))
            cp.start()
            copies.append(cp)
        for k, (ox, oy) in enumerate(others):
            landed = out_ref.at[2 * ox + oy]
            _remote(landed, landed, send_sems.at[k], recv_sems.at[k], (ox, oy, c)).wait_recv()
        for cp in copies:
            cp.wait_send()
        mine.wait()

    return pl.pallas_call(
        body, name=name, in_specs=[HBM], out_specs=HBM, out_shape=jax.ShapeDtypeStruct((4,) + small.shape, small.dtype),
        scratch_shapes=[pltpu.SemaphoreType.DMA((3,)), pltpu.SemaphoreType.DMA((3,)), pltpu.SemaphoreType.DMA],
        compiler_params=pltpu.CompilerParams(has_side_effects=True))(small)


IN_HBM = pl.BlockSpec(memory_space=pltpu.HBM)
SEMS = pl.BlockSpec(memory_space=pltpu.SEMAPHORE)
DATAFLOW = pltpu.SideEffectType.DATAFLOW_SIDE_EFFECTING


BF16_ROWS_PER_TILE = 16


def _halves_by_rows(shape):
    return (shape[-2] // 2) % BF16_ROWS_PER_TILE == 0


def _half(shape, core):
    if _halves_by_rows(shape):
        size = shape[-2] // 2
        return (pl.ds(core * size, size), slice(None))
    size = shape[-1] // 2
    return (slice(None), pl.ds(core * size, size))


def _half_shape(shape):
    r, c = shape[-2:]
    return (r // 2, c) if _halves_by_rows(shape) else (r, c // 2)


def _gather_plan(bufs):
    x, y, c, chip, others = _place()
    plan = []
    for t, ref in enumerate(bufs):
        mine = _half(ref.shape, c)
        for k, (ox, oy) in enumerate(others):
            plan.append((3 * t + k, (ox, oy, c), ref.at[(chip,) + mine], ref.at[(chip,) + mine],
                         ref.at[(2 * ox + oy,) + mine]))
    return plan


def _exchange_plan(bufs):
    x, y, c, chip, others = _place()
    n = len(bufs) // 2
    plan = []
    for t in range(n):
        for k, (ox, oy) in enumerate(others):
            plan.append((3 * t + k, (ox, oy, c), bufs[t].at[2 * ox + oy], bufs[n + t].at[chip], bufs[n + t].at[2 * ox + oy]))
    return plan


def _sibling_plan(bufs):
    x, y, c, chip, others = _place()
    n = len(bufs) // 2
    plan = []
    for t in range(n):
        plan.append((t, (x, y, 1 - c), bufs[t].at[(slice(None),) + _half(bufs[t].shape, 1 - c)], bufs[n + t], bufs[n + t]))
    return plan


PLAN_COPIES = {_gather_plan: lambda n: 3 * n, _exchange_plan: lambda n: 3 * (n // 2), _sibling_plan: lambda n: n // 2}


def split_start(name, bufs, plan_of, after):
    n = len(bufs)

    def body(*refs):
        ins = refs[:n]
        send_sems, recv_sems = refs[n + 1], refs[n + 2]
        token = refs[-1]
        for idx, peer, src, dst, _ in plan_of(ins):
            _remote(src, dst, send_sems.at[idx], recv_sems.at[idx], peer).start()
        token[...] = jnp.zeros_like(token)

    copies = PLAN_COPIES[plan_of](n)
    res = pl.pallas_call(
        body, name=name, in_specs=[IN_HBM] * n + [HBM],
        out_specs=(SEMS, SEMS) + (IN_HBM,) * n + (pl.BlockSpec(memory_space=pltpu.VMEM),),
        out_shape=(pltpu.SemaphoreType.DMA((copies,)), pltpu.SemaphoreType.DMA((copies,)))
        + tuple(pltpu.HBM(b.shape, b.dtype) for b in bufs) + (jax.ShapeDtypeStruct((8, LANES), F32),),
        input_output_aliases={t: 2 + t for t in range(n)},
        compiler_params=pltpu.CompilerParams(has_side_effects=DATAFLOW))(
            *[pltpu.with_memory_space_constraint(b, pltpu.HBM) for b in bufs], after)
    return res[0], res[1], list(res[2:2 + n]), res[-1]


def split_wait(name, send_sems, recv_sems, flying, plan_of, after):
    n = len(flying)

    def body(*refs):
        ins = refs[:n]
        send_ref, recv_ref = refs[n], refs[n + 1]
        for idx, peer, src, dst, landing in plan_of(ins):
            _remote(src, dst, send_ref.at[idx], recv_ref.at[idx], peer).wait_send()
            _remote(landing, landing, send_ref.at[idx], recv_ref.at[idx], peer).wait_recv()

    return list(pl.pallas_call(
        body, name=name, in_specs=[IN_HBM] * n + [SEMS, SEMS] + [HBM] * len(after), out_specs=(IN_HBM,) * n,
        out_shape=tuple(pltpu.HBM(b.shape, b.dtype) for b in flying),
        input_output_aliases={t: t for t in range(n)},
        compiler_params=pltpu.CompilerParams(has_side_effects=DATAFLOW))(*flying, send_sems, recv_sems, *after))


def forward_halves(name, slots):
    n = len(slots)

    def body(*refs):
        ins, outs = refs[:n], refs[n:2 * n]
        send_sems, recv_sems = refs[2 * n:]
        x, y, c, chip, others = _place()
        sibling = (x, y, 1 - c)
        copies = []
        for t in range(n):
            mine = _half(slots[t].shape, c)
            for k, (ox, oy) in enumerate(others):
                cp = _remote(ins[t].at[(2 * ox + oy,) + mine], outs[t].at[(2 * ox + oy,) + mine], send_sems.at[t, k],
                             recv_sems.at[t, k], sibling)
                cp.start()
                copies.append(cp)
        for t in range(n):
            for k, (ox, oy) in enumerate(others):
                theirs = outs[t].at[(2 * ox + oy,) + _half(slots[t].shape, 1 - c)]
                _remote(theirs, theirs, send_sems.at[t, k], recv_sems.at[t, k], sibling).wait_recv()
        for cp in copies:
            cp.wait_send()

    return list(pl.pallas_call(
        body, name=name, in_specs=[HBM] * n, out_specs=[HBM] * n,
        out_shape=[jax.ShapeDtypeStruct(s.shape, s.dtype) for s in slots],
        input_output_aliases={t: t for t in range(n)},
        scratch_shapes=[pltpu.SemaphoreType.DMA((n, 3))] * 2,
        compiler_params=pltpu.CompilerParams(has_side_effects=True))(*slots))


def _half_block(shape, tr):
    r, c = shape[-2:]
    if _halves_by_rows(shape):
        per = r // 2 // tr
        return tr, c, lambda i, core: (core * per + i, 0)
    return tr, c // 2, lambda i, core: (i, core)


def add_half(name, place, grad, theirs):
    hr, hc = _half_shape(grad.shape)
    tr = _rows(hr, hc)
    br, bc, at = _half_block(grad.shape, tr)

    def body(place_ref, g_ref, t_ref, o_ref):
        o_ref[...] = (g_ref[...] + t_ref[...]).astype(o_ref.dtype)

    whole = pl.BlockSpec((None, br, bc), lambda j, i, pr: (j, i, 0))
    return pl.pallas_call(
        body, name=name,
        grid_spec=pltpu.PrefetchScalarGridSpec(
            num_scalar_prefetch=1, grid=(4, hr // tr),
            in_specs=[pl.BlockSpec((None, br, bc), lambda j, i, pr: (j,) + at(i, pr[0])), whole],
            out_specs=whole),
        out_shape=jax.ShapeDtypeStruct((4, hr, hc), BF16),
        compiler_params=_params("parallel", "parallel"))(place, grad, theirs)


def sum_chips(name, place, own, parts, shard_shape):
    _, hr, hc = parts.shape
    tr = _rows(hr, hc)
    br, bc, at = _half_block(shard_shape, tr)

    def body(place_ref, own_ref, p1, p2, p3, o_ref):
        o_ref[...] = own_ref[...].astype(F32) + p1[...].astype(F32) + p2[...].astype(F32) + p3[...].astype(F32)

    def slot(k):
        return pl.BlockSpec((None, br, bc), lambda i, pr: (lax.rem(pr[1] + k, 4), i, 0))

    return pl.pallas_call(
        body, name=name,
        grid_spec=pltpu.PrefetchScalarGridSpec(
            num_scalar_prefetch=1, grid=(hr // tr,), in_specs=[slot(0), slot(1), slot(2), slot(3)],
            out_specs=pl.BlockSpec((br, bc), lambda i, pr: at(i, pr[0]))),
        out_shape=jax.ShapeDtypeStruct(tuple(shard_shape), F32),
        compiler_params=_params("parallel"))(place, own, parts, parts, parts)


def join_halves(name, halves):
    n = len(halves)

    def body(*refs):
        ins, outs = refs[:n], refs[n:2 * n]
        send_sems, recv_sems = refs[2 * n:]
        x, y, c, _, _ = _place()
        copies = []
        for t in range(n):
            mine = _half(halves[t].shape, c)
            cp = _remote(ins[t].at[mine], outs[t].at[mine], send_sems.at[t], recv_sems.at[t], (x, y, 1 - c))
            cp.start()
            copies.append(cp)
        for t in range(n):
            theirs = outs[t].at[_half(halves[t].shape, 1 - c)]
            _remote(theirs, theirs, send_sems.at[t], recv_sems.at[t], (x, y, 1 - c)).wait_recv()
        for cp in copies:
            cp.wait_send()

    return pl.pallas_call(
        body, name=name, in_specs=[HBM] * n, out_specs=[HBM] * n,
        out_shape=[jax.ShapeDtypeStruct(h.shape, h.dtype) for h in halves],
        input_output_aliases={t: t for t in range(n)},
        scratch_shapes=[pltpu.SemaphoreType.DMA((n,))] * 2,
        compiler_params=pltpu.CompilerParams(has_side_effects=True))(*halves)


def allreduce_small(name, buf):
    rows = buf.shape[0]

    def body(in_ref, out_ref, slots, send_sems, recv_sems):
        x, y, c, _, _ = _place()
        me = 4 * x + 2 * y + c
        slots[me] = in_ref[...]
        peers = []
        for k in range(1, 8):
            px = 1 - x if k & 4 else x
            py = 1 - y if k & 2 else y
            pc = 1 - c if k & 1 else c
            peers.append((px, py, pc))
        copies = []
        for k, peer in enumerate(peers):
            cp = _remote(in_ref, slots.at[me], send_sems.at[k], recv_sems.at[k], peer)
            cp.start()
            copies.append(cp)
        for k, (px, py, pc) in enumerate(peers):
            slot = slots.at[4 * px + 2 * py + pc]
            _remote(slot, slot, send_sems.at[k], recv_sems.at[k], (px, py, pc)).wait_recv()
        for cp in copies:
            cp.wait_send()
        acc = slots[0]
        for d in range(1, 8):
            acc = acc + slots[d]
        out_ref[...] = acc

    vm = pl.BlockSpec(memory_space=pltpu.VMEM)
    return pl.pallas_call(
        body, name=name, in_specs=[vm], out_specs=vm, out_shape=jax.ShapeDtypeStruct(buf.shape, F32),
        scratch_shapes=[pltpu.VMEM((8, rows, LANES), F32), pltpu.SemaphoreType.DMA((7,)), pltpu.SemaphoreType.DMA((7,))],
        compiler_params=pltpu.CompilerParams(has_side_effects=True, vmem_limit_bytes=VMEM_LIMIT))(buf)


def cast_layer(name, place, w, layer):
    _, r, cols = w.shape
    tr = _rows(r, cols)

    def body(place_ref, w_ref, o_ref):
        o_ref[...] = w_ref[...].astype(BF16)

    return pl.pallas_call(
        body, name=name,
        grid_spec=pltpu.PrefetchScalarGridSpec(
            num_scalar_prefetch=1, grid=(r // tr,),
            in_specs=[pl.BlockSpec((None, tr, cols), lambda i, pr: (layer, i, 0))],
            out_specs=pl.BlockSpec((None, tr, cols), lambda i, pr: (pr[1], i, 0))),
        out_shape=jax.ShapeDtypeStruct((4, r, cols), BF16),
        compiler_params=_params("parallel"))(place, w)


def _adamw_math(g, w, m, v):
    m = ADAM_B1 * m + (1.0 - ADAM_B1) * g
    v = ADAM_B2 * v + (1.0 - ADAM_B2) * (g * g)
    m_hat = m / (1.0 - ADAM_B1 ** ADAM_STEP)
    v_hat = v / (1.0 - ADAM_B2 ** ADAM_STEP)
    delta = -ADAM_LR * (m_hat / (jnp.sqrt(v_hat) + ADAM_EPS) + ADAM_WD * w)
    return delta, m, v


def adamw_layer(name, layer, g, w, m, v, prev=None):
    nl, r, cols = w.shape
    tr = _rows(r, cols, 8)

    def body(*refs):
        g_ref, w_ref, m_ref, v_ref = refs[:4]
        og, od, om, ov = refs[-4:]
        gv = g_ref[...]
        delta, m2, v2 = _adamw_math(gv, w_ref[...], m_ref[...], v_ref[...])
        og[...] = gv
        od[...] = delta
        om[...] = m2
        ov[...] = v2

    lay = pl.BlockSpec((None, tr, cols), lambda i: (layer, i, 0))
    ins = [g, w, m, v] + (list(prev) if prev is not None else [])
    return pl.pallas_call(
        body, name=name, grid=(r // tr,),
        in_specs=[pl.BlockSpec((tr, cols), lambda i: (i, 0)), lay, lay, lay] + ([HBM] * 4 if prev is not None else []),
        out_specs=[lay] * 4, out_shape=[jax.ShapeDtypeStruct((nl, r, cols), F32)] * 4,
        input_output_aliases={4 + k: k for k in range(4)} if prev is not None else {},
        compiler_params=_params("parallel"))(*ins)


def adamw_packed(name, g, w, m, v):
    def body(g_ref, w_ref, m_ref, v_ref, od, om, ov):
        delta, m2, v2 = _adamw_math(g_ref[...], w_ref[...], m_ref[...], v_ref[...])
        od[...] = delta
        om[...] = m2
        ov[...] = v2

    vm = pl.BlockSpec(memory_space=pltpu.VMEM)
    return pl.pallas_call(
        body, name=name, in_specs=[vm] * 4, out_specs=[vm] * 3, out_shape=[jax.ShapeDtypeStruct(g.shape, F32)] * 3,
        compiler_params=pltpu.CompilerParams(vmem_limit_bytes=VMEM_LIMIT))(g, w, m, v)


def _pack(arrays):
    flat = jnp.concatenate([a.reshape(-1) for a in arrays])
    pad = (-flat.shape[0]) % (8 * LANES)
    return jnp.pad(flat, (0, pad)).reshape(-1, LANES)


def _unpack(buf, shapes):
    flat = buf.reshape(-1)
    out, off = [], 0
    for shp in shapes:
        size = int(np.prod(shp))
        out.append(flat[off:off + size].reshape(shp))
        off += size
    return out


MIXER_W = ("w_in", "w_uq", "w_ukv", "w_out")
FFN_W = ("w_up", "w_down")
BIG = MIXER_W + FFN_W
TRANSPOSED = ("w_in", "w_uq")
COLUMN_CUT = ("w_ukv", "w_up")
SMALL = ("attn_norm", "a_q_norm", "a_k_norm", "c_q_norm", "c_kv_norm", "out_norm", "ffn_norm", "conv_b", "final_norm")
WEIGHTS = ("attn_norm", "w_in", "a_q_norm", "a_k_norm", "c_q_norm", "c_kv_norm", "w_uq", "w_ukv", "out_norm", "w_out",
           "ffn_norm", "w_up", "conv_w", "conv_b", "w_down", "final_norm")
INPUTS = ("x",) + WEIGHTS + ("loss_target",) + tuple("m_" + n for n in WEIGHTS) + tuple("v_" + n for n in WEIGHTS)


def _columns(g):
    return jnp.transpose(g, (1, 0, 2)).reshape(g.shape[1], 4 * g.shape[2])


def _uncolumns(w):
    r, c4 = w.shape
    return jnp.transpose(w.reshape(r, 4, c4 // 4), (1, 0, 2))


def _compute_layouts(full, conv_w, small, layer):
    p = {n: small[n][layer] for n in SMALL if n != "final_norm"}
    if "w_in" in full:
        p["w_in_t"] = full["w_in"].reshape(-1, full["w_in"].shape[2])
    if "w_uq" in full:
        uq_t = full["w_uq"].reshape(C_HEADS, HEAD + C_ROPE, C_RANK)
        p["w_uq_t"] = jnp.concatenate([uq_t[:, :HEAD].reshape(C_W, C_RANK),
                                       jnp.pad(uq_t[:, HEAD:], ((0, 0), (0, HEAD - C_ROPE), (0, 0))).reshape(C_W, C_RANK)], axis=0)
        p["w_ukv"] = _columns(full["w_ukv"])
        p["w_out"] = full["w_out"].reshape(-1, full["w_out"].shape[2])
    if "w_up" in full:
        up = full["w_up"]
        p["w_up"] = up.reshape(2, 2, up.shape[1], up.shape[2])
        p["w_down"] = full["w_down"].reshape(-1, full["w_down"].shape[2])
        fh = conv_w.shape[1] // 2
        taps = jnp.transpose(conv_w.reshape(3, 2, fh), (1, 0, 2))
        p["cwb"] = jnp.concatenate([taps, small["conv_b"][layer].reshape(2, 1, fh), jnp.zeros((2, 4, fh), F32)], axis=1)
    return p


def _shard_layout(name, g):
    if name == "w_in":
        return g.reshape(4, -1, g.shape[1])
    if name == "w_uq":
        uq_t = jnp.concatenate([g[:C_W].reshape(C_HEADS, HEAD, C_RANK),
                                g[C_W:].reshape(C_HEADS, HEAD, C_RANK)[:, :C_ROPE]], axis=1)
        return uq_t.reshape(4, -1, C_RANK)
    if name == "w_ukv":
        return _uncolumns(g)
    if name == "w_up":
        return g.reshape(4, g.shape[2], g.shape[3])
    return g.reshape(4, -1, g.shape[1])


def _conv_grads(dcwb):
    fh = dcwb.shape[2]
    return jnp.transpose(dcwb[:, 0:3, :], (1, 0, 2)).reshape(3, 2 * fh), dcwb[:, 3, :].reshape(2 * fh)


def _shard_layouts(g):
    return ({n: _shard_layout(n, g[n]) for n in BIG},) + _conv_grads(g["cwb"])


def kernel(x, attn_norm, w_in, a_q_norm, a_k_norm, c_q_norm, c_kv_norm, w_uq, w_ukv, out_norm, w_out, ffn_norm, w_up, conv_w, conv_b, w_down, final_norm, loss_target, m_attn_norm, m_w_in, m_a_q_norm, m_a_k_norm, m_c_q_norm, m_c_kv_norm, m_w_uq, m_w_ukv, m_out_norm, m_w_out, m_ffn_norm, m_w_up, m_conv_w, m_conv_b, m_w_down, m_final_norm, v_attn_norm, v_w_in, v_a_q_norm, v_a_k_norm, v_c_q_norm, v_c_kv_norm, v_w_uq, v_w_ukv, v_out_norm, v_w_out, v_ffn_norm, v_w_up, v_conv_w, v_conv_b, v_w_down, v_final_norm):
    a = dict(zip(INPUTS, (x, attn_norm, w_in, a_q_norm, a_k_norm, c_q_norm, c_kv_norm, w_uq, w_ukv, out_norm, w_out, ffn_norm, w_up, conv_w, conv_b, w_down, final_norm, loss_target, m_attn_norm, m_w_in, m_a_q_norm, m_a_k_norm, m_c_q_norm, m_c_kv_norm, m_w_uq, m_w_ukv, m_out_norm, m_w_out, m_ffn_norm, m_w_up, m_conv_w, m_conv_b, m_w_down, m_final_norm, v_attn_norm, v_w_in, v_a_q_norm, v_a_k_norm, v_c_q_norm, v_c_kv_norm, v_w_uq, v_w_ukv, v_out_norm, v_w_out, v_ffn_norm, v_w_up, v_conv_w, v_conv_b, v_w_down, v_final_norm)))
    nl = w_in.shape[0]
    seq = x.shape[1]
    for n in TRANSPOSED:
        for kind in ("", "m_", "v_"):
            a[kind + n] = jnp.swapaxes(a[kind + n], 1, 2)
    chip = 2 * lax.axis_index("x") + lax.axis_index("y")
    place = jnp.stack([lax.axis_index("c"), chip]).astype(jnp.int32)
    tabs = _rope_tables(seq)

    assert nl == 2
    quarter = conv_w.shape[2]
    taps = jnp.pad(conv_w, ((0, 0), (0, 8 - conv_w.shape[1]), (0, 0))).reshape(nl * 8, quarter)
    taps_full = gather_small("gather_taps", taps).reshape(4, nl, 8, quarter)
    conv_full = [jnp.transpose(taps_full[:, l, 0:3, :], (1, 0, 2)).reshape(3, 4 * quarter) for l in range(nl)]
    flights = {}
    here, after = place, taps_full
    for l in range(nl):
        for key, names in ((f"in{l}", ("w_in",)), (f"mix{l}", MIXER_W[1:]), (f"ffn{l}", FFN_W)):
            slots = [cast_layer(f"cast{l}_{n}", here, a[n], l) for n in names]
            send_sems, recv_sems, flying, token = split_start(f"gather_{key}_start", slots, _gather_plan, after)
            flights[key] = (names, send_sems, recv_sems, flying)
            here = after = place + token[0, 0].astype(jnp.int32)
    started = token[0, 0]

    def land(key, after):
        names, send_sems, recv_sems, flying = flights[key]
        landed = split_wait(f"gather_{key}_wait", send_sems, recv_sems, flying, _gather_plan, after)
        return dict(zip(names, forward_halves(f"gather_{key}_forward", landed)))

    h = x[0]
    layers, saved = [], []
    for l in range(nl):
        p = _compute_layouts(land(f"in{l}", [taps_full if l == 0 else h]), conv_full[l], a, l)
        if l == 0:
            p["attn_norm"] = p["attn_norm"] + started
        sv = _mixers_ab_fwd(h, p, tabs)
        p.update(_compute_layouts(land(f"mix{l}", [sv["yb"]]), conv_full[l], a, l))
        x1 = _mixer_c_out_fwd(h, sv, p, tabs)
        p.update(_compute_layouts(land(f"ffn{l}", [x1]), conv_full[l], a, l))
        h = _ffn_fwd(x1, p, sv)
        layers.append(p)
        saved.append(sv)
    dm = h.shape[1]
    yf = rmsnorm_fwd("final_norm", h, final_norm, dm, F32)
    loss, dyf = loss_head("loss_head", yf, loss_target[0])
    loss = lax.psum(loss, ("x", "y", "c"))
    dx, d_final = rmsnorm_bwd("final_norm_bwd", dyf, h, final_norm, dm, (F32,))

    def swap_begin(tag, names, g):
        g_list = [_shard_layout(n, g[n]) for n in names]
        theirs = [lax.empty((4,) + _half_shape(s.shape), s.dtype) for s in g_list]
        send_sems, recv_sems, flying, token = split_start(f"{tag}_sibling_start", g_list + theirs, _sibling_plan, place)
        return (names, send_sems, recv_sems, flying), token[0, 0]

    def exchange_begin(tag, swap, after):
        names, send_sems, recv_sems, flying = swap
        landed = split_wait(f"{tag}_sibling_wait", send_sems, recv_sems, flying, _sibling_plan, after)
        g_list, theirs = landed[:len(names)], landed[len(names):]
        sums = [add_half(f"{tag}_add_{n}", place, gr, t) for n, gr, t in zip(names, g_list, theirs)]
        parts = [lax.empty(s.shape, s.dtype) for s in sums]
        send_sems, recv_sems, flying, token = split_start(f"{tag}_chips_start", sums + parts, _exchange_plan, place)
        return (names, [gr.shape[1:] for gr in g_list], send_sems, recv_sems, flying), token

    def exchange_end(tag, flight, after):
        names, shard_shapes, send_sems, recv_sems, flying = flight
        landed = split_wait(f"{tag}_chips_wait", send_sems, recv_sems, flying, _exchange_plan, after)
        sums, parts = landed[:len(names)], landed[len(names):]
        halves = [sum_chips(f"{tag}_sum_{n}", place, s, p, shp) for n, s, p, shp in zip(names, sums, parts, shard_shapes)]
        return dict(zip(names, join_halves(f"{tag}_join", halves)))

    grads, conv_grads, flights_f, flights_m = [None] * nl, [None] * nl, [None] * nl, [None] * nl
    after = 0.0
    for l in reversed(range(nl)):
        sv, p = saved[l], layers[l]
        dxn, g = _ffn_bwd(dx, sv, p, after)
        swap_f, after = swap_begin(f"reduce{l}f", FFN_W, g)
        dx1, dy, g_out = _out_proj_bwd(dxn, dx, sv, p, after)
        flights_f[l], token = exchange_begin(f"reduce{l}f", swap_f, [dy])
        dhn, g_mix = _mixers_bwd(dy, sv, p, tabs, token[0, 0])
        g.update(g_out)
        g.update(g_mix)
        swap_m, after = swap_begin(f"reduce{l}m", MIXER_W, g)
        dx, g_norm = _attn_norm_bwd(dhn, dx1, sv, p, after)
        g.update(g_norm)
        flights_m[l], token = exchange_begin(f"reduce{l}m", swap_m, [dx])
        after = token[0, 0]
        grads[l], conv_grads[l] = g, _conv_grads(g["cwb"])
    grad_x = dx

    out = {}
    reduced1 = exchange_end("reduce1f", flights_f[1], [token])
    for n in FFN_W:
        out[n] = adamw_layer(f"adamw1_{n}", 1, reduced1[n], a[n], a["m_" + n], a["v_" + n])
    reduced1 = exchange_end("reduce1m", flights_m[1], [out[n][3] for n in FFN_W])
    for n in MIXER_W:
        out[n] = adamw_layer(f"adamw1_{n}", 1, reduced1[n], a[n], a["m_" + n], a["v_" + n])

    small_g = {n: jnp.stack([grads[l][n] for l in range(nl)]) for n in SMALL if n not in ("conv_b", "final_norm")}
    small_g["conv_b"] = jnp.stack([cg[1] for cg in conv_grads])
    small_g["final_norm"] = d_final
    conv_w_g = jnp.stack([cg[0] for cg in conv_grads])
    shapes = [a[n].shape for n in SMALL] + [conv_w_g.shape]
    summed = _unpack(allreduce_small("reduce_small", _pack([small_g[n] for n in SMALL] + [conv_w_g])), shapes)
    small_g = dict(zip(SMALL, summed[:-1]))
    small_g["conv_w"] = lax.dynamic_slice_in_dim(summed[-1], chip * quarter, quarter, axis=2)
    names = SMALL + ("conv_w",)
    shapes = [a[n].shape for n in names]
    d_small, m_small, v_small = adamw_packed(
        "adamw_small", _pack([small_g[n] for n in names]), _pack([a[n] for n in names]),
        _pack([a["m_" + n] for n in names]), _pack([a["v_" + n] for n in names]))

    reduced0 = exchange_end("reduce0f", flights_f[0], [out[n][3] for n in BIG] + [d_small])
    for n in FFN_W:
        out[n] = adamw_layer(f"adamw0_{n}", 0, reduced0[n], a[n], a["m_" + n], a["v_" + n], prev=out[n])
    reduced0 = exchange_end("reduce0m", flights_m[0], [out[n][3] for n in FFN_W])
    for n in MIXER_W:
        out[n] = adamw_layer(f"adamw0_{n}", 0, reduced0[n], a[n], a["m_" + n], a["v_" + n], prev=out[n])
    for n, d_, m_, v_ in zip(names, _unpack(d_small, shapes), _unpack(m_small, shapes), _unpack(v_small, shapes)):
        out[n] = (small_g[n], d_, m_, v_)
    for n in TRANSPOSED:
        out[n] = [jnp.swapaxes(o, 1, 2) for o in out[n]]

    return (loss, grad_x[None]) + tuple(out[n][k] for k in range(4) for n in WEIGHTS)
```

```python
import functools

import jax
import jax.numpy as jnp
import numpy as np
from jax import lax
from jax.experimental import pallas as pl
from jax.experimental.pallas import tpu as pltpu

F32 = jnp.float32
BF16 = jnp.bfloat16
MESH = pl.DeviceIdType.MESH

V7X_VMEM_BYTES = 64 * 1024 * 1024
VMEM_LIMIT = V7X_VMEM_BYTES - 6 * 1024 * 1024
LANES = 128
ELEMENTWISE_BLOCK_BYTES = 2 * 1024 * 1024

HEAD = 128
A_HEADS, A_KV = 6, 2
B_HEADS, B_GROUPS = 4, 3
B_DILATIONS = (1, 4, 16)
B_HALF = 64
C_HEADS, C_RANK, C_ROPE = 6, 512, 64
GRID_W = 64
A_THETA, B_THETA, C_THETA = 10000.0, 500000.0, 10000.0
B_ROPE_DIM = 32
EPS = 1e-6
A_W, B_W, C_W = A_HEADS * HEAD, B_HEADS * HEAD, C_HEADS * HEAD
O_AQ, O_AK, O_AV = 0, 768, 1024
O_BQ, O_BK, O_BV = 1280, 2816, 3328
O_CQ, O_CKV, O_CKR = 3840, 4352, 4864
IN_W = 4928
PROJ_W = 5120

ADAM_LR, ADAM_B1, ADAM_B2, ADAM_EPS, ADAM_WD, ADAM_STEP = 0.001, 0.9, 0.999, 1e-08, 0.01, 10

NN = (((1,), (0,)), ((), ()))
NT = (((1,), (1,)), ((), ()))
TN = (((0,), (0,)), ((), ()))


def _dot(a, b, dims=NN):
    return lax.dot_general(a, b, dims, preferred_element_type=F32)


def _params(*sem):
    return pltpu.CompilerParams(dimension_semantics=sem if sem else None, vmem_limit_bytes=VMEM_LIMIT)


def _tile(n, target, unit=LANES):
    if n <= target:
        return n
    best = 0
    for t in range(unit, target + 1, unit):
        if n % t == 0:
            best = t
    return best if best else n


def _rows(r, width, itemsize=4):
    return _tile(r, max(8, ELEMENTWISE_BLOCK_BYTES // (width * itemsize)), 8)


def _matmul_call(name, dims, grid, a_spec, b_spec, o_spec, out_shape, acc_shape, add_spec=None):
    nk = grid[2]

    def body(*refs):
        a_ref, b_ref = refs[:2]
        add_ref = None if add_spec is None else refs[2]
        o_ref = refs[2 if add_spec is None else 3]

        def finish(r):
            if add_ref is not None:
                r = r + add_ref[...]
            o_ref[...] = r.astype(o_ref.dtype)

        if nk == 1:
            finish(_dot(a_ref[...], b_ref[...], dims))
            return
        acc = refs[-1]
        k = pl.program_id(2)

        @pl.when(k == 0)
        def _():
            acc[...] = _dot(a_ref[...], b_ref[...], dims)

        if nk > 2:
            @pl.when((k > 0) & (k < nk - 1))
            def _():
                acc[...] += _dot(a_ref[...], b_ref[...], dims)

        @pl.when(k == nk - 1)
        def _():
            finish(acc[...] + _dot(a_ref[...], b_ref[...], dims))

    in_specs = [a_spec, b_spec] + ([] if add_spec is None else [add_spec])
    return pl.pallas_call(
        body, name=name, grid=grid, in_specs=in_specs, out_specs=o_spec, out_shape=out_shape,
        scratch_shapes=[pltpu.VMEM(acc_shape, F32)] if nk > 1 else [],
        compiler_params=_params("parallel", "parallel", "arbitrary"))


def matmul(name, a, b, mode, out_dtype, add=None, tm=1024, tn=1024, tk=2816):
    if mode == "nn":
        (m, k), (k2, n) = a.shape, b.shape
    elif mode == "nt":
        (m, k), (n, k2) = a.shape, b.shape
    else:
        (k, m), (k2, n) = a.shape, b.shape
    assert k == k2, (name, a.shape, b.shape)
    tm, tn, tk = _tile(m, tm), _tile(n, tn), _tile(k, tk)
    grid = (m // tm, n // tn, k // tk)
    if mode == "tn":
        a_spec = pl.BlockSpec((tk, tm), lambda i, j, kk: (kk, i))
    else:
        a_spec = pl.BlockSpec((tm, tk), lambda i, j, kk: (i, kk))
    whole_b = {"pipeline_mode": pl.Buffered(1)} if grid[1] == 1 and grid[2] == 1 and grid[0] > 1 else {}
    if mode == "nt":
        b_spec = pl.BlockSpec((tn, tk), lambda i, j, kk: (j, kk), **whole_b)
    else:
        b_spec = pl.BlockSpec((tk, tn), lambda i, j, kk: (kk, j), **whole_b)
    o_spec = pl.BlockSpec((tm, tn), lambda i, j, kk: (i, j))
    dims = {"nn": NN, "nt": NT, "tn": TN}[mode]
    call = _matmul_call(name, dims, grid, a_spec, b_spec, o_spec, jax.ShapeDtypeStruct((m, n), out_dtype),
                        (tm, tn), None if add is None else o_spec)
    return call(a, b) if add is None else call(a, b, add)


def rmsnorm_fwd(name, x, g, width, out_dtype):
    r, cols = x.shape
    nb = cols // width
    tr = _rows(r, width)

    def body(x_ref, g_ref, o_ref):
        xv = x_ref[...]
        rs = lax.rsqrt(jnp.mean(xv * xv, axis=-1, keepdims=True) + EPS)
        o_ref[...] = (xv * rs * g_ref[...]).astype(o_ref.dtype)

    blk = pl.BlockSpec((tr, width), lambda i, j: (i, j))
    return pl.pallas_call(
        body, name=name, grid=(r // tr, nb),
        in_specs=[blk, pl.BlockSpec((1, width), lambda i, j: (0, 0))], out_specs=blk,
        out_shape=jax.ShapeDtypeStruct((r, cols), out_dtype),
        compiler_params=_params("parallel", "parallel"))(x, g.reshape(1, width))


def rmsnorm_bwd(name, dy, x, g, width, out_dtypes=(F32,), add=None):
    r, cols = x.shape
    nb = cols // width
    tr = _rows(r, width)
    n_out = len(out_dtypes)

    def body(*refs):
        dy_ref, x_ref, g_ref = refs[:3]
        add_ref = refs[3] if add is not None else None
        outs = refs[-(n_out + 1):-1]
        dg_ref = refs[-1]
        xv = x_ref[...]
        dyv = dy_ref[...].astype(F32)
        rs = lax.rsqrt(jnp.mean(xv * xv, axis=-1, keepdims=True) + EPS)
        xh = xv * rs
        dyg = dyv * g_ref[...]
        dx = rs * (dyg - xh * jnp.mean(dyg * xh, axis=-1, keepdims=True))
        if add_ref is not None:
            dx = dx + add_ref[...]
        for o in outs:
            o[...] = dx.astype(o.dtype)

        @pl.when((pl.program_id(0) == 0) & (pl.program_id(1) == 0))
        def _():
            dg_ref[...] = jnp.zeros_like(dg_ref)

        dg_ref[...] += jnp.sum(dyv * xh, axis=0, keepdims=True)

    blk = pl.BlockSpec((tr, width), lambda i, j: (i, j))
    vec = pl.BlockSpec((1, width), lambda i, j: (0, 0))
    ins = [dy, x, g.reshape(1, width)] + ([add] if add is not None else [])
    res = pl.pallas_call(
        body, name=name, grid=(r // tr, nb),
        in_specs=[blk, blk, vec] + ([blk] if add is not None else []),
        out_specs=[blk] * n_out + [vec],
        out_shape=[jax.ShapeDtypeStruct((r, cols), dt) for dt in out_dtypes] + [jax.ShapeDtypeStruct((1, width), F32)],
        compiler_params=_params("arbitrary", "arbitrary"))(*ins)
    return tuple(res[:n_out]) + (res[n_out].reshape(width),)


def _rope_angles(pos, dim, theta):
    inv = theta ** (-jnp.arange(0, dim, 2, dtype=F32) / dim)
    ang = pos.astype(F32)[:, None] * inv[None, :]
    return jnp.cos(ang), jnp.sin(ang)


def _rope_tables(s):
    rows = s // GRID_W
    row_pos = jnp.repeat(jnp.arange(rows), GRID_W)
    col_pos = jnp.tile(jnp.arange(GRID_W), rows)
    t_pos = jnp.arange(s)
    z = lambda n: jnp.zeros((s, n), F32)
    o = lambda n: jnp.ones((s, n), F32)
    cr, sr = _rope_angles(row_pos, HEAD // 2, A_THETA)
    cc, sc = _rope_angles(col_pos, HEAD // 2, A_THETA)
    tab_a = (jnp.concatenate([cr, cr, cc, cc], 1), jnp.concatenate([-sr, z(32), -sc, z(32)], 1),
             jnp.concatenate([z(32), sr, z(32), sc], 1), 32)
    cp, sp = _rope_angles(t_pos, B_ROPE_DIM, B_THETA)
    tab_b = (jnp.concatenate([cp, cp, o(96)], 1), jnp.concatenate([-sp, z(112)], 1),
             jnp.concatenate([z(16), sp, z(96)], 1), 16)
    cm, sm = _rope_angles(t_pos, C_ROPE, C_THETA)
    tab_c = (jnp.concatenate([cm, cm, o(64)], 1), jnp.concatenate([-sm, z(96)], 1),
             jnp.concatenate([z(32), sm, z(64)], 1), 32)

    def transposed(tab):
        c, s1, s2, h = tab
        return (c, jnp.roll(s2, -h, axis=1), jnp.roll(s1, h, axis=1), h)

    return {k: (t, transposed(t)) for k, t in (("a", tab_a), ("b", tab_b), ("c", tab_c))}


def rope(name, x, tab, out_dtype):
    c, s1, s2, h = tab
    s, cols = x.shape
    nh = cols // HEAD
    ts = _rows(s, HEAD)

    def body(x_ref, c_ref, s1_ref, s2_ref, o_ref):
        xv = x_ref[...].astype(F32)
        out = xv * c_ref[...] + pltpu.roll(xv, HEAD - h, 1) * s1_ref[...] + pltpu.roll(xv, h, 1) * s2_ref[...]
        o_ref[...] = out.astype(o_ref.dtype)

    blk = pl.BlockSpec((ts, HEAD), lambda i, j: (i, j))
    tb = pl.BlockSpec((ts, HEAD), lambda i, j: (i, 0))
    return pl.pallas_call(
        body, name=name, grid=(s // ts, nh), in_specs=[blk, tb, tb, tb], out_specs=blk,
        out_shape=jax.ShapeDtypeStruct((s, cols), out_dtype),
        compiler_params=_params("parallel", "parallel"))(x, c, s1, s2)


PREP_ROWS = 256


def _rope_of(x, tab, h):
    c, s1, s2 = tab
    return x * c + pltpu.roll(x, HEAD - h, 1) * s1 + pltpu.roll(x, h, 1) * s2


def _heads(start, n):
    return [(j, slice(start + HEAD * j, start + HEAD * (j + 1))) for j in range(n)]


def mixer_prep(name, proj, p, tabs):
    s = proj.shape[0]
    tr = PREP_ROWS
    (ca, s1a, s2a, ha), (cb, s1b, s2b, hb) = tabs["a"][0], tabs["b"][0]

    def body(proj_ref, gaq, gak, gcq, gckv, ca_r, s1a_r, s2a_r, cb_r, s1b_r, s2b_r,
             qa_ref, ka_ref, av_ref, qb_ref, kb_ref, cqn_ref, ckvn_ref):
        ta = (ca_r[...], s1a_r[...], s2a_r[...])
        tb = (cb_r[...], s1b_r[...], s2b_r[...])

        def normed(x, g):
            return x * lax.rsqrt(jnp.mean(x * x, axis=-1, keepdims=True) + EPS) * g

        for (src0, n, gain, out) in ((O_AQ, A_HEADS, gaq, qa_ref), (O_AK, A_KV, gak, ka_ref)):
            for j, cols in _heads(src0, n):
                out[:, HEAD * j:HEAD * (j + 1)] = _rope_of(normed(proj_ref[:, cols], gain[...]), ta, ha).astype(BF16)
        av_ref[...] = proj_ref[:, O_AV:O_BQ].astype(BF16)
        for (src0, n, out) in ((O_BQ, B_GROUPS * B_HEADS, qb_ref), (O_BK, B_HEADS, kb_ref)):
            for j, cols in _heads(src0, n):
                out[:, HEAD * j:HEAD * (j + 1)] = _rope_of(proj_ref[:, cols], tb, hb)
        cqn_ref[...] = normed(proj_ref[:, O_CQ:O_CKV], gcq[...]).astype(BF16)
        ckvn_ref[...] = normed(proj_ref[:, O_CKV:O_CKR], gckv[...]).astype(BF16)

    rows = lambda w: pl.BlockSpec((tr, w), lambda i: (i, 0))
    vec = lambda w: pl.BlockSpec((1, w), lambda i: (0, 0))
    widths = (A_W, A_KV * HEAD, A_KV * HEAD, B_GROUPS * B_W, B_W, C_RANK, C_RANK)
    dtypes = (BF16, BF16, BF16, F32, F32, BF16, BF16)
    return pl.pallas_call(
        body, name=name, grid=(s // tr,),
        in_specs=[rows(proj.shape[1]), vec(HEAD), vec(HEAD), vec(C_RANK), vec(C_RANK)] + [rows(HEAD)] * 6,
        out_specs=[rows(w) for w in widths],
        out_shape=[jax.ShapeDtypeStruct((s, w), dt) for w, dt in zip(widths, dtypes)],
        compiler_params=_params("parallel"))(
            proj, p["a_q_norm"].reshape(1, HEAD), p["a_k_norm"].reshape(1, HEAD), p["c_q_norm"].reshape(1, C_RANK),
            p["c_kv_norm"].reshape(1, C_RANK), ca, s1a, s2a, cb, s1b, s2b)


def mixer_prep_bwd(name, proj, p, tabs, dqa, dka, dva, dqb, dkb, dvb, dcqn, dckvn, dckr):
    s = proj.shape[0]
    tr = PREP_ROWS
    (ca, s1a, s2a, ha), (cb, s1b, s2b, hb) = tabs["a"][1], tabs["b"][1]

    def body(proj_ref, dqa_ref, dka_ref, dva_ref, dqb_ref, dkb_ref, dvb_ref, dcqn_ref, dckvn_ref, dckr_ref,
             gaq, gak, gcq, gckv, ca_r, s1a_r, s2a_r, cb_r, s1b_r, s2b_r, dproj_ref, dgaq, dgak, dgcq, dgckv):
        ta = (ca_r[...], s1a_r[...], s2a_r[...])
        tb = (cb_r[...], s1b_r[...], s2b_r[...])

        @pl.when(pl.program_id(0) == 0)
        def _():
            for ref in (dgaq, dgak, dgcq, dgckv):
                ref[...] = jnp.zeros_like(ref)

        def norm_bwd(dy, x, gain, dgain):
            rs = lax.rsqrt(jnp.mean(x * x, axis=-1, keepdims=True) + EPS)
            xh = x * rs
            dyg = dy * gain[...]
            dgain[...] += jnp.sum(dy * xh, axis=0, keepdims=True)
            return rs * (dyg - xh * jnp.mean(dyg * xh, axis=-1, keepdims=True))

        for (dst0, n, d_ref, gain, dgain) in ((O_AQ, A_HEADS, dqa_ref, gaq, dgaq), (O_AK, A_KV, dka_ref, gak, dgak)):
            for j, cols in _heads(dst0, n):
                dy = _rope_of(d_ref[:, HEAD * j:HEAD * (j + 1)], ta, ha)
                dproj_ref[:, cols] = norm_bwd(dy, proj_ref[:, cols], gain, dgain).astype(BF16)
        dproj_ref[:, O_AV:O_BQ] = dva_ref[...].astype(BF16)
        for (dst0, n, d_ref) in ((O_BQ, B_GROUPS * B_HEADS, dqb_ref), (O_BK, B_HEADS, dkb_ref)):
            for j, cols in _heads(dst0, n):
                dproj_ref[:, cols] = _rope_of(d_ref[:, HEAD * j:HEAD * (j + 1)], tb, hb).astype(BF16)
        dproj_ref[:, O_BV:O_CQ] = dvb_ref[...].astype(BF16)
        dproj_ref[:, O_CQ:O_CKV] = norm_bwd(dcqn_ref[...], proj_ref[:, O_CQ:O_CKV], gcq, dgcq).astype(BF16)
        dproj_ref[:, O_CKV:O_CKR] = norm_bwd(dckvn_ref[...], proj_ref[:, O_CKV:O_CKR], gckv, dgckv).astype(BF16)
        dproj_ref[:, O_CKR:] = dckr_ref[:, :C_ROPE]

    rows = lambda w: pl.BlockSpec((tr, w), lambda i: (i, 0))
    vec = lambda w: pl.BlockSpec((1, w), lambda i: (0, 0))
    grads = (dqa, dka, dva, dqb, dkb, dvb, dcqn, dckvn, dckr)
    res = pl.pallas_call(
        body, name=name, grid=(s // tr,),
        in_specs=[rows(proj.shape[1])] + [rows(g.shape[1]) for g in grads]
        + [vec(HEAD), vec(HEAD), vec(C_RANK), vec(C_RANK)] + [rows(HEAD)] * 6,
        out_specs=[rows(proj.shape[1]), vec(HEAD), vec(HEAD), vec(C_RANK), vec(C_RANK)],
        out_shape=[jax.ShapeDtypeStruct(proj.shape, BF16)] + [jax.ShapeDtypeStruct((1, w), F32) for w in (HEAD, HEAD, C_RANK, C_RANK)],
        compiler_params=_params("arbitrary"))(
            proj, *grads, p["a_q_norm"].reshape(1, HEAD), p["a_k_norm"].reshape(1, HEAD), p["c_q_norm"].reshape(1, C_RANK),
            p["c_kv_norm"].reshape(1, C_RANK), ca, s1a, s2a, cb, s1b, s2b)
    return res[0], res[1].reshape(HEAD), res[2].reshape(HEAD), res[3].reshape(C_RANK), res[4].reshape(C_RANK)


OUT_GROUPS = ((0, A_W), (A_W, A_W + B_W), (A_W + B_W, A_W + B_W + C_W))


def out_norms(name, ya, yb, yc, gain):
    s = ya.shape[0]
    tr = PREP_ROWS
    width = OUT_GROUPS[-1][1]

    def body(ya_ref, yb_ref, yc_ref, g_ref, y_ref):
        for (lo, hi), src in zip(OUT_GROUPS, (ya_ref, yb_ref, yc_ref)):
            x = src[...]
            rs = lax.rsqrt(jnp.mean(x * x, axis=-1, keepdims=True) + EPS)
            y_ref[:, lo:hi] = (x * rs * g_ref[:, lo:hi]).astype(BF16)

    rows = lambda w: pl.BlockSpec((tr, w), lambda i: (i, 0))
    return pl.pallas_call(
        body, name=name, grid=(s // tr,),
        in_specs=[rows(A_W), rows(B_W), rows(C_W), pl.BlockSpec((1, width), lambda i: (0, 0))], out_specs=rows(width),
        out_shape=jax.ShapeDtypeStruct((s, width), BF16), compiler_params=_params("parallel"))(
            ya, yb, yc, gain.reshape(1, width))


def out_norms_bwd(name, dy, ya, yb, yc, gain):
    s = ya.shape[0]
    tr = PREP_ROWS
    width = OUT_GROUPS[-1][1]

    def body(dy_ref, ya_ref, yb_ref, yc_ref, g_ref, dya_ref, dyb_ref, dyc_ref, dg_ref):
        @pl.when(pl.program_id(0) == 0)
        def _():
            dg_ref[...] = jnp.zeros_like(dg_ref)

        for (lo, hi), src, dst in zip(OUT_GROUPS, (ya_ref, yb_ref, yc_ref), (dya_ref, dyb_ref, dyc_ref)):
            x = src[...]
            d = dy_ref[:, lo:hi]
            rs = lax.rsqrt(jnp.mean(x * x, axis=-1, keepdims=True) + EPS)
            xh = x * rs
            dg = d * g_ref[:, lo:hi]
            dst[...] = (rs * (dg - xh * jnp.mean(dg * xh, axis=-1, keepdims=True))).astype(dst.dtype)
            dg_ref[:, lo:hi] += jnp.sum(d * xh, axis=0, keepdims=True)

    rows = lambda w: pl.BlockSpec((tr, w), lambda i: (i, 0))
    vec = pl.BlockSpec((1, width), lambda i: (0, 0))
    dya, dyb, dyc, dg = pl.pallas_call(
        body, name=name, grid=(s // tr,),
        in_specs=[rows(width), rows(A_W), rows(B_W), rows(C_W), vec], out_specs=[rows(A_W), rows(B_W), rows(C_W), vec],
        out_shape=[jax.ShapeDtypeStruct((s, A_W), BF16), jax.ShapeDtypeStruct((s, B_W), F32),
                   jax.ShapeDtypeStruct((s, C_W), BF16), jax.ShapeDtypeStruct((1, width), F32)],
        compiler_params=_params("arbitrary"))(dy, ya, yb, yc, gain.reshape(1, width))
    return dya, dyb, dyc, dg.reshape(width)


ATTN_TQ = 256


def attn_fwd(name, parts, v, v_group, nheads, scale):
    s = v.shape[0]
    tq = ATTN_TQ
    npart = len(parts)

    def body(*refs):
        v_ref, o_ref, lse_ref = refs[2 * npart:]
        sc = None
        for p in range(npart):
            t = _dot(refs[2 * p][...], refs[2 * p + 1][...], NT)
            sc = t if sc is None else sc + t
        sc = sc * scale
        m = jnp.max(sc, axis=-1, keepdims=True)
        e = jnp.exp(sc - m)
        l = jnp.sum(e, axis=-1, keepdims=True)
        o_ref[...] = _dot(e.astype(BF16), v_ref[...]) / l
        lse_ref[...] = jnp.broadcast_to(m + jnp.log(l), (tq, HEAD))

    in_specs, ins = [], []
    for q, qoff, k, kg in parts:
        in_specs.append(pl.BlockSpec((tq, HEAD), lambda h, i, qoff=qoff: (i, qoff + h)))
        in_specs.append(pl.BlockSpec((s, HEAD), lambda h, i, kg=kg: (0, h // kg)))
        ins += [q, k]
    in_specs.append(pl.BlockSpec((s, HEAD), lambda h, i: (0, h // v_group)))
    out_blk = pl.BlockSpec((tq, HEAD), lambda h, i: (i, h))
    return pl.pallas_call(
        body, name=name, grid=(nheads, s // tq), in_specs=in_specs, out_specs=[out_blk, out_blk],
        out_shape=[jax.ShapeDtypeStruct((s, nheads * HEAD), F32)] * 2,
        compiler_params=_params("parallel", "parallel"))(*ins, v)


def attn_bwd(name, parts, v, v_group, lse, o, do, nheads, scale):
    s = v.shape[0]
    tq = ATTN_TQ
    npart = len(parts)

    def body(*refs):
        v_ref, lse_ref, o_ref, do_ref = refs[2 * npart:2 * npart + 4]
        outs = refs[2 * npart + 4:]
        dq_refs, dk_refs, dv_ref = outs[:npart], outs[npart:2 * npart], outs[2 * npart]
        h, i = pl.program_id(0), pl.program_id(1)
        sc = None
        for p in range(npart):
            t = _dot(refs[2 * p][...], refs[2 * p + 1][...], NT)
            sc = t if sc is None else sc + t
        pr = jnp.exp(sc * scale - lse_ref[:, 0:1])
        dov = do_ref[...]
        delta = jnp.sum(dov.astype(F32) * o_ref[...], axis=-1, keepdims=True)
        dp = _dot(dov, v_ref[...], NT)
        ds = (pr * (dp - delta) * scale).astype(BF16)

        @pl.when((i == 0) & (h % v_group == 0))
        def _():
            dv_ref[...] = jnp.zeros_like(dv_ref)

        dv_ref[...] += _dot(pr.astype(BF16), dov, TN)
        for p in range(npart):
            kg = parts[p][3]
            dq_refs[p][...] = _dot(ds, refs[2 * p + 1][...])

            @pl.when((i == 0) & (h % kg == 0))
            def _(p=p):
                dk_refs[p][...] = jnp.zeros_like(dk_refs[p])

            dk_refs[p][...] += _dot(ds, refs[2 * p][...], TN)

    in_specs, ins = [], []
    for q, qoff, k, kg in parts:
        in_specs.append(pl.BlockSpec((tq, HEAD), lambda h, i, qoff=qoff: (i, qoff + h)))
        in_specs.append(pl.BlockSpec((s, HEAD), lambda h, i, kg=kg: (0, h // kg)))
        ins += [q, k]
    hq_blk = pl.BlockSpec((tq, HEAD), lambda h, i: (i, h))
    in_specs += [pl.BlockSpec((s, HEAD), lambda h, i: (0, h // v_group)), hq_blk, hq_blk, hq_blk]
    out_specs = [hq_blk] * npart
    out_shape = [jax.ShapeDtypeStruct((s, nheads * HEAD), F32)] * npart
    for q, qoff, k, kg in parts:
        out_specs.append(pl.BlockSpec((s, HEAD), lambda h, i, kg=kg: (0, h // kg)))
        out_shape.append(jax.ShapeDtypeStruct((s, nheads // kg * HEAD), F32))
    out_specs.append(pl.BlockSpec((s, HEAD), lambda h, i: (0, h // v_group)))
    out_shape.append(jax.ShapeDtypeStruct((s, nheads // v_group * HEAD), F32))
    res = pl.pallas_call(
        body, name=name, grid=(nheads, s // tq), in_specs=in_specs, out_specs=out_specs, out_shape=out_shape,
        compiler_params=_params("arbitrary", "arbitrary"))(*ins, v, lse, o, do)
    return list(res[:npart]), list(res[npart:2 * npart]), res[2 * npart]


def _band_windows(lf):
    for ib in range(lf // HEAD):
        q0 = ib * HEAD
        yield q0, max(0, q0 - B_HALF), min(lf, q0 + HEAD + B_HALF)


def _band_mask(q0, lo, hi):
    qpos = q0 + lax.broadcasted_iota(jnp.int32, (HEAD, hi - lo), 0)
    kpos = lo + lax.broadcasted_iota(jnp.int32, (HEAD, hi - lo), 1)
    return jnp.abs(qpos - kpos) <= B_HALF


def _class_rows(r, d, start, size):
    return pl.ds(r + d * start, size, stride=d) if d > 1 else pl.ds(start, size)


def dilated_fwd(name, q, k, v_src, v_block):
    s = k.shape[0]
    scale = HEAD ** -0.5

    def body(q_ref, k_ref, v_ref, y_ref, lse_ref, o_scr, l_scr):
        g_now = pl.program_id(1)
        for g, d in enumerate(B_DILATIONS):
            @pl.when(g_now == g)
            def _(g=g, d=d):
                for r in range(d):
                    for q0, lo, hi in _band_windows(s // d):
                        mine = _class_rows(r, d, q0, HEAD)
                        keys = _class_rows(r, d, lo, hi - lo)
                        sc = _dot(q_ref[mine, :].astype(BF16), k_ref[keys, :].astype(BF16), NT) * scale
                        sc = jnp.where(_band_mask(q0, lo, hi), sc, -1e30)
                        m = jnp.max(sc, axis=-1, keepdims=True)
                        e = jnp.exp(sc - m)
                        l = jnp.sum(e, axis=-1, keepdims=True)
                        o_scr.at[g][mine, :] = _dot((e / l).astype(BF16), v_ref[keys, :].astype(BF16))
                        l_scr.at[g][mine, :] = jnp.broadcast_to(m + jnp.log(l), (HEAD, HEAD))

        @pl.when(g_now == B_GROUPS - 1)
        def _():
            a, b, c = l_scr[0], l_scr[1], l_scr[2]
            m = jnp.maximum(jnp.maximum(a, b), c)
            ea, eb, ec = jnp.exp(a - m), jnp.exp(b - m), jnp.exp(c - m)
            den = ea + eb + ec
            y_ref[...] = (ea / den) * o_scr[0] + (eb / den) * o_scr[1] + (ec / den) * o_scr[2]
            lse_ref[...] = m + jnp.log(den)

    blk = lambda f: pl.BlockSpec((s, HEAD), f)
    per_head = blk(lambda h, g: (0, h))
    return pl.pallas_call(
        body, name=name, grid=(B_HEADS, B_GROUPS),
        in_specs=[blk(lambda h, g: (0, B_HEADS * g + h)), per_head, blk(lambda h, g: (0, v_block + h))],
        out_specs=[per_head, per_head], out_shape=[jax.ShapeDtypeStruct((s, B_W), F32)] * 2,
        scratch_shapes=[pltpu.VMEM((B_GROUPS, s, HEAD), F32)] * 2,
        compiler_params=_params("parallel", "arbitrary"))(q, k, v_src)


def dilated_bwd(name, q, k, v_src, v_block, dy, y, lse):
    s = k.shape[0]
    scale = HEAD ** -0.5

    def body(q_ref, k_ref, v_ref, dy_ref, y_ref, lse_ref, dq_ref, dk_ref, dv_ref, delta):
        g_now = pl.program_id(1)

        @pl.when(g_now == 0)
        def _():
            dk_ref[...] = jnp.zeros_like(dk_ref)
            dv_ref[...] = jnp.zeros_like(dv_ref)
            delta[...] = jnp.broadcast_to(jnp.sum(dy_ref[...] * y_ref[...], axis=-1, keepdims=True), (s, HEAD))

        for g, d in enumerate(B_DILATIONS):
            @pl.when(g_now == g)
            def _(d=d):
                for r in range(d):
                    for q0, lo, hi in _band_windows(s // d):
                        mine = _class_rows(r, d, q0, HEAD)
                        keys = _class_rows(r, d, lo, hi - lo)
                        qv, kv = q_ref[mine, :].astype(BF16), k_ref[keys, :].astype(BF16)
                        dyv = dy_ref[mine, :].astype(BF16)
                        sc = _dot(qv, kv, NT) * scale
                        pr = jnp.where(_band_mask(q0, lo, hi), jnp.exp(sc - lse_ref[mine, :][:, 0:1]), 0.0)
                        dp = _dot(dyv, v_ref[keys, :].astype(BF16), NT)
                        ds = (pr * (dp - delta[mine, :][:, 0:1]) * scale).astype(BF16)
                        dv_ref[keys, :] = dv_ref[keys, :] + _dot(pr.astype(BF16), dyv, TN)
                        dq_ref[mine, :] = _dot(ds, kv)
                        dk_ref[keys, :] = dk_ref[keys, :] + _dot(ds, qv, TN)

    blk = lambda f: pl.BlockSpec((s, HEAD), f)
    per_head = blk(lambda h, g: (0, h))
    by_group = blk(lambda h, g: (0, B_HEADS * g + h))
    return pl.pallas_call(
        body, name=name, grid=(B_HEADS, B_GROUPS),
        in_specs=[by_group, per_head, blk(lambda h, g: (0, v_block + h)), per_head, per_head, per_head],
        out_specs=[by_group, per_head, per_head],
        out_shape=[jax.ShapeDtypeStruct((s, B_GROUPS * B_W), F32)] + [jax.ShapeDtypeStruct((s, B_W), F32)] * 2,
        scratch_shapes=[pltpu.VMEM((s, HEAD), F32)],
        compiler_params=_params("parallel", "arbitrary"))(q, k, v_src, dy, y, lse)


FFN_TN = 256


def _conv(h, cw):
    s = h.shape[0]
    row = lax.broadcasted_iota(jnp.int32, h.shape, 0)
    prev = jnp.where(row == 0, 0.0, pltpu.roll(h, 1, 0))
    nxt = jnp.where(row == s - 1, 0.0, pltpu.roll(h, s - 1, 0))
    return prev * cw[0:1, :] + h * cw[1:2, :] + nxt * cw[2:3, :] + cw[3:4, :], prev, nxt


def _sigmoid(x):
    return 1.0 / (1.0 + jnp.exp(-x))


def ffn_up(name, xn, w_up, cwb):
    s, dm = xn.shape
    quarter = w_up.shape[3]
    fh = 2 * quarter
    tn = FFN_TN
    per = quarter // tn

    def body(x_ref, w_ref, cw_ref, h_ref, act_ref):
        xv = x_ref[...]
        hg = _dot(xv, w_ref[0])
        hu = _dot(xv, w_ref[1])
        h_ref[0] = hg
        h_ref[1] = hu
        gc, _, _ = _conv(hg, cw_ref[0])
        uc, _, _ = _conv(hu, cw_ref[1])
        act_ref[...] = (gc * _sigmoid(gc) * uc).astype(BF16)

    return pl.pallas_call(
        body, name=name, grid=(fh // tn,),
        in_specs=[pl.BlockSpec((s, dm), lambda t: (0, 0), pipeline_mode=pl.Buffered(1)),
                  pl.BlockSpec((2, None, dm, tn), lambda t: (0, t // per, 0, t % per)),
                  pl.BlockSpec((2, 8, tn), lambda t: (0, 0, t))],
        out_specs=[pl.BlockSpec((2, s, tn), lambda t: (0, 0, t)), pl.BlockSpec((s, tn), lambda t: (0, t))],
        out_shape=[jax.ShapeDtypeStruct((2, s, fh), F32), jax.ShapeDtypeStruct((s, fh), BF16)],
        compiler_params=_params("parallel"))(xn, w_up, cwb)


def ffn_gate_bwd(name, h, dact, cwb):
    _, s, fh = h.shape
    tn = FFN_TN

    def body(h_ref, da_ref, cw_ref, dh_ref, dcw_ref):
        gc, gp, gn = _conv(h_ref[0], cw_ref[0])
        uc, up, un = _conv(h_ref[1], cw_ref[1])
        sg = _sigmoid(gc)
        da = da_ref[...]
        dgc = da * uc * (sg * (1.0 + gc * (1.0 - sg)))
        duc = da * (gc * sg)
        row = lax.broadcasted_iota(jnp.int32, da.shape, 0)
        for idx, (hv, prev, nxt, dc) in enumerate(((h_ref[0], gp, gn, dgc), (h_ref[1], up, un, duc))):
            cw = cw_ref[idx]
            from_next = jnp.where(row == s - 1, 0.0, pltpu.roll(dc, s - 1, 0))
            from_prev = jnp.where(row == 0, 0.0, pltpu.roll(dc, 1, 0))
            dh_ref[idx] = (from_next * cw[0:1, :] + dc * cw[1:2, :] + from_prev * cw[2:3, :]).astype(BF16)
            dcw_ref[idx, 0:1, :] = jnp.sum(prev * dc, axis=0, keepdims=True)
            dcw_ref[idx, 1:2, :] = jnp.sum(hv * dc, axis=0, keepdims=True)
            dcw_ref[idx, 2:3, :] = jnp.sum(nxt * dc, axis=0, keepdims=True)
            dcw_ref[idx, 3:4, :] = jnp.sum(dc, axis=0, keepdims=True)
            dcw_ref[idx, 4:8, :] = jnp.zeros((4, tn), F32)

    return pl.pallas_call(
        body, name=name, grid=(fh // tn,),
        in_specs=[pl.BlockSpec((2, s, tn), lambda t: (0, 0, t)), pl.BlockSpec((s, tn), lambda t: (0, t)),
                  pl.BlockSpec((2, 8, tn), lambda t: (0, 0, t))],
        out_specs=[pl.BlockSpec((2, s, tn), lambda t: (0, 0, t)), pl.BlockSpec((2, 8, tn), lambda t: (0, 0, t))],
        out_shape=[jax.ShapeDtypeStruct((2, s, fh), BF16), jax.ShapeDtypeStruct((2, 8, fh), F32)],
        compiler_params=_params("parallel"))(h, dact, cwb)


def ffn_dx(name, dh, w_up):
    _, s, fh = dh.shape
    dm, quarter = w_up.shape[2], w_up.shape[3]
    tk = _tile(quarter, 2816)
    per = quarter // tk
    tm, tn = _tile(s, 1024), _tile(dm, 1024)
    grid = (s // tm, dm // tn, 4 * per)
    a_spec = pl.BlockSpec((None, tm, tk), lambda i, j, k: (k // (2 * per), i, k % (2 * per)))
    b_spec = pl.BlockSpec((None, None, tn, tk), lambda i, j, k: (k // (2 * per), (k // per) % 2, j, k % per))
    o_spec = pl.BlockSpec((tm, tn), lambda i, j, k: (i, j))
    return _matmul_call(name, NT, grid, a_spec, b_spec, o_spec, jax.ShapeDtypeStruct((s, dm), F32), (tm, tn))(dh, w_up)


def ffn_dw_up(name, xn, dh):
    _, s, fh = dh.shape
    dm = xn.shape[1]
    quarter = fh // 2
    tn = _tile(quarter, 1536)
    per = quarter // tn
    tm, tk = _tile(dm, 1024), _tile(s, 2048)
    grid = (dm // tm, 4 * per, s // tk)
    a_spec = pl.BlockSpec((tk, tm), lambda i, j, k: (k, i))
    b_spec = pl.BlockSpec((None, tk, tn), lambda i, j, k: (j // (2 * per), k, j % (2 * per)))
    o_spec = pl.BlockSpec((None, None, tm, tn), lambda i, j, k: (j // (2 * per), (j // per) % 2, i, j % per))
    return _matmul_call(name, TN, grid, a_spec, b_spec, o_spec,
                        jax.ShapeDtypeStruct((2, 2, dm, quarter), F32), (tm, tn))(xn, dh)


def loss_head(name, y, target):
    s, dm = y.shape
    ts = _rows(s, dm)

    def body(y_ref, t_ref, loss_ref, dy_ref, acc):
        i = pl.program_id(0)

        @pl.when(i == 0)
        def _():
            acc[...] = jnp.zeros_like(acc)

        err = y_ref[...] - t_ref[...]
        dy_ref[...] = err * (1.0 / dm)
        acc[...] += jnp.sum(err * err, axis=0, keepdims=True)

        @pl.when(i == s // ts - 1)
        def _():
            loss_ref[...] = jnp.broadcast_to(jnp.sum(acc[...], axis=-1, keepdims=True) * (0.5 / dm), (1, LANES))

    blk = pl.BlockSpec((ts, dm), lambda i: (i, 0))
    loss, dy = pl.pallas_call(
        body, name=name, grid=(s // ts,), in_specs=[blk, blk],
        out_specs=[pl.BlockSpec((1, LANES), lambda i: (0, 0)), blk],
        out_shape=[jax.ShapeDtypeStruct((1, LANES), F32), jax.ShapeDtypeStruct((s, dm), F32)],
        scratch_shapes=[pltpu.VMEM((1, dm), F32)], compiler_params=_params("arbitrary"))(y, target)
    return loss[0, 0], dy


def attn_fwd_c(name, qc, kvc, krc, scale):
    s = qc.shape[0]
    tq = ATTN_TQ

    def body(qn_ref, qr_ref, kn_ref, kr_ref, v_ref, o_ref, lse_ref):
        sc = (_dot(qn_ref[...], kn_ref[...], NT) + _dot(qr_ref[...], kr_ref[...], NT)) * scale
        m = jnp.max(sc, axis=-1, keepdims=True)
        e = jnp.exp(sc - m)
        l = jnp.sum(e, axis=-1, keepdims=True)
        o_ref[...] = _dot(e.astype(BF16), v_ref[...]) / l
        lse_ref[...] = jnp.broadcast_to(m + jnp.log(l), (tq, HEAD))

    qb = lambda off: pl.BlockSpec((tq, HEAD), lambda h, i: (i, off + h))
    kb = lambda f: pl.BlockSpec((s, HEAD), f)
    out_blk = pl.BlockSpec((tq, HEAD), lambda h, i: (i, h))
    return pl.pallas_call(
        body, name=name, grid=(C_HEADS, s // tq),
        in_specs=[qb(0), qb(C_HEADS), kb(lambda h, i: (0, 2 * h)), kb(lambda h, i: (0, 0)), kb(lambda h, i: (0, 2 * h + 1))],
        out_specs=[out_blk, out_blk], out_shape=[jax.ShapeDtypeStruct((s, C_W), F32)] * 2,
        compiler_params=_params("parallel", "parallel"))(qc, qc, kvc, krc, kvc)


def attn_bwd_c(name, qc, kvc, krc, lse, o, do, scale):
    s = qc.shape[0]
    tq = ATTN_TQ

    def body(qn_ref, qr_ref, kn_ref, kr_ref, v_ref, lse_ref, o_ref, do_ref, dqn_ref, dqr_ref, dkv_ref, dkr_ref):
        h, i = pl.program_id(0), pl.program_id(1)
        sc = (_dot(qn_ref[...], kn_ref[...], NT) + _dot(qr_ref[...], kr_ref[...], NT)) * scale
        pr = jnp.exp(sc - lse_ref[:, 0:1])
        dov = do_ref[...]
        delta = jnp.sum(dov.astype(F32) * o_ref[...], axis=-1, keepdims=True)
        dp = _dot(dov, v_ref[...], NT)
        ds = (pr * (dp - delta) * scale).astype(BF16)

        @pl.when(i == 0)
        def _():
            dkv_ref[...] = jnp.zeros_like(dkv_ref)

        @pl.when((i == 0) & (h == 0))
        def _():
            dkr_ref[...] = jnp.zeros_like(dkr_ref)

        dkv_ref[:, HEAD:] += _dot(pr.astype(BF16), dov, TN)
        dqn_ref[...] = _dot(ds, kn_ref[...])
        dqr_ref[...] = _dot(ds, kr_ref[...])
        dkv_ref[:, :HEAD] += _dot(ds, qn_ref[...], TN)
        dkr_ref[...] += _dot(ds, qr_ref[...], TN)

    qb = lambda off: pl.BlockSpec((tq, HEAD), lambda h, i: (i, off + h))
    kb = lambda f: pl.BlockSpec((s, HEAD), f)
    hq = pl.BlockSpec((tq, HEAD), lambda h, i: (i, h))
    kn_map, v_map, kr_map = (lambda h, i: (0, 2 * h)), (lambda h, i: (0, 2 * h + 1)), (lambda h, i: (0, 0))
    return pl.pallas_call(
        body, name=name, grid=(C_HEADS, s // tq),
        in_specs=[qb(0), qb(C_HEADS), kb(kn_map), kb(kr_map), kb(v_map), hq, hq, hq],
        out_specs=[hq, hq, pl.BlockSpec((s, 2 * HEAD), lambda h, i: (0, h)), kb(kr_map)],
        out_shape=[jax.ShapeDtypeStruct((s, C_W), F32)] * 2
        + [jax.ShapeDtypeStruct((s, 2 * C_W), F32), jax.ShapeDtypeStruct((s, HEAD), F32)],
        compiler_params=_params("arbitrary", "arbitrary"))(qc, qc, kvc, krc, kvc, lse, o, do)


def _layer_fwd(x, p, tabs):
    sv = _mixers_ab_fwd(x, p, tabs)
    x1 = _mixer_c_out_fwd(x, sv, p, tabs)
    return _ffn_fwd(x1, p, sv), sv


def _mixers_ab_fwd(x, p, tabs):
    sv = {"x": x}
    hn = rmsnorm_fwd("attn_norm", x, p["attn_norm"], x.shape[1], BF16)
    proj = matmul("in_proj", hn, p["w_in_t"], "nt", F32, tm=256)
    qa, ka, av, qb, kb, cqn, ckvn = mixer_prep("mixer_prep", proj, p, tabs)
    sv.update(hn=hn, proj=proj, qa=qa, ka=ka, av=av, qb=qb, kb=kb, cqn=cqn, ckvn=ckvn)

    ya, lse_a = attn_fwd("a_attn", [(qa, 0, ka, A_HEADS // A_KV)], av, A_HEADS // A_KV, A_HEADS, HEAD ** -0.5)
    yb, lse_b = dilated_fwd("b_dilated", qb, kb, proj, O_BV // HEAD)
    sv.update(ya=ya, lse_a=lse_a, yb=yb, lse_b=lse_b)
    return sv


def _mixer_c_out_fwd(x, sv, p, tabs):
    ya, yb = sv["ya"], sv["yb"]
    qc_raw = matmul("c_uq", sv["cqn"], p["w_uq_t"], "nt", F32)
    qc = jnp.concatenate([qc_raw[:, :C_W].astype(BF16), rope("c_q_rope", qc_raw[:, C_W:], tabs["c"][0], BF16)], axis=1)
    kvc = matmul("c_ukv", sv["ckvn"], p["w_ukv"], "nn", BF16)
    ckr = jnp.pad(sv["proj"][:, O_CKR:], ((0, 0), (0, HEAD - C_ROPE)))
    krc = rope("c_k_rope", ckr, tabs["c"][0], BF16)
    c_scale = (HEAD + C_ROPE) ** -0.5
    yc, lse_c = attn_fwd_c("c_attn", qc, kvc, krc, c_scale)
    sv.update(qc=qc, kvc=kvc, krc=krc, yc=yc, lse_c=lse_c)

    y = out_norms("out_norms", ya, yb, yc, p["out_norm"])
    x1 = matmul("out_proj", y, p["w_out"], "nn", F32, add=x)
    sv.update(y=y, x1=x1)
    return x1


def _ffn_fwd(x1, p, sv):
    xn = rmsnorm_fwd("ffn_norm", x1, p["ffn_norm"], x1.shape[1], BF16)
    h, act = ffn_up("ffn_up", xn, p["w_up"], p["cwb"])
    x2 = matmul("ffn_down", act, p["w_down"], "nn", F32, add=x1)
    sv.update(xn=xn, h=h, act=act)
    return x2


def _layer_bwd(dx2, sv, p, tabs):
    dxn, g = _ffn_bwd(dx2, sv, p)
    dx1, dy, g_out = _out_proj_bwd(dxn, dx2, sv, p)
    dhn, g_mix = _mixers_bwd(dy, sv, p, tabs)
    dx, g_norm = _attn_norm_bwd(dhn, dx1, sv, p)
    return dx, {**g, **g_out, **g_mix, **g_norm}


def _ffn_bwd(dx2, sv, p, after=0.0):
    g = {}
    dx2b = (dx2 + after).astype(BF16)
    dact = matmul("ffn_down_dx", dx2b, p["w_down"], "nt", F32)
    g["w_down"] = matmul("ffn_down_dw", sv["act"], dx2b, "tn", F32)
    dh, g["cwb"] = ffn_gate_bwd("ffn_gate_bwd", sv["h"], dact, p["cwb"])
    dxn = ffn_dx("ffn_up_dx", dh, p["w_up"])
    g["w_up"] = ffn_dw_up("ffn_up_dw", sv["xn"], dh)
    return dxn, g


def _out_proj_bwd(dxn, dx2, sv, p, after=0.0):
    g = {}
    dm = dx2.shape[1]
    dx1, dx1b, g["ffn_norm"] = rmsnorm_bwd("ffn_norm_bwd", dxn, sv["x1"], p["ffn_norm"] + after, dm, (F32, BF16), add=dx2)
    dy = matmul("out_proj_dx", dx1b, p["w_out"], "nt", F32)
    g["w_out"] = matmul("out_proj_dw", sv["y"], dx1b, "tn", F32)
    return dx1, dy, g


def _attn_norm_bwd(dhn, dx1, sv, p, after=0.0):
    dx, d_gain = rmsnorm_bwd("attn_norm_bwd", dhn, sv["x"], p["attn_norm"] + after, dhn.shape[1], (F32,), add=dx1)
    return dx, {"attn_norm": d_gain}


def _mixers_bwd(dy, sv, p, tabs, after=0.0):
    g = {}
    dya, dyb, dyc, g["out_norm"] = out_norms_bwd("out_norms_bwd", dy, sv["ya"], sv["yb"], sv["yc"], p["out_norm"] + after)

    c_scale = (HEAD + C_ROPE) ** -0.5
    dqn, dqr, dkv, dkr = attn_bwd_c("c_attn_bwd", sv["qc"], sv["kvc"], sv["krc"], sv["lse_c"], sv["yc"], dyc, c_scale)
    dqc = jnp.concatenate([dqn.astype(BF16), rope("c_q_rope_bwd", dqr, tabs["c"][1], BF16)], axis=1)
    dcqn = matmul("c_uq_dx", dqc, p["w_uq_t"], "nn", F32)
    g["w_uq"] = matmul("c_uq_dw", dqc, sv["cqn"], "tn", F32)
    dkvc = dkv.astype(BF16)
    dckvn = matmul("c_ukv_dx", dkvc, p["w_ukv"], "nt", F32)
    g["w_ukv"] = matmul("c_ukv_dw", sv["ckvn"], dkvc, "tn", F32)
    dckr = rope("c_k_rope_bwd", dkr, tabs["c"][1], BF16)

    dqb, dkb, dvb = dilated_bwd("b_dilated_bwd", sv["qb"], sv["kb"], sv["proj"], O_BV // HEAD, dyb, sv["yb"], sv["lse_b"])

    kg = A_HEADS // A_KV
    (dqa,), (dka,), dva = attn_bwd("a_attn_bwd", [(sv["qa"], 0, sv["ka"], kg)], sv["av"], kg, sv["lse_a"], sv["ya"],
                                   dya, A_HEADS, HEAD ** -0.5)

    dproj, g["a_q_norm"], g["a_k_norm"], g["c_q_norm"], g["c_kv_norm"] = mixer_prep_bwd(
        "mixer_prep_bwd", sv["proj"], p, tabs, dqa, dka, dva, dqb, dkb, dvb, dcqn, dckvn, dckr)
    dhn = matmul("in_proj_dx", dproj, p["w_in_t"], "nn", F32, tm=512)
    g["w_in"] = matmul("in_proj_dw", dproj, sv["hn"], "tn", F32, tn=512, tk=512)
    return dhn, g


def _local_step(x, target, layers, final_norm, tabs):
    saved = []
    for p in layers:
        x, sv = _layer_fwd(x, p, tabs)
        saved.append(sv)
    dm = x.shape[1]
    yf = rmsnorm_fwd("final_norm", x, final_norm, dm, F32)
    loss, dyf = loss_head("loss_head", yf, target)
    dx, d_final = rmsnorm_bwd("final_norm_bwd", dyf, x, final_norm, dm, (F32,))
    grads = [None] * len(layers)
    for l in reversed(range(len(layers))):
        dx, grads[l] = _layer_bwd(dx, saved[l], layers[l], tabs)
    return loss, dx, grads, d_final


HBM = pl.BlockSpec(memory_space=pl.ANY)


def _place():
    x, y, c = lax.axis_index("x"), lax.axis_index("y"), lax.axis_index("c")
    others = [(1 - x, y), (x, 1 - y), (1 - x, 1 - y)]
    return x, y, c, 2 * x + y, others


def _remote(src, dst, send_sem, recv_sem, device):
    return pltpu.make_async_remote_copy(src_ref=src, dst_ref=dst, send_sem=send_sem, recv_sem=recv_sem,
                                        device_id=device, device_id_type=MESH)


def gather_small(name, small):
    def body(in_ref, out_ref, send_sems, recv_sems, local_sem):
        x, y, c, chip, others = _place()
        mine = pltpu.make_async_copy(in_ref, out_ref.at[chip], local_sem)
        mine.start()
        copies = []
        for k, (ox, oy) in enumerate(others):
            cp = _remote(in_ref, out_ref.at[chip], send_sems.at[k], recv_sems.at[k], (ox, oy, c))
            cp.start()
            copies.append(cp)
        for k, (ox, oy) in enumerate(others):
            landed = out_ref.at[2 * ox + oy]
            _remote(landed, landed, send_sems.at[k], recv_sems.at[k], (ox, oy, c)).wait_recv()
        for cp in copies:
            cp.wait_send()
        mine.wait()

    return pl.pallas_call(
        body, name=name, in_specs=[HBM], out_specs=HBM, out_shape=jax.ShapeDtypeStruct((4,) + small.shape, small.dtype),
        scratch_shapes=[pltpu.SemaphoreType.DMA((3,)), pltpu.SemaphoreType.DMA((3,)), pltpu.SemaphoreType.DMA],
        compiler_params=pltpu.CompilerParams(has_side_effects=True))(small)


IN_HBM = pl.BlockSpec(memory_space=pltpu.HBM)
SEMS = pl.BlockSpec(memory_space=pltpu.SEMAPHORE)
DATAFLOW = pltpu.SideEffectType.DATAFLOW_SIDE_EFFECTING


BF16_ROWS_PER_TILE = 16


def _halves_by_rows(shape):
    return (shape[-2] // 2) % BF16_ROWS_PER_TILE == 0


def _half(shape, core):
    if _halves_by_rows(shape):
        size = shape[-2] // 2
        return (pl.ds(core * size, size), slice(None))
    size = shape[-1] // 2
    return (slice(None), pl.ds(core * size, size))


def _half_shape(shape):
    r, c = shape[-2:]
    return (r // 2, c) if _halves_by_rows(shape) else (r, c // 2)


def _gather_plan(bufs):
    x, y, c, chip, others = _place()
    plan = []
    for t, ref in enumerate(bufs):
        mine = _half(ref.shape, c)
        for k, (ox, oy) in enumerate(others):
            plan.append((3 * t + k, (ox, oy, c), ref.at[(chip,) + mine], ref.at[(chip,) + mine],
                         ref.at[(2 * ox + oy,) + mine]))
    return plan


def _exchange_plan(bufs):
    x, y, c, chip, others = _place()
    n = len(bufs) // 2
    plan = []
    for t in range(n):
        for k, (ox, oy) in enumerate(others):
            plan.append((3 * t + k, (ox, oy, c), bufs[t].at[2 * ox + oy], bufs[n + t].at[chip], bufs[n + t].at[2 * ox + oy]))
    return plan


def _sibling_plan(bufs):
    x, y, c, chip, others = _place()
    n = len(bufs) // 2
    plan = []
    for t in range(n):
        plan.append((t, (x, y, 1 - c), bufs[t].at[(slice(None),) + _half(bufs[t].shape, 1 - c)], bufs[n + t], bufs[n + t]))
    return plan


PLAN_COPIES = {_gather_plan: lambda n: 3 * n, _exchange_plan: lambda n: 3 * (n // 2), _sibling_plan: lambda n: n // 2}


def split_start(name, bufs, plan_of, after):
    n = len(bufs)

    def body(*refs):
        ins = refs[:n]
        send_sems, recv_sems = refs[n + 1], refs[n + 2]
        token = refs[-1]
        for idx, peer, src, dst, _ in plan_of(ins):
            _remote(src, dst, send_sems.at[idx], recv_sems.at[idx], peer).start()
        token[...] = jnp.zeros_like(token)

    copies = PLAN_COPIES[plan_of](n)
    res = pl.pallas_call(
        body, name=name, in_specs=[IN_HBM] * n + [HBM],
        out_specs=(SEMS, SEMS) + (IN_HBM,) * n + (pl.BlockSpec(memory_space=pltpu.VMEM),),
        out_shape=(pltpu.SemaphoreType.DMA((copies,)), pltpu.SemaphoreType.DMA((copies,)))
        + tuple(pltpu.HBM(b.shape, b.dtype) for b in bufs) + (jax.ShapeDtypeStruct((8, LANES), F32),),
        input_output_aliases={t: 2 + t for t in range(n)},
        compiler_params=pltpu.CompilerParams(has_side_effects=DATAFLOW))(
            *[pltpu.with_memory_space_constraint(b, pltpu.HBM) for b in bufs], after)
    return res[0], res[1], list(res[2:2 + n]), res[-1]


def split_wait(name, send_sems, recv_sems, flying, plan_of, after):
    n = len(flying)

    def body(*refs):
        ins = refs[:n]
        send_ref, recv_ref = refs[n], refs[n + 1]
        for idx, peer, src, dst, landing in plan_of(ins):
            _remote(src, dst, send_ref.at[idx], recv_ref.at[idx], peer).wait_send()
            _remote(landing, landing, send_ref.at[idx], recv_ref.at[idx], peer).wait_recv()

    return list(pl.pallas_call(
        body, name=name, in_specs=[IN_HBM] * n + [SEMS, SEMS] + [HBM] * len(after), out_specs=(IN_HBM,) * n,
        out_shape=tuple(pltpu.HBM(b.shape, b.dtype) for b in flying),
        input_output_aliases={t: t for t in range(n)},
        compiler_params=pltpu.CompilerParams(has_side_effects=DATAFLOW))(*flying, send_sems, recv_sems, *after))


def forward_halves(name, slots):
    n = len(slots)

    def body(*refs):
        ins, outs = refs[:n], refs[n:2 * n]
        send_sems, recv_sems = refs[2 * n:]
        x, y, c, chip, others = _place()
        sibling = (x, y, 1 - c)
        copies = []
        for t in range(n):
            mine = _half(slots[t].shape, c)
            for k, (ox, oy) in enumerate(others):
                cp = _remote(ins[t].at[(2 * ox + oy,) + mine], outs[t].at[(2 * ox + oy,) + mine], send_sems.at[t, k],
                             recv_sems.at[t, k], sibling)
                cp.start()
                copies.append(cp)
        for t in range(n):
            for k, (ox, oy) in enumerate(others):
                theirs = outs[t].at[(2 * ox + oy,) + _half(slots[t].shape, 1 - c)]
                _remote(theirs, theirs, send_sems.at[t, k], recv_sems.at[t, k], sibling).wait_recv()
        for cp in copies:
            cp.wait_send()

    return list(pl.pallas_call(
        body, name=name, in_specs=[HBM] * n, out_specs=[HBM] * n,
        out_shape=[jax.ShapeDtypeStruct(s.shape, s.dtype) for s in slots],
        input_output_aliases={t: t for t in range(n)},
        scratch_shapes=[pltpu.SemaphoreType.DMA((n, 3))] * 2,
        compiler_params=pltpu.CompilerParams(has_side_effects=True))(*slots))


def _half_block(shape, tr):
    r, c = shape[-2:]
    if _halves_by_rows(shape):
        per = r // 2 // tr
        return tr, c, lambda i, core: (core * per + i, 0)
    return tr, c // 2, lambda i, core: (i, core)


def add_half(name, place, grad, theirs):
    hr, hc = _half_shape(grad.shape)
    tr = _rows(hr, hc)
    br, bc, at = _half_block(grad.shape, tr)

    def body(place_ref, g_ref, t_ref, o_ref):
        o_ref[...] = (g_ref[...] + t_ref[...]).astype(o_ref.dtype)

    whole = pl.BlockSpec((None, br, bc), lambda j, i, pr: (j, i, 0))
    return pl.pallas_call(
        body, name=name,
        grid_spec=pltpu.PrefetchScalarGridSpec(
            num_scalar_prefetch=1, grid=(4, hr // tr),
            in_specs=[pl.BlockSpec((None, br, bc), lambda j, i, pr: (j,) + at(i, pr[0])), whole],
            out_specs=whole),
        out_shape=jax.ShapeDtypeStruct((4, hr, hc), BF16),
        compiler_params=_params("parallel", "parallel"))(place, grad, theirs)


def sum_chips(name, place, own, parts, shard_shape):
    _, hr, hc = parts.shape
    tr = _rows(hr, hc)
    br, bc, at = _half_block(shard_shape, tr)

    def body(place_ref, own_ref, p1, p2, p3, o_ref):
        o_ref[...] = own_ref[...].astype(F32) + p1[...].astype(F32) + p2[...].astype(F32) + p3[...].astype(F32)

    def slot(k):
        return pl.BlockSpec((None, br, bc), lambda i, pr: (lax.rem(pr[1] + k, 4), i, 0))

    return pl.pallas_call(
        body, name=name,
        grid_spec=pltpu.PrefetchScalarGridSpec(
            num_scalar_prefetch=1, grid=(hr // tr,), in_specs=[slot(0), slot(1), slot(2), slot(3)],
            out_specs=pl.BlockSpec((br, bc), lambda i, pr: at(i, pr[0]))),
        out_shape=jax.ShapeDtypeStruct(tuple(shard_shape), F32),
        compiler_params=_params("parallel"))(place, own, parts, parts, parts)


def join_halves(name, halves):
    n = len(halves)

    def body(*refs):
        ins, outs = refs[:n], refs[n:2 * n]
        send_sems, recv_sems = refs[2 * n:]
        x, y, c, _, _ = _place()
        copies = []
        for t in range(n):
            mine = _half(halves[t].shape, c)
            cp = _remote(ins[t].at[mine], outs[t].at[mine], send_sems.at[t], recv_sems.at[t], (x, y, 1 - c))
            cp.start()
            copies.append(cp)
        for t in range(n):
            theirs = outs[t].at[_half(halves[t].shape, 1 - c)]
            _remote(theirs, theirs, send_sems.at[t], recv_sems.at[t], (x, y, 1 - c)).wait_recv()
        for cp in copies:
            cp.wait_send()

    return pl.pallas_call(
        body, name=name, in_specs=[HBM] * n, out_specs=[HBM] * n,
        out_shape=[jax.ShapeDtypeStruct(h.shape, h.dtype) for h in halves],
        input_output_aliases={t: t for t in range(n)},
        scratch_shapes=[pltpu.SemaphoreType.DMA((n,))] * 2,
        compiler_params=pltpu.CompilerParams(has_side_effects=True))(*halves)


def allreduce_small(name, buf):
    rows = buf.shape[0]

    def body(in_ref, out_ref, slots, send_sems, recv_sems):
        x, y, c, _, _ = _place()
        me = 4 * x + 2 * y + c
        slots[me] = in_ref[...]
        peers = []
        for k in range(1, 8):
            px = 1 - x if k & 4 else x
            py = 1 - y if k & 2 else y
            pc = 1 - c if k & 1 else c
            peers.append((px, py, pc))
        copies = []
        for k, peer in enumerate(peers):
            cp = _remote(in_ref, slots.at[me], send_sems.at[k], recv_sems.at[k], peer)
            cp.start()
            copies.append(cp)
        for k, (px, py, pc) in enumerate(peers):
            slot = slots.at[4 * px + 2 * py + pc]
            _remote(slot, slot, send_sems.at[k], recv_sems.at[k], (px, py, pc)).wait_recv()
        for cp in copies:
            cp.wait_send()
        acc = slots[0]
        for d in range(1, 8):
            acc = acc + slots[d]
        out_ref[...] = acc

    vm = pl.BlockSpec(memory_space=pltpu.VMEM)
    return pl.pallas_call(
        body, name=name, in_specs=[vm], out_specs=vm, out_shape=jax.ShapeDtypeStruct(buf.shape, F32),
        scratch_shapes=[pltpu.VMEM((8, rows, LANES), F32), pltpu.SemaphoreType.DMA((7,)), pltpu.SemaphoreType.DMA((7,))],
        compiler_params=pltpu.CompilerParams(has_side_effects=True, vmem_limit_bytes=VMEM_LIMIT))(buf)


def cast_layer(name, place, w, layer):
    _, r, cols = w.shape
    tr = _rows(r, cols)

    def body(place_ref, w_ref, o_ref):
        o_ref[...] = w_ref[...].astype(BF16)

    return pl.pallas_call(
        body, name=name,
        grid_spec=pltpu.PrefetchScalarGridSpec(
            num_scalar_prefetch=1, grid=(r // tr,),
            in_specs=[pl.BlockSpec((None, tr, cols), lambda i, pr: (layer, i, 0))],
            out_specs=pl.BlockSpec((None, tr, cols), lambda i, pr: (pr[1], i, 0))),
        out_shape=jax.ShapeDtypeStruct((4, r, cols), BF16),
        compiler_params=_params("parallel"))(place, w)


def _adamw_math(g, w, m, v):
    m = ADAM_B1 * m + (1.0 - ADAM_B1) * g
    v = ADAM_B2 * v + (1.0 - ADAM_B2) * (g * g)
    m_hat = m / (1.0 - ADAM_B1 ** ADAM_STEP)
    v_hat = v / (1.0 - ADAM_B2 ** ADAM_STEP)
    delta = -ADAM_LR * (m_hat / (jnp.sqrt(v_hat) + ADAM_EPS) + ADAM_WD * w)
    return delta, m, v


def adamw_layer(name, layer, g, w, m, v, prev=None):
    nl, r, cols = w.shape
    tr = _rows(r, cols, 8)

    def body(*refs):
        g_ref, w_ref, m_ref, v_ref = refs[:4]
        og, od, om, ov = refs[-4:]
        gv = g_ref[...]
        delta, m2, v2 = _adamw_math(gv, w_ref[...], m_ref[...], v_ref[...])
        og[...] = gv
        od[...] = delta
        om[...] = m2
        ov[...] = v2

    lay = pl.BlockSpec((None, tr, cols), lambda i: (layer, i, 0))
    ins = [g, w, m, v] + (list(prev) if prev is not None else [])
    return pl.pallas_call(
        body, name=name, grid=(r // tr,),
        in_specs=[pl.BlockSpec((tr, cols), lambda i: (i, 0)), lay, lay, lay] + ([HBM] * 4 if prev is not None else []),
        out_specs=[lay] * 4, out_shape=[jax.ShapeDtypeStruct((nl, r, cols), F32)] * 4,
        input_output_aliases={4 + k: k for k in range(4)} if prev is not None else {},
        compiler_params=_params("parallel"))(*ins)


def adamw_packed(name, g, w, m, v):
    def body(g_ref, w_ref, m_ref, v_ref, od, om, ov):
        delta, m2, v2 = _adamw_math(g_ref[...], w_ref[...], m_ref[...], v_ref[...])
        od[...] = delta
        om[...] = m2
        ov[...] = v2

    vm = pl.BlockSpec(memory_space=pltpu.VMEM)
    return pl.pallas_call(
        body, name=name, in_specs=[vm] * 4, out_specs=[vm] * 3, out_shape=[jax.ShapeDtypeStruct(g.shape, F32)] * 3,
        compiler_params=pltpu.CompilerParams(vmem_limit_bytes=VMEM_LIMIT))(g, w, m, v)


def _pack(arrays):
    flat = jnp.concatenate([a.reshape(-1) for a in arrays])
    pad = (-flat.shape[0]) % (8 * LANES)
    return jnp.pad(flat, (0, pad)).reshape(-1, LANES)


def _unpack(buf, shapes):
    flat = buf.reshape(-1)
    out, off = [], 0
    for shp in shapes:
        size = int(np.prod(shp))
        out.append(flat[off:off + size].reshape(shp))
        off += size
    return out


MIXER_W = ("w_in", "w_uq", "w_ukv", "w_out")
FFN_W = ("w_up", "w_down")
BIG = MIXER_W + FFN_W
TRANSPOSED = ("w_in", "w_uq")
COLUMN_CUT = ("w_ukv", "w_up")
SMALL = ("attn_norm", "a_q_norm", "a_k_norm", "c_q_norm", "c_kv_norm", "out_norm", "ffn_norm", "conv_b", "final_norm")
WEIGHTS = ("attn_norm", "w_in", "a_q_norm", "a_k_norm", "c_q_norm", "c_kv_norm", "w_uq", "w_ukv", "out_norm", "w_out",
           "ffn_norm", "w_up", "conv_w", "conv_b", "w_down", "final_norm")
INPUTS = ("x",) + WEIGHTS + ("loss_target",) + tuple("m_" + n for n in WEIGHTS) + tuple("v_" + n for n in WEIGHTS)


def _columns(g):
    return jnp.transpose(g, (1, 0, 2)).reshape(g.shape[1], 4 * g.shape[2])


def _uncolumns(w):
    r, c4 = w.shape
    return jnp.transpose(w.reshape(r, 4, c4 // 4), (1, 0, 2))


def _compute_layouts(full, conv_w, small, layer):
    p = {n: small[n][layer] for n in SMALL if n != "final_norm"}
    if "w_in" in full:
        p["w_in_t"] = full["w_in"].reshape(-1, full["w_in"].shape[2])
    if "w_uq" in full:
        uq_t = full["w_uq"].reshape(C_HEADS, HEAD + C_ROPE, C_RANK)
        p["w_uq_t"] = jnp.concatenate([uq_t[:, :HEAD].reshape(C_W, C_RANK),
                                       jnp.pad(uq_t[:, HEAD:], ((0, 0), (0, HEAD - C_ROPE), (0, 0))).reshape(C_W, C_RANK)], axis=0)
        p["w_ukv"] = _columns(full["w_ukv"])
        p["w_out"] = full["w_out"].reshape(-1, full["w_out"].shape[2])
    if "w_up" in full:
        up = full["w_up"]
        p["w_up"] = up.reshape(2, 2, up.shape[1], up.shape[2])
        p["w_down"] = full["w_down"].reshape(-1, full["w_down"].shape[2])
        fh = conv_w.shape[1] // 2
        taps = jnp.transpose(conv_w.reshape(3, 2, fh), (1, 0, 2))
        p["cwb"] = jnp.concatenate([taps, small["conv_b"][layer].reshape(2, 1, fh), jnp.zeros((2, 4, fh), F32)], axis=1)
    return p


def _shard_layout(name, g):
    if name == "w_in":
        return g.reshape(4, -1, g.shape[1])
    if name == "w_uq":
        uq_t = jnp.concatenate([g[:C_W].reshape(C_HEADS, HEAD, C_RANK),
                                g[C_W:].reshape(C_HEADS, HEAD, C_RANK)[:, :C_ROPE]], axis=1)
        return uq_t.reshape(4, -1, C_RANK)
    if name == "w_ukv":
        return _uncolumns(g)
    if name == "w_up":
        return g.reshape(4, g.shape[2], g.shape[3])
    return g.reshape(4, -1, g.shape[1])


def _conv_grads(dcwb):
    fh = dcwb.shape[2]
    return jnp.transpose(dcwb[:, 0:3, :], (1, 0, 2)).reshape(3, 2 * fh), dcwb[:, 3, :].reshape(2 * fh)


def _shard_layouts(g):
    return ({n: _shard_layout(n, g[n]) for n in BIG},) + _conv_grads(g["cwb"])


def kernel(x, attn_norm, w_in, a_q_norm, a_k_norm, c_q_norm, c_kv_norm, w_uq, w_ukv, out_norm, w_out, ffn_norm, w_up, conv_w, conv_b, w_down, final_norm, loss_target, m_attn_norm, m_w_in, m_a_q_norm, m_a_k_norm, m_c_q_norm, m_c_kv_norm, m_w_uq, m_w_ukv, m_out_norm, m_w_out, m_ffn_norm, m_w_up, m_conv_w, m_conv_b, m_w_down, m_final_norm, v_attn_norm, v_w_in, v_a_q_norm, v_a_k_norm, v_c_q_norm, v_c_kv_norm, v_w_uq, v_w_ukv, v_out_norm, v_w_out, v_ffn_norm, v_w_up, v_conv_w, v_conv_b, v_w_down, v_final_norm):
    a = dict(zip(INPUTS, (x, attn_norm, w_in, a_q_norm, a_k_norm, c_q_norm, c_kv_norm, w_uq, w_ukv, out_norm, w_out, ffn_norm, w_up, conv_w, conv_b, w_down, final_norm, loss_target, m_attn_norm, m_w_in, m_a_q_norm, m_a_k_norm, m_c_q_norm, m_c_kv_norm, m_w_uq, m_w_ukv, m_out_norm, m_w_out, m_ffn_norm, m_w_up, m_conv_w, m_conv_b, m_w_down, m_final_norm, v_attn_norm, v_w_in, v_a_q_norm, v_a_k_norm, v_c_q_norm, v_c_kv_norm, v_w_uq, v_w_ukv, v_out_norm, v_w_out, v_ffn_norm, v_w_up, v_conv_w, v_conv_b, v_w_down, v_final_norm)))
    nl = w_in.shape[0]
    seq = x.shape[1]
    for n in TRANSPOSED:
        for kind in ("", "m_", "v_"):
            a[kind + n] = jnp.swapaxes(a[kind + n], 1, 2)
    chip = 2 * lax.axis_index("x") + lax.axis_index("y")
    place = jnp.stack([lax.axis_index("c"), chip]).astype(jnp.int32)
    tabs = _rope_tables(seq)

    assert nl == 2
    quarter = conv_w.shape[2]
    taps = jnp.pad(conv_w, ((0, 0), (0, 8 - conv_w.shape[1]), (0, 0))).reshape(nl * 8, quarter)
    taps_full = gather_small("gather_taps", taps).reshape(4, nl, 8, quarter)
    conv_full = [jnp.transpose(taps_full[:, l, 0:3, :], (1, 0, 2)).reshape(3, 4 * quarter) for l in range(nl)]
    flights = {}
    here, after = place, taps_full
    for l in range(nl):
        for key, names in ((f"in{l}", ("w_in",)), (f"mix{l}", MIXER_W[1:]), (f"ffn{l}", FFN_W)):
            slots = [cast_layer(f"cast{l}_{n}", here, a[n], l) for n in names]
            send_sems, recv_sems, flying, token = split_start(f"gather_{key}_start", slots, _gather_plan, after)
            flights[key] = (names, send_sems, recv_sems, flying)
            here = after = place + token[0, 0].astype(jnp.int32)
    started = token[0, 0]

    def land(key, after):
        names, send_sems, recv_sems, flying = flights[key]
        landed = split_wait(f"gather_{key}_wait", send_sems, recv_sems, flying, _gather_plan, after)
        return dict(zip(names, forward_halves(f"gather_{key}_forward", landed)))

    h = x[0]
    layers, saved = [], []
    for l in range(nl):
        p = _compute_layouts(land(f"in{l}", [taps_full if l == 0 else h]), conv_full[l], a, l)
        if l == 0:
            p["attn_norm"] = p["attn_norm"] + started
        sv = _mixers_ab_fwd(h, p, tabs)
        p.update(_compute_layouts(land(f"mix{l}", [sv["yb"]]), conv_full[l], a, l))
        x1 = _mixer_c_out_fwd(h, sv, p, tabs)
        p.update(_compute_layouts(land(f"ffn{l}", [x1]), conv_full[l], a, l))
        h = _ffn_fwd(x1, p, sv)
        layers.append(p)
        saved.append(sv)
    dm = h.shape[1]
    yf = rmsnorm_fwd("final_norm", h, final_norm, dm, F32)
    loss, dyf = loss_head("loss_head", yf, loss_target[0])
    loss = lax.psum(loss, ("x", "y", "c"))
    dx, d_final = rmsnorm_bwd("final_norm_bwd", dyf, h, final_norm, dm, (F32,))

    def swap_begin(tag, names, g):
        g_list = [_shard_layout(n, g[n]) for n in names]
        theirs = [lax.empty((4,) + _half_shape(s.shape), s.dtype) for s in g_list]
        send_sems, recv_sems, flying, token = split_start(f"{tag}_sibling_start", g_list + theirs, _sibling_plan, place)
        return (names, send_sems, recv_sems, flying), token[0, 0]

    def exchange_begin(tag, swap, after):
        names, send_sems, recv_sems, flying = swap
        landed = split_wait(f"{tag}_sibling_wait", send_sems, recv_sems, flying, _sibling_plan, after)
        g_list, theirs = landed[:len(names)], landed[len(names):]
        sums = [add_half(f"{tag}_add_{n}", place, gr, t) for n, gr, t in zip(names, g_list, theirs)]
        parts = [lax.empty(s.shape, s.dtype) for s in sums]
        send_sems, recv_sems, flying, token = split_start(f"{tag}_chips_start", sums + parts, _exchange_plan, place)
        return (names, [gr.shape[1:] for gr in g_list], send_sems, recv_sems, flying), token

    def exchange_end(tag, flight, after):
        names, shard_shapes, send_sems, recv_sems, flying = flight
        landed = split_wait(f"{tag}_chips_wait", send_sems, recv_sems, flying, _exchange_plan, after)
        sums, parts = landed[:len(names)], landed[len(names):]
        halves = [sum_chips(f"{tag}_sum_{n}", place, s, p, shp) for n, s, p, shp in zip(names, sums, parts, shard_shapes)]
        return dict(zip(names, join_halves(f"{tag}_join", halves)))

    grads, conv_grads, flights_f, flights_m = [None] * nl, [None] * nl, [None] * nl, [None] * nl
    after = 0.0
    for l in reversed(range(nl)):
        sv, p = saved[l], layers[l]
        dxn, g = _ffn_bwd(dx, sv, p, after)
        swap_f, after = swap_begin(f"reduce{l}f", FFN_W, g)
        dx1, dy, g_out = _out_proj_bwd(dxn, dx, sv, p, after)
        flights_f[l], token = exchange_begin(f"reduce{l}f", swap_f, [dy])
        dhn, g_mix = _mixers_bwd(dy, sv, p, tabs, token[0, 0])
        g.update(g_out)
        g.update(g_mix)
        swap_m, after = swap_begin(f"reduce{l}m", MIXER_W, g)
        dx, g_norm = _attn_norm_bwd(dhn, dx1, sv, p, after)
        g.update(g_norm)
        flights_m[l], token = exchange_begin(f"reduce{l}m", swap_m, [dx])
        after = token[0, 0]
        grads[l], conv_grads[l] = g, _conv_grads(g["cwb"])
    grad_x = dx

    out = {}
    reduced1 = exchange_end("reduce1f", flights_f[1], [token])
    for n in FFN_W:
        out[n] = adamw_layer(f"adamw1_{n}", 1, reduced1[n], a[n], a["m_" + n], a["v_" + n])
    reduced1 = exchange_end("reduce1m", flights_m[1], [out[n][3] for n in FFN_W])
    for n in MIXER_W:
        out[n] = adamw_layer(f"adamw1_{n}", 1, reduced1[n], a[n], a["m_" + n], a["v_" + n])

    small_g = {n: jnp.stack([grads[l][n] for l in range(nl)]) for n in SMALL if n not in ("conv_b", "final_norm")}
    small_g["conv_b"] = jnp.stack([cg[1] for cg in conv_grads])
    small_g["final_norm"] = d_final
    conv_w_g = jnp.stack([cg[0] for cg in conv_grads])
    shapes = [a[n].shape for n in SMALL] + [conv_w_g.shape]
    summed = _unpack(allreduce_small("reduce_small", _pack([small_g[n] for n in SMALL] + [conv_w_g])), shapes)
    small_g = dict(zip(SMALL, summed[:-1]))
    small_g["conv_w"] = lax.dynamic_slice_in_dim(summed[-1], chip * quarter, quarter, axis=2)
    names = SMALL + ("conv_w",)
    shapes = [a[n].shape for n in names]
    d_small, m_small, v_small = adamw_packed(
        "adamw_small", _pack([small_g[n] for n in names]), _pack([a[n] for n in names]),
        _pack([a["m_" + n] for n in names]), _pack([a["v_" + n] for n in names]))

    reduced0 = exchange_end("reduce0f", flights_f[0], [out[n][3] for n in BIG] + [d_small])
    for n in FFN_W:
        out[n] = adamw_layer(f"adamw0_{n}", 0, reduced0[n], a[n], a["m_" + n], a["v_" + n], prev=out[n])
    reduced0 = exchange_end("reduce0m", flights_m[0], [out[n][3] for n in FFN_W])
    for n in MIXER_W:
        out[n] = adamw_layer(f"adamw0_{n}", 0, reduced0[n], a[n], a["m_" + n], a["v_" + n], prev=out[n])
    for n, d_, m_, v_ in zip(names, _unpack(d_small, shapes), _unpack(m_small, shapes), _unpack(v_small, shapes)):
        out[n] = (small_g[n], d_, m_, v_)
    for n in TRANSPOSED:
        out[n] = [jnp.swapaxes(o, 1, 2) for o in out[n]]

    return (loss, grad_x[None]) + tuple(out[n][k] for k in range(4) for n in WEIGHTS)
```

```python
import functools

import jax
import jax.numpy as jnp
import numpy as np
from jax import lax
from jax.experimental import pallas as pl
from jax.experimental.pallas import tpu as pltpu

F32 = jnp.float32
BF16 = jnp.bfloat16
MESH = pl.DeviceIdType.MESH

V7X_VMEM_BYTES = 64 * 1024 * 1024
VMEM_LIMIT = V7X_VMEM_BYTES - 6 * 1024 * 1024
LANES = 128
ELEMENTWISE_BLOCK_BYTES = 4 * 1024 * 1024

HEAD = 128
A_HEADS, A_KV = 6, 2
B_HEADS, B_GROUPS = 4, 3
B_DILATIONS = (1, 4, 16)
B_HALF = 64
C_HEADS, C_RANK, C_ROPE = 6, 512, 64
GRID_W = 64
A_THETA, B_THETA, C_THETA = 10000.0, 500000.0, 10000.0
B_ROPE_DIM = 32
EPS = 1e-6
A_W, B_W, C_W = A_HEADS * HEAD, B_HEADS * HEAD, C_HEADS * HEAD
O_AQ, O_AK, O_AV = 0, 768, 1024
O_BQ, O_BK, O_BV = 1280, 2816, 3328
O_CQ, O_CKV, O_CKR = 3840, 4352, 4864
IN_W = 4928
PROJ_W = 5120

ADAM_LR, ADAM_B1, ADAM_B2, ADAM_EPS, ADAM_WD, ADAM_STEP = 0.001, 0.9, 0.999, 1e-08, 0.01, 10

NN = (((1,), (0,)), ((), ()))
NT = (((1,), (1,)), ((), ()))
TN = (((0,), (0,)), ((), ()))


def _dot(a, b, dims=NN):
    return lax.dot_general(a, b, dims, preferred_element_type=F32)


def _params(*sem):
    return pltpu.CompilerParams(dimension_semantics=sem if sem else None, vmem_limit_bytes=VMEM_LIMIT)


def _tile(n, target, unit=LANES):
    if n <= target:
        return n
    best = 0
    for t in range(unit, target + 1, unit):
        if n % t == 0:
            best = t
    return best if best else n


def _rows(r, width, itemsize=4):
    return _tile(r, max(8, ELEMENTWISE_BLOCK_BYTES // (width * itemsize)), 8)


def _matmul_call(name, dims, grid, a_spec, b_spec, o_spec, out_shape, acc_shape, add_spec=None):
    nk = grid[2]

    def body(*refs):
        a_ref, b_ref = refs[:2]
        add_ref = None if add_spec is None else refs[2]
        o_ref = refs[2 if add_spec is None else 3]

        def finish(r):
            if add_ref is not None:
                r = r + add_ref[...]
            o_ref[...] = r.astype(o_ref.dtype)

        if nk == 1:
            finish(_dot(a_ref[...], b_ref[...], dims))
            return
        acc = refs[-1]
        k = pl.program_id(2)

        @pl.when(k == 0)
        def _():
            acc[...] = _dot(a_ref[...], b_ref[...], dims)

        if nk > 2:
            @pl.when((k > 0) & (k < nk - 1))
            def _():
                acc[...] += _dot(a_ref[...], b_ref[...], dims)

        @pl.when(k == nk - 1)
        def _():
            finish(acc[...] + _dot(a_ref[...], b_ref[...], dims))

    in_specs = [a_spec, b_spec] + ([] if add_spec is None else [add_spec])
    return pl.pallas_call(
        body, name=name, grid=grid, in_specs=in_specs, out_specs=o_spec, out_shape=out_shape,
        scratch_shapes=[pltpu.VMEM(acc_shape, F32)] if nk > 1 else [],
        compiler_params=_params("parallel", "parallel", "arbitrary"))


def matmul(name, a, b, mode, out_dtype, add=None, tm=1024, tn=1024, tk=2816):
    if mode == "nn":
        (m, k), (k2, n) = a.shape, b.shape
    elif mode == "nt":
        (m, k), (n, k2) = a.shape, b.shape
    else:
        (k, m), (k2, n) = a.shape, b.shape
    assert k == k2, (name, a.shape, b.shape)
    tm, tn, tk = _tile(m, tm), _tile(n, tn), _tile(k, tk)
    grid = (m // tm, n // tn, k // tk)
    if mode == "tn":
        a_spec = pl.BlockSpec((tk, tm), lambda i, j, kk: (kk, i))
    else:
        a_spec = pl.BlockSpec((tm, tk), lambda i, j, kk: (i, kk))
    whole_b = {"pipeline_mode": pl.Buffered(1)} if grid[1] == 1 and grid[2] == 1 and grid[0] > 1 else {}
    if mode == "nt":
        b_spec = pl.BlockSpec((tn, tk), lambda i, j, kk: (j, kk), **whole_b)
    else:
        b_spec = pl.BlockSpec((tk, tn), lambda i, j, kk: (kk, j), **whole_b)
    o_spec = pl.BlockSpec((tm, tn), lambda i, j, kk: (i, j))
    dims = {"nn": NN, "nt": NT, "tn": TN}[mode]
    call = _matmul_call(name, dims, grid, a_spec, b_spec, o_spec, jax.ShapeDtypeStruct((m, n), out_dtype),
                        (tm, tn), None if add is None else o_spec)
    return call(a, b) if add is None else call(a, b, add)


def rmsnorm_fwd(name, x, g, width, out_dtype):
    r, cols = x.shape
    nb = cols // width
    tr = _rows(r, width)

    def body(x_ref, g_ref, o_ref):
        xv = x_ref[...]
        rs = lax.rsqrt(jnp.mean(xv * xv, axis=-1, keepdims=True) + EPS)
        o_ref[...] = (xv * rs * g_ref[...]).astype(o_ref.dtype)

    blk = pl.BlockSpec((tr, width), lambda i, j: (i, j))
    return pl.pallas_call(
        body, name=name, grid=(r // tr, nb),
        in_specs=[blk, pl.BlockSpec((1, width), lambda i, j: (0, 0))], out_specs=blk,
        out_shape=jax.ShapeDtypeStruct((r, cols), out_dtype),
        compiler_params=_params("parallel", "parallel"))(x, g.reshape(1, width))


def rmsnorm_bwd(name, dy, x, g, width, out_dtypes=(F32,), add=None):
    r, cols = x.shape
    nb = cols // width
    tr = _rows(r, width)
    n_out = len(out_dtypes)

    def body(*refs):
        dy_ref, x_ref, g_ref = refs[:3]
        add_ref = refs[3] if add is not None else None
        outs = refs[-(n_out + 1):-1]
        dg_ref = refs[-1]
        xv = x_ref[...]
        dyv = dy_ref[...].astype(F32)
        rs = lax.rsqrt(jnp.mean(xv * xv, axis=-1, keepdims=True) + EPS)
        xh = xv * rs
        dyg = dyv * g_ref[...]
        dx = rs * (dyg - xh * jnp.mean(dyg * xh, axis=-1, keepdims=True))
        if add_ref is not None:
            dx = dx + add_ref[...]
        for o in outs:
            o[...] = dx.astype(o.dtype)

        @pl.when((pl.program_id(0) == 0) & (pl.program_id(1) == 0))
        def _():
            dg_ref[...] = jnp.zeros_like(dg_ref)

        dg_ref[...] += jnp.sum(dyv * xh, axis=0, keepdims=True)

    blk = pl.BlockSpec((tr, width), lambda i, j: (i, j))
    vec = pl.BlockSpec((1, width), lambda i, j: (0, 0))
    ins = [dy, x, g.reshape(1, width)] + ([add] if add is not None else [])
    res = pl.pallas_call(
        body, name=name, grid=(r // tr, nb),
        in_specs=[blk, blk, vec] + ([blk] if add is not None else []),
        out_specs=[blk] * n_out + [vec],
        out_shape=[jax.ShapeDtypeStruct((r, cols), dt) for dt in out_dtypes] + [jax.ShapeDtypeStruct((1, width), F32)],
        compiler_params=_params("arbitrary", "arbitrary"))(*ins)
    return tuple(res[:n_out]) + (res[n_out].reshape(width),)


def _rope_angles(pos, dim, theta):
    inv = theta ** (-jnp.arange(0, dim, 2, dtype=F32) / dim)
    ang = pos.astype(F32)[:, None] * inv[None, :]
    return jnp.cos(ang), jnp.sin(ang)


def _rope_tables(s):
    rows = s // GRID_W
    row_pos = jnp.repeat(jnp.arange(rows), GRID_W)
    col_pos = jnp.tile(jnp.arange(GRID_W), rows)
    t_pos = jnp.arange(s)
    z = lambda n: jnp.zeros((s, n), F32)
    o = lambda n: jnp.ones((s, n), F32)
    cr, sr = _rope_angles(row_pos, HEAD // 2, A_THETA)
    cc, sc = _rope_angles(col_pos, HEAD // 2, A_THETA)
    tab_a = (jnp.concatenate([cr, cr, cc, cc], 1), jnp.concatenate([-sr, z(32), -sc, z(32)], 1),
             jnp.concatenate([z(32), sr, z(32), sc], 1), 32)
    cp, sp = _rope_angles(t_pos, B_ROPE_DIM, B_THETA)
    tab_b = (jnp.concatenate([cp, cp, o(96)], 1), jnp.concatenate([-sp, z(112)], 1),
             jnp.concatenate([z(16), sp, z(96)], 1), 16)
    cm, sm = _rope_angles(t_pos, C_ROPE, C_THETA)
    tab_c = (jnp.concatenate([cm, cm, o(64)], 1), jnp.concatenate([-sm, z(96)], 1),
             jnp.concatenate([z(32), sm, z(64)], 1), 32)

    def transposed(tab):
        c, s1, s2, h = tab
        return (c, jnp.roll(s2, -h, axis=1), jnp.roll(s1, h, axis=1), h)

    return {k: (t, transposed(t)) for k, t in (("a", tab_a), ("b", tab_b), ("c", tab_c))}


def rope(name, x, tab, out_dtype):
    c, s1, s2, h = tab
    s, cols = x.shape
    nh = cols // HEAD
    ts = _rows(s, HEAD)

    def body(x_ref, c_ref, s1_ref, s2_ref, o_ref):
        xv = x_ref[...].astype(F32)
        out = xv * c_ref[...] + pltpu.roll(xv, HEAD - h, 1) * s1_ref[...] + pltpu.roll(xv, h, 1) * s2_ref[...]
        o_ref[...] = out.astype(o_ref.dtype)

    blk = pl.BlockSpec((ts, HEAD), lambda i, j: (i, j))
    tb = pl.BlockSpec((ts, HEAD), lambda i, j: (i, 0))
    return pl.pallas_call(
        body, name=name, grid=(s // ts, nh), in_specs=[blk, tb, tb, tb], out_specs=blk,
        out_shape=jax.ShapeDtypeStruct((s, cols), out_dtype),
        compiler_params=_params("parallel", "parallel"))(x, c, s1, s2)


PREP_ROWS = 256


def _rope_of(x, tab, h):
    c, s1, s2 = tab
    return x * c + pltpu.roll(x, HEAD - h, 1) * s1 + pltpu.roll(x, h, 1) * s2


def _heads(start, n):
    return [(j, slice(start + HEAD * j, start + HEAD * (j + 1))) for j in range(n)]


def mixer_prep(name, proj, p, tabs):
    s = proj.shape[0]
    tr = PREP_ROWS
    (ca, s1a, s2a, ha), (cb, s1b, s2b, hb) = tabs["a"][0], tabs["b"][0]

    def body(proj_ref, gaq, gak, gcq, gckv, ca_r, s1a_r, s2a_r, cb_r, s1b_r, s2b_r,
             qa_ref, ka_ref, av_ref, qb_ref, kb_ref, cqn_ref, ckvn_ref):
        ta = (ca_r[...], s1a_r[...], s2a_r[...])
        tb = (cb_r[...], s1b_r[...], s2b_r[...])

        def normed(x, g):
            return x * lax.rsqrt(jnp.mean(x * x, axis=-1, keepdims=True) + EPS) * g

        for (src0, n, gain, out) in ((O_AQ, A_HEADS, gaq, qa_ref), (O_AK, A_KV, gak, ka_ref)):
            for j, cols in _heads(src0, n):
                out[:, HEAD * j:HEAD * (j + 1)] = _rope_of(normed(proj_ref[:, cols], gain[...]), ta, ha).astype(BF16)
        av_ref[...] = proj_ref[:, O_AV:O_BQ].astype(BF16)
        for (src0, n, out) in ((O_BQ, B_GROUPS * B_HEADS, qb_ref), (O_BK, B_HEADS, kb_ref)):
            for j, cols in _heads(src0, n):
                out[:, HEAD * j:HEAD * (j + 1)] = _rope_of(proj_ref[:, cols], tb, hb)
        cqn_ref[...] = normed(proj_ref[:, O_CQ:O_CKV], gcq[...]).astype(BF16)
        ckvn_ref[...] = normed(proj_ref[:, O_CKV:O_CKR], gckv[...]).astype(BF16)

    rows = lambda w: pl.BlockSpec((tr, w), lambda i: (i, 0))
    vec = lambda w: pl.BlockSpec((1, w), lambda i: (0, 0))
    widths = (A_W, A_KV * HEAD, A_KV * HEAD, B_GROUPS * B_W, B_W, C_RANK, C_RANK)
    dtypes = (BF16, BF16, BF16, F32, F32, BF16, BF16)
    return pl.pallas_call(
        body, name=name, grid=(s // tr,),
        in_specs=[rows(proj.shape[1]), vec(HEAD), vec(HEAD), vec(C_RANK), vec(C_RANK)] + [rows(HEAD)] * 6,
        out_specs=[rows(w) for w in widths],
        out_shape=[jax.ShapeDtypeStruct((s, w), dt) for w, dt in zip(widths, dtypes)],
        compiler_params=_params("parallel"))(
            proj, p["a_q_norm"].reshape(1, HEAD), p["a_k_norm"].reshape(1, HEAD), p["c_q_norm"].reshape(1, C_RANK),
            p["c_kv_norm"].reshape(1, C_RANK), ca, s1a, s2a, cb, s1b, s2b)


def mixer_prep_bwd(name, proj, p, tabs, dqa, dka, dva, dqb, dkb, dvb, dcqn, dckvn, dckr):
    s = proj.shape[0]
    tr = PREP_ROWS
    (ca, s1a, s2a, ha), (cb, s1b, s2b, hb) = tabs["a"][1], tabs["b"][1]

    def body(proj_ref, dqa_ref, dka_ref, dva_ref, dqb_ref, dkb_ref, dvb_ref, dcqn_ref, dckvn_ref, dckr_ref,
             gaq, gak, gcq, gckv, ca_r, s1a_r, s2a_r, cb_r, s1b_r, s2b_r, dproj_ref, dgaq, dgak, dgcq, dgckv):
        ta = (ca_r[...], s1a_r[...], s2a_r[...])
        tb = (cb_r[...], s1b_r[...], s2b_r[...])

        @pl.when(pl.program_id(0) == 0)
        def _():
            for ref in (dgaq, dgak, dgcq, dgckv):
                ref[...] = jnp.zeros_like(ref)

        def norm_bwd(dy, x, gain, dgain):
            rs = lax.rsqrt(jnp.mean(x * x, axis=-1, keepdims=True) + EPS)
            xh = x * rs
            dyg = dy * gain[...]
            dgain[...] += jnp.sum(dy * xh, axis=0, keepdims=True)
            return rs * (dyg - xh * jnp.mean(dyg * xh, axis=-1, keepdims=True))

        for (dst0, n, d_ref, gain, dgain) in ((O_AQ, A_HEADS, dqa_ref, gaq, dgaq), (O_AK, A_KV, dka_ref, gak, dgak)):
            for j, cols in _heads(dst0, n):
                dy = _rope_of(d_ref[:, HEAD * j:HEAD * (j + 1)], ta, ha)
                dproj_ref[:, cols] = norm_bwd(dy, proj_ref[:, cols], gain, dgain).astype(BF16)
        dproj_ref[:, O_AV:O_BQ] = dva_ref[...].astype(BF16)
        for (dst0, n, d_ref) in ((O_BQ, B_GROUPS * B_HEADS, dqb_ref), (O_BK, B_HEADS, dkb_ref)):
            for j, cols in _heads(dst0, n):
                dproj_ref[:, cols] = _rope_of(d_ref[:, HEAD * j:HEAD * (j + 1)], tb, hb).astype(BF16)
        dproj_ref[:, O_BV:O_CQ] = dvb_ref[...].astype(BF16)
        dproj_ref[:, O_CQ:O_CKV] = norm_bwd(dcqn_ref[...], proj_ref[:, O_CQ:O_CKV], gcq, dgcq).astype(BF16)
        dproj_ref[:, O_CKV:O_CKR] = norm_bwd(dckvn_ref[...], proj_ref[:, O_CKV:O_CKR], gckv, dgckv).astype(BF16)
        dproj_ref[:, O_CKR:] = dckr_ref[:, :C_ROPE]

    rows = lambda w: pl.BlockSpec((tr, w), lambda i: (i, 0))
    vec = lambda w: pl.BlockSpec((1, w), lambda i: (0, 0))
    grads = (dqa, dka, dva, dqb, dkb, dvb, dcqn, dckvn, dckr)
    res = pl.pallas_call(
        body, name=name, grid=(s // tr,),
        in_specs=[rows(proj.shape[1])] + [rows(g.shape[1]) for g in grads]
        + [vec(HEAD), vec(HEAD), vec(C_RANK), vec(C_RANK)] + [rows(HEAD)] * 6,
        out_specs=[rows(proj.shape[1]), vec(HEAD), vec(HEAD), vec(C_RANK), vec(C_RANK)],
        out_shape=[jax.ShapeDtypeStruct(proj.shape, BF16)] + [jax.ShapeDtypeStruct((1, w), F32) for w in (HEAD, HEAD, C_RANK, C_RANK)],
        compiler_params=_params("arbitrary"))(
            proj, *grads, p["a_q_norm"].reshape(1, HEAD), p["a_k_norm"].reshape(1, HEAD), p["c_q_norm"].reshape(1, C_RANK),
            p["c_kv_norm"].reshape(1, C_RANK), ca, s1a, s2a, cb, s1b, s2b)
    return res[0], res[1].reshape(HEAD), res[2].reshape(HEAD), res[3].reshape(C_RANK), res[4].reshape(C_RANK)


OUT_GROUPS = ((0, A_W), (A_W, A_W + B_W), (A_W + B_W, A_W + B_W + C_W))


def out_norms(name, ya, yb, yc, gain):
    s = ya.shape[0]
    tr = PREP_ROWS
    width = OUT_GROUPS[-1][1]

    def body(ya_ref, yb_ref, yc_ref, g_ref, y_ref):
        for (lo, hi), src in zip(OUT_GROUPS, (ya_ref, yb_ref, yc_ref)):
            x = src[...]
            rs = lax.rsqrt(jnp.mean(x * x, axis=-1, keepdims=True) + EPS)
            y_ref[:, lo:hi] = (x * rs * g_ref[:, lo:hi]).astype(BF16)

    rows = lambda w: pl.BlockSpec((tr, w), lambda i: (i, 0))
    return pl.pallas_call(
        body, name=name, grid=(s // tr,),
        in_specs=[rows(A_W), rows(B_W), rows(C_W), pl.BlockSpec((1, width), lambda i: (0, 0))], out_specs=rows(width),
        out_shape=jax.ShapeDtypeStruct((s, width), BF16), compiler_params=_params("parallel"))(
            ya, yb, yc, gain.reshape(1, width))


def out_norms_bwd(name, dy, ya, yb, yc, gain):
    s = ya.shape[0]
    tr = PREP_ROWS
    width = OUT_GROUPS[-1][1]

    def body(dy_ref, ya_ref, yb_ref, yc_ref, g_ref, dya_ref, dyb_ref, dyc_ref, dg_ref):
        @pl.when(pl.program_id(0) == 0)
        def _():
            dg_ref[...] = jnp.zeros_like(dg_ref)

        for (lo, hi), src, dst in zip(OUT_GROUPS, (ya_ref, yb_ref, yc_ref), (dya_ref, dyb_ref, dyc_ref)):
            x = src[...]
            d = dy_ref[:, lo:hi]
            rs = lax.rsqrt(jnp.mean(x * x, axis=-1, keepdims=True) + EPS)
            xh = x * rs
            dg = d * g_ref[:, lo:hi]
            dst[...] = (rs * (dg - xh * jnp.mean(dg * xh, axis=-1, keepdims=True))).astype(dst.dtype)
            dg_ref[:, lo:hi] += jnp.sum(d * xh, axis=0, keepdims=True)

    rows = lambda w: pl.BlockSpec((tr, w), lambda i: (i, 0))
    vec = pl.BlockSpec((1, width), lambda i: (0, 0))
    dya, dyb, dyc, dg = pl.pallas_call(
        body, name=name, grid=(s // tr,),
        in_specs=[rows(width), rows(A_W), rows(B_W), rows(C_W), vec], out_specs=[rows(A_W), rows(B_W), rows(C_W), vec],
        out_shape=[jax.ShapeDtypeStruct((s, A_W), BF16), jax.ShapeDtypeStruct((s, B_W), F32),
                   jax.ShapeDtypeStruct((s, C_W), BF16), jax.ShapeDtypeStruct((1, width), F32)],
        compiler_params=_params("arbitrary"))(dy, ya, yb, yc, gain.reshape(1, width))
    return dya, dyb, dyc, dg.reshape(width)


ATTN_TQ = 256


def attn_fwd(name, parts, v, v_group, nheads, scale):
    s = v.shape[0]
    tq = ATTN_TQ
    npart = len(parts)

    def body(*refs):
        v_ref, o_ref, lse_ref = refs[2 * npart:]
        sc = None
        for p in range(npart):
            t = _dot(refs[2 * p][...], refs[2 * p + 1][...], NT)
            sc = t if sc is None else sc + t
        sc = sc * scale
        m = jnp.max(sc, axis=-1, keepdims=True)
        e = jnp.exp(sc - m)
        l = jnp.sum(e, axis=-1, keepdims=True)
        o_ref[...] = _dot(e.astype(BF16), v_ref[...]) / l
        lse_ref[...] = jnp.broadcast_to(m + jnp.log(l), (tq, HEAD))

    in_specs, ins = [], []
    for q, qoff, k, kg in parts:
        in_specs.append(pl.BlockSpec((tq, HEAD), lambda h, i, qoff=qoff: (i, qoff + h)))
        in_specs.append(pl.BlockSpec((s, HEAD), lambda h, i, kg=kg: (0, h // kg)))
        ins += [q, k]
    in_specs.append(pl.BlockSpec((s, HEAD), lambda h, i: (0, h // v_group)))
    out_blk = pl.BlockSpec((tq, HEAD), lambda h, i: (i, h))
    return pl.pallas_call(
        body, name=name, grid=(nheads, s // tq), in_specs=in_specs, out_specs=[out_blk, out_blk],
        out_shape=[jax.ShapeDtypeStruct((s, nheads * HEAD), F32)] * 2,
        compiler_params=_params("parallel", "parallel"))(*ins, v)


def attn_bwd(name, parts, v, v_group, lse, o, do, nheads, scale):
    s = v.shape[0]
    tq = ATTN_TQ
    npart = len(parts)

    def body(*refs):
        v_ref, lse_ref, o_ref, do_ref = refs[2 * npart:2 * npart + 4]
        outs = refs[2 * npart + 4:]
        dq_refs, dk_refs, dv_ref = outs[:npart], outs[npart:2 * npart], outs[2 * npart]
        h, i = pl.program_id(0), pl.program_id(1)
        sc = None
        for p in range(npart):
            t = _dot(refs[2 * p][...], refs[2 * p + 1][...], NT)
            sc = t if sc is None else sc + t
        pr = jnp.exp(sc * scale - lse_ref[:, 0:1])
        dov = do_ref[...]
        delta = jnp.sum(dov.astype(F32) * o_ref[...], axis=-1, keepdims=True)
        dp = _dot(dov, v_ref[...], NT)
        ds = (pr * (dp - delta) * scale).astype(BF16)

        @pl.when((i == 0) & (h % v_group == 0))
        def _():
            dv_ref[...] = jnp.zeros_like(dv_ref)

        dv_ref[...] += _dot(pr.astype(BF16), dov, TN)
        for p in range(npart):
            kg = parts[p][3]
            dq_refs[p][...] = _dot(ds, refs[2 * p + 1][...])

            @pl.when((i == 0) & (h % kg == 0))
            def _(p=p):
                dk_refs[p][...] = jnp.zeros_like(dk_refs[p])

            dk_refs[p][...] += _dot(ds, refs[2 * p][...], TN)

    in_specs, ins = [], []
    for q, qoff, k, kg in parts:
        in_specs.append(pl.BlockSpec((tq, HEAD), lambda h, i, qoff=qoff: (i, qoff + h)))
        in_specs.append(pl.BlockSpec((s, HEAD), lambda h, i, kg=kg: (0, h // kg)))
        ins += [q, k]
    hq_blk = pl.BlockSpec((tq, HEAD), lambda h, i: (i, h))
    in_specs += [pl.BlockSpec((s, HEAD), lambda h, i: (0, h // v_group)), hq_blk, hq_blk, hq_blk]
    out_specs = [hq_blk] * npart
    out_shape = [jax.ShapeDtypeStruct((s, nheads * HEAD), F32)] * npart
    for q, qoff, k, kg in parts:
        out_specs.append(pl.BlockSpec((s, HEAD), lambda h, i, kg=kg: (0, h // kg)))
        out_shape.append(jax.ShapeDtypeStruct((s, nheads // kg * HEAD), F32))
    out_specs.append(pl.BlockSpec((s, HEAD), lambda h, i: (0, h // v_group)))
    out_shape.append(jax.ShapeDtypeStruct((s, nheads // v_group * HEAD), F32))
    res = pl.pallas_call(
        body, name=name, grid=(nheads, s // tq), in_specs=in_specs, out_specs=out_specs, out_shape=out_shape,
        compiler_params=_params("arbitrary", "arbitrary"))(*ins, v, lse, o, do)
    return list(res[:npart]), list(res[npart:2 * npart]), res[2 * npart]


def _band_windows(lf):
    for ib in range(lf // HEAD):
        q0 = ib * HEAD
        yield q0, max(0, q0 - B_HALF), min(lf, q0 + HEAD + B_HALF)


def _band_mask(q0, lo, hi):
    qpos = q0 + lax.broadcasted_iota(jnp.int32, (HEAD, hi - lo), 0)
    kpos = lo + lax.broadcasted_iota(jnp.int32, (HEAD, hi - lo), 1)
    return jnp.abs(qpos - kpos) <= B_HALF


def _class_rows(r, d, start, size):
    return pl.ds(r + d * start, size, stride=d) if d > 1 else pl.ds(start, size)


def dilated_fwd(name, q, k, v_src, v_block):
    s = k.shape[0]
    scale = HEAD ** -0.5

    def body(q_ref, k_ref, v_ref, y_ref, lse_ref, o_scr, l_scr):
        g_now = pl.program_id(1)
        for g, d in enumerate(B_DILATIONS):
            @pl.when(g_now == g)
            def _(g=g, d=d):
                for r in range(d):
                    for q0, lo, hi in _band_windows(s // d):
                        mine = _class_rows(r, d, q0, HEAD)
                        keys = _class_rows(r, d, lo, hi - lo)
                        sc = _dot(q_ref[mine, :].astype(BF16), k_ref[keys, :].astype(BF16), NT) * scale
                        sc = jnp.where(_band_mask(q0, lo, hi), sc, -1e30)
                        m = jnp.max(sc, axis=-1, keepdims=True)
                        e = jnp.exp(sc - m)
                        l = jnp.sum(e, axis=-1, keepdims=True)
                        o_scr.at[g][mine, :] = _dot((e / l).astype(BF16), v_ref[keys, :].astype(BF16))
                        l_scr.at[g][mine, :] = jnp.broadcast_to(m + jnp.log(l), (HEAD, HEAD))

        @pl.when(g_now == B_GROUPS - 1)
        def _():
            a, b, c = l_scr[0], l_scr[1], l_scr[2]
            m = jnp.maximum(jnp.maximum(a, b), c)
            ea, eb, ec = jnp.exp(a - m), jnp.exp(b - m), jnp.exp(c - m)
            den = ea + eb + ec
            y_ref[...] = (ea / den) * o_scr[0] + (eb / den) * o_scr[1] + (ec / den) * o_scr[2]
            lse_ref[...] = m + jnp.log(den)

    blk = lambda f: pl.BlockSpec((s, HEAD), f)
    per_head = blk(lambda h, g: (0, h))
    return pl.pallas_call(
        body, name=name, grid=(B_HEADS, B_GROUPS),
        in_specs=[blk(lambda h, g: (0, B_HEADS * g + h)), per_head, blk(lambda h, g: (0, v_block + h))],
        out_specs=[per_head, per_head], out_shape=[jax.ShapeDtypeStruct((s, B_W), F32)] * 2,
        scratch_shapes=[pltpu.VMEM((B_GROUPS, s, HEAD), F32)] * 2,
        compiler_params=_params("parallel", "arbitrary"))(q, k, v_src)


def dilated_bwd(name, q, k, v_src, v_block, dy, y, lse):
    s = k.shape[0]
    scale = HEAD ** -0.5

    def body(q_ref, k_ref, v_ref, dy_ref, y_ref, lse_ref, dq_ref, dk_ref, dv_ref, delta):
        g_now = pl.program_id(1)

        @pl.when(g_now == 0)
        def _():
            dk_ref[...] = jnp.zeros_like(dk_ref)
            dv_ref[...] = jnp.zeros_like(dv_ref)
            delta[...] = jnp.broadcast_to(jnp.sum(dy_ref[...] * y_ref[...], axis=-1, keepdims=True), (s, HEAD))

        for g, d in enumerate(B_DILATIONS):
            @pl.when(g_now == g)
            def _(d=d):
                for r in range(d):
                    for q0, lo, hi in _band_windows(s // d):
                        mine = _class_rows(r, d, q0, HEAD)
                        keys = _class_rows(r, d, lo, hi - lo)
                        qv, kv = q_ref[mine, :].astype(BF16), k_ref[keys, :].astype(BF16)
                        dyv = dy_ref[mine, :].astype(BF16)
                        sc = _dot(qv, kv, NT) * scale
                        pr = jnp.where(_band_mask(q0, lo, hi), jnp.exp(sc - lse_ref[mine, :][:, 0:1]), 0.0)
                        dp = _dot(dyv, v_ref[keys, :].astype(BF16), NT)
                        ds = (pr * (dp - delta[mine, :][:, 0:1]) * scale).astype(BF16)
                        dv_ref[keys, :] = dv_ref[keys, :] + _dot(pr.astype(BF16), dyv, TN)
                        dq_ref[mine, :] = _dot(ds, kv)
                        dk_ref[keys, :] = dk_ref[keys, :] + _dot(ds, qv, TN)

    blk = lambda f: pl.BlockSpec((s, HEAD), f)
    per_head = blk(lambda h, g: (0, h))
    by_group = blk(lambda h, g: (0, B_HEADS * g + h))
    return pl.pallas_call(
        body, name=name, grid=(B_HEADS, B_GROUPS),
        in_specs=[by_group, per_head, blk(lambda h, g: (0, v_block + h)), per_head, per_head, per_head],
        out_specs=[by_group, per_head, per_head],
        out_shape=[jax.ShapeDtypeStruct((s, B_GROUPS * B_W), F32)] + [jax.ShapeDtypeStruct((s, B_W), F32)] * 2,
        scratch_shapes=[pltpu.VMEM((s, HEAD), F32)],
        compiler_params=_params("parallel", "arbitrary"))(q, k, v_src, dy, y, lse)


FFN_TN = 256


def _edges(shape):
    row = lax.broadcasted_iota(jnp.int32, shape, 0)
    return row == 0, row == shape[0] - 1


def _shifted(h, edges):
    first, last = edges
    s = h.shape[0]
    return jnp.where(first, 0.0, pltpu.roll(h, 1, 0)), jnp.where(last, 0.0, pltpu.roll(h, s - 1, 0))


def _conv(h, cw, edges):
    prev, nxt = _shifted(h, edges)
    return prev * cw[0:1, :] + h * cw[1:2, :] + nxt * cw[2:3, :] + cw[3:4, :], prev, nxt


def _sigmoid(x):
    return 1.0 / (1.0 + jnp.exp(-x))


def ffn_up(name, xn, w_up, cwb):
    s, dm = xn.shape
    quarter = w_up.shape[3]
    fh = 2 * quarter
    tn = FFN_TN
    per = quarter // tn

    def body(x_ref, w_ref, cw_ref, h_ref, act_ref):
        xv = x_ref[...]
        hg = _dot(xv, w_ref[0])
        hu = _dot(xv, w_ref[1])
        h_ref[0] = hg
        h_ref[1] = hu
        edges = _edges(hg.shape)
        gc, _, _ = _conv(hg, cw_ref[0], edges)
        uc, _, _ = _conv(hu, cw_ref[1], edges)
        act_ref[...] = (gc * _sigmoid(gc) * uc).astype(BF16)

    return pl.pallas_call(
        body, name=name, grid=(fh // tn,),
        in_specs=[pl.BlockSpec((s, dm), lambda t: (0, 0), pipeline_mode=pl.Buffered(1)),
                  pl.BlockSpec((2, None, dm, tn), lambda t: (0, t // per, 0, t % per)),
                  pl.BlockSpec((2, 8, tn), lambda t: (0, 0, t))],
        out_specs=[pl.BlockSpec((2, s, tn), lambda t: (0, 0, t)), pl.BlockSpec((s, tn), lambda t: (0, t))],
        out_shape=[jax.ShapeDtypeStruct((2, s, fh), F32), jax.ShapeDtypeStruct((s, fh), BF16)],
        compiler_params=_params("parallel"))(xn, w_up, cwb)


def ffn_gate_bwd(name, h, dact, cwb):
    _, s, fh = h.shape
    tn = FFN_TN

    def body(h_ref, da_ref, cw_ref, dh_ref, dcw_ref):
        edges = _edges((s, tn))
        gc, gp, gn = _conv(h_ref[0], cw_ref[0], edges)
        uc, up, un = _conv(h_ref[1], cw_ref[1], edges)
        sg = _sigmoid(gc)
        da = da_ref[...]
        dgc = da * uc * (sg * (1.0 + gc * (1.0 - sg)))
        duc = da * (gc * sg)
        for idx, (hv, prev, nxt, dc) in enumerate(((h_ref[0], gp, gn, dgc), (h_ref[1], up, un, duc))):
            cw = cw_ref[idx]
            from_prev, from_next = _shifted(dc, edges)
            dh_ref[idx] = (from_next * cw[0:1, :] + dc * cw[1:2, :] + from_prev * cw[2:3, :]).astype(BF16)
            dcw_ref[idx, 0:1, :] = jnp.sum(prev * dc, axis=0, keepdims=True)
            dcw_ref[idx, 1:2, :] = jnp.sum(hv * dc, axis=0, keepdims=True)
            dcw_ref[idx, 2:3, :] = jnp.sum(nxt * dc, axis=0, keepdims=True)
            dcw_ref[idx, 3:4, :] = jnp.sum(dc, axis=0, keepdims=True)
            dcw_ref[idx, 4:8, :] = jnp.zeros((4, tn), F32)

    return pl.pallas_call(
        body, name=name, grid=(fh // tn,),
        in_specs=[pl.BlockSpec((2, s, tn), lambda t: (0, 0, t)), pl.BlockSpec((s, tn), lambda t: (0, t)),
                  pl.BlockSpec((2, 8, tn), lambda t: (0, 0, t))],
        out_specs=[pl.BlockSpec((2, s, tn), lambda t: (0, 0, t)), pl.BlockSpec((2, 8, tn), lambda t: (0, 0, t))],
        out_shape=[jax.ShapeDtypeStruct((2, s, fh), BF16), jax.ShapeDtypeStruct((2, 8, fh), F32)],
        compiler_params=_params("parallel"))(h, dact, cwb)


def ffn_dx(name, dh, w_up):
    _, s, fh = dh.shape
    dm, quarter = w_up.shape[2], w_up.shape[3]
    tk = _tile(quarter, 2816)
    per = quarter // tk
    tm, tn = _tile(s, 1024), _tile(dm, 1024)
    grid = (s // tm, dm // tn, 4 * per)
    a_spec = pl.BlockSpec((None, tm, tk), lambda i, j, k: (k // (2 * per), i, k % (2 * per)))
    b_spec = pl.BlockSpec((None, None, tn, tk), lambda i, j, k: (k // (2 * per), (k // per) % 2, j, k % per))
    o_spec = pl.BlockSpec((tm, tn), lambda i, j, k: (i, j))
    return _matmul_call(name, NT, grid, a_spec, b_spec, o_spec, jax.ShapeDtypeStruct((s, dm), F32), (tm, tn))(dh, w_up)


def ffn_dw_up(name, xn, dh):
    _, s, fh = dh.shape
    dm = xn.shape[1]
    quarter = fh // 2
    tn = _tile(quarter, 1536)
    per = quarter // tn
    tm, tk = _tile(dm, 1024), _tile(s, 2048)
    grid = (dm // tm, 4 * per, s // tk)
    a_spec = pl.BlockSpec((tk, tm), lambda i, j, k: (k, i))
    b_spec = pl.BlockSpec((None, tk, tn), lambda i, j, k: (j // (2 * per), k, j % (2 * per)))
    o_spec = pl.BlockSpec((None, None, tm, tn), lambda i, j, k: (j // (2 * per), (j // per) % 2, i, j % per))
    return _matmul_call(name, TN, grid, a_spec, b_spec, o_spec,
                        jax.ShapeDtypeStruct((2, 2, dm, quarter), F32), (tm, tn))(xn, dh)


def loss_head(name, y, target):
    s, dm = y.shape
    ts = _rows(s, dm)

    def body(y_ref, t_ref, loss_ref, dy_ref, acc):
        i = pl.program_id(0)

        @pl.when(i == 0)
        def _():
            acc[...] = jnp.zeros_like(acc)

        err = y_ref[...] - t_ref[...]
        dy_ref[...] = err * (1.0 / dm)
        acc[...] += jnp.sum(err * err, axis=0, keepdims=True)

        @pl.when(i == s // ts - 1)
        def _():
            loss_ref[...] = jnp.broadcast_to(jnp.sum(acc[...], axis=-1, keepdims=True) * (0.5 / dm), (1, LANES))

    blk = pl.BlockSpec((ts, dm), lambda i: (i, 0))
    loss, dy = pl.pallas_call(
        body, name=name, grid=(s // ts,), in_specs=[blk, blk],
        out_specs=[pl.BlockSpec((1, LANES), lambda i: (0, 0)), blk],
        out_shape=[jax.ShapeDtypeStruct((1, LANES), F32), jax.ShapeDtypeStruct((s, dm), F32)],
        scratch_shapes=[pltpu.VMEM((1, dm), F32)], compiler_params=_params("arbitrary"))(y, target)
    return loss[0, 0], dy


def attn_fwd_c(name, qc, kvc, krc, scale):
    s = qc.shape[0]
    tq = ATTN_TQ

    def body(qn_ref, qr_ref, kn_ref, kr_ref, v_ref, o_ref, lse_ref):
        sc = (_dot(qn_ref[...], kn_ref[...], NT) + _dot(qr_ref[...], kr_ref[...], NT)) * scale
        m = jnp.max(sc, axis=-1, keepdims=True)
        e = jnp.exp(sc - m)
        l = jnp.sum(e, axis=-1, keepdims=True)
        o_ref[...] = _dot(e.astype(BF16), v_ref[...]) / l
        lse_ref[...] = jnp.broadcast_to(m + jnp.log(l), (tq, HEAD))

    qb = lambda off: pl.BlockSpec((tq, HEAD), lambda h, i: (i, off + h))
    kb = lambda f: pl.BlockSpec((s, HEAD), f)
    out_blk = pl.BlockSpec((tq, HEAD), lambda h, i: (i, h))
    return pl.pallas_call(
        body, name=name, grid=(C_HEADS, s // tq),
        in_specs=[qb(0), qb(C_HEADS), kb(lambda h, i: (0, 2 * h)), kb(lambda h, i: (0, 0)), kb(lambda h, i: (0, 2 * h + 1))],
        out_specs=[out_blk, out_blk], out_shape=[jax.ShapeDtypeStruct((s, C_W), F32)] * 2,
        compiler_params=_params("parallel", "parallel"))(qc, qc, kvc, krc, kvc)


def attn_bwd_c(name, qc, kvc, krc, lse, o, do, scale):
    s = qc.shape[0]
    tq = ATTN_TQ

    def body(qn_ref, qr_ref, kn_ref, kr_ref, v_ref, lse_ref, o_ref, do_ref, dqn_ref, dqr_ref, dkv_ref, dkr_ref):
        h, i = pl.program_id(0), pl.program_id(1)
        sc = (_dot(qn_ref[...], kn_ref[...], NT) + _dot(qr_ref[...], kr_ref[...], NT)) * scale
        pr = jnp.exp(sc - lse_ref[:, 0:1])
        dov = do_ref[...]
        delta = jnp.sum(dov.astype(F32) * o_ref[...], axis=-1, keepdims=True)
        dp = _dot(dov, v_ref[...], NT)
        ds = (pr * (dp - delta) * scale).astype(BF16)

        @pl.when(i == 0)
        def _():
            dkv_ref[...] = jnp.zeros_like(dkv_ref)

        @pl.when((i == 0) & (h == 0))
        def _():
            dkr_ref[...] = jnp.zeros_like(dkr_ref)

        dkv_ref[:, HEAD:] += _dot(pr.astype(BF16), dov, TN)
        dqn_ref[...] = _dot(ds, kn_ref[...])
        dqr_ref[...] = _dot(ds, kr_ref[...])
        dkv_ref[:, :HEAD] += _dot(ds, qn_ref[...], TN)
        dkr_ref[...] += _dot(ds, qr_ref[...], TN)

    qb = lambda off: pl.BlockSpec((tq, HEAD), lambda h, i: (i, off + h))
    kb = lambda f: pl.BlockSpec((s, HEAD), f)
    hq = pl.BlockSpec((tq, HEAD), lambda h, i: (i, h))
    kn_map, v_map, kr_map = (lambda h, i: (0, 2 * h)), (lambda h, i: (0, 2 * h + 1)), (lambda h, i: (0, 0))
    return pl.pallas_call(
        body, name=name, grid=(C_HEADS, s // tq),
        in_specs=[qb(0), qb(C_HEADS), kb(kn_map), kb(kr_map), kb(v_map), hq, hq, hq],
        out_specs=[hq, hq, pl.BlockSpec((s, 2 * HEAD), lambda h, i: (0, h)), kb(kr_map)],
        out_shape=[jax.ShapeDtypeStruct((s, C_W), F32)] * 2
        + [jax.ShapeDtypeStruct((s, 2 * C_W), F32), jax.ShapeDtypeStruct((s, HEAD), F32)],
        compiler_params=_params("arbitrary", "arbitrary"))(qc, qc, kvc, krc, kvc, lse, o, do)


def _layer_fwd(x, p, tabs):
    sv = _mixers_ab_fwd(x, p, tabs)
    x1 = _mixer_c_out_fwd(x, sv, p, tabs)
    return _ffn_fwd(x1, p, sv), sv


def _mixers_ab_fwd(x, p, tabs):
    sv = {"x": x}
    hn = rmsnorm_fwd("attn_norm", x, p["attn_norm"], x.shape[1], BF16)
    proj = matmul("in_proj", hn, p["w_in_t"], "nt", F32, tm=256)
    qa, ka, av, qb, kb, cqn, ckvn = mixer_prep("mixer_prep", proj, p, tabs)
    sv.update(hn=hn, proj=proj, qa=qa, ka=ka, av=av, qb=qb, kb=kb, cqn=cqn, ckvn=ckvn)

    ya, lse_a = attn_fwd("a_attn", [(qa, 0, ka, A_HEADS // A_KV)], av, A_HEADS // A_KV, A_HEADS, HEAD ** -0.5)
    yb, lse_b = dilated_fwd("b_dilated", qb, kb, proj, O_BV // HEAD)
    sv.update(ya=ya, lse_a=lse_a, yb=yb, lse_b=lse_b)
    return sv


def _mixer_c_out_fwd(x, sv, p, tabs):
    ya, yb = sv["ya"], sv["yb"]
    qc_raw = matmul("c_uq", sv["cqn"], p["w_uq_t"], "nt", F32)
    qc = jnp.concatenate([qc_raw[:, :C_W].astype(BF16), rope("c_q_rope", qc_raw[:, C_W:], tabs["c"][0], BF16)], axis=1)
    kvc = matmul("c_ukv", sv["ckvn"], p["w_ukv"], "nn", BF16)
    ckr = jnp.pad(sv["proj"][:, O_CKR:], ((0, 0), (0, HEAD - C_ROPE)))
    krc = rope("c_k_rope", ckr, tabs["c"][0], BF16)
    c_scale = (HEAD + C_ROPE) ** -0.5
    yc, lse_c = attn_fwd_c("c_attn", qc, kvc, krc, c_scale)
    sv.update(qc=qc, kvc=kvc, krc=krc, yc=yc, lse_c=lse_c)

    y = out_norms("out_norms", ya, yb, yc, p["out_norm"])
    x1 = matmul("out_proj", y, p["w_out"], "nn", F32, add=x)
    sv.update(y=y, x1=x1)
    return x1


def _ffn_fwd(x1, p, sv):
    xn = rmsnorm_fwd("ffn_norm", x1, p["ffn_norm"], x1.shape[1], BF16)
    h, act = ffn_up("ffn_up", xn, p["w_up"], p["cwb"])
    x2 = matmul("ffn_down", act, p["w_down"], "nn", F32, add=x1)
    sv.update(xn=xn, h=h, act=act)
    return x2


def _layer_bwd(dx2, sv, p, tabs):
    dxn, g = _ffn_bwd(dx2, sv, p)
    dx1, dy, g_out = _out_proj_bwd(dxn, dx2, sv, p)
    dhn, g_mix = _mixers_bwd(dy, sv, p, tabs)
    dx, g_norm = _attn_norm_bwd(dhn, dx1, sv, p)
    return dx, {**g, **g_out, **g_mix, **g_norm}


def _ffn_bwd(dx2, sv, p, after=0.0):
    g = {}
    dx2b = (dx2 + after).astype(BF16)
    dact = matmul("ffn_down_dx", dx2b, p["w_down"], "nt", F32)
    g["w_down"] = matmul("ffn_down_dw", sv["act"], dx2b, "tn", F32)
    dh, g["cwb"] = ffn_gate_bwd("ffn_gate_bwd", sv["h"], dact, p["cwb"])
    dxn = ffn_dx("ffn_up_dx", dh, p["w_up"])
    g["w_up"] = ffn_dw_up("ffn_up_dw", sv["xn"], dh)
    return dxn, g


def _out_proj_bwd(dxn, dx2, sv, p, after=0.0):
    g = {}
    dm = dx2.shape[1]
    dx1, dx1b, g["ffn_norm"] = rmsnorm_bwd("ffn_norm_bwd", dxn, sv["x1"], p["ffn_norm"] + after, dm, (F32, BF16), add=dx2)
    dy = matmul("out_proj_dx", dx1b, p["w_out"], "nt", F32)
    g["w_out"] = matmul("out_proj_dw", sv["y"], dx1b, "tn", F32)
    return dx1, dy, g


def _attn_norm_bwd(dhn, dx1, sv, p, after=0.0):
    dx, d_gain = rmsnorm_bwd("attn_norm_bwd", dhn, sv["x"], p["attn_norm"] + after, dhn.shape[1], (F32,), add=dx1)
    return dx, {"attn_norm": d_gain}


def _mixers_bwd(dy, sv, p, tabs):
    part, g = _mixer_c_bwd(dy, sv, p, tabs)
    dhn, g_ab = _mixers_ab_bwd(part, sv, p, tabs)
    return dhn, {**g, **g_ab}


def _mixer_c_bwd(dy, sv, p, tabs, after=0.0):
    g = {}
    dya, dyb, dyc, g["out_norm"] = out_norms_bwd("out_norms_bwd", dy, sv["ya"], sv["yb"], sv["yc"], p["out_norm"] + after)

    c_scale = (HEAD + C_ROPE) ** -0.5
    dqn, dqr, dkv, dkr = attn_bwd_c("c_attn_bwd", sv["qc"], sv["kvc"], sv["krc"], sv["lse_c"], sv["yc"], dyc, c_scale)
    dqc = jnp.concatenate([dqn.astype(BF16), rope("c_q_rope_bwd", dqr, tabs["c"][1], BF16)], axis=1)
    dcqn = matmul("c_uq_dx", dqc, p["w_uq_t"], "nn", F32)
    g["w_uq"] = matmul("c_uq_dw", dqc, sv["cqn"], "tn", F32)
    dkvc = dkv.astype(BF16)
    dckvn = matmul("c_ukv_dx", dkvc, p["w_ukv"], "nt", F32)
    g["w_ukv"] = matmul("c_ukv_dw", sv["ckvn"], dkvc, "tn", F32)
    dckr = rope("c_k_rope_bwd", dkr, tabs["c"][1], BF16)
    return (dya, dyb, dcqn, dckvn, dckr), g


def _mixers_ab_bwd(part, sv, p, tabs, after=0.0):
    g = {}
    dya, dyb, dcqn, dckvn, dckr = part

    dqb, dkb, dvb = dilated_bwd("b_dilated_bwd", sv["qb"], sv["kb"], sv["proj"], O_BV // HEAD, dyb + after, sv["yb"],
                                sv["lse_b"])

    kg = A_HEADS // A_KV
    (dqa,), (dka,), dva = attn_bwd("a_attn_bwd", [(sv["qa"], 0, sv["ka"], kg)], sv["av"], kg, sv["lse_a"], sv["ya"],
                                   dya, A_HEADS, HEAD ** -0.5)

    dproj, g["a_q_norm"], g["a_k_norm"], g["c_q_norm"], g["c_kv_norm"] = mixer_prep_bwd(
        "mixer_prep_bwd", sv["proj"], p, tabs, dqa, dka, dva, dqb, dkb, dvb, dcqn, dckvn, dckr)
    dhn = matmul("in_proj_dx", dproj, p["w_in_t"], "nn", F32, tm=512)
    g["w_in"] = matmul("in_proj_dw", dproj, sv["hn"], "tn", F32, tn=512, tk=512)
    return dhn, g


def _local_step(x, target, layers, final_norm, tabs):
    saved = []
    for p in layers:
        x, sv = _layer_fwd(x, p, tabs)
        saved.append(sv)
    dm = x.shape[1]
    yf = rmsnorm_fwd("final_norm", x, final_norm, dm, F32)
    loss, dyf = loss_head("loss_head", yf, target)
    dx, d_final = rmsnorm_bwd("final_norm_bwd", dyf, x, final_norm, dm, (F32,))
    grads = [None] * len(layers)
    for l in reversed(range(len(layers))):
        dx, grads[l] = _layer_bwd(dx, saved[l], layers[l], tabs)
    return loss, dx, grads, d_final


HBM = pl.BlockSpec(memory_space=pl.ANY)


def _place():
    x, y, c = lax.axis_index("x"), lax.axis_index("y"), lax.axis_index("c")
    others = [(1 - x, y), (x, 1 - y), (1 - x, 1 - y)]
    return x, y, c, 2 * x + y, others


def _remote(src, dst, send_sem, recv_sem, device):
    return pltpu.make_async_remote_copy(src_ref=src, dst_ref=dst, send_sem=send_sem, recv_sem=recv_sem,
                                        device_id=device, device_id_type=MESH)


def gather_small(name, small):
    def body(in_ref, out_ref, send_sems, recv_sems, local_sem):
        x, y, c, chip, others = _place()
        mine = pltpu.make_async_copy(in_ref, out_ref.at[chip], local_sem)
        mine.start()
        copies = []
        for k, (ox, oy) in enumerate(others):
            cp = _remote(in_ref, out_ref.at[chip], send_sems.at[k], recv_sems.at[k], (ox, oy, c))
            cp.start()
            copies.append(cp)
        for k, (ox, oy) in enumerate(others):
            landed = out_ref.at[2 * ox + oy]
            _remote(landed, landed, send_sems.at[k], recv_sems.at[k], (ox, oy, c)).wait_recv()
        for cp in copies:
            cp.wait_send()
        mine.wait()

    return pl.pallas_call(
        body, name=name, in_specs=[HBM], out_specs=HBM, out_shape=jax.ShapeDtypeStruct((4,) + small.shape, small.dtype),
        scratch_shapes=[pltpu.SemaphoreType.DMA((3,)), pltpu.SemaphoreType.DMA((3,)), pltpu.SemaphoreType.DMA],
        compiler_params=pltpu.CompilerParams(has_side_effects=True))(small)


IN_HBM = pl.BlockSpec(memory_space=pltpu.HBM)
SEMS = pl.BlockSpec(memory_space=pltpu.SEMAPHORE)
DATAFLOW = pltpu.SideEffectType.DATAFLOW_SIDE_EFFECTING


BF16_ROWS_PER_TILE = 16


def _halves_by_rows(shape):
    return (shape[-2] // 2) % BF16_ROWS_PER_TILE == 0


def _half(shape, core):
    if _halves_by_rows(shape):
        size = shape[-2] // 2
        return (pl.ds(core * size, size), slice(None))
    size = shape[-1] // 2
    return (slice(None), pl.ds(core * size, size))


def _half_shape(shape):
    r, c = shape[-2:]
    return (r // 2, c) if _halves_by_rows(shape) else (r, c // 2)


def _gather_plan(bufs):
    x, y, c, chip, others = _place()
    plan = []
    for t, ref in enumerate(bufs):
        mine = _half(ref.shape, c)
        for k, (ox, oy) in enumerate(others):
            plan.append((3 * t + k, (ox, oy, c), ref.at[(chip,) + mine], ref.at[(chip,) + mine],
                         ref.at[(2 * ox + oy,) + mine]))
    return plan


def _exchange_plan(bufs):
    x, y, c, chip, others = _place()
    n = len(bufs) // 2
    plan = []
    for t in range(n):
        for k, (ox, oy) in enumerate(others):
            plan.append((3 * t + k, (ox, oy, c), bufs[t].at[2 * ox + oy], bufs[n + t].at[chip], bufs[n + t].at[2 * ox + oy]))
    return plan


def _sibling_plan(bufs):
    x, y, c, chip, others = _place()
    n = len(bufs) // 2
    plan = []
    for t in range(n):
        plan.append((t, (x, y, 1 - c), bufs[t].at[(slice(None),) + _half(bufs[t].shape, 1 - c)], bufs[n + t], bufs[n + t]))
    return plan


PLAN_COPIES = {_gather_plan: lambda n: 3 * n, _exchange_plan: lambda n: 3 * (n // 2), _sibling_plan: lambda n: n // 2}


def split_start(name, bufs, plan_of, after):
    n = len(bufs)

    def body(*refs):
        ins = refs[:n]
        send_sems, recv_sems = refs[n + 1], refs[n + 2]
        token = refs[-1]
        for idx, peer, src, dst, _ in plan_of(ins):
            _remote(src, dst, send_sems.at[idx], recv_sems.at[idx], peer).start()
        token[...] = jnp.zeros_like(token)

    copies = PLAN_COPIES[plan_of](n)
    res = pl.pallas_call(
        body, name=name, in_specs=[IN_HBM] * n + [HBM],
        out_specs=(SEMS, SEMS) + (IN_HBM,) * n + (pl.BlockSpec(memory_space=pltpu.VMEM),),
        out_shape=(pltpu.SemaphoreType.DMA((copies,)), pltpu.SemaphoreType.DMA((copies,)))
        + tuple(pltpu.HBM(b.shape, b.dtype) for b in bufs) + (jax.ShapeDtypeStruct((8, LANES), F32),),
        input_output_aliases={t: 2 + t for t in range(n)},
        compiler_params=pltpu.CompilerParams(has_side_effects=DATAFLOW))(
            *[pltpu.with_memory_space_constraint(b, pltpu.HBM) for b in bufs], after)
    return res[0], res[1], list(res[2:2 + n]), res[-1]


def split_wait(name, send_sems, recv_sems, flying, plan_of, after):
    n = len(flying)

    def body(*refs):
        ins = refs[:n]
        send_ref, recv_ref = refs[n], refs[n + 1]
        for idx, peer, src, dst, landing in plan_of(ins):
            _remote(src, dst, send_ref.at[idx], recv_ref.at[idx], peer).wait_send()
            _remote(landing, landing, send_ref.at[idx], recv_ref.at[idx], peer).wait_recv()

    return list(pl.pallas_call(
        body, name=name, in_specs=[IN_HBM] * n + [SEMS, SEMS] + [HBM] * len(after), out_specs=(IN_HBM,) * n,
        out_shape=tuple(pltpu.HBM(b.shape, b.dtype) for b in flying),
        input_output_aliases={t: t for t in range(n)},
        compiler_params=pltpu.CompilerParams(has_side_effects=DATAFLOW))(*flying, send_sems, recv_sems, *after))


def forward_halves(name, slots):
    n = len(slots)

    def body(*refs):
        ins, outs = refs[:n], refs[n:2 * n]
        send_sems, recv_sems = refs[2 * n:]
        x, y, c, chip, others = _place()
        sibling = (x, y, 1 - c)
        copies = []
        for t in range(n):
            mine = _half(slots[t].shape, c)
            for k, (ox, oy) in enumerate(others):
                cp = _remote(ins[t].at[(2 * ox + oy,) + mine], outs[t].at[(2 * ox + oy,) + mine], send_sems.at[t, k],
                             recv_sems.at[t, k], sibling)
                cp.start()
                copies.append(cp)
        for t in range(n):
            for k, (ox, oy) in enumerate(others):
                theirs = outs[t].at[(2 * ox + oy,) + _half(slots[t].shape, 1 - c)]
                _remote(theirs, theirs, send_sems.at[t, k], recv_sems.at[t, k], sibling).wait_recv()
        for cp in copies:
            cp.wait_send()

    return list(pl.pallas_call(
        body, name=name, in_specs=[HBM] * n, out_specs=[HBM] * n,
        out_shape=[jax.ShapeDtypeStruct(s.shape, s.dtype) for s in slots],
        input_output_aliases={t: t for t in range(n)},
        scratch_shapes=[pltpu.SemaphoreType.DMA((n, 3))] * 2,
        compiler_params=pltpu.CompilerParams(has_side_effects=True))(*slots))


def _half_block(shape, tr):
    r, c = shape[-2:]
    if _halves_by_rows(shape):
        per = r // 2 // tr
        return tr, c, lambda i, core: (core * per + i, 0)
    return tr, c // 2, lambda i, core: (i, core)


def add_half(name, place, grad, theirs):
    hr, hc = _half_shape(grad.shape)
    tr = _rows(hr, hc)
    br, bc, at = _half_block(grad.shape, tr)

    def body(place_ref, g_ref, t_ref, o_ref):
        o_ref[...] = (g_ref[...] + t_ref[...]).astype(o_ref.dtype)

    whole = pl.BlockSpec((None, br, bc), lambda j, i, pr: (j, i, 0))
    return pl.pallas_call(
        body, name=name,
        grid_spec=pltpu.PrefetchScalarGridSpec(
            num_scalar_prefetch=1, grid=(4, hr // tr),
            in_specs=[pl.BlockSpec((None, br, bc), lambda j, i, pr: (j,) + at(i, pr[0])), whole],
            out_specs=whole),
        out_shape=jax.ShapeDtypeStruct((4, hr, hc), BF16),
        compiler_params=_params("parallel", "parallel"))(place, grad, theirs)


def sum_chips(name, place, own, parts, shard_shape):
    _, hr, hc = parts.shape
    tr = _rows(hr, hc)
    br, bc, at = _half_block(shard_shape, tr)

    def body(place_ref, own_ref, p1, p2, p3, o_ref):
        o_ref[...] = own_ref[...].astype(F32) + p1[...].astype(F32) + p2[...].astype(F32) + p3[...].astype(F32)

    def slot(k):
        return pl.BlockSpec((None, br, bc), lambda i, pr: (lax.rem(pr[1] + k, 4), i, 0))

    return pl.pallas_call(
        body, name=name,
        grid_spec=pltpu.PrefetchScalarGridSpec(
            num_scalar_prefetch=1, grid=(hr // tr,), in_specs=[slot(0), slot(1), slot(2), slot(3)],
            out_specs=pl.BlockSpec((br, bc), lambda i, pr: at(i, pr[0]))),
        out_shape=jax.ShapeDtypeStruct(tuple(shard_shape), F32),
        compiler_params=_params("parallel"))(place, own, parts, parts, parts)


def join_halves(name, halves):
    n = len(halves)

    def body(*refs):
        ins, outs = refs[:n], refs[n:2 * n]
        send_sems, recv_sems = refs[2 * n:]
        x, y, c, _, _ = _place()
        copies = []
        for t in range(n):
            mine = _half(halves[t].shape, c)
            cp = _remote(ins[t].at[mine], outs[t].at[mine], send_sems.at[t], recv_sems.at[t], (x, y, 1 - c))
            cp.start()
            copies.append(cp)
        for t in range(n):
            theirs = outs[t].at[_half(halves[t].shape, 1 - c)]
            _remote(theirs, theirs, send_sems.at[t], recv_sems.at[t], (x, y, 1 - c)).wait_recv()
        for cp in copies:
            cp.wait_send()

    return pl.pallas_call(
        body, name=name, in_specs=[HBM] * n, out_specs=[HBM] * n,
        out_shape=[jax.ShapeDtypeStruct(h.shape, h.dtype) for h in halves],
        input_output_aliases={t: t for t in range(n)},
        scratch_shapes=[pltpu.SemaphoreType.DMA((n,))] * 2,
        compiler_params=pltpu.CompilerParams(has_side_effects=True))(*halves)


def allreduce_small(name, buf):
    rows = buf.shape[0]

    def body(in_ref, out_ref, slots, send_sems, recv_sems):
        x, y, c, _, _ = _place()
        me = 4 * x + 2 * y + c
        slots[me] = in_ref[...]
        peers = []
        for k in range(1, 8):
            px = 1 - x if k & 4 else x
            py = 1 - y if k & 2 else y
            pc = 1 - c if k & 1 else c
            peers.append((px, py, pc))
        copies = []
        for k, peer in enumerate(peers):
            cp = _remote(in_ref, slots.at[me], send_sems.at[k], recv_sems.at[k], peer)
            cp.start()
            copies.append(cp)
        for k, (px, py, pc) in enumerate(peers):
            slot = slots.at[4 * px + 2 * py + pc]
            _remote(slot, slot, send_sems.at[k], recv_sems.at[k], (px, py, pc)).wait_recv()
        for cp in copies:
            cp.wait_send()
        acc = slots[0]
        for d in range(1, 8):
            acc = acc + slots[d]
        out_ref[...] = acc

    vm = pl.BlockSpec(memory_space=pltpu.VMEM)
    return pl.pallas_call(
        body, name=name, in_specs=[vm], out_specs=vm, out_shape=jax.ShapeDtypeStruct(buf.shape, F32),
        scratch_shapes=[pltpu.VMEM((8, rows, LANES), F32), pltpu.SemaphoreType.DMA((7,)), pltpu.SemaphoreType.DMA((7,))],
        compiler_params=pltpu.CompilerParams(has_side_effects=True, vmem_limit_bytes=VMEM_LIMIT))(buf)


def cast_layer(name, place, w, layer):
    _, r, cols = w.shape
    tr = _rows(r, cols)

    def body(place_ref, w_ref, o_ref):
        o_ref[...] = w_ref[...].astype(BF16)

    return pl.pallas_call(
        body, name=name,
        grid_spec=pltpu.PrefetchScalarGridSpec(
            num_scalar_prefetch=1, grid=(r // tr,),
            in_specs=[pl.BlockSpec((None, tr, cols), lambda i, pr: (layer, i, 0))],
            out_specs=pl.BlockSpec((None, tr, cols), lambda i, pr: (pr[1], i, 0))),
        out_shape=jax.ShapeDtypeStruct((4, r, cols), BF16),
        compiler_params=_params("parallel"))(place, w)


def _adamw_math(g, w, m, v):
    m = ADAM_B1 * m + (1.0 - ADAM_B1) * g
    v = ADAM_B2 * v + (1.0 - ADAM_B2) * (g * g)
    m_hat = m / (1.0 - ADAM_B1 ** ADAM_STEP)
    v_hat = v / (1.0 - ADAM_B2 ** ADAM_STEP)
    delta = -ADAM_LR * (m_hat / (jnp.sqrt(v_hat) + ADAM_EPS) + ADAM_WD * w)
    return delta, m, v


def adamw_layer(name, layer, g, w, m, v, prev=None):
    nl, r, cols = w.shape
    tr = _rows(r, cols, 8)

    def body(*refs):
        g_ref, w_ref, m_ref, v_ref = refs[:4]
        og, od, om, ov = refs[-4:]
        gv = g_ref[...]
        delta, m2, v2 = _adamw_math(gv, w_ref[...], m_ref[...], v_ref[...])
        og[...] = gv
        od[...] = delta
        om[...] = m2
        ov[...] = v2

    lay = pl.BlockSpec((None, tr, cols), lambda i: (layer, i, 0))
    ins = [g, w, m, v] + (list(prev) if prev is not None else [])
    return pl.pallas_call(
        body, name=name, grid=(r // tr,),
        in_specs=[pl.BlockSpec((tr, cols), lambda i: (i, 0)), lay, lay, lay] + ([HBM] * 4 if prev is not None else []),
        out_specs=[lay] * 4, out_shape=[jax.ShapeDtypeStruct((nl, r, cols), F32)] * 4,
        input_output_aliases={4 + k: k for k in range(4)} if prev is not None else {},
        compiler_params=_params("parallel"))(*ins)


def adamw_packed(name, g, w, m, v):
    def body(g_ref, w_ref, m_ref, v_ref, od, om, ov):
        delta, m2, v2 = _adamw_math(g_ref[...], w_ref[...], m_ref[...], v_ref[...])
        od[...] = delta
        om[...] = m2
        ov[...] = v2

    vm = pl.BlockSpec(memory_space=pltpu.VMEM)
    return pl.pallas_call(
        body, name=name, in_specs=[vm] * 4, out_specs=[vm] * 3, out_shape=[jax.ShapeDtypeStruct(g.shape, F32)] * 3,
        compiler_params=pltpu.CompilerParams(vmem_limit_bytes=VMEM_LIMIT))(g, w, m, v)


def _pack(arrays):
    flat = jnp.concatenate([a.reshape(-1) for a in arrays])
    pad = (-flat.shape[0]) % (8 * LANES)
    return jnp.pad(flat, (0, pad)).reshape(-1, LANES)


def _unpack(buf, shapes):
    flat = buf.reshape(-1)
    out, off = [], 0
    for shp in shapes:
        size = int(np.prod(shp))
        out.append(flat[off:off + size].reshape(shp))
        off += size
    return out


MIXER_W = ("w_in", "w_uq", "w_ukv", "w_out")
FFN_W = ("w_up", "w_down")
BIG = MIXER_W + FFN_W
TRANSPOSED = ("w_in", "w_uq")
COLUMN_CUT = ("w_ukv", "w_up")
SMALL = ("attn_norm", "a_q_norm", "a_k_norm", "c_q_norm", "c_kv_norm", "out_norm", "ffn_norm", "conv_b", "final_norm")
WEIGHTS = ("attn_norm", "w_in", "a_q_norm", "a_k_norm", "c_q_norm", "c_kv_norm", "w_uq", "w_ukv", "out_norm", "w_out",
           "ffn_norm", "w_up", "conv_w", "conv_b", "w_down", "final_norm")
INPUTS = ("x",) + WEIGHTS + ("loss_target",) + tuple("m_" + n for n in WEIGHTS) + tuple("v_" + n for n in WEIGHTS)


def _columns(g):
    return jnp.transpose(g, (1, 0, 2)).reshape(g.shape[1], 4 * g.shape[2])


def _uncolumns(w):
    r, c4 = w.shape
    return jnp.transpose(w.reshape(r, 4, c4 // 4), (1, 0, 2))


def _compute_layouts(full, conv_w, small, layer):
    p = {n: small[n][layer] for n in SMALL if n != "final_norm"}
    if "w_in" in full:
        p["w_in_t"] = full["w_in"].reshape(-1, full["w_in"].shape[2])
    if "w_uq" in full:
        uq_t = full["w_uq"].reshape(C_HEADS, HEAD + C_ROPE, C_RANK)
        p["w_uq_t"] = jnp.concatenate([uq_t[:, :HEAD].reshape(C_W, C_RANK),
                                       jnp.pad(uq_t[:, HEAD:], ((0, 0), (0, HEAD - C_ROPE), (0, 0))).reshape(C_W, C_RANK)], axis=0)
        p["w_ukv"] = _columns(full["w_ukv"])
        p["w_out"] = full["w_out"].reshape(-1, full["w_out"].shape[2])
    if "w_up" in full:
        up = full["w_up"]
        p["w_up"] = up.reshape(2, 2, up.shape[1], up.shape[2])
        p["w_down"] = full["w_down"].reshape(-1, full["w_down"].shape[2])
        fh = conv_w.shape[1] // 2
        taps = jnp.transpose(conv_w.reshape(3, 2, fh), (1, 0, 2))
        p["cwb"] = jnp.concatenate([taps, small["conv_b"][layer].reshape(2, 1, fh), jnp.zeros((2, 4, fh), F32)], axis=1)
    return p


def _shard_layout(name, g):
    if name == "w_in":
        return g.reshape(4, -1, g.shape[1])
    if name == "w_uq":
        uq_t = jnp.concatenate([g[:C_W].reshape(C_HEADS, HEAD, C_RANK),
                                g[C_W:].reshape(C_HEADS, HEAD, C_RANK)[:, :C_ROPE]], axis=1)
        return uq_t.reshape(4, -1, C_RANK)
    if name == "w_ukv":
        return _uncolumns(g)
    if name == "w_up":
        return g.reshape(4, g.shape[2], g.shape[3])
    return g.reshape(4, -1, g.shape[1])


def _conv_grads(dcwb):
    fh = dcwb.shape[2]
    return jnp.transpose(dcwb[:, 0:3, :], (1, 0, 2)).reshape(3, 2 * fh), dcwb[:, 3, :].reshape(2 * fh)


def _shard_layouts(g):
    return ({n: _shard_layout(n, g[n]) for n in BIG},) + _conv_grads(g["cwb"])


def kernel(x, attn_norm, w_in, a_q_norm, a_k_norm, c_q_norm, c_kv_norm, w_uq, w_ukv, out_norm, w_out, ffn_norm, w_up, conv_w, conv_b, w_down, final_norm, loss_target, m_attn_norm, m_w_in, m_a_q_norm, m_a_k_norm, m_c_q_norm, m_c_kv_norm, m_w_uq, m_w_ukv, m_out_norm, m_w_out, m_ffn_norm, m_w_up, m_conv_w, m_conv_b, m_w_down, m_final_norm, v_attn_norm, v_w_in, v_a_q_norm, v_a_k_norm, v_c_q_norm, v_c_kv_norm, v_w_uq, v_w_ukv, v_out_norm, v_w_out, v_ffn_norm, v_w_up, v_conv_w, v_conv_b, v_w_down, v_final_norm):
    a = dict(zip(INPUTS, (x, attn_norm, w_in, a_q_norm, a_k_norm, c_q_norm, c_kv_norm, w_uq, w_ukv, out_norm, w_out, ffn_norm, w_up, conv_w, conv_b, w_down, final_norm, loss_target, m_attn_norm, m_w_in, m_a_q_norm, m_a_k_norm, m_c_q_norm, m_c_kv_norm, m_w_uq, m_w_ukv, m_out_norm, m_w_out, m_ffn_norm, m_w_up, m_conv_w, m_conv_b, m_w_down, m_final_norm, v_attn_norm, v_w_in, v_a_q_norm, v_a_k_norm, v_c_q_norm, v_c_kv_norm, v_w_uq, v_w_ukv, v_out_norm, v_w_out, v_ffn_norm, v_w_up, v_conv_w, v_conv_b, v_w_down, v_final_norm)))
    nl = w_in.shape[0]
    seq = x.shape[1]
    for n in TRANSPOSED:
        for kind in ("", "m_", "v_"):
            a[kind + n] = jnp.swapaxes(a[kind + n], 1, 2)
    chip = 2 * lax.axis_index("x") + lax.axis_index("y")
    place = jnp.stack([lax.axis_index("c"), chip]).astype(jnp.int32)
    tabs = _rope_tables(seq)

    assert nl == 2
    quarter = conv_w.shape[2]
    taps = jnp.pad(conv_w, ((0, 0), (0, 8 - conv_w.shape[1]), (0, 0))).reshape(nl * 8, quarter)
    taps_full = gather_small("gather_taps", taps).reshape(4, nl, 8, quarter)
    conv_full = [jnp.transpose(taps_full[:, l, 0:3, :], (1, 0, 2)).reshape(3, 4 * quarter) for l in range(nl)]
    flights = {}
    here, after = place, taps_full
    for l in range(nl):
        for key, names in ((f"in{l}", ("w_in",)), (f"mix{l}", MIXER_W[1:]), (f"ffn{l}", FFN_W)):
            slots = [cast_layer(f"cast{l}_{n}", here, a[n], l) for n in names]
            send_sems, recv_sems, flying, token = split_start(f"gather_{key}_start", slots, _gather_plan, after)
            flights[key] = (names, send_sems, recv_sems, flying)
            here = after = place + token[0, 0].astype(jnp.int32)
    started = token[0, 0]

    def land(key, after):
        names, send_sems, recv_sems, flying = flights[key]
        landed = split_wait(f"gather_{key}_wait", send_sems, recv_sems, flying, _gather_plan, after)
        return dict(zip(names, forward_halves(f"gather_{key}_forward", landed)))

    h = x[0]
    layers, saved = [], []
    for l in range(nl):
        p = _compute_layouts(land(f"in{l}", [taps_full if l == 0 else h]), conv_full[l], a, l)
        if l == 0:
            p["attn_norm"] = p["attn_norm"] + started
        sv = _mixers_ab_fwd(h, p, tabs)
        p.update(_compute_layouts(land(f"mix{l}", [sv["yb"]]), conv_full[l], a, l))
        x1 = _mixer_c_out_fwd(h, sv, p, tabs)
        p.update(_compute_layouts(land(f"ffn{l}", [x1]), conv_full[l], a, l))
        h = _ffn_fwd(x1, p, sv)
        layers.append(p)
        saved.append(sv)
    dm = h.shape[1]
    yf = rmsnorm_fwd("final_norm", h, final_norm, dm, F32)
    loss, dyf = loss_head("loss_head", yf, loss_target[0])
    loss = lax.psum(loss, ("x", "y", "c"))
    dx, d_final = rmsnorm_bwd("final_norm_bwd", dyf, h, final_norm, dm, (F32,))

    def swap_begin(tag, names, g):
        g_list = [_shard_layout(n, g[n]) for n in names]
        theirs = [lax.empty((4,) + _half_shape(s.shape), s.dtype) for s in g_list]
        send_sems, recv_sems, flying, token = split_start(f"{tag}_sibling_start", g_list + theirs, _sibling_plan, place)
        return (names, send_sems, recv_sems, flying), token[0, 0]

    def exchange_begin(tag, swap, after):
        names, send_sems, recv_sems, flying = swap
        landed = split_wait(f"{tag}_sibling_wait", send_sems, recv_sems, flying, _sibling_plan, after)
        g_list, theirs = landed[:len(names)], landed[len(names):]
        sums = [add_half(f"{tag}_add_{n}", place, gr, t) for n, gr, t in zip(names, g_list, theirs)]
        parts = [lax.empty(s.shape, s.dtype) for s in sums]
        send_sems, recv_sems, flying, token = split_start(f"{tag}_chips_start", sums + parts, _exchange_plan, place)
        return (names, [gr.shape[1:] for gr in g_list], send_sems, recv_sems, flying), token

    def exchange_end(tag, flight, after):
        names, shard_shapes, send_sems, recv_sems, flying = flight
        landed = split_wait(f"{tag}_chips_wait", send_sems, recv_sems, flying, _exchange_plan, after)
        sums, parts = landed[:len(names)], landed[len(names):]
        halves = [sum_chips(f"{tag}_sum_{n}", place, s, p, shp) for n, s, p, shp in zip(names, sums, parts, shard_shapes)]
        return dict(zip(names, join_halves(f"{tag}_join", halves)))

    grads, conv_grads, flights_f, flights_m = [None] * nl, [None] * nl, [None] * nl, [None] * nl
    after = 0.0
    for l in reversed(range(nl)):
        sv, p = saved[l], layers[l]
        dxn, g = _ffn_bwd(dx, sv, p, after)
        swap_f, after = swap_begin(f"reduce{l}f", FFN_W, g)
        dx1, dy, g_out = _out_proj_bwd(dxn, dx, sv, p, after)
        part, g_c = _mixer_c_bwd(dy, sv, p, tabs)
        flights_f[l], token = exchange_begin(f"reduce{l}f", swap_f, [part[2]])
        dhn, g_ab = _mixers_ab_bwd(part, sv, p, tabs, token[0, 0])
        g.update(g_out)
        g.update(g_c)
        g.update(g_ab)
        swap_m, after = swap_begin(f"reduce{l}m", MIXER_W, g)
        dx, g_norm = _attn_norm_bwd(dhn, dx1, sv, p, after)
        g.update(g_norm)
        flights_m[l], token = exchange_begin(f"reduce{l}m", swap_m, [dx])
        after = token[0, 0]
        grads[l], conv_grads[l] = g, _conv_grads(g["cwb"])
    grad_x = dx

    out = {}
    reduced1 = exchange_end("reduce1f", flights_f[1], [token])
    for n in FFN_W:
        out[n] = adamw_layer(f"adamw1_{n}", 1, reduced1[n], a[n], a["m_" + n], a["v_" + n])
    reduced1 = exchange_end("reduce1m", flights_m[1], [out[n][3] for n in FFN_W])
    for n in MIXER_W:
        out[n] = adamw_layer(f"adamw1_{n}", 1, reduced1[n], a[n], a["m_" + n], a["v_" + n])

    small_g = {n: jnp.stack([grads[l][n] for l in range(nl)]) for n in SMALL if n not in ("conv_b", "final_norm")}
    small_g["conv_b"] = jnp.stack([cg[1] for cg in conv_grads])
    small_g["final_norm"] = d_final
    conv_w_g = jnp.stack([cg[0] for cg in conv_grads])
    shapes = [a[n].shape for n in SMALL] + [conv_w_g.shape]
    summed = _unpack(allreduce_small("reduce_small", _pack([small_g[n] for n in SMALL] + [conv_w_g])), shapes)
    small_g = dict(zip(SMALL, summed[:-1]))
    small_g["conv_w"] = lax.dynamic_slice_in_dim(summed[-1], chip * quarter, quarter, axis=2)
    names = SMALL + ("conv_w",)
    shapes = [a[n].shape for n in names]
    d_small, m_small, v_small = adamw_packed(
        "adamw_small", _pack([small_g[n] for n in names]), _pack([a[n] for n in names]),
        _pack([a["m_" + n] for n in names]), _pack([a["v_" + n] for n in names]))

    reduced0 = exchange_end("reduce0f", flights_f[0], [out[n][3] for n in BIG] + [d_small])
    for n in FFN_W:
        out[n] = adamw_layer(f"adamw0_{n}", 0, reduced0[n], a[n], a["m_" + n], a["v_" + n], prev=out[n])
    reduced0 = exchange_end("reduce0m", flights_m[0], [out[n][3] for n in FFN_W])
    for n in MIXER_W:
        out[n] = adamw_layer(f"adamw0_{n}", 0, reduced0[n], a[n], a["m_" + n], a["v_" + n], prev=out[n])
    for n, d_, m_, v_ in zip(names, _unpack(d_small, shapes), _unpack(m_small, shapes), _unpack(v_small, shapes)):
        out[n] = (small_g[n], d_, m_, v_)
    for n in TRANSPOSED:
        out[n] = [jnp.swapaxes(o, 1, 2) for o in out[n]]

    return (loss, grad_x[None]) + tuple(out[n][k] for k in range(4) for n in WEIGHTS)
```

```python
import functools

import jax
import jax.numpy as jnp
import numpy as np
from jax import lax
from jax.experimental import pallas as pl
from jax.experimental.pallas import tpu as pltpu

F32 = jnp.float32
BF16 = jnp.bfloat16
MESH = pl.DeviceIdType.MESH

V7X_VMEM_BYTES = 64 * 1024 * 1024
VMEM_LIMIT = V7X_VMEM_BYTES - 6 * 1024 * 1024
LANES = 128
ELEMENTWISE_BLOCK_BYTES = 4 * 1024 * 1024

HEAD = 128
A_HEADS, A_KV = 6, 2
B_HEADS, B_GROUPS = 4, 3
B_DILATIONS = (1, 4, 16)
B_HALF = 64
C_HEADS, C_RANK, C_ROPE = 6, 512, 64
GRID_W = 64
A_THETA, B_THETA, C_THETA = 10000.0, 500000.0, 10000.0
B_ROPE_DIM = 32
EPS = 1e-6
A_W, B_W, C_W = A_HEADS * HEAD, B_HEADS * HEAD, C_HEADS * HEAD
O_AQ, O_AK, O_AV = 0, 768, 1024
O_BQ, O_BK, O_BV = 1280, 2816, 3328
O_CQ, O_CKV, O_CKR = 3840, 4352, 4864
IN_W = 4928
PROJ_W = 5120

ADAM_LR, ADAM_B1, ADAM_B2, ADAM_EPS, ADAM_WD, ADAM_STEP = 0.001, 0.9, 0.999, 1e-08, 0.01, 10

NN = (((1,), (0,)), ((), ()))
NT = (((1,), (1,)), ((), ()))
TN = (((0,), (0,)), ((), ()))


def _dot(a, b, dims=NN):
    return lax.dot_general(a, b, dims, preferred_element_type=F32)


def _params(*sem):
    return pltpu.CompilerParams(dimension_semantics=sem if sem else None, vmem_limit_bytes=VMEM_LIMIT)


def _tile(n, target, unit=LANES):
    if n <= target:
        return n
    best = 0
    for t in range(unit, target + 1, unit):
        if n % t == 0:
            best = t
    return best if best else n


def _rows(r, width, itemsize=4):
    return _tile(r, max(8, ELEMENTWISE_BLOCK_BYTES // (width * itemsize)), 8)


def _matmul_call(name, dims, grid, a_spec, b_spec, o_spec, out_shape, acc_shape, add_spec=None):
    nk = grid[2]

    def body(*refs):
        a_ref, b_ref = refs[:2]
        add_ref = None if add_spec is None else refs[2]
        o_ref = refs[2 if add_spec is None else 3]

        def finish(r):
            if add_ref is not None:
                r = r + add_ref[...]
            o_ref[...] = r.astype(o_ref.dtype)

        if nk == 1:
            finish(_dot(a_ref[...], b_ref[...], dims))
            return
        acc = refs[-1]
        k = pl.program_id(2)

        @pl.when(k == 0)
        def _():
            acc[...] = _dot(a_ref[...], b_ref[...], dims)

        if nk > 2:
            @pl.when((k > 0) & (k < nk - 1))
            def _():
                acc[...] += _dot(a_ref[...], b_ref[...], dims)

        @pl.when(k == nk - 1)
        def _():
            finish(acc[...] + _dot(a_ref[...], b_ref[...], dims))

    in_specs = [a_spec, b_spec] + ([] if add_spec is None else [add_spec])
    return pl.pallas_call(
        body, name=name, grid=grid, in_specs=in_specs, out_specs=o_spec, out_shape=out_shape,
        scratch_shapes=[pltpu.VMEM(acc_shape, F32)] if nk > 1 else [],
        compiler_params=_params("parallel", "parallel", "arbitrary"))


def matmul(name, a, b, mode, out_dtype, add=None, tm=1024, tn=1024, tk=2816):
    if mode == "nn":
        (m, k), (k2, n) = a.shape, b.shape
    elif mode == "nt":
        (m, k), (n, k2) = a.shape, b.shape
    else:
        (k, m), (k2, n) = a.shape, b.shape
    assert k == k2, (name, a.shape, b.shape)
    tm, tn, tk = _tile(m, tm), _tile(n, tn), _tile(k, tk)
    grid = (m // tm, n // tn, k // tk)
    if mode == "tn":
        a_spec = pl.BlockSpec((tk, tm), lambda i, j, kk: (kk, i))
    else:
        a_spec = pl.BlockSpec((tm, tk), lambda i, j, kk: (i, kk))
    whole_b = {"pipeline_mode": pl.Buffered(1)} if grid[1] == 1 and grid[2] == 1 and grid[0] > 1 else {}
    if mode == "nt":
        b_spec = pl.BlockSpec((tn, tk), lambda i, j, kk: (j, kk), **whole_b)
    else:
        b_spec = pl.BlockSpec((tk, tn), lambda i, j, kk: (kk, j), **whole_b)
    o_spec = pl.BlockSpec((tm, tn), lambda i, j, kk: (i, j))
    dims = {"nn": NN, "nt": NT, "tn": TN}[mode]
    call = _matmul_call(name, dims, grid, a_spec, b_spec, o_spec, jax.ShapeDtypeStruct((m, n), out_dtype),
                        (tm, tn), None if add is None else o_spec)
    return call(a, b) if add is None else call(a, b, add)


def rmsnorm_fwd(name, x, g, width, out_dtype):
    r, cols = x.shape
    nb = cols // width
    tr = _rows(r, width)

    def body(x_ref, g_ref, o_ref):
        xv = x_ref[...]
        rs = lax.rsqrt(jnp.mean(xv * xv, axis=-1, keepdims=True) + EPS)
        o_ref[...] = (xv * rs * g_ref[...]).astype(o_ref.dtype)

    blk = pl.BlockSpec((tr, width), lambda i, j: (i, j))
    return pl.pallas_call(
        body, name=name, grid=(r // tr, nb),
        in_specs=[blk, pl.BlockSpec((1, width), lambda i, j: (0, 0))], out_specs=blk,
        out_shape=jax.ShapeDtypeStruct((r, cols), out_dtype),
        compiler_params=_params("parallel", "parallel"))(x, g.reshape(1, width))


def rmsnorm_bwd(name, dy, x, g, width, out_dtypes=(F32,), add=None):
    r, cols = x.shape
    nb = cols // width
    tr = _rows(r, width)
    n_out = len(out_dtypes)

    def body(*refs):
        dy_ref, x_ref, g_ref = refs[:3]
        add_ref = refs[3] if add is not None else None
        outs = refs[-(n_out + 1):-1]
        dg_ref = refs[-1]
        xv = x_ref[...]
        dyv = dy_ref[...].astype(F32)
        rs = lax.rsqrt(jnp.mean(xv * xv, axis=-1, keepdims=True) + EPS)
        xh = xv * rs
        dyg = dyv * g_ref[...]
        dx = rs * (dyg - xh * jnp.mean(dyg * xh, axis=-1, keepdims=True))
        if add_ref is not None:
            dx = dx + add_ref[...]
        for o in outs:
            o[...] = dx.astype(o.dtype)

        @pl.when((pl.program_id(0) == 0) & (pl.program_id(1) == 0))
        def _():
            dg_ref[...] = jnp.zeros_like(dg_ref)

        dg_ref[...] += jnp.sum(dyv * xh, axis=0, keepdims=True)

    blk = pl.BlockSpec((tr, width), lambda i, j: (i, j))
    vec = pl.BlockSpec((1, width), lambda i, j: (0, 0))
    ins = [dy, x, g.reshape(1, width)] + ([add] if add is not None else [])
    res = pl.pallas_call(
        body, name=name, grid=(r // tr, nb),
        in_specs=[blk, blk, vec] + ([blk] if add is not None else []),
        out_specs=[blk] * n_out + [vec],
        out_shape=[jax.ShapeDtypeStruct((r, cols), dt) for dt in out_dtypes] + [jax.ShapeDtypeStruct((1, width), F32)],
        compiler_params=_params("arbitrary", "arbitrary"))(*ins)
    return tuple(res[:n_out]) + (res[n_out].reshape(width),)


def _rope_angles(pos, dim, theta):
    inv = theta ** (-jnp.arange(0, dim, 2, dtype=F32) / dim)
    ang = pos.astype(F32)[:, None] * inv[None, :]
    return jnp.cos(ang), jnp.sin(ang)


def _rope_tables(s):
    rows = s // GRID_W
    row_pos = jnp.repeat(jnp.arange(rows), GRID_W)
    col_pos = jnp.tile(jnp.arange(GRID_W), rows)
    t_pos = jnp.arange(s)
    z = lambda n: jnp.zeros((s, n), F32)
    o = lambda n: jnp.ones((s, n), F32)
    cr, sr = _rope_angles(row_pos, HEAD // 2, A_THETA)
    cc, sc = _rope_angles(col_pos, HEAD // 2, A_THETA)
    tab_a = (jnp.concatenate([cr, cr, cc, cc], 1), jnp.concatenate([-sr, z(32), -sc, z(32)], 1),
             jnp.concatenate([z(32), sr, z(32), sc], 1), 32)
    cp, sp = _rope_angles(t_pos, B_ROPE_DIM, B_THETA)
    tab_b = (jnp.concatenate([cp, cp, o(96)], 1), jnp.concatenate([-sp, z(112)], 1),
             jnp.concatenate([z(16), sp, z(96)], 1), 16)
    cm, sm = _rope_angles(t_pos, C_ROPE, C_THETA)
    tab_c = (jnp.concatenate([cm, cm, o(64)], 1), jnp.concatenate([-sm, z(96)], 1),
             jnp.concatenate([z(32), sm, z(64)], 1), 32)

    def transposed(tab):
        c, s1, s2, h = tab
        return (c, jnp.roll(s2, -h, axis=1), jnp.roll(s1, h, axis=1), h)

    return {k: (t, transposed(t)) for k, t in (("a", tab_a), ("b", tab_b), ("c", tab_c))}


def rope(name, x, tab, out_dtype):
    c, s1, s2, h = tab
    s, cols = x.shape
    nh = cols // HEAD
    ts = _rows(s, HEAD)

    def body(x_ref, c_ref, s1_ref, s2_ref, o_ref):
        xv = x_ref[...].astype(F32)
        out = xv * c_ref[...] + pltpu.roll(xv, HEAD - h, 1) * s1_ref[...] + pltpu.roll(xv, h, 1) * s2_ref[...]
        o_ref[...] = out.astype(o_ref.dtype)

    blk = pl.BlockSpec((ts, HEAD), lambda i, j: (i, j))
    tb = pl.BlockSpec((ts, HEAD), lambda i, j: (i, 0))
    return pl.pallas_call(
        body, name=name, grid=(s // ts, nh), in_specs=[blk, tb, tb, tb], out_specs=blk,
        out_shape=jax.ShapeDtypeStruct((s, cols), out_dtype),
        compiler_params=_params("parallel", "parallel"))(x, c, s1, s2)


PREP_ROWS = 256


def _rope_of(x, tab, h):
    c, s1, s2 = tab
    return x * c + pltpu.roll(x, HEAD - h, 1) * s1 + pltpu.roll(x, h, 1) * s2


def _heads(start, n):
    return [(j, slice(start + HEAD * j, start + HEAD * (j + 1))) for j in range(n)]


def mixer_prep(name, proj, p, tabs):
    s = proj.shape[0]
    tr = PREP_ROWS
    (ca, s1a, s2a, ha), (cb, s1b, s2b, hb) = tabs["a"][0], tabs["b"][0]

    def body(proj_ref, gaq, gak, gcq, gckv, ca_r, s1a_r, s2a_r, cb_r, s1b_r, s2b_r,
             qa_ref, ka_ref, av_ref, qb_ref, kb_ref, cqn_ref, ckvn_ref):
        ta = (ca_r[...], s1a_r[...], s2a_r[...])
        tb = (cb_r[...], s1b_r[...], s2b_r[...])

        def normed(x, g):
            return x * lax.rsqrt(jnp.mean(x * x, axis=-1, keepdims=True) + EPS) * g

        for (src0, n, gain, out) in ((O_AQ, A_HEADS, gaq, qa_ref), (O_AK, A_KV, gak, ka_ref)):
            for j, cols in _heads(src0, n):
                out[:, HEAD * j:HEAD * (j + 1)] = _rope_of(normed(proj_ref[:, cols], gain[...]), ta, ha).astype(BF16)
        av_ref[...] = proj_ref[:, O_AV:O_BQ].astype(BF16)
        for (src0, n, out) in ((O_BQ, B_GROUPS * B_HEADS, qb_ref), (O_BK, B_HEADS, kb_ref)):
            for j, cols in _heads(src0, n):
                out[:, HEAD * j:HEAD * (j + 1)] = _rope_of(proj_ref[:, cols], tb, hb)
        cqn_ref[...] = normed(proj_ref[:, O_CQ:O_CKV], gcq[...]).astype(BF16)
        ckvn_ref[...] = normed(proj_ref[:, O_CKV:O_CKR], gckv[...]).astype(BF16)

    rows = lambda w: pl.BlockSpec((tr, w), lambda i: (i, 0))
    vec = lambda w: pl.BlockSpec((1, w), lambda i: (0, 0))
    widths = (A_W, A_KV * HEAD, A_KV * HEAD, B_GROUPS * B_W, B_W, C_RANK, C_RANK)
    dtypes = (BF16, BF16, BF16, F32, F32, BF16, BF16)
    return pl.pallas_call(
        body, name=name, grid=(s // tr,),
        in_specs=[rows(proj.shape[1]), vec(HEAD), vec(HEAD), vec(C_RANK), vec(C_RANK)] + [rows(HEAD)] * 6,
        out_specs=[rows(w) for w in widths],
        out_shape=[jax.ShapeDtypeStruct((s, w), dt) for w, dt in zip(widths, dtypes)],
        compiler_params=_params("parallel"))(
            proj, p["a_q_norm"].reshape(1, HEAD), p["a_k_norm"].reshape(1, HEAD), p["c_q_norm"].reshape(1, C_RANK),
            p["c_kv_norm"].reshape(1, C_RANK), ca, s1a, s2a, cb, s1b, s2b)


def mixer_prep_bwd(name, proj, p, tabs, dqa, dka, dva, dqb, dkb, dvb, dcqn, dckvn, dckr):
    s = proj.shape[0]
    tr = PREP_ROWS
    (ca, s1a, s2a, ha), (cb, s1b, s2b, hb) = tabs["a"][1], tabs["b"][1]

    def body(proj_ref, dqa_ref, dka_ref, dva_ref, dqb_ref, dkb_ref, dvb_ref, dcqn_ref, dckvn_ref, dckr_ref,
             gaq, gak, gcq, gckv, ca_r, s1a_r, s2a_r, cb_r, s1b_r, s2b_r, dproj_ref, dgaq, dgak, dgcq, dgckv):
        ta = (ca_r[...], s1a_r[...], s2a_r[...])
        tb = (cb_r[...], s1b_r[...], s2b_r[...])

        @pl.when(pl.program_id(0) == 0)
        def _():
            for ref in (dgaq, dgak, dgcq, dgckv):
                ref[...] = jnp.zeros_like(ref)

        def norm_bwd(dy, x, gain, dgain):
            rs = lax.rsqrt(jnp.mean(x * x, axis=-1, keepdims=True) + EPS)
            xh = x * rs
            dyg = dy * gain[...]
            dgain[...] += jnp.sum(dy * xh, axis=0, keepdims=True)
            return rs * (dyg - xh * jnp.mean(dyg * xh, axis=-1, keepdims=True))

        for (dst0, n, d_ref, gain, dgain) in ((O_AQ, A_HEADS, dqa_ref, gaq, dgaq), (O_AK, A_KV, dka_ref, gak, dgak)):
            for j, cols in _heads(dst0, n):
                dy = _rope_of(d_ref[:, HEAD * j:HEAD * (j + 1)], ta, ha)
                dproj_ref[:, cols] = norm_bwd(dy, proj_ref[:, cols], gain, dgain).astype(BF16)
        dproj_ref[:, O_AV:O_BQ] = dva_ref[...].astype(BF16)
        for (dst0, n, d_ref) in ((O_BQ, B_GROUPS * B_HEADS, dqb_ref), (O_BK, B_HEADS, dkb_ref)):
            for j, cols in _heads(dst0, n):
                dproj_ref[:, cols] = _rope_of(d_ref[:, HEAD * j:HEAD * (j + 1)], tb, hb).astype(BF16)
        dproj_ref[:, O_BV:O_CQ] = dvb_ref[...].astype(BF16)
        dproj_ref[:, O_CQ:O_CKV] = norm_bwd(dcqn_ref[...], proj_ref[:, O_CQ:O_CKV], gcq, dgcq).astype(BF16)
        dproj_ref[:, O_CKV:O_CKR] = norm_bwd(dckvn_ref[...], proj_ref[:, O_CKV:O_CKR], gckv, dgckv).astype(BF16)
        dproj_ref[:, O_CKR:] = dckr_ref[:, :C_ROPE]

    rows = lambda w: pl.BlockSpec((tr, w), lambda i: (i, 0))
    vec = lambda w: pl.BlockSpec((1, w), lambda i: (0, 0))
    grads = (dqa, dka, dva, dqb, dkb, dvb, dcqn, dckvn, dckr)
    res = pl.pallas_call(
        body, name=name, grid=(s // tr,),
        in_specs=[rows(proj.shape[1])] + [rows(g.shape[1]) for g in grads]
        + [vec(HEAD), vec(HEAD), vec(C_RANK), vec(C_RANK)] + [rows(HEAD)] * 6,
        out_specs=[rows(proj.shape[1]), vec(HEAD), vec(HEAD), vec(C_RANK), vec(C_RANK)],
        out_shape=[jax.ShapeDtypeStruct(proj.shape, BF16)] + [jax.ShapeDtypeStruct((1, w), F32) for w in (HEAD, HEAD, C_RANK, C_RANK)],
        compiler_params=_params("arbitrary"))(
            proj, *grads, p["a_q_norm"].reshape(1, HEAD), p["a_k_norm"].reshape(1, HEAD), p["c_q_norm"].reshape(1, C_RANK),
            p["c_kv_norm"].reshape(1, C_RANK), ca, s1a, s2a, cb, s1b, s2b)
    return res[0], res[1].reshape(HEAD), res[2].reshape(HEAD), res[3].reshape(C_RANK), res[4].reshape(C_RANK)


OUT_GROUPS = ((0, A_W), (A_W, A_W + B_W), (A_W + B_W, A_W + B_W + C_W))


def out_norms(name, ya, yb, yc, gain):
    s = ya.shape[0]
    tr = PREP_ROWS
    width = OUT_GROUPS[-1][1]

    def body(ya_ref, yb_ref, yc_ref, g_ref, y_ref):
        for (lo, hi), src in zip(OUT_GROUPS, (ya_ref, yb_ref, yc_ref)):
            x = src[...]
            rs = lax.rsqrt(jnp.mean(x * x, axis=-1, keepdims=True) + EPS)
            y_ref[:, lo:hi] = (x * rs * g_ref[:, lo:hi]).astype(BF16)

    rows = lambda w: pl.BlockSpec((tr, w), lambda i: (i, 0))
    return pl.pallas_call(
        body, name=name, grid=(s // tr,),
        in_specs=[rows(A_W), rows(B_W), rows(C_W), pl.BlockSpec((1, width), lambda i: (0, 0))], out_specs=rows(width),
        out_shape=jax.ShapeDtypeStruct((s, width), BF16), compiler_params=_params("parallel"))(
            ya, yb, yc, gain.reshape(1, width))


def out_norms_bwd(name, dy, ya, yb, yc, gain):
    s = ya.shape[0]
    tr = PREP_ROWS
    width = OUT_GROUPS[-1][1]

    def body(dy_ref, ya_ref, yb_ref, yc_ref, g_ref, dya_ref, dyb_ref, dyc_ref, dg_ref):
        @pl.when(pl.program_id(0) == 0)
        def _():
            dg_ref[...] = jnp.zeros_like(dg_ref)

        for (lo, hi), src, dst in zip(OUT_GROUPS, (ya_ref, yb_ref, yc_ref), (dya_ref, dyb_ref, dyc_ref)):
            x = src[...]
            d = dy_ref[:, lo:hi]
            rs = lax.rsqrt(jnp.mean(x * x, axis=-1, keepdims=True) + EPS)
            xh = x * rs
            dg = d * g_ref[:, lo:hi]
            dst[...] = (rs * (dg - xh * jnp.mean(dg * xh, axis=-1, keepdims=True))).astype(dst.dtype)
            dg_ref[:, lo:hi] += jnp.sum(d * xh, axis=0, keepdims=True)

    rows = lambda w: pl.BlockSpec((tr, w), lambda i: (i, 0))
    vec = pl.BlockSpec((1, width), lambda i: (0, 0))
    dya, dyb, dyc, dg = pl.pallas_call(
        body, name=name, grid=(s // tr,),
        in_specs=[rows(width), rows(A_W), rows(B_W), rows(C_W), vec], out_specs=[rows(A_W), rows(B_W), rows(C_W), vec],
        out_shape=[jax.ShapeDtypeStruct((s, A_W), BF16), jax.ShapeDtypeStruct((s, B_W), F32),
                   jax.ShapeDtypeStruct((s, C_W), BF16), jax.ShapeDtypeStruct((1, width), F32)],
        compiler_params=_params("arbitrary"))(dy, ya, yb, yc, gain.reshape(1, width))
    return dya, dyb, dyc, dg.reshape(width)


ATTN_TQ = 256
ATTN_FWD_TQ = 512


def attn_fwd(name, parts, v, v_group, nheads, scale):
    s = v.shape[0]
    tq = ATTN_FWD_TQ
    npart = len(parts)

    def body(*refs):
        v_ref, o_ref, lse_ref = refs[2 * npart:]
        for r0 in range(0, tq, ATTN_TQ):
            rows = slice(r0, r0 + ATTN_TQ)
            sc = None
            for p in range(npart):
                t = _dot(refs[2 * p][rows, :], refs[2 * p + 1][...], NT)
                sc = t if sc is None else sc + t
            sc = sc * scale
            m = jnp.max(sc, axis=-1, keepdims=True)
            e = jnp.exp(sc - m)
            l = jnp.sum(e, axis=-1, keepdims=True)
            o_ref[rows, :] = _dot(e.astype(BF16), v_ref[...]) / l
            lse_ref[rows, :] = jnp.broadcast_to(m + jnp.log(l), (ATTN_TQ, HEAD))

    in_specs, ins = [], []
    for q, qoff, k, kg in parts:
        in_specs.append(pl.BlockSpec((tq, HEAD), lambda h, i, qoff=qoff: (i, qoff + h)))
        in_specs.append(pl.BlockSpec((s, HEAD), lambda h, i, kg=kg: (0, h // kg)))
        ins += [q, k]
    in_specs.append(pl.BlockSpec((s, HEAD), lambda h, i: (0, h // v_group)))
    out_blk = pl.BlockSpec((tq, HEAD), lambda h, i: (i, h))
    return pl.pallas_call(
        body, name=name, grid=(nheads, s // tq), in_specs=in_specs, out_specs=[out_blk, out_blk],
        out_shape=[jax.ShapeDtypeStruct((s, nheads * HEAD), F32)] * 2,
        compiler_params=_params("parallel", "parallel"))(*ins, v)


def attn_bwd(name, parts, v, v_group, lse, o, do, nheads, scale):
    s = v.shape[0]
    tq = ATTN_FWD_TQ
    npart = len(parts)

    def body(*refs):
        v_ref, lse_ref, o_ref, do_ref = refs[2 * npart:2 * npart + 4]
        outs = refs[2 * npart + 4:]
        dq_refs, dk_refs, dv_ref = outs[:npart], outs[npart:2 * npart], outs[2 * npart]
        h, i = pl.program_id(0), pl.program_id(1)

        @pl.when((i == 0) & (h % v_group == 0))
        def _():
            dv_ref[...] = jnp.zeros_like(dv_ref)

        for p in range(npart):
            @pl.when((i == 0) & (h % parts[p][3] == 0))
            def _(p=p):
                dk_refs[p][...] = jnp.zeros_like(dk_refs[p])

        for r0 in range(0, tq, ATTN_TQ):
            rows = slice(r0, r0 + ATTN_TQ)
            sc = None
            for p in range(npart):
                t = _dot(refs[2 * p][rows, :], refs[2 * p + 1][...], NT)
                sc = t if sc is None else sc + t
            pr = jnp.exp(sc * scale - lse_ref[rows, 0:1])
            dov = do_ref[rows, :]
            delta = jnp.sum(dov.astype(F32) * o_ref[rows, :], axis=-1, keepdims=True)
            dp = _dot(dov, v_ref[...], NT)
            ds = (pr * (dp - delta) * scale).astype(BF16)
            dv_ref[...] += _dot(pr.astype(BF16), dov, TN)
            for p in range(npart):
                dq_refs[p][rows, :] = _dot(ds, refs[2 * p + 1][...])
                dk_refs[p][...] += _dot(ds, refs[2 * p][rows, :], TN)

    in_specs, ins = [], []
    for q, qoff, k, kg in parts:
        in_specs.append(pl.BlockSpec((tq, HEAD), lambda h, i, qoff=qoff: (i, qoff + h)))
        in_specs.append(pl.BlockSpec((s, HEAD), lambda h, i, kg=kg: (0, h // kg)))
        ins += [q, k]
    hq_blk = pl.BlockSpec((tq, HEAD), lambda h, i: (i, h))
    in_specs += [pl.BlockSpec((s, HEAD), lambda h, i: (0, h // v_group)), hq_blk, hq_blk, hq_blk]
    out_specs = [hq_blk] * npart
    out_shape = [jax.ShapeDtypeStruct((s, nheads * HEAD), F32)] * npart
    for q, qoff, k, kg in parts:
        out_specs.append(pl.BlockSpec((s, HEAD), lambda h, i, kg=kg: (0, h // kg)))
        out_shape.append(jax.ShapeDtypeStruct((s, nheads // kg * HEAD), F32))
    out_specs.append(pl.BlockSpec((s, HEAD), lambda h, i: (0, h // v_group)))
    out_shape.append(jax.ShapeDtypeStruct((s, nheads // v_group * HEAD), F32))
    res = pl.pallas_call(
        body, name=name, grid=(nheads, s // tq), in_specs=in_specs, out_specs=out_specs, out_shape=out_shape,
        compiler_params=_params("arbitrary", "arbitrary"))(*ins, v, lse, o, do)
    return list(res[:npart]), list(res[npart:2 * npart]), res[2 * npart]


def _band_windows(lf):
    for ib in range(lf // HEAD):
        q0 = ib * HEAD
        yield q0, max(0, q0 - B_HALF), min(lf, q0 + HEAD + B_HALF)


def _band_mask(q0, lo, hi):
    qpos = q0 + lax.broadcasted_iota(jnp.int32, (HEAD, hi - lo), 0)
    kpos = lo + lax.broadcasted_iota(jnp.int32, (HEAD, hi - lo), 1)
    return jnp.abs(qpos - kpos) <= B_HALF


def _class_rows(r, d, start, size):
    return pl.ds(r + d * start, size, stride=d) if d > 1 else pl.ds(start, size)


def dilated_fwd(name, q, k, v_src, v_block):
    s = k.shape[0]
    scale = HEAD ** -0.5

    def body(q_ref, k_ref, v_ref, y_ref, lse_ref, o_scr, l_scr):
        g_now = pl.program_id(1)
        for g, d in enumerate(B_DILATIONS):
            @pl.when(g_now == g)
            def _(g=g, d=d):
                for r in range(d):
                    for q0, lo, hi in _band_windows(s // d):
                        mine = _class_rows(r, d, q0, HEAD)
                        keys = _class_rows(r, d, lo, hi - lo)
                        sc = _dot(q_ref[mine, :].astype(BF16), k_ref[keys, :].astype(BF16), NT) * scale
                        sc = jnp.where(_band_mask(q0, lo, hi), sc, -1e30)
                        m = jnp.max(sc, axis=-1, keepdims=True)
                        e = jnp.exp(sc - m)
                        l = jnp.sum(e, axis=-1, keepdims=True)
                        o_scr.at[g][mine, :] = _dot((e / l).astype(BF16), v_ref[keys, :].astype(BF16))
                        l_scr.at[g][mine, :] = jnp.broadcast_to(m + jnp.log(l), (HEAD, HEAD))

        @pl.when(g_now == B_GROUPS - 1)
        def _():
            a, b, c = l_scr[0], l_scr[1], l_scr[2]
            m = jnp.maximum(jnp.maximum(a, b), c)
            ea, eb, ec = jnp.exp(a - m), jnp.exp(b - m), jnp.exp(c - m)
            den = ea + eb + ec
            y_ref[...] = (ea / den) * o_scr[0] + (eb / den) * o_scr[1] + (ec / den) * o_scr[2]
            lse_ref[...] = m + jnp.log(den)

    blk = lambda f: pl.BlockSpec((s, HEAD), f)
    per_head = blk(lambda h, g: (0, h))
    return pl.pallas_call(
        body, name=name, grid=(B_HEADS, B_GROUPS),
        in_specs=[blk(lambda h, g: (0, B_HEADS * g + h)), per_head, blk(lambda h, g: (0, v_block + h))],
        out_specs=[per_head, per_head], out_shape=[jax.ShapeDtypeStruct((s, B_W), F32)] * 2,
        scratch_shapes=[pltpu.VMEM((B_GROUPS, s, HEAD), F32)] * 2,
        compiler_params=_params("parallel", "arbitrary"))(q, k, v_src)


def dilated_bwd(name, q, k, v_src, v_block, dy, y, lse):
    s = k.shape[0]
    scale = HEAD ** -0.5

    def body(q_ref, k_ref, v_ref, dy_ref, y_ref, lse_ref, dq_ref, dk_ref, dv_ref, delta):
        g_now = pl.program_id(1)

        @pl.when(g_now == 0)
        def _():
            dk_ref[...] = jnp.zeros_like(dk_ref)
            dv_ref[...] = jnp.zeros_like(dv_ref)
            delta[...] = jnp.broadcast_to(jnp.sum(dy_ref[...] * y_ref[...], axis=-1, keepdims=True), (s, HEAD))

        for g, d in enumerate(B_DILATIONS):
            @pl.when(g_now == g)
            def _(d=d):
                for r in range(d):
                    for q0, lo, hi in _band_windows(s // d):
                        mine = _class_rows(r, d, q0, HEAD)
                        keys = _class_rows(r, d, lo, hi - lo)
                        qv, kv = q_ref[mine, :].astype(BF16), k_ref[keys, :].astype(BF16)
                        dyv = dy_ref[mine, :].astype(BF16)
                        sc = _dot(qv, kv, NT) * scale
                        pr = jnp.where(_band_mask(q0, lo, hi), jnp.exp(sc - lse_ref[mine, :][:, 0:1]), 0.0)
                        dp = _dot(dyv, v_ref[keys, :].astype(BF16), NT)
                        ds = (pr * (dp - delta[mine, :][:, 0:1]) * scale).astype(BF16)
                        dv_ref[keys, :] = dv_ref[keys, :] + _dot(pr.astype(BF16), dyv, TN)
                        dq_ref[mine, :] = _dot(ds, kv)
                        dk_ref[keys, :] = dk_ref[keys, :] + _dot(ds, qv, TN)

    blk = lambda f: pl.BlockSpec((s, HEAD), f)
    per_head = blk(lambda h, g: (0, h))
    by_group = blk(lambda h, g: (0, B_HEADS * g + h))
    return pl.pallas_call(
        body, name=name, grid=(B_HEADS, B_GROUPS),
        in_specs=[by_group, per_head, blk(lambda h, g: (0, v_block + h)), per_head, per_head, per_head],
        out_specs=[by_group, per_head, per_head],
        out_shape=[jax.ShapeDtypeStruct((s, B_GROUPS * B_W), F32)] + [jax.ShapeDtypeStruct((s, B_W), F32)] * 2,
        scratch_shapes=[pltpu.VMEM((s, HEAD), F32)],
        compiler_params=_params("parallel", "arbitrary"))(q, k, v_src, dy, y, lse)


FFN_TN = 256


def _edges(shape):
    row = lax.broadcasted_iota(jnp.int32, shape, 0)
    return row == 0, row == shape[0] - 1


def _shifted(h, edges):
    first, last = edges
    s = h.shape[0]
    return jnp.where(first, 0.0, pltpu.roll(h, 1, 0)), jnp.where(last, 0.0, pltpu.roll(h, s - 1, 0))


def _conv(h, cw, edges):
    prev, nxt = _shifted(h, edges)
    return prev * cw[0:1, :] + h * cw[1:2, :] + nxt * cw[2:3, :] + cw[3:4, :], prev, nxt


def _sigmoid(x):
    return 1.0 / (1.0 + jnp.exp(-x))


def ffn_up(name, xn, w_up, cwb):
    s, dm = xn.shape
    quarter = w_up.shape[3]
    fh = 2 * quarter
    tn = FFN_TN
    per = quarter // tn

    def body(x_ref, w_ref, cw_ref, h_ref, act_ref):
        xv = x_ref[...]
        hg = _dot(xv, w_ref[0])
        hu = _dot(xv, w_ref[1])
        h_ref[0] = hg
        h_ref[1] = hu
        edges = _edges(hg.shape)
        gc, _, _ = _conv(hg, cw_ref[0], edges)
        uc, _, _ = _conv(hu, cw_ref[1], edges)
        act_ref[...] = (gc * _sigmoid(gc) * uc).astype(BF16)

    return pl.pallas_call(
        body, name=name, grid=(fh // tn,),
        in_specs=[pl.BlockSpec((s, dm), lambda t: (0, 0), pipeline_mode=pl.Buffered(1)),
                  pl.BlockSpec((2, None, dm, tn), lambda t: (0, t // per, 0, t % per)),
                  pl.BlockSpec((2, 8, tn), lambda t: (0, 0, t))],
        out_specs=[pl.BlockSpec((2, s, tn), lambda t: (0, 0, t)), pl.BlockSpec((s, tn), lambda t: (0, t))],
        out_shape=[jax.ShapeDtypeStruct((2, s, fh), F32), jax.ShapeDtypeStruct((s, fh), BF16)],
        compiler_params=_params("parallel"))(xn, w_up, cwb)


def ffn_gate_bwd(name, h, dact, cwb):
    _, s, fh = h.shape
    tn = FFN_TN

    def body(h_ref, da_ref, cw_ref, dh_ref, dcw_ref):
        edges = _edges((s, tn))
        gc, gp, gn = _conv(h_ref[0], cw_ref[0], edges)
        uc, up, un = _conv(h_ref[1], cw_ref[1], edges)
        sg = _sigmoid(gc)
        da = da_ref[...]
        dgc = da * uc * (sg * (1.0 + gc * (1.0 - sg)))
        duc = da * (gc * sg)
        for idx, (hv, prev, nxt, dc) in enumerate(((h_ref[0], gp, gn, dgc), (h_ref[1], up, un, duc))):
            cw = cw_ref[idx]
            from_prev, from_next = _shifted(dc, edges)
            dh_ref[idx] = (from_next * cw[0:1, :] + dc * cw[1:2, :] + from_prev * cw[2:3, :]).astype(BF16)
            dcw_ref[idx, 0:1, :] = jnp.sum(prev * dc, axis=0, keepdims=True)
            dcw_ref[idx, 1:2, :] = jnp.sum(hv * dc, axis=0, keepdims=True)
            dcw_ref[idx, 2:3, :] = jnp.sum(nxt * dc, axis=0, keepdims=True)
            dcw_ref[idx, 3:4, :] = jnp.sum(dc, axis=0, keepdims=True)
            dcw_ref[idx, 4:8, :] = jnp.zeros((4, tn), F32)

    return pl.pallas_call(
        body, name=name, grid=(fh // tn,),
        in_specs=[pl.BlockSpec((2, s, tn), lambda t: (0, 0, t)), pl.BlockSpec((s, tn), lambda t: (0, t)),
                  pl.BlockSpec((2, 8, tn), lambda t: (0, 0, t))],
        out_specs=[pl.BlockSpec((2, s, tn), lambda t: (0, 0, t)), pl.BlockSpec((2, 8, tn), lambda t: (0, 0, t))],
        out_shape=[jax.ShapeDtypeStruct((2, s, fh), BF16), jax.ShapeDtypeStruct((2, 8, fh), F32)],
        compiler_params=_params("parallel"))(h, dact, cwb)


def ffn_dx(name, dh, w_up):
    _, s, fh = dh.shape
    dm, quarter = w_up.shape[2], w_up.shape[3]
    tk = _tile(quarter, 2816)
    per = quarter // tk
    tm, tn = _tile(s, 1024), _tile(dm, 1024)
    grid = (s // tm, dm // tn, 4 * per)
    a_spec = pl.BlockSpec((None, tm, tk), lambda i, j, k: (k // (2 * per), i, k % (2 * per)))
    b_spec = pl.BlockSpec((None, None, tn, tk), lambda i, j, k: (k // (2 * per), (k // per) % 2, j, k % per))
    o_spec = pl.BlockSpec((tm, tn), lambda i, j, k: (i, j))
    return _matmul_call(name, NT, grid, a_spec, b_spec, o_spec, jax.ShapeDtypeStruct((s, dm), F32), (tm, tn))(dh, w_up)


def ffn_dw_up(name, xn, dh):
    _, s, fh = dh.shape
    dm = xn.shape[1]
    quarter = fh // 2
    tn = _tile(quarter, 1536)
    per = quarter // tn
    tm, tk = _tile(dm, 1024), _tile(s, 2048)
    grid = (dm // tm, 4 * per, s // tk)
    a_spec = pl.BlockSpec((tk, tm), lambda i, j, k: (k, i))
    b_spec = pl.BlockSpec((None, tk, tn), lambda i, j, k: (j // (2 * per), k, j % (2 * per)))
    o_spec = pl.BlockSpec((None, None, tm, tn), lambda i, j, k: (j // (2 * per), (j // per) % 2, i, j % per))
    return _matmul_call(name, TN, grid, a_spec, b_spec, o_spec,
                        jax.ShapeDtypeStruct((2, 2, dm, quarter), F32), (tm, tn))(xn, dh)


def loss_head(name, y, target):
    s, dm = y.shape
    ts = _rows(s, dm)

    def body(y_ref, t_ref, loss_ref, dy_ref, acc):
        i = pl.program_id(0)

        @pl.when(i == 0)
        def _():
            acc[...] = jnp.zeros_like(acc)

        err = y_ref[...] - t_ref[...]
        dy_ref[...] = err * (1.0 / dm)
        acc[...] += jnp.sum(err * err, axis=0, keepdims=True)

        @pl.when(i == s // ts - 1)
        def _():
            loss_ref[...] = jnp.broadcast_to(jnp.sum(acc[...], axis=-1, keepdims=True) * (0.5 / dm), (1, LANES))

    blk = pl.BlockSpec((ts, dm), lambda i: (i, 0))
    loss, dy = pl.pallas_call(
        body, name=name, grid=(s // ts,), in_specs=[blk, blk],
        out_specs=[pl.BlockSpec((1, LANES), lambda i: (0, 0)), blk],
        out_shape=[jax.ShapeDtypeStruct((1, LANES), F32), jax.ShapeDtypeStruct((s, dm), F32)],
        scratch_shapes=[pltpu.VMEM((1, dm), F32)], compiler_params=_params("arbitrary"))(y, target)
    return loss[0, 0], dy


def attn_fwd_c(name, qc, kvc, krc, scale):
    s = qc.shape[0]
    tq = ATTN_FWD_TQ

    def body(qn_ref, qr_ref, kn_ref, kr_ref, v_ref, o_ref, lse_ref):
        for r0 in range(0, tq, ATTN_TQ):
            rows = slice(r0, r0 + ATTN_TQ)
            sc = (_dot(qn_ref[rows, :], kn_ref[...], NT) + _dot(qr_ref[rows, :], kr_ref[...], NT)) * scale
            m = jnp.max(sc, axis=-1, keepdims=True)
            e = jnp.exp(sc - m)
            l = jnp.sum(e, axis=-1, keepdims=True)
            o_ref[rows, :] = _dot(e.astype(BF16), v_ref[...]) / l
            lse_ref[rows, :] = jnp.broadcast_to(m + jnp.log(l), (ATTN_TQ, HEAD))

    qb = lambda off: pl.BlockSpec((tq, HEAD), lambda h, i: (i, off + h))
    kb = lambda f: pl.BlockSpec((s, HEAD), f)
    out_blk = pl.BlockSpec((tq, HEAD), lambda h, i: (i, h))
    return pl.pallas_call(
        body, name=name, grid=(C_HEADS, s // tq),
        in_specs=[qb(0), qb(C_HEADS), kb(lambda h, i: (0, 2 * h)), kb(lambda h, i: (0, 0)), kb(lambda h, i: (0, 2 * h + 1))],
        out_specs=[out_blk, out_blk], out_shape=[jax.ShapeDtypeStruct((s, C_W), F32)] * 2,
        compiler_params=_params("parallel", "parallel"))(qc, qc, kvc, krc, kvc)


def attn_bwd_c(name, qc, kvc, krc, lse, o, do, scale):
    s = qc.shape[0]
    tq = ATTN_FWD_TQ

    def body(qn_ref, qr_ref, kn_ref, kr_ref, v_ref, lse_ref, o_ref, do_ref, dqn_ref, dqr_ref, dkv_ref, dkr_ref):
        h, i = pl.program_id(0), pl.program_id(1)

        @pl.when(i == 0)
        def _():
            dkv_ref[...] = jnp.zeros_like(dkv_ref)

        @pl.when((i == 0) & (h == 0))
        def _():
            dkr_ref[...] = jnp.zeros_like(dkr_ref)

        for r0 in range(0, tq, ATTN_TQ):
            rows = slice(r0, r0 + ATTN_TQ)
            sc = (_dot(qn_ref[rows, :], kn_ref[...], NT) + _dot(qr_ref[rows, :], kr_ref[...], NT)) * scale
            pr = jnp.exp(sc - lse_ref[rows, 0:1])
            dov = do_ref[rows, :]
            delta = jnp.sum(dov.astype(F32) * o_ref[rows, :], axis=-1, keepdims=True)
            dp = _dot(dov, v_ref[...], NT)
            ds = (pr * (dp - delta) * scale).astype(BF16)
            dkv_ref[:, HEAD:] += _dot(pr.astype(BF16), dov, TN)
            dqn_ref[rows, :] = _dot(ds, kn_ref[...])
            dqr_ref[rows, :] = _dot(ds, kr_ref[...])
            dkv_ref[:, :HEAD] += _dot(ds, qn_ref[rows, :], TN)
            dkr_ref[...] += _dot(ds, qr_ref[rows, :], TN)

    qb = lambda off: pl.BlockSpec((tq, HEAD), lambda h, i: (i, off + h))
    kb = lambda f: pl.BlockSpec((s, HEAD), f)
    hq = pl.BlockSpec((tq, HEAD), lambda h, i: (i, h))
    kn_map, v_map, kr_map = (lambda h, i: (0, 2 * h)), (lambda h, i: (0, 2 * h + 1)), (lambda h, i: (0, 0))
    return pl.pallas_call(
        body, name=name, grid=(C_HEADS, s // tq),
        in_specs=[qb(0), qb(C_HEADS), kb(kn_map), kb(kr_map), kb(v_map), hq, hq, hq],
        out_specs=[hq, hq, pl.BlockSpec((s, 2 * HEAD), lambda h, i: (0, h)), kb(kr_map)],
        out_shape=[jax.ShapeDtypeStruct((s, C_W), F32)] * 2
        + [jax.ShapeDtypeStruct((s, 2 * C_W), F32), jax.ShapeDtypeStruct((s, HEAD), F32)],
        compiler_params=_params("arbitrary", "arbitrary"))(qc, qc, kvc, krc, kvc, lse, o, do)


def _layer_fwd(x, p, tabs):
    sv = _mixers_ab_fwd(x, p, tabs)
    x1 = _mixer_c_out_fwd(x, sv, p, tabs)
    return _ffn_fwd(x1, p, sv), sv


def _mixers_ab_fwd(x, p, tabs):
    sv = {"x": x}
    hn = rmsnorm_fwd("attn_norm", x, p["attn_norm"], x.shape[1], BF16)
    proj = matmul("in_proj", hn, p["w_in_t"], "nt", F32, tm=256)
    qa, ka, av, qb, kb, cqn, ckvn = mixer_prep("mixer_prep", proj, p, tabs)
    sv.update(hn=hn, proj=proj, qa=qa, ka=ka, av=av, qb=qb, kb=kb, cqn=cqn, ckvn=ckvn)

    ya, lse_a = attn_fwd("a_attn", [(qa, 0, ka, A_HEADS // A_KV)], av, A_HEADS // A_KV, A_HEADS, HEAD ** -0.5)
    yb, lse_b = dilated_fwd("b_dilated", qb, kb, proj, O_BV // HEAD)
    sv.update(ya=ya, lse_a=lse_a, yb=yb, lse_b=lse_b)
    return sv


def _mixer_c_out_fwd(x, sv, p, tabs):
    ya, yb = sv["ya"], sv["yb"]
    qc_raw = matmul("c_uq", sv["cqn"], p["w_uq_t"], "nt", F32)
    qc = jnp.concatenate([qc_raw[:, :C_W].astype(BF16), rope("c_q_rope", qc_raw[:, C_W:], tabs["c"][0], BF16)], axis=1)
    kvc = matmul("c_ukv", sv["ckvn"], p["w_ukv"], "nn", BF16)
    ckr = jnp.pad(sv["proj"][:, O_CKR:], ((0, 0), (0, HEAD - C_ROPE)))
    krc = rope("c_k_rope", ckr, tabs["c"][0], BF16)
    c_scale = (HEAD + C_ROPE) ** -0.5
    yc, lse_c = attn_fwd_c("c_attn", qc, kvc, krc, c_scale)
    sv.update(qc=qc, kvc=kvc, krc=krc, yc=yc, lse_c=lse_c)

    y = out_norms("out_norms", ya, yb, yc, p["out_norm"])
    x1 = matmul("out_proj", y, p["w_out"], "nn", F32, add=x)
    sv.update(y=y, x1=x1)
    return x1


def _ffn_fwd(x1, p, sv):
    xn = rmsnorm_fwd("ffn_norm", x1, p["ffn_norm"], x1.shape[1], BF16)
    h, act = ffn_up("ffn_up", xn, p["w_up"], p["cwb"])
    x2 = matmul("ffn_down", act, p["w_down"], "nn", F32, add=x1)
    sv.update(xn=xn, h=h, act=act)
    return x2


def _layer_bwd(dx2, sv, p, tabs):
    dxn, g = _ffn_bwd(dx2, sv, p)
    dx1, dy, g_out = _out_proj_bwd(dxn, dx2, sv, p)
    dhn, g_mix = _mixers_bwd(dy, sv, p, tabs)
    dx, g_norm = _attn_norm_bwd(dhn, dx1, sv, p)
    return dx, {**g, **g_out, **g_mix, **g_norm}


def _ffn_bwd(dx2, sv, p, after=0.0):
    g = {}
    dx2b = (dx2 + after).astype(BF16)
    dact = matmul("ffn_down_dx", dx2b, p["w_down"], "nt", F32)
    g["w_down"] = matmul("ffn_down_dw", sv["act"], dx2b, "tn", F32)
    dh, g["cwb"] = ffn_gate_bwd("ffn_gate_bwd", sv["h"], dact, p["cwb"])
    dxn = ffn_dx("ffn_up_dx", dh, p["w_up"])
    g["w_up"] = ffn_dw_up("ffn_up_dw", sv["xn"], dh)
    return dxn, g


def _out_proj_bwd(dxn, dx2, sv, p, after=0.0):
    g = {}
    dm = dx2.shape[1]
    dx1, dx1b, g["ffn_norm"] = rmsnorm_bwd("ffn_norm_bwd", dxn, sv["x1"], p["ffn_norm"] + after, dm, (F32, BF16), add=dx2)
    dy = matmul("out_proj_dx", dx1b, p["w_out"], "nt", F32)
    g["w_out"] = matmul("out_proj_dw", sv["y"], dx1b, "tn", F32)
    return dx1, dy, g


def _attn_norm_bwd(dhn, dx1, sv, p, after=0.0):
    dx, d_gain = rmsnorm_bwd("attn_norm_bwd", dhn, sv["x"], p["attn_norm"] + after, dhn.shape[1], (F32,), add=dx1)
    return dx, {"attn_norm": d_gain}


def _mixers_bwd(dy, sv, p, tabs):
    part, g = _mixer_c_bwd(dy, sv, p, tabs)
    dhn, g_ab = _mixers_ab_bwd(part, sv, p, tabs)
    return dhn, {**g, **g_ab}


def _mixer_c_bwd(dy, sv, p, tabs, after=0.0):
    g = {}
    dya, dyb, dyc, g["out_norm"] = out_norms_bwd("out_norms_bwd", dy, sv["ya"], sv["yb"], sv["yc"], p["out_norm"] + after)

    c_scale = (HEAD + C_ROPE) ** -0.5
    dqn, dqr, dkv, dkr = attn_bwd_c("c_attn_bwd", sv["qc"], sv["kvc"], sv["krc"], sv["lse_c"], sv["yc"], dyc, c_scale)
    dqc = jnp.concatenate([dqn.astype(BF16), rope("c_q_rope_bwd", dqr, tabs["c"][1], BF16)], axis=1)
    dcqn = matmul("c_uq_dx", dqc, p["w_uq_t"], "nn", F32)
    g["w_uq"] = matmul("c_uq_dw", dqc, sv["cqn"], "tn", F32)
    dkvc = dkv.astype(BF16)
    dckvn = matmul("c_ukv_dx", dkvc, p["w_ukv"], "nt", F32)
    g["w_ukv"] = matmul("c_ukv_dw", sv["ckvn"], dkvc, "tn", F32)
    dckr = rope("c_k_rope_bwd", dkr, tabs["c"][1], BF16)
    return (dya, dyb, dcqn, dckvn, dckr), g


def _mixers_ab_bwd(part, sv, p, tabs, after=0.0):
    g = {}
    dya, dyb, dcqn, dckvn, dckr = part

    dqb, dkb, dvb = dilated_bwd("b_dilated_bwd", sv["qb"], sv["kb"], sv["proj"], O_BV // HEAD, dyb + after, sv["yb"],
                                sv["lse_b"])

    kg = A_HEADS // A_KV
    (dqa,), (dka,), dva = attn_bwd("a_attn_bwd", [(sv["qa"], 0, sv["ka"], kg)], sv["av"], kg, sv["lse_a"], sv["ya"],
                                   dya, A_HEADS, HEAD ** -0.5)

    dproj, g["a_q_norm"], g["a_k_norm"], g["c_q_norm"], g["c_kv_norm"] = mixer_prep_bwd(
        "mixer_prep_bwd", sv["proj"], p, tabs, dqa, dka, dva, dqb, dkb, dvb, dcqn, dckvn, dckr)
    dhn = matmul("in_proj_dx", dproj, p["w_in_t"], "nn", F32, tm=512)
    g["w_in"] = matmul("in_proj_dw", dproj, sv["hn"], "tn", F32, tn=512, tk=512)
    return dhn, g


def _local_step(x, target, layers, final_norm, tabs):
    saved = []
    for p in layers:
        x, sv = _layer_fwd(x, p, tabs)
        saved.append(sv)
    dm = x.shape[1]
    yf = rmsnorm_fwd("final_norm", x, final_norm, dm, F32)
    loss, dyf = loss_head("loss_head", yf, target)
    dx, d_final = rmsnorm_bwd("final_norm_bwd", dyf, x, final_norm, dm, (F32,))
    grads = [None] * len(layers)
    for l in reversed(range(len(layers))):
        dx, grads[l] = _layer_bwd(dx, saved[l], layers[l], tabs)
    return loss, dx, grads, d_final


HBM = pl.BlockSpec(memory_space=pl.ANY)


def _place():
    x, y, c = lax.axis_index("x"), lax.axis_index("y"), lax.axis_index("c")
    others = [(1 - x, y), (x, 1 - y), (1 - x, 1 - y)]
    return x, y, c, 2 * x + y, others


def _remote(src, dst, send_sem, recv_sem, device):
    return pltpu.make_async_remote_copy(src_ref=src, dst_ref=dst, send_sem=send_sem, recv_sem=recv_sem,
                                        device_id=device, device_id_type=MESH)


def gather_small(name, small):
    def body(in_ref, out_ref, send_sems, recv_sems, local_sem):
        x, y, c, chip, others = _place()
        mine = pltpu.make_async_copy(in_ref, out_ref.at[chip], local_sem)
        mine.start()
        copies = []
        for k, (ox, oy) in enumerate(others):
            cp = _remote(in_ref, out_ref.at[chip], send_sems.at[k], recv_sems.at[k], (ox, oy, c))
            cp.start()
            copies.append(cp)
        for k, (ox, oy) in enumerate(others):
            landed = out_ref.at[2 * ox + oy]
            _remote(landed, landed, send_sems.at[k], recv_sems.at[k], (ox, oy, c)).wait_recv()
        for cp in copies:
            cp.wait_send()
        mine.wait()

    return pl.pallas_call(
        body, name=name, in_specs=[HBM], out_specs=HBM, out_shape=jax.ShapeDtypeStruct((4,) + small.shape, small.dtype),
        scratch_shapes=[pltpu.SemaphoreType.DMA((3,)), pltpu.SemaphoreType.DMA((3,)), pltpu.SemaphoreType.DMA],
        compiler_params=pltpu.CompilerParams(has_side_effects=True))(small)


IN_HBM = pl.BlockSpec(memory_space=pltpu.HBM)
SEMS = pl.BlockSpec(memory_space=pltpu.SEMAPHORE)
DATAFLOW = pltpu.SideEffectType.DATAFLOW_SIDE_EFFECTING


BF16_ROWS_PER_TILE = 16


def _halves_by_rows(shape):
    return (shape[-2] // 2) % BF16_ROWS_PER_TILE == 0


def _half(shape, core):
    if _halves_by_rows(shape):
        size = shape[-2] // 2
        return (pl.ds(core * size, size), slice(None))
    size = shape[-1] // 2
    return (slice(None), pl.ds(core * size, size))


def _half_shape(shape):
    r, c = shape[-2:]
    return (r // 2, c) if _halves_by_rows(shape) else (r, c // 2)


def _gather_plan(bufs):
    x, y, c, chip, others = _place()
    plan = []
    for t, ref in enumerate(bufs):
        mine = _half(ref.shape, c)
        for k, (ox, oy) in enumerate(others):
            plan.append((3 * t + k, (ox, oy, c), ref.at[(chip,) + mine], ref.at[(chip,) + mine],
                         ref.at[(2 * ox + oy,) + mine]))
    return plan


def _exchange_plan(bufs):
    x, y, c, chip, others = _place()
    n = len(bufs) // 2
    plan = []
    for t in range(n):
        for k, (ox, oy) in enumerate(others):
            plan.append((3 * t + k, (ox, oy, c), bufs[t].at[2 * ox + oy], bufs[n + t].at[chip], bufs[n + t].at[2 * ox + oy]))
    return plan


def _sibling_plan(bufs):
    x, y, c, chip, others = _place()
    n = len(bufs) // 2
    plan = []
    for t in range(n):
        plan.append((t, (x, y, 1 - c), bufs[t].at[(slice(None),) + _half(bufs[t].shape, 1 - c)], bufs[n + t], bufs[n + t]))
    return plan


PLAN_COPIES = {_gather_plan: lambda n: 3 * n, _exchange_plan: lambda n: 3 * (n // 2), _sibling_plan: lambda n: n // 2}


def split_start(name, bufs, plan_of, after):
    n = len(bufs)

    def body(*refs):
        ins = refs[:n]
        send_sems, recv_sems = refs[n + 1], refs[n + 2]
        token = refs[-1]
        for idx, peer, src, dst, _ in plan_of(ins):
            _remote(src, dst, send_sems.at[idx], recv_sems.at[idx], peer).start()
        token[...] = jnp.zeros_like(token)

    copies = PLAN_COPIES[plan_of](n)
    res = pl.pallas_call(
        body, name=name, in_specs=[IN_HBM] * n + [HBM],
        out_specs=(SEMS, SEMS) + (IN_HBM,) * n + (pl.BlockSpec(memory_space=pltpu.VMEM),),
        out_shape=(pltpu.SemaphoreType.DMA((copies,)), pltpu.SemaphoreType.DMA((copies,)))
        + tuple(pltpu.HBM(b.shape, b.dtype) for b in bufs) + (jax.ShapeDtypeStruct((8, LANES), F32),),
        input_output_aliases={t: 2 + t for t in range(n)},
        compiler_params=pltpu.CompilerParams(has_side_effects=DATAFLOW))(
            *[pltpu.with_memory_space_constraint(b, pltpu.HBM) for b in bufs], after)
    return res[0], res[1], list(res[2:2 + n]), res[-1]


def split_wait(name, send_sems, recv_sems, flying, plan_of, after):
    n = len(flying)

    def body(*refs):
        ins = refs[:n]
        send_ref, recv_ref = refs[n], refs[n + 1]
        for idx, peer, src, dst, landing in plan_of(ins):
            _remote(src, dst, send_ref.at[idx], recv_ref.at[idx], peer).wait_send()
            _remote(landing, landing, send_ref.at[idx], recv_ref.at[idx], peer).wait_recv()

    return list(pl.pallas_call(
        body, name=name, in_specs=[IN_HBM] * n + [SEMS, SEMS] + [HBM] * len(after), out_specs=(IN_HBM,) * n,
        out_shape=tuple(pltpu.HBM(b.shape, b.dtype) for b in flying),
        input_output_aliases={t: t for t in range(n)},
        compiler_params=pltpu.CompilerParams(has_side_effects=DATAFLOW))(*flying, send_sems, recv_sems, *after))


def forward_halves(name, slots):
    n = len(slots)

    def body(*refs):
        ins, outs = refs[:n], refs[n:2 * n]
        send_sems, recv_sems = refs[2 * n:]
        x, y, c, chip, others = _place()
        sibling = (x, y, 1 - c)
        copies = []
        for t in range(n):
            mine = _half(slots[t].shape, c)
            for k, (ox, oy) in enumerate(others):
                cp = _remote(ins[t].at[(2 * ox + oy,) + mine], outs[t].at[(2 * ox + oy,) + mine], send_sems.at[t, k],
                             recv_sems.at[t, k], sibling)
                cp.start()
                copies.append(cp)
        for t in range(n):
            for k, (ox, oy) in enumerate(others):
                theirs = outs[t].at[(2 * ox + oy,) + _half(slots[t].shape, 1 - c)]
                _remote(theirs, theirs, send_sems.at[t, k], recv_sems.at[t, k], sibling).wait_recv()
        for cp in copies:
            cp.wait_send()

    return list(pl.pallas_call(
        body, name=name, in_specs=[HBM] * n, out_specs=[HBM] * n,
        out_shape=[jax.ShapeDtypeStruct(s.shape, s.dtype) for s in slots],
        input_output_aliases={t: t for t in range(n)},
        scratch_shapes=[pltpu.SemaphoreType.DMA((n, 3))] * 2,
        compiler_params=pltpu.CompilerParams(has_side_effects=True))(*slots))


def _half_block(shape, tr):
    r, c = shape[-2:]
    if _halves_by_rows(shape):
        per = r // 2 // tr
        return tr, c, lambda i, core: (core * per + i, 0)
    return tr, c // 2, lambda i, core: (i, core)


def add_half(name, place, grad, theirs):
    hr, hc = _half_shape(grad.shape)
    tr = _rows(hr, hc)
    br, bc, at = _half_block(grad.shape, tr)

    def body(place_ref, g_ref, t_ref, o_ref):
        o_ref[...] = (g_ref[...] + t_ref[...]).astype(o_ref.dtype)

    whole = pl.BlockSpec((None, br, bc), lambda j, i, pr: (j, i, 0))
    return pl.pallas_call(
        body, name=name,
        grid_spec=pltpu.PrefetchScalarGridSpec(
            num_scalar_prefetch=1, grid=(4, hr // tr),
            in_specs=[pl.BlockSpec((None, br, bc), lambda j, i, pr: (j,) + at(i, pr[0])), whole],
            out_specs=whole),
        out_shape=jax.ShapeDtypeStruct((4, hr, hc), BF16),
        compiler_params=_params("parallel", "parallel"))(place, grad, theirs)


def sum_chips(name, place, own, parts, shard_shape):
    _, hr, hc = parts.shape
    tr = _rows(hr, hc)
    br, bc, at = _half_block(shard_shape, tr)

    def body(place_ref, own_ref, p1, p2, p3, o_ref):
        o_ref[...] = own_ref[...].astype(F32) + p1[...].astype(F32) + p2[...].astype(F32) + p3[...].astype(F32)

    def slot(k):
        return pl.BlockSpec((None, br, bc), lambda i, pr: (lax.rem(pr[1] + k, 4), i, 0))

    return pl.pallas_call(
        body, name=name,
        grid_spec=pltpu.PrefetchScalarGridSpec(
            num_scalar_prefetch=1, grid=(hr // tr,), in_specs=[slot(0), slot(1), slot(2), slot(3)],
            out_specs=pl.BlockSpec((br, bc), lambda i, pr: at(i, pr[0]))),
        out_shape=jax.ShapeDtypeStruct(tuple(shard_shape), F32),
        compiler_params=_params("parallel"))(place, own, parts, parts, parts)


def join_halves(name, halves):
    n = len(halves)

    def body(*refs):
        ins, outs = refs[:n], refs[n:2 * n]
        send_sems, recv_sems = refs[2 * n:]
        x, y, c, _, _ = _place()
        copies = []
        for t in range(n):
            mine = _half(halves[t].shape, c)
            cp = _remote(ins[t].at[mine], outs[t].at[mine], send_sems.at[t], recv_sems.at[t], (x, y, 1 - c))
            cp.start()
            copies.append(cp)
        for t in range(n):
            theirs = outs[t].at[_half(halves[t].shape, 1 - c)]
            _remote(theirs, theirs, send_sems.at[t], recv_sems.at[t], (x, y, 1 - c)).wait_recv()
        for cp in copies:
            cp.wait_send()

    return pl.pallas_call(
        body, name=name, in_specs=[HBM] * n, out_specs=[HBM] * n,
        out_shape=[jax.ShapeDtypeStruct(h.shape, h.dtype) for h in halves],
        input_output_aliases={t: t for t in range(n)},
        scratch_shapes=[pltpu.SemaphoreType.DMA((n,))] * 2,
        compiler_params=pltpu.CompilerParams(has_side_effects=True))(*halves)


def allreduce_small(name, buf):
    rows = buf.shape[0]

    def body(in_ref, out_ref, slots, send_sems, recv_sems):
        x, y, c, _, _ = _place()
        me = 4 * x + 2 * y + c
        slots[me] = in_ref[...]
        peers = []
        for k in range(1, 8):
            px = 1 - x if k & 4 else x
            py = 1 - y if k & 2 else y
            pc = 1 - c if k & 1 else c
            peers.append((px, py, pc))
        copies = []
        for k, peer in enumerate(peers):
            cp = _remote(in_ref, slots.at[me], send_sems.at[k], recv_sems.at[k], peer)
            cp.start()
            copies.append(cp)
        for k, (px, py, pc) in enumerate(peers):
            slot = slots.at[4 * px + 2 * py + pc]
            _remote(slot, slot, send_sems.at[k], recv_sems.at[k], (px, py, pc)).wait_recv()
        for cp in copies:
            cp.wait_send()
        acc = slots[0]
        for d in range(1, 8):
            acc = acc + slots[d]
        out_ref[...] = acc

    vm = pl.BlockSpec(memory_space=pltpu.VMEM)
    return pl.pallas_call(
        body, name=name, in_specs=[vm], out_specs=vm, out_shape=jax.ShapeDtypeStruct(buf.shape, F32),
        scratch_shapes=[pltpu.VMEM((8, rows, LANES), F32), pltpu.SemaphoreType.DMA((7,)), pltpu.SemaphoreType.DMA((7,))],
        compiler_params=pltpu.CompilerParams(has_side_effects=True, vmem_limit_bytes=VMEM_LIMIT))(buf)


def cast_layer(name, place, w, layer):
    _, r, cols = w.shape
    tr = _rows(r, cols)

    def body(place_ref, w_ref, o_ref):
        o_ref[...] = w_ref[...].astype(BF16)

    return pl.pallas_call(
        body, name=name,
        grid_spec=pltpu.PrefetchScalarGridSpec(
            num_scalar_prefetch=1, grid=(r // tr,),
            in_specs=[pl.BlockSpec((None, tr, cols), lambda i, pr: (layer, i, 0))],
            out_specs=pl.BlockSpec((None, tr, cols), lambda i, pr: (pr[1], i, 0))),
        out_shape=jax.ShapeDtypeStruct((4, r, cols), BF16),
        compiler_params=_params("parallel"))(place, w)


def _adamw_math(g, w, m, v):
    m = ADAM_B1 * m + (1.0 - ADAM_B1) * g
    v = ADAM_B2 * v + (1.0 - ADAM_B2) * (g * g)
    m_hat = m / (1.0 - ADAM_B1 ** ADAM_STEP)
    v_hat = v / (1.0 - ADAM_B2 ** ADAM_STEP)
    delta = -ADAM_LR * (m_hat / (jnp.sqrt(v_hat) + ADAM_EPS) + ADAM_WD * w)
    return delta, m, v


def adamw_layer(name, layer, g, w, m, v, prev=None):
    nl, r, cols = w.shape
    tr = _rows(r, cols, 8)

    def body(*refs):
        g_ref, w_ref, m_ref, v_ref = refs[:4]
        og, od, om, ov = refs[-4:]
        gv = g_ref[...]
        delta, m2, v2 = _adamw_math(gv, w_ref[...], m_ref[...], v_ref[...])
        og[...] = gv
        od[...] = delta
        om[...] = m2
        ov[...] = v2

    lay = pl.BlockSpec((None, tr, cols), lambda i: (layer, i, 0))
    ins = [g, w, m, v] + (list(prev) if prev is not None else [])
    return pl.pallas_call(
        body, name=name, grid=(r // tr,),
        in_specs=[pl.BlockSpec((tr, cols), lambda i: (i, 0)), lay, lay, lay] + ([HBM] * 4 if prev is not None else []),
        out_specs=[lay] * 4, out_shape=[jax.ShapeDtypeStruct((nl, r, cols), F32)] * 4,
        input_output_aliases={4 + k: k for k in range(4)} if prev is not None else {},
        compiler_params=_params("parallel"))(*ins)


def adamw_packed(name, g, w, m, v):
    def body(g_ref, w_ref, m_ref, v_ref, od, om, ov):
        delta, m2, v2 = _adamw_math(g_ref[...], w_ref[...], m_ref[...], v_ref[...])
        od[...] = delta
        om[...] = m2
        ov[...] = v2

    vm = pl.BlockSpec(memory_space=pltpu.VMEM)
    return pl.pallas_call(
        body, name=name, in_specs=[vm] * 4, out_specs=[vm] * 3, out_shape=[jax.ShapeDtypeStruct(g.shape, F32)] * 3,
        compiler_params=pltpu.CompilerParams(vmem_limit_bytes=VMEM_LIMIT))(g, w, m, v)


def _pack(arrays):
    flat = jnp.concatenate([a.reshape(-1) for a in arrays])
    pad = (-flat.shape[0]) % (8 * LANES)
    return jnp.pad(flat, (0, pad)).reshape(-1, LANES)


def _unpack(buf, shapes):
    flat = buf.reshape(-1)
    out, off = [], 0
    for shp in shapes:
        size = int(np.prod(shp))
        out.append(flat[off:off + size].reshape(shp))
        off += size
    return out


MIXER_W = ("w_in", "w_uq", "w_ukv", "w_out")
FFN_W = ("w_up", "w_down")
BIG = MIXER_W + FFN_W
TRANSPOSED = ("w_in", "w_uq")
COLUMN_CUT = ("w_ukv", "w_up")
SMALL = ("attn_norm", "a_q_norm", "a_k_norm", "c_q_norm", "c_kv_norm", "out_norm", "ffn_norm", "conv_b", "final_norm")
WEIGHTS = ("attn_norm", "w_in", "a_q_norm", "a_k_norm", "c_q_norm", "c_kv_norm", "w_uq", "w_ukv", "out_norm", "w_out",
           "ffn_norm", "w_up", "conv_w", "conv_b", "w_down", "final_norm")
INPUTS = ("x",) + WEIGHTS + ("loss_target",) + tuple("m_" + n for n in WEIGHTS) + tuple("v_" + n for n in WEIGHTS)


def _columns(g):
    return jnp.transpose(g, (1, 0, 2)).reshape(g.shape[1], 4 * g.shape[2])


def _uncolumns(w):
    r, c4 = w.shape
    return jnp.transpose(w.reshape(r, 4, c4 // 4), (1, 0, 2))


def _compute_layouts(full, conv_w, small, layer):
    p = {n: small[n][layer] for n in SMALL if n != "final_norm"}
    if "w_in" in full:
        p["w_in_t"] = full["w_in"].reshape(-1, full["w_in"].shape[2])
    if "w_uq" in full:
        uq_t = full["w_uq"].reshape(C_HEADS, HEAD + C_ROPE, C_RANK)
        p["w_uq_t"] = jnp.concatenate([uq_t[:, :HEAD].reshape(C_W, C_RANK),
                                       jnp.pad(uq_t[:, HEAD:], ((0, 0), (0, HEAD - C_ROPE), (0, 0))).reshape(C_W, C_RANK)], axis=0)
        p["w_ukv"] = _columns(full["w_ukv"])
        p["w_out"] = full["w_out"].reshape(-1, full["w_out"].shape[2])
    if "w_up" in full:
        up = full["w_up"]
        p["w_up"] = up.reshape(2, 2, up.shape[1], up.shape[2])
        p["w_down"] = full["w_down"].reshape(-1, full["w_down"].shape[2])
        fh = conv_w.shape[1] // 2
        taps = jnp.transpose(conv_w.reshape(3, 2, fh), (1, 0, 2))
        p["cwb"] = jnp.concatenate([taps, small["conv_b"][layer].reshape(2, 1, fh), jnp.zeros((2, 4, fh), F32)], axis=1)
    return p


def _shard_layout(name, g):
    if name == "w_in":
        return g.reshape(4, -1, g.shape[1])
    if name == "w_uq":
        uq_t = jnp.concatenate([g[:C_W].reshape(C_HEADS, HEAD, C_RANK),
                                g[C_W:].reshape(C_HEADS, HEAD, C_RANK)[:, :C_ROPE]], axis=1)
        return uq_t.reshape(4, -1, C_RANK)
    if name == "w_ukv":
        return _uncolumns(g)
    if name == "w_up":
        return g.reshape(4, g.shape[2], g.shape[3])
    return g.reshape(4, -1, g.shape[1])


def _conv_grads(dcwb):
    fh = dcwb.shape[2]
    return jnp.transpose(dcwb[:, 0:3, :], (1, 0, 2)).reshape(3, 2 * fh), dcwb[:, 3, :].reshape(2 * fh)


def _shard_layouts(g):
    return ({n: _shard_layout(n, g[n]) for n in BIG},) + _conv_grads(g["cwb"])


def kernel(x, attn_norm, w_in, a_q_norm, a_k_norm, c_q_norm, c_kv_norm, w_uq, w_ukv, out_norm, w_out, ffn_norm, w_up, conv_w, conv_b, w_down, final_norm, loss_target, m_attn_norm, m_w_in, m_a_q_norm, m_a_k_norm, m_c_q_norm, m_c_kv_norm, m_w_uq, m_w_ukv, m_out_norm, m_w_out, m_ffn_norm, m_w_up, m_conv_w, m_conv_b, m_w_down, m_final_norm, v_attn_norm, v_w_in, v_a_q_norm, v_a_k_norm, v_c_q_norm, v_c_kv_norm, v_w_uq, v_w_ukv, v_out_norm, v_w_out, v_ffn_norm, v_w_up, v_conv_w, v_conv_b, v_w_down, v_final_norm):
    a = dict(zip(INPUTS, (x, attn_norm, w_in, a_q_norm, a_k_norm, c_q_norm, c_kv_norm, w_uq, w_ukv, out_norm, w_out, ffn_norm, w_up, conv_w, conv_b, w_down, final_norm, loss_target, m_attn_norm, m_w_in, m_a_q_norm, m_a_k_norm, m_c_q_norm, m_c_kv_norm, m_w_uq, m_w_ukv, m_out_norm, m_w_out, m_ffn_norm, m_w_up, m_conv_w, m_conv_b, m_w_down, m_final_norm, v_attn_norm, v_w_in, v_a_q_norm, v_a_k_norm, v_c_q_norm, v_c_kv_norm, v_w_uq, v_w_ukv, v_out_norm, v_w_out, v_ffn_norm, v_w_up, v_conv_w, v_conv_b, v_w_down, v_final_norm)))
    nl = w_in.shape[0]
    seq = x.shape[1]
    for n in TRANSPOSED:
        for kind in ("", "m_", "v_"):
            a[kind + n] = jnp.swapaxes(a[kind + n], 1, 2)
    chip = 2 * lax.axis_index("x") + lax.axis_index("y")
    place = jnp.stack([lax.axis_index("c"), chip]).astype(jnp.int32)
    tabs = _rope_tables(seq)

    assert nl == 2
    quarter = conv_w.shape[2]
    taps = jnp.pad(conv_w, ((0, 0), (0, 8 - conv_w.shape[1]), (0, 0))).reshape(nl * 8, quarter)
    taps_full = gather_small("gather_taps", taps).reshape(4, nl, 8, quarter)
    conv_full = [jnp.transpose(taps_full[:, l, 0:3, :], (1, 0, 2)).reshape(3, 4 * quarter) for l in range(nl)]
    flights = {}
    here, after = place, taps_full
    for l in range(nl):
        for key, names in ((f"in{l}", ("w_in",)), (f"mix{l}", MIXER_W[1:]), (f"ffn{l}", FFN_W)):
            slots = [cast_layer(f"cast{l}_{n}", here, a[n], l) for n in names]
            send_sems, recv_sems, flying, token = split_start(f"gather_{key}_start", slots, _gather_plan, after)
            flights[key] = (names, send_sems, recv_sems, flying)
            here = after = place + token[0, 0].astype(jnp.int32)
    started = token[0, 0]

    def land(key, after):
        names, send_sems, recv_sems, flying = flights[key]
        landed = split_wait(f"gather_{key}_wait", send_sems, recv_sems, flying, _gather_plan, after)
        return dict(zip(names, forward_halves(f"gather_{key}_forward", landed)))

    h = x[0]
    layers, saved = [], []
    for l in range(nl):
        p = _compute_layouts(land(f"in{l}", [taps_full if l == 0 else h]), conv_full[l], a, l)
        if l == 0:
            p["attn_norm"] = p["attn_norm"] + started
        sv = _mixers_ab_fwd(h, p, tabs)
        p.update(_compute_layouts(land(f"mix{l}", [sv["yb"]]), conv_full[l], a, l))
        x1 = _mixer_c_out_fwd(h, sv, p, tabs)
        p.update(_compute_layouts(land(f"ffn{l}", [x1]), conv_full[l], a, l))
        h = _ffn_fwd(x1, p, sv)
        layers.append(p)
        saved.append(sv)
    dm = h.shape[1]
    yf = rmsnorm_fwd("final_norm", h, final_norm, dm, F32)
    loss, dyf = loss_head("loss_head", yf, loss_target[0])
    loss = lax.psum(loss, ("x", "y", "c"))
    dx, d_final = rmsnorm_bwd("final_norm_bwd", dyf, h, final_norm, dm, (F32,))

    def swap_begin(tag, names, g):
        g_list = [_shard_layout(n, g[n]) for n in names]
        theirs = [lax.empty((4,) + _half_shape(s.shape), s.dtype) for s in g_list]
        send_sems, recv_sems, flying, token = split_start(f"{tag}_sibling_start", g_list + theirs, _sibling_plan, place)
        return (names, send_sems, recv_sems, flying), token[0, 0]

    def exchange_begin(tag, swap, after):
        names, send_sems, recv_sems, flying = swap
        landed = split_wait(f"{tag}_sibling_wait", send_sems, recv_sems, flying, _sibling_plan, after)
        g_list, theirs = landed[:len(names)], landed[len(names):]
        sums = [add_half(f"{tag}_add_{n}", place, gr, t) for n, gr, t in zip(names, g_list, theirs)]
        parts = [lax.empty(s.shape, s.dtype) for s in sums]
        send_sems, recv_sems, flying, token = split_start(f"{tag}_chips_start", sums + parts, _exchange_plan, place)
        return (names, [gr.shape[1:] for gr in g_list], send_sems, recv_sems, flying), token

    def exchange_end(tag, flight, after):
        names, shard_shapes, send_sems, recv_sems, flying = flight
        landed = split_wait(f"{tag}_chips_wait", send_sems, recv_sems, flying, _exchange_plan, after)
        sums, parts = landed[:len(names)], landed[len(names):]
        halves = [sum_chips(f"{tag}_sum_{n}", place, s, p, shp) for n, s, p, shp in zip(names, sums, parts, shard_shapes)]
        return dict(zip(names, join_halves(f"{tag}_join", halves)))

    grads, conv_grads, flights_f, flights_m = [None] * nl, [None] * nl, [None] * nl, [None] * nl
    after = 0.0
    for l in reversed(range(nl)):
        sv, p = saved[l], layers[l]
        dxn, g = _ffn_bwd(dx, sv, p, after)
        swap_f, after = swap_begin(f"reduce{l}f", FFN_W, g)
        dx1, dy, g_out = _out_proj_bwd(dxn, dx, sv, p, after)
        part, g_c = _mixer_c_bwd(dy, sv, p, tabs)
        flights_f[l], token = exchange_begin(f"reduce{l}f", swap_f, [part[2]])
        dhn, g_ab = _mixers_ab_bwd(part, sv, p, tabs, token[0, 0])
        g.update(g_out)
        g.update(g_c)
        g.update(g_ab)
        swap_m, after = swap_begin(f"reduce{l}m", MIXER_W, g)
        dx, g_norm = _attn_norm_bwd(dhn, dx1, sv, p, after)
        g.update(g_norm)
        flights_m[l], token = exchange_begin(f"reduce{l}m", swap_m, [dx])
        after = token[0, 0]
        grads[l], conv_grads[l] = g, _conv_grads(g["cwb"])
    grad_x = dx

    out = {}
    reduced1 = exchange_end("reduce1f", flights_f[1], [token])
    for n in FFN_W:
        out[n] = adamw_layer(f"adamw1_{n}", 1, reduced1[n], a[n], a["m_" + n], a["v_" + n])
    reduced1 = exchange_end("reduce1m", flights_m[1], [out[n][3] for n in FFN_W])
    for n in MIXER_W:
        out[n] = adamw_layer(f"adamw1_{n}", 1, reduced1[n], a[n], a["m_" + n], a["v_" + n])

    small_g = {n: jnp.stack([grads[l][n] for l in range(nl)]) for n in SMALL if n not in ("conv_b", "final_norm")}
    small_g["conv_b"] = jnp.stack([cg[1] for cg in conv_grads])
    small_g["final_norm"] = d_final
    conv_w_g = jnp.stack([cg[0] for cg in conv_grads])
    shapes = [a[n].shape for n in SMALL] + [conv_w_g.shape]
    summed = _unpack(allreduce_small("reduce_small", _pack([small_g[n] for n in SMALL] + [conv_w_g])), shapes)
    small_g = dict(zip(SMALL, summed[:-1]))
    small_g["conv_w"] = lax.dynamic_slice_in_dim(summed[-1], chip * quarter, quarter, axis=2)
    names = SMALL + ("conv_w",)
    shapes = [a[n].shape for n in names]
    d_small, m_small, v_small = adamw_packed(
        "adamw_small", _pack([small_g[n] for n in names]), _pack([a[n] for n in names]),
        _pack([a["m_" + n] for n in names]), _pack([a["v_" + n] for n in names]))

    reduced0 = exchange_end("reduce0f", flights_f[0], [out[n][3] for n in BIG] + [d_small])
    for n in FFN_W:
        out[n] = adamw_layer(f"adamw0_{n}", 0, reduced0[n], a[n], a["m_" + n], a["v_" + n], prev=out[n])
    reduced0 = exchange_end("reduce0m", flights_m[0], [out[n][3] for n in FFN_W])
    for n in MIXER_W:
        out[n] = adamw_layer(f"adamw0_{n}", 0, reduced0[n], a[n], a["m_" + n], a["v_" + n], prev=out[n])
    for n, d_, m_, v_ in zip(names, _unpack(d_small, shapes), _unpack(m_small, shapes), _unpack(v_small, shapes)):
        out[n] = (small_g[n], d_, m_, v_)
    for n in TRANSPOSED:
        out[n] = [jnp.swapaxes(o, 1, 2) for o in out[n]]

    return (loss, grad_x[None]) + tuple(out[n][k] for k in range(4) for n in WEIGHTS)
```

```python
import functools

import jax
import jax.numpy as jnp
import numpy as np
from jax import lax
from jax.experimental import pallas as pl
from jax.experimental.pallas import tpu as pltpu

F32 = jnp.float32
BF16 = jnp.bfloat16
MESH = pl.DeviceIdType.MESH

V7X_VMEM_BYTES = 64 * 1024 * 1024
VMEM_LIMIT = V7X_VMEM_BYTES - 6 * 1024 * 1024
LANES = 128
ELEMENTWISE_BLOCK_BYTES = 4 * 1024 * 1024

HEAD = 128
A_HEADS, A_KV = 6, 2
B_HEADS, B_GROUPS = 4, 3
B_DILATIONS = (1, 4, 16)
B_HALF = 64
C_HEADS, C_RANK, C_ROPE = 6, 512, 64
GRID_W = 64
A_THETA, B_THETA, C_THETA = 10000.0, 500000.0, 10000.0
B_ROPE_DIM = 32
EPS = 1e-6
A_W, B_W, C_W = A_HEADS * HEAD, B_HEADS * HEAD, C_HEADS * HEAD
O_AQ, O_AK, O_AV = 0, 768, 1024
O_BQ, O_BK, O_BV = 1280, 2816, 3328
O_CQ, O_CKV, O_CKR = 3840, 4352, 4864
IN_W = 4928
PROJ_W = 5120

ADAM_LR, ADAM_B1, ADAM_B2, ADAM_EPS, ADAM_WD, ADAM_STEP = 0.001, 0.9, 0.999, 1e-08, 0.01, 10

NN = (((1,), (0,)), ((), ()))
NT = (((1,), (1,)), ((), ()))
TN = (((0,), (0,)), ((), ()))


def _dot(a, b, dims=NN):
    return lax.dot_general(a, b, dims, preferred_element_type=F32)


def _params(*sem):
    return pltpu.CompilerParams(dimension_semantics=sem if sem else None, vmem_limit_bytes=VMEM_LIMIT)


def _tile(n, target, unit=LANES):
    if n <= target:
        return n
    best = 0
    for t in range(unit, target + 1, unit):
        if n % t == 0:
            best = t
    return best if best else n


def _rows(r, width, itemsize=4):
    return _tile(r, max(8, ELEMENTWISE_BLOCK_BYTES // (width * itemsize)), 8)


def _matmul_call(name, dims, grid, a_spec, b_spec, o_spec, out_shape, acc_shape, add_spec=None):
    nk = grid[2]

    def body(*refs):
        a_ref, b_ref = refs[:2]
        add_ref = None if add_spec is None else refs[2]
        o_ref = refs[2 if add_spec is None else 3]

        def finish(r):
            if add_ref is not None:
                r = r + add_ref[...]
            o_ref[...] = r.astype(o_ref.dtype)

        if nk == 1:
            finish(_dot(a_ref[...], b_ref[...], dims))
            return
        acc = refs[-1]
        k = pl.program_id(2)

        @pl.when(k == 0)
        def _():
            acc[...] = _dot(a_ref[...], b_ref[...], dims)

        if nk > 2:
            @pl.when((k > 0) & (k < nk - 1))
            def _():
                acc[...] += _dot(a_ref[...], b_ref[...], dims)

        @pl.when(k == nk - 1)
        def _():
            finish(acc[...] + _dot(a_ref[...], b_ref[...], dims))

    in_specs = [a_spec, b_spec] + ([] if add_spec is None else [add_spec])
    return pl.pallas_call(
        body, name=name, grid=grid, in_specs=in_specs, out_specs=o_spec, out_shape=out_shape,
        scratch_shapes=[pltpu.VMEM(acc_shape, F32)] if nk > 1 else [],
        compiler_params=_params("parallel", "parallel", "arbitrary"))


def matmul(name, a, b, mode, out_dtype, add=None, tm=1024, tn=1024, tk=2816):
    if mode == "nn":
        (m, k), (k2, n) = a.shape, b.shape
    elif mode == "nt":
        (m, k), (n, k2) = a.shape, b.shape
    else:
        (k, m), (k2, n) = a.shape, b.shape
    assert k == k2, (name, a.shape, b.shape)
    tm, tn, tk = _tile(m, tm), _tile(n, tn), _tile(k, tk)
    grid = (m // tm, n // tn, k // tk)
    if mode == "tn":
        a_spec = pl.BlockSpec((tk, tm), lambda i, j, kk: (kk, i))
    else:
        a_spec = pl.BlockSpec((tm, tk), lambda i, j, kk: (i, kk))
    whole_b = {"pipeline_mode": pl.Buffered(1)} if grid[1] == 1 and grid[2] == 1 and grid[0] > 1 else {}
    if mode == "nt":
        b_spec = pl.BlockSpec((tn, tk), lambda i, j, kk: (j, kk), **whole_b)
    else:
        b_spec = pl.BlockSpec((tk, tn), lambda i, j, kk: (kk, j), **whole_b)
    o_spec = pl.BlockSpec((tm, tn), lambda i, j, kk: (i, j))
    dims = {"nn": NN, "nt": NT, "tn": TN}[mode]
    call = _matmul_call(name, dims, grid, a_spec, b_spec, o_spec, jax.ShapeDtypeStruct((m, n), out_dtype),
                        (tm, tn), None if add is None else o_spec)
    return call(a, b) if add is None else call(a, b, add)


def rmsnorm_fwd(name, x, g, width, out_dtype):
    r, cols = x.shape
    nb = cols // width
    tr = _rows(r, width)

    def body(x_ref, g_ref, o_ref):
        xv = x_ref[...]
        rs = lax.rsqrt(jnp.mean(xv * xv, axis=-1, keepdims=True) + EPS)
        o_ref[...] = (xv * rs * g_ref[...]).astype(o_ref.dtype)

    blk = pl.BlockSpec((tr, width), lambda i, j: (i, j))
    return pl.pallas_call(
        body, name=name, grid=(r // tr, nb),
        in_specs=[blk, pl.BlockSpec((1, width), lambda i, j: (0, 0))], out_specs=blk,
        out_shape=jax.ShapeDtypeStruct((r, cols), out_dtype),
        compiler_params=_params("parallel", "parallel"))(x, g.reshape(1, width))


def rmsnorm_bwd(name, dy, x, g, width, out_dtypes=(F32,), add=None):
    r, cols = x.shape
    nb = cols // width
    tr = _rows(r, width)
    n_out = len(out_dtypes)

    def body(*refs):
        dy_ref, x_ref, g_ref = refs[:3]
        add_ref = refs[3] if add is not None else None
        outs = refs[-(n_out + 1):-1]
        dg_ref = refs[-1]
        xv = x_ref[...]
        dyv = dy_ref[...].astype(F32)
        rs = lax.rsqrt(jnp.mean(xv * xv, axis=-1, keepdims=True) + EPS)
        xh = xv * rs
        dyg = dyv * g_ref[...]
        dx = rs * (dyg - xh * jnp.mean(dyg * xh, axis=-1, keepdims=True))
        if add_ref is not None:
            dx = dx + add_ref[...]
        for o in outs:
            o[...] = dx.astype(o.dtype)

        @pl.when((pl.program_id(0) == 0) & (pl.program_id(1) == 0))
        def _():
            dg_ref[...] = jnp.zeros_like(dg_ref)

        dg_ref[...] += jnp.sum(dyv * xh, axis=0, keepdims=True)

    blk = pl.BlockSpec((tr, width), lambda i, j: (i, j))
    vec = pl.BlockSpec((1, width), lambda i, j: (0, 0))
    ins = [dy, x, g.reshape(1, width)] + ([add] if add is not None else [])
    res = pl.pallas_call(
        body, name=name, grid=(r // tr, nb),
        in_specs=[blk, blk, vec] + ([blk] if add is not None else []),
        out_specs=[blk] * n_out + [vec],
        out_shape=[jax.ShapeDtypeStruct((r, cols), dt) for dt in out_dtypes] + [jax.ShapeDtypeStruct((1, width), F32)],
        compiler_params=_params("arbitrary", "arbitrary"))(*ins)
    return tuple(res[:n_out]) + (res[n_out].reshape(width),)


def _rope_angles(pos, dim, theta):
    inv = theta ** (-jnp.arange(0, dim, 2, dtype=F32) / dim)
    ang = pos.astype(F32)[:, None] * inv[None, :]
    return jnp.cos(ang), jnp.sin(ang)


def _rope_tables(s):
    rows = s // GRID_W
    row_pos = jnp.repeat(jnp.arange(rows), GRID_W)
    col_pos = jnp.tile(jnp.arange(GRID_W), rows)
    t_pos = jnp.arange(s)
    z = lambda n: jnp.zeros((s, n), F32)
    o = lambda n: jnp.ones((s, n), F32)
    cr, sr = _rope_angles(row_pos, HEAD // 2, A_THETA)
    cc, sc = _rope_angles(col_pos, HEAD // 2, A_THETA)
    tab_a = (jnp.concatenate([cr, cr, cc, cc], 1), jnp.concatenate([-sr, z(32), -sc, z(32)], 1),
             jnp.concatenate([z(32), sr, z(32), sc], 1), 32)
    cp, sp = _rope_angles(t_pos, B_ROPE_DIM, B_THETA)
    tab_b = (jnp.concatenate([cp, cp, o(96)], 1), jnp.concatenate([-sp, z(112)], 1),
             jnp.concatenate([z(16), sp, z(96)], 1), 16)
    cm, sm = _rope_angles(t_pos, C_ROPE, C_THETA)
    tab_c = (jnp.concatenate([cm, cm, o(64)], 1), jnp.concatenate([-sm, z(96)], 1),
             jnp.concatenate([z(32), sm, z(64)], 1), 32)

    def transposed(tab):
        c, s1, s2, h = tab
        return (c, jnp.roll(s2, -h, axis=1), jnp.roll(s1, h, axis=1), h)

    return {k: (t, transposed(t)) for k, t in (("a", tab_a), ("b", tab_b), ("c", tab_c))}


def rope(name, x, tab, out_dtype):
    c, s1, s2, h = tab
    s, cols = x.shape
    nh = cols // HEAD
    ts = _rows(s, HEAD)

    def body(x_ref, c_ref, s1_ref, s2_ref, o_ref):
        xv = x_ref[...].astype(F32)
        out = xv * c_ref[...] + pltpu.roll(xv, HEAD - h, 1) * s1_ref[...] + pltpu.roll(xv, h, 1) * s2_ref[...]
        o_ref[...] = out.astype(o_ref.dtype)

    blk = pl.BlockSpec((ts, HEAD), lambda i, j: (i, j))
    tb = pl.BlockSpec((ts, HEAD), lambda i, j: (i, 0))
    return pl.pallas_call(
        body, name=name, grid=(s // ts, nh), in_specs=[blk, tb, tb, tb], out_specs=blk,
        out_shape=jax.ShapeDtypeStruct((s, cols), out_dtype),
        compiler_params=_params("parallel", "parallel"))(x, c, s1, s2)


PREP_ROWS = 256


def _rope_of(x, tab, h):
    c, s1, s2 = tab
    return x * c + pltpu.roll(x, HEAD - h, 1) * s1 + pltpu.roll(x, h, 1) * s2


def _heads(start, n):
    return [(j, slice(start + HEAD * j, start + HEAD * (j + 1))) for j in range(n)]


def mixer_prep(name, proj, p, tabs):
    s = proj.shape[0]
    tr = PREP_ROWS
    (ca, s1a, s2a, ha), (cb, s1b, s2b, hb) = tabs["a"][0], tabs["b"][0]

    def body(proj_ref, gaq, gak, gcq, gckv, ca_r, s1a_r, s2a_r, cb_r, s1b_r, s2b_r,
             qa_ref, ka_ref, av_ref, qb_ref, kb_ref, cqn_ref, ckvn_ref):
        ta = (ca_r[...], s1a_r[...], s2a_r[...])
        tb = (cb_r[...], s1b_r[...], s2b_r[...])

        def normed(x, g):
            return x * lax.rsqrt(jnp.mean(x * x, axis=-1, keepdims=True) + EPS) * g

        for (src0, n, gain, out) in ((O_AQ, A_HEADS, gaq, qa_ref), (O_AK, A_KV, gak, ka_ref)):
            for j, cols in _heads(src0, n):
                out[:, HEAD * j:HEAD * (j + 1)] = _rope_of(normed(proj_ref[:, cols], gain[...]), ta, ha).astype(BF16)
        av_ref[...] = proj_ref[:, O_AV:O_BQ].astype(BF16)
        for (src0, n, out) in ((O_BQ, B_GROUPS * B_HEADS, qb_ref), (O_BK, B_HEADS, kb_ref)):
            for j, cols in _heads(src0, n):
                out[:, HEAD * j:HEAD * (j + 1)] = _rope_of(proj_ref[:, cols], tb, hb)
        cqn_ref[...] = normed(proj_ref[:, O_CQ:O_CKV], gcq[...]).astype(BF16)
        ckvn_ref[...] = normed(proj_ref[:, O_CKV:O_CKR], gckv[...]).astype(BF16)

    rows = lambda w: pl.BlockSpec((tr, w), lambda i: (i, 0))
    vec = lambda w: pl.BlockSpec((1, w), lambda i: (0, 0))
    widths = (A_W, A_KV * HEAD, A_KV * HEAD, B_GROUPS * B_W, B_W, C_RANK, C_RANK)
    dtypes = (BF16, BF16, BF16, F32, F32, BF16, BF16)
    return pl.pallas_call(
        body, name=name, grid=(s // tr,),
        in_specs=[rows(proj.shape[1]), vec(HEAD), vec(HEAD), vec(C_RANK), vec(C_RANK)] + [rows(HEAD)] * 6,
        out_specs=[rows(w) for w in widths],
        out_shape=[jax.ShapeDtypeStruct((s, w), dt) for w, dt in zip(widths, dtypes)],
        compiler_params=_params("parallel"))(
            proj, p["a_q_norm"].reshape(1, HEAD), p["a_k_norm"].reshape(1, HEAD), p["c_q_norm"].reshape(1, C_RANK),
            p["c_kv_norm"].reshape(1, C_RANK), ca, s1a, s2a, cb, s1b, s2b)


def mixer_prep_bwd(name, proj, p, tabs, dqa, dka, dva, dqb, dkb, dvb, dcqn, dckvn, dckr):
    s = proj.shape[0]
    tr = PREP_ROWS
    (ca, s1a, s2a, ha), (cb, s1b, s2b, hb) = tabs["a"][1], tabs["b"][1]

    def body(proj_ref, dqa_ref, dka_ref, dva_ref, dqb_ref, dkb_ref, dvb_ref, dcqn_ref, dckvn_ref, dckr_ref,
             gaq, gak, gcq, gckv, ca_r, s1a_r, s2a_r, cb_r, s1b_r, s2b_r, dproj_ref, dgaq, dgak, dgcq, dgckv):
        ta = (ca_r[...], s1a_r[...], s2a_r[...])
        tb = (cb_r[...], s1b_r[...], s2b_r[...])

        @pl.when(pl.program_id(0) == 0)
        def _():
            for ref in (dgaq, dgak, dgcq, dgckv):
                ref[...] = jnp.zeros_like(ref)

        def norm_bwd(dy, x, gain, dgain):
            rs = lax.rsqrt(jnp.mean(x * x, axis=-1, keepdims=True) + EPS)
            xh = x * rs
            dyg = dy * gain[...]
            dgain[...] += jnp.sum(dy * xh, axis=0, keepdims=True)
            return rs * (dyg - xh * jnp.mean(dyg * xh, axis=-1, keepdims=True))

        for (dst0, n, d_ref, gain, dgain) in ((O_AQ, A_HEADS, dqa_ref, gaq, dgaq), (O_AK, A_KV, dka_ref, gak, dgak)):
            for j, cols in _heads(dst0, n):
                dy = _rope_of(d_ref[:, HEAD * j:HEAD * (j + 1)], ta, ha)
                dproj_ref[:, cols] = norm_bwd(dy, proj_ref[:, cols], gain, dgain).astype(BF16)
        dproj_ref[:, O_AV:O_BQ] = dva_ref[...].astype(BF16)
        for (dst0, n, d_ref) in ((O_BQ, B_GROUPS * B_HEADS, dqb_ref), (O_BK, B_HEADS, dkb_ref)):
            for j, cols in _heads(dst0, n):
                dproj_ref[:, cols] = _rope_of(d_ref[:, HEAD * j:HEAD * (j + 1)], tb, hb).astype(BF16)
        dproj_ref[:, O_BV:O_CQ] = dvb_ref[...].astype(BF16)
        dproj_ref[:, O_CQ:O_CKV] = norm_bwd(dcqn_ref[...], proj_ref[:, O_CQ:O_CKV], gcq, dgcq).astype(BF16)
        dproj_ref[:, O_CKV:O_CKR] = norm_bwd(dckvn_ref[...], proj_ref[:, O_CKV:O_CKR], gckv, dgckv).astype(BF16)
        dproj_ref[:, O_CKR:] = dckr_ref[:, :C_ROPE]

    rows = lambda w: pl.BlockSpec((tr, w), lambda i: (i, 0))
    vec = lambda w: pl.BlockSpec((1, w), lambda i: (0, 0))
    grads = (dqa, dka, dva, dqb, dkb, dvb, dcqn, dckvn, dckr)
    res = pl.pallas_call(
        body, name=name, grid=(s // tr,),
        in_specs=[rows(proj.shape[1])] + [rows(g.shape[1]) for g in grads]
        + [vec(HEAD), vec(HEAD), vec(C_RANK), vec(C_RANK)] + [rows(HEAD)] * 6,
        out_specs=[rows(proj.shape[1]), vec(HEAD), vec(HEAD), vec(C_RANK), vec(C_RANK)],
        out_shape=[jax.ShapeDtypeStruct(proj.shape, BF16)] + [jax.ShapeDtypeStruct((1, w), F32) for w in (HEAD, HEAD, C_RANK, C_RANK)],
        compiler_params=_params("arbitrary"))(
            proj, *grads, p["a_q_norm"].reshape(1, HEAD), p["a_k_norm"].reshape(1, HEAD), p["c_q_norm"].reshape(1, C_RANK),
            p["c_kv_norm"].reshape(1, C_RANK), ca, s1a, s2a, cb, s1b, s2b)
    return res[0], res[1].reshape(HEAD), res[2].reshape(HEAD), res[3].reshape(C_RANK), res[4].reshape(C_RANK)


OUT_GROUPS = ((0, A_W), (A_W, A_W + B_W), (A_W + B_W, A_W + B_W + C_W))


def out_norms(name, ya, yb, yc, gain):
    s = ya.shape[0]
    tr = PREP_ROWS
    width = OUT_GROUPS[-1][1]

    def body(ya_ref, yb_ref, yc_ref, g_ref, y_ref):
        for (lo, hi), src in zip(OUT_GROUPS, (ya_ref, yb_ref, yc_ref)):
            x = src[...]
            rs = lax.rsqrt(jnp.mean(x * x, axis=-1, keepdims=True) + EPS)
            y_ref[:, lo:hi] = (x * rs * g_ref[:, lo:hi]).astype(BF16)

    rows = lambda w: pl.BlockSpec((tr, w), lambda i: (i, 0))
    return pl.pallas_call(
        body, name=name, grid=(s // tr,),
        in_specs=[rows(A_W), rows(B_W), rows(C_W), pl.BlockSpec((1, width), lambda i: (0, 0))], out_specs=rows(width),
        out_shape=jax.ShapeDtypeStruct((s, width), BF16), compiler_params=_params("parallel"))(
            ya, yb, yc, gain.reshape(1, width))


def out_norms_bwd(name, dy, ya, yb, yc, gain):
    s = ya.shape[0]
    tr = PREP_ROWS
    width = OUT_GROUPS[-1][1]

    def body(dy_ref, ya_ref, yb_ref, yc_ref, g_ref, dya_ref, dyb_ref, dyc_ref, dg_ref):
        @pl.when(pl.program_id(0) == 0)
        def _():
            dg_ref[...] = jnp.zeros_like(dg_ref)

        for (lo, hi), src, dst in zip(OUT_GROUPS, (ya_ref, yb_ref, yc_ref), (dya_ref, dyb_ref, dyc_ref)):
            x = src[...]
            d = dy_ref[:, lo:hi]
            rs = lax.rsqrt(jnp.mean(x * x, axis=-1, keepdims=True) + EPS)
            xh = x * rs
            dg = d * g_ref[:, lo:hi]
            dst[...] = (rs * (dg - xh * jnp.mean(dg * xh, axis=-1, keepdims=True))).astype(dst.dtype)
            dg_ref[:, lo:hi] += jnp.sum(d * xh, axis=0, keepdims=True)

    rows = lambda w: pl.BlockSpec((tr, w), lambda i: (i, 0))
    vec = pl.BlockSpec((1, width), lambda i: (0, 0))
    dya, dyb, dyc, dg = pl.pallas_call(
        body, name=name, grid=(s // tr,),
        in_specs=[rows(width), rows(A_W), rows(B_W), rows(C_W), vec], out_specs=[rows(A_W), rows(B_W), rows(C_W), vec],
        out_shape=[jax.ShapeDtypeStruct((s, A_W), BF16), jax.ShapeDtypeStruct((s, B_W), F32),
                   jax.ShapeDtypeStruct((s, C_W), BF16), jax.ShapeDtypeStruct((1, width), F32)],
        compiler_params=_params("arbitrary"))(dy, ya, yb, yc, gain.reshape(1, width))
    return dya, dyb, dyc, dg.reshape(width)


ATTN_TQ = 256
ATTN_FWD_TQ = 512


def attn_fwd(name, parts, v, v_group, nheads, scale):
    s = v.shape[0]
    tq = ATTN_FWD_TQ
    npart = len(parts)

    def body(*refs):
        v_ref, o_ref, lse_ref = refs[2 * npart:]
        for r0 in range(0, tq, ATTN_TQ):
            rows = slice(r0, r0 + ATTN_TQ)
            sc = None
            for p in range(npart):
                t = _dot(refs[2 * p][rows, :], refs[2 * p + 1][...], NT)
                sc = t if sc is None else sc + t
            sc = sc * scale
            m = jnp.max(sc, axis=-1, keepdims=True)
            e = jnp.exp(sc - m)
            l = jnp.sum(e, axis=-1, keepdims=True)
            o_ref[rows, :] = _dot(e.astype(BF16), v_ref[...]) / l
            lse_ref[rows, :] = jnp.broadcast_to(m + jnp.log(l), (ATTN_TQ, HEAD))

    in_specs, ins = [], []
    for q, qoff, k, kg in parts:
        in_specs.append(pl.BlockSpec((tq, HEAD), lambda h, i, qoff=qoff: (i, qoff + h)))
        in_specs.append(pl.BlockSpec((s, HEAD), lambda h, i, kg=kg: (0, h // kg)))
        ins += [q, k]
    in_specs.append(pl.BlockSpec((s, HEAD), lambda h, i: (0, h // v_group)))
    out_blk = pl.BlockSpec((tq, HEAD), lambda h, i: (i, h))
    return pl.pallas_call(
        body, name=name, grid=(nheads, s // tq), in_specs=in_specs, out_specs=[out_blk, out_blk],
        out_shape=[jax.ShapeDtypeStruct((s, nheads * HEAD), F32)] * 2,
        compiler_params=_params("parallel", "parallel"))(*ins, v)


def attn_bwd(name, parts, v, v_group, lse, o, do, nheads, scale):
    s = v.shape[0]
    tq = ATTN_FWD_TQ
    npart = len(parts)

    def body(*refs):
        v_ref, lse_ref, o_ref, do_ref = refs[2 * npart:2 * npart + 4]
        outs = refs[2 * npart + 4:]
        dq_refs, dk_refs, dv_ref = outs[:npart], outs[npart:2 * npart], outs[2 * npart]
        h, i = pl.program_id(0), pl.program_id(1)

        @pl.when((i == 0) & (h % v_group == 0))
        def _():
            dv_ref[...] = jnp.zeros_like(dv_ref)

        for p in range(npart):
            @pl.when((i == 0) & (h % parts[p][3] == 0))
            def _(p=p):
                dk_refs[p][...] = jnp.zeros_like(dk_refs[p])

        for r0 in range(0, tq, ATTN_TQ):
            rows = slice(r0, r0 + ATTN_TQ)
            sc = None
            for p in range(npart):
                t = _dot(refs[2 * p][rows, :], refs[2 * p + 1][...], NT)
                sc = t if sc is None else sc + t
            pr = jnp.exp(sc * scale - lse_ref[rows, 0:1])
            dov = do_ref[rows, :]
            delta = jnp.sum(dov.astype(F32) * o_ref[rows, :], axis=-1, keepdims=True)
            dp = _dot(dov, v_ref[...], NT)
            ds = (pr * (dp - delta) * scale).astype(BF16)
            dv_ref[...] += _dot(pr.astype(BF16), dov, TN)
            for p in range(npart):
                dq_refs[p][rows, :] = _dot(ds, refs[2 * p + 1][...])
                dk_refs[p][...] += _dot(ds, refs[2 * p][rows, :], TN)

    in_specs, ins = [], []
    for q, qoff, k, kg in parts:
        in_specs.append(pl.BlockSpec((tq, HEAD), lambda h, i, qoff=qoff: (i, qoff + h)))
        in_specs.append(pl.BlockSpec((s, HEAD), lambda h, i, kg=kg: (0, h // kg)))
        ins += [q, k]
    hq_blk = pl.BlockSpec((tq, HEAD), lambda h, i: (i, h))
    in_specs += [pl.BlockSpec((s, HEAD), lambda h, i: (0, h // v_group)), hq_blk, hq_blk, hq_blk]
    out_specs = [hq_blk] * npart
    out_shape = [jax.ShapeDtypeStruct((s, nheads * HEAD), F32)] * npart
    for q, qoff, k, kg in parts:
        out_specs.append(pl.BlockSpec((s, HEAD), lambda h, i, kg=kg: (0, h // kg)))
        out_shape.append(jax.ShapeDtypeStruct((s, nheads // kg * HEAD), F32))
    out_specs.append(pl.BlockSpec((s, HEAD), lambda h, i: (0, h // v_group)))
    out_shape.append(jax.ShapeDtypeStruct((s, nheads // v_group * HEAD), F32))
    res = pl.pallas_call(
        body, name=name, grid=(nheads, s // tq), in_specs=in_specs, out_specs=out_specs, out_shape=out_shape,
        compiler_params=_params("arbitrary", "arbitrary"))(*ins, v, lse, o, do)
    return list(res[:npart]), list(res[npart:2 * npart]), res[2 * npart]


def _band_windows(lf):
    for ib in range(lf // HEAD):
        q0 = ib * HEAD
        yield q0, max(0, q0 - B_HALF), min(lf, q0 + HEAD + B_HALF)


def _band_mask(q0, lo, hi):
    qpos = q0 + lax.broadcasted_iota(jnp.int32, (HEAD, hi - lo), 0)
    kpos = lo + lax.broadcasted_iota(jnp.int32, (HEAD, hi - lo), 1)
    return jnp.abs(qpos - kpos) <= B_HALF


def _class_rows(r, d, start, size):
    return pl.ds(r + d * start, size, stride=d) if d > 1 else pl.ds(start, size)


def dilated_fwd(name, q, k, v_src, v_block):
    s = k.shape[0]
    scale = HEAD ** -0.5

    def body(q_ref, k_ref, v_ref, y_ref, lse_ref, o_scr, l_scr):
        g_now = pl.program_id(1)
        for g, d in enumerate(B_DILATIONS):
            @pl.when(g_now == g)
            def _(g=g, d=d):
                for r in range(d):
                    for q0, lo, hi in _band_windows(s // d):
                        mine = _class_rows(r, d, q0, HEAD)
                        keys = _class_rows(r, d, lo, hi - lo)
                        sc = _dot(q_ref[mine, :].astype(BF16), k_ref[keys, :].astype(BF16), NT) * scale
                        sc = jnp.where(_band_mask(q0, lo, hi), sc, -1e30)
                        m = jnp.max(sc, axis=-1, keepdims=True)
                        e = jnp.exp(sc - m)
                        l = jnp.sum(e, axis=-1, keepdims=True)
                        o_scr.at[g][mine, :] = _dot((e / l).astype(BF16), v_ref[keys, :].astype(BF16))
                        l_scr.at[g][mine, :] = jnp.broadcast_to(m + jnp.log(l), (HEAD, HEAD))

        @pl.when(g_now == B_GROUPS - 1)
        def _():
            a, b, c = l_scr[0], l_scr[1], l_scr[2]
            m = jnp.maximum(jnp.maximum(a, b), c)
            ea, eb, ec = jnp.exp(a - m), jnp.exp(b - m), jnp.exp(c - m)
            den = ea + eb + ec
            y_ref[...] = (ea / den) * o_scr[0] + (eb / den) * o_scr[1] + (ec / den) * o_scr[2]
            lse_ref[...] = m + jnp.log(den)

    blk = lambda f: pl.BlockSpec((s, HEAD), f)
    per_head = blk(lambda h, g: (0, h))
    return pl.pallas_call(
        body, name=name, grid=(B_HEADS, B_GROUPS),
        in_specs=[blk(lambda h, g: (0, B_HEADS * g + h)), per_head, blk(lambda h, g: (0, v_block + h))],
        out_specs=[per_head, per_head], out_shape=[jax.ShapeDtypeStruct((s, B_W), F32)] * 2,
        scratch_shapes=[pltpu.VMEM((B_GROUPS, s, HEAD), F32)] * 2,
        compiler_params=_params("parallel", "arbitrary"))(q, k, v_src)


def dilated_bwd(name, q, k, v_src, v_block, dy, y, lse):
    s = k.shape[0]
    scale = HEAD ** -0.5

    def body(q_ref, k_ref, v_ref, dy_ref, y_ref, lse_ref, dq_ref, dk_ref, dv_ref, delta):
        g_now = pl.program_id(1)

        @pl.when(g_now == 0)
        def _():
            dk_ref[...] = jnp.zeros_like(dk_ref)
            dv_ref[...] = jnp.zeros_like(dv_ref)
            delta[...] = jnp.broadcast_to(jnp.sum(dy_ref[...] * y_ref[...], axis=-1, keepdims=True), (s, HEAD))

        for g, d in enumerate(B_DILATIONS):
            @pl.when(g_now == g)
            def _(d=d):
                for r in range(d):
                    for q0, lo, hi in _band_windows(s // d):
                        mine = _class_rows(r, d, q0, HEAD)
                        keys = _class_rows(r, d, lo, hi - lo)
                        qv, kv = q_ref[mine, :].astype(BF16), k_ref[keys, :].astype(BF16)
                        dyv = dy_ref[mine, :].astype(BF16)
                        sc = _dot(qv, kv, NT) * scale
                        pr = jnp.where(_band_mask(q0, lo, hi), jnp.exp(sc - lse_ref[mine, :][:, 0:1]), 0.0)
                        dp = _dot(dyv, v_ref[keys, :].astype(BF16), NT)
                        ds = (pr * (dp - delta[mine, :][:, 0:1]) * scale).astype(BF16)
                        dv_ref[keys, :] = dv_ref[keys, :] + _dot(pr.astype(BF16), dyv, TN)
                        dq_ref[mine, :] = _dot(ds, kv)
                        dk_ref[keys, :] = dk_ref[keys, :] + _dot(ds, qv, TN)

    blk = lambda f: pl.BlockSpec((s, HEAD), f)
    per_head = blk(lambda h, g: (0, h))
    by_group = blk(lambda h, g: (0, B_HEADS * g + h))
    return pl.pallas_call(
        body, name=name, grid=(B_HEADS, B_GROUPS),
        in_specs=[by_group, per_head, blk(lambda h, g: (0, v_block + h)), per_head, per_head, per_head],
        out_specs=[by_group, per_head, per_head],
        out_shape=[jax.ShapeDtypeStruct((s, B_GROUPS * B_W), F32)] + [jax.ShapeDtypeStruct((s, B_W), F32)] * 2,
        scratch_shapes=[pltpu.VMEM((s, HEAD), F32)],
        compiler_params=_params("parallel", "arbitrary"))(q, k, v_src, dy, y, lse)


FFN_TN = 256


def _edges(shape):
    row = lax.broadcasted_iota(jnp.int32, shape, 0)
    return row == 0, row == shape[0] - 1


def _shifted(h, edges):
    first, last = edges
    s = h.shape[0]
    return jnp.where(first, 0.0, pltpu.roll(h, 1, 0)), jnp.where(last, 0.0, pltpu.roll(h, s - 1, 0))


def _conv(h, cw, edges):
    prev, nxt = _shifted(h, edges)
    return prev * cw[0:1, :] + h * cw[1:2, :] + nxt * cw[2:3, :] + cw[3:4, :], prev, nxt


def _sigmoid(x):
    return 1.0 / (1.0 + jnp.exp(-x))


def ffn_up(name, xn, w_up, cwb):
    s, dm = xn.shape
    quarter = w_up.shape[3]
    fh = 2 * quarter
    tn = FFN_TN
    per = quarter // tn

    def body(x_ref, w_ref, cw_ref, h_ref, act_ref):
        xv = x_ref[...]
        hg = _dot(xv, w_ref[0])
        hu = _dot(xv, w_ref[1])
        h_ref[0] = hg
        h_ref[1] = hu
        edges = _edges(hg.shape)
        gc, _, _ = _conv(hg, cw_ref[0], edges)
        uc, _, _ = _conv(hu, cw_ref[1], edges)
        act_ref[...] = (gc * _sigmoid(gc) * uc).astype(BF16)

    return pl.pallas_call(
        body, name=name, grid=(fh // tn,),
        in_specs=[pl.BlockSpec((s, dm), lambda t: (0, 0), pipeline_mode=pl.Buffered(1)),
                  pl.BlockSpec((2, None, dm, tn), lambda t: (0, t // per, 0, t % per)),
                  pl.BlockSpec((2, 8, tn), lambda t: (0, 0, t))],
        out_specs=[pl.BlockSpec((2, s, tn), lambda t: (0, 0, t)), pl.BlockSpec((s, tn), lambda t: (0, t))],
        out_shape=[jax.ShapeDtypeStruct((2, s, fh), F32), jax.ShapeDtypeStruct((s, fh), BF16)],
        compiler_params=_params("parallel"))(xn, w_up, cwb)


def ffn_gate_bwd(name, h, dact, cwb):
    _, s, fh = h.shape
    tn = FFN_TN

    def body(h_ref, da_ref, cw_ref, dh_ref, dcw_ref):
        edges = _edges((s, tn))
        gc, gp, gn = _conv(h_ref[0], cw_ref[0], edges)
        uc, up, un = _conv(h_ref[1], cw_ref[1], edges)
        sg = _sigmoid(gc)
        da = da_ref[...]
        dgc = da * uc * (sg * (1.0 + gc * (1.0 - sg)))
        duc = da * (gc * sg)
        for idx, (hv, prev, nxt, dc) in enumerate(((h_ref[0], gp, gn, dgc), (h_ref[1], up, un, duc))):
            cw = cw_ref[idx]
            from_prev, from_next = _shifted(dc, edges)
            dh_ref[idx] = (from_next * cw[0:1, :] + dc * cw[1:2, :] + from_prev * cw[2:3, :]).astype(BF16)
            dcw_ref[idx, 0:1, :] = jnp.sum(prev * dc, axis=0, keepdims=True)
            dcw_ref[idx, 1:2, :] = jnp.sum(hv * dc, axis=0, keepdims=True)
            dcw_ref[idx, 2:3, :] = jnp.sum(nxt * dc, axis=0, keepdims=True)
            dcw_ref[idx, 3:4, :] = jnp.sum(dc, axis=0, keepdims=True)
            dcw_ref[idx, 4:8, :] = jnp.zeros((4, tn), F32)

    return pl.pallas_call(
        body, name=name, grid=(fh // tn,),
        in_specs=[pl.BlockSpec((2, s, tn), lambda t: (0, 0, t)), pl.BlockSpec((s, tn), lambda t: (0, t)),
                  pl.BlockSpec((2, 8, tn), lambda t: (0, 0, t))],
        out_specs=[pl.BlockSpec((2, s, tn), lambda t: (0, 0, t)), pl.BlockSpec((2, 8, tn), lambda t: (0, 0, t))],
        out_shape=[jax.ShapeDtypeStruct((2, s, fh), BF16), jax.ShapeDtypeStruct((2, 8, fh), F32)],
        compiler_params=_params("parallel"))(h, dact, cwb)


def ffn_dx(name, dh, w_up):
    _, s, fh = dh.shape
    dm, quarter = w_up.shape[2], w_up.shape[3]
    tk = _tile(quarter, 2816)
    per = quarter // tk
    tm, tn = _tile(s, 1024), _tile(dm, 1024)
    grid = (s // tm, dm // tn, 4 * per)
    a_spec = pl.BlockSpec((None, tm, tk), lambda i, j, k: (k // (2 * per), i, k % (2 * per)))
    b_spec = pl.BlockSpec((None, None, tn, tk), lambda i, j, k: (k // (2 * per), (k // per) % 2, j, k % per))
    o_spec = pl.BlockSpec((tm, tn), lambda i, j, k: (i, j))
    return _matmul_call(name, NT, grid, a_spec, b_spec, o_spec, jax.ShapeDtypeStruct((s, dm), F32), (tm, tn))(dh, w_up)


def ffn_dw_up(name, xn, dh):
    _, s, fh = dh.shape
    dm = xn.shape[1]
    quarter = fh // 2
    tn = _tile(quarter, 1536)
    per = quarter // tn
    tm, tk = _tile(dm, 1024), _tile(s, 2048)
    grid = (dm // tm, 4 * per, s // tk)
    a_spec = pl.BlockSpec((tk, tm), lambda i, j, k: (k, i))
    b_spec = pl.BlockSpec((None, tk, tn), lambda i, j, k: (j // (2 * per), k, j % (2 * per)))
    o_spec = pl.BlockSpec((None, None, tm, tn), lambda i, j, k: (j // (2 * per), (j // per) % 2, i, j % per))
    return _matmul_call(name, TN, grid, a_spec, b_spec, o_spec,
                        jax.ShapeDtypeStruct((2, 2, dm, quarter), F32), (tm, tn))(xn, dh)


def loss_head(name, y, target):
    s, dm = y.shape
    ts = _rows(s, dm)

    def body(y_ref, t_ref, loss_ref, dy_ref, acc):
        i = pl.program_id(0)

        @pl.when(i == 0)
        def _():
            acc[...] = jnp.zeros_like(acc)

        err = y_ref[...] - t_ref[...]
        dy_ref[...] = err * (1.0 / dm)
        acc[...] += jnp.sum(err * err, axis=0, keepdims=True)

        @pl.when(i == s // ts - 1)
        def _():
            loss_ref[...] = jnp.broadcast_to(jnp.sum(acc[...], axis=-1, keepdims=True) * (0.5 / dm), (1, LANES))

    blk = pl.BlockSpec((ts, dm), lambda i: (i, 0))
    loss, dy = pl.pallas_call(
        body, name=name, grid=(s // ts,), in_specs=[blk, blk],
        out_specs=[pl.BlockSpec((1, LANES), lambda i: (0, 0)), blk],
        out_shape=[jax.ShapeDtypeStruct((1, LANES), F32), jax.ShapeDtypeStruct((s, dm), F32)],
        scratch_shapes=[pltpu.VMEM((1, dm), F32)], compiler_params=_params("arbitrary"))(y, target)
    return loss[0, 0], dy


def attn_fwd_c(name, qc, kvc, krc, scale):
    s = qc.shape[0]
    tq = ATTN_FWD_TQ

    def body(qn_ref, qr_ref, kn_ref, kr_ref, v_ref, o_ref, lse_ref):
        for r0 in range(0, tq, ATTN_TQ):
            rows = slice(r0, r0 + ATTN_TQ)
            sc = (_dot(qn_ref[rows, :], kn_ref[...], NT) + _dot(qr_ref[rows, :], kr_ref[...], NT)) * scale
            m = jnp.max(sc, axis=-1, keepdims=True)
            e = jnp.exp(sc - m)
            l = jnp.sum(e, axis=-1, keepdims=True)
            o_ref[rows, :] = _dot(e.astype(BF16), v_ref[...]) / l
            lse_ref[rows, :] = jnp.broadcast_to(m + jnp.log(l), (ATTN_TQ, HEAD))

    qb = lambda off: pl.BlockSpec((tq, HEAD), lambda h, i: (i, off + h))
    kb = lambda f: pl.BlockSpec((s, HEAD), f)
    out_blk = pl.BlockSpec((tq, HEAD), lambda h, i: (i, h))
    return pl.pallas_call(
        body, name=name, grid=(C_HEADS, s // tq),
        in_specs=[qb(0), qb(C_HEADS), kb(lambda h, i: (0, 2 * h)), kb(lambda h, i: (0, 0)), kb(lambda h, i: (0, 2 * h + 1))],
        out_specs=[out_blk, out_blk], out_shape=[jax.ShapeDtypeStruct((s, C_W), F32)] * 2,
        compiler_params=_params("parallel", "parallel"))(qc, qc, kvc, krc, kvc)


def attn_bwd_c(name, qc, kvc, krc, lse, o, do, scale):
    s = qc.shape[0]
    tq = ATTN_FWD_TQ

    def body(qn_ref, qr_ref, kn_ref, kr_ref, v_ref, lse_ref, o_ref, do_ref, dqn_ref, dqr_ref, dkv_ref, dkr_ref):
        h, i = pl.program_id(0), pl.program_id(1)

        @pl.when(i == 0)
        def _():
            dkv_ref[...] = jnp.zeros_like(dkv_ref)

        @pl.when((i == 0) & (h == 0))
        def _():
            dkr_ref[...] = jnp.zeros_like(dkr_ref)

        for r0 in range(0, tq, ATTN_TQ):
            rows = slice(r0, r0 + ATTN_TQ)
            sc = (_dot(qn_ref[rows, :], kn_ref[...], NT) + _dot(qr_ref[rows, :], kr_ref[...], NT)) * scale
            pr = jnp.exp(sc - lse_ref[rows, 0:1])
            dov = do_ref[rows, :]
            delta = jnp.sum(dov.astype(F32) * o_ref[rows, :], axis=-1, keepdims=True)
            dp = _dot(dov, v_ref[...], NT)
            ds = (pr * (dp - delta) * scale).astype(BF16)
            dkv_ref[:, HEAD:] += _dot(pr.astype(BF16), dov, TN)
            dqn_ref[rows, :] = _dot(ds, kn_ref[...])
            dqr_ref[rows, :] = _dot(ds, kr_ref[...])
            dkv_ref[:, :HEAD] += _dot(ds, qn_ref[rows, :], TN)
            dkr_ref[...] += _dot(ds, qr_ref[rows, :], TN)

    qb = lambda off: pl.BlockSpec((tq, HEAD), lambda h, i: (i, off + h))
    kb = lambda f: pl.BlockSpec((s, HEAD), f)
    hq = pl.BlockSpec((tq, HEAD), lambda h, i: (i, h))
    kn_map, v_map, kr_map = (lambda h, i: (0, 2 * h)), (lambda h, i: (0, 2 * h + 1)), (lambda h, i: (0, 0))
    return pl.pallas_call(
        body, name=name, grid=(C_HEADS, s // tq),
        in_specs=[qb(0), qb(C_HEADS), kb(kn_map), kb(kr_map), kb(v_map), hq, hq, hq],
        out_specs=[hq, hq, pl.BlockSpec((s, 2 * HEAD), lambda h, i: (0, h)), kb(kr_map)],
        out_shape=[jax.ShapeDtypeStruct((s, C_W), F32)] * 2
        + [jax.ShapeDtypeStruct((s, 2 * C_W), F32), jax.ShapeDtypeStruct((s, HEAD), F32)],
        compiler_params=_params("arbitrary", "arbitrary"))(qc, qc, kvc, krc, kvc, lse, o, do)


def _layer_fwd(x, p, tabs):
    sv = _mixers_ab_fwd(x, p, tabs)
    x1 = _mixer_c_out_fwd(x, sv, p, tabs)
    return _ffn_fwd(x1, p, sv), sv


def _mixers_ab_fwd(x, p, tabs):
    sv = {"x": x}
    hn = rmsnorm_fwd("attn_norm", x, p["attn_norm"], x.shape[1], BF16)
    proj = matmul("in_proj", hn, p["w_in_t"], "nt", F32, tm=256)
    qa, ka, av, qb, kb, cqn, ckvn = mixer_prep("mixer_prep", proj, p, tabs)
    sv.update(hn=hn, proj=proj, qa=qa, ka=ka, av=av, qb=qb, kb=kb, cqn=cqn, ckvn=ckvn)

    ya, lse_a = attn_fwd("a_attn", [(qa, 0, ka, A_HEADS // A_KV)], av, A_HEADS // A_KV, A_HEADS, HEAD ** -0.5)
    yb, lse_b = dilated_fwd("b_dilated", qb, kb, proj, O_BV // HEAD)
    sv.update(ya=ya, lse_a=lse_a, yb=yb, lse_b=lse_b)
    return sv


def _mixer_c_out_fwd(x, sv, p, tabs):
    _mixer_c_fwd(sv, p, tabs)
    return _out_fwd(x, sv, p)


def _mixer_c_fwd(sv, p, tabs):
    qc_raw = matmul("c_uq", sv["cqn"], p["w_uq_t"], "nt", F32)
    qc = jnp.concatenate([qc_raw[:, :C_W].astype(BF16), rope("c_q_rope", qc_raw[:, C_W:], tabs["c"][0], BF16)], axis=1)
    kvc = matmul("c_ukv", sv["ckvn"], p["w_ukv"], "nn", BF16)
    ckr = jnp.pad(sv["proj"][:, O_CKR:], ((0, 0), (0, HEAD - C_ROPE)))
    krc = rope("c_k_rope", ckr, tabs["c"][0], BF16)
    c_scale = (HEAD + C_ROPE) ** -0.5
    yc, lse_c = attn_fwd_c("c_attn", qc, kvc, krc, c_scale)
    sv.update(qc=qc, kvc=kvc, krc=krc, yc=yc, lse_c=lse_c)


def _out_fwd(x, sv, p, after=0.0):
    y = out_norms("out_norms", sv["ya"], sv["yb"], sv["yc"], p["out_norm"] + after)
    x1 = matmul("out_proj", y, p["w_out"], "nn", F32, add=x)
    sv.update(y=y, x1=x1)
    return x1


def _ffn_fwd(x1, p, sv):
    _ffn_up_fwd(x1, p, sv)
    return _ffn_down_fwd(x1, p, sv)


def _ffn_up_fwd(x1, p, sv):
    xn = rmsnorm_fwd("ffn_norm", x1, p["ffn_norm"], x1.shape[1], BF16)
    h, act = ffn_up("ffn_up", xn, p["w_up"], p["cwb"])
    sv.update(xn=xn, h=h, act=act)


def _ffn_down_fwd(x1, p, sv):
    return matmul("ffn_down", sv["act"], p["w_down"], "nn", F32, add=x1)


def _layer_bwd(dx2, sv, p, tabs):
    dxn, g = _ffn_bwd(dx2, sv, p)
    dx1, dy, g_out = _out_proj_bwd(dxn, dx2, sv, p)
    dhn, g_mix = _mixers_bwd(dy, sv, p, tabs)
    dx, g_norm = _attn_norm_bwd(dhn, dx1, sv, p)
    return dx, {**g, **g_out, **g_mix, **g_norm}


def _ffn_bwd(dx2, sv, p, after=0.0):
    g = {}
    dx2b = (dx2 + after).astype(BF16)
    dact = matmul("ffn_down_dx", dx2b, p["w_down"], "nt", F32)
    g["w_down"] = matmul("ffn_down_dw", sv["act"], dx2b, "tn", F32)
    dh, g["cwb"] = ffn_gate_bwd("ffn_gate_bwd", sv["h"], dact, p["cwb"])
    dxn = ffn_dx("ffn_up_dx", dh, p["w_up"])
    g["w_up"] = ffn_dw_up("ffn_up_dw", sv["xn"], dh)
    return dxn, g


def _out_proj_bwd(dxn, dx2, sv, p, after=0.0):
    g = {}
    dm = dx2.shape[1]
    dx1, dx1b, g["ffn_norm"] = rmsnorm_bwd("ffn_norm_bwd", dxn, sv["x1"], p["ffn_norm"] + after, dm, (F32, BF16), add=dx2)
    dy = matmul("out_proj_dx", dx1b, p["w_out"], "nt", F32)
    g["w_out"] = matmul("out_proj_dw", sv["y"], dx1b, "tn", F32)
    return dx1, dy, g


def _attn_norm_bwd(dhn, dx1, sv, p, after=0.0):
    dx, d_gain = rmsnorm_bwd("attn_norm_bwd", dhn, sv["x"], p["attn_norm"] + after, dhn.shape[1], (F32,), add=dx1)
    return dx, {"attn_norm": d_gain}


def _mixers_bwd(dy, sv, p, tabs):
    part, g = _mixer_c_bwd(dy, sv, p, tabs)
    dhn, g_ab = _mixers_ab_bwd(part, sv, p, tabs)
    return dhn, {**g, **g_ab}


def _mixer_c_bwd(dy, sv, p, tabs, after=0.0):
    g = {}
    dya, dyb, dyc, g["out_norm"] = out_norms_bwd("out_norms_bwd", dy, sv["ya"], sv["yb"], sv["yc"], p["out_norm"] + after)

    c_scale = (HEAD + C_ROPE) ** -0.5
    dqn, dqr, dkv, dkr = attn_bwd_c("c_attn_bwd", sv["qc"], sv["kvc"], sv["krc"], sv["lse_c"], sv["yc"], dyc, c_scale)
    dqc = jnp.concatenate([dqn.astype(BF16), rope("c_q_rope_bwd", dqr, tabs["c"][1], BF16)], axis=1)
    dcqn = matmul("c_uq_dx", dqc, p["w_uq_t"], "nn", F32)
    g["w_uq"] = matmul("c_uq_dw", dqc, sv["cqn"], "tn", F32)
    dkvc = dkv.astype(BF16)
    dckvn = matmul("c_ukv_dx", dkvc, p["w_ukv"], "nt", F32)
    g["w_ukv"] = matmul("c_ukv_dw", sv["ckvn"], dkvc, "tn", F32)
    dckr = rope("c_k_rope_bwd", dkr, tabs["c"][1], BF16)
    return (dya, dyb, dcqn, dckvn, dckr), g


def _mixers_ab_bwd(part, sv, p, tabs, after=0.0):
    g = {}
    dya, dyb, dcqn, dckvn, dckr = part

    dqb, dkb, dvb = dilated_bwd("b_dilated_bwd", sv["qb"], sv["kb"], sv["proj"], O_BV // HEAD, dyb + after, sv["yb"],
                                sv["lse_b"])

    kg = A_HEADS // A_KV
    (dqa,), (dka,), dva = attn_bwd("a_attn_bwd", [(sv["qa"], 0, sv["ka"], kg)], sv["av"], kg, sv["lse_a"], sv["ya"],
                                   dya, A_HEADS, HEAD ** -0.5)

    dproj, g["a_q_norm"], g["a_k_norm"], g["c_q_norm"], g["c_kv_norm"] = mixer_prep_bwd(
        "mixer_prep_bwd", sv["proj"], p, tabs, dqa, dka, dva, dqb, dkb, dvb, dcqn, dckvn, dckr)
    dhn = matmul("in_proj_dx", dproj, p["w_in_t"], "nn", F32, tm=512)
    g["w_in"] = matmul("in_proj_dw", dproj, sv["hn"], "tn", F32, tn=512, tk=512)
    return dhn, g


def _local_step(x, target, layers, final_norm, tabs):
    saved = []
    for p in layers:
        x, sv = _layer_fwd(x, p, tabs)
        saved.append(sv)
    dm = x.shape[1]
    yf = rmsnorm_fwd("final_norm", x, final_norm, dm, F32)
    loss, dyf = loss_head("loss_head", yf, target)
    dx, d_final = rmsnorm_bwd("final_norm_bwd", dyf, x, final_norm, dm, (F32,))
    grads = [None] * len(layers)
    for l in reversed(range(len(layers))):
        dx, grads[l] = _layer_bwd(dx, saved[l], layers[l], tabs)
    return loss, dx, grads, d_final


HBM = pl.BlockSpec(memory_space=pl.ANY)


def _place():
    x, y, c = lax.axis_index("x"), lax.axis_index("y"), lax.axis_index("c")
    others = [(1 - x, y), (x, 1 - y), (1 - x, 1 - y)]
    return x, y, c, 2 * x + y, others


def _remote(src, dst, send_sem, recv_sem, device):
    return pltpu.make_async_remote_copy(src_ref=src, dst_ref=dst, send_sem=send_sem, recv_sem=recv_sem,
                                        device_id=device, device_id_type=MESH)


def gather_small(name, small):
    def body(in_ref, out_ref, send_sems, recv_sems, local_sem):
        x, y, c, chip, others = _place()
        mine = pltpu.make_async_copy(in_ref, out_ref.at[chip], local_sem)
        mine.start()
        copies = []
        for k, (ox, oy) in enumerate(others):
            cp = _remote(in_ref, out_ref.at[chip], send_sems.at[k], recv_sems.at[k], (ox, oy, c))
            cp.start()
            copies.append(cp)
        for k, (ox, oy) in enumerate(others):
            landed = out_ref.at[2 * ox + oy]
            _remote(landed, landed, send_sems.at[k], recv_sems.at[k], (ox, oy, c)).wait_recv()
        for cp in copies:
            cp.wait_send()
        mine.wait()

    return pl.pallas_call(
        body, name=name, in_specs=[HBM], out_specs=HBM, out_shape=jax.ShapeDtypeStruct((4,) + small.shape, small.dtype),
        scratch_shapes=[pltpu.SemaphoreType.DMA((3,)), pltpu.SemaphoreType.DMA((3,)), pltpu.SemaphoreType.DMA],
        compiler_params=pltpu.CompilerParams(has_side_effects=True))(small)


IN_HBM = pl.BlockSpec(memory_space=pltpu.HBM)
SEMS = pl.BlockSpec(memory_space=pltpu.SEMAPHORE)
DATAFLOW = pltpu.SideEffectType.DATAFLOW_SIDE_EFFECTING


BF16_ROWS_PER_TILE = 16


def _halves_by_rows(shape):
    return (shape[-2] // 2) % BF16_ROWS_PER_TILE == 0


def _half(shape, core):
    if _halves_by_rows(shape):
        size = shape[-2] // 2
        return (pl.ds(core * size, size), slice(None))
    size = shape[-1] // 2
    return (slice(None), pl.ds(core * size, size))


def _half_shape(shape):
    r, c = shape[-2:]
    return (r // 2, c) if _halves_by_rows(shape) else (r, c // 2)


def _gather_plan(bufs):
    x, y, c, chip, others = _place()
    plan = []
    for t, ref in enumerate(bufs):
        mine = _half(ref.shape, c)
        for k, (ox, oy) in enumerate(others):
            plan.append((3 * t + k, (ox, oy, c), ref.at[(chip,) + mine], ref.at[(chip,) + mine],
                         ref.at[(2 * ox + oy,) + mine]))
    return plan


def _exchange_plan(bufs):
    x, y, c, chip, others = _place()
    n = len(bufs) // 2
    plan = []
    for t in range(n):
        for k, (ox, oy) in enumerate(others):
            plan.append((3 * t + k, (ox, oy, c), bufs[t].at[2 * ox + oy], bufs[n + t].at[chip], bufs[n + t].at[2 * ox + oy]))
    return plan


def _sibling_plan(bufs):
    x, y, c, chip, others = _place()
    n = len(bufs) // 2
    plan = []
    for t in range(n):
        plan.append((t, (x, y, 1 - c), bufs[t].at[(slice(None),) + _half(bufs[t].shape, 1 - c)], bufs[n + t], bufs[n + t]))
    return plan


def _forward_plan(bufs):
    x, y, c, chip, others = _place()
    plan = []
    for t, ref in enumerate(bufs):
        mine, theirs = _half(ref.shape, c), _half(ref.shape, 1 - c)
        for k, (ox, oy) in enumerate(others):
            slot = 2 * ox + oy
            plan.append((3 * t + k, (x, y, 1 - c), ref.at[(slot,) + mine], ref.at[(slot,) + mine], ref.at[(slot,) + theirs]))
    return plan


def _join_plan(bufs):
    x, y, c, chip, others = _place()
    return [(t, (x, y, 1 - c), ref.at[_half(ref.shape, c)], ref.at[_half(ref.shape, c)], ref.at[_half(ref.shape, 1 - c)])
            for t, ref in enumerate(bufs)]


PLAN_COPIES = {_gather_plan: lambda n: 3 * n, _exchange_plan: lambda n: 3 * (n // 2), _sibling_plan: lambda n: n // 2,
               _forward_plan: lambda n: 3 * n, _join_plan: lambda n: n}


def split_start(name, bufs, plan_of, after):
    n = len(bufs)

    def body(*refs):
        ins = refs[:n]
        send_sems, recv_sems = refs[n + 1], refs[n + 2]
        token = refs[-1]
        for idx, peer, src, dst, _ in plan_of(ins):
            _remote(src, dst, send_sems.at[idx], recv_sems.at[idx], peer).start()
        token[...] = jnp.zeros_like(token)

    copies = PLAN_COPIES[plan_of](n)
    res = pl.pallas_call(
        body, name=name, in_specs=[IN_HBM] * n + [HBM],
        out_specs=(SEMS, SEMS) + (IN_HBM,) * n + (pl.BlockSpec(memory_space=pltpu.VMEM),),
        out_shape=(pltpu.SemaphoreType.DMA((copies,)), pltpu.SemaphoreType.DMA((copies,)))
        + tuple(pltpu.HBM(b.shape, b.dtype) for b in bufs) + (jax.ShapeDtypeStruct((8, LANES), F32),),
        input_output_aliases={t: 2 + t for t in range(n)},
        compiler_params=pltpu.CompilerParams(has_side_effects=DATAFLOW))(
            *[pltpu.with_memory_space_constraint(b, pltpu.HBM) for b in bufs], after)
    return res[0], res[1], list(res[2:2 + n]), res[-1]


def split_wait(name, send_sems, recv_sems, flying, plan_of, after):
    n = len(flying)

    def body(*refs):
        ins = refs[:n]
        send_ref, recv_ref = refs[n], refs[n + 1]
        for idx, peer, src, dst, landing in plan_of(ins):
            _remote(src, dst, send_ref.at[idx], recv_ref.at[idx], peer).wait_send()
            _remote(landing, landing, send_ref.at[idx], recv_ref.at[idx], peer).wait_recv()

    return list(pl.pallas_call(
        body, name=name, in_specs=[IN_HBM] * n + [SEMS, SEMS] + [HBM] * len(after), out_specs=(IN_HBM,) * n,
        out_shape=tuple(pltpu.HBM(b.shape, b.dtype) for b in flying),
        input_output_aliases={t: t for t in range(n)},
        compiler_params=pltpu.CompilerParams(has_side_effects=DATAFLOW))(*flying, send_sems, recv_sems, *after))


def _half_block(shape, tr):
    r, c = shape[-2:]
    if _halves_by_rows(shape):
        per = r // 2 // tr
        return tr, c, lambda i, core: (core * per + i, 0)
    return tr, c // 2, lambda i, core: (i, core)


def add_half(name, place, grad, theirs):
    hr, hc = _half_shape(grad.shape)
    tr = _rows(hr, hc)
    br, bc, at = _half_block(grad.shape, tr)

    def body(place_ref, g_ref, t_ref, o_ref):
        o_ref[...] = (g_ref[...] + t_ref[...]).astype(o_ref.dtype)

    whole = pl.BlockSpec((None, br, bc), lambda j, i, pr: (j, i, 0))
    return pl.pallas_call(
        body, name=name,
        grid_spec=pltpu.PrefetchScalarGridSpec(
            num_scalar_prefetch=1, grid=(4, hr // tr),
            in_specs=[pl.BlockSpec((None, br, bc), lambda j, i, pr: (j,) + at(i, pr[0])), whole],
            out_specs=whole),
        out_shape=jax.ShapeDtypeStruct((4, hr, hc), BF16),
        compiler_params=_params("parallel", "parallel"))(place, grad, theirs)


def sum_chips(name, place, own, parts, shard_shape):
    _, hr, hc = parts.shape
    tr = _rows(hr, hc)
    br, bc, at = _half_block(shard_shape, tr)

    def body(place_ref, own_ref, p1, p2, p3, o_ref):
        o_ref[...] = own_ref[...].astype(F32) + p1[...].astype(F32) + p2[...].astype(F32) + p3[...].astype(F32)

    def slot(k):
        return pl.BlockSpec((None, br, bc), lambda i, pr: (lax.rem(pr[1] + k, 4), i, 0))

    return pl.pallas_call(
        body, name=name,
        grid_spec=pltpu.PrefetchScalarGridSpec(
            num_scalar_prefetch=1, grid=(hr // tr,), in_specs=[slot(0), slot(1), slot(2), slot(3)],
            out_specs=pl.BlockSpec((br, bc), lambda i, pr: at(i, pr[0]))),
        out_shape=jax.ShapeDtypeStruct(tuple(shard_shape), F32),
        compiler_params=_params("parallel"))(place, own, parts, parts, parts)


def allreduce_small(name, buf):
    rows = buf.shape[0]

    def body(in_ref, out_ref, slots, send_sems, recv_sems):
        x, y, c, _, _ = _place()
        me = 4 * x + 2 * y + c
        slots[me] = in_ref[...]
        peers = []
        for k in range(1, 8):
            px = 1 - x if k & 4 else x
            py = 1 - y if k & 2 else y
            pc = 1 - c if k & 1 else c
            peers.append((px, py, pc))
        copies = []
        for k, peer in enumerate(peers):
            cp = _remote(in_ref, slots.at[me], send_sems.at[k], recv_sems.at[k], peer)
            cp.start()
            copies.append(cp)
        for k, (px, py, pc) in enumerate(peers):
            slot = slots.at[4 * px + 2 * py + pc]
            _remote(slot, slot, send_sems.at[k], recv_sems.at[k], (px, py, pc)).wait_recv()
        for cp in copies:
            cp.wait_send()
        acc = slots[0]
        for d in range(1, 8):
            acc = acc + slots[d]
        out_ref[...] = acc

    vm = pl.BlockSpec(memory_space=pltpu.VMEM)
    return pl.pallas_call(
        body, name=name, in_specs=[vm], out_specs=vm, out_shape=jax.ShapeDtypeStruct(buf.shape, F32),
        scratch_shapes=[pltpu.VMEM((8, rows, LANES), F32), pltpu.SemaphoreType.DMA((7,)), pltpu.SemaphoreType.DMA((7,))],
        compiler_params=pltpu.CompilerParams(has_side_effects=True, vmem_limit_bytes=VMEM_LIMIT))(buf)


def cast_layer(name, place, w, layer):
    _, r, cols = w.shape
    tr = _rows(r, cols)

    def body(place_ref, w_ref, o_ref):
        o_ref[...] = w_ref[...].astype(BF16)

    return pl.pallas_call(
        body, name=name,
        grid_spec=pltpu.PrefetchScalarGridSpec(
            num_scalar_prefetch=1, grid=(r // tr,),
            in_specs=[pl.BlockSpec((None, tr, cols), lambda i, pr: (layer, i, 0))],
            out_specs=pl.BlockSpec((None, tr, cols), lambda i, pr: (pr[1], i, 0))),
        out_shape=jax.ShapeDtypeStruct((4, r, cols), BF16),
        compiler_params=_params("parallel"))(place, w)


def _adamw_math(g, w, m, v):
    m = ADAM_B1 * m + (1.0 - ADAM_B1) * g
    v = ADAM_B2 * v + (1.0 - ADAM_B2) * (g * g)
    m_hat = m / (1.0 - ADAM_B1 ** ADAM_STEP)
    v_hat = v / (1.0 - ADAM_B2 ** ADAM_STEP)
    delta = -ADAM_LR * (m_hat / (jnp.sqrt(v_hat) + ADAM_EPS) + ADAM_WD * w)
    return delta, m, v


def adamw_layer(name, layer, g, w, m, v, prev=None):
    nl, r, cols = w.shape
    tr = _rows(r, cols, 8)

    def body(*refs):
        g_ref, w_ref, m_ref, v_ref = refs[:4]
        og, od, om, ov = refs[-4:]
        gv = g_ref[...]
        delta, m2, v2 = _adamw_math(gv, w_ref[...], m_ref[...], v_ref[...])
        og[...] = gv
        od[...] = delta
        om[...] = m2
        ov[...] = v2

    lay = pl.BlockSpec((None, tr, cols), lambda i: (layer, i, 0))
    ins = [g, w, m, v] + (list(prev) if prev is not None else [])
    return pl.pallas_call(
        body, name=name, grid=(r // tr,),
        in_specs=[pl.BlockSpec((tr, cols), lambda i: (i, 0)), lay, lay, lay] + ([HBM] * 4 if prev is not None else []),
        out_specs=[lay] * 4, out_shape=[jax.ShapeDtypeStruct((nl, r, cols), F32)] * 4,
        input_output_aliases={4 + k: k for k in range(4)} if prev is not None else {},
        compiler_params=_params("parallel"))(*ins)


def adamw_packed(name, g, w, m, v):
    def body(g_ref, w_ref, m_ref, v_ref, od, om, ov):
        delta, m2, v2 = _adamw_math(g_ref[...], w_ref[...], m_ref[...], v_ref[...])
        od[...] = delta
        om[...] = m2
        ov[...] = v2

    vm = pl.BlockSpec(memory_space=pltpu.VMEM)
    return pl.pallas_call(
        body, name=name, in_specs=[vm] * 4, out_specs=[vm] * 3, out_shape=[jax.ShapeDtypeStruct(g.shape, F32)] * 3,
        compiler_params=pltpu.CompilerParams(vmem_limit_bytes=VMEM_LIMIT))(g, w, m, v)


def _pack(arrays):
    flat = jnp.concatenate([a.reshape(-1) for a in arrays])
    pad = (-flat.shape[0]) % (8 * LANES)
    return jnp.pad(flat, (0, pad)).reshape(-1, LANES)


def _unpack(buf, shapes):
    flat = buf.reshape(-1)
    out, off = [], 0
    for shp in shapes:
        size = int(np.prod(shp))
        out.append(flat[off:off + size].reshape(shp))
        off += size
    return out


MIXER_W = ("w_in", "w_uq", "w_ukv", "w_out")
FFN_W = ("w_up", "w_down")
BIG = MIXER_W + FFN_W
TRANSPOSED = ("w_in", "w_uq")
COLUMN_CUT = ("w_ukv", "w_up")
SMALL = ("attn_norm", "a_q_norm", "a_k_norm", "c_q_norm", "c_kv_norm", "out_norm", "ffn_norm", "conv_b", "final_norm")
WEIGHTS = ("attn_norm", "w_in", "a_q_norm", "a_k_norm", "c_q_norm", "c_kv_norm", "w_uq", "w_ukv", "out_norm", "w_out",
           "ffn_norm", "w_up", "conv_w", "conv_b", "w_down", "final_norm")
INPUTS = ("x",) + WEIGHTS + ("loss_target",) + tuple("m_" + n for n in WEIGHTS) + tuple("v_" + n for n in WEIGHTS)


def _columns(g):
    return jnp.transpose(g, (1, 0, 2)).reshape(g.shape[1], 4 * g.shape[2])


def _uncolumns(w):
    r, c4 = w.shape
    return jnp.transpose(w.reshape(r, 4, c4 // 4), (1, 0, 2))


def _compute_layouts(full, conv_w, small, layer):
    p = {n: small[n][layer] for n in SMALL if n != "final_norm"}
    if "w_in" in full:
        p["w_in_t"] = full["w_in"].reshape(-1, full["w_in"].shape[2])
    if "w_uq" in full:
        uq_t = full["w_uq"].reshape(C_HEADS, HEAD + C_ROPE, C_RANK)
        p["w_uq_t"] = jnp.concatenate([uq_t[:, :HEAD].reshape(C_W, C_RANK),
                                       jnp.pad(uq_t[:, HEAD:], ((0, 0), (0, HEAD - C_ROPE), (0, 0))).reshape(C_W, C_RANK)], axis=0)
        p["w_ukv"] = _columns(full["w_ukv"])
        p["w_out"] = full["w_out"].reshape(-1, full["w_out"].shape[2])
    if "w_up" in full:
        up = full["w_up"]
        p["w_up"] = up.reshape(2, 2, up.shape[1], up.shape[2])
        p["w_down"] = full["w_down"].reshape(-1, full["w_down"].shape[2])
        fh = conv_w.shape[1] // 2
        taps = jnp.transpose(conv_w.reshape(3, 2, fh), (1, 0, 2))
        p["cwb"] = jnp.concatenate([taps, small["conv_b"][layer].reshape(2, 1, fh), jnp.zeros((2, 4, fh), F32)], axis=1)
    return p


def _shard_layout(name, g):
    if name == "w_in":
        return g.reshape(4, -1, g.shape[1])
    if name == "w_uq":
        uq_t = jnp.concatenate([g[:C_W].reshape(C_HEADS, HEAD, C_RANK),
                                g[C_W:].reshape(C_HEADS, HEAD, C_RANK)[:, :C_ROPE]], axis=1)
        return uq_t.reshape(4, -1, C_RANK)
    if name == "w_ukv":
        return _uncolumns(g)
    if name == "w_up":
        return g.reshape(4, g.shape[2], g.shape[3])
    return g.reshape(4, -1, g.shape[1])


def _conv_grads(dcwb):
    fh = dcwb.shape[2]
    return jnp.transpose(dcwb[:, 0:3, :], (1, 0, 2)).reshape(3, 2 * fh), dcwb[:, 3, :].reshape(2 * fh)


def _shard_layouts(g):
    return ({n: _shard_layout(n, g[n]) for n in BIG},) + _conv_grads(g["cwb"])


def kernel(x, attn_norm, w_in, a_q_norm, a_k_norm, c_q_norm, c_kv_norm, w_uq, w_ukv, out_norm, w_out, ffn_norm, w_up, conv_w, conv_b, w_down, final_norm, loss_target, m_attn_norm, m_w_in, m_a_q_norm, m_a_k_norm, m_c_q_norm, m_c_kv_norm, m_w_uq, m_w_ukv, m_out_norm, m_w_out, m_ffn_norm, m_w_up, m_conv_w, m_conv_b, m_w_down, m_final_norm, v_attn_norm, v_w_in, v_a_q_norm, v_a_k_norm, v_c_q_norm, v_c_kv_norm, v_w_uq, v_w_ukv, v_out_norm, v_w_out, v_ffn_norm, v_w_up, v_conv_w, v_conv_b, v_w_down, v_final_norm):
    a = dict(zip(INPUTS, (x, attn_norm, w_in, a_q_norm, a_k_norm, c_q_norm, c_kv_norm, w_uq, w_ukv, out_norm, w_out, ffn_norm, w_up, conv_w, conv_b, w_down, final_norm, loss_target, m_attn_norm, m_w_in, m_a_q_norm, m_a_k_norm, m_c_q_norm, m_c_kv_norm, m_w_uq, m_w_ukv, m_out_norm, m_w_out, m_ffn_norm, m_w_up, m_conv_w, m_conv_b, m_w_down, m_final_norm, v_attn_norm, v_w_in, v_a_q_norm, v_a_k_norm, v_c_q_norm, v_c_kv_norm, v_w_uq, v_w_ukv, v_out_norm, v_w_out, v_ffn_norm, v_w_up, v_conv_w, v_conv_b, v_w_down, v_final_norm)))
    nl = w_in.shape[0]
    seq = x.shape[1]
    for n in TRANSPOSED:
        for kind in ("", "m_", "v_"):
            a[kind + n] = jnp.swapaxes(a[kind + n], 1, 2)
    chip = 2 * lax.axis_index("x") + lax.axis_index("y")
    place = jnp.stack([lax.axis_index("c"), chip]).astype(jnp.int32)
    tabs = _rope_tables(seq)

    assert nl == 2
    quarter = conv_w.shape[2]
    taps = jnp.pad(conv_w, ((0, 0), (0, 8 - conv_w.shape[1]), (0, 0))).reshape(nl * 8, quarter)
    taps_full = gather_small("gather_taps", taps).reshape(4, nl, 8, quarter)
    conv_full = [jnp.transpose(taps_full[:, l, 0:3, :], (1, 0, 2)).reshape(3, 4 * quarter) for l in range(nl)]
    flights = {}
    here, after = place, taps_full
    for l in range(nl):
        for key, names in ((f"in{l}", ("w_in",)), (f"mix{l}", MIXER_W[1:]), (f"ffn{l}", FFN_W)):
            slots = [cast_layer(f"cast{l}_{n}", here, a[n], l) for n in names]
            send_sems, recv_sems, flying, token = split_start(f"gather_{key}_start", slots, _gather_plan, after)
            flights[key] = (names, send_sems, recv_sems, flying)
            here = after = place + token[0, 0].astype(jnp.int32)
    started = token[0, 0]

    def land(key, after):
        names, send_sems, recv_sems, flying = flights[key]
        landed = split_wait(f"gather_{key}_wait", send_sems, recv_sems, flying, _gather_plan, after)
        send_sems, recv_sems, passing, token = split_start(f"gather_{key}_pass_start", landed, _forward_plan, place)
        return (key, names, send_sems, recv_sems, passing), token

    def landed(passing, after, layer):
        key, names, send_sems, recv_sems, flying = passing
        full = split_wait(f"gather_{key}_pass_wait", send_sems, recv_sems, flying, _forward_plan, after)
        return _compute_layouts(dict(zip(names, full)), conv_full[layer], a, layer)

    h = x[0]
    layers, saved = [], []
    early = None
    for l in range(nl):
        if early is None:
            pass_in, token = land(f"in{l}", [taps_full])
            p = landed(pass_in, [token], l)
            p["attn_norm"] = p["attn_norm"] + started
        else:
            p = landed(early[0], [h], l)
        sv = _mixers_ab_fwd(h, p, tabs)
        if early is None:
            pass_mix, token = land(f"mix{l}", [sv["yb"]])
            p.update(landed(pass_mix, [token], l))
        else:
            p.update(landed(early[1], [sv["yb"]], l))
        _mixer_c_fwd(sv, p, tabs)
        pass_ffn, token = land(f"ffn{l}", [sv["yc"]])
        x1 = _out_fwd(h, sv, p, token[0, 0])
        p.update(landed(pass_ffn, [x1], l))
        _ffn_up_fwd(x1, p, sv)
        if l + 1 < nl:
            pass_in, _ = land(f"in{l + 1}", [sv["act"]])
            pass_mix, _ = land(f"mix{l + 1}", [sv["act"]])
            early = (pass_in, pass_mix)
        h = _ffn_down_fwd(x1, p, sv)
        layers.append(p)
        saved.append(sv)
    dm = h.shape[1]
    yf = rmsnorm_fwd("final_norm", h, final_norm, dm, F32)
    loss, dyf = loss_head("loss_head", yf, loss_target[0])
    loss = lax.psum(loss, ("x", "y", "c"))
    dx, d_final = rmsnorm_bwd("final_norm_bwd", dyf, h, final_norm, dm, (F32,))

    def swap_begin(tag, names, g):
        g_list = [_shard_layout(n, g[n]) for n in names]
        theirs = [lax.empty((4,) + _half_shape(s.shape), s.dtype) for s in g_list]
        send_sems, recv_sems, flying, token = split_start(f"{tag}_sibling_start", g_list + theirs, _sibling_plan, place)
        return (names, send_sems, recv_sems, flying), token[0, 0]

    def exchange_begin(tag, swap, after):
        names, send_sems, recv_sems, flying = swap
        landed = split_wait(f"{tag}_sibling_wait", send_sems, recv_sems, flying, _sibling_plan, after)
        g_list, theirs = landed[:len(names)], landed[len(names):]
        sums = [add_half(f"{tag}_add_{n}", place, gr, t) for n, gr, t in zip(names, g_list, theirs)]
        parts = [lax.empty(s.shape, s.dtype) for s in sums]
        send_sems, recv_sems, flying, token = split_start(f"{tag}_chips_start", sums + parts, _exchange_plan, place)
        return (names, [gr.shape[1:] for gr in g_list], send_sems, recv_sems, flying), token

    def exchange_end(tag, flight, after):
        names, shard_shapes, send_sems, recv_sems, flying = flight
        landed = split_wait(f"{tag}_chips_wait", send_sems, recv_sems, flying, _exchange_plan, after)
        sums, parts = landed[:len(names)], landed[len(names):]
        halves = [sum_chips(f"{tag}_sum_{n}", place, s, p, shp) for n, s, p, shp in zip(names, sums, parts, shard_shapes)]
        send_sems, recv_sems, joining, token = split_start(f"{tag}_join_start", halves, _join_plan, place)
        return (names, send_sems, recv_sems, joining), token

    def joined(tag, joining, after):
        names, send_sems, recv_sems, flying = joining
        return dict(zip(names, split_wait(f"{tag}_join_wait", send_sems, recv_sems, flying, _join_plan, after)))

    grads, conv_grads, flights_f, flights_m = [None] * nl, [None] * nl, [None] * nl, [None] * nl
    after = 0.0
    for l in reversed(range(nl)):
        sv, p = saved[l], layers[l]
        dxn, g = _ffn_bwd(dx, sv, p, after)
        swap_f, after = swap_begin(f"reduce{l}f", FFN_W, g)
        dx1, dy, g_out = _out_proj_bwd(dxn, dx, sv, p, after)
        part, g_c = _mixer_c_bwd(dy, sv, p, tabs)
        flights_f[l], token = exchange_begin(f"reduce{l}f", swap_f, [part[2]])
        dhn, g_ab = _mixers_ab_bwd(part, sv, p, tabs, token[0, 0])
        g.update(g_out)
        g.update(g_c)
        g.update(g_ab)
        swap_m, after = swap_begin(f"reduce{l}m", MIXER_W, g)
        dx, g_norm = _attn_norm_bwd(dhn, dx1, sv, p, after)
        g.update(g_norm)
        flights_m[l], token = exchange_begin(f"reduce{l}m", swap_m, [dx])
        after = token[0, 0]
        grads[l], conv_grads[l] = g, _conv_grads(g["cwb"])
    grad_x = dx

    out = {}
    join_f, token = exchange_end("reduce1f", flights_f[1], [token])
    join_m, token = exchange_end("reduce1m", flights_m[1], [token])
    reduced1 = joined("reduce1f", join_f, [token])
    for n in FFN_W:
        out[n] = adamw_layer(f"adamw1_{n}", 1, reduced1[n], a[n], a["m_" + n], a["v_" + n])
    reduced1 = joined("reduce1m", join_m, [out[n][3] for n in FFN_W])
    for n in MIXER_W:
        out[n] = adamw_layer(f"adamw1_{n}", 1, reduced1[n], a[n], a["m_" + n], a["v_" + n])

    small_g = {n: jnp.stack([grads[l][n] for l in range(nl)]) for n in SMALL if n not in ("conv_b", "final_norm")}
    small_g["conv_b"] = jnp.stack([cg[1] for cg in conv_grads])
    small_g["final_norm"] = d_final
    conv_w_g = jnp.stack([cg[0] for cg in conv_grads])
    shapes = [a[n].shape for n in SMALL] + [conv_w_g.shape]
    summed = _unpack(allreduce_small("reduce_small", _pack([small_g[n] for n in SMALL] + [conv_w_g])), shapes)
    small_g = dict(zip(SMALL, summed[:-1]))
    small_g["conv_w"] = lax.dynamic_slice_in_dim(summed[-1], chip * quarter, quarter, axis=2)
    names = SMALL + ("conv_w",)
    shapes = [a[n].shape for n in names]
    d_small, m_small, v_small = adamw_packed(
        "adamw_small", _pack([small_g[n] for n in names]), _pack([a[n] for n in names]),
        _pack([a["m_" + n] for n in names]), _pack([a["v_" + n] for n in names]))

    join_f, token = exchange_end("reduce0f", flights_f[0], [out[n][3] for n in BIG] + [d_small])
    join_m, token = exchange_end("reduce0m", flights_m[0], [token])
    reduced0 = joined("reduce0f", join_f, [token])
    for n in FFN_W:
        out[n] = adamw_layer(f"adamw0_{n}", 0, reduced0[n], a[n], a["m_" + n], a["v_" + n], prev=out[n])
    reduced0 = joined("reduce0m", join_m, [out[n][3] for n in FFN_W])
    for n in MIXER_W:
        out[n] = adamw_layer(f"adamw0_{n}", 0, reduced0[n], a[n], a["m_" + n], a["v_" + n], prev=out[n])
    for n, d_, m_, v_ in zip(names, _unpack(d_small, shapes), _unpack(m_small, shapes), _unpack(v_small, shapes)):
        out[n] = (small_g[n], d_, m_, v_)
    for n in TRANSPOSED:
        out[n] = [jnp.swapaxes(o, 1, 2) for o in out[n]]

    return (loss, grad_x[None]) + tuple(out[n][k] for k in range(4) for n in WEIGHTS)
```

```python
import functools

import jax
import jax.numpy as jnp
import numpy as np
from jax import lax
from jax.experimental import pallas as pl
from jax.experimental.pallas import tpu as pltpu

F32 = jnp.float32
BF16 = jnp.bfloat16
MESH = pl.DeviceIdType.MESH

V7X_VMEM_BYTES = 64 * 1024 * 1024
VMEM_LIMIT = V7X_VMEM_BYTES - 6 * 1024 * 1024
LANES = 128
ELEMENTWISE_BLOCK_BYTES = 4 * 1024 * 1024

HEAD = 128
A_HEADS, A_KV = 6, 2
B_HEADS, B_GROUPS = 4, 3
B_DILATIONS = (1, 4, 16)
B_HALF = 64
C_HEADS, C_RANK, C_ROPE = 6, 512, 64
GRID_W = 64
A_THETA, B_THETA, C_THETA = 10000.0, 500000.0, 10000.0
B_ROPE_DIM = 32
EPS = 1e-6
A_W, B_W, C_W = A_HEADS * HEAD, B_HEADS * HEAD, C_HEADS * HEAD
O_AQ, O_AK, O_AV = 0, 768, 1024
O_BQ, O_BK, O_BV = 1280, 2816, 3328
O_CQ, O_CKV, O_CKR = 3840, 4352, 4864
IN_W = 4928
PROJ_W = 5120

ADAM_LR, ADAM_B1, ADAM_B2, ADAM_EPS, ADAM_WD, ADAM_STEP = 0.001, 0.9, 0.999, 1e-08, 0.01, 10

NN = (((1,), (0,)), ((), ()))
NT = (((1,), (1,)), ((), ()))
TN = (((0,), (0,)), ((), ()))


def _dot(a, b, dims=NN):
    return lax.dot_general(a, b, dims, preferred_element_type=F32)


def _params(*sem):
    return pltpu.CompilerParams(dimension_semantics=sem if sem else None, vmem_limit_bytes=VMEM_LIMIT)


def _tile(n, target, unit=LANES):
    if n <= target:
        return n
    best = 0
    for t in range(unit, target + 1, unit):
        if n % t == 0:
            best = t
    return best if best else n


def _rows(r, width, itemsize=4):
    return _tile(r, max(8, ELEMENTWISE_BLOCK_BYTES // (width * itemsize)), 8)


def _matmul_call(name, dims, grid, a_spec, b_spec, o_spec, out_shape, acc_shape, add_spec=None):
    nk = grid[2]

    def body(*refs):
        a_ref, b_ref = refs[:2]
        add_ref = None if add_spec is None else refs[2]
        o_ref = refs[2 if add_spec is None else 3]

        def finish(r):
            if add_ref is not None:
                r = r + add_ref[...]
            o_ref[...] = r.astype(o_ref.dtype)

        if nk == 1:
            finish(_dot(a_ref[...], b_ref[...], dims))
            return
        acc = refs[-1]
        k = pl.program_id(2)

        @pl.when(k == 0)
        def _():
            acc[...] = _dot(a_ref[...], b_ref[...], dims)

        if nk > 2:
            @pl.when((k > 0) & (k < nk - 1))
            def _():
                acc[...] += _dot(a_ref[...], b_ref[...], dims)

        @pl.when(k == nk - 1)
        def _():
            finish(acc[...] + _dot(a_ref[...], b_ref[...], dims))

    in_specs = [a_spec, b_spec] + ([] if add_spec is None else [add_spec])
    return pl.pallas_call(
        body, name=name, grid=grid, in_specs=in_specs, out_specs=o_spec, out_shape=out_shape,
        scratch_shapes=[pltpu.VMEM(acc_shape, F32)] if nk > 1 else [],
        compiler_params=_params("parallel", "parallel", "arbitrary"))


def matmul(name, a, b, mode, out_dtype, add=None, tm=1024, tn=1024, tk=2816):
    if mode == "nn":
        (m, k), (k2, n) = a.shape, b.shape
    elif mode == "nt":
        (m, k), (n, k2) = a.shape, b.shape
    else:
        (k, m), (k2, n) = a.shape, b.shape
    assert k == k2, (name, a.shape, b.shape)
    tm, tn, tk = _tile(m, tm), _tile(n, tn), _tile(k, tk)
    grid = (m // tm, n // tn, k // tk)
    if mode == "tn":
        a_spec = pl.BlockSpec((tk, tm), lambda i, j, kk: (kk, i))
    else:
        a_spec = pl.BlockSpec((tm, tk), lambda i, j, kk: (i, kk))
    whole_b = {"pipeline_mode": pl.Buffered(1)} if grid[1] == 1 and grid[2] == 1 and grid[0] > 1 else {}
    if mode == "nt":
        b_spec = pl.BlockSpec((tn, tk), lambda i, j, kk: (j, kk), **whole_b)
    else:
        b_spec = pl.BlockSpec((tk, tn), lambda i, j, kk: (kk, j), **whole_b)
    o_spec = pl.BlockSpec((tm, tn), lambda i, j, kk: (i, j))
    dims = {"nn": NN, "nt": NT, "tn": TN}[mode]
    call = _matmul_call(name, dims, grid, a_spec, b_spec, o_spec, jax.ShapeDtypeStruct((m, n), out_dtype),
                        (tm, tn), None if add is None else o_spec)
    return call(a, b) if add is None else call(a, b, add)


def rmsnorm_fwd(name, x, g, width, out_dtype):
    r, cols = x.shape
    nb = cols // width
    tr = _rows(r, width)

    def body(x_ref, g_ref, o_ref):
        xv = x_ref[...]
        rs = lax.rsqrt(jnp.mean(xv * xv, axis=-1, keepdims=True) + EPS)
        o_ref[...] = (xv * rs * g_ref[...]).astype(o_ref.dtype)

    blk = pl.BlockSpec((tr, width), lambda i, j: (i, j))
    return pl.pallas_call(
        body, name=name, grid=(r // tr, nb),
        in_specs=[blk, pl.BlockSpec((1, width), lambda i, j: (0, 0))], out_specs=blk,
        out_shape=jax.ShapeDtypeStruct((r, cols), out_dtype),
        compiler_params=_params("parallel", "parallel"))(x, g.reshape(1, width))


def rmsnorm_bwd(name, dy, x, g, width, out_dtypes=(F32,), add=None):
    r, cols = x.shape
    nb = cols // width
    tr = _rows(r, width)
    n_out = len(out_dtypes)

    def body(*refs):
        dy_ref, x_ref, g_ref = refs[:3]
        add_ref = refs[3] if add is not None else None
        outs = refs[-(n_out + 1):-1]
        dg_ref = refs[-1]
        xv = x_ref[...]
        dyv = dy_ref[...].astype(F32)
        rs = lax.rsqrt(jnp.mean(xv * xv, axis=-1, keepdims=True) + EPS)
        xh = xv * rs
        dyg = dyv * g_ref[...]
        dx = rs * (dyg - xh * jnp.mean(dyg * xh, axis=-1, keepdims=True))
        if add_ref is not None:
            dx = dx + add_ref[...]
        for o in outs:
            o[...] = dx.astype(o.dtype)

        @pl.when((pl.program_id(0) == 0) & (pl.program_id(1) == 0))
        def _():
            dg_ref[...] = jnp.zeros_like(dg_ref)

        dg_ref[...] += jnp.sum(dyv * xh, axis=0, keepdims=True)

    blk = pl.BlockSpec((tr, width), lambda i, j: (i, j))
    vec = pl.BlockSpec((1, width), lambda i, j: (0, 0))
    ins = [dy, x, g.reshape(1, width)] + ([add] if add is not None else [])
    res = pl.pallas_call(
        body, name=name, grid=(r // tr, nb),
        in_specs=[blk, blk, vec] + ([blk] if add is not None else []),
        out_specs=[blk] * n_out + [vec],
        out_shape=[jax.ShapeDtypeStruct((r, cols), dt) for dt in out_dtypes] + [jax.ShapeDtypeStruct((1, width), F32)],
        compiler_params=_params("arbitrary", "arbitrary"))(*ins)
    return tuple(res[:n_out]) + (res[n_out].reshape(width),)


def _rope_angles(pos, dim, theta):
    inv = theta ** (-jnp.arange(0, dim, 2, dtype=F32) / dim)
    ang = pos.astype(F32)[:, None] * inv[None, :]
    return jnp.cos(ang), jnp.sin(ang)


def _rope_tables(s):
    rows = s // GRID_W
    row_pos = jnp.repeat(jnp.arange(rows), GRID_W)
    col_pos = jnp.tile(jnp.arange(GRID_W), rows)
    t_pos = jnp.arange(s)
    z = lambda n: jnp.zeros((s, n), F32)
    o = lambda n: jnp.ones((s, n), F32)
    cr, sr = _rope_angles(row_pos, HEAD // 2, A_THETA)
    cc, sc = _rope_angles(col_pos, HEAD // 2, A_THETA)
    tab_a = (jnp.concatenate([cr, cr, cc, cc], 1), jnp.concatenate([-sr, z(32), -sc, z(32)], 1),
             jnp.concatenate([z(32), sr, z(32), sc], 1), 32)
    cp, sp = _rope_angles(t_pos, B_ROPE_DIM, B_THETA)
    tab_b = (jnp.concatenate([cp, cp, o(96)], 1), jnp.concatenate([-sp, z(112)], 1),
             jnp.concatenate([z(16), sp, z(96)], 1), 16)
    cm, sm = _rope_angles(t_pos, C_ROPE, C_THETA)
    tab_c = (jnp.concatenate([cm, cm, o(64)], 1), jnp.concatenate([-sm, z(96)], 1),
             jnp.concatenate([z(32), sm, z(64)], 1), 32)

    def transposed(tab):
        c, s1, s2, h = tab
        return (c, jnp.roll(s2, -h, axis=1), jnp.roll(s1, h, axis=1), h)

    return {k: (t, transposed(t)) for k, t in (("a", tab_a), ("b", tab_b), ("c", tab_c))}


def rope(name, x, tab, out_dtype):
    c, s1, s2, h = tab
    s, cols = x.shape
    ts = _rows(s, cols)

    def body(x_ref, c_ref, s1_ref, s2_ref, o_ref):
        cv, s1v, s2v = c_ref[...], s1_ref[...], s2_ref[...]
        for j in range(cols // HEAD):
            lanes = slice(HEAD * j, HEAD * (j + 1))
            xv = x_ref[:, lanes].astype(F32)
            o_ref[:, lanes] = (xv * cv + pltpu.roll(xv, HEAD - h, 1) * s1v + pltpu.roll(xv, h, 1) * s2v).astype(o_ref.dtype)

    blk = pl.BlockSpec((ts, cols), lambda i: (i, 0))
    tb = pl.BlockSpec((ts, HEAD), lambda i: (i, 0))
    return pl.pallas_call(
        body, name=name, grid=(s // ts,), in_specs=[blk, tb, tb, tb], out_specs=blk,
        out_shape=jax.ShapeDtypeStruct((s, cols), out_dtype),
        compiler_params=_params("parallel"))(x, c, s1, s2)


PREP_ROWS = 256


def _rope_of(x, tab, h):
    c, s1, s2 = tab
    return x * c + pltpu.roll(x, HEAD - h, 1) * s1 + pltpu.roll(x, h, 1) * s2


def _heads(start, n):
    return [(j, slice(start + HEAD * j, start + HEAD * (j + 1))) for j in range(n)]


def mixer_prep(name, proj, p, tabs):
    s = proj.shape[0]
    tr = PREP_ROWS
    (ca, s1a, s2a, ha), (cb, s1b, s2b, hb) = tabs["a"][0], tabs["b"][0]

    def body(proj_ref, gaq, gak, gcq, gckv, ca_r, s1a_r, s2a_r, cb_r, s1b_r, s2b_r,
             qa_ref, ka_ref, av_ref, qb_ref, kb_ref, cqn_ref, ckvn_ref):
        ta = (ca_r[...], s1a_r[...], s2a_r[...])
        tb = (cb_r[...], s1b_r[...], s2b_r[...])

        def normed(x, g):
            return x * lax.rsqrt(jnp.mean(x * x, axis=-1, keepdims=True) + EPS) * g

        for (src0, n, gain, out) in ((O_AQ, A_HEADS, gaq, qa_ref), (O_AK, A_KV, gak, ka_ref)):
            for j, cols in _heads(src0, n):
                out[:, HEAD * j:HEAD * (j + 1)] = _rope_of(normed(proj_ref[:, cols], gain[...]), ta, ha).astype(BF16)
        av_ref[...] = proj_ref[:, O_AV:O_BQ].astype(BF16)
        for (src0, n, out) in ((O_BQ, B_GROUPS * B_HEADS, qb_ref), (O_BK, B_HEADS, kb_ref)):
            for j, cols in _heads(src0, n):
                out[:, HEAD * j:HEAD * (j + 1)] = _rope_of(proj_ref[:, cols], tb, hb)
        cqn_ref[...] = normed(proj_ref[:, O_CQ:O_CKV], gcq[...]).astype(BF16)
        ckvn_ref[...] = normed(proj_ref[:, O_CKV:O_CKR], gckv[...]).astype(BF16)

    rows = lambda w: pl.BlockSpec((tr, w), lambda i: (i, 0))
    vec = lambda w: pl.BlockSpec((1, w), lambda i: (0, 0))
    widths = (A_W, A_KV * HEAD, A_KV * HEAD, B_GROUPS * B_W, B_W, C_RANK, C_RANK)
    dtypes = (BF16, BF16, BF16, F32, F32, BF16, BF16)
    return pl.pallas_call(
        body, name=name, grid=(s // tr,),
        in_specs=[rows(proj.shape[1]), vec(HEAD), vec(HEAD), vec(C_RANK), vec(C_RANK)] + [rows(HEAD)] * 6,
        out_specs=[rows(w) for w in widths],
        out_shape=[jax.ShapeDtypeStruct((s, w), dt) for w, dt in zip(widths, dtypes)],
        compiler_params=_params("parallel"))(
            proj, p["a_q_norm"].reshape(1, HEAD), p["a_k_norm"].reshape(1, HEAD), p["c_q_norm"].reshape(1, C_RANK),
            p["c_kv_norm"].reshape(1, C_RANK), ca, s1a, s2a, cb, s1b, s2b)


def mixer_prep_bwd(name, proj, p, tabs, dqa, dka, dva, dqb, dkb, dvb, dcqn, dckvn, dckr):
    s = proj.shape[0]
    tr = PREP_ROWS
    (ca, s1a, s2a, ha), (cb, s1b, s2b, hb) = tabs["a"][1], tabs["b"][1]

    def body(proj_ref, dqa_ref, dka_ref, dva_ref, dqb_ref, dkb_ref, dvb_ref, dcqn_ref, dckvn_ref, dckr_ref,
             gaq, gak, gcq, gckv, ca_r, s1a_r, s2a_r, cb_r, s1b_r, s2b_r, dproj_ref, dgaq, dgak, dgcq, dgckv):
        ta = (ca_r[...], s1a_r[...], s2a_r[...])
        tb = (cb_r[...], s1b_r[...], s2b_r[...])

        @pl.when(pl.program_id(0) == 0)
        def _():
            for ref in (dgaq, dgak, dgcq, dgckv):
                ref[...] = jnp.zeros_like(ref)

        def norm_bwd(dy, x, gain, dgain):
            rs = lax.rsqrt(jnp.mean(x * x, axis=-1, keepdims=True) + EPS)
            xh = x * rs
            dyg = dy * gain[...]
            dgain[...] += jnp.sum(dy * xh, axis=0, keepdims=True)
            return rs * (dyg - xh * jnp.mean(dyg * xh, axis=-1, keepdims=True))

        for (dst0, n, d_ref, gain, dgain) in ((O_AQ, A_HEADS, dqa_ref, gaq, dgaq), (O_AK, A_KV, dka_ref, gak, dgak)):
            for j, cols in _heads(dst0, n):
                dy = _rope_of(d_ref[:, HEAD * j:HEAD * (j + 1)], ta, ha)
                dproj_ref[:, cols] = norm_bwd(dy, proj_ref[:, cols], gain, dgain).astype(BF16)
        dproj_ref[:, O_AV:O_BQ] = dva_ref[...].astype(BF16)
        for (dst0, n, d_ref) in ((O_BQ, B_GROUPS * B_HEADS, dqb_ref), (O_BK, B_HEADS, dkb_ref)):
            for j, cols in _heads(dst0, n):
                dproj_ref[:, cols] = _rope_of(d_ref[:, HEAD * j:HEAD * (j + 1)], tb, hb).astype(BF16)
        dproj_ref[:, O_BV:O_CQ] = dvb_ref[...].astype(BF16)
        dproj_ref[:, O_CQ:O_CKV] = norm_bwd(dcqn_ref[...], proj_ref[:, O_CQ:O_CKV], gcq, dgcq).astype(BF16)
        dproj_ref[:, O_CKV:O_CKR] = norm_bwd(dckvn_ref[...], proj_ref[:, O_CKV:O_CKR], gckv, dgckv).astype(BF16)
        dproj_ref[:, O_CKR:] = dckr_ref[:, :C_ROPE]

    rows = lambda w: pl.BlockSpec((tr, w), lambda i: (i, 0))
    vec = lambda w: pl.BlockSpec((1, w), lambda i: (0, 0))
    grads = (dqa, dka, dva, dqb, dkb, dvb, dcqn, dckvn, dckr)
    res = pl.pallas_call(
        body, name=name, grid=(s // tr,),
        in_specs=[rows(proj.shape[1])] + [rows(g.shape[1]) for g in grads]
        + [vec(HEAD), vec(HEAD), vec(C_RANK), vec(C_RANK)] + [rows(HEAD)] * 6,
        out_specs=[rows(proj.shape[1]), vec(HEAD), vec(HEAD), vec(C_RANK), vec(C_RANK)],
        out_shape=[jax.ShapeDtypeStruct(proj.shape, BF16)] + [jax.ShapeDtypeStruct((1, w), F32) for w in (HEAD, HEAD, C_RANK, C_RANK)],
        compiler_params=_params("arbitrary"))(
            proj, *grads, p["a_q_norm"].reshape(1, HEAD), p["a_k_norm"].reshape(1, HEAD), p["c_q_norm"].reshape(1, C_RANK),
            p["c_kv_norm"].reshape(1, C_RANK), ca, s1a, s2a, cb, s1b, s2b)
    return res[0], res[1].reshape(HEAD), res[2].reshape(HEAD), res[3].reshape(C_RANK), res[4].reshape(C_RANK)


OUT_GROUPS = ((0, A_W), (A_W, A_W + B_W), (A_W + B_W, A_W + B_W + C_W))


def out_norms(name, ya, yb, yc, gain):
    s = ya.shape[0]
    tr = PREP_ROWS
    width = OUT_GROUPS[-1][1]

    def body(ya_ref, yb_ref, yc_ref, g_ref, y_ref):
        for (lo, hi), src in zip(OUT_GROUPS, (ya_ref, yb_ref, yc_ref)):
            x = src[...]
            rs = lax.rsqrt(jnp.mean(x * x, axis=-1, keepdims=True) + EPS)
            y_ref[:, lo:hi] = (x * rs * g_ref[:, lo:hi]).astype(BF16)

    rows = lambda w: pl.BlockSpec((tr, w), lambda i: (i, 0))
    return pl.pallas_call(
        body, name=name, grid=(s // tr,),
        in_specs=[rows(A_W), rows(B_W), rows(C_W), pl.BlockSpec((1, width), lambda i: (0, 0))], out_specs=rows(width),
        out_shape=jax.ShapeDtypeStruct((s, width), BF16), compiler_params=_params("parallel"))(
            ya, yb, yc, gain.reshape(1, width))


def out_norms_bwd(name, dy, ya, yb, yc, gain):
    s = ya.shape[0]
    tr = PREP_ROWS
    width = OUT_GROUPS[-1][1]

    def body(dy_ref, ya_ref, yb_ref, yc_ref, g_ref, dya_ref, dyb_ref, dyc_ref, dg_ref):
        @pl.when(pl.program_id(0) == 0)
        def _():
            dg_ref[...] = jnp.zeros_like(dg_ref)

        for (lo, hi), src, dst in zip(OUT_GROUPS, (ya_ref, yb_ref, yc_ref), (dya_ref, dyb_ref, dyc_ref)):
            x = src[...]
            d = dy_ref[:, lo:hi]
            rs = lax.rsqrt(jnp.mean(x * x, axis=-1, keepdims=True) + EPS)
            xh = x * rs
            dg = d * g_ref[:, lo:hi]
            dst[...] = (rs * (dg - xh * jnp.mean(dg * xh, axis=-1, keepdims=True))).astype(dst.dtype)
            dg_ref[:, lo:hi] += jnp.sum(d * xh, axis=0, keepdims=True)

    rows = lambda w: pl.BlockSpec((tr, w), lambda i: (i, 0))
    vec = pl.BlockSpec((1, width), lambda i: (0, 0))
    dya, dyb, dyc, dg = pl.pallas_call(
        body, name=name, grid=(s // tr,),
        in_specs=[rows(width), rows(A_W), rows(B_W), rows(C_W), vec], out_specs=[rows(A_W), rows(B_W), rows(C_W), vec],
        out_shape=[jax.ShapeDtypeStruct((s, A_W), BF16), jax.ShapeDtypeStruct((s, B_W), F32),
                   jax.ShapeDtypeStruct((s, C_W), BF16), jax.ShapeDtypeStruct((1, width), F32)],
        compiler_params=_params("arbitrary"))(dy, ya, yb, yc, gain.reshape(1, width))
    return dya, dyb, dyc, dg.reshape(width)


ATTN_TQ = 256
ATTN_FWD_TQ = 512


def attn_fwd(name, parts, v, v_group, nheads, scale):
    s = v.shape[0]
    tq = ATTN_FWD_TQ
    npart = len(parts)

    def body(*refs):
        v_ref, o_ref, lse_ref = refs[2 * npart:]
        for r0 in range(0, tq, ATTN_TQ):
            rows = slice(r0, r0 + ATTN_TQ)
            sc = None
            for p in range(npart):
                t = _dot(refs[2 * p][rows, :], refs[2 * p + 1][...], NT)
                sc = t if sc is None else sc + t
            sc = sc * scale
            m = jnp.max(sc, axis=-1, keepdims=True)
            e = jnp.exp(sc - m)
            l = jnp.sum(e, axis=-1, keepdims=True)
            o_ref[rows, :] = _dot(e.astype(BF16), v_ref[...]) / l
            lse_ref[rows, :] = jnp.broadcast_to(m + jnp.log(l), (ATTN_TQ, HEAD))

    in_specs, ins = [], []
    for q, qoff, k, kg in parts:
        in_specs.append(pl.BlockSpec((tq, HEAD), lambda h, i, qoff=qoff: (i, qoff + h)))
        in_specs.append(pl.BlockSpec((s, HEAD), lambda h, i, kg=kg: (0, h // kg)))
        ins += [q, k]
    in_specs.append(pl.BlockSpec((s, HEAD), lambda h, i: (0, h // v_group)))
    out_blk = pl.BlockSpec((tq, HEAD), lambda h, i: (i, h))
    return pl.pallas_call(
        body, name=name, grid=(nheads, s // tq), in_specs=in_specs, out_specs=[out_blk, out_blk],
        out_shape=[jax.ShapeDtypeStruct((s, nheads * HEAD), F32)] * 2,
        compiler_params=_params("parallel", "parallel"))(*ins, v)


def attn_bwd(name, parts, v, v_group, lse, o, do, nheads, scale):
    s = v.shape[0]
    tq = ATTN_FWD_TQ
    npart = len(parts)

    def body(*refs):
        v_ref, lse_ref, o_ref, do_ref = refs[2 * npart:2 * npart + 4]
        outs = refs[2 * npart + 4:]
        dq_refs, dk_refs, dv_ref = outs[:npart], outs[npart:2 * npart], outs[2 * npart]
        h, i = pl.program_id(0), pl.program_id(1)

        @pl.when((i == 0) & (h % v_group == 0))
        def _():
            dv_ref[...] = jnp.zeros_like(dv_ref)

        for p in range(npart):
            @pl.when((i == 0) & (h % parts[p][3] == 0))
            def _(p=p):
                dk_refs[p][...] = jnp.zeros_like(dk_refs[p])

        for r0 in range(0, tq, ATTN_TQ):
            rows = slice(r0, r0 + ATTN_TQ)
            sc = None
            for p in range(npart):
                t = _dot(refs[2 * p][rows, :], refs[2 * p + 1][...], NT)
                sc = t if sc is None else sc + t
            pr = jnp.exp(sc * scale - lse_ref[rows, 0:1])
            dov = do_ref[rows, :]
            delta = jnp.sum(dov.astype(F32) * o_ref[rows, :], axis=-1, keepdims=True)
            dp = _dot(dov, v_ref[...], NT)
            ds = (pr * (dp - delta) * scale).astype(BF16)
            dv_ref[...] += _dot(pr.astype(BF16), dov, TN)
            for p in range(npart):
                dq_refs[p][rows, :] = _dot(ds, refs[2 * p + 1][...])
                dk_refs[p][...] += _dot(ds, refs[2 * p][rows, :], TN)

    in_specs, ins = [], []
    for q, qoff, k, kg in parts:
        in_specs.append(pl.BlockSpec((tq, HEAD), lambda h, i, qoff=qoff: (i, qoff + h)))
        in_specs.append(pl.BlockSpec((s, HEAD), lambda h, i, kg=kg: (0, h // kg)))
        ins += [q, k]
    hq_blk = pl.BlockSpec((tq, HEAD), lambda h, i: (i, h))
    in_specs += [pl.BlockSpec((s, HEAD), lambda h, i: (0, h // v_group)), hq_blk, hq_blk, hq_blk]
    out_specs = [hq_blk] * npart
    out_shape = [jax.ShapeDtypeStruct((s, nheads * HEAD), F32)] * npart
    for q, qoff, k, kg in parts:
        out_specs.append(pl.BlockSpec((s, HEAD), lambda h, i, kg=kg: (0, h // kg)))
        out_shape.append(jax.ShapeDtypeStruct((s, nheads // kg * HEAD), F32))
    out_specs.append(pl.BlockSpec((s, HEAD), lambda h, i: (0, h // v_group)))
    out_shape.append(jax.ShapeDtypeStruct((s, nheads // v_group * HEAD), F32))
    res = pl.pallas_call(
        body, name=name, grid=(nheads, s // tq), in_specs=in_specs, out_specs=out_specs, out_shape=out_shape,
        compiler_params=_params("arbitrary", "arbitrary"))(*ins, v, lse, o, do)
    return list(res[:npart]), list(res[npart:2 * npart]), res[2 * npart]


def _band_windows(lf):
    for ib in range(lf // HEAD):
        q0 = ib * HEAD
        yield q0, max(0, q0 - B_HALF), min(lf, q0 + HEAD + B_HALF)


def _band_mask(q0, lo, hi):
    qpos = q0 + lax.broadcasted_iota(jnp.int32, (HEAD, hi - lo), 0)
    kpos = lo + lax.broadcasted_iota(jnp.int32, (HEAD, hi - lo), 1)
    return jnp.abs(qpos - kpos) <= B_HALF


def _class_rows(r, d, start, size):
    return pl.ds(r + d * start, size, stride=d) if d > 1 else pl.ds(start, size)


def dilated_fwd(name, q, k, v_src, v_block):
    s = k.shape[0]
    scale = HEAD ** -0.5

    def body(q_ref, k_ref, v_ref, y_ref, lse_ref, o_scr, l_scr):
        g_now = pl.program_id(1)
        for g, d in enumerate(B_DILATIONS):
            @pl.when(g_now == g)
            def _(g=g, d=d):
                for r in range(d):
                    for q0, lo, hi in _band_windows(s // d):
                        mine = _class_rows(r, d, q0, HEAD)
                        keys = _class_rows(r, d, lo, hi - lo)
                        sc = _dot(q_ref[mine, :].astype(BF16), k_ref[keys, :].astype(BF16), NT) * scale
                        sc = jnp.where(_band_mask(q0, lo, hi), sc, -1e30)
                        m = jnp.max(sc, axis=-1, keepdims=True)
                        e = jnp.exp(sc - m)
                        l = jnp.sum(e, axis=-1, keepdims=True)
                        o_scr.at[g][mine, :] = _dot((e / l).astype(BF16), v_ref[keys, :].astype(BF16))
                        l_scr.at[g][mine, :] = jnp.broadcast_to(m + jnp.log(l), (HEAD, HEAD))

        @pl.when(g_now == B_GROUPS - 1)
        def _():
            a, b, c = l_scr[0], l_scr[1], l_scr[2]
            m = jnp.maximum(jnp.maximum(a, b), c)
            ea, eb, ec = jnp.exp(a - m), jnp.exp(b - m), jnp.exp(c - m)
            den = ea + eb + ec
            y_ref[...] = (ea / den) * o_scr[0] + (eb / den) * o_scr[1] + (ec / den) * o_scr[2]
            lse_ref[...] = m + jnp.log(den)

    blk = lambda f: pl.BlockSpec((s, HEAD), f)
    per_head = blk(lambda h, g: (0, h))
    return pl.pallas_call(
        body, name=name, grid=(B_HEADS, B_GROUPS),
        in_specs=[blk(lambda h, g: (0, B_HEADS * g + h)), per_head, blk(lambda h, g: (0, v_block + h))],
        out_specs=[per_head, per_head], out_shape=[jax.ShapeDtypeStruct((s, B_W), F32)] * 2,
        scratch_shapes=[pltpu.VMEM((B_GROUPS, s, HEAD), F32)] * 2,
        compiler_params=_params("parallel", "arbitrary"))(q, k, v_src)


def dilated_bwd(name, q, k, v_src, v_block, dy, y, lse):
    s = k.shape[0]
    scale = HEAD ** -0.5

    def body(q_ref, k_ref, v_ref, dy_ref, y_ref, lse_ref, dq_ref, dk_ref, dv_ref, delta):
        g_now = pl.program_id(1)

        @pl.when(g_now == 0)
        def _():
            dk_ref[...] = jnp.zeros_like(dk_ref)
            dv_ref[...] = jnp.zeros_like(dv_ref)
            delta[...] = jnp.broadcast_to(jnp.sum(dy_ref[...] * y_ref[...], axis=-1, keepdims=True), (s, HEAD))

        for g, d in enumerate(B_DILATIONS):
            @pl.when(g_now == g)
            def _(d=d):
                for r in range(d):
                    for q0, lo, hi in _band_windows(s // d):
                        mine = _class_rows(r, d, q0, HEAD)
                        keys = _class_rows(r, d, lo, hi - lo)
                        qv, kv = q_ref[mine, :].astype(BF16), k_ref[keys, :].astype(BF16)
                        dyv = dy_ref[mine, :].astype(BF16)
                        sc = _dot(qv, kv, NT) * scale
                        pr = jnp.where(_band_mask(q0, lo, hi), jnp.exp(sc - lse_ref[mine, :][:, 0:1]), 0.0)
                        dp = _dot(dyv, v_ref[keys, :].astype(BF16), NT)
                        ds = (pr * (dp - delta[mine, :][:, 0:1]) * scale).astype(BF16)
                        dv_ref[keys, :] = dv_ref[keys, :] + _dot(pr.astype(BF16), dyv, TN)
                        dq_ref[mine, :] = _dot(ds, kv)
                        dk_ref[keys, :] = dk_ref[keys, :] + _dot(ds, qv, TN)

    blk = lambda f: pl.BlockSpec((s, HEAD), f)
    per_head = blk(lambda h, g: (0, h))
    by_group = blk(lambda h, g: (0, B_HEADS * g + h))
    return pl.pallas_call(
        body, name=name, grid=(B_HEADS, B_GROUPS),
        in_specs=[by_group, per_head, blk(lambda h, g: (0, v_block + h)), per_head, per_head, per_head],
        out_specs=[by_group, per_head, per_head],
        out_shape=[jax.ShapeDtypeStruct((s, B_GROUPS * B_W), F32)] + [jax.ShapeDtypeStruct((s, B_W), F32)] * 2,
        scratch_shapes=[pltpu.VMEM((s, HEAD), F32)],
        compiler_params=_params("parallel", "arbitrary"))(q, k, v_src, dy, y, lse)


FFN_TN = 256


def _edges(shape):
    row = lax.broadcasted_iota(jnp.int32, shape, 0)
    return row == 0, row == shape[0] - 1


def _shifted(h, edges):
    first, last = edges
    s = h.shape[0]
    return jnp.where(first, 0.0, pltpu.roll(h, 1, 0)), jnp.where(last, 0.0, pltpu.roll(h, s - 1, 0))


def _conv(h, cw, edges):
    prev, nxt = _shifted(h, edges)
    return prev * cw[0:1, :] + h * cw[1:2, :] + nxt * cw[2:3, :] + cw[3:4, :], prev, nxt


def _sigmoid(x):
    return 1.0 / (1.0 + jnp.exp(-x))


def ffn_up(name, xn, w_up, cwb):
    s, dm = xn.shape
    quarter = w_up.shape[3]
    fh = 2 * quarter
    tn = FFN_TN
    per = quarter // tn

    def body(x_ref, w_ref, cw_ref, h_ref, act_ref):
        xv = x_ref[...]
        hg = _dot(xv, w_ref[0])
        hu = _dot(xv, w_ref[1])
        h_ref[0] = hg
        h_ref[1] = hu
        edges = _edges(hg.shape)
        gc, _, _ = _conv(hg, cw_ref[0], edges)
        uc, _, _ = _conv(hu, cw_ref[1], edges)
        act_ref[...] = (gc * _sigmoid(gc) * uc).astype(BF16)

    return pl.pallas_call(
        body, name=name, grid=(fh // tn,),
        in_specs=[pl.BlockSpec((s, dm), lambda t: (0, 0), pipeline_mode=pl.Buffered(1)),
                  pl.BlockSpec((2, None, dm, tn), lambda t: (0, t // per, 0, t % per)),
                  pl.BlockSpec((2, 8, tn), lambda t: (0, 0, t))],
        out_specs=[pl.BlockSpec((2, s, tn), lambda t: (0, 0, t)), pl.BlockSpec((s, tn), lambda t: (0, t))],
        out_shape=[jax.ShapeDtypeStruct((2, s, fh), F32), jax.ShapeDtypeStruct((s, fh), BF16)],
        compiler_params=_params("parallel"))(xn, w_up, cwb)


def ffn_gate_bwd(name, h, dact, cwb):
    _, s, fh = h.shape
    tn = FFN_TN

    def body(h_ref, da_ref, cw_ref, dh_ref, dcw_ref):
        edges = _edges((s, tn))
        gc, gp, gn = _conv(h_ref[0], cw_ref[0], edges)
        uc, up, un = _conv(h_ref[1], cw_ref[1], edges)
        sg = _sigmoid(gc)
        da = da_ref[...]
        dgc = da * uc * (sg * (1.0 + gc * (1.0 - sg)))
        duc = da * (gc * sg)
        for idx, (hv, prev, nxt, dc) in enumerate(((h_ref[0], gp, gn, dgc), (h_ref[1], up, un, duc))):
            cw = cw_ref[idx]
            from_prev, from_next = _shifted(dc, edges)
            dh_ref[idx] = (from_next * cw[0:1, :] + dc * cw[1:2, :] + from_prev * cw[2:3, :]).astype(BF16)
            dcw_ref[idx, 0:1, :] = jnp.sum(prev * dc, axis=0, keepdims=True)
            dcw_ref[idx, 1:2, :] = jnp.sum(hv * dc, axis=0, keepdims=True)
            dcw_ref[idx, 2:3, :] = jnp.sum(nxt * dc, axis=0, keepdims=True)
            dcw_ref[idx, 3:4, :] = jnp.sum(dc, axis=0, keepdims=True)
            dcw_ref[idx, 4:8, :] = jnp.zeros((4, tn), F32)

    return pl.pallas_call(
        body, name=name, grid=(fh // tn,),
        in_specs=[pl.BlockSpec((2, s, tn), lambda t: (0, 0, t)), pl.BlockSpec((s, tn), lambda t: (0, t)),
                  pl.BlockSpec((2, 8, tn), lambda t: (0, 0, t))],
        out_specs=[pl.BlockSpec((2, s, tn), lambda t: (0, 0, t)), pl.BlockSpec((2, 8, tn), lambda t: (0, 0, t))],
        out_shape=[jax.ShapeDtypeStruct((2, s, fh), BF16), jax.ShapeDtypeStruct((2, 8, fh), F32)],
        compiler_params=_params("parallel"))(h, dact, cwb)


def ffn_dx(name, dh, w_up):
    _, s, fh = dh.shape
    dm, quarter = w_up.shape[2], w_up.shape[3]
    tk = _tile(quarter, 2816)
    per = quarter // tk
    tm, tn = _tile(s, 1024), _tile(dm, 1024)
    grid = (s // tm, dm // tn, 4 * per)
    a_spec = pl.BlockSpec((None, tm, tk), lambda i, j, k: (k // (2 * per), i, k % (2 * per)))
    b_spec = pl.BlockSpec((None, None, tn, tk), lambda i, j, k: (k // (2 * per), (k // per) % 2, j, k % per))
    o_spec = pl.BlockSpec((tm, tn), lambda i, j, k: (i, j))
    return _matmul_call(name, NT, grid, a_spec, b_spec, o_spec, jax.ShapeDtypeStruct((s, dm), F32), (tm, tn))(dh, w_up)


def ffn_dw_up(name, xn, dh):
    _, s, fh = dh.shape
    dm = xn.shape[1]
    quarter = fh // 2
    tn = _tile(quarter, 1536)
    per = quarter // tn
    tm, tk = _tile(dm, 1024), _tile(s, 2048)
    grid = (dm // tm, 4 * per, s // tk)
    a_spec = pl.BlockSpec((tk, tm), lambda i, j, k: (k, i))
    b_spec = pl.BlockSpec((None, tk, tn), lambda i, j, k: (j // (2 * per), k, j % (2 * per)))
    o_spec = pl.BlockSpec((None, None, tm, tn), lambda i, j, k: (j // (2 * per), (j // per) % 2, i, j % per))
    return _matmul_call(name, TN, grid, a_spec, b_spec, o_spec,
                        jax.ShapeDtypeStruct((2, 2, dm, quarter), F32), (tm, tn))(xn, dh)


def loss_head(name, y, target):
    s, dm = y.shape
    ts = _rows(s, dm)

    def body(y_ref, t_ref, loss_ref, dy_ref, acc):
        i = pl.program_id(0)

        @pl.when(i == 0)
        def _():
            acc[...] = jnp.zeros_like(acc)

        err = y_ref[...] - t_ref[...]
        dy_ref[...] = err * (1.0 / dm)
        acc[...] += jnp.sum(err * err, axis=0, keepdims=True)

        @pl.when(i == s // ts - 1)
        def _():
            loss_ref[...] = jnp.broadcast_to(jnp.sum(acc[...], axis=-1, keepdims=True) * (0.5 / dm), (1, LANES))

    blk = pl.BlockSpec((ts, dm), lambda i: (i, 0))
    loss, dy = pl.pallas_call(
        body, name=name, grid=(s // ts,), in_specs=[blk, blk],
        out_specs=[pl.BlockSpec((1, LANES), lambda i: (0, 0)), blk],
        out_shape=[jax.ShapeDtypeStruct((1, LANES), F32), jax.ShapeDtypeStruct((s, dm), F32)],
        scratch_shapes=[pltpu.VMEM((1, dm), F32)], compiler_params=_params("arbitrary"))(y, target)
    return loss[0, 0], dy


def attn_fwd_c(name, qc, kvc, krc, scale):
    s = qc.shape[0]
    tq = ATTN_FWD_TQ

    def body(qn_ref, qr_ref, kn_ref, kr_ref, v_ref, o_ref, lse_ref):
        for r0 in range(0, tq, ATTN_TQ):
            rows = slice(r0, r0 + ATTN_TQ)
            sc = (_dot(qn_ref[rows, :], kn_ref[...], NT) + _dot(qr_ref[rows, :], kr_ref[...], NT)) * scale
            m = jnp.max(sc, axis=-1, keepdims=True)
            e = jnp.exp(sc - m)
            l = jnp.sum(e, axis=-1, keepdims=True)
            o_ref[rows, :] = _dot(e.astype(BF16), v_ref[...]) / l
            lse_ref[rows, :] = jnp.broadcast_to(m + jnp.log(l), (ATTN_TQ, HEAD))

    qb = lambda off: pl.BlockSpec((tq, HEAD), lambda h, i: (i, off + h))
    kb = lambda f: pl.BlockSpec((s, HEAD), f)
    out_blk = pl.BlockSpec((tq, HEAD), lambda h, i: (i, h))
    return pl.pallas_call(
        body, name=name, grid=(C_HEADS, s // tq),
        in_specs=[qb(0), qb(C_HEADS), kb(lambda h, i: (0, 2 * h)), kb(lambda h, i: (0, 0)), kb(lambda h, i: (0, 2 * h + 1))],
        out_specs=[out_blk, out_blk], out_shape=[jax.ShapeDtypeStruct((s, C_W), F32)] * 2,
        compiler_params=_params("parallel", "parallel"))(qc, qc, kvc, krc, kvc)


def attn_bwd_c(name, qc, kvc, krc, lse, o, do, scale):
    s = qc.shape[0]
    tq = ATTN_FWD_TQ

    def body(qn_ref, qr_ref, kn_ref, kr_ref, v_ref, lse_ref, o_ref, do_ref, dqn_ref, dqr_ref, dkv_ref, dkr_ref):
        h, i = pl.program_id(0), pl.program_id(1)

        @pl.when(i == 0)
        def _():
            dkv_ref[...] = jnp.zeros_like(dkv_ref)

        @pl.when((i == 0) & (h == 0))
        def _():
            dkr_ref[...] = jnp.zeros_like(dkr_ref)

        for r0 in range(0, tq, ATTN_TQ):
            rows = slice(r0, r0 + ATTN_TQ)
            sc = (_dot(qn_ref[rows, :], kn_ref[...], NT) + _dot(qr_ref[rows, :], kr_ref[...], NT)) * scale
            pr = jnp.exp(sc - lse_ref[rows, 0:1])
            dov = do_ref[rows, :]
            delta = jnp.sum(dov.astype(F32) * o_ref[rows, :], axis=-1, keepdims=True)
            dp = _dot(dov, v_ref[...], NT)
            ds = (pr * (dp - delta) * scale).astype(BF16)
            dkv_ref[:, HEAD:] += _dot(pr.astype(BF16), dov, TN)
            dqn_ref[rows, :] = _dot(ds, kn_ref[...])
            dqr_ref[rows, :] = _dot(ds, kr_ref[...])
            dkv_ref[:, :HEAD] += _dot(ds, qn_ref[rows, :], TN)
            dkr_ref[...] += _dot(ds, qr_ref[rows, :], TN)

    qb = lambda off: pl.BlockSpec((tq, HEAD), lambda h, i: (i, off + h))
    kb = lambda f: pl.BlockSpec((s, HEAD), f)
    hq = pl.BlockSpec((tq, HEAD), lambda h, i: (i, h))
    kn_map, v_map, kr_map = (lambda h, i: (0, 2 * h)), (lambda h, i: (0, 2 * h + 1)), (lambda h, i: (0, 0))
    return pl.pallas_call(
        body, name=name, grid=(C_HEADS, s // tq),
        in_specs=[qb(0), qb(C_HEADS), kb(kn_map), kb(kr_map), kb(v_map), hq, hq, hq],
        out_specs=[hq, hq, pl.BlockSpec((s, 2 * HEAD), lambda h, i: (0, h)), kb(kr_map)],
        out_shape=[jax.ShapeDtypeStruct((s, C_W), F32)] * 2
        + [jax.ShapeDtypeStruct((s, 2 * C_W), F32), jax.ShapeDtypeStruct((s, HEAD), F32)],
        compiler_params=_params("arbitrary", "arbitrary"))(qc, qc, kvc, krc, kvc, lse, o, do)


def _layer_fwd(x, p, tabs):
    sv = _mixers_ab_fwd(x, p, tabs)
    x1 = _mixer_c_out_fwd(x, sv, p, tabs)
    return _ffn_fwd(x1, p, sv), sv


def _mixers_ab_fwd(x, p, tabs):
    sv = {"x": x}
    hn = rmsnorm_fwd("attn_norm", x, p["attn_norm"], x.shape[1], BF16)
    proj = matmul("in_proj", hn, p["w_in_t"], "nt", F32, tm=256)
    qa, ka, av, qb, kb, cqn, ckvn = mixer_prep("mixer_prep", proj, p, tabs)
    sv.update(hn=hn, proj=proj, qa=qa, ka=ka, av=av, qb=qb, kb=kb, cqn=cqn, ckvn=ckvn)

    ya, lse_a = attn_fwd("a_attn", [(qa, 0, ka, A_HEADS // A_KV)], av, A_HEADS // A_KV, A_HEADS, HEAD ** -0.5)
    yb, lse_b = dilated_fwd("b_dilated", qb, kb, proj, O_BV // HEAD)
    sv.update(ya=ya, lse_a=lse_a, yb=yb, lse_b=lse_b)
    return sv


def _mixer_c_out_fwd(x, sv, p, tabs):
    _mixer_c_fwd(sv, p, tabs)
    return _out_fwd(x, sv, p)


def _mixer_c_fwd(sv, p, tabs):
    qc_raw = matmul("c_uq", sv["cqn"], p["w_uq_t"], "nt", F32)
    qc = jnp.concatenate([qc_raw[:, :C_W].astype(BF16), rope("c_q_rope", qc_raw[:, C_W:], tabs["c"][0], BF16)], axis=1)
    kvc = matmul("c_ukv", sv["ckvn"], p["w_ukv"], "nn", BF16)
    ckr = jnp.pad(sv["proj"][:, O_CKR:], ((0, 0), (0, HEAD - C_ROPE)))
    krc = rope("c_k_rope", ckr, tabs["c"][0], BF16)
    c_scale = (HEAD + C_ROPE) ** -0.5
    yc, lse_c = attn_fwd_c("c_attn", qc, kvc, krc, c_scale)
    sv.update(qc=qc, kvc=kvc, krc=krc, yc=yc, lse_c=lse_c)


def _out_fwd(x, sv, p, after=0.0):
    y = out_norms("out_norms", sv["ya"], sv["yb"], sv["yc"], p["out_norm"] + after)
    x1 = matmul("out_proj", y, p["w_out"], "nn", F32, add=x)
    sv.update(y=y, x1=x1)
    return x1


def _ffn_fwd(x1, p, sv):
    _ffn_up_fwd(x1, p, sv)
    return _ffn_down_fwd(x1, p, sv)


def _ffn_up_fwd(x1, p, sv):
    xn = rmsnorm_fwd("ffn_norm", x1, p["ffn_norm"], x1.shape[1], BF16)
    h, act = ffn_up("ffn_up", xn, p["w_up"], p["cwb"])
    sv.update(xn=xn, h=h, act=act)


def _ffn_down_fwd(x1, p, sv):
    return matmul("ffn_down", sv["act"], p["w_down"], "nn", F32, add=x1)


def _layer_bwd(dx2, sv, p, tabs):
    dxn, g = _ffn_bwd(dx2, sv, p)
    dx1, dy, g_out = _out_proj_bwd(dxn, dx2, sv, p)
    dhn, g_mix = _mixers_bwd(dy, sv, p, tabs)
    dx, g_norm = _attn_norm_bwd(dhn, dx1, sv, p)
    return dx, {**g, **g_out, **g_mix, **g_norm}


def _ffn_bwd(dx2, sv, p, after=0.0):
    g = {}
    dx2b = (dx2 + after).astype(BF16)
    dact = matmul("ffn_down_dx", dx2b, p["w_down"], "nt", F32)
    g["w_down"] = matmul("ffn_down_dw", sv["act"], dx2b, "tn", F32)
    dh, g["cwb"] = ffn_gate_bwd("ffn_gate_bwd", sv["h"], dact, p["cwb"])
    dxn = ffn_dx("ffn_up_dx", dh, p["w_up"])
    g["w_up"] = ffn_dw_up("ffn_up_dw", sv["xn"], dh)
    return dxn, g


def _out_proj_bwd(dxn, dx2, sv, p, after=0.0):
    g = {}
    dm = dx2.shape[1]
    dx1, dx1b, g["ffn_norm"] = rmsnorm_bwd("ffn_norm_bwd", dxn, sv["x1"], p["ffn_norm"] + after, dm, (F32, BF16), add=dx2)
    dy = matmul("out_proj_dx", dx1b, p["w_out"], "nt", F32)
    g["w_out"] = matmul("out_proj_dw", sv["y"], dx1b, "tn", F32)
    return dx1, dy, g


def _attn_norm_bwd(dhn, dx1, sv, p, after=0.0):
    dx, d_gain = rmsnorm_bwd("attn_norm_bwd", dhn, sv["x"], p["attn_norm"] + after, dhn.shape[1], (F32,), add=dx1)
    return dx, {"attn_norm": d_gain}


def _mixers_bwd(dy, sv, p, tabs):
    part, g = _mixer_c_bwd(dy, sv, p, tabs)
    dhn, g_ab = _mixers_ab_bwd(part, sv, p, tabs)
    return dhn, {**g, **g_ab}


def _mixer_c_bwd(dy, sv, p, tabs, after=0.0):
    g = {}
    dya, dyb, dyc, g["out_norm"] = out_norms_bwd("out_norms_bwd", dy, sv["ya"], sv["yb"], sv["yc"], p["out_norm"] + after)

    c_scale = (HEAD + C_ROPE) ** -0.5
    dqn, dqr, dkv, dkr = attn_bwd_c("c_attn_bwd", sv["qc"], sv["kvc"], sv["krc"], sv["lse_c"], sv["yc"], dyc, c_scale)
    dqc = jnp.concatenate([dqn.astype(BF16), rope("c_q_rope_bwd", dqr, tabs["c"][1], BF16)], axis=1)
    dcqn = matmul("c_uq_dx", dqc, p["w_uq_t"], "nn", F32)
    g["w_uq"] = matmul("c_uq_dw", dqc, sv["cqn"], "tn", F32)
    dkvc = dkv.astype(BF16)
    dckvn = matmul("c_ukv_dx", dkvc, p["w_ukv"], "nt", F32)
    g["w_ukv"] = matmul("c_ukv_dw", sv["ckvn"], dkvc, "tn", F32)
    dckr = rope("c_k_rope_bwd", dkr, tabs["c"][1], BF16)
    return (dya, dyb, dcqn, dckvn, dckr), g


def _mixers_ab_bwd(part, sv, p, tabs, after=0.0):
    g = {}
    dya, dyb, dcqn, dckvn, dckr = part

    dqb, dkb, dvb = dilated_bwd("b_dilated_bwd", sv["qb"], sv["kb"], sv["proj"], O_BV // HEAD, dyb + after, sv["yb"],
                                sv["lse_b"])

    kg = A_HEADS // A_KV
    (dqa,), (dka,), dva = attn_bwd("a_attn_bwd", [(sv["qa"], 0, sv["ka"], kg)], sv["av"], kg, sv["lse_a"], sv["ya"],
                                   dya, A_HEADS, HEAD ** -0.5)

    dproj, g["a_q_norm"], g["a_k_norm"], g["c_q_norm"], g["c_kv_norm"] = mixer_prep_bwd(
        "mixer_prep_bwd", sv["proj"], p, tabs, dqa, dka, dva, dqb, dkb, dvb, dcqn, dckvn, dckr)
    dhn = matmul("in_proj_dx", dproj, p["w_in_t"], "nn", F32, tm=512)
    g["w_in"] = matmul("in_proj_dw", dproj, sv["hn"], "tn", F32, tn=512, tk=512)
    return dhn, g


def _local_step(x, target, layers, final_norm, tabs):
    saved = []
    for p in layers:
        x, sv = _layer_fwd(x, p, tabs)
        saved.append(sv)
    dm = x.shape[1]
    yf = rmsnorm_fwd("final_norm", x, final_norm, dm, F32)
    loss, dyf = loss_head("loss_head", yf, target)
    dx, d_final = rmsnorm_bwd("final_norm_bwd", dyf, x, final_norm, dm, (F32,))
    grads = [None] * len(layers)
    for l in reversed(range(len(layers))):
        dx, grads[l] = _layer_bwd(dx, saved[l], layers[l], tabs)
    return loss, dx, grads, d_final


HBM = pl.BlockSpec(memory_space=pl.ANY)


def _place():
    x, y, c = lax.axis_index("x"), lax.axis_index("y"), lax.axis_index("c")
    others = [(1 - x, y), (x, 1 - y), (1 - x, 1 - y)]
    return x, y, c, 2 * x + y, others


def _remote(src, dst, send_sem, recv_sem, device):
    return pltpu.make_async_remote_copy(src_ref=src, dst_ref=dst, send_sem=send_sem, recv_sem=recv_sem,
                                        device_id=device, device_id_type=MESH)


def gather_small(name, small):
    def body(in_ref, out_ref, send_sems, recv_sems, local_sem):
        x, y, c, chip, others = _place()
        mine = pltpu.make_async_copy(in_ref, out_ref.at[chip], local_sem)
        mine.start()
        copies = []
        for k, (ox, oy) in enumerate(others):
            cp = _remote(in_ref, out_ref.at[chip], send_sems.at[k], recv_sems.at[k], (ox, oy, c))
            cp.start()
            copies.append(cp)
        for k, (ox, oy) in enumerate(others):
            landed = out_ref.at[2 * ox + oy]
            _remote(landed, landed, send_sems.at[k], recv_sems.at[k], (ox, oy, c)).wait_recv()
        for cp in copies:
            cp.wait_send()
        mine.wait()

    return pl.pallas_call(
        body, name=name, in_specs=[HBM], out_specs=HBM, out_shape=jax.ShapeDtypeStruct((4,) + small.shape, small.dtype),
        scratch_shapes=[pltpu.SemaphoreType.DMA((3,)), pltpu.SemaphoreType.DMA((3,)), pltpu.SemaphoreType.DMA],
        compiler_params=pltpu.CompilerParams(has_side_effects=True))(small)


IN_HBM = pl.BlockSpec(memory_space=pltpu.HBM)
SEMS = pl.BlockSpec(memory_space=pltpu.SEMAPHORE)
DATAFLOW = pltpu.SideEffectType.DATAFLOW_SIDE_EFFECTING


BF16_ROWS_PER_TILE = 16


def _halves_by_rows(shape):
    return (shape[-2] // 2) % BF16_ROWS_PER_TILE == 0


def _half(shape, core):
    if _halves_by_rows(shape):
        size = shape[-2] // 2
        return (pl.ds(core * size, size), slice(None))
    size = shape[-1] // 2
    return (slice(None), pl.ds(core * size, size))


def _half_shape(shape):
    r, c = shape[-2:]
    return (r // 2, c) if _halves_by_rows(shape) else (r, c // 2)


def _gather_plan(bufs):
    x, y, c, chip, others = _place()
    plan = []
    for t, ref in enumerate(bufs):
        mine = _half(ref.shape, c)
        for k, (ox, oy) in enumerate(others):
            plan.append((3 * t + k, (ox, oy, c), ref.at[(chip,) + mine], ref.at[(chip,) + mine],
                         ref.at[(2 * ox + oy,) + mine]))
    return plan


def _exchange_plan(bufs):
    x, y, c, chip, others = _place()
    n = len(bufs) // 2
    plan = []
    for t in range(n):
        for k, (ox, oy) in enumerate(others):
            plan.append((3 * t + k, (ox, oy, c), bufs[t].at[2 * ox + oy], bufs[n + t].at[chip], bufs[n + t].at[2 * ox + oy]))
    return plan


def _sibling_plan(bufs):
    x, y, c, chip, others = _place()
    n = len(bufs) // 2
    plan = []
    for t in range(n):
        plan.append((t, (x, y, 1 - c), bufs[t].at[(slice(None),) + _half(bufs[t].shape, 1 - c)], bufs[n + t], bufs[n + t]))
    return plan


def _forward_plan(bufs):
    x, y, c, chip, others = _place()
    plan = []
    for t, ref in enumerate(bufs):
        mine, theirs = _half(ref.shape, c), _half(ref.shape, 1 - c)
        for k, (ox, oy) in enumerate(others):
            slot = 2 * ox + oy
            plan.append((3 * t + k, (x, y, 1 - c), ref.at[(slot,) + mine], ref.at[(slot,) + mine], ref.at[(slot,) + theirs]))
    return plan


def _join_plan(bufs):
    x, y, c, chip, others = _place()
    return [(t, (x, y, 1 - c), ref.at[_half(ref.shape, c)], ref.at[_half(ref.shape, c)], ref.at[_half(ref.shape, 1 - c)])
            for t, ref in enumerate(bufs)]


PLAN_COPIES = {_gather_plan: lambda n: 3 * n, _exchange_plan: lambda n: 3 * (n // 2), _sibling_plan: lambda n: n // 2,
               _forward_plan: lambda n: 3 * n, _join_plan: lambda n: n}


def split_start(name, bufs, plan_of, after):
    n = len(bufs)

    def body(*refs):
        ins = refs[:n]
        send_sems, recv_sems = refs[n + 1], refs[n + 2]
        token = refs[-1]
        for idx, peer, src, dst, _ in plan_of(ins):
            _remote(src, dst, send_sems.at[idx], recv_sems.at[idx], peer).start()
        token[...] = jnp.zeros_like(token)

    copies = PLAN_COPIES[plan_of](n)
    res = pl.pallas_call(
        body, name=name, in_specs=[IN_HBM] * n + [HBM],
        out_specs=(SEMS, SEMS) + (IN_HBM,) * n + (pl.BlockSpec(memory_space=pltpu.VMEM),),
        out_shape=(pltpu.SemaphoreType.DMA((copies,)), pltpu.SemaphoreType.DMA((copies,)))
        + tuple(pltpu.HBM(b.shape, b.dtype) for b in bufs) + (jax.ShapeDtypeStruct((8, LANES), F32),),
        input_output_aliases={t: 2 + t for t in range(n)},
        compiler_params=pltpu.CompilerParams(has_side_effects=DATAFLOW))(
            *[pltpu.with_memory_space_constraint(b, pltpu.HBM) for b in bufs], after)
    return res[0], res[1], list(res[2:2 + n]), res[-1]


def split_wait(name, send_sems, recv_sems, flying, plan_of, after):
    n = len(flying)

    def body(*refs):
        ins = refs[:n]
        send_ref, recv_ref = refs[n], refs[n + 1]
        for idx, peer, src, dst, landing in plan_of(ins):
            _remote(src, dst, send_ref.at[idx], recv_ref.at[idx], peer).wait_send()
            _remote(landing, landing, send_ref.at[idx], recv_ref.at[idx], peer).wait_recv()

    return list(pl.pallas_call(
        body, name=name, in_specs=[IN_HBM] * n + [SEMS, SEMS] + [HBM] * len(after), out_specs=(IN_HBM,) * n,
        out_shape=tuple(pltpu.HBM(b.shape, b.dtype) for b in flying),
        input_output_aliases={t: t for t in range(n)},
        compiler_params=pltpu.CompilerParams(has_side_effects=DATAFLOW))(*flying, send_sems, recv_sems, *after))


def _half_block(shape, tr):
    r, c = shape[-2:]
    if _halves_by_rows(shape):
        per = r // 2 // tr
        return tr, c, lambda i, core: (core * per + i, 0)
    return tr, c // 2, lambda i, core: (i, core)


def add_half(name, place, grad, theirs):
    hr, hc = _half_shape(grad.shape)
    tr = _rows(hr, hc)
    br, bc, at = _half_block(grad.shape, tr)

    def body(place_ref, g_ref, t_ref, o_ref):
        o_ref[...] = (g_ref[...] + t_ref[...]).astype(o_ref.dtype)

    whole = pl.BlockSpec((None, br, bc), lambda j, i, pr: (j, i, 0))
    return pl.pallas_call(
        body, name=name,
        grid_spec=pltpu.PrefetchScalarGridSpec(
            num_scalar_prefetch=1, grid=(4, hr // tr),
            in_specs=[pl.BlockSpec((None, br, bc), lambda j, i, pr: (j,) + at(i, pr[0])), whole],
            out_specs=whole),
        out_shape=jax.ShapeDtypeStruct((4, hr, hc), BF16),
        compiler_params=_params("parallel", "parallel"))(place, grad, theirs)


def sum_chips(name, place, own, parts, shard_shape):
    _, hr, hc = parts.shape
    tr = _rows(hr, hc)
    br, bc, at = _half_block(shard_shape, tr)

    def body(place_ref, own_ref, p1, p2, p3, o_ref):
        o_ref[...] = own_ref[...].astype(F32) + p1[...].astype(F32) + p2[...].astype(F32) + p3[...].astype(F32)

    def slot(k):
        return pl.BlockSpec((None, br, bc), lambda i, pr: (lax.rem(pr[1] + k, 4), i, 0))

    return pl.pallas_call(
        body, name=name,
        grid_spec=pltpu.PrefetchScalarGridSpec(
            num_scalar_prefetch=1, grid=(hr // tr,), in_specs=[slot(0), slot(1), slot(2), slot(3)],
            out_specs=pl.BlockSpec((br, bc), lambda i, pr: at(i, pr[0]))),
        out_shape=jax.ShapeDtypeStruct(tuple(shard_shape), F32),
        compiler_params=_params("parallel"))(place, own, parts, parts, parts)


def allreduce_small(name, buf):
    rows = buf.shape[0]

    def body(in_ref, out_ref, slots, send_sems, recv_sems):
        x, y, c, _, _ = _place()
        me = 4 * x + 2 * y + c
        slots[me] = in_ref[...]
        peers = []
        for k in range(1, 8):
            px = 1 - x if k & 4 else x
            py = 1 - y if k & 2 else y
            pc = 1 - c if k & 1 else c
            peers.append((px, py, pc))
        copies = []
        for k, peer in enumerate(peers):
            cp = _remote(in_ref, slots.at[me], send_sems.at[k], recv_sems.at[k], peer)
            cp.start()
            copies.append(cp)
        for k, (px, py, pc) in enumerate(peers):
            slot = slots.at[4 * px + 2 * py + pc]
            _remote(slot, slot, send_sems.at[k], recv_sems.at[k], (px, py, pc)).wait_recv()
        for cp in copies:
            cp.wait_send()
        acc = slots[0]
        for d in range(1, 8):
            acc = acc + slots[d]
        out_ref[...] = acc

    vm = pl.BlockSpec(memory_space=pltpu.VMEM)
    return pl.pallas_call(
        body, name=name, in_specs=[vm], out_specs=vm, out_shape=jax.ShapeDtypeStruct(buf.shape, F32),
        scratch_shapes=[pltpu.VMEM((8, rows, LANES), F32), pltpu.SemaphoreType.DMA((7,)), pltpu.SemaphoreType.DMA((7,))],
        compiler_params=pltpu.CompilerParams(has_side_effects=True, vmem_limit_bytes=VMEM_LIMIT))(buf)


def cast_layer(name, place, w, layer):
    _, r, cols = w.shape
    tr = _rows(r, cols)

    def body(place_ref, w_ref, o_ref):
        o_ref[...] = w_ref[...].astype(BF16)

    return pl.pallas_call(
        body, name=name,
        grid_spec=pltpu.PrefetchScalarGridSpec(
            num_scalar_prefetch=1, grid=(r // tr,),
            in_specs=[pl.BlockSpec((None, tr, cols), lambda i, pr: (layer, i, 0))],
            out_specs=pl.BlockSpec((None, tr, cols), lambda i, pr: (pr[1], i, 0))),
        out_shape=jax.ShapeDtypeStruct((4, r, cols), BF16),
        compiler_params=_params("parallel"))(place, w)


def _adamw_math(g, w, m, v):
    m = ADAM_B1 * m + (1.0 - ADAM_B1) * g
    v = ADAM_B2 * v + (1.0 - ADAM_B2) * (g * g)
    m_hat = m / (1.0 - ADAM_B1 ** ADAM_STEP)
    v_hat = v / (1.0 - ADAM_B2 ** ADAM_STEP)
    delta = -ADAM_LR * (m_hat / (jnp.sqrt(v_hat) + ADAM_EPS) + ADAM_WD * w)
    return delta, m, v


def adamw_layer(name, layer, g, w, m, v, prev=None):
    nl, r, cols = w.shape
    tr = _rows(r, cols, 8)

    def body(*refs):
        g_ref, w_ref, m_ref, v_ref = refs[:4]
        og, od, om, ov = refs[-4:]
        gv = g_ref[...]
        delta, m2, v2 = _adamw_math(gv, w_ref[...], m_ref[...], v_ref[...])
        og[...] = gv
        od[...] = delta
        om[...] = m2
        ov[...] = v2

    lay = pl.BlockSpec((None, tr, cols), lambda i: (layer, i, 0))
    ins = [g, w, m, v] + (list(prev) if prev is not None else [])
    return pl.pallas_call(
        body, name=name, grid=(r // tr,),
        in_specs=[pl.BlockSpec((tr, cols), lambda i: (i, 0)), lay, lay, lay] + ([HBM] * 4 if prev is not None else []),
        out_specs=[lay] * 4, out_shape=[jax.ShapeDtypeStruct((nl, r, cols), F32)] * 4,
        input_output_aliases={4 + k: k for k in range(4)} if prev is not None else {},
        compiler_params=_params("parallel"))(*ins)


def adamw_packed(name, g, w, m, v):
    def body(g_ref, w_ref, m_ref, v_ref, od, om, ov):
        delta, m2, v2 = _adamw_math(g_ref[...], w_ref[...], m_ref[...], v_ref[...])
        od[...] = delta
        om[...] = m2
        ov[...] = v2

    vm = pl.BlockSpec(memory_space=pltpu.VMEM)
    return pl.pallas_call(
        body, name=name, in_specs=[vm] * 4, out_specs=[vm] * 3, out_shape=[jax.ShapeDtypeStruct(g.shape, F32)] * 3,
        compiler_params=pltpu.CompilerParams(vmem_limit_bytes=VMEM_LIMIT))(g, w, m, v)


def _pack(arrays):
    flat = jnp.concatenate([a.reshape(-1) for a in arrays])
    pad = (-flat.shape[0]) % (8 * LANES)
    return jnp.pad(flat, (0, pad)).reshape(-1, LANES)


def _unpack(buf, shapes):
    flat = buf.reshape(-1)
    out, off = [], 0
    for shp in shapes:
        size = int(np.prod(shp))
        out.append(flat[off:off + size].reshape(shp))
        off += size
    return out


MIXER_W = ("w_in", "w_uq", "w_ukv", "w_out")
FFN_W = ("w_up", "w_down")
BIG = MIXER_W + FFN_W
TRANSPOSED = ("w_in", "w_uq")
COLUMN_CUT = ("w_ukv", "w_up")
SMALL = ("attn_norm", "a_q_norm", "a_k_norm", "c_q_norm", "c_kv_norm", "out_norm", "ffn_norm", "conv_b", "final_norm")
WEIGHTS = ("attn_norm", "w_in", "a_q_norm", "a_k_norm", "c_q_norm", "c_kv_norm", "w_uq", "w_ukv", "out_norm", "w_out",
           "ffn_norm", "w_up", "conv_w", "conv_b", "w_down", "final_norm")
INPUTS = ("x",) + WEIGHTS + ("loss_target",) + tuple("m_" + n for n in WEIGHTS) + tuple("v_" + n for n in WEIGHTS)


def _columns(g):
    return jnp.transpose(g, (1, 0, 2)).reshape(g.shape[1], 4 * g.shape[2])


def _uncolumns(w):
    r, c4 = w.shape
    return jnp.transpose(w.reshape(r, 4, c4 // 4), (1, 0, 2))


def _compute_layouts(full, conv_w, small, layer):
    p = {n: small[n][layer] for n in SMALL if n != "final_norm"}
    if "w_in" in full:
        p["w_in_t"] = full["w_in"].reshape(-1, full["w_in"].shape[2])
    if "w_uq" in full:
        uq_t = full["w_uq"].reshape(C_HEADS, HEAD + C_ROPE, C_RANK)
        p["w_uq_t"] = jnp.concatenate([uq_t[:, :HEAD].reshape(C_W, C_RANK),
                                       jnp.pad(uq_t[:, HEAD:], ((0, 0), (0, HEAD - C_ROPE), (0, 0))).reshape(C_W, C_RANK)], axis=0)
        p["w_ukv"] = _columns(full["w_ukv"])
        p["w_out"] = full["w_out"].reshape(-1, full["w_out"].shape[2])
    if "w_up" in full:
        up = full["w_up"]
        p["w_up"] = up.reshape(2, 2, up.shape[1], up.shape[2])
        p["w_down"] = full["w_down"].reshape(-1, full["w_down"].shape[2])
        fh = conv_w.shape[1] // 2
        taps = jnp.transpose(conv_w.reshape(3, 2, fh), (1, 0, 2))
        p["cwb"] = jnp.concatenate([taps, small["conv_b"][layer].reshape(2, 1, fh), jnp.zeros((2, 4, fh), F32)], axis=1)
    return p


def _shard_layout(name, g):
    if name == "w_in":
        return g.reshape(4, -1, g.shape[1])
    if name == "w_uq":
        uq_t = jnp.concatenate([g[:C_W].reshape(C_HEADS, HEAD, C_RANK),
                                g[C_W:].reshape(C_HEADS, HEAD, C_RANK)[:, :C_ROPE]], axis=1)
        return uq_t.reshape(4, -1, C_RANK)
    if name == "w_ukv":
        return _uncolumns(g)
    if name == "w_up":
        return g.reshape(4, g.shape[2], g.shape[3])
    return g.reshape(4, -1, g.shape[1])


def _conv_grads(dcwb):
    fh = dcwb.shape[2]
    return jnp.transpose(dcwb[:, 0:3, :], (1, 0, 2)).reshape(3, 2 * fh), dcwb[:, 3, :].reshape(2 * fh)


def _shard_layouts(g):
    return ({n: _shard_layout(n, g[n]) for n in BIG},) + _conv_grads(g["cwb"])


def kernel(x, attn_norm, w_in, a_q_norm, a_k_norm, c_q_norm, c_kv_norm, w_uq, w_ukv, out_norm, w_out, ffn_norm, w_up, conv_w, conv_b, w_down, final_norm, loss_target, m_attn_norm, m_w_in, m_a_q_norm, m_a_k_norm, m_c_q_norm, m_c_kv_norm, m_w_uq, m_w_ukv, m_out_norm, m_w_out, m_ffn_norm, m_w_up, m_conv_w, m_conv_b, m_w_down, m_final_norm, v_attn_norm, v_w_in, v_a_q_norm, v_a_k_norm, v_c_q_norm, v_c_kv_norm, v_w_uq, v_w_ukv, v_out_norm, v_w_out, v_ffn_norm, v_w_up, v_conv_w, v_conv_b, v_w_down, v_final_norm):
    a = dict(zip(INPUTS, (x, attn_norm, w_in, a_q_norm, a_k_norm, c_q_norm, c_kv_norm, w_uq, w_ukv, out_norm, w_out, ffn_norm, w_up, conv_w, conv_b, w_down, final_norm, loss_target, m_attn_norm, m_w_in, m_a_q_norm, m_a_k_norm, m_c_q_norm, m_c_kv_norm, m_w_uq, m_w_ukv, m_out_norm, m_w_out, m_ffn_norm, m_w_up, m_conv_w, m_conv_b, m_w_down, m_final_norm, v_attn_norm, v_w_in, v_a_q_norm, v_a_k_norm, v_c_q_norm, v_c_kv_norm, v_w_uq, v_w_ukv, v_out_norm, v_w_out, v_ffn_norm, v_w_up, v_conv_w, v_conv_b, v_w_down, v_final_norm)))
    nl = w_in.shape[0]
    seq = x.shape[1]
    for n in TRANSPOSED:
        for kind in ("", "m_", "v_"):
            a[kind + n] = jnp.swapaxes(a[kind + n], 1, 2)
    chip = 2 * lax.axis_index("x") + lax.axis_index("y")
    place = jnp.stack([lax.axis_index("c"), chip]).astype(jnp.int32)
    tabs = _rope_tables(seq)

    assert nl == 2
    quarter = conv_w.shape[2]
    taps = jnp.pad(conv_w, ((0, 0), (0, 8 - conv_w.shape[1]), (0, 0))).reshape(nl * 8, quarter)
    taps_full = gather_small("gather_taps", taps).reshape(4, nl, 8, quarter)
    conv_full = [jnp.transpose(taps_full[:, l, 0:3, :], (1, 0, 2)).reshape(3, 4 * quarter) for l in range(nl)]
    flights = {}
    here, after = place, taps_full
    for l in range(nl):
        for key, names in ((f"in{l}", ("w_in",)), (f"mix{l}", MIXER_W[1:]), (f"ffn{l}", FFN_W)):
            slots = [cast_layer(f"cast{l}_{n}", here, a[n], l) for n in names]
            send_sems, recv_sems, flying, token = split_start(f"gather_{key}_start", slots, _gather_plan, after)
            flights[key] = (names, send_sems, recv_sems, flying)
            here = after = place + token[0, 0].astype(jnp.int32)
    started = token[0, 0]

    def land(key, after):
        names, send_sems, recv_sems, flying = flights[key]
        landed = split_wait(f"gather_{key}_wait", send_sems, recv_sems, flying, _gather_plan, after)
        send_sems, recv_sems, passing, token = split_start(f"gather_{key}_pass_start", landed, _forward_plan, place)
        return (key, names, send_sems, recv_sems, passing), token

    def landed(passing, after, layer):
        key, names, send_sems, recv_sems, flying = passing
        full = split_wait(f"gather_{key}_pass_wait", send_sems, recv_sems, flying, _forward_plan, after)
        return _compute_layouts(dict(zip(names, full)), conv_full[layer], a, layer)

    h = x[0]
    layers, saved = [], []
    early = None
    for l in range(nl):
        if early is None:
            pass_in, token = land(f"in{l}", [taps_full])
            p = landed(pass_in, [token], l)
            p["attn_norm"] = p["attn_norm"] + started
        else:
            p = landed(early[0], [h], l)
        sv = _mixers_ab_fwd(h, p, tabs)
        if early is None:
            pass_mix, token = land(f"mix{l}", [sv["yb"]])
            p.update(landed(pass_mix, [token], l))
        else:
            p.update(landed(early[1], [sv["yb"]], l))
        _mixer_c_fwd(sv, p, tabs)
        pass_ffn, token = land(f"ffn{l}", [sv["yc"]])
        x1 = _out_fwd(h, sv, p, token[0, 0])
        p.update(landed(pass_ffn, [x1], l))
        _ffn_up_fwd(x1, p, sv)
        if l + 1 < nl:
            pass_in, _ = land(f"in{l + 1}", [sv["act"]])
            pass_mix, _ = land(f"mix{l + 1}", [sv["act"]])
            early = (pass_in, pass_mix)
        h = _ffn_down_fwd(x1, p, sv)
        layers.append(p)
        saved.append(sv)
    dm = h.shape[1]
    yf = rmsnorm_fwd("final_norm", h, final_norm, dm, F32)
    loss, dyf = loss_head("loss_head", yf, loss_target[0])
    loss = lax.psum(loss, ("x", "y", "c"))
    dx, d_final = rmsnorm_bwd("final_norm_bwd", dyf, h, final_norm, dm, (F32,))

    def swap_begin(tag, names, g):
        g_list = [_shard_layout(n, g[n]) for n in names]
        theirs = [lax.empty((4,) + _half_shape(s.shape), s.dtype) for s in g_list]
        send_sems, recv_sems, flying, token = split_start(f"{tag}_sibling_start", g_list + theirs, _sibling_plan, place)
        return (names, send_sems, recv_sems, flying), token[0, 0]

    def exchange_begin(tag, swap, after):
        names, send_sems, recv_sems, flying = swap
        landed = split_wait(f"{tag}_sibling_wait", send_sems, recv_sems, flying, _sibling_plan, after)
        g_list, theirs = landed[:len(names)], landed[len(names):]
        sums = [add_half(f"{tag}_add_{n}", place, gr, t) for n, gr, t in zip(names, g_list, theirs)]
        parts = [lax.empty(s.shape, s.dtype) for s in sums]
        send_sems, recv_sems, flying, token = split_start(f"{tag}_chips_start", sums + parts, _exchange_plan, place)
        return (names, [gr.shape[1:] for gr in g_list], send_sems, recv_sems, flying), token

    def exchange_end(tag, flight, after):
        names, shard_shapes, send_sems, recv_sems, flying = flight
        landed = split_wait(f"{tag}_chips_wait", send_sems, recv_sems, flying, _exchange_plan, after)
        sums, parts = landed[:len(names)], landed[len(names):]
        halves = [sum_chips(f"{tag}_sum_{n}", place, s, p, shp) for n, s, p, shp in zip(names, sums, parts, shard_shapes)]
        send_sems, recv_sems, joining, token = split_start(f"{tag}_join_start", halves, _join_plan, place)
        return (names, send_sems, recv_sems, joining), token

    def joined(tag, joining, after):
        names, send_sems, recv_sems, flying = joining
        return dict(zip(names, split_wait(f"{tag}_join_wait", send_sems, recv_sems, flying, _join_plan, after)))

    grads, conv_grads, flights_f, flights_m = [None] * nl, [None] * nl, [None] * nl, [None] * nl
    after = 0.0
    for l in reversed(range(nl)):
        sv, p = saved[l], layers[l]
        dxn, g = _ffn_bwd(dx, sv, p, after)
        swap_f, after = swap_begin(f"reduce{l}f", FFN_W, g)
        dx1, dy, g_out = _out_proj_bwd(dxn, dx, sv, p, after)
        part, g_c = _mixer_c_bwd(dy, sv, p, tabs)
        flights_f[l], token = exchange_begin(f"reduce{l}f", swap_f, [part[2]])
        dhn, g_ab = _mixers_ab_bwd(part, sv, p, tabs, token[0, 0])
        g.update(g_out)
        g.update(g_c)
        g.update(g_ab)
        swap_m, after = swap_begin(f"reduce{l}m", MIXER_W, g)
        dx, g_norm = _attn_norm_bwd(dhn, dx1, sv, p, after)
        g.update(g_norm)
        grads[l], conv_grads[l] = g, _conv_grads(g["cwb"])
        before = [dx]
        if l == 0:
            small_g = {n: jnp.stack([grads[k][n] for k in range(nl)]) for n in SMALL if n not in ("conv_b", "final_norm")}
            small_g["conv_b"] = jnp.stack([cg[1] for cg in conv_grads])
            small_g["final_norm"] = d_final
            conv_w_g = jnp.stack([cg[0] for cg in conv_grads])
            small_sum = allreduce_small("reduce_small", _pack([small_g[n] for n in SMALL] + [conv_w_g]))
            before.append(small_sum)
        flights_m[l], token = exchange_begin(f"reduce{l}m", swap_m, before)
        after = token[0, 0]
    grad_x = dx

    out = {}
    join_f, token = exchange_end("reduce1f", flights_f[1], [token])
    join_m, token = exchange_end("reduce1m", flights_m[1], [token])
    reduced1 = joined("reduce1f", join_f, [token])
    for n in FFN_W:
        out[n] = adamw_layer(f"adamw1_{n}", 1, reduced1[n], a[n], a["m_" + n], a["v_" + n])
    reduced1 = joined("reduce1m", join_m, [out[n][3] for n in FFN_W])
    for n in MIXER_W:
        out[n] = adamw_layer(f"adamw1_{n}", 1, reduced1[n], a[n], a["m_" + n], a["v_" + n])

    summed = _unpack(small_sum, [a[n].shape for n in SMALL] + [conv_w_g.shape])
    small_g = dict(zip(SMALL, summed[:-1]))
    small_g["conv_w"] = lax.dynamic_slice_in_dim(summed[-1], chip * quarter, quarter, axis=2)
    names = SMALL + ("conv_w",)
    shapes = [a[n].shape for n in names]
    d_small, m_small, v_small = adamw_packed(
        "adamw_small", _pack([small_g[n] for n in names]), _pack([a[n] for n in names]),
        _pack([a["m_" + n] for n in names]), _pack([a["v_" + n] for n in names]))

    join_f, token = exchange_end("reduce0f", flights_f[0], [out[n][3] for n in BIG] + [d_small])
    join_m, token = exchange_end("reduce0m", flights_m[0], [token])
    reduced0 = joined("reduce0f", join_f, [token])
    for n in FFN_W:
        out[n] = adamw_layer(f"adamw0_{n}", 0, reduced0[n], a[n], a["m_" + n], a["v_" + n], prev=out[n])
    reduced0 = joined("reduce0m", join_m, [out[n][3] for n in FFN_W])
    for n in MIXER_W:
        out[n] = adamw_layer(f"adamw0_{n}", 0, reduced0[n], a[n], a["m_" + n], a["v_" + n], prev=out[n])
    for n, d_, m_, v_ in zip(names, _unpack(d_small, shapes), _unpack(m_small, shapes), _unpack(v_small, shapes)):
        out[n] = (small_g[n], d_, m_, v_)
    for n in TRANSPOSED:
        out[n] = [jnp.swapaxes(o, 1, 2) for o in out[n]]

    return (loss, grad_x[None]) + tuple(out[n][k] for k in range(4) for n in WEIGHTS)
```

```python
import functools

import jax
import jax.numpy as jnp
import numpy as np
from jax import lax
from jax.experimental import pallas as pl
from jax.experimental.pallas import tpu as pltpu

F32 = jnp.float32
BF16 = jnp.bfloat16
MESH = pl.DeviceIdType.MESH

V7X_VMEM_BYTES = 64 * 1024 * 1024
VMEM_LIMIT = V7X_VMEM_BYTES - 6 * 1024 * 1024
LANES = 128
ELEMENTWISE_BLOCK_BYTES = 4 * 1024 * 1024

HEAD = 128
A_HEADS, A_KV = 6, 2
B_HEADS, B_GROUPS = 4, 3
B_DILATIONS = (1, 4, 16)
B_HALF = 64
C_HEADS, C_RANK, C_ROPE = 6, 512, 64
GRID_W = 64
A_THETA, B_THETA, C_THETA = 10000.0, 500000.0, 10000.0
B_ROPE_DIM = 32
EPS = 1e-6
A_W, B_W, C_W = A_HEADS * HEAD, B_HEADS * HEAD, C_HEADS * HEAD
O_AQ, O_AK, O_AV = 0, 768, 1024
O_BQ, O_BK, O_BV = 1280, 2816, 3328
O_CQ, O_CKV, O_CKR = 3840, 4352, 4864
IN_W = 4928
PROJ_W = 5120

ADAM_LR, ADAM_B1, ADAM_B2, ADAM_EPS, ADAM_WD, ADAM_STEP = 0.001, 0.9, 0.999, 1e-08, 0.01, 10

NN = (((1,), (0,)), ((), ()))
NT = (((1,), (1,)), ((), ()))
TN = (((0,), (0,)), ((), ()))


def _dot(a, b, dims=NN):
    return lax.dot_general(a.astype(BF16), b.astype(BF16), dims, preferred_element_type=F32)


def _params(*sem):
    return pltpu.CompilerParams(dimension_semantics=sem if sem else None, vmem_limit_bytes=VMEM_LIMIT)


def _tile(n, target, unit=LANES):
    if n <= target:
        return n
    best = 0
    for t in range(unit, target + 1, unit):
        if n % t == 0:
            best = t
    return best if best else n


def _rows(r, width, itemsize=4):
    return _tile(r, max(8, ELEMENTWISE_BLOCK_BYTES // (width * itemsize)), 8)


def _matmul_call(name, dims, grid, a_spec, b_spec, o_spec, out_shape, acc_shape, add_spec=None):
    nk = grid[2]

    def body(*refs):
        a_ref, b_ref = refs[:2]
        add_ref = None if add_spec is None else refs[2]
        o_ref = refs[2 if add_spec is None else 3]

        def finish(r):
            if add_ref is not None:
                r = r + add_ref[...]
            o_ref[...] = r.astype(o_ref.dtype)

        if nk == 1:
            finish(_dot(a_ref[...], b_ref[...], dims))
            return
        acc = refs[-1]
        k = pl.program_id(2)

        @pl.when(k == 0)
        def _():
            acc[...] = _dot(a_ref[...], b_ref[...], dims)

        if nk > 2:
            @pl.when((k > 0) & (k < nk - 1))
            def _():
                acc[...] += _dot(a_ref[...], b_ref[...], dims)

        @pl.when(k == nk - 1)
        def _():
            finish(acc[...] + _dot(a_ref[...], b_ref[...], dims))

    in_specs = [a_spec, b_spec] + ([] if add_spec is None else [add_spec])
    return pl.pallas_call(
        body, name=name, grid=grid, in_specs=in_specs, out_specs=o_spec, out_shape=out_shape,
        scratch_shapes=[pltpu.VMEM(acc_shape, F32)] if nk > 1 else [],
        compiler_params=_params("parallel", "parallel", "arbitrary"))


def matmul(name, a, b, mode, out_dtype, add=None, tm=1024, tn=1024, tk=2816):
    if mode == "nn":
        (m, k), (k2, n) = a.shape, b.shape
    elif mode == "nt":
        (m, k), (n, k2) = a.shape, b.shape
    else:
        (k, m), (k2, n) = a.shape, b.shape
    assert k == k2, (name, a.shape, b.shape)
    tm, tn, tk = _tile(m, tm), _tile(n, tn), _tile(k, tk)
    grid = (m // tm, n // tn, k // tk)
    if mode == "tn":
        a_spec = pl.BlockSpec((tk, tm), lambda i, j, kk: (kk, i))
    else:
        a_spec = pl.BlockSpec((tm, tk), lambda i, j, kk: (i, kk))
    whole_b = {"pipeline_mode": pl.Buffered(1)} if grid[1] == 1 and grid[2] == 1 and grid[0] > 1 else {}
    if mode == "nt":
        b_spec = pl.BlockSpec((tn, tk), lambda i, j, kk: (j, kk), **whole_b)
    else:
        b_spec = pl.BlockSpec((tk, tn), lambda i, j, kk: (kk, j), **whole_b)
    o_spec = pl.BlockSpec((tm, tn), lambda i, j, kk: (i, j))
    dims = {"nn": NN, "nt": NT, "tn": TN}[mode]
    call = _matmul_call(name, dims, grid, a_spec, b_spec, o_spec, jax.ShapeDtypeStruct((m, n), out_dtype),
                        (tm, tn), None if add is None else o_spec)
    return call(a, b) if add is None else call(a, b, add)


def rmsnorm_fwd(name, x, g, width, out_dtype):
    r, cols = x.shape
    nb = cols // width
    tr = _rows(r, width)

    def body(x_ref, g_ref, o_ref):
        xv = x_ref[...]
        rs = lax.rsqrt(jnp.mean(xv * xv, axis=-1, keepdims=True) + EPS)
        o_ref[...] = (xv * rs * g_ref[...]).astype(o_ref.dtype)

    blk = pl.BlockSpec((tr, width), lambda i, j: (i, j))
    return pl.pallas_call(
        body, name=name, grid=(r // tr, nb),
        in_specs=[blk, pl.BlockSpec((1, width), lambda i, j: (0, 0))], out_specs=blk,
        out_shape=jax.ShapeDtypeStruct((r, cols), out_dtype),
        compiler_params=_params("parallel", "parallel"))(x, g.reshape(1, width))


def rmsnorm_bwd(name, dy, x, g, width, out_dtypes=(F32,), add=None):
    r, cols = x.shape
    nb = cols // width
    tr = _rows(r, width)
    n_out = len(out_dtypes)

    def body(*refs):
        dy_ref, x_ref, g_ref = refs[:3]
        add_ref = refs[3] if add is not None else None
        outs = refs[-(n_out + 1):-1]
        dg_ref = refs[-1]
        xv = x_ref[...]
        dyv = dy_ref[...].astype(F32)
        rs = lax.rsqrt(jnp.mean(xv * xv, axis=-1, keepdims=True) + EPS)
        xh = xv * rs
        dyg = dyv * g_ref[...]
        dx = rs * (dyg - xh * jnp.mean(dyg * xh, axis=-1, keepdims=True))
        if add_ref is not None:
            dx = dx + add_ref[...]
        for o in outs:
            o[...] = dx.astype(o.dtype)

        @pl.when((pl.program_id(0) == 0) & (pl.program_id(1) == 0))
        def _():
            dg_ref[...] = jnp.zeros_like(dg_ref)

        dg_ref[...] += jnp.sum(dyv * xh, axis=0, keepdims=True)

    blk = pl.BlockSpec((tr, width), lambda i, j: (i, j))
    vec = pl.BlockSpec((1, width), lambda i, j: (0, 0))
    ins = [dy, x, g.reshape(1, width)] + ([add] if add is not None else [])
    res = pl.pallas_call(
        body, name=name, grid=(r // tr, nb),
        in_specs=[blk, blk, vec] + ([blk] if add is not None else []),
        out_specs=[blk] * n_out + [vec],
        out_shape=[jax.ShapeDtypeStruct((r, cols), dt) for dt in out_dtypes] + [jax.ShapeDtypeStruct((1, width), F32)],
        compiler_params=_params("arbitrary", "arbitrary"))(*ins)
    return tuple(res[:n_out]) + (res[n_out].reshape(width),)


def _rope_angles(pos, dim, theta):
    inv = theta ** (-jnp.arange(0, dim, 2, dtype=F32) / dim)
    ang = pos.astype(F32)[:, None] * inv[None, :]
    return jnp.cos(ang), jnp.sin(ang)


def _rope_tables(s):
    rows = s // GRID_W
    row_pos = jnp.repeat(jnp.arange(rows), GRID_W)
    col_pos = jnp.tile(jnp.arange(GRID_W), rows)
    t_pos = jnp.arange(s)
    z = lambda n: jnp.zeros((s, n), F32)
    o = lambda n: jnp.ones((s, n), F32)
    cr, sr = _rope_angles(row_pos, HEAD // 2, A_THETA)
    cc, sc = _rope_angles(col_pos, HEAD // 2, A_THETA)
    tab_a = (jnp.concatenate([cr, cr, cc, cc], 1), jnp.concatenate([-sr, z(32), -sc, z(32)], 1),
             jnp.concatenate([z(32), sr, z(32), sc], 1), 32)
    cp, sp = _rope_angles(t_pos, B_ROPE_DIM, B_THETA)
    tab_b = (jnp.concatenate([cp, cp, o(96)], 1), jnp.concatenate([-sp, z(112)], 1),
             jnp.concatenate([z(16), sp, z(96)], 1), 16)
    cm, sm = _rope_angles(t_pos, C_ROPE, C_THETA)
    tab_c = (jnp.concatenate([cm, cm, o(64)], 1), jnp.concatenate([-sm, z(96)], 1),
             jnp.concatenate([z(32), sm, z(64)], 1), 32)

    def transposed(tab):
        c, s1, s2, h = tab
        return (c, jnp.roll(s2, -h, axis=1), jnp.roll(s1, h, axis=1), h)

    return {k: (t, transposed(t)) for k, t in (("a", tab_a), ("b", tab_b), ("c", tab_c))}


def rope(name, x, tab, out_dtype):
    c, s1, s2, h = tab
    s, cols = x.shape
    ts = _rows(s, cols)

    def body(x_ref, c_ref, s1_ref, s2_ref, o_ref):
        cv, s1v, s2v = c_ref[...], s1_ref[...], s2_ref[...]
        for j in range(cols // HEAD):
            lanes = slice(HEAD * j, HEAD * (j + 1))
            xv = x_ref[:, lanes].astype(F32)
            o_ref[:, lanes] = (xv * cv + pltpu.roll(xv, HEAD - h, 1) * s1v + pltpu.roll(xv, h, 1) * s2v).astype(o_ref.dtype)

    blk = pl.BlockSpec((ts, cols), lambda i: (i, 0))
    tb = pl.BlockSpec((ts, HEAD), lambda i: (i, 0))
    return pl.pallas_call(
        body, name=name, grid=(s // ts,), in_specs=[blk, tb, tb, tb], out_specs=blk,
        out_shape=jax.ShapeDtypeStruct((s, cols), out_dtype),
        compiler_params=_params("parallel"))(x, c, s1, s2)


PREP_ROWS = 256


def _rope_of(x, tab, h):
    c, s1, s2 = tab
    return x * c + pltpu.roll(x, HEAD - h, 1) * s1 + pltpu.roll(x, h, 1) * s2


def _heads(start, n):
    return [(j, slice(start + HEAD * j, start + HEAD * (j + 1))) for j in range(n)]


def mixer_prep(name, proj, p, tabs):
    s = proj.shape[0]
    tr = PREP_ROWS
    (ca, s1a, s2a, ha), (cb, s1b, s2b, hb) = tabs["a"][0], tabs["b"][0]

    def body(proj_ref, gaq, gak, gcq, gckv, ca_r, s1a_r, s2a_r, cb_r, s1b_r, s2b_r,
             qa_ref, ka_ref, av_ref, qb_ref, kb_ref, cqn_ref, ckvn_ref):
        ta = (ca_r[...], s1a_r[...], s2a_r[...])
        tb = (cb_r[...], s1b_r[...], s2b_r[...])

        def normed(x, g):
            return x * lax.rsqrt(jnp.mean(x * x, axis=-1, keepdims=True) + EPS) * g

        for (src0, n, gain, out) in ((O_AQ, A_HEADS, gaq, qa_ref), (O_AK, A_KV, gak, ka_ref)):
            for j, cols in _heads(src0, n):
                out[:, HEAD * j:HEAD * (j + 1)] = _rope_of(normed(proj_ref[:, cols], gain[...]), ta, ha).astype(BF16)
        av_ref[...] = proj_ref[:, O_AV:O_BQ].astype(BF16)
        for (src0, n, out) in ((O_BQ, B_GROUPS * B_HEADS, qb_ref), (O_BK, B_HEADS, kb_ref)):
            for j, cols in _heads(src0, n):
                out[:, HEAD * j:HEAD * (j + 1)] = _rope_of(proj_ref[:, cols], tb, hb)
        cqn_ref[...] = normed(proj_ref[:, O_CQ:O_CKV], gcq[...]).astype(BF16)
        ckvn_ref[...] = normed(proj_ref[:, O_CKV:O_CKR], gckv[...]).astype(BF16)

    rows = lambda w: pl.BlockSpec((tr, w), lambda i: (i, 0))
    vec = lambda w: pl.BlockSpec((1, w), lambda i: (0, 0))
    widths = (A_W, A_KV * HEAD, A_KV * HEAD, B_GROUPS * B_W, B_W, C_RANK, C_RANK)
    dtypes = (BF16, BF16, BF16, F32, F32, BF16, BF16)
    return pl.pallas_call(
        body, name=name, grid=(s // tr,),
        in_specs=[rows(proj.shape[1]), vec(HEAD), vec(HEAD), vec(C_RANK), vec(C_RANK)] + [rows(HEAD)] * 6,
        out_specs=[rows(w) for w in widths],
        out_shape=[jax.ShapeDtypeStruct((s, w), dt) for w, dt in zip(widths, dtypes)],
        compiler_params=_params("parallel"))(
            proj, p["a_q_norm"].reshape(1, HEAD), p["a_k_norm"].reshape(1, HEAD), p["c_q_norm"].reshape(1, C_RANK),
            p["c_kv_norm"].reshape(1, C_RANK), ca, s1a, s2a, cb, s1b, s2b)


def mixer_prep_bwd(name, proj, p, tabs, dqa, dka, dva, dqb, dkb, dvb, dcqn, dckvn, dckr):
    s = proj.shape[0]
    tr = PREP_ROWS
    (ca, s1a, s2a, ha), (cb, s1b, s2b, hb) = tabs["a"][1], tabs["b"][1]

    def body(proj_ref, dqa_ref, dka_ref, dva_ref, dqb_ref, dkb_ref, dvb_ref, dcqn_ref, dckvn_ref, dckr_ref,
             gaq, gak, gcq, gckv, ca_r, s1a_r, s2a_r, cb_r, s1b_r, s2b_r, dproj_ref, dgaq, dgak, dgcq, dgckv):
        ta = (ca_r[...], s1a_r[...], s2a_r[...])
        tb = (cb_r[...], s1b_r[...], s2b_r[...])

        @pl.when(pl.program_id(0) == 0)
        def _():
            for ref in (dgaq, dgak, dgcq, dgckv):
                ref[...] = jnp.zeros_like(ref)

        def norm_bwd(dy, x, gain, dgain):
            rs = lax.rsqrt(jnp.mean(x * x, axis=-1, keepdims=True) + EPS)
            xh = x * rs
            dyg = dy * gain[...]
            dgain[...] += jnp.sum(dy * xh, axis=0, keepdims=True)
            return rs * (dyg - xh * jnp.mean(dyg * xh, axis=-1, keepdims=True))

        for (dst0, n, d_ref, gain, dgain) in ((O_AQ, A_HEADS, dqa_ref, gaq, dgaq), (O_AK, A_KV, dka_ref, gak, dgak)):
            for j, cols in _heads(dst0, n):
                dy = _rope_of(d_ref[:, HEAD * j:HEAD * (j + 1)], ta, ha)
                dproj_ref[:, cols] = norm_bwd(dy, proj_ref[:, cols], gain, dgain).astype(BF16)
        dproj_ref[:, O_AV:O_BQ] = dva_ref[...].astype(BF16)
        for (dst0, n, d_ref) in ((O_BQ, B_GROUPS * B_HEADS, dqb_ref), (O_BK, B_HEADS, dkb_ref)):
            for j, cols in _heads(dst0, n):
                dproj_ref[:, cols] = _rope_of(d_ref[:, HEAD * j:HEAD * (j + 1)], tb, hb).astype(BF16)
        dproj_ref[:, O_BV:O_CQ] = dvb_ref[...].astype(BF16)
        dproj_ref[:, O_CQ:O_CKV] = norm_bwd(dcqn_ref[...], proj_ref[:, O_CQ:O_CKV], gcq, dgcq).astype(BF16)
        dproj_ref[:, O_CKV:O_CKR] = norm_bwd(dckvn_ref[...], proj_ref[:, O_CKV:O_CKR], gckv, dgckv).astype(BF16)
        dproj_ref[:, O_CKR:] = dckr_ref[:, :C_ROPE]

    rows = lambda w: pl.BlockSpec((tr, w), lambda i: (i, 0))
    vec = lambda w: pl.BlockSpec((1, w), lambda i: (0, 0))
    grads = (dqa, dka, dva, dqb, dkb, dvb, dcqn, dckvn, dckr)
    res = pl.pallas_call(
        body, name=name, grid=(s // tr,),
        in_specs=[rows(proj.shape[1])] + [rows(g.shape[1]) for g in grads]
        + [vec(HEAD), vec(HEAD), vec(C_RANK), vec(C_RANK)] + [rows(HEAD)] * 6,
        out_specs=[rows(proj.shape[1]), vec(HEAD), vec(HEAD), vec(C_RANK), vec(C_RANK)],
        out_shape=[jax.ShapeDtypeStruct(proj.shape, BF16)] + [jax.ShapeDtypeStruct((1, w), F32) for w in (HEAD, HEAD, C_RANK, C_RANK)],
        compiler_params=_params("arbitrary"))(
            proj, *grads, p["a_q_norm"].reshape(1, HEAD), p["a_k_norm"].reshape(1, HEAD), p["c_q_norm"].reshape(1, C_RANK),
            p["c_kv_norm"].reshape(1, C_RANK), ca, s1a, s2a, cb, s1b, s2b)
    return res[0], res[1].reshape(HEAD), res[2].reshape(HEAD), res[3].reshape(C_RANK), res[4].reshape(C_RANK)


OUT_GROUPS = ((0, A_W), (A_W, A_W + B_W), (A_W + B_W, A_W + B_W + C_W))


def out_norms(name, ya, yb, yc, gain):
    s = ya.shape[0]
    tr = PREP_ROWS
    width = OUT_GROUPS[-1][1]

    def body(ya_ref, yb_ref, yc_ref, g_ref, y_ref):
        for (lo, hi), src in zip(OUT_GROUPS, (ya_ref, yb_ref, yc_ref)):
            x = src[...]
            rs = lax.rsqrt(jnp.mean(x * x, axis=-1, keepdims=True) + EPS)
            y_ref[:, lo:hi] = (x * rs * g_ref[:, lo:hi]).astype(BF16)

    rows = lambda w: pl.BlockSpec((tr, w), lambda i: (i, 0))
    return pl.pallas_call(
        body, name=name, grid=(s // tr,),
        in_specs=[rows(A_W), rows(B_W), rows(C_W), pl.BlockSpec((1, width), lambda i: (0, 0))], out_specs=rows(width),
        out_shape=jax.ShapeDtypeStruct((s, width), BF16), compiler_params=_params("parallel"))(
            ya, yb, yc, gain.reshape(1, width))


def out_norms_bwd(name, dy, ya, yb, yc, gain):
    s = ya.shape[0]
    tr = PREP_ROWS
    width = OUT_GROUPS[-1][1]

    def body(dy_ref, ya_ref, yb_ref, yc_ref, g_ref, dya_ref, dyb_ref, dyc_ref, dg_ref):
        @pl.when(pl.program_id(0) == 0)
        def _():
            dg_ref[...] = jnp.zeros_like(dg_ref)

        for (lo, hi), src, dst in zip(OUT_GROUPS, (ya_ref, yb_ref, yc_ref), (dya_ref, dyb_ref, dyc_ref)):
            x = src[...]
            d = dy_ref[:, lo:hi]
            rs = lax.rsqrt(jnp.mean(x * x, axis=-1, keepdims=True) + EPS)
            xh = x * rs
            dg = d * g_ref[:, lo:hi]
            dst[...] = (rs * (dg - xh * jnp.mean(dg * xh, axis=-1, keepdims=True))).astype(dst.dtype)
            dg_ref[:, lo:hi] += jnp.sum(d * xh, axis=0, keepdims=True)

    rows = lambda w: pl.BlockSpec((tr, w), lambda i: (i, 0))
    vec = pl.BlockSpec((1, width), lambda i: (0, 0))
    dya, dyb, dyc, dg = pl.pallas_call(
        body, name=name, grid=(s // tr,),
        in_specs=[rows(width), rows(A_W), rows(B_W), rows(C_W), vec], out_specs=[rows(A_W), rows(B_W), rows(C_W), vec],
        out_shape=[jax.ShapeDtypeStruct((s, A_W), BF16), jax.ShapeDtypeStruct((s, B_W), F32),
                   jax.ShapeDtypeStruct((s, C_W), BF16), jax.ShapeDtypeStruct((1, width), F32)],
        compiler_params=_params("arbitrary"))(dy, ya, yb, yc, gain.reshape(1, width))
    return dya, dyb, dyc, dg.reshape(width)


ATTN_TQ = 256
ATTN_FWD_TQ = 512


def attn_fwd(name, parts, v, v_group, nheads, scale):
    s = v.shape[0]
    tq = ATTN_FWD_TQ
    npart = len(parts)

    def body(*refs):
        v_ref, o_ref, lse_ref = refs[2 * npart:]
        for r0 in range(0, tq, ATTN_TQ):
            rows = slice(r0, r0 + ATTN_TQ)
            sc = None
            for p in range(npart):
                t = _dot(refs[2 * p][rows, :], refs[2 * p + 1][...], NT)
                sc = t if sc is None else sc + t
            sc = sc * scale
            m = jnp.max(sc, axis=-1, keepdims=True)
            e = jnp.exp(sc - m)
            l = jnp.sum(e, axis=-1, keepdims=True)
            o_ref[rows, :] = _dot(e.astype(BF16), v_ref[...]) / l
            lse_ref[rows, :] = jnp.broadcast_to(m + jnp.log(l), (ATTN_TQ, HEAD))

    in_specs, ins = [], []
    for q, qoff, k, kg in parts:
        in_specs.append(pl.BlockSpec((tq, HEAD), lambda h, i, qoff=qoff: (i, qoff + h)))
        in_specs.append(pl.BlockSpec((s, HEAD), lambda h, i, kg=kg: (0, h // kg)))
        ins += [q, k]
    in_specs.append(pl.BlockSpec((s, HEAD), lambda h, i: (0, h // v_group)))
    out_blk = pl.BlockSpec((tq, HEAD), lambda h, i: (i, h))
    return pl.pallas_call(
        body, name=name, grid=(nheads, s // tq), in_specs=in_specs, out_specs=[out_blk, out_blk],
        out_shape=[jax.ShapeDtypeStruct((s, nheads * HEAD), F32)] * 2,
        compiler_params=_params("parallel", "parallel"))(*ins, v)


def attn_bwd(name, parts, v, v_group, lse, o, do, nheads, scale):
    s = v.shape[0]
    tq = ATTN_FWD_TQ
    npart = len(parts)

    def body(*refs):
        v_ref, lse_ref, o_ref, do_ref = refs[2 * npart:2 * npart + 4]
        outs = refs[2 * npart + 4:]
        dq_refs, dk_refs, dv_ref = outs[:npart], outs[npart:2 * npart], outs[2 * npart]
        h, i = pl.program_id(0), pl.program_id(1)

        @pl.when((i == 0) & (h % v_group == 0))
        def _():
            dv_ref[...] = jnp.zeros_like(dv_ref)

        for p in range(npart):
            @pl.when((i == 0) & (h % parts[p][3] == 0))
            def _(p=p):
                dk_refs[p][...] = jnp.zeros_like(dk_refs[p])

        for r0 in range(0, tq, ATTN_TQ):
            rows = slice(r0, r0 + ATTN_TQ)
            sc = None
            for p in range(npart):
                t = _dot(refs[2 * p][rows, :], refs[2 * p + 1][...], NT)
                sc = t if sc is None else sc + t
            pr = jnp.exp(sc * scale - lse_ref[rows, 0:1])
            dov = do_ref[rows, :]
            delta = jnp.sum(dov.astype(F32) * o_ref[rows, :], axis=-1, keepdims=True)
            dp = _dot(dov, v_ref[...], NT)
            ds = (pr * (dp - delta) * scale).astype(BF16)
            dv_ref[...] += _dot(pr.astype(BF16), dov, TN)
            for p in range(npart):
                dq_refs[p][rows, :] = _dot(ds, refs[2 * p + 1][...])
                dk_refs[p][...] += _dot(ds, refs[2 * p][rows, :], TN)

    in_specs, ins = [], []
    for q, qoff, k, kg in parts:
        in_specs.append(pl.BlockSpec((tq, HEAD), lambda h, i, qoff=qoff: (i, qoff + h)))
        in_specs.append(pl.BlockSpec((s, HEAD), lambda h, i, kg=kg: (0, h // kg)))
        ins += [q, k]
    hq_blk = pl.BlockSpec((tq, HEAD), lambda h, i: (i, h))
    in_specs += [pl.BlockSpec((s, HEAD), lambda h, i: (0, h // v_group)), hq_blk, hq_blk, hq_blk]
    out_specs = [hq_blk] * npart
    out_shape = [jax.ShapeDtypeStruct((s, nheads * HEAD), F32)] * npart
    for q, qoff, k, kg in parts:
        out_specs.append(pl.BlockSpec((s, HEAD), lambda h, i, kg=kg: (0, h // kg)))
        out_shape.append(jax.ShapeDtypeStruct((s, nheads // kg * HEAD), F32))
    out_specs.append(pl.BlockSpec((s, HEAD), lambda h, i: (0, h // v_group)))
    out_shape.append(jax.ShapeDtypeStruct((s, nheads // v_group * HEAD), F32))
    res = pl.pallas_call(
        body, name=name, grid=(nheads, s // tq), in_specs=in_specs, out_specs=out_specs, out_shape=out_shape,
        compiler_params=_params("arbitrary", "arbitrary"))(*ins, v, lse, o, do)
    return list(res[:npart]), list(res[npart:2 * npart]), res[2 * npart]


def _band_windows(lf):
    for ib in range(lf // HEAD):
        q0 = ib * HEAD
        yield q0, max(0, q0 - B_HALF), min(lf, q0 + HEAD + B_HALF)


def _band_mask(q0, lo, hi):
    qpos = q0 + lax.broadcasted_iota(jnp.int32, (HEAD, hi - lo), 0)
    kpos = lo + lax.broadcasted_iota(jnp.int32, (HEAD, hi - lo), 1)
    return jnp.abs(qpos - kpos) <= B_HALF


def _class_rows(r, d, start, size):
    return pl.ds(r + d * start, size, stride=d) if d > 1 else pl.ds(start, size)


def dilated_fwd(name, q, k, v_src, v_block):
    s = k.shape[0]
    scale = HEAD ** -0.5

    def body(q_ref, k_ref, v_ref, y_ref, lse_ref, o_scr, l_scr):
        g_now = pl.program_id(1)
        for g, d in enumerate(B_DILATIONS):
            @pl.when(g_now == g)
            def _(g=g, d=d):
                for r in range(d):
                    for q0, lo, hi in _band_windows(s // d):
                        mine = _class_rows(r, d, q0, HEAD)
                        keys = _class_rows(r, d, lo, hi - lo)
                        sc = _dot(q_ref[mine, :].astype(BF16), k_ref[keys, :].astype(BF16), NT) * scale
                        sc = jnp.where(_band_mask(q0, lo, hi), sc, -1e30)
                        m = jnp.max(sc, axis=-1, keepdims=True)
                        e = jnp.exp(sc - m)
                        l = jnp.sum(e, axis=-1, keepdims=True)
                        o_scr.at[g][mine, :] = _dot((e / l).astype(BF16), v_ref[keys, :].astype(BF16))
                        l_scr.at[g][mine, :] = jnp.broadcast_to(m + jnp.log(l), (HEAD, HEAD))

        @pl.when(g_now == B_GROUPS - 1)
        def _():
            a, b, c = l_scr[0], l_scr[1], l_scr[2]
            m = jnp.maximum(jnp.maximum(a, b), c)
            ea, eb, ec = jnp.exp(a - m), jnp.exp(b - m), jnp.exp(c - m)
            den = ea + eb + ec
            y_ref[...] = (ea / den) * o_scr[0] + (eb / den) * o_scr[1] + (ec / den) * o_scr[2]
            lse_ref[...] = m + jnp.log(den)

    blk = lambda f: pl.BlockSpec((s, HEAD), f)
    per_head = blk(lambda h, g: (0, h))
    return pl.pallas_call(
        body, name=name, grid=(B_HEADS, B_GROUPS),
        in_specs=[blk(lambda h, g: (0, B_HEADS * g + h)), per_head, blk(lambda h, g: (0, v_block + h))],
        out_specs=[per_head, per_head], out_shape=[jax.ShapeDtypeStruct((s, B_W), F32)] * 2,
        scratch_shapes=[pltpu.VMEM((B_GROUPS, s, HEAD), F32)] * 2,
        compiler_params=_params("parallel", "arbitrary"))(q, k, v_src)


def dilated_bwd(name, q, k, v_src, v_block, dy, y, lse):
    s = k.shape[0]
    scale = HEAD ** -0.5

    def body(q_ref, k_ref, v_ref, dy_ref, y_ref, lse_ref, dq_ref, dk_ref, dv_ref, delta):
        g_now = pl.program_id(1)

        @pl.when(g_now == 0)
        def _():
            dk_ref[...] = jnp.zeros_like(dk_ref)
            dv_ref[...] = jnp.zeros_like(dv_ref)
            delta[...] = jnp.broadcast_to(jnp.sum(dy_ref[...] * y_ref[...], axis=-1, keepdims=True), (s, HEAD))

        for g, d in enumerate(B_DILATIONS):
            @pl.when(g_now == g)
            def _(d=d):
                for r in range(d):
                    for q0, lo, hi in _band_windows(s // d):
                        mine = _class_rows(r, d, q0, HEAD)
                        keys = _class_rows(r, d, lo, hi - lo)
                        qv, kv = q_ref[mine, :].astype(BF16), k_ref[keys, :].astype(BF16)
                        dyv = dy_ref[mine, :].astype(BF16)
                        sc = _dot(qv, kv, NT) * scale
                        pr = jnp.where(_band_mask(q0, lo, hi), jnp.exp(sc - lse_ref[mine, :][:, 0:1]), 0.0)
                        dp = _dot(dyv, v_ref[keys, :].astype(BF16), NT)
                        ds = (pr * (dp - delta[mine, :][:, 0:1]) * scale).astype(BF16)
                        dv_ref[keys, :] = dv_ref[keys, :] + _dot(pr.astype(BF16), dyv, TN)
                        dq_ref[mine, :] = _dot(ds, kv)
                        dk_ref[keys, :] = dk_ref[keys, :] + _dot(ds, qv, TN)

    blk = lambda f: pl.BlockSpec((s, HEAD), f)
    per_head = blk(lambda h, g: (0, h))
    by_group = blk(lambda h, g: (0, B_HEADS * g + h))
    return pl.pallas_call(
        body, name=name, grid=(B_HEADS, B_GROUPS),
        in_specs=[by_group, per_head, blk(lambda h, g: (0, v_block + h)), per_head, per_head, per_head],
        out_specs=[by_group, per_head, per_head],
        out_shape=[jax.ShapeDtypeStruct((s, B_GROUPS * B_W), F32)] + [jax.ShapeDtypeStruct((s, B_W), F32)] * 2,
        scratch_shapes=[pltpu.VMEM((s, HEAD), F32)],
        compiler_params=_params("parallel", "arbitrary"))(q, k, v_src, dy, y, lse)


FFN_TN = 256


def _edges(shape):
    row = lax.broadcasted_iota(jnp.int32, shape, 0)
    return row == 0, row == shape[0] - 1


def _shifted(h, edges):
    first, last = edges
    s = h.shape[0]
    return jnp.where(first, 0.0, pltpu.roll(h, 1, 0)), jnp.where(last, 0.0, pltpu.roll(h, s - 1, 0))


def _conv(h, cw, edges):
    prev, nxt = _shifted(h, edges)
    return prev * cw[0:1, :] + h * cw[1:2, :] + nxt * cw[2:3, :] + cw[3:4, :], prev, nxt


def _sigmoid(x):
    return 1.0 / (1.0 + jnp.exp(-x))


def ffn_up(name, xn, w_up, cwb):
    s, dm = xn.shape
    quarter = w_up.shape[3]
    fh = 2 * quarter
    tn = FFN_TN
    per = quarter // tn

    def body(x_ref, w_ref, cw_ref, h_ref, act_ref):
        xv = x_ref[...]
        hg = _dot(xv, w_ref[0])
        hu = _dot(xv, w_ref[1])
        h_ref[0] = hg
        h_ref[1] = hu
        edges = _edges(hg.shape)
        gc, _, _ = _conv(hg, cw_ref[0], edges)
        uc, _, _ = _conv(hu, cw_ref[1], edges)
        act_ref[...] = (gc * _sigmoid(gc) * uc).astype(BF16)

    return pl.pallas_call(
        body, name=name, grid=(fh // tn,),
        in_specs=[pl.BlockSpec((s, dm), lambda t: (0, 0), pipeline_mode=pl.Buffered(1)),
                  pl.BlockSpec((2, None, dm, tn), lambda t: (0, t // per, 0, t % per)),
                  pl.BlockSpec((2, 8, tn), lambda t: (0, 0, t))],
        out_specs=[pl.BlockSpec((2, s, tn), lambda t: (0, 0, t)), pl.BlockSpec((s, tn), lambda t: (0, t))],
        out_shape=[jax.ShapeDtypeStruct((2, s, fh), F32), jax.ShapeDtypeStruct((s, fh), BF16)],
        compiler_params=_params("parallel"))(xn, w_up, cwb)


def ffn_gate_bwd(name, h, dact, cwb):
    _, s, fh = h.shape
    tn = FFN_TN

    def body(h_ref, da_ref, cw_ref, dh_ref, dcw_ref):
        edges = _edges((s, tn))
        gc, gp, gn = _conv(h_ref[0], cw_ref[0], edges)
        uc, up, un = _conv(h_ref[1], cw_ref[1], edges)
        sg = _sigmoid(gc)
        da = da_ref[...]
        dgc = da * uc * (sg * (1.0 + gc * (1.0 - sg)))
        duc = da * (gc * sg)
        for idx, (hv, prev, nxt, dc) in enumerate(((h_ref[0], gp, gn, dgc), (h_ref[1], up, un, duc))):
            cw = cw_ref[idx]
            from_prev, from_next = _shifted(dc, edges)
            dh_ref[idx] = (from_next * cw[0:1, :] + dc * cw[1:2, :] + from_prev * cw[2:3, :]).astype(BF16)
            dcw_ref[idx, 0:1, :] = jnp.sum(prev * dc, axis=0, keepdims=True)
            dcw_ref[idx, 1:2, :] = jnp.sum(hv * dc, axis=0, keepdims=True)
            dcw_ref[idx, 2:3, :] = jnp.sum(nxt * dc, axis=0, keepdims=True)
            dcw_ref[idx, 3:4, :] = jnp.sum(dc, axis=0, keepdims=True)
            dcw_ref[idx, 4:8, :] = jnp.zeros((4, tn), F32)

    return pl.pallas_call(
        body, name=name, grid=(fh // tn,),
        in_specs=[pl.BlockSpec((2, s, tn), lambda t: (0, 0, t)), pl.BlockSpec((s, tn), lambda t: (0, t)),
                  pl.BlockSpec((2, 8, tn), lambda t: (0, 0, t))],
        out_specs=[pl.BlockSpec((2, s, tn), lambda t: (0, 0, t)), pl.BlockSpec((2, 8, tn), lambda t: (0, 0, t))],
        out_shape=[jax.ShapeDtypeStruct((2, s, fh), BF16), jax.ShapeDtypeStruct((2, 8, fh), F32)],
        compiler_params=_params("parallel"))(h, dact, cwb)


def ffn_dx(name, dh, w_up):
    _, s, fh = dh.shape
    dm, quarter = w_up.shape[2], w_up.shape[3]
    tk = _tile(quarter, 2816)
    per = quarter // tk
    tm, tn = _tile(s, 1024), _tile(dm, 1024)
    grid = (s // tm, dm // tn, 4 * per)
    a_spec = pl.BlockSpec((None, tm, tk), lambda i, j, k: (k // (2 * per), i, k % (2 * per)))
    b_spec = pl.BlockSpec((None, None, tn, tk), lambda i, j, k: (k // (2 * per), (k // per) % 2, j, k % per))
    o_spec = pl.BlockSpec((tm, tn), lambda i, j, k: (i, j))
    return _matmul_call(name, NT, grid, a_spec, b_spec, o_spec, jax.ShapeDtypeStruct((s, dm), F32), (tm, tn))(dh, w_up)


def ffn_dw_up(name, xn, dh):
    _, s, fh = dh.shape
    dm = xn.shape[1]
    quarter = fh // 2
    tn = _tile(quarter, 1536)
    per = quarter // tn
    tm, tk = _tile(dm, 1024), _tile(s, 2048)
    grid = (dm // tm, 4 * per, s // tk)
    a_spec = pl.BlockSpec((tk, tm), lambda i, j, k: (k, i))
    b_spec = pl.BlockSpec((None, tk, tn), lambda i, j, k: (j // (2 * per), k, j % (2 * per)))
    o_spec = pl.BlockSpec((None, None, tm, tn), lambda i, j, k: (j // (2 * per), (j // per) % 2, i, j % per))
    return _matmul_call(name, TN, grid, a_spec, b_spec, o_spec,
                        jax.ShapeDtypeStruct((2, 2, dm, quarter), F32), (tm, tn))(xn, dh)


def loss_head(name, y, target):
    s, dm = y.shape
    ts = _rows(s, dm)

    def body(y_ref, t_ref, loss_ref, dy_ref, acc):
        i = pl.program_id(0)

        @pl.when(i == 0)
        def _():
            acc[...] = jnp.zeros_like(acc)

        err = y_ref[...] - t_ref[...]
        dy_ref[...] = err * (1.0 / dm)
        acc[...] += jnp.sum(err * err, axis=0, keepdims=True)

        @pl.when(i == s // ts - 1)
        def _():
            loss_ref[...] = jnp.broadcast_to(jnp.sum(acc[...], axis=-1, keepdims=True) * (0.5 / dm), (1, LANES))

    blk = pl.BlockSpec((ts, dm), lambda i: (i, 0))
    loss, dy = pl.pallas_call(
        body, name=name, grid=(s // ts,), in_specs=[blk, blk],
        out_specs=[pl.BlockSpec((1, LANES), lambda i: (0, 0)), blk],
        out_shape=[jax.ShapeDtypeStruct((1, LANES), F32), jax.ShapeDtypeStruct((s, dm), F32)],
        scratch_shapes=[pltpu.VMEM((1, dm), F32)], compiler_params=_params("arbitrary"))(y, target)
    return loss[0, 0], dy


def attn_fwd_c(name, qc, kvc, krc, scale):
    s = qc.shape[0]
    tq = ATTN_FWD_TQ

    def body(qn_ref, qr_ref, kn_ref, kr_ref, v_ref, o_ref, lse_ref):
        for r0 in range(0, tq, ATTN_TQ):
            rows = slice(r0, r0 + ATTN_TQ)
            sc = (_dot(qn_ref[rows, :], kn_ref[...], NT) + _dot(qr_ref[rows, :], kr_ref[...], NT)) * scale
            m = jnp.max(sc, axis=-1, keepdims=True)
            e = jnp.exp(sc - m)
            l = jnp.sum(e, axis=-1, keepdims=True)
            o_ref[rows, :] = _dot(e.astype(BF16), v_ref[...]) / l
            lse_ref[rows, :] = jnp.broadcast_to(m + jnp.log(l), (ATTN_TQ, HEAD))

    qb = lambda off: pl.BlockSpec((tq, HEAD), lambda h, i: (i, off + h))
    kb = lambda f: pl.BlockSpec((s, HEAD), f)
    out_blk = pl.BlockSpec((tq, HEAD), lambda h, i: (i, h))
    return pl.pallas_call(
        body, name=name, grid=(C_HEADS, s // tq),
        in_specs=[qb(0), qb(C_HEADS), kb(lambda h, i: (0, 2 * h)), kb(lambda h, i: (0, 0)), kb(lambda h, i: (0, 2 * h + 1))],
        out_specs=[out_blk, out_blk], out_shape=[jax.ShapeDtypeStruct((s, C_W), F32)] * 2,
        compiler_params=_params("parallel", "parallel"))(qc, qc, kvc, krc, kvc)


def attn_bwd_c(name, qc, kvc, krc, lse, o, do, scale):
    s = qc.shape[0]
    tq = ATTN_FWD_TQ

    def body(qn_ref, qr_ref, kn_ref, kr_ref, v_ref, lse_ref, o_ref, do_ref, dqn_ref, dqr_ref, dkv_ref, dkr_ref):
        h, i = pl.program_id(0), pl.program_id(1)

        @pl.when(i == 0)
        def _():
            dkv_ref[...] = jnp.zeros_like(dkv_ref)

        @pl.when((i == 0) & (h == 0))
        def _():
            dkr_ref[...] = jnp.zeros_like(dkr_ref)

        for r0 in range(0, tq, ATTN_TQ):
            rows = slice(r0, r0 + ATTN_TQ)
            sc = (_dot(qn_ref[rows, :], kn_ref[...], NT) + _dot(qr_ref[rows, :], kr_ref[...], NT)) * scale
            pr = jnp.exp(sc - lse_ref[rows, 0:1])
            dov = do_ref[rows, :]
            delta = jnp.sum(dov.astype(F32) * o_ref[rows, :], axis=-1, keepdims=True)
            dp = _dot(dov, v_ref[...], NT)
            ds = (pr * (dp - delta) * scale).astype(BF16)
            dkv_ref[:, HEAD:] += _dot(pr.astype(BF16), dov, TN)
            dqn_ref[rows, :] = _dot(ds, kn_ref[...])
            dqr_ref[rows, :] = _dot(ds, kr_ref[...])
            dkv_ref[:, :HEAD] += _dot(ds, qn_ref[rows, :], TN)
            dkr_ref[...] += _dot(ds, qr_ref[rows, :], TN)

    qb = lambda off: pl.BlockSpec((tq, HEAD), lambda h, i: (i, off + h))
    kb = lambda f: pl.BlockSpec((s, HEAD), f)
    hq = pl.BlockSpec((tq, HEAD), lambda h, i: (i, h))
    kn_map, v_map, kr_map = (lambda h, i: (0, 2 * h)), (lambda h, i: (0, 2 * h + 1)), (lambda h, i: (0, 0))
    return pl.pallas_call(
        body, name=name, grid=(C_HEADS, s // tq),
        in_specs=[qb(0), qb(C_HEADS), kb(kn_map), kb(kr_map), kb(v_map), hq, hq, hq],
        out_specs=[hq, hq, pl.BlockSpec((s, 2 * HEAD), lambda h, i: (0, h)), kb(kr_map)],
        out_shape=[jax.ShapeDtypeStruct((s, C_W), F32)] * 2
        + [jax.ShapeDtypeStruct((s, 2 * C_W), F32), jax.ShapeDtypeStruct((s, HEAD), F32)],
        compiler_params=_params("arbitrary", "arbitrary"))(qc, qc, kvc, krc, kvc, lse, o, do)


def _layer_fwd(x, p, tabs):
    sv = _mixers_ab_fwd(x, p, tabs)
    x1 = _mixer_c_out_fwd(x, sv, p, tabs)
    return _ffn_fwd(x1, p, sv), sv


def _mixers_ab_fwd(x, p, tabs):
    sv = {"x": x}
    hn = rmsnorm_fwd("attn_norm", x, p["attn_norm"], x.shape[1], BF16)
    proj = matmul("in_proj", hn, p["w_in_t"], "nt", F32, tm=256)
    qa, ka, av, qb, kb, cqn, ckvn = mixer_prep("mixer_prep", proj, p, tabs)
    sv.update(hn=hn, proj=proj, qa=qa, ka=ka, av=av, qb=qb, kb=kb, cqn=cqn, ckvn=ckvn)

    ya, lse_a = attn_fwd("a_attn", [(qa, 0, ka, A_HEADS // A_KV)], av, A_HEADS // A_KV, A_HEADS, HEAD ** -0.5)
    yb, lse_b = dilated_fwd("b_dilated", qb, kb, proj, O_BV // HEAD)
    sv.update(ya=ya, lse_a=lse_a, yb=yb, lse_b=lse_b)
    return sv


def _mixer_c_out_fwd(x, sv, p, tabs):
    _mixer_c_fwd(sv, p, tabs)
    return _out_fwd(x, sv, p)


def _mixer_c_fwd(sv, p, tabs):
    qc_raw = matmul("c_uq", sv["cqn"], p["w_uq_t"], "nt", F32)
    qc = jnp.concatenate([qc_raw[:, :C_W].astype(BF16), rope("c_q_rope", qc_raw[:, C_W:], tabs["c"][0], BF16)], axis=1)
    kvc = matmul("c_ukv", sv["ckvn"], p["w_ukv"], "nn", BF16)
    ckr = jnp.pad(sv["proj"][:, O_CKR:], ((0, 0), (0, HEAD - C_ROPE)))
    krc = rope("c_k_rope", ckr, tabs["c"][0], BF16)
    c_scale = (HEAD + C_ROPE) ** -0.5
    yc, lse_c = attn_fwd_c("c_attn", qc, kvc, krc, c_scale)
    sv.update(qc=qc, kvc=kvc, krc=krc, yc=yc, lse_c=lse_c)


def _out_fwd(x, sv, p, after=0.0):
    y = out_norms("out_norms", sv["ya"], sv["yb"], sv["yc"], p["out_norm"] + after)
    x1 = matmul("out_proj", y, p["w_out"], "nn", F32, add=x)
    sv.update(y=y, x1=x1)
    return x1


def _ffn_fwd(x1, p, sv):
    _ffn_up_fwd(x1, p, sv)
    return _ffn_down_fwd(x1, p, sv)


def _ffn_up_fwd(x1, p, sv):
    xn = rmsnorm_fwd("ffn_norm", x1, p["ffn_norm"], x1.shape[1], BF16)
    h, act = ffn_up("ffn_up", xn, p["w_up"], p["cwb"])
    sv.update(xn=xn, h=h, act=act)


def _ffn_down_fwd(x1, p, sv):
    return matmul("ffn_down", sv["act"], p["w_down"], "nn", F32, add=x1)


def _layer_bwd(dx2, sv, p, tabs):
    dxn, g = _ffn_bwd(dx2.astype(BF16), sv, p)
    dx1, dy, g_out = _out_proj_bwd(dxn, dx2, sv, p)
    dhn, g_mix = _mixers_bwd(dy, sv, p, tabs)
    dx, _, g_norm = _attn_norm_bwd(dhn, dx1, sv, p)
    return dx, {**g, **g_out, **g_mix, **g_norm}


def _ffn_bwd(dx2b, sv, p, after=0.0):
    g = {}
    dact = matmul("ffn_down_dx", dx2b, p["w_down"], "nt", F32)
    g["w_down"] = matmul("ffn_down_dw", sv["act"], dx2b, "tn", F32)
    dh, g["cwb"] = ffn_gate_bwd("ffn_gate_bwd", sv["h"], dact, p["cwb"] + after)
    dxn = ffn_dx("ffn_up_dx", dh, p["w_up"])
    g["w_up"] = ffn_dw_up("ffn_up_dw", sv["xn"], dh)
    return dxn, g


def _out_proj_bwd(dxn, dx2, sv, p, after=0.0):
    g = {}
    dm = dx2.shape[1]
    dx1, dx1b, g["ffn_norm"] = rmsnorm_bwd("ffn_norm_bwd", dxn, sv["x1"], p["ffn_norm"] + after, dm, (F32, BF16), add=dx2)
    dy = matmul("out_proj_dx", dx1b, p["w_out"], "nt", F32)
    g["w_out"] = matmul("out_proj_dw", sv["y"], dx1b, "tn", F32)
    return dx1, dy, g


def _attn_norm_bwd(dhn, dx1, sv, p, after=0.0):
    dx, dxb, d_gain = rmsnorm_bwd("attn_norm_bwd", dhn, sv["x"], p["attn_norm"] + after, dhn.shape[1], (F32, BF16), add=dx1)
    return dx, dxb, {"attn_norm": d_gain}


def _mixers_bwd(dy, sv, p, tabs):
    part, g = _mixer_c_bwd(dy, sv, p, tabs)
    dhn, g_ab = _mixers_ab_bwd(part, sv, p, tabs)
    return dhn, {**g, **g_ab}


def _mixer_c_bwd(dy, sv, p, tabs, after=0.0):
    g = {}
    dya, dyb, dyc, g["out_norm"] = out_norms_bwd("out_norms_bwd", dy, sv["ya"], sv["yb"], sv["yc"], p["out_norm"] + after)

    c_scale = (HEAD + C_ROPE) ** -0.5
    dqn, dqr, dkv, dkr = attn_bwd_c("c_attn_bwd", sv["qc"], sv["kvc"], sv["krc"], sv["lse_c"], sv["yc"], dyc, c_scale)
    dqc = jnp.concatenate([dqn.astype(BF16), rope("c_q_rope_bwd", dqr, tabs["c"][1], BF16)], axis=1)
    dcqn = matmul("c_uq_dx", dqc, p["w_uq_t"], "nn", F32)
    g["w_uq"] = matmul("c_uq_dw", dqc, sv["cqn"], "tn", F32)
    dckvn = matmul("c_ukv_dx", dkv, p["w_ukv"], "nt", F32)
    g["w_ukv"] = matmul("c_ukv_dw", sv["ckvn"], dkv, "tn", F32)
    dckr = rope("c_k_rope_bwd", dkr, tabs["c"][1], BF16)
    return (dya, dyb, dcqn, dckvn, dckr), g


def _mixers_ab_bwd(part, sv, p, tabs, after=0.0):
    g = {}
    dya, dyb, dcqn, dckvn, dckr = part

    dqb, dkb, dvb = dilated_bwd("b_dilated_bwd", sv["qb"], sv["kb"], sv["proj"], O_BV // HEAD, dyb + after, sv["yb"],
                                sv["lse_b"])

    kg = A_HEADS // A_KV
    (dqa,), (dka,), dva = attn_bwd("a_attn_bwd", [(sv["qa"], 0, sv["ka"], kg)], sv["av"], kg, sv["lse_a"], sv["ya"],
                                   dya, A_HEADS, HEAD ** -0.5)

    dproj, g["a_q_norm"], g["a_k_norm"], g["c_q_norm"], g["c_kv_norm"] = mixer_prep_bwd(
        "mixer_prep_bwd", sv["proj"], p, tabs, dqa, dka, dva, dqb, dkb, dvb, dcqn, dckvn, dckr)
    dhn = matmul("in_proj_dx", dproj, p["w_in_t"], "nn", F32, tm=512)
    g["w_in"] = matmul("in_proj_dw", dproj, sv["hn"], "tn", F32, tn=512, tk=512)
    return dhn, g


def _local_step(x, target, layers, final_norm, tabs):
    saved = []
    for p in layers:
        x, sv = _layer_fwd(x, p, tabs)
        saved.append(sv)
    dm = x.shape[1]
    yf = rmsnorm_fwd("final_norm", x, final_norm, dm, F32)
    loss, dyf = loss_head("loss_head", yf, target)
    dx, d_final = rmsnorm_bwd("final_norm_bwd", dyf, x, final_norm, dm, (F32,))
    grads = [None] * len(layers)
    for l in reversed(range(len(layers))):
        dx, grads[l] = _layer_bwd(dx, saved[l], layers[l], tabs)
    return loss, dx, grads, d_final


HBM = pl.BlockSpec(memory_space=pl.ANY)


def _place():
    x, y, c = lax.axis_index("x"), lax.axis_index("y"), lax.axis_index("c")
    others = [(1 - x, y), (x, 1 - y), (1 - x, 1 - y)]
    return x, y, c, 2 * x + y, others


def _remote(src, dst, send_sem, recv_sem, device):
    return pltpu.make_async_remote_copy(src_ref=src, dst_ref=dst, send_sem=send_sem, recv_sem=recv_sem,
                                        device_id=device, device_id_type=MESH)


def gather_small(name, small):
    def body(in_ref, out_ref, send_sems, recv_sems, local_sem):
        x, y, c, chip, others = _place()
        mine = pltpu.make_async_copy(in_ref, out_ref.at[chip], local_sem)
        mine.start()
        copies = []
        for k, (ox, oy) in enumerate(others):
            cp = _remote(in_ref, out_ref.at[chip], send_sems.at[k], recv_sems.at[k], (ox, oy, c))
            cp.start()
            copies.append(cp)
        for k, (ox, oy) in enumerate(others):
            landed = out_ref.at[2 * ox + oy]
            _remote(landed, landed, send_sems.at[k], recv_sems.at[k], (ox, oy, c)).wait_recv()
        for cp in copies:
            cp.wait_send()
        mine.wait()

    return pl.pallas_call(
        body, name=name, in_specs=[HBM], out_specs=HBM, out_shape=jax.ShapeDtypeStruct((4,) + small.shape, small.dtype),
        scratch_shapes=[pltpu.SemaphoreType.DMA((3,)), pltpu.SemaphoreType.DMA((3,)), pltpu.SemaphoreType.DMA],
        compiler_params=pltpu.CompilerParams(has_side_effects=True))(small)


IN_HBM = pl.BlockSpec(memory_space=pltpu.HBM)
SEMS = pl.BlockSpec(memory_space=pltpu.SEMAPHORE)
DATAFLOW = pltpu.SideEffectType.DATAFLOW_SIDE_EFFECTING


BF16_ROWS_PER_TILE = 16


def _halves_by_rows(shape):
    return (shape[-2] // 2) % BF16_ROWS_PER_TILE == 0


def _half(shape, core):
    if _halves_by_rows(shape):
        size = shape[-2] // 2
        return (pl.ds(core * size, size), slice(None))
    size = shape[-1] // 2
    return (slice(None), pl.ds(core * size, size))


def _half_shape(shape):
    r, c = shape[-2:]
    return (r // 2, c) if _halves_by_rows(shape) else (r, c // 2)


def _gather_plan(bufs):
    x, y, c, chip, others = _place()
    plan = []
    for t, ref in enumerate(bufs):
        mine = _half(ref.shape, c)
        for k, (ox, oy) in enumerate(others):
            plan.append((3 * t + k, (ox, oy, c), ref.at[(chip,) + mine], ref.at[(chip,) + mine],
                         ref.at[(2 * ox + oy,) + mine]))
    return plan


def _exchange_plan(bufs):
    x, y, c, chip, others = _place()
    n = len(bufs) // 2
    plan = []
    for t in range(n):
        for k, (ox, oy) in enumerate(others):
            plan.append((3 * t + k, (ox, oy, c), bufs[t].at[2 * ox + oy], bufs[n + t].at[chip], bufs[n + t].at[2 * ox + oy]))
    return plan


def _sibling_plan(bufs):
    x, y, c, chip, others = _place()
    n = len(bufs) // 2
    plan = []
    for t in range(n):
        plan.append((t, (x, y, 1 - c), bufs[t].at[(slice(None),) + _half(bufs[t].shape, 1 - c)], bufs[n + t], bufs[n + t]))
    return plan


def _forward_plan(bufs):
    x, y, c, chip, others = _place()
    plan = []
    for t, ref in enumerate(bufs):
        mine, theirs = _half(ref.shape, c), _half(ref.shape, 1 - c)
        for k, (ox, oy) in enumerate(others):
            slot = 2 * ox + oy
            plan.append((3 * t + k, (x, y, 1 - c), ref.at[(slot,) + mine], ref.at[(slot,) + mine], ref.at[(slot,) + theirs]))
    return plan


def _join_plan(bufs):
    x, y, c, chip, others = _place()
    return [(t, (x, y, 1 - c), ref.at[_half(ref.shape, c)], ref.at[_half(ref.shape, c)], ref.at[_half(ref.shape, 1 - c)])
            for t, ref in enumerate(bufs)]


PLAN_COPIES = {_gather_plan: lambda n: 3 * n, _exchange_plan: lambda n: 3 * (n // 2), _sibling_plan: lambda n: n // 2,
               _forward_plan: lambda n: 3 * n, _join_plan: lambda n: n}


def split_start(name, bufs, plan_of, after):
    n = len(bufs)

    def body(*refs):
        ins = refs[:n]
        send_sems, recv_sems = refs[n + 1], refs[n + 2]
        token = refs[-1]
        for idx, peer, src, dst, _ in plan_of(ins):
            _remote(src, dst, send_sems.at[idx], recv_sems.at[idx], peer).start()
        token[...] = jnp.zeros_like(token)

    copies = PLAN_COPIES[plan_of](n)
    res = pl.pallas_call(
        body, name=name, in_specs=[IN_HBM] * n + [HBM],
        out_specs=(SEMS, SEMS) + (IN_HBM,) * n + (pl.BlockSpec(memory_space=pltpu.VMEM),),
        out_shape=(pltpu.SemaphoreType.DMA((copies,)), pltpu.SemaphoreType.DMA((copies,)))
        + tuple(pltpu.HBM(b.shape, b.dtype) for b in bufs) + (jax.ShapeDtypeStruct((8, LANES), F32),),
        input_output_aliases={t: 2 + t for t in range(n)},
        compiler_params=pltpu.CompilerParams(has_side_effects=DATAFLOW))(
            *[pltpu.with_memory_space_constraint(b, pltpu.HBM) for b in bufs], after)
    return res[0], res[1], list(res[2:2 + n]), res[-1]


def split_wait(name, send_sems, recv_sems, flying, plan_of, after):
    n = len(flying)

    def body(*refs):
        ins = refs[:n]
        send_ref, recv_ref = refs[n], refs[n + 1]
        for idx, peer, src, dst, landing in plan_of(ins):
            _remote(src, dst, send_ref.at[idx], recv_ref.at[idx], peer).wait_send()
            _remote(landing, landing, send_ref.at[idx], recv_ref.at[idx], peer).wait_recv()

    return list(pl.pallas_call(
        body, name=name, in_specs=[IN_HBM] * n + [SEMS, SEMS] + [HBM] * len(after), out_specs=(IN_HBM,) * n,
        out_shape=tuple(pltpu.HBM(b.shape, b.dtype) for b in flying),
        input_output_aliases={t: t for t in range(n)},
        compiler_params=pltpu.CompilerParams(has_side_effects=DATAFLOW))(*flying, send_sems, recv_sems, *after))


def _half_block(shape, tr):
    r, c = shape[-2:]
    if _halves_by_rows(shape):
        per = r // 2 // tr
        return tr, c, lambda i, core: (core * per + i, 0)
    return tr, c // 2, lambda i, core: (i, core)


def add_half(name, place, grad, theirs):
    hr, hc = _half_shape(grad.shape)
    tr = _rows(hr, hc)
    br, bc, at = _half_block(grad.shape, tr)

    def body(place_ref, g_ref, t_ref, o_ref):
        o_ref[...] = (g_ref[...] + t_ref[...]).astype(o_ref.dtype)

    whole = pl.BlockSpec((None, br, bc), lambda j, i, pr: (j, i, 0))
    return pl.pallas_call(
        body, name=name,
        grid_spec=pltpu.PrefetchScalarGridSpec(
            num_scalar_prefetch=1, grid=(4, hr // tr),
            in_specs=[pl.BlockSpec((None, br, bc), lambda j, i, pr: (j,) + at(i, pr[0])), whole],
            out_specs=whole),
        out_shape=jax.ShapeDtypeStruct((4, hr, hc), BF16),
        compiler_params=_params("parallel", "parallel"))(place, grad, theirs)


def sum_chips(name, place, own, parts, shard_shape):
    _, hr, hc = parts.shape
    tr = _rows(hr, hc)
    br, bc, at = _half_block(shard_shape, tr)

    def body(place_ref, own_ref, p1, p2, p3, o_ref):
        o_ref[...] = own_ref[...].astype(F32) + p1[...].astype(F32) + p2[...].astype(F32) + p3[...].astype(F32)

    def slot(k):
        return pl.BlockSpec((None, br, bc), lambda i, pr: (lax.rem(pr[1] + k, 4), i, 0))

    return pl.pallas_call(
        body, name=name,
        grid_spec=pltpu.PrefetchScalarGridSpec(
            num_scalar_prefetch=1, grid=(hr // tr,), in_specs=[slot(0), slot(1), slot(2), slot(3)],
            out_specs=pl.BlockSpec((br, bc), lambda i, pr: at(i, pr[0]))),
        out_shape=jax.ShapeDtypeStruct(tuple(shard_shape), F32),
        compiler_params=_params("parallel"))(place, own, parts, parts, parts)


def allreduce_small(name, buf):
    rows = buf.shape[0]

    def body(in_ref, out_ref, slots, send_sems, recv_sems):
        x, y, c, _, _ = _place()
        me = 4 * x + 2 * y + c
        slots[me] = in_ref[...]
        peers = []
        for k in range(1, 8):
            px = 1 - x if k & 4 else x
            py = 1 - y if k & 2 else y
            pc = 1 - c if k & 1 else c
            peers.append((px, py, pc))
        copies = []
        for k, peer in enumerate(peers):
            cp = _remote(in_ref, slots.at[me], send_sems.at[k], recv_sems.at[k], peer)
            cp.start()
            copies.append(cp)
        for k, (px, py, pc) in enumerate(peers):
            slot = slots.at[4 * px + 2 * py + pc]
            _remote(slot, slot, send_sems.at[k], recv_sems.at[k], (px, py, pc)).wait_recv()
        for cp in copies:
            cp.wait_send()
        acc = slots[0]
        for d in range(1, 8):
            acc = acc + slots[d]
        out_ref[...] = acc

    vm = pl.BlockSpec(memory_space=pltpu.VMEM)
    return pl.pallas_call(
        body, name=name, in_specs=[vm], out_specs=vm, out_shape=jax.ShapeDtypeStruct(buf.shape, F32),
        scratch_shapes=[pltpu.VMEM((8, rows, LANES), F32), pltpu.SemaphoreType.DMA((7,)), pltpu.SemaphoreType.DMA((7,))],
        compiler_params=pltpu.CompilerParams(has_side_effects=True, vmem_limit_bytes=VMEM_LIMIT))(buf)


def cast_layer(name, place, w, layer):
    _, r, cols = w.shape
    tr = _rows(r, cols)

    def body(place_ref, w_ref, o_ref):
        o_ref[...] = w_ref[...].astype(BF16)

    return pl.pallas_call(
        body, name=name,
        grid_spec=pltpu.PrefetchScalarGridSpec(
            num_scalar_prefetch=1, grid=(r // tr,),
            in_specs=[pl.BlockSpec((None, tr, cols), lambda i, pr: (layer, i, 0))],
            out_specs=pl.BlockSpec((None, tr, cols), lambda i, pr: (pr[1], i, 0))),
        out_shape=jax.ShapeDtypeStruct((4, r, cols), BF16),
        compiler_params=_params("parallel"))(place, w)


def _adamw_math(g, w, m, v):
    m = ADAM_B1 * m + (1.0 - ADAM_B1) * g
    v = ADAM_B2 * v + (1.0 - ADAM_B2) * (g * g)
    m_hat = m / (1.0 - ADAM_B1 ** ADAM_STEP)
    v_hat = v / (1.0 - ADAM_B2 ** ADAM_STEP)
    delta = -ADAM_LR * (m_hat / (jnp.sqrt(v_hat) + ADAM_EPS) + ADAM_WD * w)
    return delta, m, v


def adamw_layer(name, layer, g, w, m, v, prev=None):
    nl, r, cols = w.shape
    tr = _rows(r, cols, 8)

    def body(*refs):
        g_ref, w_ref, m_ref, v_ref = refs[:4]
        og, od, om, ov = refs[-4:]
        gv = g_ref[...]
        delta, m2, v2 = _adamw_math(gv, w_ref[...], m_ref[...], v_ref[...])
        og[...] = gv
        od[...] = delta
        om[...] = m2
        ov[...] = v2

    lay = pl.BlockSpec((None, tr, cols), lambda i: (layer, i, 0))
    ins = [g, w, m, v] + (list(prev) if prev is not None else [])
    return pl.pallas_call(
        body, name=name, grid=(r // tr,),
        in_specs=[pl.BlockSpec((tr, cols), lambda i: (i, 0)), lay, lay, lay] + ([HBM] * 4 if prev is not None else []),
        out_specs=[lay] * 4, out_shape=[jax.ShapeDtypeStruct((nl, r, cols), F32)] * 4,
        input_output_aliases={4 + k: k for k in range(4)} if prev is not None else {},
        compiler_params=_params("parallel"))(*ins)


def adamw_packed(name, g, w, m, v):
    def body(g_ref, w_ref, m_ref, v_ref, od, om, ov):
        delta, m2, v2 = _adamw_math(g_ref[...], w_ref[...], m_ref[...], v_ref[...])
        od[...] = delta
        om[...] = m2
        ov[...] = v2

    vm = pl.BlockSpec(memory_space=pltpu.VMEM)
    return pl.pallas_call(
        body, name=name, in_specs=[vm] * 4, out_specs=[vm] * 3, out_shape=[jax.ShapeDtypeStruct(g.shape, F32)] * 3,
        compiler_params=pltpu.CompilerParams(vmem_limit_bytes=VMEM_LIMIT))(g, w, m, v)


def _pack(arrays):
    flat = jnp.concatenate([a.reshape(-1) for a in arrays])
    pad = (-flat.shape[0]) % (8 * LANES)
    return jnp.pad(flat, (0, pad)).reshape(-1, LANES)


def _unpack(buf, shapes):
    flat = buf.reshape(-1)
    out, off = [], 0
    for shp in shapes:
        size = int(np.prod(shp))
        out.append(flat[off:off + size].reshape(shp))
        off += size
    return out


MIXER_W = ("w_in", "w_uq", "w_ukv", "w_out")
FFN_W = ("w_up", "w_down")
BIG = MIXER_W + FFN_W
TRANSPOSED = ("w_in", "w_uq")
COLUMN_CUT = ("w_ukv", "w_up")
SMALL = ("attn_norm", "a_q_norm", "a_k_norm", "c_q_norm", "c_kv_norm", "out_norm", "ffn_norm", "conv_b", "final_norm")
WEIGHTS = ("attn_norm", "w_in", "a_q_norm", "a_k_norm", "c_q_norm", "c_kv_norm", "w_uq", "w_ukv", "out_norm", "w_out",
           "ffn_norm", "w_up", "conv_w", "conv_b", "w_down", "final_norm")
INPUTS = ("x",) + WEIGHTS + ("loss_target",) + tuple("m_" + n for n in WEIGHTS) + tuple("v_" + n for n in WEIGHTS)


def _columns(g):
    return jnp.transpose(g, (1, 0, 2)).reshape(g.shape[1], 4 * g.shape[2])


def _uncolumns(w):
    r, c4 = w.shape
    return jnp.transpose(w.reshape(r, 4, c4 // 4), (1, 0, 2))


def _compute_layouts(full, conv_w, small, layer):
    p = {n: small[n][layer] for n in SMALL if n != "final_norm"}
    if "w_in" in full:
        p["w_in_t"] = full["w_in"].reshape(-1, full["w_in"].shape[2])
    if "w_uq" in full:
        uq_t = full["w_uq"].reshape(C_HEADS, HEAD + C_ROPE, C_RANK)
        p["w_uq_t"] = jnp.concatenate([uq_t[:, :HEAD].reshape(C_W, C_RANK),
                                       jnp.pad(uq_t[:, HEAD:], ((0, 0), (0, HEAD - C_ROPE), (0, 0))).reshape(C_W, C_RANK)], axis=0)
        p["w_ukv"] = _columns(full["w_ukv"])
        p["w_out"] = full["w_out"].reshape(-1, full["w_out"].shape[2])
    if "w_up" in full:
        up = full["w_up"]
        p["w_up"] = up.reshape(2, 2, up.shape[1], up.shape[2])
        p["w_down"] = full["w_down"].reshape(-1, full["w_down"].shape[2])
        fh = conv_w.shape[1] // 2
        taps = jnp.transpose(conv_w.reshape(3, 2, fh), (1, 0, 2))
        p["cwb"] = jnp.concatenate([taps, small["conv_b"][layer].reshape(2, 1, fh), jnp.zeros((2, 4, fh), F32)], axis=1)
    return p


def _shard_layout(name, g):
    if name == "w_in":
        return g.reshape(4, -1, g.shape[1])
    if name == "w_uq":
        uq_t = jnp.concatenate([g[:C_W].reshape(C_HEADS, HEAD, C_RANK),
                                g[C_W:].reshape(C_HEADS, HEAD, C_RANK)[:, :C_ROPE]], axis=1)
        return uq_t.reshape(4, -1, C_RANK)
    if name == "w_ukv":
        return _uncolumns(g)
    if name == "w_up":
        return g.reshape(4, g.shape[2], g.shape[3])
    return g.reshape(4, -1, g.shape[1])


def _conv_grads(dcwb):
    fh = dcwb.shape[2]
    return jnp.transpose(dcwb[:, 0:3, :], (1, 0, 2)).reshape(3, 2 * fh), dcwb[:, 3, :].reshape(2 * fh)


def _shard_layouts(g):
    return ({n: _shard_layout(n, g[n]) for n in BIG},) + _conv_grads(g["cwb"])


def kernel(x, attn_norm, w_in, a_q_norm, a_k_norm, c_q_norm, c_kv_norm, w_uq, w_ukv, out_norm, w_out, ffn_norm, w_up, conv_w, conv_b, w_down, final_norm, loss_target, m_attn_norm, m_w_in, m_a_q_norm, m_a_k_norm, m_c_q_norm, m_c_kv_norm, m_w_uq, m_w_ukv, m_out_norm, m_w_out, m_ffn_norm, m_w_up, m_conv_w, m_conv_b, m_w_down, m_final_norm, v_attn_norm, v_w_in, v_a_q_norm, v_a_k_norm, v_c_q_norm, v_c_kv_norm, v_w_uq, v_w_ukv, v_out_norm, v_w_out, v_ffn_norm, v_w_up, v_conv_w, v_conv_b, v_w_down, v_final_norm):
    a = dict(zip(INPUTS, (x, attn_norm, w_in, a_q_norm, a_k_norm, c_q_norm, c_kv_norm, w_uq, w_ukv, out_norm, w_out, ffn_norm, w_up, conv_w, conv_b, w_down, final_norm, loss_target, m_attn_norm, m_w_in, m_a_q_norm, m_a_k_norm, m_c_q_norm, m_c_kv_norm, m_w_uq, m_w_ukv, m_out_norm, m_w_out, m_ffn_norm, m_w_up, m_conv_w, m_conv_b, m_w_down, m_final_norm, v_attn_norm, v_w_in, v_a_q_norm, v_a_k_norm, v_c_q_norm, v_c_kv_norm, v_w_uq, v_w_ukv, v_out_norm, v_w_out, v_ffn_norm, v_w_up, v_conv_w, v_conv_b, v_w_down, v_final_norm)))
    nl = w_in.shape[0]
    seq = x.shape[1]
    for n in TRANSPOSED:
        for kind in ("", "m_", "v_"):
            a[kind + n] = jnp.swapaxes(a[kind + n], 1, 2)
    chip = 2 * lax.axis_index("x") + lax.axis_index("y")
    place = jnp.stack([lax.axis_index("c"), chip]).astype(jnp.int32)
    tabs = _rope_tables(seq)

    assert nl == 2
    quarter = conv_w.shape[2]
    taps = jnp.pad(conv_w, ((0, 0), (0, 8 - conv_w.shape[1]), (0, 0))).reshape(nl * 8, quarter)
    taps_full = gather_small("gather_taps", taps).reshape(4, nl, 8, quarter)
    conv_full = [jnp.transpose(taps_full[:, l, 0:3, :], (1, 0, 2)).reshape(3, 4 * quarter) for l in range(nl)]
    flights = {}
    here, after = place, taps_full
    for l in range(nl):
        for key, names in ((f"in{l}", ("w_in",)), (f"mix{l}", MIXER_W[1:]), (f"ffn{l}", FFN_W)):
            slots = [cast_layer(f"cast{l}_{n}", here, a[n], l) for n in names]
            send_sems, recv_sems, flying, token = split_start(f"gather_{key}_start", slots, _gather_plan, after)
            flights[key] = (names, send_sems, recv_sems, flying)
            here = after = place + token[0, 0].astype(jnp.int32)
    started = token[0, 0]

    def land(key, after):
        names, send_sems, recv_sems, flying = flights[key]
        landed = split_wait(f"gather_{key}_wait", send_sems, recv_sems, flying, _gather_plan, after)
        send_sems, recv_sems, passing, token = split_start(f"gather_{key}_pass_start", landed, _forward_plan, place)
        return (key, names, send_sems, recv_sems, passing), token

    def landed(passing, after, layer):
        key, names, send_sems, recv_sems, flying = passing
        full = split_wait(f"gather_{key}_pass_wait", send_sems, recv_sems, flying, _forward_plan, after)
        return _compute_layouts(dict(zip(names, full)), conv_full[layer], a, layer)

    h = x[0]
    layers, saved = [], []
    early = None
    for l in range(nl):
        if early is None:
            pass_in, token = land(f"in{l}", [taps_full])
            p = landed(pass_in, [token], l)
            p["attn_norm"] = p["attn_norm"] + started
        else:
            p = landed(early[0], [h], l)
        sv = _mixers_ab_fwd(h, p, tabs)
        if early is None:
            pass_mix, token = land(f"mix{l}", [sv["yb"]])
            p.update(landed(pass_mix, [token], l))
        else:
            p.update(landed(early[1], [sv["yb"]], l))
        _mixer_c_fwd(sv, p, tabs)
        pass_ffn, token = land(f"ffn{l}", [sv["yc"]])
        x1 = _out_fwd(h, sv, p, token[0, 0])
        p.update(landed(pass_ffn, [x1], l))
        _ffn_up_fwd(x1, p, sv)
        if l + 1 < nl:
            pass_in, _ = land(f"in{l + 1}", [sv["act"]])
            pass_mix, _ = land(f"mix{l + 1}", [sv["act"]])
            early = (pass_in, pass_mix)
        h = _ffn_down_fwd(x1, p, sv)
        layers.append(p)
        saved.append(sv)
    dm = h.shape[1]
    yf = rmsnorm_fwd("final_norm", h, final_norm, dm, F32)
    loss_here, dyf = loss_head("loss_head", yf, loss_target[0])
    dx, dxb, d_final = rmsnorm_bwd("final_norm_bwd", dyf, h, final_norm, dm, (F32, BF16))

    def swap_begin(tag, names, g):
        g_list = [_shard_layout(n, g[n]) for n in names]
        theirs = [lax.empty((4,) + _half_shape(s.shape), s.dtype) for s in g_list]
        send_sems, recv_sems, flying, token = split_start(f"{tag}_sibling_start", g_list + theirs, _sibling_plan, place)
        return (names, send_sems, recv_sems, flying), token[0, 0]

    def exchange_begin(tag, swap, after):
        names, send_sems, recv_sems, flying = swap
        landed = split_wait(f"{tag}_sibling_wait", send_sems, recv_sems, flying, _sibling_plan, after)
        g_list, theirs = landed[:len(names)], landed[len(names):]
        sums = [add_half(f"{tag}_add_{n}", place, gr, t) for n, gr, t in zip(names, g_list, theirs)]
        parts = [lax.empty(s.shape, s.dtype) for s in sums]
        send_sems, recv_sems, flying, token = split_start(f"{tag}_chips_start", sums + parts, _exchange_plan, place)
        return (names, [gr.shape[1:] for gr in g_list], send_sems, recv_sems, flying), token

    def exchange_end(tag, flight, after):
        names, shard_shapes, send_sems, recv_sems, flying = flight
        landed = split_wait(f"{tag}_chips_wait", send_sems, recv_sems, flying, _exchange_plan, after)
        sums, parts = landed[:len(names)], landed[len(names):]
        halves = [sum_chips(f"{tag}_sum_{n}", place, s, p, shp) for n, s, p, shp in zip(names, sums, parts, shard_shapes)]
        send_sems, recv_sems, joining, token = split_start(f"{tag}_join_start", halves, _join_plan, place)
        return (names, send_sems, recv_sems, joining), token

    def joined(tag, joining, after):
        names, send_sems, recv_sems, flying = joining
        return dict(zip(names, split_wait(f"{tag}_join_wait", send_sems, recv_sems, flying, _join_plan, after)))

    grads, conv_grads, flights_f, flights_m = [None] * nl, [None] * nl, [None] * nl, [None] * nl
    after = 0.0
    for l in reversed(range(nl)):
        sv, p = saved[l], layers[l]
        dxn, g = _ffn_bwd(dxb, sv, p, after)
        swap_f, after = swap_begin(f"reduce{l}f", FFN_W, g)
        dx1, dy, g_out = _out_proj_bwd(dxn, dx, sv, p, after)
        part, g_c = _mixer_c_bwd(dy, sv, p, tabs)
        flights_f[l], token = exchange_begin(f"reduce{l}f", swap_f, [part[2]])
        dhn, g_ab = _mixers_ab_bwd(part, sv, p, tabs, token[0, 0])
        g.update(g_out)
        g.update(g_c)
        g.update(g_ab)
        swap_m, after = swap_begin(f"reduce{l}m", MIXER_W, g)
        dx, dxb, g_norm = _attn_norm_bwd(dhn, dx1, sv, p, after)
        g.update(g_norm)
        grads[l], conv_grads[l] = g, _conv_grads(g["cwb"])
        before = [dx]
        if l == 0:
            small_g = {n: jnp.stack([grads[k][n] for k in range(nl)]) for n in SMALL if n not in ("conv_b", "final_norm")}
            small_g["conv_b"] = jnp.stack([cg[1] for cg in conv_grads])
            small_g["final_norm"] = d_final
            conv_w_g = jnp.stack([cg[0] for cg in conv_grads])
            small_sum = allreduce_small("reduce_small", _pack([small_g[n] for n in SMALL] + [conv_w_g, loss_here.reshape(1)]))
            before.append(small_sum)
        flights_m[l], token = exchange_begin(f"reduce{l}m", swap_m, before)
        after = token[0, 0]
    grad_x = dx

    out = {}
    join_f, token = exchange_end("reduce1f", flights_f[1], [token])
    join_m, token = exchange_end("reduce1m", flights_m[1], [token])
    reduced1 = joined("reduce1f", join_f, [token])
    for n in FFN_W:
        out[n] = adamw_layer(f"adamw1_{n}", 1, reduced1[n], a[n], a["m_" + n], a["v_" + n])
    reduced1 = joined("reduce1m", join_m, [out[n][3] for n in FFN_W])
    for n in MIXER_W:
        out[n] = adamw_layer(f"adamw1_{n}", 1, reduced1[n], a[n], a["m_" + n], a["v_" + n])

    summed = _unpack(small_sum, [a[n].shape for n in SMALL] + [conv_w_g.shape, (1,)])
    small_g = dict(zip(SMALL, summed[:-2]))
    small_g["conv_w"] = lax.dynamic_slice_in_dim(summed[-2], chip * quarter, quarter, axis=2)
    loss = summed[-1][0]
    names = SMALL + ("conv_w",)
    shapes = [a[n].shape for n in names]
    d_small, m_small, v_small = adamw_packed(
        "adamw_small", _pack([small_g[n] for n in names]), _pack([a[n] for n in names]),
        _pack([a["m_" + n] for n in names]), _pack([a["v_" + n] for n in names]))

    join_f, token = exchange_end("reduce0f", flights_f[0], [out[n][3] for n in BIG] + [d_small])
    join_m, token = exchange_end("reduce0m", flights_m[0], [token])
    reduced0 = joined("reduce0f", join_f, [token])
    for n in FFN_W:
        out[n] = adamw_layer(f"adamw0_{n}", 0, reduced0[n], a[n], a["m_" + n], a["v_" + n], prev=out[n])
    reduced0 = joined("reduce0m", join_m, [out[n][3] for n in FFN_W])
    for n in MIXER_W:
        out[n] = adamw_layer(f"adamw0_{n}", 0, reduced0[n], a[n], a["m_" + n], a["v_" + n], prev=out[n])
    for n, d_, m_, v_ in zip(names, _unpack(d_small, shapes), _unpack(m_small, shapes), _unpack(v_small, shapes)):
        out[n] = (small_g[n], d_, m_, v_)
    for n in TRANSPOSED:
        out[n] = [jnp.swapaxes(o, 1, 2) for o in out[n]]

    return (loss, grad_x[None]) + tuple(out[n][k] for k in range(4) for n in WEIGHTS)
```

```python
import functools

import jax
import jax.numpy as jnp
import numpy as np
from jax import lax
from jax.experimental import pallas as pl
from jax.experimental.pallas import tpu as pltpu

F32 = jnp.float32
BF16 = jnp.bfloat16
MESH = pl.DeviceIdType.MESH

V7X_VMEM_BYTES = 64 * 1024 * 1024
VMEM_LIMIT = V7X_VMEM_BYTES - 6 * 1024 * 1024
LANES = 128
ELEMENTWISE_BLOCK_BYTES = 4 * 1024 * 1024

HEAD = 128
A_HEADS, A_KV = 6, 2
B_HEADS, B_GROUPS = 4, 3
B_DILATIONS = (1, 4, 16)
B_HALF = 64
C_HEADS, C_RANK, C_ROPE = 6, 512, 64
GRID_W = 64
A_THETA, B_THETA, C_THETA = 10000.0, 500000.0, 10000.0
B_ROPE_DIM = 32
EPS = 1e-6
A_W, B_W, C_W = A_HEADS * HEAD, B_HEADS * HEAD, C_HEADS * HEAD
O_AQ, O_AK, O_AV = 0, 768, 1024
O_BQ, O_BK, O_BV = 1280, 2816, 3328
O_CQ, O_CKV, O_CKR = 3840, 4352, 4864
IN_W = 4928
PROJ_W = 5120

ADAM_LR, ADAM_B1, ADAM_B2, ADAM_EPS, ADAM_WD, ADAM_STEP = 0.001, 0.9, 0.999, 1e-08, 0.01, 10

NN = (((1,), (0,)), ((), ()))
NT = (((1,), (1,)), ((), ()))
TN = (((0,), (0,)), ((), ()))


def _dot(a, b, dims=NN):
    return lax.dot_general(a.astype(BF16), b.astype(BF16), dims, preferred_element_type=F32)


def _params(*sem):
    return pltpu.CompilerParams(dimension_semantics=sem if sem else None, vmem_limit_bytes=VMEM_LIMIT)


def _tile(n, target, unit=LANES):
    if n <= target:
        return n
    best = 0
    for t in range(unit, target + 1, unit):
        if n % t == 0:
            best = t
    return best if best else n


def _rows(r, width, itemsize=4):
    return _tile(r, max(8, ELEMENTWISE_BLOCK_BYTES // (width * itemsize)), 8)


def _matmul_call(name, dims, grid, a_spec, b_spec, o_spec, out_shape, acc_shape, add_spec=None):
    nk = grid[2]

    def body(*refs):
        a_ref, b_ref = refs[:2]
        add_ref = None if add_spec is None else refs[2]
        o_ref = refs[2 if add_spec is None else 3]

        def finish(r):
            if add_ref is not None:
                r = r + add_ref[...]
            o_ref[...] = r.astype(o_ref.dtype)

        if nk == 1:
            finish(_dot(a_ref[...], b_ref[...], dims))
            return
        acc = refs[-1]
        k = pl.program_id(2)

        @pl.when(k == 0)
        def _():
            acc[...] = _dot(a_ref[...], b_ref[...], dims)

        if nk > 2:
            @pl.when((k > 0) & (k < nk - 1))
            def _():
                acc[...] += _dot(a_ref[...], b_ref[...], dims)

        @pl.when(k == nk - 1)
        def _():
            finish(acc[...] + _dot(a_ref[...], b_ref[...], dims))

    in_specs = [a_spec, b_spec] + ([] if add_spec is None else [add_spec])
    return pl.pallas_call(
        body, name=name, grid=grid, in_specs=in_specs, out_specs=o_spec, out_shape=out_shape,
        scratch_shapes=[pltpu.VMEM(acc_shape, F32)] if nk > 1 else [],
        compiler_params=_params("parallel", "parallel", "arbitrary"))


def matmul(name, a, b, mode, out_dtype, add=None, tm=1024, tn=1024, tk=2816):
    if mode == "nn":
        (m, k), (k2, n) = a.shape, b.shape
    elif mode == "nt":
        (m, k), (n, k2) = a.shape, b.shape
    else:
        (k, m), (k2, n) = a.shape, b.shape
    assert k == k2, (name, a.shape, b.shape)
    tm, tn, tk = _tile(m, tm), _tile(n, tn), _tile(k, tk)
    grid = (m // tm, n // tn, k // tk)
    if mode == "tn":
        a_spec = pl.BlockSpec((tk, tm), lambda i, j, kk: (kk, i))
    else:
        a_spec = pl.BlockSpec((tm, tk), lambda i, j, kk: (i, kk))
    whole_b = {"pipeline_mode": pl.Buffered(1)} if grid[1] == 1 and grid[2] == 1 and grid[0] > 1 else {}
    if mode == "nt":
        b_spec = pl.BlockSpec((tn, tk), lambda i, j, kk: (j, kk), **whole_b)
    else:
        b_spec = pl.BlockSpec((tk, tn), lambda i, j, kk: (kk, j), **whole_b)
    o_spec = pl.BlockSpec((tm, tn), lambda i, j, kk: (i, j))
    dims = {"nn": NN, "nt": NT, "tn": TN}[mode]
    call = _matmul_call(name, dims, grid, a_spec, b_spec, o_spec, jax.ShapeDtypeStruct((m, n), out_dtype),
                        (tm, tn), None if add is None else o_spec)
    return call(a, b) if add is None else call(a, b, add)


def rmsnorm_fwd(name, x, g, width, out_dtype):
    r, cols = x.shape
    nb = cols // width
    tr = _rows(r, width)

    def body(x_ref, g_ref, o_ref):
        xv = x_ref[...]
        rs = lax.rsqrt(jnp.mean(xv * xv, axis=-1, keepdims=True) + EPS)
        o_ref[...] = (xv * rs * g_ref[...]).astype(o_ref.dtype)

    blk = pl.BlockSpec((tr, width), lambda i, j: (i, j))
    return pl.pallas_call(
        body, name=name, grid=(r // tr, nb),
        in_specs=[blk, pl.BlockSpec((1, width), lambda i, j: (0, 0))], out_specs=blk,
        out_shape=jax.ShapeDtypeStruct((r, cols), out_dtype),
        compiler_params=_params("parallel", "parallel"))(x, g.reshape(1, width))


def rmsnorm_bwd(name, dy, x, g, width, out_dtypes=(F32,), add=None):
    r, cols = x.shape
    nb = cols // width
    tr = _rows(r, width)
    n_out = len(out_dtypes)

    def body(*refs):
        dy_ref, x_ref, g_ref = refs[:3]
        add_ref = refs[3] if add is not None else None
        outs = refs[-(n_out + 1):-1]
        dg_ref = refs[-1]
        xv = x_ref[...]
        dyv = dy_ref[...].astype(F32)
        rs = lax.rsqrt(jnp.mean(xv * xv, axis=-1, keepdims=True) + EPS)
        xh = xv * rs
        dyg = dyv * g_ref[...]
        dx = rs * (dyg - xh * jnp.mean(dyg * xh, axis=-1, keepdims=True))
        if add_ref is not None:
            dx = dx + add_ref[...]
        for o in outs:
            o[...] = dx.astype(o.dtype)

        @pl.when((pl.program_id(0) == 0) & (pl.program_id(1) == 0))
        def _():
            dg_ref[...] = jnp.zeros_like(dg_ref)

        dg_ref[...] += jnp.sum(dyv * xh, axis=0, keepdims=True)

    blk = pl.BlockSpec((tr, width), lambda i, j: (i, j))
    vec = pl.BlockSpec((1, width), lambda i, j: (0, 0))
    ins = [dy, x, g.reshape(1, width)] + ([add] if add is not None else [])
    res = pl.pallas_call(
        body, name=name, grid=(r // tr, nb),
        in_specs=[blk, blk, vec] + ([blk] if add is not None else []),
        out_specs=[blk] * n_out + [vec],
        out_shape=[jax.ShapeDtypeStruct((r, cols), dt) for dt in out_dtypes] + [jax.ShapeDtypeStruct((1, width), F32)],
        compiler_params=_params("arbitrary", "arbitrary"))(*ins)
    return tuple(res[:n_out]) + (res[n_out].reshape(width),)


def _rope_angles(pos, dim, theta):
    inv = theta ** (-jnp.arange(0, dim, 2, dtype=F32) / dim)
    ang = pos.astype(F32)[:, None] * inv[None, :]
    return jnp.cos(ang), jnp.sin(ang)


def _rope_tables(s):
    rows = s // GRID_W
    row_pos = jnp.repeat(jnp.arange(rows), GRID_W)
    col_pos = jnp.tile(jnp.arange(GRID_W), rows)
    t_pos = jnp.arange(s)
    z = lambda n: jnp.zeros((s, n), F32)
    o = lambda n: jnp.ones((s, n), F32)
    cr, sr = _rope_angles(row_pos, HEAD // 2, A_THETA)
    cc, sc = _rope_angles(col_pos, HEAD // 2, A_THETA)
    tab_a = (jnp.concatenate([cr, cr, cc, cc], 1), jnp.concatenate([-sr, z(32), -sc, z(32)], 1),
             jnp.concatenate([z(32), sr, z(32), sc], 1), 32)
    cp, sp = _rope_angles(t_pos, B_ROPE_DIM, B_THETA)
    tab_b = (jnp.concatenate([cp, cp, o(96)], 1), jnp.concatenate([-sp, z(112)], 1),
             jnp.concatenate([z(16), sp, z(96)], 1), 16)
    cm, sm = _rope_angles(t_pos, C_ROPE, C_THETA)
    tab_c = (jnp.concatenate([cm, cm, o(64)], 1), jnp.concatenate([-sm, z(96)], 1),
             jnp.concatenate([z(32), sm, z(64)], 1), 32)

    def transposed(tab):
        c, s1, s2, h = tab
        return (c, jnp.roll(s2, -h, axis=1), jnp.roll(s1, h, axis=1), h)

    return {k: (t, transposed(t)) for k, t in (("a", tab_a), ("b", tab_b), ("c", tab_c))}


def rope(name, x, tab, out_dtype, plain_heads=0):
    c, s1, s2, h = tab
    s, cols = x.shape
    ts = _rows(s, cols)

    def body(x_ref, c_ref, s1_ref, s2_ref, o_ref):
        cv, s1v, s2v = c_ref[...], s1_ref[...], s2_ref[...]
        for j in range(cols // HEAD):
            lanes = slice(HEAD * j, HEAD * (j + 1))
            xv = x_ref[:, lanes].astype(F32)
            if j >= plain_heads:
                xv = xv * cv + pltpu.roll(xv, HEAD - h, 1) * s1v + pltpu.roll(xv, h, 1) * s2v
            o_ref[:, lanes] = xv.astype(o_ref.dtype)

    blk = pl.BlockSpec((ts, cols), lambda i: (i, 0))
    tb = pl.BlockSpec((ts, HEAD), lambda i: (i, 0))
    return pl.pallas_call(
        body, name=name, grid=(s // ts,), in_specs=[blk, tb, tb, tb], out_specs=blk,
        out_shape=jax.ShapeDtypeStruct((s, cols), out_dtype),
        compiler_params=_params("parallel"))(x, c, s1, s2)


PREP_ROWS = 256


def _rope_of(x, tab, h):
    c, s1, s2 = tab
    return x * c + pltpu.roll(x, HEAD - h, 1) * s1 + pltpu.roll(x, h, 1) * s2


def _heads(start, n):
    return [(j, slice(start + HEAD * j, start + HEAD * (j + 1))) for j in range(n)]


def mixer_prep(name, proj, p, tabs):
    s = proj.shape[0]
    tr = PREP_ROWS
    (ca, s1a, s2a, ha), (cb, s1b, s2b, hb) = tabs["a"][0], tabs["b"][0]

    def body(proj_ref, gaq, gak, gcq, gckv, ca_r, s1a_r, s2a_r, cb_r, s1b_r, s2b_r,
             qa_ref, ka_ref, av_ref, qb_ref, kb_ref, cqn_ref, ckvn_ref):
        ta = (ca_r[...], s1a_r[...], s2a_r[...])
        tb = (cb_r[...], s1b_r[...], s2b_r[...])

        def normed(x, g):
            return x * lax.rsqrt(jnp.mean(x * x, axis=-1, keepdims=True) + EPS) * g

        for (src0, n, gain, out) in ((O_AQ, A_HEADS, gaq, qa_ref), (O_AK, A_KV, gak, ka_ref)):
            for j, cols in _heads(src0, n):
                out[:, HEAD * j:HEAD * (j + 1)] = _rope_of(normed(proj_ref[:, cols], gain[...]), ta, ha).astype(BF16)
        av_ref[...] = proj_ref[:, O_AV:O_BQ].astype(BF16)
        for (src0, n, out) in ((O_BQ, B_GROUPS * B_HEADS, qb_ref), (O_BK, B_HEADS, kb_ref)):
            for j, cols in _heads(src0, n):
                out[:, HEAD * j:HEAD * (j + 1)] = _rope_of(proj_ref[:, cols], tb, hb)
        cqn_ref[...] = normed(proj_ref[:, O_CQ:O_CKV], gcq[...]).astype(BF16)
        ckvn_ref[...] = normed(proj_ref[:, O_CKV:O_CKR], gckv[...]).astype(BF16)

    rows = lambda w: pl.BlockSpec((tr, w), lambda i: (i, 0))
    vec = lambda w: pl.BlockSpec((1, w), lambda i: (0, 0))
    widths = (A_W, A_KV * HEAD, A_KV * HEAD, B_GROUPS * B_W, B_W, C_RANK, C_RANK)
    dtypes = (BF16, BF16, BF16, F32, F32, BF16, BF16)
    return pl.pallas_call(
        body, name=name, grid=(s // tr,),
        in_specs=[rows(proj.shape[1]), vec(HEAD), vec(HEAD), vec(C_RANK), vec(C_RANK)] + [rows(HEAD)] * 6,
        out_specs=[rows(w) for w in widths],
        out_shape=[jax.ShapeDtypeStruct((s, w), dt) for w, dt in zip(widths, dtypes)],
        compiler_params=_params("parallel"))(
            proj, p["a_q_norm"].reshape(1, HEAD), p["a_k_norm"].reshape(1, HEAD), p["c_q_norm"].reshape(1, C_RANK),
            p["c_kv_norm"].reshape(1, C_RANK), ca, s1a, s2a, cb, s1b, s2b)


def mixer_prep_bwd(name, proj, p, tabs, dqa, dka, dva, dqb, dkb, dvb, dcqn, dckvn, dckr):
    s = proj.shape[0]
    tr = PREP_ROWS
    (ca, s1a, s2a, ha), (cb, s1b, s2b, hb) = tabs["a"][1], tabs["b"][1]

    def body(proj_ref, dqa_ref, dka_ref, dva_ref, dqb_ref, dkb_ref, dvb_ref, dcqn_ref, dckvn_ref, dckr_ref,
             gaq, gak, gcq, gckv, ca_r, s1a_r, s2a_r, cb_r, s1b_r, s2b_r, dproj_ref, dgaq, dgak, dgcq, dgckv):
        ta = (ca_r[...], s1a_r[...], s2a_r[...])
        tb = (cb_r[...], s1b_r[...], s2b_r[...])

        @pl.when(pl.program_id(0) == 0)
        def _():
            for ref in (dgaq, dgak, dgcq, dgckv):
                ref[...] = jnp.zeros_like(ref)

        def norm_bwd(dy, x, gain, dgain):
            rs = lax.rsqrt(jnp.mean(x * x, axis=-1, keepdims=True) + EPS)
            xh = x * rs
            dyg = dy * gain[...]
            dgain[...] += jnp.sum(dy * xh, axis=0, keepdims=True)
            return rs * (dyg - xh * jnp.mean(dyg * xh, axis=-1, keepdims=True))

        for (dst0, n, d_ref, gain, dgain) in ((O_AQ, A_HEADS, dqa_ref, gaq, dgaq), (O_AK, A_KV, dka_ref, gak, dgak)):
            for j, cols in _heads(dst0, n):
                dy = _rope_of(d_ref[:, HEAD * j:HEAD * (j + 1)], ta, ha)
                dproj_ref[:, cols] = norm_bwd(dy, proj_ref[:, cols], gain, dgain).astype(BF16)
        dproj_ref[:, O_AV:O_BQ] = dva_ref[...].astype(BF16)
        for (dst0, n, d_ref) in ((O_BQ, B_GROUPS * B_HEADS, dqb_ref), (O_BK, B_HEADS, dkb_ref)):
            for j, cols in _heads(dst0, n):
                dproj_ref[:, cols] = _rope_of(d_ref[:, HEAD * j:HEAD * (j + 1)], tb, hb).astype(BF16)
        dproj_ref[:, O_BV:O_CQ] = dvb_ref[...].astype(BF16)
        dproj_ref[:, O_CQ:O_CKV] = norm_bwd(dcqn_ref[...], proj_ref[:, O_CQ:O_CKV], gcq, dgcq).astype(BF16)
        dproj_ref[:, O_CKV:O_CKR] = norm_bwd(dckvn_ref[...], proj_ref[:, O_CKV:O_CKR], gckv, dgckv).astype(BF16)
        dproj_ref[:, O_CKR:] = dckr_ref[:, :C_ROPE]

    rows = lambda w: pl.BlockSpec((tr, w), lambda i: (i, 0))
    vec = lambda w: pl.BlockSpec((1, w), lambda i: (0, 0))
    grads = (dqa, dka, dva, dqb, dkb, dvb, dcqn, dckvn, dckr)
    res = pl.pallas_call(
        body, name=name, grid=(s // tr,),
        in_specs=[rows(proj.shape[1])] + [rows(g.shape[1]) for g in grads]
        + [vec(HEAD), vec(HEAD), vec(C_RANK), vec(C_RANK)] + [rows(HEAD)] * 6,
        out_specs=[rows(proj.shape[1]), vec(HEAD), vec(HEAD), vec(C_RANK), vec(C_RANK)],
        out_shape=[jax.ShapeDtypeStruct(proj.shape, BF16)] + [jax.ShapeDtypeStruct((1, w), F32) for w in (HEAD, HEAD, C_RANK, C_RANK)],
        compiler_params=_params("arbitrary"))(
            proj, *grads, p["a_q_norm"].reshape(1, HEAD), p["a_k_norm"].reshape(1, HEAD), p["c_q_norm"].reshape(1, C_RANK),
            p["c_kv_norm"].reshape(1, C_RANK), ca, s1a, s2a, cb, s1b, s2b)
    return res[0], res[1].reshape(HEAD), res[2].reshape(HEAD), res[3].reshape(C_RANK), res[4].reshape(C_RANK)


OUT_GROUPS = ((0, A_W), (A_W, A_W + B_W), (A_W + B_W, A_W + B_W + C_W))


def out_norms(name, ya, yb, yc, gain):
    s = ya.shape[0]
    tr = PREP_ROWS
    width = OUT_GROUPS[-1][1]

    def body(ya_ref, yb_ref, yc_ref, g_ref, y_ref):
        for (lo, hi), src in zip(OUT_GROUPS, (ya_ref, yb_ref, yc_ref)):
            x = src[...]
            rs = lax.rsqrt(jnp.mean(x * x, axis=-1, keepdims=True) + EPS)
            y_ref[:, lo:hi] = (x * rs * g_ref[:, lo:hi]).astype(BF16)

    rows = lambda w: pl.BlockSpec((tr, w), lambda i: (i, 0))
    return pl.pallas_call(
        body, name=name, grid=(s // tr,),
        in_specs=[rows(A_W), rows(B_W), rows(C_W), pl.BlockSpec((1, width), lambda i: (0, 0))], out_specs=rows(width),
        out_shape=jax.ShapeDtypeStruct((s, width), BF16), compiler_params=_params("parallel"))(
            ya, yb, yc, gain.reshape(1, width))


def out_norms_bwd(name, dy, ya, yb, yc, gain):
    s = ya.shape[0]
    tr = PREP_ROWS
    width = OUT_GROUPS[-1][1]

    def body(dy_ref, ya_ref, yb_ref, yc_ref, g_ref, dya_ref, dyb_ref, dyc_ref, dg_ref):
        @pl.when(pl.program_id(0) == 0)
        def _():
            dg_ref[...] = jnp.zeros_like(dg_ref)

        for (lo, hi), src, dst in zip(OUT_GROUPS, (ya_ref, yb_ref, yc_ref), (dya_ref, dyb_ref, dyc_ref)):
            x = src[...]
            d = dy_ref[:, lo:hi]
            rs = lax.rsqrt(jnp.mean(x * x, axis=-1, keepdims=True) + EPS)
            xh = x * rs
            dg = d * g_ref[:, lo:hi]
            dst[...] = (rs * (dg - xh * jnp.mean(dg * xh, axis=-1, keepdims=True))).astype(dst.dtype)
            dg_ref[:, lo:hi] += jnp.sum(d * xh, axis=0, keepdims=True)

    rows = lambda w: pl.BlockSpec((tr, w), lambda i: (i, 0))
    vec = pl.BlockSpec((1, width), lambda i: (0, 0))
    dya, dyb, dyc, dg = pl.pallas_call(
        body, name=name, grid=(s // tr,),
        in_specs=[rows(width), rows(A_W), rows(B_W), rows(C_W), vec], out_specs=[rows(A_W), rows(B_W), rows(C_W), vec],
        out_shape=[jax.ShapeDtypeStruct((s, A_W), BF16), jax.ShapeDtypeStruct((s, B_W), F32),
                   jax.ShapeDtypeStruct((s, C_W), BF16), jax.ShapeDtypeStruct((1, width), F32)],
        compiler_params=_params("arbitrary"))(dy, ya, yb, yc, gain.reshape(1, width))
    return dya, dyb, dyc, dg.reshape(width)


ATTN_TQ = 256
ATTN_STEP_ROWS = 512


def attn_fwd(name, parts, v, v_group, nheads, scale):
    s = v.shape[0]
    tq = ATTN_STEP_ROWS
    npart = len(parts)

    def body(*refs):
        v_ref, o_ref, lse_ref = refs[2 * npart:]
        for r0 in range(0, tq, ATTN_TQ):
            rows = slice(r0, r0 + ATTN_TQ)
            sc = None
            for p in range(npart):
                t = _dot(refs[2 * p][rows, :], refs[2 * p + 1][...], NT)
                sc = t if sc is None else sc + t
            sc = sc * scale
            m = jnp.max(sc, axis=-1, keepdims=True)
            e = jnp.exp(sc - m)
            l = jnp.sum(e, axis=-1, keepdims=True)
            o_ref[rows, :] = _dot(e.astype(BF16), v_ref[...]) / l
            lse_ref[rows, :] = jnp.broadcast_to(m + jnp.log(l), (ATTN_TQ, HEAD))

    in_specs, ins = [], []
    for q, qoff, k, kg in parts:
        in_specs.append(pl.BlockSpec((tq, HEAD), lambda h, i, qoff=qoff: (i, qoff + h)))
        in_specs.append(pl.BlockSpec((s, HEAD), lambda h, i, kg=kg: (0, h // kg)))
        ins += [q, k]
    in_specs.append(pl.BlockSpec((s, HEAD), lambda h, i: (0, h // v_group)))
    out_blk = pl.BlockSpec((tq, HEAD), lambda h, i: (i, h))
    return pl.pallas_call(
        body, name=name, grid=(nheads, s // tq), in_specs=in_specs, out_specs=[out_blk, out_blk],
        out_shape=[jax.ShapeDtypeStruct((s, nheads * HEAD), F32)] * 2,
        compiler_params=_params("parallel", "parallel"))(*ins, v)


def attn_bwd(name, parts, v, v_group, lse, o, do, nheads, scale):
    s = v.shape[0]
    tq = ATTN_STEP_ROWS
    npart = len(parts)

    def body(*refs):
        v_ref, lse_ref, o_ref, do_ref = refs[2 * npart:2 * npart + 4]
        outs = refs[2 * npart + 4:]
        dq_refs, dk_refs, dv_ref = outs[:npart], outs[npart:2 * npart], outs[2 * npart]
        h, i = pl.program_id(0), pl.program_id(1)

        @pl.when((i == 0) & (h % v_group == 0))
        def _():
            dv_ref[...] = jnp.zeros_like(dv_ref)

        for p in range(npart):
            @pl.when((i == 0) & (h % parts[p][3] == 0))
            def _(p=p):
                dk_refs[p][...] = jnp.zeros_like(dk_refs[p])

        for r0 in range(0, tq, ATTN_TQ):
            rows = slice(r0, r0 + ATTN_TQ)
            sc = None
            for p in range(npart):
                t = _dot(refs[2 * p][rows, :], refs[2 * p + 1][...], NT)
                sc = t if sc is None else sc + t
            pr = jnp.exp(sc * scale - lse_ref[rows, 0:1])
            dov = do_ref[rows, :]
            delta = jnp.sum(dov.astype(F32) * o_ref[rows, :], axis=-1, keepdims=True)
            dp = _dot(dov, v_ref[...], NT)
            ds = (pr * (dp - delta) * scale).astype(BF16)
            dv_ref[...] += _dot(pr.astype(BF16), dov, TN)
            for p in range(npart):
                dq_refs[p][rows, :] = _dot(ds, refs[2 * p + 1][...])
                dk_refs[p][...] += _dot(ds, refs[2 * p][rows, :], TN)

    in_specs, ins = [], []
    for q, qoff, k, kg in parts:
        in_specs.append(pl.BlockSpec((tq, HEAD), lambda h, i, qoff=qoff: (i, qoff + h)))
        in_specs.append(pl.BlockSpec((s, HEAD), lambda h, i, kg=kg: (0, h // kg)))
        ins += [q, k]
    hq_blk = pl.BlockSpec((tq, HEAD), lambda h, i: (i, h))
    in_specs += [pl.BlockSpec((s, HEAD), lambda h, i: (0, h // v_group)), hq_blk, hq_blk, hq_blk]
    out_specs = [hq_blk] * npart
    out_shape = [jax.ShapeDtypeStruct((s, nheads * HEAD), F32)] * npart
    for q, qoff, k, kg in parts:
        out_specs.append(pl.BlockSpec((s, HEAD), lambda h, i, kg=kg: (0, h // kg)))
        out_shape.append(jax.ShapeDtypeStruct((s, nheads // kg * HEAD), F32))
    out_specs.append(pl.BlockSpec((s, HEAD), lambda h, i: (0, h // v_group)))
    out_shape.append(jax.ShapeDtypeStruct((s, nheads // v_group * HEAD), F32))
    res = pl.pallas_call(
        body, name=name, grid=(nheads, s // tq), in_specs=in_specs, out_specs=out_specs, out_shape=out_shape,
        compiler_params=_params("arbitrary", "arbitrary"))(*ins, v, lse, o, do)
    return list(res[:npart]), list(res[npart:2 * npart]), res[2 * npart]


def _band_windows(lf):
    for ib in range(lf // HEAD):
        q0 = ib * HEAD
        yield q0, max(0, q0 - B_HALF), min(lf, q0 + HEAD + B_HALF)


def _band_mask(q0, lo, hi):
    qpos = q0 + lax.broadcasted_iota(jnp.int32, (HEAD, hi - lo), 0)
    kpos = lo + lax.broadcasted_iota(jnp.int32, (HEAD, hi - lo), 1)
    return jnp.abs(qpos - kpos) <= B_HALF


def _class_rows(r, d, start, size):
    return pl.ds(r + d * start, size, stride=d) if d > 1 else pl.ds(start, size)


def dilated_fwd(name, q, k, v_src, v_block):
    s = k.shape[0]
    scale = HEAD ** -0.5

    def body(q_ref, k_ref, v_ref, y_ref, lse_ref, o_scr, l_scr):
        g_now = pl.program_id(1)
        for g, d in enumerate(B_DILATIONS):
            @pl.when(g_now == g)
            def _(g=g, d=d):
                for r in range(d):
                    for q0, lo, hi in _band_windows(s // d):
                        mine = _class_rows(r, d, q0, HEAD)
                        keys = _class_rows(r, d, lo, hi - lo)
                        sc = _dot(q_ref[mine, :].astype(BF16), k_ref[keys, :].astype(BF16), NT) * scale
                        sc = jnp.where(_band_mask(q0, lo, hi), sc, -1e30)
                        m = jnp.max(sc, axis=-1, keepdims=True)
                        e = jnp.exp(sc - m)
                        l = jnp.sum(e, axis=-1, keepdims=True)
                        o_scr.at[g][mine, :] = _dot((e / l).astype(BF16), v_ref[keys, :].astype(BF16))
                        l_scr.at[g][mine, :] = jnp.broadcast_to(m + jnp.log(l), (HEAD, HEAD))

        @pl.when(g_now == B_GROUPS - 1)
        def _():
            a, b, c = l_scr[0], l_scr[1], l_scr[2]
            m = jnp.maximum(jnp.maximum(a, b), c)
            ea, eb, ec = jnp.exp(a - m), jnp.exp(b - m), jnp.exp(c - m)
            den = ea + eb + ec
            y_ref[...] = (ea / den) * o_scr[0] + (eb / den) * o_scr[1] + (ec / den) * o_scr[2]
            lse_ref[...] = m + jnp.log(den)

    blk = lambda f: pl.BlockSpec((s, HEAD), f)
    per_head = blk(lambda h, g: (0, h))
    return pl.pallas_call(
        body, name=name, grid=(B_HEADS, B_GROUPS),
        in_specs=[blk(lambda h, g: (0, B_HEADS * g + h)), per_head, blk(lambda h, g: (0, v_block + h))],
        out_specs=[per_head, per_head], out_shape=[jax.ShapeDtypeStruct((s, B_W), F32)] * 2,
        scratch_shapes=[pltpu.VMEM((B_GROUPS, s, HEAD), F32)] * 2,
        compiler_params=_params("parallel", "arbitrary"))(q, k, v_src)


def dilated_bwd(name, q, k, v_src, v_block, dy, y, lse):
    s = k.shape[0]
    scale = HEAD ** -0.5

    def body(q_ref, k_ref, v_ref, dy_ref, y_ref, lse_ref, dq_ref, dk_ref, dv_ref, delta):
        g_now = pl.program_id(1)

        @pl.when(g_now == 0)
        def _():
            dk_ref[...] = jnp.zeros_like(dk_ref)
            dv_ref[...] = jnp.zeros_like(dv_ref)
            delta[...] = jnp.broadcast_to(jnp.sum(dy_ref[...] * y_ref[...], axis=-1, keepdims=True), (s, HEAD))

        for g, d in enumerate(B_DILATIONS):
            @pl.when(g_now == g)
            def _(d=d):
                for r in range(d):
                    for q0, lo, hi in _band_windows(s // d):
                        mine = _class_rows(r, d, q0, HEAD)
                        keys = _class_rows(r, d, lo, hi - lo)
                        qv, kv = q_ref[mine, :].astype(BF16), k_ref[keys, :].astype(BF16)
                        dyv = dy_ref[mine, :].astype(BF16)
                        sc = _dot(qv, kv, NT) * scale
                        pr = jnp.where(_band_mask(q0, lo, hi), jnp.exp(sc - lse_ref[mine, :][:, 0:1]), 0.0)
                        dp = _dot(dyv, v_ref[keys, :].astype(BF16), NT)
                        ds = (pr * (dp - delta[mine, :][:, 0:1]) * scale).astype(BF16)
                        dv_ref[keys, :] = dv_ref[keys, :] + _dot(pr.astype(BF16), dyv, TN)
                        dq_ref[mine, :] = _dot(ds, kv)
                        dk_ref[keys, :] = dk_ref[keys, :] + _dot(ds, qv, TN)

    blk = lambda f: pl.BlockSpec((s, HEAD), f)
    per_head = blk(lambda h, g: (0, h))
    by_group = blk(lambda h, g: (0, B_HEADS * g + h))
    return pl.pallas_call(
        body, name=name, grid=(B_HEADS, B_GROUPS),
        in_specs=[by_group, per_head, blk(lambda h, g: (0, v_block + h)), per_head, per_head, per_head],
        out_specs=[by_group, per_head, per_head],
        out_shape=[jax.ShapeDtypeStruct((s, B_GROUPS * B_W), F32)] + [jax.ShapeDtypeStruct((s, B_W), F32)] * 2,
        scratch_shapes=[pltpu.VMEM((s, HEAD), F32)],
        compiler_params=_params("parallel", "arbitrary"))(q, k, v_src, dy, y, lse)


FFN_TN = 256


def _edges(shape):
    row = lax.broadcasted_iota(jnp.int32, shape, 0)
    return row == 0, row == shape[0] - 1


def _shifted(h, edges):
    first, last = edges
    s = h.shape[0]
    return jnp.where(first, 0.0, pltpu.roll(h, 1, 0)), jnp.where(last, 0.0, pltpu.roll(h, s - 1, 0))


def _conv(h, cw, edges):
    prev, nxt = _shifted(h, edges)
    return prev * cw[0:1, :] + h * cw[1:2, :] + nxt * cw[2:3, :] + cw[3:4, :], prev, nxt


def _sigmoid(x):
    return 1.0 / (1.0 + jnp.exp(-x))


def ffn_up(name, xn, w_up, cwb):
    s, dm = xn.shape
    quarter = w_up.shape[3]
    fh = 2 * quarter
    tn = FFN_TN
    per = quarter // tn

    def body(x_ref, w_ref, cw_ref, h_ref, act_ref):
        xv = x_ref[...]
        hg = _dot(xv, w_ref[0])
        hu = _dot(xv, w_ref[1])
        h_ref[0] = hg
        h_ref[1] = hu
        edges = _edges(hg.shape)
        gc, _, _ = _conv(hg, cw_ref[0], edges)
        uc, _, _ = _conv(hu, cw_ref[1], edges)
        act_ref[...] = (gc * _sigmoid(gc) * uc).astype(BF16)

    return pl.pallas_call(
        body, name=name, grid=(fh // tn,),
        in_specs=[pl.BlockSpec((s, dm), lambda t: (0, 0), pipeline_mode=pl.Buffered(1)),
                  pl.BlockSpec((2, None, dm, tn), lambda t: (0, t // per, 0, t % per)),
                  pl.BlockSpec((2, 8, tn), lambda t: (0, 0, t))],
        out_specs=[pl.BlockSpec((2, s, tn), lambda t: (0, 0, t)), pl.BlockSpec((s, tn), lambda t: (0, t))],
        out_shape=[jax.ShapeDtypeStruct((2, s, fh), F32), jax.ShapeDtypeStruct((s, fh), BF16)],
        compiler_params=_params("parallel"))(xn, w_up, cwb)


def ffn_gate_bwd(name, h, dact, cwb):
    _, s, fh = h.shape
    tn = FFN_TN

    def body(h_ref, da_ref, cw_ref, dh_ref, dcw_ref):
        edges = _edges((s, tn))
        gc, gp, gn = _conv(h_ref[0], cw_ref[0], edges)
        uc, up, un = _conv(h_ref[1], cw_ref[1], edges)
        sg = _sigmoid(gc)
        da = da_ref[...]
        dgc = da * uc * (sg * (1.0 + gc * (1.0 - sg)))
        duc = da * (gc * sg)
        for idx, (hv, prev, nxt, dc) in enumerate(((h_ref[0], gp, gn, dgc), (h_ref[1], up, un, duc))):
            cw = cw_ref[idx]
            from_prev, from_next = _shifted(dc, edges)
            dh_ref[idx] = (from_next * cw[0:1, :] + dc * cw[1:2, :] + from_prev * cw[2:3, :]).astype(BF16)
            dcw_ref[idx, 0:1, :] = jnp.sum(prev * dc, axis=0, keepdims=True)
            dcw_ref[idx, 1:2, :] = jnp.sum(hv * dc, axis=0, keepdims=True)
            dcw_ref[idx, 2:3, :] = jnp.sum(nxt * dc, axis=0, keepdims=True)
            dcw_ref[idx, 3:4, :] = jnp.sum(dc, axis=0, keepdims=True)
            dcw_ref[idx, 4:8, :] = jnp.zeros((4, tn), F32)

    return pl.pallas_call(
        body, name=name, grid=(fh // tn,),
        in_specs=[pl.BlockSpec((2, s, tn), lambda t: (0, 0, t)), pl.BlockSpec((s, tn), lambda t: (0, t)),
                  pl.BlockSpec((2, 8, tn), lambda t: (0, 0, t))],
        out_specs=[pl.BlockSpec((2, s, tn), lambda t: (0, 0, t)), pl.BlockSpec((2, 8, tn), lambda t: (0, 0, t))],
        out_shape=[jax.ShapeDtypeStruct((2, s, fh), BF16), jax.ShapeDtypeStruct((2, 8, fh), F32)],
        compiler_params=_params("parallel"))(h, dact, cwb)


def ffn_dx(name, dh, w_up):
    _, s, fh = dh.shape
    dm, quarter = w_up.shape[2], w_up.shape[3]
    tk = _tile(quarter, 2816)
    per = quarter // tk
    tm, tn = _tile(s, 1024), _tile(dm, 1024)
    grid = (s // tm, dm // tn, 4 * per)
    a_spec = pl.BlockSpec((None, tm, tk), lambda i, j, k: (k // (2 * per), i, k % (2 * per)))
    b_spec = pl.BlockSpec((None, None, tn, tk), lambda i, j, k: (k // (2 * per), (k // per) % 2, j, k % per))
    o_spec = pl.BlockSpec((tm, tn), lambda i, j, k: (i, j))
    return _matmul_call(name, NT, grid, a_spec, b_spec, o_spec, jax.ShapeDtypeStruct((s, dm), F32), (tm, tn))(dh, w_up)


def ffn_dw_up(name, xn, dh):
    _, s, fh = dh.shape
    dm = xn.shape[1]
    quarter = fh // 2
    tn = _tile(quarter, 1536)
    per = quarter // tn
    tm, tk = _tile(dm, 1024), _tile(s, 2048)
    grid = (dm // tm, 4 * per, s // tk)
    a_spec = pl.BlockSpec((tk, tm), lambda i, j, k: (k, i))
    b_spec = pl.BlockSpec((None, tk, tn), lambda i, j, k: (j // (2 * per), k, j % (2 * per)))
    o_spec = pl.BlockSpec((None, None, tm, tn), lambda i, j, k: (j // (2 * per), (j // per) % 2, i, j % per))
    return _matmul_call(name, TN, grid, a_spec, b_spec, o_spec,
                        jax.ShapeDtypeStruct((2, 2, dm, quarter), F32), (tm, tn))(xn, dh)


def loss_head(name, y, target):
    s, dm = y.shape
    ts = _rows(s, dm)

    def body(y_ref, t_ref, loss_ref, dy_ref, acc):
        i = pl.program_id(0)

        @pl.when(i == 0)
        def _():
            acc[...] = jnp.zeros_like(acc)

        err = y_ref[...] - t_ref[...]
        dy_ref[...] = err * (1.0 / dm)
        acc[...] += jnp.sum(err * err, axis=0, keepdims=True)

        @pl.when(i == s // ts - 1)
        def _():
            loss_ref[...] = jnp.broadcast_to(jnp.sum(acc[...], axis=-1, keepdims=True) * (0.5 / dm), (1, LANES))

    blk = pl.BlockSpec((ts, dm), lambda i: (i, 0))
    loss, dy = pl.pallas_call(
        body, name=name, grid=(s // ts,), in_specs=[blk, blk],
        out_specs=[pl.BlockSpec((1, LANES), lambda i: (0, 0)), blk],
        out_shape=[jax.ShapeDtypeStruct((1, LANES), F32), jax.ShapeDtypeStruct((s, dm), F32)],
        scratch_shapes=[pltpu.VMEM((1, dm), F32)], compiler_params=_params("arbitrary"))(y, target)
    return loss[0, 0], dy


def attn_fwd_c(name, qc, kvc, krc, scale):
    s = qc.shape[0]
    tq = ATTN_STEP_ROWS

    def body(qn_ref, qr_ref, kn_ref, kr_ref, v_ref, o_ref, lse_ref):
        for r0 in range(0, tq, ATTN_TQ):
            rows = slice(r0, r0 + ATTN_TQ)
            sc = (_dot(qn_ref[rows, :], kn_ref[...], NT) + _dot(qr_ref[rows, :], kr_ref[...], NT)) * scale
            m = jnp.max(sc, axis=-1, keepdims=True)
            e = jnp.exp(sc - m)
            l = jnp.sum(e, axis=-1, keepdims=True)
            o_ref[rows, :] = _dot(e.astype(BF16), v_ref[...]) / l
            lse_ref[rows, :] = jnp.broadcast_to(m + jnp.log(l), (ATTN_TQ, HEAD))

    qb = lambda off: pl.BlockSpec((tq, HEAD), lambda h, i: (i, off + h))
    kb = lambda f: pl.BlockSpec((s, HEAD), f)
    out_blk = pl.BlockSpec((tq, HEAD), lambda h, i: (i, h))
    return pl.pallas_call(
        body, name=name, grid=(C_HEADS, s // tq),
        in_specs=[qb(0), qb(C_HEADS), kb(lambda h, i: (0, 2 * h)), kb(lambda h, i: (0, 0)), kb(lambda h, i: (0, 2 * h + 1))],
        out_specs=[out_blk, out_blk], out_shape=[jax.ShapeDtypeStruct((s, C_W), F32)] * 2,
        compiler_params=_params("parallel", "parallel"))(qc, qc, kvc, krc, kvc)


def attn_bwd_c(name, qc, kvc, krc, lse, o, do, scale):
    s = qc.shape[0]
    tq = ATTN_STEP_ROWS

    def body(qn_ref, qr_ref, kn_ref, kr_ref, v_ref, lse_ref, o_ref, do_ref, dqn_ref, dqr_ref, dkv_ref, dkr_ref):
        h, i = pl.program_id(0), pl.program_id(1)

        @pl.when(i == 0)
        def _():
            dkv_ref[...] = jnp.zeros_like(dkv_ref)

        @pl.when((i == 0) & (h == 0))
        def _():
            dkr_ref[...] = jnp.zeros_like(dkr_ref)

        for r0 in range(0, tq, ATTN_TQ):
            rows = slice(r0, r0 + ATTN_TQ)
            sc = (_dot(qn_ref[rows, :], kn_ref[...], NT) + _dot(qr_ref[rows, :], kr_ref[...], NT)) * scale
            pr = jnp.exp(sc - lse_ref[rows, 0:1])
            dov = do_ref[rows, :]
            delta = jnp.sum(dov.astype(F32) * o_ref[rows, :], axis=-1, keepdims=True)
            dp = _dot(dov, v_ref[...], NT)
            ds = (pr * (dp - delta) * scale).astype(BF16)
            dkv_ref[:, HEAD:] += _dot(pr.astype(BF16), dov, TN)
            dqn_ref[rows, :] = _dot(ds, kn_ref[...])
            dqr_ref[rows, :] = _dot(ds, kr_ref[...])
            dkv_ref[:, :HEAD] += _dot(ds, qn_ref[rows, :], TN)
            dkr_ref[...] += _dot(ds, qr_ref[rows, :], TN)

    qb = lambda off: pl.BlockSpec((tq, HEAD), lambda h, i: (i, off + h))
    kb = lambda f: pl.BlockSpec((s, HEAD), f)
    hq = pl.BlockSpec((tq, HEAD), lambda h, i: (i, h))
    kn_map, v_map, kr_map = (lambda h, i: (0, 2 * h)), (lambda h, i: (0, 2 * h + 1)), (lambda h, i: (0, 0))
    return pl.pallas_call(
        body, name=name, grid=(C_HEADS, s // tq),
        in_specs=[qb(0), qb(C_HEADS), kb(kn_map), kb(kr_map), kb(v_map), hq, hq, hq],
        out_specs=[hq, hq, pl.BlockSpec((s, 2 * HEAD), lambda h, i: (0, h)), kb(kr_map)],
        out_shape=[jax.ShapeDtypeStruct((s, C_W), F32)] * 2
        + [jax.ShapeDtypeStruct((s, 2 * C_W), F32), jax.ShapeDtypeStruct((s, HEAD), F32)],
        compiler_params=_params("arbitrary", "arbitrary"))(qc, qc, kvc, krc, kvc, lse, o, do)


def _layer_fwd(x, p, tabs):
    sv = _mixers_ab_fwd(x, p, tabs)
    x1 = _mixer_c_out_fwd(x, sv, p, tabs)
    return _ffn_fwd(x1, p, sv), sv


def _mixers_ab_fwd(x, p, tabs):
    sv = {"x": x}
    hn = rmsnorm_fwd("attn_norm", x, p["attn_norm"], x.shape[1], BF16)
    proj = matmul("in_proj", hn, p["w_in_t"], "nt", F32, tm=256)
    qa, ka, av, qb, kb, cqn, ckvn = mixer_prep("mixer_prep", proj, p, tabs)
    sv.update(hn=hn, proj=proj, qa=qa, ka=ka, av=av, qb=qb, kb=kb, cqn=cqn, ckvn=ckvn)

    ya, lse_a = attn_fwd("a_attn", [(qa, 0, ka, A_HEADS // A_KV)], av, A_HEADS // A_KV, A_HEADS, HEAD ** -0.5)
    yb, lse_b = dilated_fwd("b_dilated", qb, kb, proj, O_BV // HEAD)
    sv.update(ya=ya, lse_a=lse_a, yb=yb, lse_b=lse_b)
    return sv


def _mixer_c_out_fwd(x, sv, p, tabs):
    _mixer_c_fwd(sv, p, tabs)
    return _out_fwd(x, sv, p)


def _mixer_c_fwd(sv, p, tabs):
    qc_raw = matmul("c_uq", sv["cqn"], p["w_uq_t"], "nt", F32)
    qc = rope("c_q_rope", qc_raw, tabs["c"][0], BF16, plain_heads=C_HEADS)
    kvc = matmul("c_ukv", sv["ckvn"], p["w_ukv"], "nn", BF16)
    ckr = jnp.pad(sv["proj"][:, O_CKR:], ((0, 0), (0, HEAD - C_ROPE)))
    krc = rope("c_k_rope", ckr, tabs["c"][0], BF16)
    c_scale = (HEAD + C_ROPE) ** -0.5
    yc, lse_c = attn_fwd_c("c_attn", qc, kvc, krc, c_scale)
    sv.update(qc=qc, kvc=kvc, krc=krc, yc=yc, lse_c=lse_c)


def _out_fwd(x, sv, p, after=0.0):
    y = out_norms("out_norms", sv["ya"], sv["yb"], sv["yc"], p["out_norm"] + after)
    x1 = matmul("out_proj", y, p["w_out"], "nn", F32, add=x)
    sv.update(y=y, x1=x1)
    return x1


def _ffn_fwd(x1, p, sv):
    _ffn_up_fwd(x1, p, sv)
    return _ffn_down_fwd(x1, p, sv)


def _ffn_up_fwd(x1, p, sv):
    xn = rmsnorm_fwd("ffn_norm", x1, p["ffn_norm"], x1.shape[1], BF16)
    h, act = ffn_up("ffn_up", xn, p["w_up"], p["cwb"])
    sv.update(xn=xn, h=h, act=act)


def _ffn_down_fwd(x1, p, sv):
    return matmul("ffn_down", sv["act"], p["w_down"], "nn", F32, add=x1)


def _layer_bwd(dx2, sv, p, tabs):
    dxn, g = _ffn_bwd(dx2.astype(BF16), sv, p)
    dx1, dy, g_out = _out_proj_bwd(dxn, dx2, sv, p)
    dhn, g_mix = _mixers_bwd(dy, sv, p, tabs)
    dx, _, g_norm = _attn_norm_bwd(dhn, dx1, sv, p)
    return dx, {**g, **g_out, **g_mix, **g_norm}


def _ffn_bwd(dx2b, sv, p, after=0.0):
    g = {}
    dact = matmul("ffn_down_dx", dx2b, p["w_down"], "nt", F32)
    g["w_down"] = matmul("ffn_down_dw", sv["act"], dx2b, "tn", F32)
    dh, g["cwb"] = ffn_gate_bwd("ffn_gate_bwd", sv["h"], dact, p["cwb"] + after)
    dxn = ffn_dx("ffn_up_dx", dh, p["w_up"])
    g["w_up"] = ffn_dw_up("ffn_up_dw", sv["xn"], dh)
    return dxn, g


def _out_proj_bwd(dxn, dx2, sv, p, after=0.0):
    g = {}
    dm = dx2.shape[1]
    dx1, dx1b, g["ffn_norm"] = rmsnorm_bwd("ffn_norm_bwd", dxn, sv["x1"], p["ffn_norm"] + after, dm, (F32, BF16), add=dx2)
    dy = matmul("out_proj_dx", dx1b, p["w_out"], "nt", F32)
    g["w_out"] = matmul("out_proj_dw", sv["y"], dx1b, "tn", F32)
    return dx1, dy, g


def _attn_norm_bwd(dhn, dx1, sv, p, after=0.0):
    dx, dxb, d_gain = rmsnorm_bwd("attn_norm_bwd", dhn, sv["x"], p["attn_norm"] + after, dhn.shape[1], (F32, BF16), add=dx1)
    return dx, dxb, {"attn_norm": d_gain}


def _mixers_bwd(dy, sv, p, tabs):
    part, g = _mixer_c_bwd(dy, sv, p, tabs)
    dhn, g_ab = _mixers_ab_bwd(part, sv, p, tabs)
    return dhn, {**g, **g_ab}


def _mixer_c_bwd(dy, sv, p, tabs, after=0.0):
    g = {}
    dya, dyb, dyc, g["out_norm"] = out_norms_bwd("out_norms_bwd", dy, sv["ya"], sv["yb"], sv["yc"], p["out_norm"] + after)

    c_scale = (HEAD + C_ROPE) ** -0.5
    dqn, dqr, dkv, dkr = attn_bwd_c("c_attn_bwd", sv["qc"], sv["kvc"], sv["krc"], sv["lse_c"], sv["yc"], dyc, c_scale)
    dqc = jnp.concatenate([dqn.astype(BF16), rope("c_q_rope_bwd", dqr, tabs["c"][1], BF16)], axis=1)
    dcqn = matmul("c_uq_dx", dqc, p["w_uq_t"], "nn", F32)
    g["w_uq"] = matmul("c_uq_dw", dqc, sv["cqn"], "tn", F32)
    dckvn = matmul("c_ukv_dx", dkv, p["w_ukv"], "nt", F32)
    g["w_ukv"] = matmul("c_ukv_dw", sv["ckvn"], dkv, "tn", F32)
    dckr = rope("c_k_rope_bwd", dkr, tabs["c"][1], BF16)
    return (dya, dyb, dcqn, dckvn, dckr), g


def _mixers_ab_bwd(part, sv, p, tabs, after=0.0):
    g = {}
    dya, dyb, dcqn, dckvn, dckr = part

    dqb, dkb, dvb = dilated_bwd("b_dilated_bwd", sv["qb"], sv["kb"], sv["proj"], O_BV // HEAD, dyb + after, sv["yb"],
                                sv["lse_b"])

    kg = A_HEADS // A_KV
    (dqa,), (dka,), dva = attn_bwd("a_attn_bwd", [(sv["qa"], 0, sv["ka"], kg)], sv["av"], kg, sv["lse_a"], sv["ya"],
                                   dya, A_HEADS, HEAD ** -0.5)

    dproj, g["a_q_norm"], g["a_k_norm"], g["c_q_norm"], g["c_kv_norm"] = mixer_prep_bwd(
        "mixer_prep_bwd", sv["proj"], p, tabs, dqa, dka, dva, dqb, dkb, dvb, dcqn, dckvn, dckr)
    dhn = matmul("in_proj_dx", dproj, p["w_in_t"], "nn", F32, tm=512)
    g["w_in"] = matmul("in_proj_dw", dproj, sv["hn"], "tn", F32, tn=512, tk=512)
    return dhn, g


def _local_step(x, target, layers, final_norm, tabs):
    saved = []
    for p in layers:
        x, sv = _layer_fwd(x, p, tabs)
        saved.append(sv)
    dm = x.shape[1]
    yf = rmsnorm_fwd("final_norm", x, final_norm, dm, F32)
    loss, dyf = loss_head("loss_head", yf, target)
    dx, d_final = rmsnorm_bwd("final_norm_bwd", dyf, x, final_norm, dm, (F32,))
    grads = [None] * len(layers)
    for l in reversed(range(len(layers))):
        dx, grads[l] = _layer_bwd(dx, saved[l], layers[l], tabs)
    return loss, dx, grads, d_final


HBM = pl.BlockSpec(memory_space=pl.ANY)


def _place():
    x, y, c = lax.axis_index("x"), lax.axis_index("y"), lax.axis_index("c")
    others = [(1 - x, y), (x, 1 - y), (1 - x, 1 - y)]
    return x, y, c, 2 * x + y, others


def _remote(src, dst, send_sem, recv_sem, device):
    return pltpu.make_async_remote_copy(src_ref=src, dst_ref=dst, send_sem=send_sem, recv_sem=recv_sem,
                                        device_id=device, device_id_type=MESH)


def gather_small(name, small):
    def body(in_ref, out_ref, send_sems, recv_sems, local_sem):
        x, y, c, chip, others = _place()
        mine = pltpu.make_async_copy(in_ref, out_ref.at[chip], local_sem)
        mine.start()
        copies = []
        for k, (ox, oy) in enumerate(others):
            cp = _remote(in_ref, out_ref.at[chip], send_sems.at[k], recv_sems.at[k], (ox, oy, c))
            cp.start()
            copies.append(cp)
        for k, (ox, oy) in enumerate(others):
            landed = out_ref.at[2 * ox + oy]
            _remote(landed, landed, send_sems.at[k], recv_sems.at[k], (ox, oy, c)).wait_recv()
        for cp in copies:
            cp.wait_send()
        mine.wait()

    return pl.pallas_call(
        body, name=name, in_specs=[HBM], out_specs=HBM, out_shape=jax.ShapeDtypeStruct((4,) + small.shape, small.dtype),
        scratch_shapes=[pltpu.SemaphoreType.DMA((3,)), pltpu.SemaphoreType.DMA((3,)), pltpu.SemaphoreType.DMA],
        compiler_params=pltpu.CompilerParams(has_side_effects=True))(small)


IN_HBM = pl.BlockSpec(memory_space=pltpu.HBM)
SEMS = pl.BlockSpec(memory_space=pltpu.SEMAPHORE)
DATAFLOW = pltpu.SideEffectType.DATAFLOW_SIDE_EFFECTING


BF16_ROWS_PER_TILE = 16


def _halves_by_rows(shape):
    return (shape[-2] // 2) % BF16_ROWS_PER_TILE == 0


def _half(shape, core):
    if _halves_by_rows(shape):
        size = shape[-2] // 2
        return (pl.ds(core * size, size), slice(None))
    size = shape[-1] // 2
    return (slice(None), pl.ds(core * size, size))


def _half_shape(shape):
    r, c = shape[-2:]
    return (r // 2, c) if _halves_by_rows(shape) else (r, c // 2)


def _gather_plan(bufs):
    x, y, c, chip, others = _place()
    plan = []
    for t, ref in enumerate(bufs):
        mine = _half(ref.shape, c)
        for k, (ox, oy) in enumerate(others):
            plan.append((3 * t + k, (ox, oy, c), ref.at[(chip,) + mine], ref.at[(chip,) + mine],
                         ref.at[(2 * ox + oy,) + mine]))
    return plan


def _exchange_plan(bufs):
    x, y, c, chip, others = _place()
    n = len(bufs) // 2
    plan = []
    for t in range(n):
        for k, (ox, oy) in enumerate(others):
            plan.append((3 * t + k, (ox, oy, c), bufs[t].at[2 * ox + oy], bufs[n + t].at[chip], bufs[n + t].at[2 * ox + oy]))
    return plan


def _sibling_plan(bufs):
    x, y, c, chip, others = _place()
    n = len(bufs) // 2
    plan = []
    for t in range(n):
        plan.append((t, (x, y, 1 - c), bufs[t].at[(slice(None),) + _half(bufs[t].shape, 1 - c)], bufs[n + t], bufs[n + t]))
    return plan


def _forward_plan(bufs):
    x, y, c, chip, others = _place()
    plan = []
    for t, ref in enumerate(bufs):
        mine, theirs = _half(ref.shape, c), _half(ref.shape, 1 - c)
        for k, (ox, oy) in enumerate(others):
            slot = 2 * ox + oy
            plan.append((3 * t + k, (x, y, 1 - c), ref.at[(slot,) + mine], ref.at[(slot,) + mine], ref.at[(slot,) + theirs]))
    return plan


def _join_plan(bufs):
    x, y, c, chip, others = _place()
    return [(t, (x, y, 1 - c), ref.at[_half(ref.shape, c)], ref.at[_half(ref.shape, c)], ref.at[_half(ref.shape, 1 - c)])
            for t, ref in enumerate(bufs)]


PLAN_COPIES = {_gather_plan: lambda n: 3 * n, _exchange_plan: lambda n: 3 * (n // 2), _sibling_plan: lambda n: n // 2,
               _forward_plan: lambda n: 3 * n, _join_plan: lambda n: n}


def split_start(name, bufs, plan_of, after):
    n = len(bufs)

    def body(*refs):
        ins = refs[:n]
        send_sems, recv_sems = refs[n + 1], refs[n + 2]
        token = refs[-1]
        for idx, peer, src, dst, _ in plan_of(ins):
            _remote(src, dst, send_sems.at[idx], recv_sems.at[idx], peer).start()
        token[...] = jnp.zeros_like(token)

    copies = PLAN_COPIES[plan_of](n)
    res = pl.pallas_call(
        body, name=name, in_specs=[IN_HBM] * n + [HBM],
        out_specs=(SEMS, SEMS) + (IN_HBM,) * n + (pl.BlockSpec(memory_space=pltpu.VMEM),),
        out_shape=(pltpu.SemaphoreType.DMA((copies,)), pltpu.SemaphoreType.DMA((copies,)))
        + tuple(pltpu.HBM(b.shape, b.dtype) for b in bufs) + (jax.ShapeDtypeStruct((8, LANES), F32),),
        input_output_aliases={t: 2 + t for t in range(n)},
        compiler_params=pltpu.CompilerParams(has_side_effects=DATAFLOW))(
            *[pltpu.with_memory_space_constraint(b, pltpu.HBM) for b in bufs], after)
    return res[0], res[1], list(res[2:2 + n]), res[-1]


def split_wait(name, send_sems, recv_sems, flying, plan_of, after):
    n = len(flying)

    def body(*refs):
        ins = refs[:n]
        send_ref, recv_ref = refs[n], refs[n + 1]
        for idx, peer, src, dst, landing in plan_of(ins):
            _remote(src, dst, send_ref.at[idx], recv_ref.at[idx], peer).wait_send()
            _remote(landing, landing, send_ref.at[idx], recv_ref.at[idx], peer).wait_recv()

    return list(pl.pallas_call(
        body, name=name, in_specs=[IN_HBM] * n + [SEMS, SEMS] + [HBM] * len(after), out_specs=(IN_HBM,) * n,
        out_shape=tuple(pltpu.HBM(b.shape, b.dtype) for b in flying),
        input_output_aliases={t: t for t in range(n)},
        compiler_params=pltpu.CompilerParams(has_side_effects=DATAFLOW))(*flying, send_sems, recv_sems, *after))


def _half_block(shape, tr):
    r, c = shape[-2:]
    if _halves_by_rows(shape):
        per = r // 2 // tr
        return tr, c, lambda i, core: (core * per + i, 0)
    return tr, c // 2, lambda i, core: (i, core)


def add_half(name, place, grad, theirs):
    hr, hc = _half_shape(grad.shape)
    tr = _rows(hr, hc)
    br, bc, at = _half_block(grad.shape, tr)

    def body(place_ref, g_ref, t_ref, o_ref):
        o_ref[...] = (g_ref[...] + t_ref[...]).astype(o_ref.dtype)

    whole = pl.BlockSpec((None, br, bc), lambda j, i, pr: (j, i, 0))
    return pl.pallas_call(
        body, name=name,
        grid_spec=pltpu.PrefetchScalarGridSpec(
            num_scalar_prefetch=1, grid=(4, hr // tr),
            in_specs=[pl.BlockSpec((None, br, bc), lambda j, i, pr: (j,) + at(i, pr[0])), whole],
            out_specs=whole),
        out_shape=jax.ShapeDtypeStruct((4, hr, hc), BF16),
        compiler_params=_params("parallel", "parallel"))(place, grad, theirs)


def sum_chips(name, place, own, parts, shard_shape):
    _, hr, hc = parts.shape
    tr = _rows(hr, hc)
    br, bc, at = _half_block(shard_shape, tr)

    def body(place_ref, own_ref, p1, p2, p3, o_ref):
        o_ref[...] = own_ref[...].astype(F32) + p1[...].astype(F32) + p2[...].astype(F32) + p3[...].astype(F32)

    def slot(k):
        return pl.BlockSpec((None, br, bc), lambda i, pr: (lax.rem(pr[1] + k, 4), i, 0))

    return pl.pallas_call(
        body, name=name,
        grid_spec=pltpu.PrefetchScalarGridSpec(
            num_scalar_prefetch=1, grid=(hr // tr,), in_specs=[slot(0), slot(1), slot(2), slot(3)],
            out_specs=pl.BlockSpec((br, bc), lambda i, pr: at(i, pr[0]))),
        out_shape=jax.ShapeDtypeStruct(tuple(shard_shape), F32),
        compiler_params=_params("parallel"))(place, own, parts, parts, parts)


def allreduce_small(name, buf):
    rows = buf.shape[0]

    def body(in_ref, out_ref, slots, send_sems, recv_sems):
        x, y, c, _, _ = _place()
        me = 4 * x + 2 * y + c
        slots[me] = in_ref[...]
        peers = []
        for k in range(1, 8):
            px = 1 - x if k & 4 else x
            py = 1 - y if k & 2 else y
            pc = 1 - c if k & 1 else c
            peers.append((px, py, pc))
        copies = []
        for k, peer in enumerate(peers):
            cp = _remote(in_ref, slots.at[me], send_sems.at[k], recv_sems.at[k], peer)
            cp.start()
            copies.append(cp)
        for k, (px, py, pc) in enumerate(peers):
            slot = slots.at[4 * px + 2 * py + pc]
            _remote(slot, slot, send_sems.at[k], recv_sems.at[k], (px, py, pc)).wait_recv()
        for cp in copies:
            cp.wait_send()
        acc = slots[0]
        for d in range(1, 8):
            acc = acc + slots[d]
        out_ref[...] = acc

    vm = pl.BlockSpec(memory_space=pltpu.VMEM)
    return pl.pallas_call(
        body, name=name, in_specs=[vm], out_specs=vm, out_shape=jax.ShapeDtypeStruct(buf.shape, F32),
        scratch_shapes=[pltpu.VMEM((8, rows, LANES), F32), pltpu.SemaphoreType.DMA((7,)), pltpu.SemaphoreType.DMA((7,))],
        compiler_params=pltpu.CompilerParams(has_side_effects=True, vmem_limit_bytes=VMEM_LIMIT))(buf)


def cast_layer(name, place, w, layer):
    _, r, cols = w.shape
    tr = _rows(r, cols)

    def body(place_ref, w_ref, o_ref):
        o_ref[...] = w_ref[...].astype(BF16)

    return pl.pallas_call(
        body, name=name,
        grid_spec=pltpu.PrefetchScalarGridSpec(
            num_scalar_prefetch=1, grid=(r // tr,),
            in_specs=[pl.BlockSpec((None, tr, cols), lambda i, pr: (layer, i, 0))],
            out_specs=pl.BlockSpec((None, tr, cols), lambda i, pr: (pr[1], i, 0))),
        out_shape=jax.ShapeDtypeStruct((4, r, cols), BF16),
        compiler_params=_params("parallel"))(place, w)


def _adamw_math(g, w, m, v):
    m = ADAM_B1 * m + (1.0 - ADAM_B1) * g
    v = ADAM_B2 * v + (1.0 - ADAM_B2) * (g * g)
    m_hat = m / (1.0 - ADAM_B1 ** ADAM_STEP)
    v_hat = v / (1.0 - ADAM_B2 ** ADAM_STEP)
    delta = -ADAM_LR * (m_hat / (jnp.sqrt(v_hat) + ADAM_EPS) + ADAM_WD * w)
    return delta, m, v


def adamw_layer(name, layer, g, w, m, v, prev=None):
    nl, r, cols = w.shape
    tr = _rows(r, cols, 8)

    def body(*refs):
        g_ref, w_ref, m_ref, v_ref = refs[:4]
        og, od, om, ov = refs[-4:]
        gv = g_ref[...]
        delta, m2, v2 = _adamw_math(gv, w_ref[...], m_ref[...], v_ref[...])
        og[...] = gv
        od[...] = delta
        om[...] = m2
        ov[...] = v2

    lay = pl.BlockSpec((None, tr, cols), lambda i: (layer, i, 0))
    ins = [g, w, m, v] + (list(prev) if prev is not None else [])
    return pl.pallas_call(
        body, name=name, grid=(r // tr,),
        in_specs=[pl.BlockSpec((tr, cols), lambda i: (i, 0)), lay, lay, lay] + ([HBM] * 4 if prev is not None else []),
        out_specs=[lay] * 4, out_shape=[jax.ShapeDtypeStruct((nl, r, cols), F32)] * 4,
        input_output_aliases={4 + k: k for k in range(4)} if prev is not None else {},
        compiler_params=_params("parallel"))(*ins)


def adamw_packed(name, g, w, m, v):
    def body(g_ref, w_ref, m_ref, v_ref, od, om, ov):
        delta, m2, v2 = _adamw_math(g_ref[...], w_ref[...], m_ref[...], v_ref[...])
        od[...] = delta
        om[...] = m2
        ov[...] = v2

    vm = pl.BlockSpec(memory_space=pltpu.VMEM)
    return pl.pallas_call(
        body, name=name, in_specs=[vm] * 4, out_specs=[vm] * 3, out_shape=[jax.ShapeDtypeStruct(g.shape, F32)] * 3,
        compiler_params=pltpu.CompilerParams(vmem_limit_bytes=VMEM_LIMIT))(g, w, m, v)


def _pack(arrays):
    flat = jnp.concatenate([a.reshape(-1) for a in arrays])
    pad = (-flat.shape[0]) % (8 * LANES)
    return jnp.pad(flat, (0, pad)).reshape(-1, LANES)


def _unpack(buf, shapes):
    flat = buf.reshape(-1)
    out, off = [], 0
    for shp in shapes:
        size = int(np.prod(shp))
        out.append(flat[off:off + size].reshape(shp))
        off += size
    return out


MIXER_W = ("w_in", "w_uq", "w_ukv", "w_out")
FFN_W = ("w_up", "w_down")
BIG = MIXER_W + FFN_W
TRANSPOSED = ("w_in", "w_uq")
COLUMN_CUT = ("w_ukv", "w_up")
SMALL = ("attn_norm", "a_q_norm", "a_k_norm", "c_q_norm", "c_kv_norm", "out_norm", "ffn_norm", "conv_b", "final_norm")
WEIGHTS = ("attn_norm", "w_in", "a_q_norm", "a_k_norm", "c_q_norm", "c_kv_norm", "w_uq", "w_ukv", "out_norm", "w_out",
           "ffn_norm", "w_up", "conv_w", "conv_b", "w_down", "final_norm")
INPUTS = ("x",) + WEIGHTS + ("loss_target",) + tuple("m_" + n for n in WEIGHTS) + tuple("v_" + n for n in WEIGHTS)


def _columns(g):
    return jnp.transpose(g, (1, 0, 2)).reshape(g.shape[1], 4 * g.shape[2])


def _uncolumns(w):
    r, c4 = w.shape
    return jnp.transpose(w.reshape(r, 4, c4 // 4), (1, 0, 2))


def _compute_layouts(full, conv_w, small, layer):
    p = {n: small[n][layer] for n in SMALL if n != "final_norm"}
    if "w_in" in full:
        p["w_in_t"] = full["w_in"].reshape(-1, full["w_in"].shape[2])
    if "w_uq" in full:
        uq_t = full["w_uq"].reshape(C_HEADS, HEAD + C_ROPE, C_RANK)
        p["w_uq_t"] = jnp.concatenate([uq_t[:, :HEAD].reshape(C_W, C_RANK),
                                       jnp.pad(uq_t[:, HEAD:], ((0, 0), (0, HEAD - C_ROPE), (0, 0))).reshape(C_W, C_RANK)], axis=0)
        p["w_ukv"] = _columns(full["w_ukv"])
        p["w_out"] = full["w_out"].reshape(-1, full["w_out"].shape[2])
    if "w_down" in full:
        p["w_down"] = full["w_down"].reshape(-1, full["w_down"].shape[2])
    if "w_up" in full:
        up = full["w_up"]
        p["w_up"] = up.reshape(2, 2, up.shape[1], up.shape[2])
        fh = conv_w.shape[1] // 2
        taps = jnp.transpose(conv_w.reshape(3, 2, fh), (1, 0, 2))
        p["cwb"] = jnp.concatenate([taps, small["conv_b"][layer].reshape(2, 1, fh), jnp.zeros((2, 4, fh), F32)], axis=1)
    return p


def _shard_layout(name, g):
    if name == "w_in":
        return g.reshape(4, -1, g.shape[1])
    if name == "w_uq":
        uq_t = jnp.concatenate([g[:C_W].reshape(C_HEADS, HEAD, C_RANK),
                                g[C_W:].reshape(C_HEADS, HEAD, C_RANK)[:, :C_ROPE]], axis=1)
        return uq_t.reshape(4, -1, C_RANK)
    if name == "w_ukv":
        return _uncolumns(g)
    if name == "w_up":
        return g.reshape(4, g.shape[2], g.shape[3])
    return g.reshape(4, -1, g.shape[1])


def _conv_grads(dcwb):
    fh = dcwb.shape[2]
    return jnp.transpose(dcwb[:, 0:3, :], (1, 0, 2)).reshape(3, 2 * fh), dcwb[:, 3, :].reshape(2 * fh)


def _shard_layouts(g):
    return ({n: _shard_layout(n, g[n]) for n in BIG},) + _conv_grads(g["cwb"])


def kernel(x, attn_norm, w_in, a_q_norm, a_k_norm, c_q_norm, c_kv_norm, w_uq, w_ukv, out_norm, w_out, ffn_norm, w_up, conv_w, conv_b, w_down, final_norm, loss_target, m_attn_norm, m_w_in, m_a_q_norm, m_a_k_norm, m_c_q_norm, m_c_kv_norm, m_w_uq, m_w_ukv, m_out_norm, m_w_out, m_ffn_norm, m_w_up, m_conv_w, m_conv_b, m_w_down, m_final_norm, v_attn_norm, v_w_in, v_a_q_norm, v_a_k_norm, v_c_q_norm, v_c_kv_norm, v_w_uq, v_w_ukv, v_out_norm, v_w_out, v_ffn_norm, v_w_up, v_conv_w, v_conv_b, v_w_down, v_final_norm):
    a = dict(zip(INPUTS, (x, attn_norm, w_in, a_q_norm, a_k_norm, c_q_norm, c_kv_norm, w_uq, w_ukv, out_norm, w_out, ffn_norm, w_up, conv_w, conv_b, w_down, final_norm, loss_target, m_attn_norm, m_w_in, m_a_q_norm, m_a_k_norm, m_c_q_norm, m_c_kv_norm, m_w_uq, m_w_ukv, m_out_norm, m_w_out, m_ffn_norm, m_w_up, m_conv_w, m_conv_b, m_w_down, m_final_norm, v_attn_norm, v_w_in, v_a_q_norm, v_a_k_norm, v_c_q_norm, v_c_kv_norm, v_w_uq, v_w_ukv, v_out_norm, v_w_out, v_ffn_norm, v_w_up, v_conv_w, v_conv_b, v_w_down, v_final_norm)))
    nl = w_in.shape[0]
    seq = x.shape[1]
    for n in TRANSPOSED:
        for kind in ("", "m_", "v_"):
            a[kind + n] = jnp.swapaxes(a[kind + n], 1, 2)
    chip = 2 * lax.axis_index("x") + lax.axis_index("y")
    place = jnp.stack([lax.axis_index("c"), chip]).astype(jnp.int32)
    tabs = _rope_tables(seq)

    assert nl == 2
    quarter = conv_w.shape[2]
    taps = jnp.pad(conv_w, ((0, 0), (0, 8 - conv_w.shape[1]), (0, 0))).reshape(nl * 8, quarter)
    taps_full = gather_small("gather_taps", taps).reshape(4, nl, 8, quarter)
    conv_full = [jnp.transpose(taps_full[:, l, 0:3, :], (1, 0, 2)).reshape(3, 4 * quarter) for l in range(nl)]
    flights = {}
    here, after = place, taps_full
    for l in range(nl):
        for key, names in ((f"in{l}", ("w_in",)), (f"mix{l}", MIXER_W[1:]), (f"up{l}", ("w_up",)), (f"down{l}", ("w_down",))):
            slots = [cast_layer(f"cast{l}_{n}", here, a[n], l) for n in names]
            send_sems, recv_sems, flying, token = split_start(f"gather_{key}_start", slots, _gather_plan, after)
            flights[key] = (names, send_sems, recv_sems, flying)
            here = after = place + token[0, 0].astype(jnp.int32)
    started = token[0, 0]

    def land(key, after):
        names, send_sems, recv_sems, flying = flights[key]
        landed = split_wait(f"gather_{key}_wait", send_sems, recv_sems, flying, _gather_plan, after)
        send_sems, recv_sems, passing, token = split_start(f"gather_{key}_pass_start", landed, _forward_plan, place)
        return (key, names, send_sems, recv_sems, passing), token

    def landed(passing, after, layer):
        key, names, send_sems, recv_sems, flying = passing
        full = split_wait(f"gather_{key}_pass_wait", send_sems, recv_sems, flying, _forward_plan, after)
        return _compute_layouts(dict(zip(names, full)), conv_full[layer], a, layer)

    h = x[0]
    layers, saved = [], []
    early = None
    for l in range(nl):
        if early is None:
            pass_in, token = land(f"in{l}", [taps_full])
            p = landed(pass_in, [token], l)
            p["attn_norm"] = p["attn_norm"] + started
        else:
            p = landed(early[0], [h], l)
        sv = _mixers_ab_fwd(h, p, tabs)
        if early is None:
            pass_mix, token = land(f"mix{l}", [sv["yb"]])
            p.update(landed(pass_mix, [token], l))
        else:
            p.update(landed(early[1], [sv["yb"]], l))
        _mixer_c_fwd(sv, p, tabs)
        pass_up, token = land(f"up{l}", [sv["yc"]])
        x1 = _out_fwd(h, sv, p, token[0, 0])
        p.update(landed(pass_up, [x1], l))
        _ffn_up_fwd(x1, p, sv)
        pass_down, token = land(f"down{l}", [sv["act"]])
        if l + 1 < nl:
            pass_in, _ = land(f"in{l + 1}", [token])
            pass_mix, _ = land(f"mix{l + 1}", [token])
            early = (pass_in, pass_mix)
        p.update(landed(pass_down, [token], l))
        h = _ffn_down_fwd(x1, p, sv)
        layers.append(p)
        saved.append(sv)
    dm = h.shape[1]
    yf = rmsnorm_fwd("final_norm", h, final_norm, dm, F32)
    loss_here, dyf = loss_head("loss_head", yf, loss_target[0])
    dx, dxb, d_final = rmsnorm_bwd("final_norm_bwd", dyf, h, final_norm, dm, (F32, BF16))

    def swap_begin(tag, names, g):
        g_list = [_shard_layout(n, g[n]) for n in names]
        theirs = [lax.empty((4,) + _half_shape(s.shape), s.dtype) for s in g_list]
        send_sems, recv_sems, flying, token = split_start(f"{tag}_sibling_start", g_list + theirs, _sibling_plan, place)
        return (names, send_sems, recv_sems, flying), token[0, 0]

    def exchange_begin(tag, swap, after):
        names, send_sems, recv_sems, flying = swap
        landed = split_wait(f"{tag}_sibling_wait", send_sems, recv_sems, flying, _sibling_plan, after)
        g_list, theirs = landed[:len(names)], landed[len(names):]
        sums = [add_half(f"{tag}_add_{n}", place, gr, t) for n, gr, t in zip(names, g_list, theirs)]
        parts = [lax.empty(s.shape, s.dtype) for s in sums]
        send_sems, recv_sems, flying, token = split_start(f"{tag}_chips_start", sums + parts, _exchange_plan, place)
        return (names, [gr.shape[1:] for gr in g_list], send_sems, recv_sems, flying), token

    def exchange_end(tag, flight, after):
        names, shard_shapes, send_sems, recv_sems, flying = flight
        landed = split_wait(f"{tag}_chips_wait", send_sems, recv_sems, flying, _exchange_plan, after)
        sums, parts = landed[:len(names)], landed[len(names):]
        halves = [sum_chips(f"{tag}_sum_{n}", place, s, p, shp) for n, s, p, shp in zip(names, sums, parts, shard_shapes)]
        send_sems, recv_sems, joining, token = split_start(f"{tag}_join_start", halves, _join_plan, place)
        return (names, send_sems, recv_sems, joining), token

    def joined(tag, joining, after):
        names, send_sems, recv_sems, flying = joining
        return dict(zip(names, split_wait(f"{tag}_join_wait", send_sems, recv_sems, flying, _join_plan, after)))

    grads, conv_grads, flights_f, flights_m = [None] * nl, [None] * nl, [None] * nl, [None] * nl
    after = 0.0
    for l in reversed(range(nl)):
        sv, p = saved[l], layers[l]
        dxn, g = _ffn_bwd(dxb, sv, p, after)
        swap_f, after = swap_begin(f"reduce{l}f", FFN_W, g)
        dx1, dy, g_out = _out_proj_bwd(dxn, dx, sv, p, after)
        part, g_c = _mixer_c_bwd(dy, sv, p, tabs)
        flights_f[l], token = exchange_begin(f"reduce{l}f", swap_f, [part[2]])
        dhn, g_ab = _mixers_ab_bwd(part, sv, p, tabs, token[0, 0])
        g.update(g_out)
        g.update(g_c)
        g.update(g_ab)
        swap_m, after = swap_begin(f"reduce{l}m", MIXER_W, g)
        dx, dxb, g_norm = _attn_norm_bwd(dhn, dx1, sv, p, after)
        g.update(g_norm)
        grads[l], conv_grads[l] = g, _conv_grads(g["cwb"])
        before = [dx]
        if l == 0:
            small_g = {n: jnp.stack([grads[k][n] for k in range(nl)]) for n in SMALL if n not in ("conv_b", "final_norm")}
            small_g["conv_b"] = jnp.stack([cg[1] for cg in conv_grads])
            small_g["final_norm"] = d_final
            conv_w_g = jnp.stack([cg[0] for cg in conv_grads])
            small_sum = allreduce_small("reduce_small", _pack([small_g[n] for n in SMALL] + [conv_w_g, loss_here.reshape(1)]))
            before.append(small_sum)
        flights_m[l], token = exchange_begin(f"reduce{l}m", swap_m, before)
        after = token[0, 0]
    grad_x = dx

    out = {}
    join_f, token = exchange_end("reduce1f", flights_f[1], [token])
    join_m, token = exchange_end("reduce1m", flights_m[1], [token])
    reduced1 = joined("reduce1f", join_f, [token])
    for n in FFN_W:
        out[n] = adamw_layer(f"adamw1_{n}", 1, reduced1[n], a[n], a["m_" + n], a["v_" + n])
    reduced1 = joined("reduce1m", join_m, [out[n][3] for n in FFN_W])
    for n in MIXER_W:
        out[n] = adamw_layer(f"adamw1_{n}", 1, reduced1[n], a[n], a["m_" + n], a["v_" + n])

    summed = _unpack(small_sum, [a[n].shape for n in SMALL] + [conv_w_g.shape, (1,)])
    small_g = dict(zip(SMALL, summed[:-2]))
    small_g["conv_w"] = lax.dynamic_slice_in_dim(summed[-2], chip * quarter, quarter, axis=2)
    loss = summed[-1][0]
    names = SMALL + ("conv_w",)
    shapes = [a[n].shape for n in names]
    d_small, m_small, v_small = adamw_packed(
        "adamw_small", _pack([small_g[n] for n in names]), _pack([a[n] for n in names]),
        _pack([a["m_" + n] for n in names]), _pack([a["v_" + n] for n in names]))

    join_f, token = exchange_end("reduce0f", flights_f[0], [out[n][3] for n in BIG] + [d_small])
    join_m, token = exchange_end("reduce0m", flights_m[0], [token])
    reduced0 = joined("reduce0f", join_f, [token])
    for n in FFN_W:
        out[n] = adamw_layer(f"adamw0_{n}", 0, reduced0[n], a[n], a["m_" + n], a["v_" + n], prev=out[n])
    reduced0 = joined("reduce0m", join_m, [out[n][3] for n in FFN_W])
    for n in MIXER_W:
        out[n] = adamw_layer(f"adamw0_{n}", 0, reduced0[n], a[n], a["m_" + n], a["v_" + n], prev=out[n])
    for n, d_, m_, v_ in zip(names, _unpack(d_small, shapes), _unpack(m_small, shapes), _unpack(v_small, shapes)):
        out[n] = (small_g[n], d_, m_, v_)
    for n in TRANSPOSED:
        out[n] = [jnp.swapaxes(o, 1, 2) for o in out[n]]

    return (loss, grad_x[None]) + tuple(out[n][k] for k in range(4) for n in WEIGHTS)
```

```python
import functools

import jax
import jax.numpy as jnp
import numpy as np
from jax import lax
from jax.experimental import pallas as pl
from jax.experimental.pallas import tpu as pltpu

F32 = jnp.float32
BF16 = jnp.bfloat16
MESH = pl.DeviceIdType.MESH

V7X_VMEM_BYTES = 64 * 1024 * 1024
VMEM_LIMIT = V7X_VMEM_BYTES - 6 * 1024 * 1024
LANES = 128
ELEMENTWISE_BLOCK_BYTES = 4 * 1024 * 1024

HEAD = 128
A_HEADS, A_KV = 6, 2
B_HEADS, B_GROUPS = 4, 3
B_DILATIONS = (1, 4, 16)
B_HALF = 64
C_HEADS, C_RANK, C_ROPE = 6, 512, 64
GRID_W = 64
A_THETA, B_THETA, C_THETA = 10000.0, 500000.0, 10000.0
B_ROPE_DIM = 32
EPS = 1e-6
A_W, B_W, C_W = A_HEADS * HEAD, B_HEADS * HEAD, C_HEADS * HEAD
O_AQ, O_AK, O_AV = 0, 768, 1024
O_BQ, O_BK, O_BV = 1280, 2816, 3328
O_CQ, O_CKV, O_CKR = 3840, 4352, 4864
IN_W = 4928
PROJ_W = 5120

ADAM_LR, ADAM_B1, ADAM_B2, ADAM_EPS, ADAM_WD, ADAM_STEP = 0.001, 0.9, 0.999, 1e-08, 0.01, 10

NN = (((1,), (0,)), ((), ()))
NT = (((1,), (1,)), ((), ()))
TN = (((0,), (0,)), ((), ()))


def _dot(a, b, dims=NN):
    return lax.dot_general(a.astype(BF16), b.astype(BF16), dims, preferred_element_type=F32)


def _params(*sem):
    return pltpu.CompilerParams(dimension_semantics=sem if sem else None, vmem_limit_bytes=VMEM_LIMIT)


def _tile(n, target, unit=LANES):
    if n <= target:
        return n
    best = 0
    for t in range(unit, target + 1, unit):
        if n % t == 0:
            best = t
    return best if best else n


def _rows(r, width, itemsize=4):
    return _tile(r, max(8, ELEMENTWISE_BLOCK_BYTES // (width * itemsize)), 8)


def _matmul_call(name, dims, grid, a_spec, b_spec, o_spec, out_shape, acc_shape, add_spec=None):
    nk = grid[2]

    def body(*refs):
        a_ref, b_ref = refs[:2]
        add_ref = None if add_spec is None else refs[2]
        o_ref = refs[2 if add_spec is None else 3]

        def finish(r):
            if add_ref is not None:
                r = r + add_ref[...]
            o_ref[...] = r.astype(o_ref.dtype)

        if nk == 1:
            finish(_dot(a_ref[...], b_ref[...], dims))
            return
        acc = refs[-1]
        k = pl.program_id(2)

        @pl.when(k == 0)
        def _():
            acc[...] = _dot(a_ref[...], b_ref[...], dims)

        if nk > 2:
            @pl.when((k > 0) & (k < nk - 1))
            def _():
                acc[...] += _dot(a_ref[...], b_ref[...], dims)

        @pl.when(k == nk - 1)
        def _():
            finish(acc[...] + _dot(a_ref[...], b_ref[...], dims))

    in_specs = [a_spec, b_spec] + ([] if add_spec is None else [add_spec])
    return pl.pallas_call(
        body, name=name, grid=grid, in_specs=in_specs, out_specs=o_spec, out_shape=out_shape,
        scratch_shapes=[pltpu.VMEM(acc_shape, F32)] if nk > 1 else [],
        compiler_params=_params("parallel", "parallel", "arbitrary"))


def matmul(name, a, b, mode, out_dtype, add=None, tm=1024, tn=1024, tk=2816):
    if mode == "nn":
        (m, k), (k2, n) = a.shape, b.shape
    elif mode == "nt":
        (m, k), (n, k2) = a.shape, b.shape
    else:
        (k, m), (k2, n) = a.shape, b.shape
    assert k == k2, (name, a.shape, b.shape)
    tm, tn, tk = _tile(m, tm), _tile(n, tn), _tile(k, tk)
    grid = (m // tm, n // tn, k // tk)
    if mode == "tn":
        a_spec = pl.BlockSpec((tk, tm), lambda i, j, kk: (kk, i))
    else:
        a_spec = pl.BlockSpec((tm, tk), lambda i, j, kk: (i, kk))
    whole_b = {"pipeline_mode": pl.Buffered(1)} if grid[1] == 1 and grid[2] == 1 and grid[0] > 1 else {}
    if mode == "nt":
        b_spec = pl.BlockSpec((tn, tk), lambda i, j, kk: (j, kk), **whole_b)
    else:
        b_spec = pl.BlockSpec((tk, tn), lambda i, j, kk: (kk, j), **whole_b)
    o_spec = pl.BlockSpec((tm, tn), lambda i, j, kk: (i, j))
    dims = {"nn": NN, "nt": NT, "tn": TN}[mode]
    call = _matmul_call(name, dims, grid, a_spec, b_spec, o_spec, jax.ShapeDtypeStruct((m, n), out_dtype),
                        (tm, tn), None if add is None else o_spec)
    return call(a, b) if add is None else call(a, b, add)


def rmsnorm_fwd(name, x, g, width, out_dtype):
    r, cols = x.shape
    nb = cols // width
    tr = _rows(r, width)

    def body(x_ref, g_ref, o_ref):
        xv = x_ref[...]
        rs = lax.rsqrt(jnp.mean(xv * xv, axis=-1, keepdims=True) + EPS)
        o_ref[...] = (xv * rs * g_ref[...]).astype(o_ref.dtype)

    blk = pl.BlockSpec((tr, width), lambda i, j: (i, j))
    return pl.pallas_call(
        body, name=name, grid=(r // tr, nb),
        in_specs=[blk, pl.BlockSpec((1, width), lambda i, j: (0, 0))], out_specs=blk,
        out_shape=jax.ShapeDtypeStruct((r, cols), out_dtype),
        compiler_params=_params("parallel", "parallel"))(x, g.reshape(1, width))


def rmsnorm_bwd(name, dy, x, g, width, out_dtypes=(F32,), add=None):
    r, cols = x.shape
    nb = cols // width
    tr = _rows(r, width)
    n_out = len(out_dtypes)

    def body(*refs):
        dy_ref, x_ref, g_ref = refs[:3]
        add_ref = refs[3] if add is not None else None
        outs = refs[-(n_out + 1):-1]
        dg_ref = refs[-1]
        xv = x_ref[...]
        dyv = dy_ref[...].astype(F32)
        rs = lax.rsqrt(jnp.mean(xv * xv, axis=-1, keepdims=True) + EPS)
        xh = xv * rs
        dyg = dyv * g_ref[...]
        dx = rs * (dyg - xh * jnp.mean(dyg * xh, axis=-1, keepdims=True))
        if add_ref is not None:
            dx = dx + add_ref[...]
        for o in outs:
            o[...] = dx.astype(o.dtype)

        @pl.when((pl.program_id(0) == 0) & (pl.program_id(1) == 0))
        def _():
            dg_ref[...] = jnp.zeros_like(dg_ref)

        dg_ref[...] += jnp.sum(dyv * xh, axis=0, keepdims=True)

    blk = pl.BlockSpec((tr, width), lambda i, j: (i, j))
    vec = pl.BlockSpec((1, width), lambda i, j: (0, 0))
    ins = [dy, x, g.reshape(1, width)] + ([add] if add is not None else [])
    res = pl.pallas_call(
        body, name=name, grid=(r // tr, nb),
        in_specs=[blk, blk, vec] + ([blk] if add is not None else []),
        out_specs=[blk] * n_out + [vec],
        out_shape=[jax.ShapeDtypeStruct((r, cols), dt) for dt in out_dtypes] + [jax.ShapeDtypeStruct((1, width), F32)],
        compiler_params=_params("arbitrary", "arbitrary"))(*ins)
    return tuple(res[:n_out]) + (res[n_out].reshape(width),)


def _rope_angles(pos, dim, theta):
    inv = theta ** (-jnp.arange(0, dim, 2, dtype=F32) / dim)
    ang = pos.astype(F32)[:, None] * inv[None, :]
    return jnp.cos(ang), jnp.sin(ang)


def _rope_tables(s):
    rows = s // GRID_W
    row_pos = jnp.repeat(jnp.arange(rows), GRID_W)
    col_pos = jnp.tile(jnp.arange(GRID_W), rows)
    t_pos = jnp.arange(s)
    z = lambda n: jnp.zeros((s, n), F32)
    o = lambda n: jnp.ones((s, n), F32)
    cr, sr = _rope_angles(row_pos, HEAD // 2, A_THETA)
    cc, sc = _rope_angles(col_pos, HEAD // 2, A_THETA)
    tab_a = (jnp.concatenate([cr, cr, cc, cc], 1), jnp.concatenate([-sr, z(32), -sc, z(32)], 1),
             jnp.concatenate([z(32), sr, z(32), sc], 1), 32)
    cp, sp = _rope_angles(t_pos, B_ROPE_DIM, B_THETA)
    tab_b = (jnp.concatenate([cp, cp, o(96)], 1), jnp.concatenate([-sp, z(112)], 1),
             jnp.concatenate([z(16), sp, z(96)], 1), 16)
    cm, sm = _rope_angles(t_pos, C_ROPE, C_THETA)
    tab_c = (jnp.concatenate([cm, cm, o(64)], 1), jnp.concatenate([-sm, z(96)], 1),
             jnp.concatenate([z(32), sm, z(64)], 1), 32)

    def transposed(tab):
        c, s1, s2, h = tab
        return (c, jnp.roll(s2, -h, axis=1), jnp.roll(s1, h, axis=1), h)

    return {k: (t, transposed(t)) for k, t in (("a", tab_a), ("b", tab_b), ("c", tab_c))}


def rope(name, x, tab, out_dtype, plain_heads=0):
    c, s1, s2, h = tab
    s, cols = x.shape
    ts = _rows(s, cols)

    def body(x_ref, c_ref, s1_ref, s2_ref, o_ref):
        cv, s1v, s2v = c_ref[...], s1_ref[...], s2_ref[...]
        for j in range(cols // HEAD):
            lanes = slice(HEAD * j, HEAD * (j + 1))
            xv = x_ref[:, lanes].astype(F32)
            if j >= plain_heads:
                xv = xv * cv + pltpu.roll(xv, HEAD - h, 1) * s1v + pltpu.roll(xv, h, 1) * s2v
            o_ref[:, lanes] = xv.astype(o_ref.dtype)

    blk = pl.BlockSpec((ts, cols), lambda i: (i, 0))
    tb = pl.BlockSpec((ts, HEAD), lambda i: (i, 0))
    return pl.pallas_call(
        body, name=name, grid=(s // ts,), in_specs=[blk, tb, tb, tb], out_specs=blk,
        out_shape=jax.ShapeDtypeStruct((s, cols), out_dtype),
        compiler_params=_params("parallel"))(x, c, s1, s2)


PREP_ROWS = 256


def _rope_of(x, tab, h):
    c, s1, s2 = tab
    return x * c + pltpu.roll(x, HEAD - h, 1) * s1 + pltpu.roll(x, h, 1) * s2


def _heads(start, n):
    return [(j, slice(start + HEAD * j, start + HEAD * (j + 1))) for j in range(n)]


def mixer_prep(name, proj, p, tabs):
    s = proj.shape[0]
    tr = PREP_ROWS
    (ca, s1a, s2a, ha), (cb, s1b, s2b, hb) = tabs["a"][0], tabs["b"][0]

    def body(proj_ref, gaq, gak, gcq, gckv, ca_r, s1a_r, s2a_r, cb_r, s1b_r, s2b_r,
             qa_ref, ka_ref, av_ref, qb_ref, kb_ref, cqn_ref, ckvn_ref):
        ta = (ca_r[...], s1a_r[...], s2a_r[...])
        tb = (cb_r[...], s1b_r[...], s2b_r[...])

        def normed(x, g):
            return x * lax.rsqrt(jnp.mean(x * x, axis=-1, keepdims=True) + EPS) * g

        for (src0, n, gain, out) in ((O_AQ, A_HEADS, gaq, qa_ref), (O_AK, A_KV, gak, ka_ref)):
            for j, cols in _heads(src0, n):
                out[:, HEAD * j:HEAD * (j + 1)] = _rope_of(normed(proj_ref[:, cols], gain[...]), ta, ha).astype(BF16)
        av_ref[...] = proj_ref[:, O_AV:O_BQ].astype(BF16)
        for (src0, n, out) in ((O_BQ, B_GROUPS * B_HEADS, qb_ref), (O_BK, B_HEADS, kb_ref)):
            for j, cols in _heads(src0, n):
                out[:, HEAD * j:HEAD * (j + 1)] = _rope_of(proj_ref[:, cols], tb, hb)
        cqn_ref[...] = normed(proj_ref[:, O_CQ:O_CKV], gcq[...]).astype(BF16)
        ckvn_ref[...] = normed(proj_ref[:, O_CKV:O_CKR], gckv[...]).astype(BF16)

    rows = lambda w: pl.BlockSpec((tr, w), lambda i: (i, 0))
    vec = lambda w: pl.BlockSpec((1, w), lambda i: (0, 0))
    widths = (A_W, A_KV * HEAD, A_KV * HEAD, B_GROUPS * B_W, B_W, C_RANK, C_RANK)
    dtypes = (BF16, BF16, BF16, F32, F32, BF16, BF16)
    return pl.pallas_call(
        body, name=name, grid=(s // tr,),
        in_specs=[rows(proj.shape[1]), vec(HEAD), vec(HEAD), vec(C_RANK), vec(C_RANK)] + [rows(HEAD)] * 6,
        out_specs=[rows(w) for w in widths],
        out_shape=[jax.ShapeDtypeStruct((s, w), dt) for w, dt in zip(widths, dtypes)],
        compiler_params=_params("parallel"))(
            proj, p["a_q_norm"].reshape(1, HEAD), p["a_k_norm"].reshape(1, HEAD), p["c_q_norm"].reshape(1, C_RANK),
            p["c_kv_norm"].reshape(1, C_RANK), ca, s1a, s2a, cb, s1b, s2b)


def mixer_prep_bwd(name, proj, p, tabs, dqa, dka, dva, dqb, dkb, dvb, dcqn, dckvn, dckr):
    s = proj.shape[0]
    tr = PREP_ROWS
    (ca, s1a, s2a, ha), (cb, s1b, s2b, hb) = tabs["a"][1], tabs["b"][1]

    def body(proj_ref, dqa_ref, dka_ref, dva_ref, dqb_ref, dkb_ref, dvb_ref, dcqn_ref, dckvn_ref, dckr_ref,
             gaq, gak, gcq, gckv, ca_r, s1a_r, s2a_r, cb_r, s1b_r, s2b_r, dproj_ref, dgaq, dgak, dgcq, dgckv):
        ta = (ca_r[...], s1a_r[...], s2a_r[...])
        tb = (cb_r[...], s1b_r[...], s2b_r[...])

        @pl.when(pl.program_id(0) == 0)
        def _():
            for ref in (dgaq, dgak, dgcq, dgckv):
                ref[...] = jnp.zeros_like(ref)

        def norm_bwd(dy, x, gain, dgain):
            rs = lax.rsqrt(jnp.mean(x * x, axis=-1, keepdims=True) + EPS)
            xh = x * rs
            dyg = dy * gain[...]
            dgain[...] += jnp.sum(dy * xh, axis=0, keepdims=True)
            return rs * (dyg - xh * jnp.mean(dyg * xh, axis=-1, keepdims=True))

        for (dst0, n, d_ref, gain, dgain) in ((O_AQ, A_HEADS, dqa_ref, gaq, dgaq), (O_AK, A_KV, dka_ref, gak, dgak)):
            for j, cols in _heads(dst0, n):
                dy = _rope_of(d_ref[:, HEAD * j:HEAD * (j + 1)], ta, ha)
                dproj_ref[:, cols] = norm_bwd(dy, proj_ref[:, cols], gain, dgain).astype(BF16)
        dproj_ref[:, O_AV:O_BQ] = dva_ref[...].astype(BF16)
        for (dst0, n, d_ref) in ((O_BQ, B_GROUPS * B_HEADS, dqb_ref), (O_BK, B_HEADS, dkb_ref)):
            for j, cols in _heads(dst0, n):
                dproj_ref[:, cols] = _rope_of(d_ref[:, HEAD * j:HEAD * (j + 1)], tb, hb).astype(BF16)
        dproj_ref[:, O_BV:O_CQ] = dvb_ref[...].astype(BF16)
        dproj_ref[:, O_CQ:O_CKV] = norm_bwd(dcqn_ref[...], proj_ref[:, O_CQ:O_CKV], gcq, dgcq).astype(BF16)
        dproj_ref[:, O_CKV:O_CKR] = norm_bwd(dckvn_ref[...], proj_ref[:, O_CKV:O_CKR], gckv, dgckv).astype(BF16)
        dproj_ref[:, O_CKR:] = dckr_ref[:, :C_ROPE]

    rows = lambda w: pl.BlockSpec((tr, w), lambda i: (i, 0))
    vec = lambda w: pl.BlockSpec((1, w), lambda i: (0, 0))
    grads = (dqa, dka, dva, dqb, dkb, dvb, dcqn, dckvn, dckr)
    res = pl.pallas_call(
        body, name=name, grid=(s // tr,),
        in_specs=[rows(proj.shape[1])] + [rows(g.shape[1]) for g in grads]
        + [vec(HEAD), vec(HEAD), vec(C_RANK), vec(C_RANK)] + [rows(HEAD)] * 6,
        out_specs=[rows(proj.shape[1]), vec(HEAD), vec(HEAD), vec(C_RANK), vec(C_RANK)],
        out_shape=[jax.ShapeDtypeStruct(proj.shape, BF16)] + [jax.ShapeDtypeStruct((1, w), F32) for w in (HEAD, HEAD, C_RANK, C_RANK)],
        compiler_params=_params("arbitrary"))(
            proj, *grads, p["a_q_norm"].reshape(1, HEAD), p["a_k_norm"].reshape(1, HEAD), p["c_q_norm"].reshape(1, C_RANK),
            p["c_kv_norm"].reshape(1, C_RANK), ca, s1a, s2a, cb, s1b, s2b)
    return res[0], res[1].reshape(HEAD), res[2].reshape(HEAD), res[3].reshape(C_RANK), res[4].reshape(C_RANK)


OUT_GROUPS = ((0, A_W), (A_W, A_W + B_W), (A_W + B_W, A_W + B_W + C_W))


def out_norms(name, ya, yb, yc, gain):
    s = ya.shape[0]
    tr = PREP_ROWS
    width = OUT_GROUPS[-1][1]

    def body(ya_ref, yb_ref, yc_ref, g_ref, y_ref):
        for (lo, hi), src in zip(OUT_GROUPS, (ya_ref, yb_ref, yc_ref)):
            x = src[...]
            rs = lax.rsqrt(jnp.mean(x * x, axis=-1, keepdims=True) + EPS)
            y_ref[:, lo:hi] = (x * rs * g_ref[:, lo:hi]).astype(BF16)

    rows = lambda w: pl.BlockSpec((tr, w), lambda i: (i, 0))
    return pl.pallas_call(
        body, name=name, grid=(s // tr,),
        in_specs=[rows(A_W), rows(B_W), rows(C_W), pl.BlockSpec((1, width), lambda i: (0, 0))], out_specs=rows(width),
        out_shape=jax.ShapeDtypeStruct((s, width), BF16), compiler_params=_params("parallel"))(
            ya, yb, yc, gain.reshape(1, width))


def out_norms_bwd(name, dy, ya, yb, yc, gain):
    s = ya.shape[0]
    tr = PREP_ROWS
    width = OUT_GROUPS[-1][1]

    def body(dy_ref, ya_ref, yb_ref, yc_ref, g_ref, dya_ref, dyb_ref, dyc_ref, dg_ref):
        @pl.when(pl.program_id(0) == 0)
        def _():
            dg_ref[...] = jnp.zeros_like(dg_ref)

        for (lo, hi), src, dst in zip(OUT_GROUPS, (ya_ref, yb_ref, yc_ref), (dya_ref, dyb_ref, dyc_ref)):
            x = src[...]
            d = dy_ref[:, lo:hi]
            rs = lax.rsqrt(jnp.mean(x * x, axis=-1, keepdims=True) + EPS)
            xh = x * rs
            dg = d * g_ref[:, lo:hi]
            dst[...] = (rs * (dg - xh * jnp.mean(dg * xh, axis=-1, keepdims=True))).astype(dst.dtype)
            dg_ref[:, lo:hi] += jnp.sum(d * xh, axis=0, keepdims=True)

    rows = lambda w: pl.BlockSpec((tr, w), lambda i: (i, 0))
    vec = pl.BlockSpec((1, width), lambda i: (0, 0))
    dya, dyb, dyc, dg = pl.pallas_call(
        body, name=name, grid=(s // tr,),
        in_specs=[rows(width), rows(A_W), rows(B_W), rows(C_W), vec], out_specs=[rows(A_W), rows(B_W), rows(C_W), vec],
        out_shape=[jax.ShapeDtypeStruct((s, A_W), BF16), jax.ShapeDtypeStruct((s, B_W), F32),
                   jax.ShapeDtypeStruct((s, C_W), BF16), jax.ShapeDtypeStruct((1, width), F32)],
        compiler_params=_params("arbitrary"))(dy, ya, yb, yc, gain.reshape(1, width))
    return dya, dyb, dyc, dg.reshape(width)


ATTN_TQ = 256
ATTN_STEP_ROWS = 512


def attn_fwd(name, parts, v, v_group, nheads, scale):
    s = v.shape[0]
    tq = ATTN_STEP_ROWS
    npart = len(parts)

    def body(*refs):
        v_ref, o_ref, lse_ref = refs[2 * npart:]
        for r0 in range(0, tq, ATTN_TQ):
            rows = slice(r0, r0 + ATTN_TQ)
            sc = None
            for p in range(npart):
                t = _dot(refs[2 * p][rows, :], refs[2 * p + 1][...], NT)
                sc = t if sc is None else sc + t
            sc = sc * scale
            m = jnp.max(sc, axis=-1, keepdims=True)
            e = jnp.exp(sc - m)
            l = jnp.sum(e, axis=-1, keepdims=True)
            o_ref[rows, :] = _dot(e.astype(BF16), v_ref[...]) / l
            lse_ref[rows, :] = jnp.broadcast_to(m + jnp.log(l), (ATTN_TQ, HEAD))

    in_specs, ins = [], []
    for q, qoff, k, kg in parts:
        in_specs.append(pl.BlockSpec((tq, HEAD), lambda h, i, qoff=qoff: (i, qoff + h)))
        in_specs.append(pl.BlockSpec((s, HEAD), lambda h, i, kg=kg: (0, h // kg)))
        ins += [q, k]
    in_specs.append(pl.BlockSpec((s, HEAD), lambda h, i: (0, h // v_group)))
    out_blk = pl.BlockSpec((tq, HEAD), lambda h, i: (i, h))
    return pl.pallas_call(
        body, name=name, grid=(nheads, s // tq), in_specs=in_specs, out_specs=[out_blk, out_blk],
        out_shape=[jax.ShapeDtypeStruct((s, nheads * HEAD), F32)] * 2,
        compiler_params=_params("parallel", "parallel"))(*ins, v)


def attn_bwd(name, parts, v, v_group, lse, o, do, nheads, scale):
    s = v.shape[0]
    tq = ATTN_STEP_ROWS
    npart = len(parts)

    def body(*refs):
        v_ref, lse_ref, o_ref, do_ref = refs[2 * npart:2 * npart + 4]
        outs = refs[2 * npart + 4:]
        dq_refs, dk_refs, dv_ref = outs[:npart], outs[npart:2 * npart], outs[2 * npart]
        h, i = pl.program_id(0), pl.program_id(1)

        @pl.when((i == 0) & (h % v_group == 0))
        def _():
            dv_ref[...] = jnp.zeros_like(dv_ref)

        for p in range(npart):
            @pl.when((i == 0) & (h % parts[p][3] == 0))
            def _(p=p):
                dk_refs[p][...] = jnp.zeros_like(dk_refs[p])

        for r0 in range(0, tq, ATTN_TQ):
            rows = slice(r0, r0 + ATTN_TQ)
            sc = None
            for p in range(npart):
                t = _dot(refs[2 * p][rows, :], refs[2 * p + 1][...], NT)
                sc = t if sc is None else sc + t
            pr = jnp.exp(sc * scale - lse_ref[rows, 0:1])
            dov = do_ref[rows, :]
            delta = jnp.sum(dov.astype(F32) * o_ref[rows, :], axis=-1, keepdims=True)
            dp = _dot(dov, v_ref[...], NT)
            ds = (pr * (dp - delta) * scale).astype(BF16)
            dv_ref[...] += _dot(pr.astype(BF16), dov, TN)
            for p in range(npart):
                dq_refs[p][rows, :] = _dot(ds, refs[2 * p + 1][...])
                dk_refs[p][...] += _dot(ds, refs[2 * p][rows, :], TN)

    in_specs, ins = [], []
    for q, qoff, k, kg in parts:
        in_specs.append(pl.BlockSpec((tq, HEAD), lambda h, i, qoff=qoff: (i, qoff + h)))
        in_specs.append(pl.BlockSpec((s, HEAD), lambda h, i, kg=kg: (0, h // kg)))
        ins += [q, k]
    hq_blk = pl.BlockSpec((tq, HEAD), lambda h, i: (i, h))
    in_specs += [pl.BlockSpec((s, HEAD), lambda h, i: (0, h // v_group)), hq_blk, hq_blk, hq_blk]
    out_specs = [hq_blk] * npart
    out_shape = [jax.ShapeDtypeStruct((s, nheads * HEAD), F32)] * npart
    for q, qoff, k, kg in parts:
        out_specs.append(pl.BlockSpec((s, HEAD), lambda h, i, kg=kg: (0, h // kg)))
        out_shape.append(jax.ShapeDtypeStruct((s, nheads // kg * HEAD), F32))
    out_specs.append(pl.BlockSpec((s, HEAD), lambda h, i: (0, h // v_group)))
    out_shape.append(jax.ShapeDtypeStruct((s, nheads // v_group * HEAD), F32))
    res = pl.pallas_call(
        body, name=name, grid=(nheads, s // tq), in_specs=in_specs, out_specs=out_specs, out_shape=out_shape,
        compiler_params=_params("arbitrary", "arbitrary"))(*ins, v, lse, o, do)
    return list(res[:npart]), list(res[npart:2 * npart]), res[2 * npart]


def _band_windows(lf):
    for ib in range(lf // HEAD):
        q0 = ib * HEAD
        yield q0, max(0, q0 - B_HALF), min(lf, q0 + HEAD + B_HALF)


def _band_mask(q0, lo, hi):
    qpos = q0 + lax.broadcasted_iota(jnp.int32, (HEAD, hi - lo), 0)
    kpos = lo + lax.broadcasted_iota(jnp.int32, (HEAD, hi - lo), 1)
    return jnp.abs(qpos - kpos) <= B_HALF


def _class_rows(r, d, start, size):
    return pl.ds(r + d * start, size, stride=d) if d > 1 else pl.ds(start, size)


def dilated_fwd(name, q, k, v_src, v_block):
    s = k.shape[0]
    scale = HEAD ** -0.5

    def body(q_ref, k_ref, v_ref, y_ref, lse_ref, o_scr, l_scr):
        g_now = pl.program_id(1)
        for g, d in enumerate(B_DILATIONS):
            @pl.when(g_now == g)
            def _(g=g, d=d):
                for r in range(d):
                    for q0, lo, hi in _band_windows(s // d):
                        mine = _class_rows(r, d, q0, HEAD)
                        keys = _class_rows(r, d, lo, hi - lo)
                        sc = _dot(q_ref[mine, :].astype(BF16), k_ref[keys, :].astype(BF16), NT) * scale
                        sc = jnp.where(_band_mask(q0, lo, hi), sc, -1e30)
                        m = jnp.max(sc, axis=-1, keepdims=True)
                        e = jnp.exp(sc - m)
                        l = jnp.sum(e, axis=-1, keepdims=True)
                        o_scr.at[g][mine, :] = _dot((e / l).astype(BF16), v_ref[keys, :].astype(BF16))
                        l_scr.at[g][mine, :] = jnp.broadcast_to(m + jnp.log(l), (HEAD, HEAD))

        @pl.when(g_now == B_GROUPS - 1)
        def _():
            a, b, c = l_scr[0], l_scr[1], l_scr[2]
            m = jnp.maximum(jnp.maximum(a, b), c)
            ea, eb, ec = jnp.exp(a - m), jnp.exp(b - m), jnp.exp(c - m)
            den = ea + eb + ec
            y_ref[...] = (ea / den) * o_scr[0] + (eb / den) * o_scr[1] + (ec / den) * o_scr[2]
            lse_ref[...] = m + jnp.log(den)

    blk = lambda f: pl.BlockSpec((s, HEAD), f)
    per_head = blk(lambda h, g: (0, h))
    return pl.pallas_call(
        body, name=name, grid=(B_HEADS, B_GROUPS),
        in_specs=[blk(lambda h, g: (0, B_HEADS * g + h)), per_head, blk(lambda h, g: (0, v_block + h))],
        out_specs=[per_head, per_head], out_shape=[jax.ShapeDtypeStruct((s, B_W), F32)] * 2,
        scratch_shapes=[pltpu.VMEM((B_GROUPS, s, HEAD), F32)] * 2,
        compiler_params=_params("parallel", "arbitrary"))(q, k, v_src)


def dilated_bwd(name, q, k, v_src, v_block, dy, y, lse):
    s = k.shape[0]
    scale = HEAD ** -0.5

    def body(q_ref, k_ref, v_ref, dy_ref, y_ref, lse_ref, dq_ref, dk_ref, dv_ref, delta):
        g_now = pl.program_id(1)

        @pl.when(g_now == 0)
        def _():
            dk_ref[...] = jnp.zeros_like(dk_ref)
            dv_ref[...] = jnp.zeros_like(dv_ref)
            delta[...] = jnp.broadcast_to(jnp.sum(dy_ref[...] * y_ref[...], axis=-1, keepdims=True), (s, HEAD))

        for g, d in enumerate(B_DILATIONS):
            @pl.when(g_now == g)
            def _(d=d):
                for r in range(d):
                    for q0, lo, hi in _band_windows(s // d):
                        mine = _class_rows(r, d, q0, HEAD)
                        keys = _class_rows(r, d, lo, hi - lo)
                        qv, kv = q_ref[mine, :].astype(BF16), k_ref[keys, :].astype(BF16)
                        dyv = dy_ref[mine, :].astype(BF16)
                        sc = _dot(qv, kv, NT) * scale
                        pr = jnp.where(_band_mask(q0, lo, hi), jnp.exp(sc - lse_ref[mine, :][:, 0:1]), 0.0)
                        dp = _dot(dyv, v_ref[keys, :].astype(BF16), NT)
                        ds = (pr * (dp - delta[mine, :][:, 0:1]) * scale).astype(BF16)
                        dv_ref[keys, :] = dv_ref[keys, :] + _dot(pr.astype(BF16), dyv, TN)
                        dq_ref[mine, :] = _dot(ds, kv)
                        dk_ref[keys, :] = dk_ref[keys, :] + _dot(ds, qv, TN)

    blk = lambda f: pl.BlockSpec((s, HEAD), f)
    per_head = blk(lambda h, g: (0, h))
    by_group = blk(lambda h, g: (0, B_HEADS * g + h))
    return pl.pallas_call(
        body, name=name, grid=(B_HEADS, B_GROUPS),
        in_specs=[by_group, per_head, blk(lambda h, g: (0, v_block + h)), per_head, per_head, per_head],
        out_specs=[by_group, per_head, per_head],
        out_shape=[jax.ShapeDtypeStruct((s, B_GROUPS * B_W), F32)] + [jax.ShapeDtypeStruct((s, B_W), F32)] * 2,
        scratch_shapes=[pltpu.VMEM((s, HEAD), F32)],
        compiler_params=_params("parallel", "arbitrary"))(q, k, v_src, dy, y, lse)


FFN_TN = 256


def _edges(shape):
    row = lax.broadcasted_iota(jnp.int32, shape, 0)
    return row == 0, row == shape[0] - 1


def _shifted(h, edges):
    first, last = edges
    s = h.shape[0]
    return jnp.where(first, 0.0, pltpu.roll(h, 1, 0)), jnp.where(last, 0.0, pltpu.roll(h, s - 1, 0))


def _conv(h, cw, edges):
    prev, nxt = _shifted(h, edges)
    return prev * cw[0:1, :] + h * cw[1:2, :] + nxt * cw[2:3, :] + cw[3:4, :], prev, nxt


def _sigmoid(x):
    return 1.0 / (1.0 + jnp.exp(-x))


def ffn_up(name, xn, w_up, cwb):
    s, dm = xn.shape
    quarter = w_up.shape[3]
    fh = 2 * quarter
    tn = FFN_TN
    per = quarter // tn

    def body(x_ref, w_ref, cw_ref, h_ref, act_ref):
        xv = x_ref[...]
        hg = _dot(xv, w_ref[0])
        hu = _dot(xv, w_ref[1])
        h_ref[0] = hg
        h_ref[1] = hu
        edges = _edges(hg.shape)
        gc, _, _ = _conv(hg, cw_ref[0], edges)
        uc, _, _ = _conv(hu, cw_ref[1], edges)
        act_ref[...] = (gc * _sigmoid(gc) * uc).astype(BF16)

    return pl.pallas_call(
        body, name=name, grid=(fh // tn,),
        in_specs=[pl.BlockSpec((s, dm), lambda t: (0, 0), pipeline_mode=pl.Buffered(1)),
                  pl.BlockSpec((2, None, dm, tn), lambda t: (0, t // per, 0, t % per)),
                  pl.BlockSpec((2, 8, tn), lambda t: (0, 0, t))],
        out_specs=[pl.BlockSpec((2, s, tn), lambda t: (0, 0, t)), pl.BlockSpec((s, tn), lambda t: (0, t))],
        out_shape=[jax.ShapeDtypeStruct((2, s, fh), F32), jax.ShapeDtypeStruct((s, fh), BF16)],
        compiler_params=_params("parallel"))(xn, w_up, cwb)


def ffn_gate_bwd(name, h, dact, cwb):
    _, s, fh = h.shape
    tn = FFN_TN

    def body(h_ref, da_ref, cw_ref, dh_ref, dcw_ref):
        edges = _edges((s, tn))
        gc, gp, gn = _conv(h_ref[0], cw_ref[0], edges)
        uc, up, un = _conv(h_ref[1], cw_ref[1], edges)
        sg = _sigmoid(gc)
        da = da_ref[...]
        dgc = da * uc * (sg * (1.0 + gc * (1.0 - sg)))
        duc = da * (gc * sg)
        for idx, (hv, prev, nxt, dc) in enumerate(((h_ref[0], gp, gn, dgc), (h_ref[1], up, un, duc))):
            cw = cw_ref[idx]
            from_prev, from_next = _shifted(dc, edges)
            dh_ref[idx] = (from_next * cw[0:1, :] + dc * cw[1:2, :] + from_prev * cw[2:3, :]).astype(BF16)
            dcw_ref[idx, 0:1, :] = jnp.sum(prev * dc, axis=0, keepdims=True)
            dcw_ref[idx, 1:2, :] = jnp.sum(hv * dc, axis=0, keepdims=True)
            dcw_ref[idx, 2:3, :] = jnp.sum(nxt * dc, axis=0, keepdims=True)
            dcw_ref[idx, 3:4, :] = jnp.sum(dc, axis=0, keepdims=True)
            dcw_ref[idx, 4:8, :] = jnp.zeros((4, tn), F32)

    return pl.pallas_call(
        body, name=name, grid=(fh // tn,),
        in_specs=[pl.BlockSpec((2, s, tn), lambda t: (0, 0, t)), pl.BlockSpec((s, tn), lambda t: (0, t)),
                  pl.BlockSpec((2, 8, tn), lambda t: (0, 0, t))],
        out_specs=[pl.BlockSpec((2, s, tn), lambda t: (0, 0, t)), pl.BlockSpec((2, 8, tn), lambda t: (0, 0, t))],
        out_shape=[jax.ShapeDtypeStruct((2, s, fh), BF16), jax.ShapeDtypeStruct((2, 8, fh), F32)],
        compiler_params=_params("parallel"))(h, dact, cwb)


def ffn_dx(name, dh, w_up):
    _, s, fh = dh.shape
    dm, quarter = w_up.shape[2], w_up.shape[3]
    tk = _tile(quarter, 2816)
    per = quarter // tk
    tm, tn = _tile(s, 1024), _tile(dm, 1024)
    grid = (s // tm, dm // tn, 4 * per)
    a_spec = pl.BlockSpec((None, tm, tk), lambda i, j, k: (k // (2 * per), i, k % (2 * per)))
    b_spec = pl.BlockSpec((None, None, tn, tk), lambda i, j, k: (k // (2 * per), (k // per) % 2, j, k % per))
    o_spec = pl.BlockSpec((tm, tn), lambda i, j, k: (i, j))
    return _matmul_call(name, NT, grid, a_spec, b_spec, o_spec, jax.ShapeDtypeStruct((s, dm), F32), (tm, tn))(dh, w_up)


def ffn_dw_up(name, xn, dh):
    _, s, fh = dh.shape
    dm = xn.shape[1]
    quarter = fh // 2
    tn = _tile(quarter, 1536)
    per = quarter // tn
    tm, tk = _tile(dm, 1024), _tile(s, 2048)
    grid = (dm // tm, 4 * per, s // tk)
    a_spec = pl.BlockSpec((tk, tm), lambda i, j, k: (k, i))
    b_spec = pl.BlockSpec((None, tk, tn), lambda i, j, k: (j // (2 * per), k, j % (2 * per)))
    o_spec = pl.BlockSpec((None, None, tm, tn), lambda i, j, k: (j // (2 * per), (j // per) % 2, i, j % per))
    return _matmul_call(name, TN, grid, a_spec, b_spec, o_spec,
                        jax.ShapeDtypeStruct((2, 2, dm, quarter), F32), (tm, tn))(xn, dh)


def loss_head(name, y, target):
    s, dm = y.shape
    ts = _rows(s, dm)

    def body(y_ref, t_ref, loss_ref, dy_ref, acc):
        i = pl.program_id(0)

        @pl.when(i == 0)
        def _():
            acc[...] = jnp.zeros_like(acc)

        err = y_ref[...] - t_ref[...]
        dy_ref[...] = err * (1.0 / dm)
        acc[...] += jnp.sum(err * err, axis=0, keepdims=True)

        @pl.when(i == s // ts - 1)
        def _():
            loss_ref[...] = jnp.broadcast_to(jnp.sum(acc[...], axis=-1, keepdims=True) * (0.5 / dm), (1, LANES))

    blk = pl.BlockSpec((ts, dm), lambda i: (i, 0))
    loss, dy = pl.pallas_call(
        body, name=name, grid=(s // ts,), in_specs=[blk, blk],
        out_specs=[pl.BlockSpec((1, LANES), lambda i: (0, 0)), blk],
        out_shape=[jax.ShapeDtypeStruct((1, LANES), F32), jax.ShapeDtypeStruct((s, dm), F32)],
        scratch_shapes=[pltpu.VMEM((1, dm), F32)], compiler_params=_params("arbitrary"))(y, target)
    return loss[0, 0], dy


def attn_fwd_c(name, qc, kvc, krc, scale):
    s = qc.shape[0]
    tq = ATTN_STEP_ROWS

    def body(qn_ref, qr_ref, kn_ref, kr_ref, v_ref, o_ref, lse_ref):
        for r0 in range(0, tq, ATTN_TQ):
            rows = slice(r0, r0 + ATTN_TQ)
            sc = (_dot(qn_ref[rows, :], kn_ref[...], NT) + _dot(qr_ref[rows, :], kr_ref[...], NT)) * scale
            m = jnp.max(sc, axis=-1, keepdims=True)
            e = jnp.exp(sc - m)
            l = jnp.sum(e, axis=-1, keepdims=True)
            o_ref[rows, :] = _dot(e.astype(BF16), v_ref[...]) / l
            lse_ref[rows, :] = jnp.broadcast_to(m + jnp.log(l), (ATTN_TQ, HEAD))

    qb = lambda off: pl.BlockSpec((tq, HEAD), lambda h, i: (i, off + h))
    kb = lambda f: pl.BlockSpec((s, HEAD), f)
    out_blk = pl.BlockSpec((tq, HEAD), lambda h, i: (i, h))
    return pl.pallas_call(
        body, name=name, grid=(C_HEADS, s // tq),
        in_specs=[qb(0), qb(C_HEADS), kb(lambda h, i: (0, 2 * h)), kb(lambda h, i: (0, 0)), kb(lambda h, i: (0, 2 * h + 1))],
        out_specs=[out_blk, out_blk], out_shape=[jax.ShapeDtypeStruct((s, C_W), F32)] * 2,
        compiler_params=_params("parallel", "parallel"))(qc, qc, kvc, krc, kvc)


def attn_bwd_c(name, qc, kvc, krc, lse, o, do, scale):
    s = qc.shape[0]
    tq = ATTN_STEP_ROWS

    def body(qn_ref, qr_ref, kn_ref, kr_ref, v_ref, lse_ref, o_ref, do_ref, dqn_ref, dqr_ref, dkv_ref, dkr_ref):
        h, i = pl.program_id(0), pl.program_id(1)

        @pl.when(i == 0)
        def _():
            dkv_ref[...] = jnp.zeros_like(dkv_ref)

        @pl.when((i == 0) & (h == 0))
        def _():
            dkr_ref[...] = jnp.zeros_like(dkr_ref)

        for r0 in range(0, tq, ATTN_TQ):
            rows = slice(r0, r0 + ATTN_TQ)
            sc = (_dot(qn_ref[rows, :], kn_ref[...], NT) + _dot(qr_ref[rows, :], kr_ref[...], NT)) * scale
            pr = jnp.exp(sc - lse_ref[rows, 0:1])
            dov = do_ref[rows, :]
            delta = jnp.sum(dov.astype(F32) * o_ref[rows, :], axis=-1, keepdims=True)
            dp = _dot(dov, v_ref[...], NT)
            ds = (pr * (dp - delta) * scale).astype(BF16)
            dkv_ref[:, HEAD:] += _dot(pr.astype(BF16), dov, TN)
            dqn_ref[rows, :] = _dot(ds, kn_ref[...])
            dqr_ref[rows, :] = _dot(ds, kr_ref[...])
            dkv_ref[:, :HEAD] += _dot(ds, qn_ref[rows, :], TN)
            dkr_ref[...] += _dot(ds, qr_ref[rows, :], TN)

    qb = lambda off: pl.BlockSpec((tq, HEAD), lambda h, i: (i, off + h))
    kb = lambda f: pl.BlockSpec((s, HEAD), f)
    hq = pl.BlockSpec((tq, HEAD), lambda h, i: (i, h))
    kn_map, v_map, kr_map = (lambda h, i: (0, 2 * h)), (lambda h, i: (0, 2 * h + 1)), (lambda h, i: (0, 0))
    return pl.pallas_call(
        body, name=name, grid=(C_HEADS, s // tq),
        in_specs=[qb(0), qb(C_HEADS), kb(kn_map), kb(kr_map), kb(v_map), hq, hq, hq],
        out_specs=[hq, hq, pl.BlockSpec((s, 2 * HEAD), lambda h, i: (0, h)), kb(kr_map)],
        out_shape=[jax.ShapeDtypeStruct((s, C_W), F32)] * 2
        + [jax.ShapeDtypeStruct((s, 2 * C_W), F32), jax.ShapeDtypeStruct((s, HEAD), F32)],
        compiler_params=_params("arbitrary", "arbitrary"))(qc, qc, kvc, krc, kvc, lse, o, do)


def _layer_fwd(x, p, tabs):
    sv = _mixers_ab_fwd(x, p, tabs)
    x1 = _mixer_c_out_fwd(x, sv, p, tabs)
    return _ffn_fwd(x1, p, sv), sv


def _mixers_ab_fwd(x, p, tabs):
    sv = {"x": x}
    hn = rmsnorm_fwd("attn_norm", x, p["attn_norm"], x.shape[1], BF16)
    proj = matmul("in_proj", hn, p["w_in_t"], "nt", F32, tm=256)
    qa, ka, av, qb, kb, cqn, ckvn = mixer_prep("mixer_prep", proj, p, tabs)
    sv.update(hn=hn, proj=proj, qa=qa, ka=ka, av=av, qb=qb, kb=kb, cqn=cqn, ckvn=ckvn)

    ya, lse_a = attn_fwd("a_attn", [(qa, 0, ka, A_HEADS // A_KV)], av, A_HEADS // A_KV, A_HEADS, HEAD ** -0.5)
    yb, lse_b = dilated_fwd("b_dilated", qb, kb, proj, O_BV // HEAD)
    sv.update(ya=ya, lse_a=lse_a, yb=yb, lse_b=lse_b)
    return sv


def _mixer_c_out_fwd(x, sv, p, tabs):
    _mixer_c_fwd(sv, p, tabs)
    return _out_fwd(x, sv, p)


def _mixer_c_fwd(sv, p, tabs):
    qc_raw = matmul("c_uq", sv["cqn"], p["w_uq_t"], "nt", F32)
    qc = rope("c_q_rope", qc_raw, tabs["c"][0], BF16, plain_heads=C_HEADS)
    kvc = matmul("c_ukv", sv["ckvn"], p["w_ukv"], "nn", BF16)
    ckr = jnp.pad(sv["proj"][:, O_CKR:], ((0, 0), (0, HEAD - C_ROPE)))
    krc = rope("c_k_rope", ckr, tabs["c"][0], BF16)
    c_scale = (HEAD + C_ROPE) ** -0.5
    yc, lse_c = attn_fwd_c("c_attn", qc, kvc, krc, c_scale)
    sv.update(qc=qc, kvc=kvc, krc=krc, yc=yc, lse_c=lse_c)


def _out_fwd(x, sv, p, after=0.0):
    y = out_norms("out_norms", sv["ya"], sv["yb"], sv["yc"], p["out_norm"] + after)
    x1 = matmul("out_proj", y, p["w_out"], "nn", F32, add=x)
    sv.update(y=y, x1=x1)
    return x1


def _ffn_fwd(x1, p, sv):
    _ffn_up_fwd(x1, p, sv)
    return _ffn_down_fwd(x1, p, sv)


def _ffn_up_fwd(x1, p, sv):
    xn = rmsnorm_fwd("ffn_norm", x1, p["ffn_norm"], x1.shape[1], BF16)
    h, act = ffn_up("ffn_up", xn, p["w_up"], p["cwb"])
    sv.update(xn=xn, h=h, act=act)


def _ffn_down_fwd(x1, p, sv):
    return matmul("ffn_down", sv["act"], p["w_down"], "nn", F32, add=x1)


def _layer_bwd(dx2, sv, p, tabs):
    dxn, g = _ffn_bwd(dx2.astype(BF16), sv, p)
    dx1, dy, g_out = _out_proj_bwd(dxn, dx2, sv, p)
    dhn, g_mix = _mixers_bwd(dy, sv, p, tabs)
    dx, _, g_norm = _attn_norm_bwd(dhn, dx1, sv, p)
    return dx, {**g, **g_out, **g_mix, **g_norm}


def _ffn_bwd(dx2b, sv, p, after=0.0):
    g = {}
    dact = matmul("ffn_down_dx", dx2b, p["w_down"], "nt", F32)
    g["w_down"] = matmul("ffn_down_dw", sv["act"], dx2b, "tn", F32)
    dh, g["cwb"] = ffn_gate_bwd("ffn_gate_bwd", sv["h"], dact, p["cwb"] + after)
    dxn = ffn_dx("ffn_up_dx", dh, p["w_up"])
    g["w_up"] = ffn_dw_up("ffn_up_dw", sv["xn"], dh)
    return dxn, g


def _out_proj_bwd(dxn, dx2, sv, p, after=0.0):
    g = {}
    dm = dx2.shape[1]
    dx1, dx1b, g["ffn_norm"] = rmsnorm_bwd("ffn_norm_bwd", dxn, sv["x1"], p["ffn_norm"] + after, dm, (F32, BF16), add=dx2)
    dy = matmul("out_proj_dx", dx1b, p["w_out"], "nt", F32)
    g["w_out"] = matmul("out_proj_dw", sv["y"], dx1b, "tn", F32)
    return dx1, dy, g


def _attn_norm_bwd(dhn, dx1, sv, p, after=0.0):
    dx, dxb, d_gain = rmsnorm_bwd("attn_norm_bwd", dhn, sv["x"], p["attn_norm"] + after, dhn.shape[1], (F32, BF16), add=dx1)
    return dx, dxb, {"attn_norm": d_gain}


def _mixers_bwd(dy, sv, p, tabs):
    part, g = _mixer_c_bwd(dy, sv, p, tabs)
    dhn, g_ab = _mixers_ab_bwd(part, sv, p, tabs)
    return dhn, {**g, **g_ab}


def _mixer_c_bwd(dy, sv, p, tabs, after=0.0):
    g = {}
    dya, dyb, dyc, g["out_norm"] = out_norms_bwd("out_norms_bwd", dy, sv["ya"], sv["yb"], sv["yc"], p["out_norm"] + after)

    c_scale = (HEAD + C_ROPE) ** -0.5
    dqn, dqr, dkv, dkr = attn_bwd_c("c_attn_bwd", sv["qc"], sv["kvc"], sv["krc"], sv["lse_c"], sv["yc"], dyc, c_scale)
    dqc = jnp.concatenate([dqn.astype(BF16), rope("c_q_rope_bwd", dqr, tabs["c"][1], BF16)], axis=1)
    dcqn = matmul("c_uq_dx", dqc, p["w_uq_t"], "nn", F32)
    g["w_uq"] = matmul("c_uq_dw", dqc, sv["cqn"], "tn", F32)
    dckvn = matmul("c_ukv_dx", dkv, p["w_ukv"], "nt", F32)
    g["w_ukv"] = matmul("c_ukv_dw", sv["ckvn"], dkv, "tn", F32)
    dckr = rope("c_k_rope_bwd", dkr, tabs["c"][1], BF16)
    return (dya, dyb, dcqn, dckvn, dckr), g


def _mixers_ab_bwd(part, sv, p, tabs, after=0.0):
    g = {}
    dya, dyb, dcqn, dckvn, dckr = part

    dqb, dkb, dvb = dilated_bwd("b_dilated_bwd", sv["qb"], sv["kb"], sv["proj"], O_BV // HEAD, dyb + after, sv["yb"],
                                sv["lse_b"])

    kg = A_HEADS // A_KV
    (dqa,), (dka,), dva = attn_bwd("a_attn_bwd", [(sv["qa"], 0, sv["ka"], kg)], sv["av"], kg, sv["lse_a"], sv["ya"],
                                   dya, A_HEADS, HEAD ** -0.5)

    dproj, g["a_q_norm"], g["a_k_norm"], g["c_q_norm"], g["c_kv_norm"] = mixer_prep_bwd(
        "mixer_prep_bwd", sv["proj"], p, tabs, dqa, dka, dva, dqb, dkb, dvb, dcqn, dckvn, dckr)
    dhn = matmul("in_proj_dx", dproj, p["w_in_t"], "nn", F32, tm=512)
    g["w_in"] = matmul("in_proj_dw", dproj, sv["hn"], "tn", F32, tn=512, tk=512)
    return dhn, g


def _local_step(x, target, layers, final_norm, tabs):
    saved = []
    for p in layers:
        x, sv = _layer_fwd(x, p, tabs)
        saved.append(sv)
    dm = x.shape[1]
    yf = rmsnorm_fwd("final_norm", x, final_norm, dm, F32)
    loss, dyf = loss_head("loss_head", yf, target)
    dx, d_final = rmsnorm_bwd("final_norm_bwd", dyf, x, final_norm, dm, (F32,))
    grads = [None] * len(layers)
    for l in reversed(range(len(layers))):
        dx, grads[l] = _layer_bwd(dx, saved[l], layers[l], tabs)
    return loss, dx, grads, d_final


HBM = pl.BlockSpec(memory_space=pl.ANY)


def _place():
    x, y, c = lax.axis_index("x"), lax.axis_index("y"), lax.axis_index("c")
    others = [(1 - x, y), (x, 1 - y), (1 - x, 1 - y)]
    return x, y, c, 2 * x + y, others


def _remote(src, dst, send_sem, recv_sem, device):
    return pltpu.make_async_remote_copy(src_ref=src, dst_ref=dst, send_sem=send_sem, recv_sem=recv_sem,
                                        device_id=device, device_id_type=MESH)


def gather_small(name, small):
    def body(in_ref, out_ref, send_sems, recv_sems, local_sem):
        x, y, c, chip, others = _place()
        mine = pltpu.make_async_copy(in_ref, out_ref.at[chip], local_sem)
        mine.start()
        copies = []
        for k, (ox, oy) in enumerate(others):
            cp = _remote(in_ref, out_ref.at[chip], send_sems.at[k], recv_sems.at[k], (ox, oy, c))
            cp.start()
            copies.append(cp)
        for k, (ox, oy) in enumerate(others):
            landed = out_ref.at[2 * ox + oy]
            _remote(landed, landed, send_sems.at[k], recv_sems.at[k], (ox, oy, c)).wait_recv()
        for cp in copies:
            cp.wait_send()
        mine.wait()

    return pl.pallas_call(
        body, name=name, in_specs=[HBM], out_specs=HBM, out_shape=jax.ShapeDtypeStruct((4,) + small.shape, small.dtype),
        scratch_shapes=[pltpu.SemaphoreType.DMA((3,)), pltpu.SemaphoreType.DMA((3,)), pltpu.SemaphoreType.DMA],
        compiler_params=pltpu.CompilerParams(has_side_effects=True))(small)


IN_HBM = pl.BlockSpec(memory_space=pltpu.HBM)
SEMS = pl.BlockSpec(memory_space=pltpu.SEMAPHORE)
DATAFLOW = pltpu.SideEffectType.DATAFLOW_SIDE_EFFECTING


BF16_ROWS_PER_TILE = 16


def _halves_by_rows(shape):
    return (shape[-2] // 2) % BF16_ROWS_PER_TILE == 0


def _half(shape, core):
    if _halves_by_rows(shape):
        size = shape[-2] // 2
        return (pl.ds(core * size, size), slice(None))
    size = shape[-1] // 2
    return (slice(None), pl.ds(core * size, size))


def _half_shape(shape):
    r, c = shape[-2:]
    return (r // 2, c) if _halves_by_rows(shape) else (r, c // 2)


def _gather_plan(bufs):
    x, y, c, chip, others = _place()
    plan = []
    for t, ref in enumerate(bufs):
        mine = _half(ref.shape, c)
        for k, (ox, oy) in enumerate(others):
            plan.append((3 * t + k, (ox, oy, c), ref.at[(chip,) + mine], ref.at[(chip,) + mine],
                         ref.at[(2 * ox + oy,) + mine]))
    return plan


def _exchange_plan(bufs):
    x, y, c, chip, others = _place()
    n = len(bufs) // 2
    plan = []
    for t in range(n):
        for k, (ox, oy) in enumerate(others):
            plan.append((3 * t + k, (ox, oy, c), bufs[t].at[2 * ox + oy], bufs[n + t].at[chip], bufs[n + t].at[2 * ox + oy]))
    return plan


def _sibling_plan(bufs):
    x, y, c, chip, others = _place()
    n = len(bufs) // 2
    plan = []
    for t in range(n):
        plan.append((t, (x, y, 1 - c), bufs[t].at[(slice(None),) + _half(bufs[t].shape, 1 - c)], bufs[n + t], bufs[n + t]))
    return plan


def _forward_plan(bufs):
    x, y, c, chip, others = _place()
    plan = []
    for t, ref in enumerate(bufs):
        mine, theirs = _half(ref.shape, c), _half(ref.shape, 1 - c)
        for k, (ox, oy) in enumerate(others):
            slot = 2 * ox + oy
            plan.append((3 * t + k, (x, y, 1 - c), ref.at[(slot,) + mine], ref.at[(slot,) + mine], ref.at[(slot,) + theirs]))
    return plan


def _join_plan(bufs):
    x, y, c, chip, others = _place()
    return [(t, (x, y, 1 - c), ref.at[_half(ref.shape, c)], ref.at[_half(ref.shape, c)], ref.at[_half(ref.shape, 1 - c)])
            for t, ref in enumerate(bufs)]


PLAN_COPIES = {_gather_plan: lambda n: 3 * n, _exchange_plan: lambda n: 3 * (n // 2), _sibling_plan: lambda n: n // 2,
               _forward_plan: lambda n: 3 * n, _join_plan: lambda n: n}


def split_start(name, bufs, plan_of, after):
    n = len(bufs)

    def body(*refs):
        ins = refs[:n]
        send_sems, recv_sems = refs[n + 1], refs[n + 2]
        token = refs[-1]
        for idx, peer, src, dst, _ in plan_of(ins):
            _remote(src, dst, send_sems.at[idx], recv_sems.at[idx], peer).start()
        token[...] = jnp.zeros_like(token)

    copies = PLAN_COPIES[plan_of](n)
    res = pl.pallas_call(
        body, name=name, in_specs=[IN_HBM] * n + [HBM],
        out_specs=(SEMS, SEMS) + (IN_HBM,) * n + (pl.BlockSpec(memory_space=pltpu.VMEM),),
        out_shape=(pltpu.SemaphoreType.DMA((copies,)), pltpu.SemaphoreType.DMA((copies,)))
        + tuple(pltpu.HBM(b.shape, b.dtype) for b in bufs) + (jax.ShapeDtypeStruct((8, LANES), F32),),
        input_output_aliases={t: 2 + t for t in range(n)},
        compiler_params=pltpu.CompilerParams(has_side_effects=DATAFLOW))(
            *[pltpu.with_memory_space_constraint(b, pltpu.HBM) for b in bufs], after)
    return res[0], res[1], list(res[2:2 + n]), res[-1]


def split_wait(name, send_sems, recv_sems, flying, plan_of, after):
    n = len(flying)

    def body(*refs):
        ins = refs[:n]
        send_ref, recv_ref = refs[n], refs[n + 1]
        for idx, peer, src, dst, landing in plan_of(ins):
            _remote(src, dst, send_ref.at[idx], recv_ref.at[idx], peer).wait_send()
            _remote(landing, landing, send_ref.at[idx], recv_ref.at[idx], peer).wait_recv()

    return list(pl.pallas_call(
        body, name=name, in_specs=[IN_HBM] * n + [SEMS, SEMS] + [HBM] * len(after), out_specs=(IN_HBM,) * n,
        out_shape=tuple(pltpu.HBM(b.shape, b.dtype) for b in flying),
        input_output_aliases={t: t for t in range(n)},
        compiler_params=pltpu.CompilerParams(has_side_effects=DATAFLOW))(*flying, send_sems, recv_sems, *after))


def _half_block(shape, tr):
    r, c = shape[-2:]
    if _halves_by_rows(shape):
        per = r // 2 // tr
        return tr, c, lambda i, core: (core * per + i, 0)
    return tr, c // 2, lambda i, core: (i, core)


def add_half(name, place, grad, theirs):
    hr, hc = _half_shape(grad.shape)
    tr = _rows(hr, hc)
    br, bc, at = _half_block(grad.shape, tr)

    def body(place_ref, g_ref, t_ref, o_ref):
        o_ref[...] = (g_ref[...] + t_ref[...]).astype(o_ref.dtype)

    whole = pl.BlockSpec((None, br, bc), lambda j, i, pr: (j, i, 0))
    return pl.pallas_call(
        body, name=name,
        grid_spec=pltpu.PrefetchScalarGridSpec(
            num_scalar_prefetch=1, grid=(4, hr // tr),
            in_specs=[pl.BlockSpec((None, br, bc), lambda j, i, pr: (j,) + at(i, pr[0])), whole],
            out_specs=whole),
        out_shape=jax.ShapeDtypeStruct((4, hr, hc), BF16),
        compiler_params=_params("parallel", "parallel"))(place, grad, theirs)


def sum_chips(name, place, own, parts, shard_shape):
    _, hr, hc = parts.shape
    tr = _rows(hr, hc)
    br, bc, at = _half_block(shard_shape, tr)

    def body(place_ref, own_ref, p1, p2, p3, o_ref):
        o_ref[...] = own_ref[...].astype(F32) + p1[...].astype(F32) + p2[...].astype(F32) + p3[...].astype(F32)

    def slot(k):
        return pl.BlockSpec((None, br, bc), lambda i, pr: (lax.rem(pr[1] + k, 4), i, 0))

    return pl.pallas_call(
        body, name=name,
        grid_spec=pltpu.PrefetchScalarGridSpec(
            num_scalar_prefetch=1, grid=(hr // tr,), in_specs=[slot(0), slot(1), slot(2), slot(3)],
            out_specs=pl.BlockSpec((br, bc), lambda i, pr: at(i, pr[0]))),
        out_shape=jax.ShapeDtypeStruct(tuple(shard_shape), F32),
        compiler_params=_params("parallel"))(place, own, parts, parts, parts)


def allreduce_small(name, buf):
    rows = buf.shape[0]

    def body(in_ref, out_ref, slots, send_sems, recv_sems):
        x, y, c, _, _ = _place()
        me = 4 * x + 2 * y + c
        slots[me] = in_ref[...]
        peers = []
        for k in range(1, 8):
            px = 1 - x if k & 4 else x
            py = 1 - y if k & 2 else y
            pc = 1 - c if k & 1 else c
            peers.append((px, py, pc))
        copies = []
        for k, peer in enumerate(peers):
            cp = _remote(in_ref, slots.at[me], send_sems.at[k], recv_sems.at[k], peer)
            cp.start()
            copies.append(cp)
        for k, (px, py, pc) in enumerate(peers):
            slot = slots.at[4 * px + 2 * py + pc]
            _remote(slot, slot, send_sems.at[k], recv_sems.at[k], (px, py, pc)).wait_recv()
        for cp in copies:
            cp.wait_send()
        acc = slots[0]
        for d in range(1, 8):
            acc = acc + slots[d]
        out_ref[...] = acc

    vm = pl.BlockSpec(memory_space=pltpu.VMEM)
    return pl.pallas_call(
        body, name=name, in_specs=[vm], out_specs=vm, out_shape=jax.ShapeDtypeStruct(buf.shape, F32),
        scratch_shapes=[pltpu.VMEM((8, rows, LANES), F32), pltpu.SemaphoreType.DMA((7,)), pltpu.SemaphoreType.DMA((7,))],
        compiler_params=pltpu.CompilerParams(has_side_effects=True, vmem_limit_bytes=VMEM_LIMIT))(buf)


def cast_layer(name, place, w, layer):
    _, r, cols = w.shape
    tr = _rows(r, cols)

    def body(place_ref, w_ref, o_ref):
        o_ref[...] = w_ref[...].astype(BF16)

    return pl.pallas_call(
        body, name=name,
        grid_spec=pltpu.PrefetchScalarGridSpec(
            num_scalar_prefetch=1, grid=(r // tr,),
            in_specs=[pl.BlockSpec((None, tr, cols), lambda i, pr: (layer, i, 0))],
            out_specs=pl.BlockSpec((None, tr, cols), lambda i, pr: (pr[1], i, 0))),
        out_shape=jax.ShapeDtypeStruct((4, r, cols), BF16),
        compiler_params=_params("parallel"))(place, w)


def _adamw_math(g, w, m, v):
    m = ADAM_B1 * m + (1.0 - ADAM_B1) * g
    v = ADAM_B2 * v + (1.0 - ADAM_B2) * (g * g)
    m_hat = m / (1.0 - ADAM_B1 ** ADAM_STEP)
    v_hat = v / (1.0 - ADAM_B2 ** ADAM_STEP)
    delta = -ADAM_LR * (m_hat / (jnp.sqrt(v_hat) + ADAM_EPS) + ADAM_WD * w)
    return delta, m, v


def adamw_layer(name, layer, g, w, m, v, prev=None):
    nl, r, cols = w.shape
    tr = _rows(r, cols, 8)

    def body(*refs):
        g_ref, w_ref, m_ref, v_ref = refs[:4]
        og, od, om, ov = refs[-4:]
        gv = g_ref[...]
        delta, m2, v2 = _adamw_math(gv, w_ref[...], m_ref[...], v_ref[...])
        og[...] = gv
        od[...] = delta
        om[...] = m2
        ov[...] = v2

    lay = pl.BlockSpec((None, tr, cols), lambda i: (layer, i, 0))
    ins = [g, w, m, v] + (list(prev) if prev is not None else [])
    return pl.pallas_call(
        body, name=name, grid=(r // tr,),
        in_specs=[pl.BlockSpec((tr, cols), lambda i: (i, 0)), lay, lay, lay] + ([HBM] * 4 if prev is not None else []),
        out_specs=[lay] * 4, out_shape=[jax.ShapeDtypeStruct((nl, r, cols), F32)] * 4,
        input_output_aliases={4 + k: k for k in range(4)} if prev is not None else {},
        compiler_params=_params("parallel"))(*ins)


def adamw_packed(name, g, w, m, v):
    def body(g_ref, w_ref, m_ref, v_ref, od, om, ov):
        delta, m2, v2 = _adamw_math(g_ref[...], w_ref[...], m_ref[...], v_ref[...])
        od[...] = delta
        om[...] = m2
        ov[...] = v2

    vm = pl.BlockSpec(memory_space=pltpu.VMEM)
    return pl.pallas_call(
        body, name=name, in_specs=[vm] * 4, out_specs=[vm] * 3, out_shape=[jax.ShapeDtypeStruct(g.shape, F32)] * 3,
        compiler_params=pltpu.CompilerParams(vmem_limit_bytes=VMEM_LIMIT))(g, w, m, v)


def _pack(arrays):
    flat = jnp.concatenate([a.reshape(-1) for a in arrays])
    pad = (-flat.shape[0]) % (8 * LANES)
    return jnp.pad(flat, (0, pad)).reshape(-1, LANES)


def _unpack(buf, shapes):
    flat = buf.reshape(-1)
    out, off = [], 0
    for shp in shapes:
        size = int(np.prod(shp))
        out.append(flat[off:off + size].reshape(shp))
        off += size
    return out


MIXER_W = ("w_in", "w_uq", "w_ukv", "w_out")
FFN_W = ("w_up", "w_down")
BIG = MIXER_W + FFN_W
TRANSPOSED = ("w_in", "w_uq")
COLUMN_CUT = ("w_ukv", "w_up")
SMALL = ("attn_norm", "a_q_norm", "a_k_norm", "c_q_norm", "c_kv_norm", "out_norm", "ffn_norm", "conv_b", "final_norm")
WEIGHTS = ("attn_norm", "w_in", "a_q_norm", "a_k_norm", "c_q_norm", "c_kv_norm", "w_uq", "w_ukv", "out_norm", "w_out",
           "ffn_norm", "w_up", "conv_w", "conv_b", "w_down", "final_norm")
INPUTS = ("x",) + WEIGHTS + ("loss_target",) + tuple("m_" + n for n in WEIGHTS) + tuple("v_" + n for n in WEIGHTS)


def _columns(g):
    return jnp.transpose(g, (1, 0, 2)).reshape(g.shape[1], 4 * g.shape[2])


def _uncolumns(w):
    r, c4 = w.shape
    return jnp.transpose(w.reshape(r, 4, c4 // 4), (1, 0, 2))


def _compute_layouts(full, conv_w, small, layer):
    p = {n: small[n][layer] for n in SMALL if n != "final_norm"}
    if "w_in" in full:
        p["w_in_t"] = full["w_in"].reshape(-1, full["w_in"].shape[2])
    if "w_uq" in full:
        uq_t = full["w_uq"].reshape(C_HEADS, HEAD + C_ROPE, C_RANK)
        p["w_uq_t"] = jnp.concatenate([uq_t[:, :HEAD].reshape(C_W, C_RANK),
                                       jnp.pad(uq_t[:, HEAD:], ((0, 0), (0, HEAD - C_ROPE), (0, 0))).reshape(C_W, C_RANK)], axis=0)
        p["w_ukv"] = _columns(full["w_ukv"])
        p["w_out"] = full["w_out"].reshape(-1, full["w_out"].shape[2])
    if "w_down" in full:
        p["w_down"] = full["w_down"].reshape(-1, full["w_down"].shape[2])
    if "w_up" in full:
        up = full["w_up"]
        p["w_up"] = up.reshape(2, 2, up.shape[1], up.shape[2])
        fh = conv_w.shape[1] // 2
        taps = jnp.transpose(conv_w.reshape(3, 2, fh), (1, 0, 2))
        p["cwb"] = jnp.concatenate([taps, small["conv_b"][layer].reshape(2, 1, fh), jnp.zeros((2, 4, fh), F32)], axis=1)
    return p


def _shard_layout(name, g):
    if name == "w_in":
        return g.reshape(4, -1, g.shape[1])
    if name == "w_uq":
        uq_t = jnp.concatenate([g[:C_W].reshape(C_HEADS, HEAD, C_RANK),
                                g[C_W:].reshape(C_HEADS, HEAD, C_RANK)[:, :C_ROPE]], axis=1)
        return uq_t.reshape(4, -1, C_RANK)
    if name == "w_ukv":
        return _uncolumns(g)
    if name == "w_up":
        return g.reshape(4, g.shape[2], g.shape[3])
    return g.reshape(4, -1, g.shape[1])


def _conv_grads(dcwb):
    fh = dcwb.shape[2]
    return jnp.transpose(dcwb[:, 0:3, :], (1, 0, 2)).reshape(3, 2 * fh), dcwb[:, 3, :].reshape(2 * fh)


def _shard_layouts(g):
    return ({n: _shard_layout(n, g[n]) for n in BIG},) + _conv_grads(g["cwb"])


def kernel(x, attn_norm, w_in, a_q_norm, a_k_norm, c_q_norm, c_kv_norm, w_uq, w_ukv, out_norm, w_out, ffn_norm, w_up, conv_w, conv_b, w_down, final_norm, loss_target, m_attn_norm, m_w_in, m_a_q_norm, m_a_k_norm, m_c_q_norm, m_c_kv_norm, m_w_uq, m_w_ukv, m_out_norm, m_w_out, m_ffn_norm, m_w_up, m_conv_w, m_conv_b, m_w_down, m_final_norm, v_attn_norm, v_w_in, v_a_q_norm, v_a_k_norm, v_c_q_norm, v_c_kv_norm, v_w_uq, v_w_ukv, v_out_norm, v_w_out, v_ffn_norm, v_w_up, v_conv_w, v_conv_b, v_w_down, v_final_norm):
    a = dict(zip(INPUTS, (x, attn_norm, w_in, a_q_norm, a_k_norm, c_q_norm, c_kv_norm, w_uq, w_ukv, out_norm, w_out, ffn_norm, w_up, conv_w, conv_b, w_down, final_norm, loss_target, m_attn_norm, m_w_in, m_a_q_norm, m_a_k_norm, m_c_q_norm, m_c_kv_norm, m_w_uq, m_w_ukv, m_out_norm, m_w_out, m_ffn_norm, m_w_up, m_conv_w, m_conv_b, m_w_down, m_final_norm, v_attn_norm, v_w_in, v_a_q_norm, v_a_k_norm, v_c_q_norm, v_c_kv_norm, v_w_uq, v_w_ukv, v_out_norm, v_w_out, v_ffn_norm, v_w_up, v_conv_w, v_conv_b, v_w_down, v_final_norm)))
    nl = w_in.shape[0]
    seq = x.shape[1]
    for n in TRANSPOSED:
        for kind in ("", "m_", "v_"):
            a[kind + n] = jnp.swapaxes(a[kind + n], 1, 2)
    chip = 2 * lax.axis_index("x") + lax.axis_index("y")
    place = jnp.stack([lax.axis_index("c"), chip]).astype(jnp.int32)
    tabs = _rope_tables(seq)

    assert nl == 2
    quarter = conv_w.shape[2]
    taps = jnp.pad(conv_w, ((0, 0), (0, 8 - conv_w.shape[1]), (0, 0))).reshape(nl * 8, quarter)
    taps_full = gather_small("gather_taps", taps).reshape(4, nl, 8, quarter)
    conv_full = [jnp.transpose(taps_full[:, l, 0:3, :], (1, 0, 2)).reshape(3, 4 * quarter) for l in range(nl)]
    flights = {}
    here, after = place, taps_full
    for l in range(nl):
        for key, names in ((f"in{l}", ("w_in",)), (f"mix{l}", MIXER_W[1:]), (f"up{l}", ("w_up",)), (f"down{l}", ("w_down",))):
            slots = [cast_layer(f"cast{l}_{n}", here, a[n], l) for n in names]
            send_sems, recv_sems, flying, token = split_start(f"gather_{key}_start", slots, _gather_plan, after)
            flights[key] = (names, send_sems, recv_sems, flying)
            here = after = place + token[0, 0].astype(jnp.int32)
    started = token[0, 0]

    def land(key, after):
        names, send_sems, recv_sems, flying = flights[key]
        landed = split_wait(f"gather_{key}_wait", send_sems, recv_sems, flying, _gather_plan, after)
        send_sems, recv_sems, passing, token = split_start(f"gather_{key}_pass_start", landed, _forward_plan, place)
        return (key, names, send_sems, recv_sems, passing), token

    def landed(passing, after, layer):
        key, names, send_sems, recv_sems, flying = passing
        full = split_wait(f"gather_{key}_pass_wait", send_sems, recv_sems, flying, _forward_plan, after)
        return _compute_layouts(dict(zip(names, full)), conv_full[layer], a, layer)

    h = x[0]
    layers, saved = [], []
    early = None
    for l in range(nl):
        if early is None:
            pass_in, token = land(f"in{l}", [taps_full])
            p = landed(pass_in, [token], l)
            p["attn_norm"] = p["attn_norm"] + started
        else:
            p = landed(early[0], [h], l)
        sv = _mixers_ab_fwd(h, p, tabs)
        if early is None:
            pass_mix, token = land(f"mix{l}", [sv["yb"]])
            p.update(landed(pass_mix, [token], l))
        else:
            p.update(landed(early[1], [sv["yb"]], l))
        _mixer_c_fwd(sv, p, tabs)
        pass_up, token = land(f"up{l}", [sv["yc"]])
        if l > 0:
            pass_down, token = land(f"down{l}", [token])
        x1 = _out_fwd(h, sv, p, token[0, 0])
        p.update(landed(pass_up, [x1], l))
        _ffn_up_fwd(x1, p, sv)
        if l == 0:
            pass_down, token = land(f"down{l}", [sv["act"]])
        else:
            token = sv["act"]
        if l + 1 < nl:
            pass_in, _ = land(f"in{l + 1}", [token])
            pass_mix, _ = land(f"mix{l + 1}", [token])
            early = (pass_in, pass_mix)
        p.update(landed(pass_down, [token], l))
        h = _ffn_down_fwd(x1, p, sv)
        layers.append(p)
        saved.append(sv)
    dm = h.shape[1]
    yf = rmsnorm_fwd("final_norm", h, final_norm, dm, F32)
    loss_here, dyf = loss_head("loss_head", yf, loss_target[0])
    dx, dxb, d_final = rmsnorm_bwd("final_norm_bwd", dyf, h, final_norm, dm, (F32, BF16))

    def swap_begin(tag, names, g):
        g_list = [_shard_layout(n, g[n]) for n in names]
        theirs = [lax.empty((4,) + _half_shape(s.shape), s.dtype) for s in g_list]
        send_sems, recv_sems, flying, token = split_start(f"{tag}_sibling_start", g_list + theirs, _sibling_plan, place)
        return (names, send_sems, recv_sems, flying), token[0, 0]

    def exchange_begin(tag, swap, after):
        names, send_sems, recv_sems, flying = swap
        landed = split_wait(f"{tag}_sibling_wait", send_sems, recv_sems, flying, _sibling_plan, after)
        g_list, theirs = landed[:len(names)], landed[len(names):]
        sums = [add_half(f"{tag}_add_{n}", place, gr, t) for n, gr, t in zip(names, g_list, theirs)]
        parts = [lax.empty(s.shape, s.dtype) for s in sums]
        send_sems, recv_sems, flying, token = split_start(f"{tag}_chips_start", sums + parts, _exchange_plan, place)
        return (names, [gr.shape[1:] for gr in g_list], send_sems, recv_sems, flying), token

    def exchange_end(tag, flight, after):
        names, shard_shapes, send_sems, recv_sems, flying = flight
        landed = split_wait(f"{tag}_chips_wait", send_sems, recv_sems, flying, _exchange_plan, after)
        sums, parts = landed[:len(names)], landed[len(names):]
        halves = [sum_chips(f"{tag}_sum_{n}", place, s, p, shp) for n, s, p, shp in zip(names, sums, parts, shard_shapes)]
        send_sems, recv_sems, joining, token = split_start(f"{tag}_join_start", halves, _join_plan, place)
        return (names, send_sems, recv_sems, joining), token

    def joined(tag, joining, after):
        names, send_sems, recv_sems, flying = joining
        return dict(zip(names, split_wait(f"{tag}_join_wait", send_sems, recv_sems, flying, _join_plan, after)))

    grads, conv_grads, flights_f, flights_m = [None] * nl, [None] * nl, [None] * nl, [None] * nl
    after = 0.0
    for l in reversed(range(nl)):
        sv, p = saved[l], layers[l]
        dxn, g = _ffn_bwd(dxb, sv, p, after)
        swap_f, after = swap_begin(f"reduce{l}f", FFN_W, g)
        dx1, dy, g_out = _out_proj_bwd(dxn, dx, sv, p, after)
        part, g_c = _mixer_c_bwd(dy, sv, p, tabs)
        flights_f[l], token = exchange_begin(f"reduce{l}f", swap_f, [part[2]])
        dhn, g_ab = _mixers_ab_bwd(part, sv, p, tabs, token[0, 0])
        g.update(g_out)
        g.update(g_c)
        g.update(g_ab)
        swap_m, after = swap_begin(f"reduce{l}m", MIXER_W, g)
        dx, dxb, g_norm = _attn_norm_bwd(dhn, dx1, sv, p, after)
        g.update(g_norm)
        grads[l], conv_grads[l] = g, _conv_grads(g["cwb"])
        before = [dx]
        if l == 0:
            small_g = {n: jnp.stack([grads[k][n] for k in range(nl)]) for n in SMALL if n not in ("conv_b", "final_norm")}
            small_g["conv_b"] = jnp.stack([cg[1] for cg in conv_grads])
            small_g["final_norm"] = d_final
            conv_w_g = jnp.stack([cg[0] for cg in conv_grads])
            small_sum = allreduce_small("reduce_small", _pack([small_g[n] for n in SMALL] + [conv_w_g, loss_here.reshape(1)]))
            before.append(small_sum)
        flights_m[l], token = exchange_begin(f"reduce{l}m", swap_m, before)
        after = token[0, 0]
    grad_x = dx

    out = {}
    join_f, token = exchange_end("reduce1f", flights_f[1], [token])
    join_m, token = exchange_end("reduce1m", flights_m[1], [token])
    reduced1 = joined("reduce1f", join_f, [token])
    for n in FFN_W:
        out[n] = adamw_layer(f"adamw1_{n}", 1, reduced1[n], a[n], a["m_" + n], a["v_" + n])
    reduced1 = joined("reduce1m", join_m, [out[n][3] for n in FFN_W])
    for n in MIXER_W:
        out[n] = adamw_layer(f"adamw1_{n}", 1, reduced1[n], a[n], a["m_" + n], a["v_" + n])

    summed = _unpack(small_sum, [a[n].shape for n in SMALL] + [conv_w_g.shape, (1,)])
    small_g = dict(zip(SMALL, summed[:-2]))
    small_g["conv_w"] = lax.dynamic_slice_in_dim(summed[-2], chip * quarter, quarter, axis=2)
    loss = summed[-1][0]
    names = SMALL + ("conv_w",)
    shapes = [a[n].shape for n in names]
    d_small, m_small, v_small = adamw_packed(
        "adamw_small", _pack([small_g[n] for n in names]), _pack([a[n] for n in names]),
        _pack([a["m_" + n] for n in names]), _pack([a["v_" + n] for n in names]))

    join_f, token = exchange_end("reduce0f", flights_f[0], [out[n][3] for n in BIG] + [d_small])
    join_m, token = exchange_end("reduce0m", flights_m[0], [token])
    reduced0 = joined("reduce0f", join_f, [token])
    for n in FFN_W:
        out[n] = adamw_layer(f"adamw0_{n}", 0, reduced0[n], a[n], a["m_" + n], a["v_" + n], prev=out[n])
    reduced0 = joined("reduce0m", join_m, [out[n][3] for n in FFN_W])
    for n in MIXER_W:
        out[n] = adamw_layer(f"adamw0_{n}", 0, reduced0[n], a[n], a["m_" + n], a["v_" + n], prev=out[n])
    for n, d_, m_, v_ in zip(names, _unpack(d_small, shapes), _unpack(m_small, shapes), _unpack(v_small, shapes)):
        out[n] = (small_g[n], d_, m_, v_)
    for n in TRANSPOSED:
        out[n] = [jnp.swapaxes(o, 1, 2) for o in out[n]]

    return (loss, grad_x[None]) + tuple(out[n][k] for k in range(4) for n in WEIGHTS)
```

```python
import functools

import jax
import jax.numpy as jnp
import numpy as np
from jax import lax
from jax.experimental import pallas as pl
from jax.experimental.pallas import tpu as pltpu

F32 = jnp.float32
BF16 = jnp.bfloat16
MESH = pl.DeviceIdType.MESH

V7X_VMEM_BYTES = 64 * 1024 * 1024
VMEM_LIMIT = V7X_VMEM_BYTES - 6 * 1024 * 1024
LANES = 128
ELEMENTWISE_BLOCK_BYTES = 4 * 1024 * 1024

HEAD = 128
A_HEADS, A_KV = 6, 2
B_HEADS, B_GROUPS = 4, 3
B_DILATIONS = (1, 4, 16)
B_HALF = 64
C_HEADS, C_RANK, C_ROPE = 6, 512, 64
GRID_W = 64
A_THETA, B_THETA, C_THETA = 10000.0, 500000.0, 10000.0
B_ROPE_DIM = 32
EPS = 1e-6
A_W, B_W, C_W = A_HEADS * HEAD, B_HEADS * HEAD, C_HEADS * HEAD
O_AQ, O_AK, O_AV = 0, 768, 1024
O_BQ, O_BK, O_BV = 1280, 2816, 3328
O_CQ, O_CKV, O_CKR = 3840, 4352, 4864
IN_W = 4928
PROJ_W = 5120

ADAM_LR, ADAM_B1, ADAM_B2, ADAM_EPS, ADAM_WD, ADAM_STEP = 0.001, 0.9, 0.999, 1e-08, 0.01, 10

NN = (((1,), (0,)), ((), ()))
NT = (((1,), (1,)), ((), ()))
TN = (((0,), (0,)), ((), ()))


def _dot(a, b, dims=NN):
    return lax.dot_general(a.astype(BF16), b.astype(BF16), dims, preferred_element_type=F32)


def _params(*sem):
    return pltpu.CompilerParams(dimension_semantics=sem if sem else None, vmem_limit_bytes=VMEM_LIMIT)


def _tile(n, target, unit=LANES):
    if n <= target:
        return n
    best = 0
    for t in range(unit, target + 1, unit):
        if n % t == 0:
            best = t
    return best if best else n


def _rows(r, width, itemsize=4):
    return _tile(r, max(8, ELEMENTWISE_BLOCK_BYTES // (width * itemsize)), 8)


def _matmul_call(name, dims, grid, a_spec, b_spec, o_spec, out_shape, acc_shape, add_spec=None):
    nk = grid[2]

    def body(*refs):
        a_ref, b_ref = refs[:2]
        add_ref = None if add_spec is None else refs[2]
        o_ref = refs[2 if add_spec is None else 3]

        def finish(r):
            if add_ref is not None:
                r = r + add_ref[...]
            o_ref[...] = r.astype(o_ref.dtype)

        if nk == 1:
            finish(_dot(a_ref[...], b_ref[...], dims))
            return
        acc = refs[-1]
        k = pl.program_id(2)

        @pl.when(k == 0)
        def _():
            acc[...] = _dot(a_ref[...], b_ref[...], dims)

        if nk > 2:
            @pl.when((k > 0) & (k < nk - 1))
            def _():
                acc[...] += _dot(a_ref[...], b_ref[...], dims)

        @pl.when(k == nk - 1)
        def _():
            finish(acc[...] + _dot(a_ref[...], b_ref[...], dims))

    in_specs = [a_spec, b_spec] + ([] if add_spec is None else [add_spec])
    return pl.pallas_call(
        body, name=name, grid=grid, in_specs=in_specs, out_specs=o_spec, out_shape=out_shape,
        scratch_shapes=[pltpu.VMEM(acc_shape, F32)] if nk > 1 else [],
        compiler_params=_params("parallel", "parallel", "arbitrary"))


def matmul(name, a, b, mode, out_dtype, add=None, tm=1024, tn=1024, tk=2816):
    if mode == "nn":
        (m, k), (k2, n) = a.shape, b.shape
    elif mode == "nt":
        (m, k), (n, k2) = a.shape, b.shape
    else:
        (k, m), (k2, n) = a.shape, b.shape
    assert k == k2, (name, a.shape, b.shape)
    tm, tn, tk = _tile(m, tm), _tile(n, tn), _tile(k, tk)
    grid = (m // tm, n // tn, k // tk)
    if mode == "tn":
        a_spec = pl.BlockSpec((tk, tm), lambda i, j, kk: (kk, i))
    else:
        a_spec = pl.BlockSpec((tm, tk), lambda i, j, kk: (i, kk))
    whole_b = {"pipeline_mode": pl.Buffered(1)} if grid[1] == 1 and grid[2] == 1 and grid[0] > 1 else {}
    if mode == "nt":
        b_spec = pl.BlockSpec((tn, tk), lambda i, j, kk: (j, kk), **whole_b)
    else:
        b_spec = pl.BlockSpec((tk, tn), lambda i, j, kk: (kk, j), **whole_b)
    o_spec = pl.BlockSpec((tm, tn), lambda i, j, kk: (i, j))
    dims = {"nn": NN, "nt": NT, "tn": TN}[mode]
    call = _matmul_call(name, dims, grid, a_spec, b_spec, o_spec, jax.ShapeDtypeStruct((m, n), out_dtype),
                        (tm, tn), None if add is None else o_spec)
    return call(a, b) if add is None else call(a, b, add)


def rmsnorm_fwd(name, x, g, width, out_dtype):
    r, cols = x.shape
    nb = cols // width
    tr = _rows(r, width)

    def body(x_ref, g_ref, o_ref):
        xv = x_ref[...]
        rs = lax.rsqrt(jnp.mean(xv * xv, axis=-1, keepdims=True) + EPS)
        o_ref[...] = (xv * rs * g_ref[...]).astype(o_ref.dtype)

    blk = pl.BlockSpec((tr, width), lambda i, j: (i, j))
    return pl.pallas_call(
        body, name=name, grid=(r // tr, nb),
        in_specs=[blk, pl.BlockSpec((1, width), lambda i, j: (0, 0))], out_specs=blk,
        out_shape=jax.ShapeDtypeStruct((r, cols), out_dtype),
        compiler_params=_params("parallel", "parallel"))(x, g.reshape(1, width))


def rmsnorm_bwd(name, dy, x, g, width, out_dtypes=(F32,), add=None):
    r, cols = x.shape
    nb = cols // width
    tr = _rows(r, width)
    n_out = len(out_dtypes)

    def body(*refs):
        dy_ref, x_ref, g_ref = refs[:3]
        add_ref = refs[3] if add is not None else None
        outs = refs[-(n_out + 1):-1]
        dg_ref = refs[-1]
        xv = x_ref[...]
        dyv = dy_ref[...].astype(F32)
        rs = lax.rsqrt(jnp.mean(xv * xv, axis=-1, keepdims=True) + EPS)
        xh = xv * rs
        dyg = dyv * g_ref[...]
        dx = rs * (dyg - xh * jnp.mean(dyg * xh, axis=-1, keepdims=True))
        if add_ref is not None:
            dx = dx + add_ref[...]
        for o in outs:
            o[...] = dx.astype(o.dtype)

        @pl.when((pl.program_id(0) == 0) & (pl.program_id(1) == 0))
        def _():
            dg_ref[...] = jnp.zeros_like(dg_ref)

        dg_ref[...] += jnp.sum(dyv * xh, axis=0, keepdims=True)

    blk = pl.BlockSpec((tr, width), lambda i, j: (i, j))
    vec = pl.BlockSpec((1, width), lambda i, j: (0, 0))
    ins = [dy, x, g.reshape(1, width)] + ([add] if add is not None else [])
    res = pl.pallas_call(
        body, name=name, grid=(r // tr, nb),
        in_specs=[blk, blk, vec] + ([blk] if add is not None else []),
        out_specs=[blk] * n_out + [vec],
        out_shape=[jax.ShapeDtypeStruct((r, cols), dt) for dt in out_dtypes] + [jax.ShapeDtypeStruct((1, width), F32)],
        compiler_params=_params("arbitrary", "arbitrary"))(*ins)
    return tuple(res[:n_out]) + (res[n_out].reshape(width),)


def _rope_angles(pos, dim, theta):
    inv = theta ** (-jnp.arange(0, dim, 2, dtype=F32) / dim)
    ang = pos.astype(F32)[:, None] * inv[None, :]
    return jnp.cos(ang), jnp.sin(ang)


def _rope_tables(s):
    rows = s // GRID_W
    row_pos = jnp.repeat(jnp.arange(rows), GRID_W)
    col_pos = jnp.tile(jnp.arange(GRID_W), rows)
    t_pos = jnp.arange(s)
    z = lambda n: jnp.zeros((s, n), F32)
    o = lambda n: jnp.ones((s, n), F32)
    cr, sr = _rope_angles(row_pos, HEAD // 2, A_THETA)
    cc, sc = _rope_angles(col_pos, HEAD // 2, A_THETA)
    tab_a = (jnp.concatenate([cr, cr, cc, cc], 1), jnp.concatenate([-sr, z(32), -sc, z(32)], 1),
             jnp.concatenate([z(32), sr, z(32), sc], 1), 32)
    cp, sp = _rope_angles(t_pos, B_ROPE_DIM, B_THETA)
    tab_b = (jnp.concatenate([cp, cp, o(96)], 1), jnp.concatenate([-sp, z(112)], 1),
             jnp.concatenate([z(16), sp, z(96)], 1), 16)
    cm, sm = _rope_angles(t_pos, C_ROPE, C_THETA)
    tab_c = (jnp.concatenate([cm, cm, o(64)], 1), jnp.concatenate([-sm, z(96)], 1),
             jnp.concatenate([z(32), sm, z(64)], 1), 32)

    def transposed(tab):
        c, s1, s2, h = tab
        return (c, jnp.roll(s2, -h, axis=1), jnp.roll(s1, h, axis=1), h)

    return {k: (t, transposed(t)) for k, t in (("a", tab_a), ("b", tab_b), ("c", tab_c))}


def rope(name, x, tab, out_dtype, plain_heads=0):
    c, s1, s2, h = tab
    s, cols = x.shape
    ts = _rows(s, cols)

    def body(x_ref, c_ref, s1_ref, s2_ref, o_ref):
        cv, s1v, s2v = c_ref[...], s1_ref[...], s2_ref[...]
        for j in range(cols // HEAD):
            lanes = slice(HEAD * j, HEAD * (j + 1))
            xv = x_ref[:, lanes].astype(F32)
            if j >= plain_heads:
                xv = xv * cv + pltpu.roll(xv, HEAD - h, 1) * s1v + pltpu.roll(xv, h, 1) * s2v
            o_ref[:, lanes] = xv.astype(o_ref.dtype)

    blk = pl.BlockSpec((ts, cols), lambda i: (i, 0))
    tb = pl.BlockSpec((ts, HEAD), lambda i: (i, 0))
    return pl.pallas_call(
        body, name=name, grid=(s // ts,), in_specs=[blk, tb, tb, tb], out_specs=blk,
        out_shape=jax.ShapeDtypeStruct((s, cols), out_dtype),
        compiler_params=_params("parallel"))(x, c, s1, s2)


PREP_ROWS = 256


def _rope_of(x, tab, h):
    c, s1, s2 = tab
    return x * c + pltpu.roll(x, HEAD - h, 1) * s1 + pltpu.roll(x, h, 1) * s2


def _heads(start, n):
    return [(j, slice(start + HEAD * j, start + HEAD * (j + 1))) for j in range(n)]


def mixer_prep(name, proj, p, tabs):
    s = proj.shape[0]
    tr = PREP_ROWS
    (ca, s1a, s2a, ha), (cb, s1b, s2b, hb) = tabs["a"][0], tabs["b"][0]

    def body(proj_ref, gaq, gak, gcq, gckv, ca_r, s1a_r, s2a_r, cb_r, s1b_r, s2b_r,
             qa_ref, ka_ref, av_ref, qb_ref, kb_ref, cqn_ref, ckvn_ref):
        ta = (ca_r[...], s1a_r[...], s2a_r[...])
        tb = (cb_r[...], s1b_r[...], s2b_r[...])

        def normed(x, g):
            return x * lax.rsqrt(jnp.mean(x * x, axis=-1, keepdims=True) + EPS) * g

        for (src0, n, gain, out) in ((O_AQ, A_HEADS, gaq, qa_ref), (O_AK, A_KV, gak, ka_ref)):
            for j, cols in _heads(src0, n):
                out[:, HEAD * j:HEAD * (j + 1)] = _rope_of(normed(proj_ref[:, cols], gain[...]), ta, ha).astype(BF16)
        av_ref[...] = proj_ref[:, O_AV:O_BQ].astype(BF16)
        for (src0, n, out) in ((O_BQ, B_GROUPS * B_HEADS, qb_ref), (O_BK, B_HEADS, kb_ref)):
            for j, cols in _heads(src0, n):
                out[:, HEAD * j:HEAD * (j + 1)] = _rope_of(proj_ref[:, cols], tb, hb)
        cqn_ref[...] = normed(proj_ref[:, O_CQ:O_CKV], gcq[...]).astype(BF16)
        ckvn_ref[...] = normed(proj_ref[:, O_CKV:O_CKR], gckv[...]).astype(BF16)

    rows = lambda w: pl.BlockSpec((tr, w), lambda i: (i, 0))
    vec = lambda w: pl.BlockSpec((1, w), lambda i: (0, 0))
    widths = (A_W, A_KV * HEAD, A_KV * HEAD, B_GROUPS * B_W, B_W, C_RANK, C_RANK)
    dtypes = (BF16, BF16, BF16, F32, F32, BF16, BF16)
    return pl.pallas_call(
        body, name=name, grid=(s // tr,),
        in_specs=[rows(proj.shape[1]), vec(HEAD), vec(HEAD), vec(C_RANK), vec(C_RANK)] + [rows(HEAD)] * 6,
        out_specs=[rows(w) for w in widths],
        out_shape=[jax.ShapeDtypeStruct((s, w), dt) for w, dt in zip(widths, dtypes)],
        compiler_params=_params("parallel"))(
            proj, p["a_q_norm"].reshape(1, HEAD), p["a_k_norm"].reshape(1, HEAD), p["c_q_norm"].reshape(1, C_RANK),
            p["c_kv_norm"].reshape(1, C_RANK), ca, s1a, s2a, cb, s1b, s2b)


def mixer_prep_bwd(name, proj, p, tabs, dqa, dka, dva, dqb, dkb, dvb, dcqn, dckvn, dckr):
    s = proj.shape[0]
    tr = PREP_ROWS
    (ca, s1a, s2a, ha), (cb, s1b, s2b, hb) = tabs["a"][1], tabs["b"][1]

    def body(proj_ref, dqa_ref, dka_ref, dva_ref, dqb_ref, dkb_ref, dvb_ref, dcqn_ref, dckvn_ref, dckr_ref,
             gaq, gak, gcq, gckv, ca_r, s1a_r, s2a_r, cb_r, s1b_r, s2b_r, dproj_ref, dgaq, dgak, dgcq, dgckv):
        ta = (ca_r[...], s1a_r[...], s2a_r[...])
        tb = (cb_r[...], s1b_r[...], s2b_r[...])

        @pl.when(pl.program_id(0) == 0)
        def _():
            for ref in (dgaq, dgak, dgcq, dgckv):
                ref[...] = jnp.zeros_like(ref)

        def norm_bwd(dy, x, gain, dgain):
            rs = lax.rsqrt(jnp.mean(x * x, axis=-1, keepdims=True) + EPS)
            xh = x * rs
            dyg = dy * gain[...]
            dgain[...] += jnp.sum(dy * xh, axis=0, keepdims=True)
            return rs * (dyg - xh * jnp.mean(dyg * xh, axis=-1, keepdims=True))

        for (dst0, n, d_ref, gain, dgain) in ((O_AQ, A_HEADS, dqa_ref, gaq, dgaq), (O_AK, A_KV, dka_ref, gak, dgak)):
            for j, cols in _heads(dst0, n):
                dy = _rope_of(d_ref[:, HEAD * j:HEAD * (j + 1)], ta, ha)
                dproj_ref[:, cols] = norm_bwd(dy, proj_ref[:, cols], gain, dgain).astype(BF16)
        dproj_ref[:, O_AV:O_BQ] = dva_ref[...].astype(BF16)
        for (dst0, n, d_ref) in ((O_BQ, B_GROUPS * B_HEADS, dqb_ref), (O_BK, B_HEADS, dkb_ref)):
            for j, cols in _heads(dst0, n):
                dproj_ref[:, cols] = _rope_of(d_ref[:, HEAD * j:HEAD * (j + 1)], tb, hb).astype(BF16)
        dproj_ref[:, O_BV:O_CQ] = dvb_ref[...].astype(BF16)
        dproj_ref[:, O_CQ:O_CKV] = norm_bwd(dcqn_ref[...], proj_ref[:, O_CQ:O_CKV], gcq, dgcq).astype(BF16)
        dproj_ref[:, O_CKV:O_CKR] = norm_bwd(dckvn_ref[...], proj_ref[:, O_CKV:O_CKR], gckv, dgckv).astype(BF16)
        dproj_ref[:, O_CKR:] = dckr_ref[:, :C_ROPE]

    rows = lambda w: pl.BlockSpec((tr, w), lambda i: (i, 0))
    vec = lambda w: pl.BlockSpec((1, w), lambda i: (0, 0))
    grads = (dqa, dka, dva, dqb, dkb, dvb, dcqn, dckvn, dckr)
    res = pl.pallas_call(
        body, name=name, grid=(s // tr,),
        in_specs=[rows(proj.shape[1])] + [rows(g.shape[1]) for g in grads]
        + [vec(HEAD), vec(HEAD), vec(C_RANK), vec(C_RANK)] + [rows(HEAD)] * 6,
        out_specs=[rows(proj.shape[1]), vec(HEAD), vec(HEAD), vec(C_RANK), vec(C_RANK)],
        out_shape=[jax.ShapeDtypeStruct(proj.shape, BF16)] + [jax.ShapeDtypeStruct((1, w), F32) for w in (HEAD, HEAD, C_RANK, C_RANK)],
        compiler_params=_params("arbitrary"))(
            proj, *grads, p["a_q_norm"].reshape(1, HEAD), p["a_k_norm"].reshape(1, HEAD), p["c_q_norm"].reshape(1, C_RANK),
            p["c_kv_norm"].reshape(1, C_RANK), ca, s1a, s2a, cb, s1b, s2b)
    return res[0], res[1].reshape(HEAD), res[2].reshape(HEAD), res[3].reshape(C_RANK), res[4].reshape(C_RANK)


OUT_GROUPS = ((0, A_W), (A_W, A_W + B_W), (A_W + B_W, A_W + B_W + C_W))


def out_norms(name, ya, yb, yc, gain):
    s = ya.shape[0]
    tr = PREP_ROWS
    width = OUT_GROUPS[-1][1]

    def body(ya_ref, yb_ref, yc_ref, g_ref, y_ref):
        for (lo, hi), src in zip(OUT_GROUPS, (ya_ref, yb_ref, yc_ref)):
            x = src[...]
            rs = lax.rsqrt(jnp.mean(x * x, axis=-1, keepdims=True) + EPS)
            y_ref[:, lo:hi] = (x * rs * g_ref[:, lo:hi]).astype(BF16)

    rows = lambda w: pl.BlockSpec((tr, w), lambda i: (i, 0))
    return pl.pallas_call(
        body, name=name, grid=(s // tr,),
        in_specs=[rows(A_W), rows(B_W), rows(C_W), pl.BlockSpec((1, width), lambda i: (0, 0))], out_specs=rows(width),
        out_shape=jax.ShapeDtypeStruct((s, width), BF16), compiler_params=_params("parallel"))(
            ya, yb, yc, gain.reshape(1, width))


def out_norms_bwd(name, dy, ya, yb, yc, gain):
    s = ya.shape[0]
    tr = PREP_ROWS
    width = OUT_GROUPS[-1][1]

    def body(dy_ref, ya_ref, yb_ref, yc_ref, g_ref, dya_ref, dyb_ref, dyc_ref, dg_ref):
        @pl.when(pl.program_id(0) == 0)
        def _():
            dg_ref[...] = jnp.zeros_like(dg_ref)

        for (lo, hi), src, dst in zip(OUT_GROUPS, (ya_ref, yb_ref, yc_ref), (dya_ref, dyb_ref, dyc_ref)):
            x = src[...]
            d = dy_ref[:, lo:hi]
            rs = lax.rsqrt(jnp.mean(x * x, axis=-1, keepdims=True) + EPS)
            xh = x * rs
            dg = d * g_ref[:, lo:hi]
            dst[...] = (rs * (dg - xh * jnp.mean(dg * xh, axis=-1, keepdims=True))).astype(dst.dtype)
            dg_ref[:, lo:hi] += jnp.sum(d * xh, axis=0, keepdims=True)

    rows = lambda w: pl.BlockSpec((tr, w), lambda i: (i, 0))
    vec = pl.BlockSpec((1, width), lambda i: (0, 0))
    dya, dyb, dyc, dg = pl.pallas_call(
        body, name=name, grid=(s // tr,),
        in_specs=[rows(width), rows(A_W), rows(B_W), rows(C_W), vec], out_specs=[rows(A_W), rows(B_W), rows(C_W), vec],
        out_shape=[jax.ShapeDtypeStruct((s, A_W), BF16), jax.ShapeDtypeStruct((s, B_W), F32),
                   jax.ShapeDtypeStruct((s, C_W), BF16), jax.ShapeDtypeStruct((1, width), F32)],
        compiler_params=_params("arbitrary"))(dy, ya, yb, yc, gain.reshape(1, width))
    return dya, dyb, dyc, dg.reshape(width)


ATTN_TQ = 256
ATTN_STEP_ROWS = 512
ATTN_FWD_STEP_ROWS = 1024


def attn_fwd(name, parts, v, v_group, nheads, scale):
    s = v.shape[0]
    tq = ATTN_FWD_STEP_ROWS
    npart = len(parts)

    def body(*refs):
        v_ref, o_ref, lse_ref = refs[2 * npart:]
        for r0 in range(0, tq, ATTN_TQ):
            rows = slice(r0, r0 + ATTN_TQ)
            sc = None
            for p in range(npart):
                t = _dot(refs[2 * p][rows, :], refs[2 * p + 1][...], NT)
                sc = t if sc is None else sc + t
            sc = sc * scale
            m = jnp.max(sc, axis=-1, keepdims=True)
            e = jnp.exp(sc - m)
            l = jnp.sum(e, axis=-1, keepdims=True)
            o_ref[rows, :] = _dot(e.astype(BF16), v_ref[...]) / l
            lse_ref[rows, :] = jnp.broadcast_to(m + jnp.log(l), (ATTN_TQ, HEAD))

    in_specs, ins = [], []
    for q, qoff, k, kg in parts:
        in_specs.append(pl.BlockSpec((tq, HEAD), lambda h, i, qoff=qoff: (i, qoff + h)))
        in_specs.append(pl.BlockSpec((s, HEAD), lambda h, i, kg=kg: (0, h // kg)))
        ins += [q, k]
    in_specs.append(pl.BlockSpec((s, HEAD), lambda h, i: (0, h // v_group)))
    out_blk = pl.BlockSpec((tq, HEAD), lambda h, i: (i, h))
    return pl.pallas_call(
        body, name=name, grid=(nheads, s // tq), in_specs=in_specs, out_specs=[out_blk, out_blk],
        out_shape=[jax.ShapeDtypeStruct((s, nheads * HEAD), F32)] * 2,
        compiler_params=_params("parallel", "parallel"))(*ins, v)


def attn_bwd(name, parts, v, v_group, lse, o, do, nheads, scale):
    s = v.shape[0]
    tq = ATTN_STEP_ROWS
    npart = len(parts)

    def body(*refs):
        v_ref, lse_ref, o_ref, do_ref = refs[2 * npart:2 * npart + 4]
        outs = refs[2 * npart + 4:]
        dq_refs, dk_refs, dv_ref = outs[:npart], outs[npart:2 * npart], outs[2 * npart]
        h, i = pl.program_id(0), pl.program_id(1)

        @pl.when((i == 0) & (h % v_group == 0))
        def _():
            dv_ref[...] = jnp.zeros_like(dv_ref)

        for p in range(npart):
            @pl.when((i == 0) & (h % parts[p][3] == 0))
            def _(p=p):
                dk_refs[p][...] = jnp.zeros_like(dk_refs[p])

        for r0 in range(0, tq, ATTN_TQ):
            rows = slice(r0, r0 + ATTN_TQ)
            sc = None
            for p in range(npart):
                t = _dot(refs[2 * p][rows, :], refs[2 * p + 1][...], NT)
                sc = t if sc is None else sc + t
            pr = jnp.exp(sc * scale - lse_ref[rows, 0:1])
            dov = do_ref[rows, :]
            delta = jnp.sum(dov.astype(F32) * o_ref[rows, :], axis=-1, keepdims=True)
            dp = _dot(dov, v_ref[...], NT)
            ds = (pr * (dp - delta) * scale).astype(BF16)
            dv_ref[...] += _dot(pr.astype(BF16), dov, TN)
            for p in range(npart):
                dq_refs[p][rows, :] = _dot(ds, refs[2 * p + 1][...])
                dk_refs[p][...] += _dot(ds, refs[2 * p][rows, :], TN)

    in_specs, ins = [], []
    for q, qoff, k, kg in parts:
        in_specs.append(pl.BlockSpec((tq, HEAD), lambda h, i, qoff=qoff: (i, qoff + h)))
        in_specs.append(pl.BlockSpec((s, HEAD), lambda h, i, kg=kg: (0, h // kg)))
        ins += [q, k]
    hq_blk = pl.BlockSpec((tq, HEAD), lambda h, i: (i, h))
    in_specs += [pl.BlockSpec((s, HEAD), lambda h, i: (0, h // v_group)), hq_blk, hq_blk, hq_blk]
    out_specs = [hq_blk] * npart
    out_shape = [jax.ShapeDtypeStruct((s, nheads * HEAD), F32)] * npart
    for q, qoff, k, kg in parts:
        out_specs.append(pl.BlockSpec((s, HEAD), lambda h, i, kg=kg: (0, h // kg)))
        out_shape.append(jax.ShapeDtypeStruct((s, nheads // kg * HEAD), F32))
    out_specs.append(pl.BlockSpec((s, HEAD), lambda h, i: (0, h // v_group)))
    out_shape.append(jax.ShapeDtypeStruct((s, nheads // v_group * HEAD), F32))
    res = pl.pallas_call(
        body, name=name, grid=(nheads, s // tq), in_specs=in_specs, out_specs=out_specs, out_shape=out_shape,
        compiler_params=_params("arbitrary", "arbitrary"))(*ins, v, lse, o, do)
    return list(res[:npart]), list(res[npart:2 * npart]), res[2 * npart]


def _band_windows(lf):
    for ib in range(lf // HEAD):
        q0 = ib * HEAD
        yield q0, max(0, q0 - B_HALF), min(lf, q0 + HEAD + B_HALF)


def _band_mask(q0, lo, hi):
    qpos = q0 + lax.broadcasted_iota(jnp.int32, (HEAD, hi - lo), 0)
    kpos = lo + lax.broadcasted_iota(jnp.int32, (HEAD, hi - lo), 1)
    return jnp.abs(qpos - kpos) <= B_HALF


def _class_rows(r, d, start, size):
    return pl.ds(r + d * start, size, stride=d) if d > 1 else pl.ds(start, size)


def dilated_fwd(name, q, k, v_src, v_block):
    s = k.shape[0]
    scale = HEAD ** -0.5

    def body(q_ref, k_ref, v_ref, y_ref, lse_ref, o_scr, l_scr):
        g_now = pl.program_id(1)
        for g, d in enumerate(B_DILATIONS):
            @pl.when(g_now == g)
            def _(g=g, d=d):
                for r in range(d):
                    for q0, lo, hi in _band_windows(s // d):
                        mine = _class_rows(r, d, q0, HEAD)
                        keys = _class_rows(r, d, lo, hi - lo)
                        sc = _dot(q_ref[mine, :].astype(BF16), k_ref[keys, :].astype(BF16), NT) * scale
                        sc = jnp.where(_band_mask(q0, lo, hi), sc, -1e30)
                        m = jnp.max(sc, axis=-1, keepdims=True)
                        e = jnp.exp(sc - m)
                        l = jnp.sum(e, axis=-1, keepdims=True)
                        o_scr.at[g][mine, :] = _dot((e / l).astype(BF16), v_ref[keys, :].astype(BF16))
                        l_scr.at[g][mine, :] = jnp.broadcast_to(m + jnp.log(l), (HEAD, HEAD))

        @pl.when(g_now == B_GROUPS - 1)
        def _():
            a, b, c = l_scr[0], l_scr[1], l_scr[2]
            m = jnp.maximum(jnp.maximum(a, b), c)
            ea, eb, ec = jnp.exp(a - m), jnp.exp(b - m), jnp.exp(c - m)
            den = ea + eb + ec
            y_ref[...] = (ea / den) * o_scr[0] + (eb / den) * o_scr[1] + (ec / den) * o_scr[2]
            lse_ref[...] = m + jnp.log(den)

    blk = lambda f: pl.BlockSpec((s, HEAD), f)
    per_head = blk(lambda h, g: (0, h))
    return pl.pallas_call(
        body, name=name, grid=(B_HEADS, B_GROUPS),
        in_specs=[blk(lambda h, g: (0, B_HEADS * g + h)), per_head, blk(lambda h, g: (0, v_block + h))],
        out_specs=[per_head, per_head], out_shape=[jax.ShapeDtypeStruct((s, B_W), F32)] * 2,
        scratch_shapes=[pltpu.VMEM((B_GROUPS, s, HEAD), F32)] * 2,
        compiler_params=_params("parallel", "arbitrary"))(q, k, v_src)


def dilated_bwd(name, q, k, v_src, v_block, dy, y, lse):
    s = k.shape[0]
    scale = HEAD ** -0.5

    def body(q_ref, k_ref, v_ref, dy_ref, y_ref, lse_ref, dq_ref, dk_ref, dv_ref, delta):
        g_now = pl.program_id(1)

        @pl.when(g_now == 0)
        def _():
            dk_ref[...] = jnp.zeros_like(dk_ref)
            dv_ref[...] = jnp.zeros_like(dv_ref)
            delta[...] = jnp.broadcast_to(jnp.sum(dy_ref[...] * y_ref[...], axis=-1, keepdims=True), (s, HEAD))

        for g, d in enumerate(B_DILATIONS):
            @pl.when(g_now == g)
            def _(d=d):
                for r in range(d):
                    for q0, lo, hi in _band_windows(s // d):
                        mine = _class_rows(r, d, q0, HEAD)
                        keys = _class_rows(r, d, lo, hi - lo)
                        qv, kv = q_ref[mine, :].astype(BF16), k_ref[keys, :].astype(BF16)
                        dyv = dy_ref[mine, :].astype(BF16)
                        sc = _dot(qv, kv, NT) * scale
                        pr = jnp.where(_band_mask(q0, lo, hi), jnp.exp(sc - lse_ref[mine, :][:, 0:1]), 0.0)
                        dp = _dot(dyv, v_ref[keys, :].astype(BF16), NT)
                        ds = (pr * (dp - delta[mine, :][:, 0:1]) * scale).astype(BF16)
                        dv_ref[keys, :] = dv_ref[keys, :] + _dot(pr.astype(BF16), dyv, TN)
                        dq_ref[mine, :] = _dot(ds, kv)
                        dk_ref[keys, :] = dk_ref[keys, :] + _dot(ds, qv, TN)

    blk = lambda f: pl.BlockSpec((s, HEAD), f)
    per_head = blk(lambda h, g: (0, h))
    by_group = blk(lambda h, g: (0, B_HEADS * g + h))
    return pl.pallas_call(
        body, name=name, grid=(B_HEADS, B_GROUPS),
        in_specs=[by_group, per_head, blk(lambda h, g: (0, v_block + h)), per_head, per_head, per_head],
        out_specs=[by_group, per_head, per_head],
        out_shape=[jax.ShapeDtypeStruct((s, B_GROUPS * B_W), F32)] + [jax.ShapeDtypeStruct((s, B_W), F32)] * 2,
        scratch_shapes=[pltpu.VMEM((s, HEAD), F32)],
        compiler_params=_params("parallel", "arbitrary"))(q, k, v_src, dy, y, lse)


FFN_TN = 256


def _edges(shape):
    row = lax.broadcasted_iota(jnp.int32, shape, 0)
    return row == 0, row == shape[0] - 1


def _shifted(h, edges):
    first, last = edges
    s = h.shape[0]
    return jnp.where(first, 0.0, pltpu.roll(h, 1, 0)), jnp.where(last, 0.0, pltpu.roll(h, s - 1, 0))


def _conv(h, cw, edges):
    prev, nxt = _shifted(h, edges)
    return prev * cw[0:1, :] + h * cw[1:2, :] + nxt * cw[2:3, :] + cw[3:4, :], prev, nxt


def _sigmoid(x):
    return 1.0 / (1.0 + jnp.exp(-x))


def ffn_up(name, xn, w_up, cwb):
    s, dm = xn.shape
    quarter = w_up.shape[3]
    fh = 2 * quarter
    tn = FFN_TN
    per = quarter // tn

    def body(x_ref, w_ref, cw_ref, h_ref, act_ref):
        xv = x_ref[...]
        hg = _dot(xv, w_ref[0])
        hu = _dot(xv, w_ref[1])
        h_ref[0] = hg
        h_ref[1] = hu
        edges = _edges(hg.shape)
        gc, _, _ = _conv(hg, cw_ref[0], edges)
        uc, _, _ = _conv(hu, cw_ref[1], edges)
        act_ref[...] = (gc * _sigmoid(gc) * uc).astype(BF16)

    return pl.pallas_call(
        body, name=name, grid=(fh // tn,),
        in_specs=[pl.BlockSpec((s, dm), lambda t: (0, 0), pipeline_mode=pl.Buffered(1)),
                  pl.BlockSpec((2, None, dm, tn), lambda t: (0, t // per, 0, t % per)),
                  pl.BlockSpec((2, 8, tn), lambda t: (0, 0, t))],
        out_specs=[pl.BlockSpec((2, s, tn), lambda t: (0, 0, t)), pl.BlockSpec((s, tn), lambda t: (0, t))],
        out_shape=[jax.ShapeDtypeStruct((2, s, fh), F32), jax.ShapeDtypeStruct((s, fh), BF16)],
        compiler_params=_params("parallel"))(xn, w_up, cwb)


def ffn_gate_bwd(name, h, dact, cwb):
    _, s, fh = h.shape
    tn = FFN_TN

    def body(h_ref, da_ref, cw_ref, dh_ref, dcw_ref):
        edges = _edges((s, tn))
        gc, gp, gn = _conv(h_ref[0], cw_ref[0], edges)
        uc, up, un = _conv(h_ref[1], cw_ref[1], edges)
        sg = _sigmoid(gc)
        da = da_ref[...]
        dgc = da * uc * (sg * (1.0 + gc * (1.0 - sg)))
        duc = da * (gc * sg)
        for idx, (hv, prev, nxt, dc) in enumerate(((h_ref[0], gp, gn, dgc), (h_ref[1], up, un, duc))):
            cw = cw_ref[idx]
            from_prev, from_next = _shifted(dc, edges)
            dh_ref[idx] = (from_next * cw[0:1, :] + dc * cw[1:2, :] + from_prev * cw[2:3, :]).astype(BF16)
            dcw_ref[idx, 0:1, :] = jnp.sum(prev * dc, axis=0, keepdims=True)
            dcw_ref[idx, 1:2, :] = jnp.sum(hv * dc, axis=0, keepdims=True)
            dcw_ref[idx, 2:3, :] = jnp.sum(nxt * dc, axis=0, keepdims=True)
            dcw_ref[idx, 3:4, :] = jnp.sum(dc, axis=0, keepdims=True)
            dcw_ref[idx, 4:8, :] = jnp.zeros((4, tn), F32)

    return pl.pallas_call(
        body, name=name, grid=(fh // tn,),
        in_specs=[pl.BlockSpec((2, s, tn), lambda t: (0, 0, t)), pl.BlockSpec((s, tn), lambda t: (0, t)),
                  pl.BlockSpec((2, 8, tn), lambda t: (0, 0, t))],
        out_specs=[pl.BlockSpec((2, s, tn), lambda t: (0, 0, t)), pl.BlockSpec((2, 8, tn), lambda t: (0, 0, t))],
        out_shape=[jax.ShapeDtypeStruct((2, s, fh), BF16), jax.ShapeDtypeStruct((2, 8, fh), F32)],
        compiler_params=_params("parallel"))(h, dact, cwb)


def ffn_dx(name, dh, w_up):
    _, s, fh = dh.shape
    dm, quarter = w_up.shape[2], w_up.shape[3]
    tk = _tile(quarter, 2816)
    per = quarter // tk
    tm, tn = _tile(s, 1024), _tile(dm, 1024)
    grid = (s // tm, dm // tn, 4 * per)
    a_spec = pl.BlockSpec((None, tm, tk), lambda i, j, k: (k // (2 * per), i, k % (2 * per)))
    b_spec = pl.BlockSpec((None, None, tn, tk), lambda i, j, k: (k // (2 * per), (k // per) % 2, j, k % per))
    o_spec = pl.BlockSpec((tm, tn), lambda i, j, k: (i, j))
    return _matmul_call(name, NT, grid, a_spec, b_spec, o_spec, jax.ShapeDtypeStruct((s, dm), F32), (tm, tn))(dh, w_up)


def ffn_dw_up(name, xn, dh):
    _, s, fh = dh.shape
    dm = xn.shape[1]
    quarter = fh // 2
    tn = _tile(quarter, 1536)
    per = quarter // tn
    tm, tk = _tile(dm, 1024), _tile(s, 2048)
    grid = (dm // tm, 4 * per, s // tk)
    a_spec = pl.BlockSpec((tk, tm), lambda i, j, k: (k, i))
    b_spec = pl.BlockSpec((None, tk, tn), lambda i, j, k: (j // (2 * per), k, j % (2 * per)))
    o_spec = pl.BlockSpec((None, None, tm, tn), lambda i, j, k: (j // (2 * per), (j // per) % 2, i, j % per))
    return _matmul_call(name, TN, grid, a_spec, b_spec, o_spec,
                        jax.ShapeDtypeStruct((2, 2, dm, quarter), F32), (tm, tn))(xn, dh)


def loss_head(name, y, target):
    s, dm = y.shape
    ts = _rows(s, dm)

    def body(y_ref, t_ref, loss_ref, dy_ref, acc):
        i = pl.program_id(0)

        @pl.when(i == 0)
        def _():
            acc[...] = jnp.zeros_like(acc)

        err = y_ref[...] - t_ref[...]
        dy_ref[...] = err * (1.0 / dm)
        acc[...] += jnp.sum(err * err, axis=0, keepdims=True)

        @pl.when(i == s // ts - 1)
        def _():
            loss_ref[...] = jnp.broadcast_to(jnp.sum(acc[...], axis=-1, keepdims=True) * (0.5 / dm), (1, LANES))

    blk = pl.BlockSpec((ts, dm), lambda i: (i, 0))
    loss, dy = pl.pallas_call(
        body, name=name, grid=(s // ts,), in_specs=[blk, blk],
        out_specs=[pl.BlockSpec((1, LANES), lambda i: (0, 0)), blk],
        out_shape=[jax.ShapeDtypeStruct((1, LANES), F32), jax.ShapeDtypeStruct((s, dm), F32)],
        scratch_shapes=[pltpu.VMEM((1, dm), F32)], compiler_params=_params("arbitrary"))(y, target)
    return loss[0, 0], dy


def attn_fwd_c(name, qc, kvc, krc, scale):
    s = qc.shape[0]
    tq = ATTN_FWD_STEP_ROWS

    def body(qn_ref, qr_ref, kn_ref, kr_ref, v_ref, o_ref, lse_ref):
        for r0 in range(0, tq, ATTN_TQ):
            rows = slice(r0, r0 + ATTN_TQ)
            sc = (_dot(qn_ref[rows, :], kn_ref[...], NT) + _dot(qr_ref[rows, :], kr_ref[...], NT)) * scale
            m = jnp.max(sc, axis=-1, keepdims=True)
            e = jnp.exp(sc - m)
            l = jnp.sum(e, axis=-1, keepdims=True)
            o_ref[rows, :] = _dot(e.astype(BF16), v_ref[...]) / l
            lse_ref[rows, :] = jnp.broadcast_to(m + jnp.log(l), (ATTN_TQ, HEAD))

    qb = lambda off: pl.BlockSpec((tq, HEAD), lambda h, i: (i, off + h))
    kb = lambda f: pl.BlockSpec((s, HEAD), f)
    out_blk = pl.BlockSpec((tq, HEAD), lambda h, i: (i, h))
    return pl.pallas_call(
        body, name=name, grid=(C_HEADS, s // tq),
        in_specs=[qb(0), qb(C_HEADS), kb(lambda h, i: (0, 2 * h)), kb(lambda h, i: (0, 0)), kb(lambda h, i: (0, 2 * h + 1))],
        out_specs=[out_blk, out_blk], out_shape=[jax.ShapeDtypeStruct((s, C_W), F32)] * 2,
        compiler_params=_params("parallel", "parallel"))(qc, qc, kvc, krc, kvc)


def attn_bwd_c(name, qc, kvc, krc, lse, o, do, scale):
    s = qc.shape[0]
    tq = ATTN_STEP_ROWS

    def body(qn_ref, qr_ref, kn_ref, kr_ref, v_ref, lse_ref, o_ref, do_ref, dqn_ref, dqr_ref, dkv_ref, dkr_ref):
        h, i = pl.program_id(0), pl.program_id(1)

        @pl.when(i == 0)
        def _():
            dkv_ref[...] = jnp.zeros_like(dkv_ref)

        @pl.when((i == 0) & (h == 0))
        def _():
            dkr_ref[...] = jnp.zeros_like(dkr_ref)

        for r0 in range(0, tq, ATTN_TQ):
            rows = slice(r0, r0 + ATTN_TQ)
            sc = (_dot(qn_ref[rows, :], kn_ref[...], NT) + _dot(qr_ref[rows, :], kr_ref[...], NT)) * scale
            pr = jnp.exp(sc - lse_ref[rows, 0:1])
            dov = do_ref[rows, :]
            delta = jnp.sum(dov.astype(F32) * o_ref[rows, :], axis=-1, keepdims=True)
            dp = _dot(dov, v_ref[...], NT)
            ds = (pr * (dp - delta) * scale).astype(BF16)
            dkv_ref[:, HEAD:] += _dot(pr.astype(BF16), dov, TN)
            dqn_ref[rows, :] = _dot(ds, kn_ref[...])
            dqr_ref[rows, :] = _dot(ds, kr_ref[...])
            dkv_ref[:, :HEAD] += _dot(ds, qn_ref[rows, :], TN)
            dkr_ref[...] += _dot(ds, qr_ref[rows, :], TN)

    qb = lambda off: pl.BlockSpec((tq, HEAD), lambda h, i: (i, off + h))
    kb = lambda f: pl.BlockSpec((s, HEAD), f)
    hq = pl.BlockSpec((tq, HEAD), lambda h, i: (i, h))
    kn_map, v_map, kr_map = (lambda h, i: (0, 2 * h)), (lambda h, i: (0, 2 * h + 1)), (lambda h, i: (0, 0))
    return pl.pallas_call(
        body, name=name, grid=(C_HEADS, s // tq),
        in_specs=[qb(0), qb(C_HEADS), kb(kn_map), kb(kr_map), kb(v_map), hq, hq, hq],
        out_specs=[hq, hq, pl.BlockSpec((s, 2 * HEAD), lambda h, i: (0, h)), kb(kr_map)],
        out_shape=[jax.ShapeDtypeStruct((s, C_W), F32)] * 2
        + [jax.ShapeDtypeStruct((s, 2 * C_W), F32), jax.ShapeDtypeStruct((s, HEAD), F32)],
        compiler_params=_params("arbitrary", "arbitrary"))(qc, qc, kvc, krc, kvc, lse, o, do)


def _layer_fwd(x, p, tabs):
    sv = _mixers_ab_fwd(x, p, tabs)
    x1 = _mixer_c_out_fwd(x, sv, p, tabs)
    return _ffn_fwd(x1, p, sv), sv


def _mixers_ab_fwd(x, p, tabs):
    sv = {"x": x}
    hn = rmsnorm_fwd("attn_norm", x, p["attn_norm"], x.shape[1], BF16)
    proj = matmul("in_proj", hn, p["w_in_t"], "nt", F32, tm=256)
    qa, ka, av, qb, kb, cqn, ckvn = mixer_prep("mixer_prep", proj, p, tabs)
    sv.update(hn=hn, proj=proj, qa=qa, ka=ka, av=av, qb=qb, kb=kb, cqn=cqn, ckvn=ckvn)

    ya, lse_a = attn_fwd("a_attn", [(qa, 0, ka, A_HEADS // A_KV)], av, A_HEADS // A_KV, A_HEADS, HEAD ** -0.5)
    yb, lse_b = dilated_fwd("b_dilated", qb, kb, proj, O_BV // HEAD)
    sv.update(ya=ya, lse_a=lse_a, yb=yb, lse_b=lse_b)
    return sv


def _mixer_c_out_fwd(x, sv, p, tabs):
    _mixer_c_fwd(sv, p, tabs)
    return _out_fwd(x, sv, p)


def _mixer_c_fwd(sv, p, tabs):
    qc_raw = matmul("c_uq", sv["cqn"], p["w_uq_t"], "nt", F32)
    qc = rope("c_q_rope", qc_raw, tabs["c"][0], BF16, plain_heads=C_HEADS)
    kvc = matmul("c_ukv", sv["ckvn"], p["w_ukv"], "nn", BF16)
    ckr = jnp.pad(sv["proj"][:, O_CKR:], ((0, 0), (0, HEAD - C_ROPE)))
    krc = rope("c_k_rope", ckr, tabs["c"][0], BF16)
    c_scale = (HEAD + C_ROPE) ** -0.5
    yc, lse_c = attn_fwd_c("c_attn", qc, kvc, krc, c_scale)
    sv.update(qc=qc, kvc=kvc, krc=krc, yc=yc, lse_c=lse_c)


def _out_fwd(x, sv, p, after=0.0):
    y = out_norms("out_norms", sv["ya"], sv["yb"], sv["yc"], p["out_norm"] + after)
    x1 = matmul("out_proj", y, p["w_out"], "nn", F32, add=x)
    sv.update(y=y, x1=x1)
    return x1


def _ffn_fwd(x1, p, sv):
    _ffn_up_fwd(x1, p, sv)
    return _ffn_down_fwd(x1, p, sv)


def _ffn_up_fwd(x1, p, sv):
    xn = rmsnorm_fwd("ffn_norm", x1, p["ffn_norm"], x1.shape[1], BF16)
    h, act = ffn_up("ffn_up", xn, p["w_up"], p["cwb"])
    sv.update(xn=xn, h=h, act=act)


def _ffn_down_fwd(x1, p, sv):
    return matmul("ffn_down", sv["act"], p["w_down"], "nn", F32, add=x1)


def _layer_bwd(dx2, sv, p, tabs):
    dxn, g = _ffn_bwd(dx2.astype(BF16), sv, p)
    dx1, dy, g_out = _out_proj_bwd(dxn, dx2, sv, p)
    dhn, g_mix = _mixers_bwd(dy, sv, p, tabs)
    dx, _, g_norm = _attn_norm_bwd(dhn, dx1, sv, p)
    return dx, {**g, **g_out, **g_mix, **g_norm}


def _ffn_bwd(dx2b, sv, p, after=0.0):
    g = {}
    dact = matmul("ffn_down_dx", dx2b, p["w_down"], "nt", F32)
    g["w_down"] = matmul("ffn_down_dw", sv["act"], dx2b, "tn", F32)
    dh, g["cwb"] = ffn_gate_bwd("ffn_gate_bwd", sv["h"], dact, p["cwb"] + after)
    dxn = ffn_dx("ffn_up_dx", dh, p["w_up"])
    g["w_up"] = ffn_dw_up("ffn_up_dw", sv["xn"], dh)
    return dxn, g


def _out_proj_bwd(dxn, dx2, sv, p, after=0.0):
    g = {}
    dm = dx2.shape[1]
    dx1, dx1b, g["ffn_norm"] = rmsnorm_bwd("ffn_norm_bwd", dxn, sv["x1"], p["ffn_norm"] + after, dm, (F32, BF16), add=dx2)
    dy = matmul("out_proj_dx", dx1b, p["w_out"], "nt", F32)
    g["w_out"] = matmul("out_proj_dw", sv["y"], dx1b, "tn", F32)
    return dx1, dy, g


def _attn_norm_bwd(dhn, dx1, sv, p, after=0.0):
    dx, dxb, d_gain = rmsnorm_bwd("attn_norm_bwd", dhn, sv["x"], p["attn_norm"] + after, dhn.shape[1], (F32, BF16), add=dx1)
    return dx, dxb, {"attn_norm": d_gain}


def _mixers_bwd(dy, sv, p, tabs):
    part, g = _mixer_c_bwd(dy, sv, p, tabs)
    dhn, g_ab = _mixers_ab_bwd(part, sv, p, tabs)
    return dhn, {**g, **g_ab}


def _mixer_c_bwd(dy, sv, p, tabs, after=0.0):
    g = {}
    dya, dyb, dyc, g["out_norm"] = out_norms_bwd("out_norms_bwd", dy, sv["ya"], sv["yb"], sv["yc"], p["out_norm"] + after)

    c_scale = (HEAD + C_ROPE) ** -0.5
    dqn, dqr, dkv, dkr = attn_bwd_c("c_attn_bwd", sv["qc"], sv["kvc"], sv["krc"], sv["lse_c"], sv["yc"], dyc, c_scale)
    dqc = jnp.concatenate([dqn.astype(BF16), rope("c_q_rope_bwd", dqr, tabs["c"][1], BF16)], axis=1)
    dcqn = matmul("c_uq_dx", dqc, p["w_uq_t"], "nn", F32)
    g["w_uq"] = matmul("c_uq_dw", dqc, sv["cqn"], "tn", F32)
    dckvn = matmul("c_ukv_dx", dkv, p["w_ukv"], "nt", F32)
    g["w_ukv"] = matmul("c_ukv_dw", sv["ckvn"], dkv, "tn", F32)
    dckr = rope("c_k_rope_bwd", dkr, tabs["c"][1], BF16)
    return (dya, dyb, dcqn, dckvn, dckr), g


def _mixers_ab_bwd(part, sv, p, tabs, after=0.0):
    g = {}
    dya, dyb, dcqn, dckvn, dckr = part

    dqb, dkb, dvb = dilated_bwd("b_dilated_bwd", sv["qb"], sv["kb"], sv["proj"], O_BV // HEAD, dyb + after, sv["yb"],
                                sv["lse_b"])

    kg = A_HEADS // A_KV
    (dqa,), (dka,), dva = attn_bwd("a_attn_bwd", [(sv["qa"], 0, sv["ka"], kg)], sv["av"], kg, sv["lse_a"], sv["ya"],
                                   dya, A_HEADS, HEAD ** -0.5)

    dproj, g["a_q_norm"], g["a_k_norm"], g["c_q_norm"], g["c_kv_norm"] = mixer_prep_bwd(
        "mixer_prep_bwd", sv["proj"], p, tabs, dqa, dka, dva, dqb, dkb, dvb, dcqn, dckvn, dckr)
    dhn = matmul("in_proj_dx", dproj, p["w_in_t"], "nn", F32, tm=512)
    g["w_in"] = matmul("in_proj_dw", dproj, sv["hn"], "tn", F32, tn=512, tk=512)
    return dhn, g


def _local_step(x, target, layers, final_norm, tabs):
    saved = []
    for p in layers:
        x, sv = _layer_fwd(x, p, tabs)
        saved.append(sv)
    dm = x.shape[1]
    yf = rmsnorm_fwd("final_norm", x, final_norm, dm, F32)
    loss, dyf = loss_head("loss_head", yf, target)
    dx, d_final = rmsnorm_bwd("final_norm_bwd", dyf, x, final_norm, dm, (F32,))
    grads = [None] * len(layers)
    for l in reversed(range(len(layers))):
        dx, grads[l] = _layer_bwd(dx, saved[l], layers[l], tabs)
    return loss, dx, grads, d_final


HBM = pl.BlockSpec(memory_space=pl.ANY)


def _place():
    x, y, c = lax.axis_index("x"), lax.axis_index("y"), lax.axis_index("c")
    others = [(1 - x, y), (x, 1 - y), (1 - x, 1 - y)]
    return x, y, c, 2 * x + y, others


def _remote(src, dst, send_sem, recv_sem, device):
    return pltpu.make_async_remote_copy(src_ref=src, dst_ref=dst, send_sem=send_sem, recv_sem=recv_sem,
                                        device_id=device, device_id_type=MESH)


def gather_small(name, small):
    def body(in_ref, out_ref, send_sems, recv_sems, local_sem):
        x, y, c, chip, others = _place()
        mine = pltpu.make_async_copy(in_ref, out_ref.at[chip], local_sem)
        mine.start()
        copies = []
        for k, (ox, oy) in enumerate(others):
            cp = _remote(in_ref, out_ref.at[chip], send_sems.at[k], recv_sems.at[k], (ox, oy, c))
            cp.start()
            copies.append(cp)
        for k, (ox, oy) in enumerate(others):
            landed = out_ref.at[2 * ox + oy]
            _remote(landed, landed, send_sems.at[k], recv_sems.at[k], (ox, oy, c)).wait_recv()
        for cp in copies:
            cp.wait_send()
        mine.wait()

    return pl.pallas_call(
        body, name=name, in_specs=[HBM], out_specs=HBM, out_shape=jax.ShapeDtypeStruct((4,) + small.shape, small.dtype),
        scratch_shapes=[pltpu.SemaphoreType.DMA((3,)), pltpu.SemaphoreType.DMA((3,)), pltpu.SemaphoreType.DMA],
        compiler_params=pltpu.CompilerParams(has_side_effects=True))(small)


IN_HBM = pl.BlockSpec(memory_space=pltpu.HBM)
SEMS = pl.BlockSpec(memory_space=pltpu.SEMAPHORE)
DATAFLOW = pltpu.SideEffectType.DATAFLOW_SIDE_EFFECTING


BF16_ROWS_PER_TILE = 16


def _halves_by_rows(shape):
    return (shape[-2] // 2) % BF16_ROWS_PER_TILE == 0


def _half(shape, core):
    if _halves_by_rows(shape):
        size = shape[-2] // 2
        return (pl.ds(core * size, size), slice(None))
    size = shape[-1] // 2
    return (slice(None), pl.ds(core * size, size))


def _half_shape(shape):
    r, c = shape[-2:]
    return (r // 2, c) if _halves_by_rows(shape) else (r, c // 2)


def _gather_plan(bufs):
    x, y, c, chip, others = _place()
    plan = []
    for t, ref in enumerate(bufs):
        mine = _half(ref.shape, c)
        for k, (ox, oy) in enumerate(others):
            plan.append((3 * t + k, (ox, oy, c), ref.at[(chip,) + mine], ref.at[(chip,) + mine],
                         ref.at[(2 * ox + oy,) + mine]))
    return plan


def _exchange_plan(bufs):
    x, y, c, chip, others = _place()
    n = len(bufs) // 2
    plan = []
    for t in range(n):
        for k, (ox, oy) in enumerate(others):
            plan.append((3 * t + k, (ox, oy, c), bufs[t].at[2 * ox + oy], bufs[n + t].at[chip], bufs[n + t].at[2 * ox + oy]))
    return plan


def _sibling_plan(bufs):
    x, y, c, chip, others = _place()
    n = len(bufs) // 2
    plan = []
    for t in range(n):
        plan.append((t, (x, y, 1 - c), bufs[t].at[(slice(None),) + _half(bufs[t].shape, 1 - c)], bufs[n + t], bufs[n + t]))
    return plan


def _forward_plan(bufs):
    x, y, c, chip, others = _place()
    plan = []
    for t, ref in enumerate(bufs):
        mine, theirs = _half(ref.shape, c), _half(ref.shape, 1 - c)
        for k, (ox, oy) in enumerate(others):
            slot = 2 * ox + oy
            plan.append((3 * t + k, (x, y, 1 - c), ref.at[(slot,) + mine], ref.at[(slot,) + mine], ref.at[(slot,) + theirs]))
    return plan


def _join_plan(bufs):
    x, y, c, chip, others = _place()
    return [(t, (x, y, 1 - c), ref.at[_half(ref.shape, c)], ref.at[_half(ref.shape, c)], ref.at[_half(ref.shape, 1 - c)])
            for t, ref in enumerate(bufs)]


PLAN_COPIES = {_gather_plan: lambda n: 3 * n, _exchange_plan: lambda n: 3 * (n // 2), _sibling_plan: lambda n: n // 2,
               _forward_plan: lambda n: 3 * n, _join_plan: lambda n: n}


def split_start(name, bufs, plan_of, after):
    n = len(bufs)

    def body(*refs):
        ins = refs[:n]
        send_sems, recv_sems = refs[n + 1], refs[n + 2]
        token = refs[-1]
        for idx, peer, src, dst, _ in plan_of(ins):
            _remote(src, dst, send_sems.at[idx], recv_sems.at[idx], peer).start()
        token[...] = jnp.zeros_like(token)

    copies = PLAN_COPIES[plan_of](n)
    res = pl.pallas_call(
        body, name=name, in_specs=[IN_HBM] * n + [HBM],
        out_specs=(SEMS, SEMS) + (IN_HBM,) * n + (pl.BlockSpec(memory_space=pltpu.VMEM),),
        out_shape=(pltpu.SemaphoreType.DMA((copies,)), pltpu.SemaphoreType.DMA((copies,)))
        + tuple(pltpu.HBM(b.shape, b.dtype) for b in bufs) + (jax.ShapeDtypeStruct((8, LANES), F32),),
        input_output_aliases={t: 2 + t for t in range(n)},
        compiler_params=pltpu.CompilerParams(has_side_effects=DATAFLOW))(
            *[pltpu.with_memory_space_constraint(b, pltpu.HBM) for b in bufs], after)
    return res[0], res[1], list(res[2:2 + n]), res[-1]


def split_wait(name, send_sems, recv_sems, flying, plan_of, after):
    n = len(flying)

    def body(*refs):
        ins = refs[:n]
        send_ref, recv_ref = refs[n], refs[n + 1]
        for idx, peer, src, dst, landing in plan_of(ins):
            _remote(src, dst, send_ref.at[idx], recv_ref.at[idx], peer).wait_send()
            _remote(landing, landing, send_ref.at[idx], recv_ref.at[idx], peer).wait_recv()

    return list(pl.pallas_call(
        body, name=name, in_specs=[IN_HBM] * n + [SEMS, SEMS] + [HBM] * len(after), out_specs=(IN_HBM,) * n,
        out_shape=tuple(pltpu.HBM(b.shape, b.dtype) for b in flying),
        input_output_aliases={t: t for t in range(n)},
        compiler_params=pltpu.CompilerParams(has_side_effects=DATAFLOW))(*flying, send_sems, recv_sems, *after))


def _half_block(shape, tr):
    r, c = shape[-2:]
    if _halves_by_rows(shape):
        per = r // 2 // tr
        return tr, c, lambda i, core: (core * per + i, 0)
    return tr, c // 2, lambda i, core: (i, core)


def add_half(name, place, grad, theirs):
    hr, hc = _half_shape(grad.shape)
    tr = _rows(hr, hc)
    br, bc, at = _half_block(grad.shape, tr)

    def body(place_ref, g_ref, t_ref, o_ref):
        o_ref[...] = (g_ref[...] + t_ref[...]).astype(o_ref.dtype)

    whole = pl.BlockSpec((None, br, bc), lambda j, i, pr: (j, i, 0))
    return pl.pallas_call(
        body, name=name,
        grid_spec=pltpu.PrefetchScalarGridSpec(
            num_scalar_prefetch=1, grid=(4, hr // tr),
            in_specs=[pl.BlockSpec((None, br, bc), lambda j, i, pr: (j,) + at(i, pr[0])), whole],
            out_specs=whole),
        out_shape=jax.ShapeDtypeStruct((4, hr, hc), BF16),
        compiler_params=_params("parallel", "parallel"))(place, grad, theirs)


def sum_chips(name, place, own, parts, shard_shape):
    _, hr, hc = parts.shape
    tr = _rows(hr, hc)
    br, bc, at = _half_block(shard_shape, tr)

    def body(place_ref, own_ref, p1, p2, p3, o_ref):
        o_ref[...] = own_ref[...].astype(F32) + p1[...].astype(F32) + p2[...].astype(F32) + p3[...].astype(F32)

    def slot(k):
        return pl.BlockSpec((None, br, bc), lambda i, pr: (lax.rem(pr[1] + k, 4), i, 0))

    return pl.pallas_call(
        body, name=name,
        grid_spec=pltpu.PrefetchScalarGridSpec(
            num_scalar_prefetch=1, grid=(hr // tr,), in_specs=[slot(0), slot(1), slot(2), slot(3)],
            out_specs=pl.BlockSpec((br, bc), lambda i, pr: at(i, pr[0]))),
        out_shape=jax.ShapeDtypeStruct(tuple(shard_shape), F32),
        compiler_params=_params("parallel"))(place, own, parts, parts, parts)


def allreduce_small(name, buf):
    rows = buf.shape[0]

    def body(in_ref, out_ref, slots, send_sems, recv_sems):
        x, y, c, _, _ = _place()
        me = 4 * x + 2 * y + c
        slots[me] = in_ref[...]
        peers = []
        for k in range(1, 8):
            px = 1 - x if k & 4 else x
            py = 1 - y if k & 2 else y
            pc = 1 - c if k & 1 else c
            peers.append((px, py, pc))
        copies = []
        for k, peer in enumerate(peers):
            cp = _remote(in_ref, slots.at[me], send_sems.at[k], recv_sems.at[k], peer)
            cp.start()
            copies.append(cp)
        for k, (px, py, pc) in enumerate(peers):
            slot = slots.at[4 * px + 2 * py + pc]
            _remote(slot, slot, send_sems.at[k], recv_sems.at[k], (px, py, pc)).wait_recv()
        for cp in copies:
            cp.wait_send()
        acc = slots[0]
        for d in range(1, 8):
            acc = acc + slots[d]
        out_ref[...] = acc

    vm = pl.BlockSpec(memory_space=pltpu.VMEM)
    return pl.pallas_call(
        body, name=name, in_specs=[vm], out_specs=vm, out_shape=jax.ShapeDtypeStruct(buf.shape, F32),
        scratch_shapes=[pltpu.VMEM((8, rows, LANES), F32), pltpu.SemaphoreType.DMA((7,)), pltpu.SemaphoreType.DMA((7,))],
        compiler_params=pltpu.CompilerParams(has_side_effects=True, vmem_limit_bytes=VMEM_LIMIT))(buf)


def cast_layer(name, place, w, layer):
    _, r, cols = w.shape
    tr = _rows(r, cols)

    def body(place_ref, w_ref, o_ref):
        o_ref[...] = w_ref[...].astype(BF16)

    return pl.pallas_call(
        body, name=name,
        grid_spec=pltpu.PrefetchScalarGridSpec(
            num_scalar_prefetch=1, grid=(r // tr,),
            in_specs=[pl.BlockSpec((None, tr, cols), lambda i, pr: (layer, i, 0))],
            out_specs=pl.BlockSpec((None, tr, cols), lambda i, pr: (pr[1], i, 0))),
        out_shape=jax.ShapeDtypeStruct((4, r, cols), BF16),
        compiler_params=_params("parallel"))(place, w)


def _adamw_math(g, w, m, v):
    m = ADAM_B1 * m + (1.0 - ADAM_B1) * g
    v = ADAM_B2 * v + (1.0 - ADAM_B2) * (g * g)
    m_hat = m / (1.0 - ADAM_B1 ** ADAM_STEP)
    v_hat = v / (1.0 - ADAM_B2 ** ADAM_STEP)
    delta = -ADAM_LR * (m_hat / (jnp.sqrt(v_hat) + ADAM_EPS) + ADAM_WD * w)
    return delta, m, v


def adamw_layer(name, layer, g, w, m, v, prev=None):
    nl, r, cols = w.shape
    tr = _rows(r, cols, 8)

    def body(*refs):
        g_ref, w_ref, m_ref, v_ref = refs[:4]
        og, od, om, ov = refs[-4:]
        gv = g_ref[...]
        delta, m2, v2 = _adamw_math(gv, w_ref[...], m_ref[...], v_ref[...])
        og[...] = gv
        od[...] = delta
        om[...] = m2
        ov[...] = v2

    lay = pl.BlockSpec((None, tr, cols), lambda i: (layer, i, 0))
    ins = [g, w, m, v] + (list(prev) if prev is not None else [])
    return pl.pallas_call(
        body, name=name, grid=(r // tr,),
        in_specs=[pl.BlockSpec((tr, cols), lambda i: (i, 0)), lay, lay, lay] + ([HBM] * 4 if prev is not None else []),
        out_specs=[lay] * 4, out_shape=[jax.ShapeDtypeStruct((nl, r, cols), F32)] * 4,
        input_output_aliases={4 + k: k for k in range(4)} if prev is not None else {},
        compiler_params=_params("parallel"))(*ins)


def adamw_packed(name, g, w, m, v):
    def body(g_ref, w_ref, m_ref, v_ref, od, om, ov):
        delta, m2, v2 = _adamw_math(g_ref[...], w_ref[...], m_ref[...], v_ref[...])
        od[...] = delta
        om[...] = m2
        ov[...] = v2

    vm = pl.BlockSpec(memory_space=pltpu.VMEM)
    return pl.pallas_call(
        body, name=name, in_specs=[vm] * 4, out_specs=[vm] * 3, out_shape=[jax.ShapeDtypeStruct(g.shape, F32)] * 3,
        compiler_params=pltpu.CompilerParams(vmem_limit_bytes=VMEM_LIMIT))(g, w, m, v)


def _pack(arrays):
    flat = jnp.concatenate([a.reshape(-1) for a in arrays])
    pad = (-flat.shape[0]) % (8 * LANES)
    return jnp.pad(flat, (0, pad)).reshape(-1, LANES)


def _unpack(buf, shapes):
    flat = buf.reshape(-1)
    out, off = [], 0
    for shp in shapes:
        size = int(np.prod(shp))
        out.append(flat[off:off + size].reshape(shp))
        off += size
    return out


MIXER_W = ("w_in", "w_uq", "w_ukv", "w_out")
FFN_W = ("w_up", "w_down")
BIG = MIXER_W + FFN_W
TRANSPOSED = ("w_in", "w_uq")
COLUMN_CUT = ("w_ukv", "w_up")
SMALL = ("attn_norm", "a_q_norm", "a_k_norm", "c_q_norm", "c_kv_norm", "out_norm", "ffn_norm", "conv_b", "final_norm")
WEIGHTS = ("attn_norm", "w_in", "a_q_norm", "a_k_norm", "c_q_norm", "c_kv_norm", "w_uq", "w_ukv", "out_norm", "w_out",
           "ffn_norm", "w_up", "conv_w", "conv_b", "w_down", "final_norm")
INPUTS = ("x",) + WEIGHTS + ("loss_target",) + tuple("m_" + n for n in WEIGHTS) + tuple("v_" + n for n in WEIGHTS)


def _columns(g):
    return jnp.transpose(g, (1, 0, 2)).reshape(g.shape[1], 4 * g.shape[2])


def _uncolumns(w):
    r, c4 = w.shape
    return jnp.transpose(w.reshape(r, 4, c4 // 4), (1, 0, 2))


def _compute_layouts(full, conv_w, small, layer):
    p = {n: small[n][layer] for n in SMALL if n != "final_norm"}
    if "w_in" in full:
        p["w_in_t"] = full["w_in"].reshape(-1, full["w_in"].shape[2])
    if "w_uq" in full:
        uq_t = full["w_uq"].reshape(C_HEADS, HEAD + C_ROPE, C_RANK)
        p["w_uq_t"] = jnp.concatenate([uq_t[:, :HEAD].reshape(C_W, C_RANK),
                                       jnp.pad(uq_t[:, HEAD:], ((0, 0), (0, HEAD - C_ROPE), (0, 0))).reshape(C_W, C_RANK)], axis=0)
        p["w_ukv"] = _columns(full["w_ukv"])
        p["w_out"] = full["w_out"].reshape(-1, full["w_out"].shape[2])
    if "w_down" in full:
        p["w_down"] = full["w_down"].reshape(-1, full["w_down"].shape[2])
    if "w_up" in full:
        up = full["w_up"]
        p["w_up"] = up.reshape(2, 2, up.shape[1], up.shape[2])
        fh = conv_w.shape[1] // 2
        taps = jnp.transpose(conv_w.reshape(3, 2, fh), (1, 0, 2))
        p["cwb"] = jnp.concatenate([taps, small["conv_b"][layer].reshape(2, 1, fh), jnp.zeros((2, 4, fh), F32)], axis=1)
    return p


def _shard_layout(name, g):
    if name == "w_in":
        return g.reshape(4, -1, g.shape[1])
    if name == "w_uq":
        uq_t = jnp.concatenate([g[:C_W].reshape(C_HEADS, HEAD, C_RANK),
                                g[C_W:].reshape(C_HEADS, HEAD, C_RANK)[:, :C_ROPE]], axis=1)
        return uq_t.reshape(4, -1, C_RANK)
    if name == "w_ukv":
        return _uncolumns(g)
    if name == "w_up":
        return g.reshape(4, g.shape[2], g.shape[3])
    return g.reshape(4, -1, g.shape[1])


def _conv_grads(dcwb):
    fh = dcwb.shape[2]
    return jnp.transpose(dcwb[:, 0:3, :], (1, 0, 2)).reshape(3, 2 * fh), dcwb[:, 3, :].reshape(2 * fh)


def _shard_layouts(g):
    return ({n: _shard_layout(n, g[n]) for n in BIG},) + _conv_grads(g["cwb"])


def kernel(x, attn_norm, w_in, a_q_norm, a_k_norm, c_q_norm, c_kv_norm, w_uq, w_ukv, out_norm, w_out, ffn_norm, w_up, conv_w, conv_b, w_down, final_norm, loss_target, m_attn_norm, m_w_in, m_a_q_norm, m_a_k_norm, m_c_q_norm, m_c_kv_norm, m_w_uq, m_w_ukv, m_out_norm, m_w_out, m_ffn_norm, m_w_up, m_conv_w, m_conv_b, m_w_down, m_final_norm, v_attn_norm, v_w_in, v_a_q_norm, v_a_k_norm, v_c_q_norm, v_c_kv_norm, v_w_uq, v_w_ukv, v_out_norm, v_w_out, v_ffn_norm, v_w_up, v_conv_w, v_conv_b, v_w_down, v_final_norm):
    a = dict(zip(INPUTS, (x, attn_norm, w_in, a_q_norm, a_k_norm, c_q_norm, c_kv_norm, w_uq, w_ukv, out_norm, w_out, ffn_norm, w_up, conv_w, conv_b, w_down, final_norm, loss_target, m_attn_norm, m_w_in, m_a_q_norm, m_a_k_norm, m_c_q_norm, m_c_kv_norm, m_w_uq, m_w_ukv, m_out_norm, m_w_out, m_ffn_norm, m_w_up, m_conv_w, m_conv_b, m_w_down, m_final_norm, v_attn_norm, v_w_in, v_a_q_norm, v_a_k_norm, v_c_q_norm, v_c_kv_norm, v_w_uq, v_w_ukv, v_out_norm, v_w_out, v_ffn_norm, v_w_up, v_conv_w, v_conv_b, v_w_down, v_final_norm)))
    nl = w_in.shape[0]
    seq = x.shape[1]
    for n in TRANSPOSED:
        for kind in ("", "m_", "v_"):
            a[kind + n] = jnp.swapaxes(a[kind + n], 1, 2)
    chip = 2 * lax.axis_index("x") + lax.axis_index("y")
    place = jnp.stack([lax.axis_index("c"), chip]).astype(jnp.int32)
    tabs = _rope_tables(seq)

    assert nl == 2
    quarter = conv_w.shape[2]
    taps = jnp.pad(conv_w, ((0, 0), (0, 8 - conv_w.shape[1]), (0, 0))).reshape(nl * 8, quarter)
    taps_full = gather_small("gather_taps", taps).reshape(4, nl, 8, quarter)
    conv_full = [jnp.transpose(taps_full[:, l, 0:3, :], (1, 0, 2)).reshape(3, 4 * quarter) for l in range(nl)]
    flights = {}
    here, after = place, taps_full
    for l in range(nl):
        for key, names in ((f"in{l}", ("w_in",)), (f"mix{l}", MIXER_W[1:]), (f"up{l}", ("w_up",)), (f"down{l}", ("w_down",))):
            slots = [cast_layer(f"cast{l}_{n}", here, a[n], l) for n in names]
            send_sems, recv_sems, flying, token = split_start(f"gather_{key}_start", slots, _gather_plan, after)
            flights[key] = (names, send_sems, recv_sems, flying)
            here = after = place + token[0, 0].astype(jnp.int32)
    started = token[0, 0]

    def land(key, after):
        names, send_sems, recv_sems, flying = flights[key]
        landed = split_wait(f"gather_{key}_wait", send_sems, recv_sems, flying, _gather_plan, after)
        send_sems, recv_sems, passing, token = split_start(f"gather_{key}_pass_start", landed, _forward_plan, place)
        return (key, names, send_sems, recv_sems, passing), token

    def landed(passing, after, layer):
        key, names, send_sems, recv_sems, flying = passing
        full = split_wait(f"gather_{key}_pass_wait", send_sems, recv_sems, flying, _forward_plan, after)
        return _compute_layouts(dict(zip(names, full)), conv_full[layer], a, layer)

    h = x[0]
    layers, saved = [], []
    early = None
    for l in range(nl):
        if early is None:
            pass_in, token = land(f"in{l}", [taps_full])
            p = landed(pass_in, [token], l)
            p["attn_norm"] = p["attn_norm"] + started
        else:
            p = landed(early[0], [h], l)
        sv = _mixers_ab_fwd(h, p, tabs)
        if early is None:
            pass_mix, token = land(f"mix{l}", [sv["yb"]])
            p.update(landed(pass_mix, [token], l))
        else:
            p.update(landed(early[1], [sv["yb"]], l))
        _mixer_c_fwd(sv, p, tabs)
        pass_up, token = land(f"up{l}", [sv["yc"]])
        if l > 0:
            pass_down, token = land(f"down{l}", [token])
        x1 = _out_fwd(h, sv, p, token[0, 0])
        p.update(landed(pass_up, [x1], l))
        _ffn_up_fwd(x1, p, sv)
        if l == 0:
            pass_down, token = land(f"down{l}", [sv["act"]])
        else:
            token = sv["act"]
        if l + 1 < nl:
            pass_in, _ = land(f"in{l + 1}", [token])
            pass_mix, _ = land(f"mix{l + 1}", [token])
            early = (pass_in, pass_mix)
        p.update(landed(pass_down, [token], l))
        h = _ffn_down_fwd(x1, p, sv)
        layers.append(p)
        saved.append(sv)
    dm = h.shape[1]
    yf = rmsnorm_fwd("final_norm", h, final_norm, dm, F32)
    loss_here, dyf = loss_head("loss_head", yf, loss_target[0])
    dx, dxb, d_final = rmsnorm_bwd("final_norm_bwd", dyf, h, final_norm, dm, (F32, BF16))

    def swap_begin(tag, names, g):
        g_list = [_shard_layout(n, g[n]) for n in names]
        theirs = [lax.empty((4,) + _half_shape(s.shape), s.dtype) for s in g_list]
        send_sems, recv_sems, flying, token = split_start(f"{tag}_sibling_start", g_list + theirs, _sibling_plan, place)
        return (names, send_sems, recv_sems, flying), token[0, 0]

    def exchange_begin(tag, swap, after):
        names, send_sems, recv_sems, flying = swap
        landed = split_wait(f"{tag}_sibling_wait", send_sems, recv_sems, flying, _sibling_plan, after)
        g_list, theirs = landed[:len(names)], landed[len(names):]
        sums = [add_half(f"{tag}_add_{n}", place, gr, t) for n, gr, t in zip(names, g_list, theirs)]
        parts = [lax.empty(s.shape, s.dtype) for s in sums]
        send_sems, recv_sems, flying, token = split_start(f"{tag}_chips_start", sums + parts, _exchange_plan, place)
        return (names, [gr.shape[1:] for gr in g_list], send_sems, recv_sems, flying), token

    def exchange_end(tag, flight, after):
        names, shard_shapes, send_sems, recv_sems, flying = flight
        landed = split_wait(f"{tag}_chips_wait", send_sems, recv_sems, flying, _exchange_plan, after)
        sums, parts = landed[:len(names)], landed[len(names):]
        halves = [sum_chips(f"{tag}_sum_{n}", place, s, p, shp) for n, s, p, shp in zip(names, sums, parts, shard_shapes)]
        send_sems, recv_sems, joining, token = split_start(f"{tag}_join_start", halves, _join_plan, place)
        return (names, send_sems, recv_sems, joining), token

    def joined(tag, joining, after):
        names, send_sems, recv_sems, flying = joining
        return dict(zip(names, split_wait(f"{tag}_join_wait", send_sems, recv_sems, flying, _join_plan, after)))

    grads, conv_grads, flights_f, flights_m = [None] * nl, [None] * nl, [None] * nl, [None] * nl
    after = 0.0
    for l in reversed(range(nl)):
        sv, p = saved[l], layers[l]
        dxn, g = _ffn_bwd(dxb, sv, p, after)
        swap_f, after = swap_begin(f"reduce{l}f", FFN_W, g)
        dx1, dy, g_out = _out_proj_bwd(dxn, dx, sv, p, after)
        part, g_c = _mixer_c_bwd(dy, sv, p, tabs)
        flights_f[l], token = exchange_begin(f"reduce{l}f", swap_f, [part[2]])
        dhn, g_ab = _mixers_ab_bwd(part, sv, p, tabs, token[0, 0])
        g.update(g_out)
        g.update(g_c)
        g.update(g_ab)
        swap_m, after = swap_begin(f"reduce{l}m", MIXER_W, g)
        dx, dxb, g_norm = _attn_norm_bwd(dhn, dx1, sv, p, after)
        g.update(g_norm)
        grads[l], conv_grads[l] = g, _conv_grads(g["cwb"])
        before = [dx]
        if l == 0:
            small_g = {n: jnp.stack([grads[k][n] for k in range(nl)]) for n in SMALL if n not in ("conv_b", "final_norm")}
            small_g["conv_b"] = jnp.stack([cg[1] for cg in conv_grads])
            small_g["final_norm"] = d_final
            conv_w_g = jnp.stack([cg[0] for cg in conv_grads])
            small_sum = allreduce_small("reduce_small", _pack([small_g[n] for n in SMALL] + [conv_w_g, loss_here.reshape(1)]))
            before.append(small_sum)
        flights_m[l], token = exchange_begin(f"reduce{l}m", swap_m, before)
        after = token[0, 0]
    grad_x = dx

    out = {}
    join_f, token = exchange_end("reduce1f", flights_f[1], [token])
    join_m, token = exchange_end("reduce1m", flights_m[1], [token])
    reduced1 = joined("reduce1f", join_f, [token])
    for n in FFN_W:
        out[n] = adamw_layer(f"adamw1_{n}", 1, reduced1[n], a[n], a["m_" + n], a["v_" + n])
    reduced1 = joined("reduce1m", join_m, [out[n][3] for n in FFN_W])
    for n in MIXER_W:
        out[n] = adamw_layer(f"adamw1_{n}", 1, reduced1[n], a[n], a["m_" + n], a["v_" + n])

    summed = _unpack(small_sum, [a[n].shape for n in SMALL] + [conv_w_g.shape, (1,)])
    small_g = dict(zip(SMALL, summed[:-2]))
    small_g["conv_w"] = lax.dynamic_slice_in_dim(summed[-2], chip * quarter, quarter, axis=2)
    loss = summed[-1][0]
    names = SMALL + ("conv_w",)
    shapes = [a[n].shape for n in names]
    d_small, m_small, v_small = adamw_packed(
        "adamw_small", _pack([small_g[n] for n in names]), _pack([a[n] for n in names]),
        _pack([a["m_" + n] for n in names]), _pack([a["v_" + n] for n in names]))

    join_f, token = exchange_end("reduce0f", flights_f[0], [out[n][3] for n in BIG] + [d_small])
    join_m, token = exchange_end("reduce0m", flights_m[0], [token])
    reduced0 = joined("reduce0f", join_f, [token])
    for n in FFN_W:
        out[n] = adamw_layer(f"adamw0_{n}", 0, reduced0[n], a[n], a["m_" + n], a["v_" + n], prev=out[n])
    reduced0 = joined("reduce0m", join_m, [out[n][3] for n in FFN_W])
    for n in MIXER_W:
        out[n] = adamw_layer(f"adamw0_{n}", 0, reduced0[n], a[n], a["m_" + n], a["v_" + n], prev=out[n])
    for n, d_, m_, v_ in zip(names, _unpack(d_small, shapes), _unpack(m_small, shapes), _unpack(v_small, shapes)):
        out[n] = (small_g[n], d_, m_, v_)
    for n in TRANSPOSED:
        out[n] = [jnp.swapaxes(o, 1, 2) for o in out[n]]

    return (loss, grad_x[None]) + tuple(out[n][k] for k in range(4) for n in WEIGHTS)
```

```python
import functools

import jax
import jax.numpy as jnp
import numpy as np
from jax import lax
from jax.experimental import pallas as pl
from jax.experimental.pallas import tpu as pltpu

F32 = jnp.float32
BF16 = jnp.bfloat16
MESH = pl.DeviceIdType.MESH

V7X_VMEM_BYTES = 64 * 1024 * 1024
VMEM_LIMIT = V7X_VMEM_BYTES - 6 * 1024 * 1024
LANES = 128
ELEMENTWISE_BLOCK_BYTES = 4 * 1024 * 1024

HEAD = 128
A_HEADS, A_KV = 6, 2
B_HEADS, B_GROUPS = 4, 3
B_DILATIONS = (1, 4, 16)
B_HALF = 64
C_HEADS, C_RANK, C_ROPE = 6, 512, 64
GRID_W = 64
A_THETA, B_THETA, C_THETA = 10000.0, 500000.0, 10000.0
B_ROPE_DIM = 32
EPS = 1e-6
A_W, B_W, C_W = A_HEADS * HEAD, B_HEADS * HEAD, C_HEADS * HEAD
O_AQ, O_AK, O_AV = 0, 768, 1024
O_BQ, O_BK, O_BV = 1280, 2816, 3328
O_CQ, O_CKV, O_CKR = 3840, 4352, 4864
IN_W = 4928
PROJ_W = 5120

ADAM_LR, ADAM_B1, ADAM_B2, ADAM_EPS, ADAM_WD, ADAM_STEP = 0.001, 0.9, 0.999, 1e-08, 0.01, 10

NN = (((1,), (0,)), ((), ()))
NT = (((1,), (1,)), ((), ()))
TN = (((0,), (0,)), ((), ()))


def _dot(a, b, dims=NN):
    return lax.dot_general(a.astype(BF16), b.astype(BF16), dims, preferred_element_type=F32)


def _params(*sem):
    return pltpu.CompilerParams(dimension_semantics=sem if sem else None, vmem_limit_bytes=VMEM_LIMIT)


def _tile(n, target, unit=LANES):
    if n <= target:
        return n
    best = 0
    for t in range(unit, target + 1, unit):
        if n % t == 0:
            best = t
    return best if best else n


def _rows(r, width, itemsize=4):
    return _tile(r, max(8, ELEMENTWISE_BLOCK_BYTES // (width * itemsize)), 8)


def _matmul_call(name, dims, grid, a_spec, b_spec, o_spec, out_shape, acc_shape, add_spec=None):
    nk = grid[2]

    def body(*refs):
        a_ref, b_ref = refs[:2]
        add_ref = None if add_spec is None else refs[2]
        o_ref = refs[2 if add_spec is None else 3]

        def finish(r):
            if add_ref is not None:
                r = r + add_ref[...]
            o_ref[...] = r.astype(o_ref.dtype)

        if nk == 1:
            finish(_dot(a_ref[...], b_ref[...], dims))
            return
        acc = refs[-1]
        k = pl.program_id(2)

        @pl.when(k == 0)
        def _():
            acc[...] = _dot(a_ref[...], b_ref[...], dims)

        if nk > 2:
            @pl.when((k > 0) & (k < nk - 1))
            def _():
                acc[...] += _dot(a_ref[...], b_ref[...], dims)

        @pl.when(k == nk - 1)
        def _():
            finish(acc[...] + _dot(a_ref[...], b_ref[...], dims))

    in_specs = [a_spec, b_spec] + ([] if add_spec is None else [add_spec])
    return pl.pallas_call(
        body, name=name, grid=grid, in_specs=in_specs, out_specs=o_spec, out_shape=out_shape,
        scratch_shapes=[pltpu.VMEM(acc_shape, F32)] if nk > 1 else [],
        compiler_params=_params("parallel", "parallel", "arbitrary"))


def matmul(name, a, b, mode, out_dtype, add=None, tm=1024, tn=1024, tk=2816):
    if mode == "nn":
        (m, k), (k2, n) = a.shape, b.shape
    elif mode == "nt":
        (m, k), (n, k2) = a.shape, b.shape
    else:
        (k, m), (k2, n) = a.shape, b.shape
    assert k == k2, (name, a.shape, b.shape)
    tm, tn, tk = _tile(m, tm), _tile(n, tn), _tile(k, tk)
    grid = (m // tm, n // tn, k // tk)
    if mode == "tn":
        a_spec = pl.BlockSpec((tk, tm), lambda i, j, kk: (kk, i))
    else:
        a_spec = pl.BlockSpec((tm, tk), lambda i, j, kk: (i, kk))
    whole_b = {"pipeline_mode": pl.Buffered(1)} if grid[1] == 1 and grid[2] == 1 and grid[0] > 1 else {}
    if mode == "nt":
        b_spec = pl.BlockSpec((tn, tk), lambda i, j, kk: (j, kk), **whole_b)
    else:
        b_spec = pl.BlockSpec((tk, tn), lambda i, j, kk: (kk, j), **whole_b)
    o_spec = pl.BlockSpec((tm, tn), lambda i, j, kk: (i, j))
    dims = {"nn": NN, "nt": NT, "tn": TN}[mode]
    call = _matmul_call(name, dims, grid, a_spec, b_spec, o_spec, jax.ShapeDtypeStruct((m, n), out_dtype),
                        (tm, tn), None if add is None else o_spec)
    return call(a, b) if add is None else call(a, b, add)


def rmsnorm_fwd(name, x, g, width, out_dtype):
    r, cols = x.shape
    nb = cols // width
    tr = _rows(r, width)

    def body(x_ref, g_ref, o_ref):
        xv = x_ref[...]
        rs = lax.rsqrt(jnp.mean(xv * xv, axis=-1, keepdims=True) + EPS)
        o_ref[...] = (xv * rs * g_ref[...]).astype(o_ref.dtype)

    blk = pl.BlockSpec((tr, width), lambda i, j: (i, j))
    return pl.pallas_call(
        body, name=name, grid=(r // tr, nb),
        in_specs=[blk, pl.BlockSpec((1, width), lambda i, j: (0, 0))], out_specs=blk,
        out_shape=jax.ShapeDtypeStruct((r, cols), out_dtype),
        compiler_params=_params("parallel", "parallel"))(x, g.reshape(1, width))


def rmsnorm_bwd(name, dy, x, g, width, out_dtypes=(F32,), add=None):
    r, cols = x.shape
    nb = cols // width
    tr = _rows(r, width)
    n_out = len(out_dtypes)

    def body(*refs):
        dy_ref, x_ref, g_ref = refs[:3]
        add_ref = refs[3] if add is not None else None
        outs = refs[-(n_out + 1):-1]
        dg_ref = refs[-1]
        xv = x_ref[...]
        dyv = dy_ref[...].astype(F32)
        rs = lax.rsqrt(jnp.mean(xv * xv, axis=-1, keepdims=True) + EPS)
        xh = xv * rs
        dyg = dyv * g_ref[...]
        dx = rs * (dyg - xh * jnp.mean(dyg * xh, axis=-1, keepdims=True))
        if add_ref is not None:
            dx = dx + add_ref[...]
        for o in outs:
            o[...] = dx.astype(o.dtype)

        @pl.when((pl.program_id(0) == 0) & (pl.program_id(1) == 0))
        def _():
            dg_ref[...] = jnp.zeros_like(dg_ref)

        dg_ref[...] += jnp.sum(dyv * xh, axis=0, keepdims=True)

    blk = pl.BlockSpec((tr, width), lambda i, j: (i, j))
    vec = pl.BlockSpec((1, width), lambda i, j: (0, 0))
    ins = [dy, x, g.reshape(1, width)] + ([add] if add is not None else [])
    res = pl.pallas_call(
        body, name=name, grid=(r // tr, nb),
        in_specs=[blk, blk, vec] + ([blk] if add is not None else []),
        out_specs=[blk] * n_out + [vec],
        out_shape=[jax.ShapeDtypeStruct((r, cols), dt) for dt in out_dtypes] + [jax.ShapeDtypeStruct((1, width), F32)],
        compiler_params=_params("arbitrary", "arbitrary"))(*ins)
    return tuple(res[:n_out]) + (res[n_out].reshape(width),)


def _rope_angles(pos, dim, theta):
    inv = theta ** (-jnp.arange(0, dim, 2, dtype=F32) / dim)
    ang = pos.astype(F32)[:, None] * inv[None, :]
    return jnp.cos(ang), jnp.sin(ang)


def _rope_tables(s):
    rows = s // GRID_W
    row_pos = jnp.repeat(jnp.arange(rows), GRID_W)
    col_pos = jnp.tile(jnp.arange(GRID_W), rows)
    t_pos = jnp.arange(s)
    z = lambda n: jnp.zeros((s, n), F32)
    o = lambda n: jnp.ones((s, n), F32)
    cr, sr = _rope_angles(row_pos, HEAD // 2, A_THETA)
    cc, sc = _rope_angles(col_pos, HEAD // 2, A_THETA)
    tab_a = (jnp.concatenate([cr, cr, cc, cc], 1), jnp.concatenate([-sr, z(32), -sc, z(32)], 1),
             jnp.concatenate([z(32), sr, z(32), sc], 1), 32)
    cp, sp = _rope_angles(t_pos, B_ROPE_DIM, B_THETA)
    tab_b = (jnp.concatenate([cp, cp, o(96)], 1), jnp.concatenate([-sp, z(112)], 1),
             jnp.concatenate([z(16), sp, z(96)], 1), 16)
    cm, sm = _rope_angles(t_pos, C_ROPE, C_THETA)
    tab_c = (jnp.concatenate([cm, cm, o(64)], 1), jnp.concatenate([-sm, z(96)], 1),
             jnp.concatenate([z(32), sm, z(64)], 1), 32)

    def transposed(tab):
        c, s1, s2, h = tab
        return (c, jnp.roll(s2, -h, axis=1), jnp.roll(s1, h, axis=1), h)

    return {k: (t, transposed(t)) for k, t in (("a", tab_a), ("b", tab_b), ("c", tab_c))}


def rope(name, x, tab, out_dtype, plain_heads=0):
    c, s1, s2, h = tab
    s, cols = x.shape
    ts = _rows(s, cols)

    def body(x_ref, c_ref, s1_ref, s2_ref, o_ref):
        cv, s1v, s2v = c_ref[...], s1_ref[...], s2_ref[...]
        for j in range(cols // HEAD):
            lanes = slice(HEAD * j, HEAD * (j + 1))
            xv = x_ref[:, lanes].astype(F32)
            if j >= plain_heads:
                xv = xv * cv + pltpu.roll(xv, HEAD - h, 1) * s1v + pltpu.roll(xv, h, 1) * s2v
            o_ref[:, lanes] = xv.astype(o_ref.dtype)

    blk = pl.BlockSpec((ts, cols), lambda i: (i, 0))
    tb = pl.BlockSpec((ts, HEAD), lambda i: (i, 0))
    return pl.pallas_call(
        body, name=name, grid=(s // ts,), in_specs=[blk, tb, tb, tb], out_specs=blk,
        out_shape=jax.ShapeDtypeStruct((s, cols), out_dtype),
        compiler_params=_params("parallel"))(x, c, s1, s2)


PREP_ROWS = 256


def _rope_of(x, tab, h):
    c, s1, s2 = tab
    return x * c + pltpu.roll(x, HEAD - h, 1) * s1 + pltpu.roll(x, h, 1) * s2


def _heads(start, n):
    return [(j, slice(start + HEAD * j, start + HEAD * (j + 1))) for j in range(n)]


def mixer_prep(name, proj, p, tabs):
    s = proj.shape[0]
    tr = PREP_ROWS
    (ca, s1a, s2a, ha), (cb, s1b, s2b, hb) = tabs["a"][0], tabs["b"][0]

    def body(proj_ref, gaq, gak, gcq, gckv, ca_r, s1a_r, s2a_r, cb_r, s1b_r, s2b_r,
             qa_ref, ka_ref, av_ref, qb_ref, kb_ref, cqn_ref, ckvn_ref):
        ta = (ca_r[...], s1a_r[...], s2a_r[...])
        tb = (cb_r[...], s1b_r[...], s2b_r[...])

        def normed(x, g):
            return x * lax.rsqrt(jnp.mean(x * x, axis=-1, keepdims=True) + EPS) * g

        for (src0, n, gain, out) in ((O_AQ, A_HEADS, gaq, qa_ref), (O_AK, A_KV, gak, ka_ref)):
            for j, cols in _heads(src0, n):
                out[:, HEAD * j:HEAD * (j + 1)] = _rope_of(normed(proj_ref[:, cols], gain[...]), ta, ha).astype(BF16)
        av_ref[...] = proj_ref[:, O_AV:O_BQ].astype(BF16)
        for (src0, n, out) in ((O_BQ, B_GROUPS * B_HEADS, qb_ref), (O_BK, B_HEADS, kb_ref)):
            for j, cols in _heads(src0, n):
                out[:, HEAD * j:HEAD * (j + 1)] = _rope_of(proj_ref[:, cols], tb, hb)
        cqn_ref[...] = normed(proj_ref[:, O_CQ:O_CKV], gcq[...]).astype(BF16)
        ckvn_ref[...] = normed(proj_ref[:, O_CKV:O_CKR], gckv[...]).astype(BF16)

    rows = lambda w: pl.BlockSpec((tr, w), lambda i: (i, 0))
    vec = lambda w: pl.BlockSpec((1, w), lambda i: (0, 0))
    widths = (A_W, A_KV * HEAD, A_KV * HEAD, B_GROUPS * B_W, B_W, C_RANK, C_RANK)
    dtypes = (BF16, BF16, BF16, F32, F32, BF16, BF16)
    return pl.pallas_call(
        body, name=name, grid=(s // tr,),
        in_specs=[rows(proj.shape[1]), vec(HEAD), vec(HEAD), vec(C_RANK), vec(C_RANK)] + [rows(HEAD)] * 6,
        out_specs=[rows(w) for w in widths],
        out_shape=[jax.ShapeDtypeStruct((s, w), dt) for w, dt in zip(widths, dtypes)],
        compiler_params=_params("parallel"))(
            proj, p["a_q_norm"].reshape(1, HEAD), p["a_k_norm"].reshape(1, HEAD), p["c_q_norm"].reshape(1, C_RANK),
            p["c_kv_norm"].reshape(1, C_RANK), ca, s1a, s2a, cb, s1b, s2b)


def mixer_prep_bwd(name, proj, p, tabs, dqa, dka, dva, dqb, dkb, dvb, dcqn, dckvn, dckr):
    s = proj.shape[0]
    tr = PREP_ROWS
    (ca, s1a, s2a, ha), (cb, s1b, s2b, hb) = tabs["a"][1], tabs["b"][1]

    def body(proj_ref, dqa_ref, dka_ref, dva_ref, dqb_ref, dkb_ref, dvb_ref, dcqn_ref, dckvn_ref, dckr_ref,
             gaq, gak, gcq, gckv, ca_r, s1a_r, s2a_r, cb_r, s1b_r, s2b_r, dproj_ref, dgaq, dgak, dgcq, dgckv):
        ta = (ca_r[...], s1a_r[...], s2a_r[...])
        tb = (cb_r[...], s1b_r[...], s2b_r[...])

        @pl.when(pl.program_id(0) == 0)
        def _():
            for ref in (dgaq, dgak, dgcq, dgckv):
                ref[...] = jnp.zeros_like(ref)

        def norm_bwd(dy, x, gain, dgain):
            rs = lax.rsqrt(jnp.mean(x * x, axis=-1, keepdims=True) + EPS)
            xh = x * rs
            dyg = dy * gain[...]
            dgain[...] += jnp.sum(dy * xh, axis=0, keepdims=True)
            return rs * (dyg - xh * jnp.mean(dyg * xh, axis=-1, keepdims=True))

        for (dst0, n, d_ref, gain, dgain) in ((O_AQ, A_HEADS, dqa_ref, gaq, dgaq), (O_AK, A_KV, dka_ref, gak, dgak)):
            for j, cols in _heads(dst0, n):
                dy = _rope_of(d_ref[:, HEAD * j:HEAD * (j + 1)], ta, ha)
                dproj_ref[:, cols] = norm_bwd(dy, proj_ref[:, cols], gain, dgain).astype(BF16)
        dproj_ref[:, O_AV:O_BQ] = dva_ref[...].astype(BF16)
        for (dst0, n, d_ref) in ((O_BQ, B_GROUPS * B_HEADS, dqb_ref), (O_BK, B_HEADS, dkb_ref)):
            for j, cols in _heads(dst0, n):
                dproj_ref[:, cols] = _rope_of(d_ref[:, HEAD * j:HEAD * (j + 1)], tb, hb).astype(BF16)
        dproj_ref[:, O_BV:O_CQ] = dvb_ref[...].astype(BF16)
        dproj_ref[:, O_CQ:O_CKV] = norm_bwd(dcqn_ref[...], proj_ref[:, O_CQ:O_CKV], gcq, dgcq).astype(BF16)
        dproj_ref[:, O_CKV:O_CKR] = norm_bwd(dckvn_ref[...], proj_ref[:, O_CKV:O_CKR], gckv, dgckv).astype(BF16)
        dproj_ref[:, O_CKR:] = dckr_ref[:, :C_ROPE]

    rows = lambda w: pl.BlockSpec((tr, w), lambda i: (i, 0))
    vec = lambda w: pl.BlockSpec((1, w), lambda i: (0, 0))
    grads = (dqa, dka, dva, dqb, dkb, dvb, dcqn, dckvn, dckr)
    res = pl.pallas_call(
        body, name=name, grid=(s // tr,),
        in_specs=[rows(proj.shape[1])] + [rows(g.shape[1]) for g in grads]
        + [vec(HEAD), vec(HEAD), vec(C_RANK), vec(C_RANK)] + [rows(HEAD)] * 6,
        out_specs=[rows(proj.shape[1]), vec(HEAD), vec(HEAD), vec(C_RANK), vec(C_RANK)],
        out_shape=[jax.ShapeDtypeStruct(proj.shape, BF16)] + [jax.ShapeDtypeStruct((1, w), F32) for w in (HEAD, HEAD, C_RANK, C_RANK)],
        compiler_params=_params("arbitrary"))(
            proj, *grads, p["a_q_norm"].reshape(1, HEAD), p["a_k_norm"].reshape(1, HEAD), p["c_q_norm"].reshape(1, C_RANK),
            p["c_kv_norm"].reshape(1, C_RANK), ca, s1a, s2a, cb, s1b, s2b)
    return res[0], res[1].reshape(HEAD), res[2].reshape(HEAD), res[3].reshape(C_RANK), res[4].reshape(C_RANK)


OUT_GROUPS = ((0, A_W), (A_W, A_W + B_W), (A_W + B_W, A_W + B_W + C_W))


def out_norms(name, ya, yb, yc, gain):
    s = ya.shape[0]
    tr = PREP_ROWS
    width = OUT_GROUPS[-1][1]

    def body(ya_ref, yb_ref, yc_ref, g_ref, y_ref):
        for (lo, hi), src in zip(OUT_GROUPS, (ya_ref, yb_ref, yc_ref)):
            x = src[...]
            rs = lax.rsqrt(jnp.mean(x * x, axis=-1, keepdims=True) + EPS)
            y_ref[:, lo:hi] = (x * rs * g_ref[:, lo:hi]).astype(BF16)

    rows = lambda w: pl.BlockSpec((tr, w), lambda i: (i, 0))
    return pl.pallas_call(
        body, name=name, grid=(s // tr,),
        in_specs=[rows(A_W), rows(B_W), rows(C_W), pl.BlockSpec((1, width), lambda i: (0, 0))], out_specs=rows(width),
        out_shape=jax.ShapeDtypeStruct((s, width), BF16), compiler_params=_params("parallel"))(
            ya, yb, yc, gain.reshape(1, width))


def out_norms_bwd(name, dy, ya, yb, yc, gain):
    s = ya.shape[0]
    tr = PREP_ROWS
    width = OUT_GROUPS[-1][1]

    def body(dy_ref, ya_ref, yb_ref, yc_ref, g_ref, dya_ref, dyb_ref, dyc_ref, dg_ref):
        @pl.when(pl.program_id(0) == 0)
        def _():
            dg_ref[...] = jnp.zeros_like(dg_ref)

        for (lo, hi), src, dst in zip(OUT_GROUPS, (ya_ref, yb_ref, yc_ref), (dya_ref, dyb_ref, dyc_ref)):
            x = src[...]
            d = dy_ref[:, lo:hi]
            rs = lax.rsqrt(jnp.mean(x * x, axis=-1, keepdims=True) + EPS)
            xh = x * rs
            dg = d * g_ref[:, lo:hi]
            dst[...] = (rs * (dg - xh * jnp.mean(dg * xh, axis=-1, keepdims=True))).astype(dst.dtype)
            dg_ref[:, lo:hi] += jnp.sum(d * xh, axis=0, keepdims=True)

    rows = lambda w: pl.BlockSpec((tr, w), lambda i: (i, 0))
    vec = pl.BlockSpec((1, width), lambda i: (0, 0))
    dya, dyb, dyc, dg = pl.pallas_call(
        body, name=name, grid=(s // tr,),
        in_specs=[rows(width), rows(A_W), rows(B_W), rows(C_W), vec], out_specs=[rows(A_W), rows(B_W), rows(C_W), vec],
        out_shape=[jax.ShapeDtypeStruct((s, A_W), BF16), jax.ShapeDtypeStruct((s, B_W), F32),
                   jax.ShapeDtypeStruct((s, C_W), BF16), jax.ShapeDtypeStruct((1, width), F32)],
        compiler_params=_params("arbitrary"))(dy, ya, yb, yc, gain.reshape(1, width))
    return dya, dyb, dyc, dg.reshape(width)


ATTN_TQ = 256
ATTN_STEP_ROWS = 512


def attn_fwd(name, parts, v, v_group, nheads, scale):
    s = v.shape[0]
    tq = ATTN_STEP_ROWS
    npart = len(parts)

    def body(*refs):
        v_ref, o_ref, lse_ref = refs[2 * npart:]
        for r0 in range(0, tq, ATTN_TQ):
            rows = slice(r0, r0 + ATTN_TQ)
            sc = None
            for p in range(npart):
                t = _dot(refs[2 * p][rows, :], refs[2 * p + 1][...], NT)
                sc = t if sc is None else sc + t
            sc = sc * scale
            m = jnp.max(sc, axis=-1, keepdims=True)
            e = jnp.exp(sc - m)
            l = jnp.sum(e, axis=-1, keepdims=True)
            o_ref[rows, :] = _dot(e.astype(BF16), v_ref[...]) / l
            lse_ref[rows, :] = jnp.broadcast_to(m + jnp.log(l), (ATTN_TQ, HEAD))

    in_specs, ins = [], []
    for q, qoff, k, kg in parts:
        in_specs.append(pl.BlockSpec((tq, HEAD), lambda h, i, qoff=qoff: (i, qoff + h)))
        in_specs.append(pl.BlockSpec((s, HEAD), lambda h, i, kg=kg: (0, h // kg)))
        ins += [q, k]
    in_specs.append(pl.BlockSpec((s, HEAD), lambda h, i: (0, h // v_group)))
    out_blk = pl.BlockSpec((tq, HEAD), lambda h, i: (i, h))
    return pl.pallas_call(
        body, name=name, grid=(nheads, s // tq), in_specs=in_specs, out_specs=[out_blk, out_blk],
        out_shape=[jax.ShapeDtypeStruct((s, nheads * HEAD), F32)] * 2,
        compiler_params=_params("parallel", "parallel"))(*ins, v)


def attn_bwd(name, parts, v, v_group, lse, o, do, nheads, scale):
    s = v.shape[0]
    tq = ATTN_STEP_ROWS
    npart = len(parts)

    def body(*refs):
        v_ref, lse_ref, o_ref, do_ref = refs[2 * npart:2 * npart + 4]
        outs = refs[2 * npart + 4:]
        dq_refs, dk_refs, dv_ref = outs[:npart], outs[npart:2 * npart], outs[2 * npart]
        h, i = pl.program_id(0), pl.program_id(1)

        @pl.when((i == 0) & (h % v_group == 0))
        def _():
            dv_ref[...] = jnp.zeros_like(dv_ref)

        for p in range(npart):
            @pl.when((i == 0) & (h % parts[p][3] == 0))
            def _(p=p):
                dk_refs[p][...] = jnp.zeros_like(dk_refs[p])

        for r0 in range(0, tq, ATTN_TQ):
            rows = slice(r0, r0 + ATTN_TQ)
            sc = None
            for p in range(npart):
                t = _dot(refs[2 * p][rows, :], refs[2 * p + 1][...], NT)
                sc = t if sc is None else sc + t
            pr = jnp.exp(sc * scale - lse_ref[rows, 0:1])
            dov = do_ref[rows, :]
            delta = jnp.sum(dov.astype(F32) * o_ref[rows, :], axis=-1, keepdims=True)
            dp = _dot(dov, v_ref[...], NT)
            ds = (pr * (dp - delta) * scale).astype(BF16)
            dv_ref[...] += _dot(pr.astype(BF16), dov, TN)
            for p in range(npart):
                dq_refs[p][rows, :] = _dot(ds, refs[2 * p + 1][...])
                dk_refs[p][...] += _dot(ds, refs[2 * p][rows, :], TN)

    in_specs, ins = [], []
    for q, qoff, k, kg in parts:
        in_specs.append(pl.BlockSpec((tq, HEAD), lambda h, i, qoff=qoff: (i, qoff + h)))
        in_specs.append(pl.BlockSpec((s, HEAD), lambda h, i, kg=kg: (0, h // kg)))
        ins += [q, k]
    hq_blk = pl.BlockSpec((tq, HEAD), lambda h, i: (i, h))
    in_specs += [pl.BlockSpec((s, HEAD), lambda h, i: (0, h // v_group)), hq_blk, hq_blk, hq_blk]
    out_specs = [hq_blk] * npart
    out_shape = [jax.ShapeDtypeStruct((s, nheads * HEAD), F32)] * npart
    for q, qoff, k, kg in parts:
        out_specs.append(pl.BlockSpec((s, HEAD), lambda h, i, kg=kg: (0, h // kg)))
        out_shape.append(jax.ShapeDtypeStruct((s, nheads // kg * HEAD), F32))
    out_specs.append(pl.BlockSpec((s, HEAD), lambda h, i: (0, h // v_group)))
    out_shape.append(jax.ShapeDtypeStruct((s, nheads // v_group * HEAD), F32))
    res = pl.pallas_call(
        body, name=name, grid=(nheads, s // tq), in_specs=in_specs, out_specs=out_specs, out_shape=out_shape,
        compiler_params=_params("arbitrary", "arbitrary"))(*ins, v, lse, o, do)
    return list(res[:npart]), list(res[npart:2 * npart]), res[2 * npart]


def _band_windows(lf):
    for ib in range(lf // HEAD):
        q0 = ib * HEAD
        yield q0, max(0, q0 - B_HALF), min(lf, q0 + HEAD + B_HALF)


def _band_mask(q0, lo, hi):
    qpos = q0 + lax.broadcasted_iota(jnp.int32, (HEAD, hi - lo), 0)
    kpos = lo + lax.broadcasted_iota(jnp.int32, (HEAD, hi - lo), 1)
    return jnp.abs(qpos - kpos) <= B_HALF


def _class_rows(r, d, start, size):
    return pl.ds(r + d * start, size, stride=d) if d > 1 else pl.ds(start, size)


def dilated_fwd(name, q, k, v_src, v_block):
    s = k.shape[0]
    scale = HEAD ** -0.5

    def body(q_ref, k_ref, v_ref, y_ref, lse_ref, o_scr, l_scr):
        g_now = pl.program_id(1)
        for g, d in enumerate(B_DILATIONS):
            @pl.when(g_now == g)
            def _(g=g, d=d):
                for r in range(d):
                    for q0, lo, hi in _band_windows(s // d):
                        mine = _class_rows(r, d, q0, HEAD)
                        keys = _class_rows(r, d, lo, hi - lo)
                        sc = _dot(q_ref[mine, :].astype(BF16), k_ref[keys, :].astype(BF16), NT) * scale
                        sc = jnp.where(_band_mask(q0, lo, hi), sc, -1e30)
                        m = jnp.max(sc, axis=-1, keepdims=True)
                        e = jnp.exp(sc - m)
                        l = jnp.sum(e, axis=-1, keepdims=True)
                        o_scr.at[g][mine, :] = _dot((e / l).astype(BF16), v_ref[keys, :].astype(BF16))
                        l_scr.at[g][mine, :] = jnp.broadcast_to(m + jnp.log(l), (HEAD, HEAD))

        @pl.when(g_now == B_GROUPS - 1)
        def _():
            a, b, c = l_scr[0], l_scr[1], l_scr[2]
            m = jnp.maximum(jnp.maximum(a, b), c)
            ea, eb, ec = jnp.exp(a - m), jnp.exp(b - m), jnp.exp(c - m)
            den = ea + eb + ec
            y_ref[...] = (ea / den) * o_scr[0] + (eb / den) * o_scr[1] + (ec / den) * o_scr[2]
            lse_ref[...] = m + jnp.log(den)

    blk = lambda f: pl.BlockSpec((s, HEAD), f)
    per_head = blk(lambda h, g: (0, h))
    return pl.pallas_call(
        body, name=name, grid=(B_HEADS, B_GROUPS),
        in_specs=[blk(lambda h, g: (0, B_HEADS * g + h)), per_head, blk(lambda h, g: (0, v_block + h))],
        out_specs=[per_head, per_head], out_shape=[jax.ShapeDtypeStruct((s, B_W), F32)] * 2,
        scratch_shapes=[pltpu.VMEM((B_GROUPS, s, HEAD), F32)] * 2,
        compiler_params=_params("parallel", "arbitrary"))(q, k, v_src)


def dilated_bwd(name, q, k, v_src, v_block, dy, y, lse):
    s = k.shape[0]
    scale = HEAD ** -0.5

    def body(q_ref, k_ref, v_ref, dy_ref, y_ref, lse_ref, dq_ref, dk_ref, dv_ref, delta):
        g_now = pl.program_id(1)

        @pl.when(g_now == 0)
        def _():
            dk_ref[...] = jnp.zeros_like(dk_ref)
            dv_ref[...] = jnp.zeros_like(dv_ref)
            delta[...] = jnp.broadcast_to(jnp.sum(dy_ref[...] * y_ref[...], axis=-1, keepdims=True), (s, HEAD))

        for g, d in enumerate(B_DILATIONS):
            @pl.when(g_now == g)
            def _(d=d):
                for r in range(d):
                    for q0, lo, hi in _band_windows(s // d):
                        mine = _class_rows(r, d, q0, HEAD)
                        keys = _class_rows(r, d, lo, hi - lo)
                        qv, kv = q_ref[mine, :].astype(BF16), k_ref[keys, :].astype(BF16)
                        dyv = dy_ref[mine, :].astype(BF16)
                        sc = _dot(qv, kv, NT) * scale
                        pr = jnp.where(_band_mask(q0, lo, hi), jnp.exp(sc - lse_ref[mine, :][:, 0:1]), 0.0)
                        dp = _dot(dyv, v_ref[keys, :].astype(BF16), NT)
                        ds = (pr * (dp - delta[mine, :][:, 0:1]) * scale).astype(BF16)
                        dv_ref[keys, :] = dv_ref[keys, :] + _dot(pr.astype(BF16), dyv, TN)
                        dq_ref[mine, :] = _dot(ds, kv)
                        dk_ref[keys, :] = dk_ref[keys, :] + _dot(ds, qv, TN)

    blk = lambda f: pl.BlockSpec((s, HEAD), f)
    per_head = blk(lambda h, g: (0, h))
    by_group = blk(lambda h, g: (0, B_HEADS * g + h))
    return pl.pallas_call(
        body, name=name, grid=(B_HEADS, B_GROUPS),
        in_specs=[by_group, per_head, blk(lambda h, g: (0, v_block + h)), per_head, per_head, per_head],
        out_specs=[by_group, per_head, per_head],
        out_shape=[jax.ShapeDtypeStruct((s, B_GROUPS * B_W), F32)] + [jax.ShapeDtypeStruct((s, B_W), F32)] * 2,
        scratch_shapes=[pltpu.VMEM((s, HEAD), F32)],
        compiler_params=_params("parallel", "arbitrary"))(q, k, v_src, dy, y, lse)


FFN_TN = 256


def _edges(shape):
    row = lax.broadcasted_iota(jnp.int32, shape, 0)
    return row == 0, row == shape[0] - 1


def _shifted(h, edges):
    first, last = edges
    s = h.shape[0]
    return jnp.where(first, 0.0, pltpu.roll(h, 1, 0)), jnp.where(last, 0.0, pltpu.roll(h, s - 1, 0))


def _conv(h, cw, edges):
    prev, nxt = _shifted(h, edges)
    return prev * cw[0:1, :] + h * cw[1:2, :] + nxt * cw[2:3, :] + cw[3:4, :], prev, nxt


def _sigmoid(x):
    return 1.0 / (1.0 + jnp.exp(-x))


def ffn_up(name, xn, w_up, cwb):
    s, dm = xn.shape
    quarter = w_up.shape[3]
    fh = 2 * quarter
    tn = FFN_TN
    per = quarter // tn

    def body(x_ref, w_ref, cw_ref, h_ref, act_ref):
        xv = x_ref[...]
        hg = _dot(xv, w_ref[0])
        hu = _dot(xv, w_ref[1])
        h_ref[0] = hg
        h_ref[1] = hu
        edges = _edges(hg.shape)
        gc, _, _ = _conv(hg, cw_ref[0], edges)
        uc, _, _ = _conv(hu, cw_ref[1], edges)
        act_ref[...] = (gc * _sigmoid(gc) * uc).astype(BF16)

    return pl.pallas_call(
        body, name=name, grid=(fh // tn,),
        in_specs=[pl.BlockSpec((s, dm), lambda t: (0, 0), pipeline_mode=pl.Buffered(1)),
                  pl.BlockSpec((2, None, dm, tn), lambda t: (0, t // per, 0, t % per)),
                  pl.BlockSpec((2, 8, tn), lambda t: (0, 0, t))],
        out_specs=[pl.BlockSpec((2, s, tn), lambda t: (0, 0, t)), pl.BlockSpec((s, tn), lambda t: (0, t))],
        out_shape=[jax.ShapeDtypeStruct((2, s, fh), F32), jax.ShapeDtypeStruct((s, fh), BF16)],
        compiler_params=_params("parallel"))(xn, w_up, cwb)


def ffn_gate_bwd(name, h, dact, cwb):
    _, s, fh = h.shape
    tn = FFN_TN

    def body(h_ref, da_ref, cw_ref, dh_ref, dcw_ref):
        edges = _edges((s, tn))
        gc, gp, gn = _conv(h_ref[0], cw_ref[0], edges)
        uc, up, un = _conv(h_ref[1], cw_ref[1], edges)
        sg = _sigmoid(gc)
        da = da_ref[...]
        dgc = da * uc * (sg * (1.0 + gc * (1.0 - sg)))
        duc = da * (gc * sg)
        for idx, (hv, prev, nxt, dc) in enumerate(((h_ref[0], gp, gn, dgc), (h_ref[1], up, un, duc))):
            cw = cw_ref[idx]
            from_prev, from_next = _shifted(dc, edges)
            dh_ref[idx] = (from_next * cw[0:1, :] + dc * cw[1:2, :] + from_prev * cw[2:3, :]).astype(BF16)
            dcw_ref[idx, 0:1, :] = jnp.sum(prev * dc, axis=0, keepdims=True)
            dcw_ref[idx, 1:2, :] = jnp.sum(hv * dc, axis=0, keepdims=True)
            dcw_ref[idx, 2:3, :] = jnp.sum(nxt * dc, axis=0, keepdims=True)
            dcw_ref[idx, 3:4, :] = jnp.sum(dc, axis=0, keepdims=True)
            dcw_ref[idx, 4:8, :] = jnp.zeros((4, tn), F32)

    return pl.pallas_call(
        body, name=name, grid=(fh // tn,),
        in_specs=[pl.BlockSpec((2, s, tn), lambda t: (0, 0, t)), pl.BlockSpec((s, tn), lambda t: (0, t)),
                  pl.BlockSpec((2, 8, tn), lambda t: (0, 0, t))],
        out_specs=[pl.BlockSpec((2, s, tn), lambda t: (0, 0, t)), pl.BlockSpec((2, 8, tn), lambda t: (0, 0, t))],
        out_shape=[jax.ShapeDtypeStruct((2, s, fh), BF16), jax.ShapeDtypeStruct((2, 8, fh), F32)],
        compiler_params=_params("parallel"))(h, dact, cwb)


def ffn_dx(name, dh, w_up):
    _, s, fh = dh.shape
    dm, quarter = w_up.shape[2], w_up.shape[3]
    tk = _tile(quarter, 2816)
    per = quarter // tk
    tm, tn = _tile(s, 1024), _tile(dm, 1024)
    grid = (s // tm, dm // tn, 4 * per)
    a_spec = pl.BlockSpec((None, tm, tk), lambda i, j, k: (k // (2 * per), i, k % (2 * per)))
    b_spec = pl.BlockSpec((None, None, tn, tk), lambda i, j, k: (k // (2 * per), (k // per) % 2, j, k % per))
    o_spec = pl.BlockSpec((tm, tn), lambda i, j, k: (i, j))
    return _matmul_call(name, NT, grid, a_spec, b_spec, o_spec, jax.ShapeDtypeStruct((s, dm), F32), (tm, tn))(dh, w_up)


def ffn_dw_up(name, xn, dh):
    _, s, fh = dh.shape
    dm = xn.shape[1]
    quarter = fh // 2
    tn = _tile(quarter, 1536)
    per = quarter // tn
    tm, tk = _tile(dm, 1024), _tile(s, 2048)
    grid = (dm // tm, 4 * per, s // tk)
    a_spec = pl.BlockSpec((tk, tm), lambda i, j, k: (k, i))
    b_spec = pl.BlockSpec((None, tk, tn), lambda i, j, k: (j // (2 * per), k, j % (2 * per)))
    o_spec = pl.BlockSpec((None, None, tm, tn), lambda i, j, k: (j // (2 * per), (j // per) % 2, i, j % per))
    return _matmul_call(name, TN, grid, a_spec, b_spec, o_spec,
                        jax.ShapeDtypeStruct((2, 2, dm, quarter), F32), (tm, tn))(xn, dh)


def loss_head(name, y, target):
    s, dm = y.shape
    ts = _rows(s, dm)

    def body(y_ref, t_ref, loss_ref, dy_ref, acc):
        i = pl.program_id(0)

        @pl.when(i == 0)
        def _():
            acc[...] = jnp.zeros_like(acc)

        err = y_ref[...] - t_ref[...]
        dy_ref[...] = err * (1.0 / dm)
        acc[...] += jnp.sum(err * err, axis=0, keepdims=True)

        @pl.when(i == s // ts - 1)
        def _():
            loss_ref[...] = jnp.broadcast_to(jnp.sum(acc[...], axis=-1, keepdims=True) * (0.5 / dm), (1, LANES))

    blk = pl.BlockSpec((ts, dm), lambda i: (i, 0))
    loss, dy = pl.pallas_call(
        body, name=name, grid=(s // ts,), in_specs=[blk, blk],
        out_specs=[pl.BlockSpec((1, LANES), lambda i: (0, 0)), blk],
        out_shape=[jax.ShapeDtypeStruct((1, LANES), F32), jax.ShapeDtypeStruct((s, dm), F32)],
        scratch_shapes=[pltpu.VMEM((1, dm), F32)], compiler_params=_params("arbitrary"))(y, target)
    return loss[0, 0], dy


def attn_fwd_c(name, qc, kvc, krc, scale):
    s = qc.shape[0]
    tq = ATTN_STEP_ROWS

    def body(qn_ref, qr_ref, kn_ref, kr_ref, v_ref, o_ref, lse_ref):
        for r0 in range(0, tq, ATTN_TQ):
            rows = slice(r0, r0 + ATTN_TQ)
            sc = (_dot(qn_ref[rows, :], kn_ref[...], NT) + _dot(qr_ref[rows, :], kr_ref[...], NT)) * scale
            m = jnp.max(sc, axis=-1, keepdims=True)
            e = jnp.exp(sc - m)
            l = jnp.sum(e, axis=-1, keepdims=True)
            o_ref[rows, :] = _dot(e.astype(BF16), v_ref[...]) / l
            lse_ref[rows, :] = jnp.broadcast_to(m + jnp.log(l), (ATTN_TQ, HEAD))

    qb = lambda off: pl.BlockSpec((tq, HEAD), lambda h, i: (i, off + h))
    kb = lambda f: pl.BlockSpec((s, HEAD), f)
    out_blk = pl.BlockSpec((tq, HEAD), lambda h, i: (i, h))
    return pl.pallas_call(
        body, name=name, grid=(C_HEADS, s // tq),
        in_specs=[qb(0), qb(C_HEADS), kb(lambda h, i: (0, 2 * h)), kb(lambda h, i: (0, 0)), kb(lambda h, i: (0, 2 * h + 1))],
        out_specs=[out_blk, out_blk], out_shape=[jax.ShapeDtypeStruct((s, C_W), F32)] * 2,
        compiler_params=_params("parallel", "parallel"))(qc, qc, kvc, krc, kvc)


def attn_bwd_c(name, qc, kvc, krc, lse, o, do, scale):
    s = qc.shape[0]
    tq = ATTN_STEP_ROWS

    def body(qn_ref, qr_ref, kn_ref, kr_ref, v_ref, lse_ref, o_ref, do_ref, dqn_ref, dqr_ref, dkv_ref, dkr_ref):
        h, i = pl.program_id(0), pl.program_id(1)

        @pl.when(i == 0)
        def _():
            dkv_ref[...] = jnp.zeros_like(dkv_ref)

        @pl.when((i == 0) & (h == 0))
        def _():
            dkr_ref[...] = jnp.zeros_like(dkr_ref)

        for r0 in range(0, tq, ATTN_TQ):
            rows = slice(r0, r0 + ATTN_TQ)
            sc = (_dot(qn_ref[rows, :], kn_ref[...], NT) + _dot(qr_ref[rows, :], kr_ref[...], NT)) * scale
            pr = jnp.exp(sc - lse_ref[rows, 0:1])
            dov = do_ref[rows, :]
            delta = jnp.sum(dov.astype(F32) * o_ref[rows, :], axis=-1, keepdims=True)
            dp = _dot(dov, v_ref[...], NT)
            ds = (pr * (dp - delta) * scale).astype(BF16)
            dkv_ref[:, HEAD:] += _dot(pr.astype(BF16), dov, TN)
            dqn_ref[rows, :] = _dot(ds, kn_ref[...])
            dqr_ref[rows, :] = _dot(ds, kr_ref[...])
            dkv_ref[:, :HEAD] += _dot(ds, qn_ref[rows, :], TN)
            dkr_ref[...] += _dot(ds, qr_ref[rows, :], TN)

    qb = lambda off: pl.BlockSpec((tq, HEAD), lambda h, i: (i, off + h))
    kb = lambda f: pl.BlockSpec((s, HEAD), f)
    hq = pl.BlockSpec((tq, HEAD), lambda h, i: (i, h))
    kn_map, v_map, kr_map = (lambda h, i: (0, 2 * h)), (lambda h, i: (0, 2 * h + 1)), (lambda h, i: (0, 0))
    return pl.pallas_call(
        body, name=name, grid=(C_HEADS, s // tq),
        in_specs=[qb(0), qb(C_HEADS), kb(kn_map), kb(kr_map), kb(v_map), hq, hq, hq],
        out_specs=[hq, hq, pl.BlockSpec((s, 2 * HEAD), lambda h, i: (0, h)), kb(kr_map)],
        out_shape=[jax.ShapeDtypeStruct((s, C_W), F32)] * 2
        + [jax.ShapeDtypeStruct((s, 2 * C_W), F32), jax.ShapeDtypeStruct((s, HEAD), F32)],
        compiler_params=_params("arbitrary", "arbitrary"))(qc, qc, kvc, krc, kvc, lse, o, do)


def _layer_fwd(x, p, tabs):
    sv = _mixers_ab_fwd(x, p, tabs)
    x1 = _mixer_c_out_fwd(x, sv, p, tabs)
    return _ffn_fwd(x1, p, sv), sv


def _mixers_ab_fwd(x, p, tabs):
    sv = {"x": x}
    hn = rmsnorm_fwd("attn_norm", x, p["attn_norm"], x.shape[1], BF16)
    proj = matmul("in_proj", hn, p["w_in_t"], "nt", F32, tm=256)
    qa, ka, av, qb, kb, cqn, ckvn = mixer_prep("mixer_prep", proj, p, tabs)
    sv.update(hn=hn, proj=proj, qa=qa, ka=ka, av=av, qb=qb, kb=kb, cqn=cqn, ckvn=ckvn)

    ya, lse_a = attn_fwd("a_attn", [(qa, 0, ka, A_HEADS // A_KV)], av, A_HEADS // A_KV, A_HEADS, HEAD ** -0.5)
    yb, lse_b = dilated_fwd("b_dilated", qb, kb, proj, O_BV // HEAD)
    sv.update(ya=ya, lse_a=lse_a, yb=yb, lse_b=lse_b)
    return sv


def _mixer_c_out_fwd(x, sv, p, tabs):
    _mixer_c_fwd(sv, p, tabs)
    return _out_fwd(x, sv, p)


def _mixer_c_fwd(sv, p, tabs):
    qc_raw = matmul("c_uq", sv["cqn"], p["w_uq_t"], "nt", F32)
    qc = rope("c_q_rope", qc_raw, tabs["c"][0], BF16, plain_heads=C_HEADS)
    kvc = matmul("c_ukv", sv["ckvn"], p["w_ukv"], "nn", BF16)
    ckr = jnp.pad(sv["proj"][:, O_CKR:], ((0, 0), (0, HEAD - C_ROPE)))
    krc = rope("c_k_rope", ckr, tabs["c"][0], BF16)
    c_scale = (HEAD + C_ROPE) ** -0.5
    yc, lse_c = attn_fwd_c("c_attn", qc, kvc, krc, c_scale)
    sv.update(qc=qc, kvc=kvc, krc=krc, yc=yc, lse_c=lse_c)


def _out_fwd(x, sv, p, after=0.0):
    y = out_norms("out_norms", sv["ya"], sv["yb"], sv["yc"], p["out_norm"] + after)
    x1 = matmul("out_proj", y, p["w_out"], "nn", F32, add=x)
    sv.update(y=y, x1=x1)
    return x1


def _ffn_fwd(x1, p, sv):
    _ffn_up_fwd(x1, p, sv)
    return _ffn_down_fwd(x1, p, sv)


def _ffn_up_fwd(x1, p, sv):
    xn = rmsnorm_fwd("ffn_norm", x1, p["ffn_norm"], x1.shape[1], BF16)
    h, act = ffn_up("ffn_up", xn, p["w_up"], p["cwb"])
    sv.update(xn=xn, h=h, act=act)


def _ffn_down_fwd(x1, p, sv):
    return matmul("ffn_down", sv["act"], p["w_down"], "nn", F32, add=x1)


def _layer_bwd(dx2, sv, p, tabs):
    dxn, g = _ffn_bwd(dx2.astype(BF16), sv, p)
    dx1, dy, g_out = _out_proj_bwd(dxn, dx2, sv, p)
    dhn, g_mix = _mixers_bwd(dy, sv, p, tabs)
    dx, _, g_norm = _attn_norm_bwd(dhn, dx1, sv, p)
    return dx, {**g, **g_out, **g_mix, **g_norm}


def _ffn_bwd(dx2b, sv, p, after=0.0):
    g = {}
    dact = matmul("ffn_down_dx", dx2b, p["w_down"], "nt", F32)
    g["w_down"] = matmul("ffn_down_dw", sv["act"], dx2b, "tn", F32)
    dh, g["cwb"] = ffn_gate_bwd("ffn_gate_bwd", sv["h"], dact, p["cwb"] + after)
    dxn = ffn_dx("ffn_up_dx", dh, p["w_up"])
    g["w_up"] = ffn_dw_up("ffn_up_dw", sv["xn"], dh)
    return dxn, g


def _out_proj_bwd(dxn, dx2, sv, p, after=0.0):
    g = {}
    dm = dx2.shape[1]
    dx1, dx1b, g["ffn_norm"] = rmsnorm_bwd("ffn_norm_bwd", dxn, sv["x1"], p["ffn_norm"] + after, dm, (F32, BF16), add=dx2)
    dy = matmul("out_proj_dx", dx1b, p["w_out"], "nt", F32)
    g["w_out"] = matmul("out_proj_dw", sv["y"], dx1b, "tn", F32)
    return dx1, dy, g


def _attn_norm_bwd(dhn, dx1, sv, p, after=0.0):
    dx, dxb, d_gain = rmsnorm_bwd("attn_norm_bwd", dhn, sv["x"], p["attn_norm"] + after, dhn.shape[1], (F32, BF16), add=dx1)
    return dx, dxb, {"attn_norm": d_gain}


def _mixers_bwd(dy, sv, p, tabs):
    part, g = _mixer_c_bwd(dy, sv, p, tabs)
    dhn, g_ab = _mixers_ab_bwd(part, sv, p, tabs)
    return dhn, {**g, **g_ab}


def _mixer_c_bwd(dy, sv, p, tabs, after=0.0):
    g = {}
    dya, dyb, dyc, g["out_norm"] = out_norms_bwd("out_norms_bwd", dy, sv["ya"], sv["yb"], sv["yc"], p["out_norm"] + after)

    c_scale = (HEAD + C_ROPE) ** -0.5
    dqn, dqr, dkv, dkr = attn_bwd_c("c_attn_bwd", sv["qc"], sv["kvc"], sv["krc"], sv["lse_c"], sv["yc"], dyc, c_scale)
    dqc = jnp.concatenate([dqn.astype(BF16), rope("c_q_rope_bwd", dqr, tabs["c"][1], BF16)], axis=1)
    dcqn = matmul("c_uq_dx", dqc, p["w_uq_t"], "nn", F32)
    g["w_uq"] = matmul("c_uq_dw", dqc, sv["cqn"], "tn", F32)
    dckvn = matmul("c_ukv_dx", dkv, p["w_ukv"], "nt", F32)
    g["w_ukv"] = matmul("c_ukv_dw", sv["ckvn"], dkv, "tn", F32)
    dckr = rope("c_k_rope_bwd", dkr, tabs["c"][1], BF16)
    return (dya, dyb, dcqn, dckvn, dckr), g


def _mixers_ab_bwd(part, sv, p, tabs, after=0.0):
    g = {}
    dya, dyb, dcqn, dckvn, dckr = part

    dqb, dkb, dvb = dilated_bwd("b_dilated_bwd", sv["qb"], sv["kb"], sv["proj"], O_BV // HEAD, dyb + after, sv["yb"],
                                sv["lse_b"])

    kg = A_HEADS // A_KV
    (dqa,), (dka,), dva = attn_bwd("a_attn_bwd", [(sv["qa"], 0, sv["ka"], kg)], sv["av"], kg, sv["lse_a"], sv["ya"],
                                   dya, A_HEADS, HEAD ** -0.5)

    dproj, g["a_q_norm"], g["a_k_norm"], g["c_q_norm"], g["c_kv_norm"] = mixer_prep_bwd(
        "mixer_prep_bwd", sv["proj"], p, tabs, dqa, dka, dva, dqb, dkb, dvb, dcqn, dckvn, dckr)
    dhn = matmul("in_proj_dx", dproj, p["w_in_t"], "nn", F32, tm=512)
    g["w_in"] = matmul("in_proj_dw", dproj, sv["hn"], "tn", F32, tn=512, tk=512)
    return dhn, g


def _local_step(x, target, layers, final_norm, tabs):
    saved = []
    for p in layers:
        x, sv = _layer_fwd(x, p, tabs)
        saved.append(sv)
    dm = x.shape[1]
    yf = rmsnorm_fwd("final_norm", x, final_norm, dm, F32)
    loss, dyf = loss_head("loss_head", yf, target)
    dx, d_final = rmsnorm_bwd("final_norm_bwd", dyf, x, final_norm, dm, (F32,))
    grads = [None] * len(layers)
    for l in reversed(range(len(layers))):
        dx, grads[l] = _layer_bwd(dx, saved[l], layers[l], tabs)
    return loss, dx, grads, d_final


HBM = pl.BlockSpec(memory_space=pl.ANY)


def _place():
    x, y, c = lax.axis_index("x"), lax.axis_index("y"), lax.axis_index("c")
    others = [(1 - x, y), (x, 1 - y), (1 - x, 1 - y)]
    return x, y, c, 2 * x + y, others


def _remote(src, dst, send_sem, recv_sem, device):
    return pltpu.make_async_remote_copy(src_ref=src, dst_ref=dst, send_sem=send_sem, recv_sem=recv_sem,
                                        device_id=device, device_id_type=MESH)


def gather_small(name, small):
    def body(in_ref, out_ref, send_sems, recv_sems, local_sem):
        x, y, c, chip, others = _place()
        mine = pltpu.make_async_copy(in_ref, out_ref.at[chip], local_sem)
        mine.start()
        copies = []
        for k, (ox, oy) in enumerate(others):
            cp = _remote(in_ref, out_ref.at[chip], send_sems.at[k], recv_sems.at[k], (ox, oy, c))
            cp.start()
            copies.append(cp)
        for k, (ox, oy) in enumerate(others):
            landed = out_ref.at[2 * ox + oy]
            _remote(landed, landed, send_sems.at[k], recv_sems.at[k], (ox, oy, c)).wait_recv()
        for cp in copies:
            cp.wait_send()
        mine.wait()

    return pl.pallas_call(
        body, name=name, in_specs=[HBM], out_specs=HBM, out_shape=jax.ShapeDtypeStruct((4,) + small.shape, small.dtype),
        scratch_shapes=[pltpu.SemaphoreType.DMA((3,)), pltpu.SemaphoreType.DMA((3,)), pltpu.SemaphoreType.DMA],
        compiler_params=pltpu.CompilerParams(has_side_effects=True))(small)


IN_HBM = pl.BlockSpec(memory_space=pltpu.HBM)
SEMS = pl.BlockSpec(memory_space=pltpu.SEMAPHORE)
DATAFLOW = pltpu.SideEffectType.DATAFLOW_SIDE_EFFECTING


BF16_ROWS_PER_TILE = 16


def _halves_by_rows(shape):
    return (shape[-2] // 2) % BF16_ROWS_PER_TILE == 0


def _half(shape, core):
    if _halves_by_rows(shape):
        size = shape[-2] // 2
        return (pl.ds(core * size, size), slice(None))
    size = shape[-1] // 2
    return (slice(None), pl.ds(core * size, size))


def _half_shape(shape):
    r, c = shape[-2:]
    return (r // 2, c) if _halves_by_rows(shape) else (r, c // 2)


def _gather_plan(bufs):
    x, y, c, chip, others = _place()
    plan = []
    for t, ref in enumerate(bufs):
        mine = _half(ref.shape, c)
        for k, (ox, oy) in enumerate(others):
            plan.append((3 * t + k, (ox, oy, c), ref.at[(chip,) + mine], ref.at[(chip,) + mine],
                         ref.at[(2 * ox + oy,) + mine]))
    return plan


def _exchange_plan(bufs):
    x, y, c, chip, others = _place()
    n = len(bufs) // 2
    plan = []
    for t in range(n):
        for k, (ox, oy) in enumerate(others):
            plan.append((3 * t + k, (ox, oy, c), bufs[t].at[2 * ox + oy], bufs[n + t].at[chip], bufs[n + t].at[2 * ox + oy]))
    return plan


def _sibling_plan(bufs):
    x, y, c, chip, others = _place()
    n = len(bufs) // 2
    plan = []
    for t in range(n):
        plan.append((t, (x, y, 1 - c), bufs[t].at[(slice(None),) + _half(bufs[t].shape, 1 - c)], bufs[n + t], bufs[n + t]))
    return plan


def _forward_plan(bufs):
    x, y, c, chip, others = _place()
    plan = []
    for t, ref in enumerate(bufs):
        mine, theirs = _half(ref.shape, c), _half(ref.shape, 1 - c)
        for k, (ox, oy) in enumerate(others):
            slot = 2 * ox + oy
            plan.append((3 * t + k, (x, y, 1 - c), ref.at[(slot,) + mine], ref.at[(slot,) + mine], ref.at[(slot,) + theirs]))
    return plan


def _join_plan(bufs):
    x, y, c, chip, others = _place()
    return [(t, (x, y, 1 - c), ref.at[_half(ref.shape, c)], ref.at[_half(ref.shape, c)], ref.at[_half(ref.shape, 1 - c)])
            for t, ref in enumerate(bufs)]


PLAN_COPIES = {_gather_plan: lambda n: 3 * n, _exchange_plan: lambda n: 3 * (n // 2), _sibling_plan: lambda n: n // 2,
               _forward_plan: lambda n: 3 * n, _join_plan: lambda n: n}


def split_start(name, bufs, plan_of, after):
    n = len(bufs)

    def body(*refs):
        ins = refs[:n]
        send_sems, recv_sems = refs[n + 1], refs[n + 2]
        token = refs[-1]
        for idx, peer, src, dst, _ in plan_of(ins):
            _remote(src, dst, send_sems.at[idx], recv_sems.at[idx], peer).start()
        token[...] = jnp.zeros_like(token)

    copies = PLAN_COPIES[plan_of](n)
    res = pl.pallas_call(
        body, name=name, in_specs=[IN_HBM] * n + [HBM],
        out_specs=(SEMS, SEMS) + (IN_HBM,) * n + (pl.BlockSpec(memory_space=pltpu.VMEM),),
        out_shape=(pltpu.SemaphoreType.DMA((copies,)), pltpu.SemaphoreType.DMA((copies,)))
        + tuple(pltpu.HBM(b.shape, b.dtype) for b in bufs) + (jax.ShapeDtypeStruct((8, LANES), F32),),
        input_output_aliases={t: 2 + t for t in range(n)},
        compiler_params=pltpu.CompilerParams(has_side_effects=DATAFLOW))(
            *[pltpu.with_memory_space_constraint(b, pltpu.HBM) for b in bufs], after)
    return res[0], res[1], list(res[2:2 + n]), res[-1]


def split_wait(name, send_sems, recv_sems, flying, plan_of, after):
    n = len(flying)

    def body(*refs):
        ins = refs[:n]
        send_ref, recv_ref = refs[n], refs[n + 1]
        for idx, peer, src, dst, landing in plan_of(ins):
            _remote(src, dst, send_ref.at[idx], recv_ref.at[idx], peer).wait_send()
            _remote(landing, landing, send_ref.at[idx], recv_ref.at[idx], peer).wait_recv()

    return list(pl.pallas_call(
        body, name=name, in_specs=[IN_HBM] * n + [SEMS, SEMS] + [HBM] * len(after), out_specs=(IN_HBM,) * n,
        out_shape=tuple(pltpu.HBM(b.shape, b.dtype) for b in flying),
        input_output_aliases={t: t for t in range(n)},
        compiler_params=pltpu.CompilerParams(has_side_effects=DATAFLOW))(*flying, send_sems, recv_sems, *after))


def _half_block(shape, tr):
    r, c = shape[-2:]
    if _halves_by_rows(shape):
        per = r // 2 // tr
        return tr, c, lambda i, core: (core * per + i, 0)
    return tr, c // 2, lambda i, core: (i, core)


def add_half(name, place, grad, theirs):
    hr, hc = _half_shape(grad.shape)
    tr = _rows(hr, hc)
    br, bc, at = _half_block(grad.shape, tr)

    def body(place_ref, g_ref, t_ref, o_ref):
        o_ref[...] = (g_ref[...] + t_ref[...]).astype(o_ref.dtype)

    whole = pl.BlockSpec((None, br, bc), lambda j, i, pr: (j, i, 0))
    return pl.pallas_call(
        body, name=name,
        grid_spec=pltpu.PrefetchScalarGridSpec(
            num_scalar_prefetch=1, grid=(4, hr // tr),
            in_specs=[pl.BlockSpec((None, br, bc), lambda j, i, pr: (j,) + at(i, pr[0])), whole],
            out_specs=whole),
        out_shape=jax.ShapeDtypeStruct((4, hr, hc), BF16),
        compiler_params=_params("parallel", "parallel"))(place, grad, theirs)


def sum_chips(name, place, own, parts, shard_shape):
    _, hr, hc = parts.shape
    tr = _rows(hr, hc)
    br, bc, at = _half_block(shard_shape, tr)

    def body(place_ref, own_ref, p1, p2, p3, o_ref):
        o_ref[...] = own_ref[...].astype(F32) + p1[...].astype(F32) + p2[...].astype(F32) + p3[...].astype(F32)

    def slot(k):
        return pl.BlockSpec((None, br, bc), lambda i, pr: (lax.rem(pr[1] + k, 4), i, 0))

    return pl.pallas_call(
        body, name=name,
        grid_spec=pltpu.PrefetchScalarGridSpec(
            num_scalar_prefetch=1, grid=(hr // tr,), in_specs=[slot(0), slot(1), slot(2), slot(3)],
            out_specs=pl.BlockSpec((br, bc), lambda i, pr: at(i, pr[0]))),
        out_shape=jax.ShapeDtypeStruct(tuple(shard_shape), F32),
        compiler_params=_params("parallel"))(place, own, parts, parts, parts)


def allreduce_small(name, buf):
    rows = buf.shape[0]

    def body(in_ref, out_ref, slots, send_sems, recv_sems):
        x, y, c, _, _ = _place()
        me = 4 * x + 2 * y + c
        slots[me] = in_ref[...]
        peers = []
        for k in range(1, 8):
            px = 1 - x if k & 4 else x
            py = 1 - y if k & 2 else y
            pc = 1 - c if k & 1 else c
            peers.append((px, py, pc))
        copies = []
        for k, peer in enumerate(peers):
            cp = _remote(in_ref, slots.at[me], send_sems.at[k], recv_sems.at[k], peer)
            cp.start()
            copies.append(cp)
        for k, (px, py, pc) in enumerate(peers):
            slot = slots.at[4 * px + 2 * py + pc]
            _remote(slot, slot, send_sems.at[k], recv_sems.at[k], (px, py, pc)).wait_recv()
        for cp in copies:
            cp.wait_send()
        acc = slots[0]
        for d in range(1, 8):
            acc = acc + slots[d]
        out_ref[...] = acc

    vm = pl.BlockSpec(memory_space=pltpu.VMEM)
    return pl.pallas_call(
        body, name=name, in_specs=[vm], out_specs=vm, out_shape=jax.ShapeDtypeStruct(buf.shape, F32),
        scratch_shapes=[pltpu.VMEM((8, rows, LANES), F32), pltpu.SemaphoreType.DMA((7,)), pltpu.SemaphoreType.DMA((7,))],
        compiler_params=pltpu.CompilerParams(has_side_effects=True, vmem_limit_bytes=VMEM_LIMIT))(buf)


def cast_layer(name, place, w, layer, out_dtype=None):
    out_dtype = out_dtype or BF16
    _, r, cols = w.shape
    tr = _rows(r, cols)

    def body(place_ref, w_ref, o_ref):
        o_ref[...] = w_ref[...].astype(o_ref.dtype)

    return pl.pallas_call(
        body, name=name,
        grid_spec=pltpu.PrefetchScalarGridSpec(
            num_scalar_prefetch=1, grid=(r // tr,),
            in_specs=[pl.BlockSpec((None, tr, cols), lambda i, pr: (layer, i, 0))],
            out_specs=pl.BlockSpec((None, tr, cols), lambda i, pr: (pr[1], i, 0))),
        out_shape=jax.ShapeDtypeStruct((4, r, cols), out_dtype),
        compiler_params=_params("parallel"))(place, w)


def _adamw_math(g, w, m, v):
    m = ADAM_B1 * m + (1.0 - ADAM_B1) * g
    v = ADAM_B2 * v + (1.0 - ADAM_B2) * (g * g)
    m_hat = m / (1.0 - ADAM_B1 ** ADAM_STEP)
    v_hat = v / (1.0 - ADAM_B2 ** ADAM_STEP)
    delta = -ADAM_LR * (m_hat / (jnp.sqrt(v_hat) + ADAM_EPS) + ADAM_WD * w)
    return delta, m, v


def adamw_layer(name, layer, g, w, m, v, prev=None):
    nl, r, cols = w.shape
    tr = _rows(r, cols, 8)

    def body(*refs):
        g_ref, w_ref, m_ref, v_ref = refs[:4]
        og, od, om, ov = refs[-4:]
        gv = g_ref[...]
        delta, m2, v2 = _adamw_math(gv, w_ref[...], m_ref[...], v_ref[...])
        og[...] = gv
        od[...] = delta
        om[...] = m2
        ov[...] = v2

    lay = pl.BlockSpec((None, tr, cols), lambda i: (layer, i, 0))
    ins = [g, w, m, v] + (list(prev) if prev is not None else [])
    return pl.pallas_call(
        body, name=name, grid=(r // tr,),
        in_specs=[pl.BlockSpec((tr, cols), lambda i: (i, 0)), lay, lay, lay] + ([HBM] * 4 if prev is not None else []),
        out_specs=[lay] * 4, out_shape=[jax.ShapeDtypeStruct((nl, r, cols), F32)] * 4,
        input_output_aliases={4 + k: k for k in range(4)} if prev is not None else {},
        compiler_params=_params("parallel"))(*ins)


def adamw_packed(name, g, w, m, v):
    def body(g_ref, w_ref, m_ref, v_ref, od, om, ov):
        delta, m2, v2 = _adamw_math(g_ref[...], w_ref[...], m_ref[...], v_ref[...])
        od[...] = delta
        om[...] = m2
        ov[...] = v2

    vm = pl.BlockSpec(memory_space=pltpu.VMEM)
    return pl.pallas_call(
        body, name=name, in_specs=[vm] * 4, out_specs=[vm] * 3, out_shape=[jax.ShapeDtypeStruct(g.shape, F32)] * 3,
        compiler_params=pltpu.CompilerParams(vmem_limit_bytes=VMEM_LIMIT))(g, w, m, v)


def _pack(arrays):
    flat = jnp.concatenate([a.reshape(-1) for a in arrays])
    pad = (-flat.shape[0]) % (8 * LANES)
    return jnp.pad(flat, (0, pad)).reshape(-1, LANES)


def _unpack(buf, shapes):
    flat = buf.reshape(-1)
    out, off = [], 0
    for shp in shapes:
        size = int(np.prod(shp))
        out.append(flat[off:off + size].reshape(shp))
        off += size
    return out


MIXER_W = ("w_in", "w_uq", "w_ukv", "w_out")
FFN_W = ("w_up", "w_down")
BIG = MIXER_W + FFN_W
TRANSPOSED = ("w_in", "w_uq")
COLUMN_CUT = ("w_ukv", "w_up")
SMALL = ("attn_norm", "a_q_norm", "a_k_norm", "c_q_norm", "c_kv_norm", "out_norm", "ffn_norm", "conv_b", "final_norm")
WEIGHTS = ("attn_norm", "w_in", "a_q_norm", "a_k_norm", "c_q_norm", "c_kv_norm", "w_uq", "w_ukv", "out_norm", "w_out",
           "ffn_norm", "w_up", "conv_w", "conv_b", "w_down", "final_norm")
INPUTS = ("x",) + WEIGHTS + ("loss_target",) + tuple("m_" + n for n in WEIGHTS) + tuple("v_" + n for n in WEIGHTS)


def _columns(g):
    return jnp.transpose(g, (1, 0, 2)).reshape(g.shape[1], 4 * g.shape[2])


def _uncolumns(w):
    r, c4 = w.shape
    return jnp.transpose(w.reshape(r, 4, c4 // 4), (1, 0, 2))


def _compute_layouts(full, conv_w, small, layer):
    p = {n: small[n][layer] for n in SMALL if n != "final_norm"}
    if "w_in" in full:
        p["w_in_t"] = full["w_in"].reshape(-1, full["w_in"].shape[2])
    if "w_uq" in full:
        uq_t = full["w_uq"].reshape(C_HEADS, HEAD + C_ROPE, C_RANK)
        p["w_uq_t"] = jnp.concatenate([uq_t[:, :HEAD].reshape(C_W, C_RANK),
                                       jnp.pad(uq_t[:, HEAD:], ((0, 0), (0, HEAD - C_ROPE), (0, 0))).reshape(C_W, C_RANK)], axis=0)
        p["w_ukv"] = _columns(full["w_ukv"])
        p["w_out"] = full["w_out"].reshape(-1, full["w_out"].shape[2])
    if "w_down" in full:
        p["w_down"] = full["w_down"].reshape(-1, full["w_down"].shape[2])
    if "w_up" in full:
        up = full["w_up"]
        p["w_up"] = up.reshape(2, 2, up.shape[1], up.shape[2])
        fh = conv_w.shape[1] // 2
        taps = jnp.transpose(conv_w.reshape(3, 2, fh), (1, 0, 2))
        p["cwb"] = jnp.concatenate([taps, small["conv_b"][layer].reshape(2, 1, fh), jnp.zeros((2, 4, fh), F32)], axis=1)
    return p


def _shard_layout(name, g):
    if name == "w_in":
        return g.reshape(4, -1, g.shape[1])
    if name == "w_uq":
        uq_t = jnp.concatenate([g[:C_W].reshape(C_HEADS, HEAD, C_RANK),
                                g[C_W:].reshape(C_HEADS, HEAD, C_RANK)[:, :C_ROPE]], axis=1)
        return uq_t.reshape(4, -1, C_RANK)
    if name == "w_ukv":
        return _uncolumns(g)
    if name == "w_up":
        return g.reshape(4, g.shape[2], g.shape[3])
    return g.reshape(4, -1, g.shape[1])


def _conv_grads(dcwb):
    fh = dcwb.shape[2]
    return jnp.transpose(dcwb[:, 0:3, :], (1, 0, 2)).reshape(3, 2 * fh), dcwb[:, 3, :].reshape(2 * fh)


def _shard_layouts(g):
    return ({n: _shard_layout(n, g[n]) for n in BIG},) + _conv_grads(g["cwb"])


def kernel(x, attn_norm, w_in, a_q_norm, a_k_norm, c_q_norm, c_kv_norm, w_uq, w_ukv, out_norm, w_out, ffn_norm, w_up, conv_w, conv_b, w_down, final_norm, loss_target, m_attn_norm, m_w_in, m_a_q_norm, m_a_k_norm, m_c_q_norm, m_c_kv_norm, m_w_uq, m_w_ukv, m_out_norm, m_w_out, m_ffn_norm, m_w_up, m_conv_w, m_conv_b, m_w_down, m_final_norm, v_attn_norm, v_w_in, v_a_q_norm, v_a_k_norm, v_c_q_norm, v_c_kv_norm, v_w_uq, v_w_ukv, v_out_norm, v_w_out, v_ffn_norm, v_w_up, v_conv_w, v_conv_b, v_w_down, v_final_norm):
    a = dict(zip(INPUTS, (x, attn_norm, w_in, a_q_norm, a_k_norm, c_q_norm, c_kv_norm, w_uq, w_ukv, out_norm, w_out, ffn_norm, w_up, conv_w, conv_b, w_down, final_norm, loss_target, m_attn_norm, m_w_in, m_a_q_norm, m_a_k_norm, m_c_q_norm, m_c_kv_norm, m_w_uq, m_w_ukv, m_out_norm, m_w_out, m_ffn_norm, m_w_up, m_conv_w, m_conv_b, m_w_down, m_final_norm, v_attn_norm, v_w_in, v_a_q_norm, v_a_k_norm, v_c_q_norm, v_c_kv_norm, v_w_uq, v_w_ukv, v_out_norm, v_w_out, v_ffn_norm, v_w_up, v_conv_w, v_conv_b, v_w_down, v_final_norm)))
    nl = w_in.shape[0]
    seq = x.shape[1]
    for n in TRANSPOSED:
        for kind in ("", "m_", "v_"):
            a[kind + n] = jnp.swapaxes(a[kind + n], 1, 2)
    chip = 2 * lax.axis_index("x") + lax.axis_index("y")
    place = jnp.stack([lax.axis_index("c"), chip]).astype(jnp.int32)
    tabs = _rope_tables(seq)

    assert nl == 2
    quarter = conv_w.shape[2]
    taps = jnp.pad(conv_w, ((0, 0), (0, 8 - conv_w.shape[1]), (0, 0))).reshape(1, nl * 8, quarter)
    conv_full = []
    flights = {}
    here = place
    for l in range(nl):
        for key, names in ((f"in{l}", ("w_in",)), (f"mix{l}", MIXER_W[1:]), (f"up{l}", ("w_up",)), (f"down{l}", ("w_down",))):
            slots = [cast_layer(f"cast{l}_{n}", here, a[n], l) for n in names]
            if not flights:
                slots.append(cast_layer("cast_taps", here, taps, 0, F32))
                names = names + ("conv_taps",)
            send_sems, recv_sems, flying, token = split_start(f"gather_{key}_start", slots, _gather_plan, here)
            flights[key] = (names, send_sems, recv_sems, flying)
            here = place + token[0, 0].astype(jnp.int32)
    started = token[0, 0]

    def land(key, after):
        names, send_sems, recv_sems, flying = flights[key]
        landed = split_wait(f"gather_{key}_wait", send_sems, recv_sems, flying, _gather_plan, after)
        send_sems, recv_sems, passing, token = split_start(f"gather_{key}_pass_start", landed, _forward_plan, place)
        return (key, names, send_sems, recv_sems, passing), token

    def landed(passing, after, layer):
        key, names, send_sems, recv_sems, flying = passing
        full = dict(zip(names, split_wait(f"gather_{key}_pass_wait", send_sems, recv_sems, flying, _forward_plan, after)))
        if "conv_taps" in full:
            taps_full = full.pop("conv_taps").reshape(4, nl, 8, quarter)
            conv_full.extend(jnp.transpose(taps_full[:, k, 0:3, :], (1, 0, 2)).reshape(3, 4 * quarter) for k in range(nl))
        return _compute_layouts(full, conv_full[layer], a, layer)

    h = x[0]
    layers, saved = [], []
    early = None
    for l in range(nl):
        if early is None:
            pass_in, token = land(f"in{l}", [token])
            p = landed(pass_in, [token], l)
            p["attn_norm"] = p["attn_norm"] + started
        else:
            p = landed(early[0], [h], l)
        sv = _mixers_ab_fwd(h, p, tabs)
        if early is None:
            pass_mix, token = land(f"mix{l}", [sv["yb"]])
            p.update(landed(pass_mix, [token], l))
        else:
            p.update(landed(early[1], [sv["yb"]], l))
        _mixer_c_fwd(sv, p, tabs)
        pass_up, token = land(f"up{l}", [sv["yc"]])
        if l > 0:
            pass_down, token = land(f"down{l}", [token])
        x1 = _out_fwd(h, sv, p, token[0, 0])
        p.update(landed(pass_up, [x1], l))
        _ffn_up_fwd(x1, p, sv)
        if l == 0:
            pass_down, token = land(f"down{l}", [sv["act"]])
        else:
            token = sv["act"]
        if l + 1 < nl:
            pass_in, _ = land(f"in{l + 1}", [token])
            pass_mix, _ = land(f"mix{l + 1}", [token])
            early = (pass_in, pass_mix)
        p.update(landed(pass_down, [token], l))
        h = _ffn_down_fwd(x1, p, sv)
        layers.append(p)
        saved.append(sv)
    dm = h.shape[1]
    yf = rmsnorm_fwd("final_norm", h, final_norm, dm, F32)
    loss_here, dyf = loss_head("loss_head", yf, loss_target[0])
    dx, dxb, d_final = rmsnorm_bwd("final_norm_bwd", dyf, h, final_norm, dm, (F32, BF16))

    def swap_begin(tag, names, g):
        g_list = [_shard_layout(n, g[n]) for n in names]
        theirs = [lax.empty((4,) + _half_shape(s.shape), s.dtype) for s in g_list]
        send_sems, recv_sems, flying, token = split_start(f"{tag}_sibling_start", g_list + theirs, _sibling_plan, place)
        return (names, send_sems, recv_sems, flying), token[0, 0]

    def exchange_begin(tag, swap, after):
        names, send_sems, recv_sems, flying = swap
        landed = split_wait(f"{tag}_sibling_wait", send_sems, recv_sems, flying, _sibling_plan, after)
        g_list, theirs = landed[:len(names)], landed[len(names):]
        sums = [add_half(f"{tag}_add_{n}", place, gr, t) for n, gr, t in zip(names, g_list, theirs)]
        parts = [lax.empty(s.shape, s.dtype) for s in sums]
        send_sems, recv_sems, flying, token = split_start(f"{tag}_chips_start", sums + parts, _exchange_plan, place)
        return (names, [gr.shape[1:] for gr in g_list], send_sems, recv_sems, flying), token

    def exchange_end(tag, flight, after):
        names, shard_shapes, send_sems, recv_sems, flying = flight
        landed = split_wait(f"{tag}_chips_wait", send_sems, recv_sems, flying, _exchange_plan, after)
        sums, parts = landed[:len(names)], landed[len(names):]
        halves = [sum_chips(f"{tag}_sum_{n}", place, s, p, shp) for n, s, p, shp in zip(names, sums, parts, shard_shapes)]
        send_sems, recv_sems, joining, token = split_start(f"{tag}_join_start", halves, _join_plan, place)
        return (names, send_sems, recv_sems, joining), token

    def joined(tag, joining, after):
        names, send_sems, recv_sems, flying = joining
        return dict(zip(names, split_wait(f"{tag}_join_wait", send_sems, recv_sems, flying, _join_plan, after)))

    grads, conv_grads, flights_f, flights_m = [None] * nl, [None] * nl, [None] * nl, [None] * nl
    after = 0.0
    for l in reversed(range(nl)):
        sv, p = saved[l], layers[l]
        dxn, g = _ffn_bwd(dxb, sv, p, after)
        swap_f, after = swap_begin(f"reduce{l}f", FFN_W, g)
        dx1, dy, g_out = _out_proj_bwd(dxn, dx, sv, p, after)
        part, g_c = _mixer_c_bwd(dy, sv, p, tabs)
        flights_f[l], token = exchange_begin(f"reduce{l}f", swap_f, [part[2]])
        dhn, g_ab = _mixers_ab_bwd(part, sv, p, tabs, token[0, 0])
        g.update(g_out)
        g.update(g_c)
        g.update(g_ab)
        swap_m, after = swap_begin(f"reduce{l}m", MIXER_W, g)
        dx, dxb, g_norm = _attn_norm_bwd(dhn, dx1, sv, p, after)
        g.update(g_norm)
        grads[l], conv_grads[l] = g, _conv_grads(g["cwb"])
        before = [dx]
        if l == 0:
            small_g = {n: jnp.stack([grads[k][n] for k in range(nl)]) for n in SMALL if n not in ("conv_b", "final_norm")}
            small_g["conv_b"] = jnp.stack([cg[1] for cg in conv_grads])
            small_g["final_norm"] = d_final
            conv_w_g = jnp.stack([cg[0] for cg in conv_grads])
            small_sum = allreduce_small("reduce_small", _pack([small_g[n] for n in SMALL] + [conv_w_g, loss_here.reshape(1)]))
            before.append(small_sum)
        flights_m[l], token = exchange_begin(f"reduce{l}m", swap_m, before)
        after = token[0, 0]
    grad_x = dx

    out = {}
    join_f, token = exchange_end("reduce1f", flights_f[1], [token])
    join_m, token = exchange_end("reduce1m", flights_m[1], [token])
    reduced1 = joined("reduce1f", join_f, [token])
    for n in FFN_W:
        out[n] = adamw_layer(f"adamw1_{n}", 1, reduced1[n], a[n], a["m_" + n], a["v_" + n])
    reduced1 = joined("reduce1m", join_m, [out[n][3] for n in FFN_W])
    for n in MIXER_W:
        out[n] = adamw_layer(f"adamw1_{n}", 1, reduced1[n], a[n], a["m_" + n], a["v_" + n])

    summed = _unpack(small_sum, [a[n].shape for n in SMALL] + [conv_w_g.shape, (1,)])
    small_g = dict(zip(SMALL, summed[:-2]))
    small_g["conv_w"] = lax.dynamic_slice_in_dim(summed[-2], chip * quarter, quarter, axis=2)
    loss = summed[-1][0]
    names = SMALL + ("conv_w",)
    shapes = [a[n].shape for n in names]
    d_small, m_small, v_small = adamw_packed(
        "adamw_small", _pack([small_g[n] for n in names]), _pack([a[n] for n in names]),
        _pack([a["m_" + n] for n in names]), _pack([a["v_" + n] for n in names]))

    join_f, token = exchange_end("reduce0f", flights_f[0], [out[n][3] for n in BIG] + [d_small])
    join_m, token = exchange_end("reduce0m", flights_m[0], [token])
    reduced0 = joined("reduce0f", join_f, [token])
    for n in FFN_W:
        out[n] = adamw_layer(f"adamw0_{n}", 0, reduced0[n], a[n], a["m_" + n], a["v_" + n], prev=out[n])
    reduced0 = joined("reduce0m", join_m, [out[n][3] for n in FFN_W])
    for n in MIXER_W:
        out[n] = adamw_layer(f"adamw0_{n}", 0, reduced0[n], a[n], a["m_" + n], a["v_" + n], prev=out[n])
    for n, d_, m_, v_ in zip(names, _unpack(d_small, shapes), _unpack(m_small, shapes), _unpack(v_small, shapes)):
        out[n] = (small_g[n], d_, m_, v_)
    for n in TRANSPOSED:
        out[n] = [jnp.swapaxes(o, 1, 2) for o in out[n]]

    return (loss, grad_x[None]) + tuple(out[n][k] for k in range(4) for n in WEIGHTS)
```
